```python
import math
import jax, jax.numpy as jnp
from jax import lax
import numpy as np

D_MODEL = 1024
BATCH = 16
SEQ = 4096
DEPTH = 1

EXPAND = 2
D_INNER = EXPAND * D_MODEL
D_SB = D_INNER // 2
D_SSD = D_INNER - D_SB
SB_HEAD_DIM = 64
SB_HEADS = D_SB // SB_HEAD_DIM
SB_BLOCK = 128
SSD_HEAD_DIM = 64
SSD_HEADS = D_SSD // SSD_HEAD_DIM
SSD_GROUPS = 2
SSD_STATE = 128
SSD_CHUNK = 128
CONV_W = 4
CONV_DIM = D_SSD + 2 * SSD_GROUPS * SSD_STATE
D_IN_PROJ = 4 * D_SB + D_SSD + CONV_DIM + SSD_HEADS
EPS = 1e-6

kernel_name = "hymba_stickbreaking_ssd_layer"


def rmsnorm(x, w):
    xf = x.astype(jnp.float32)
    y = xf * lax.rsqrt(jnp.mean(xf * xf, axis=-1, keepdims=True) + EPS)
    return (y * w.astype(jnp.float32)).astype(x.dtype)


def causal_depthwise_conv(u, w, b):
    l = u.shape[1]
    up = jnp.pad(u, ((0, 0), (CONV_W - 1, 0), (0, 0)))
    out = b
    for i in range(CONV_W):
        out = out + up[:, i:i + l] * w[i]
    return out


def stick_breaking_attention(q, k, v):
    b, l, h, d = q.shape
    nb = l // SB_BLOCK
    scale = 1.0 / math.sqrt(d)
    qb = q.astype(jnp.float32).reshape(b, nb, SB_BLOCK, h, d).transpose(1, 0, 2, 3, 4)
    kf = k.astype(jnp.float32)
    vf = v.astype(jnp.float32)
    key_pos = jnp.arange(l)

    def block(args):
        qi, i = args
        z = jnp.einsum('bqhd,bkhd->bhqk', qi, kf) * scale
        q_pos = i * SB_BLOCK + jnp.arange(SB_BLOCK)
        mask = key_pos[None, :] < q_pos[:, None]
        log_keep = jnp.where(mask, jax.nn.log_sigmoid(-z), 0.0)
        rest = lax.cumsum(log_keep, axis=3, reverse=True) - log_keep
        w = jnp.where(mask, jnp.exp(jax.nn.log_sigmoid(z) + rest), 0.0)
        return jnp.einsum('bhqk,bkhd->bqhd', w, vf)

    o = lax.map(block, (qb, jnp.arange(nb)))
    return o.transpose(1, 0, 2, 3, 4).reshape(b, l, h, d)


def ssd_chunked(xh, dt, a, Bm, Cm):
    b, l, h, p = xh.shape
    g, n = Bm.shape[2], Bm.shape[3]
    hpg = h // g
    c = l // SSD_CHUNK
    Q = SSD_CHUNK
    x = (xh.astype(jnp.float32) * dt[..., None]).reshape(b, c, Q, g, hpg, p)
    dA = (dt * a).reshape(b, c, Q, g, hpg).transpose(0, 1, 3, 4, 2)
    Bc = Bm.astype(jnp.float32).reshape(b, c, Q, g, n)
    Cc = Cm.astype(jnp.float32).reshape(b, c, Q, g, n)
    a_cs = jnp.cumsum(dA, axis=-1)

    seg = a_cs[..., :, None] - a_cs[..., None, :]
    tri = jnp.tril(jnp.ones((Q, Q), dtype=bool))
    Lmat = jnp.exp(jnp.where(tri, seg, -jnp.inf))
    cb = jnp.einsum('bctgn,bcsgn->bcgts', Cc, Bc)
    wts = cb[:, :, :, None] * Lmat
    y_diag = jnp.einsum('bcgkts,bcsgkp->bctgkp', wts, x)

    decay_to_end = jnp.exp(a_cs[..., -1:] - a_cs)
    states = jnp.einsum('bcsgn,bcgks,bcsgkp->bcgkpn', Bc, decay_to_end, x)
    chunk_decay = jnp.exp(a_cs[..., -1])

    def step(carry, inp):
        st, dec = inp
        return carry * dec[..., None, None] + st, carry

    init = jnp.zeros((b, g, hpg, p, n), jnp.float32)
    _, states_in = lax.scan(step, init, (jnp.moveaxis(states, 1, 0), jnp.moveaxis(chunk_decay, 1, 0)))
    states_in = jnp.moveaxis(states_in, 0, 1)

    y_off = jnp.einsum('bctgn,bcgkpn,bcgkt->bctgkp', Cc, states_in, jnp.exp(a_cs))
    return (y_diag + y_off).reshape(b, l, h, p)


def hybrid_layer(x, norm_w, w_in, q_norm_w, k_norm_w, conv_w, conv_b, dt_bias,
                 A_log, D_skip, sb_norm_w, ssd_norm_w, w_out):
    b, l, _ = x.shape
    hn = rmsnorm(x, norm_w)
    proj = hn @ w_in
    i1 = D_SB; i2 = 2 * D_SB; i3 = 3 * D_SB; i4 = 4 * D_SB
    i5 = i4 + D_SSD; i6 = i5 + CONV_DIM
    q, k, v, z_sb, z_ssd, xBC, dt_raw = jnp.split(proj, [i1, i2, i3, i4, i5, i6], axis=-1)

    q = rmsnorm(q.reshape(b, l, SB_HEADS, SB_HEAD_DIM), q_norm_w)
    k = rmsnorm(k.reshape(b, l, SB_HEADS, SB_HEAD_DIM), k_norm_w)
    v = v.reshape(b, l, SB_HEADS, SB_HEAD_DIM)
    o_sb = stick_breaking_attention(q, k, v).reshape(b, l, D_SB).astype(x.dtype)
    y_sb = rmsnorm(o_sb * jax.nn.silu(z_sb), sb_norm_w)

    xBC = jax.nn.silu(causal_depthwise_conv(xBC, conv_w, conv_b))
    xs, Bm, Cm = jnp.split(xBC, [D_SSD, D_SSD + SSD_GROUPS * SSD_STATE], axis=-1)
    xs = xs.reshape(b, l, SSD_HEADS, SSD_HEAD_DIM)
    dt = jax.nn.softplus(dt_raw.astype(jnp.float32) + dt_bias.astype(jnp.float32))
    a = -jnp.exp(A_log.astype(jnp.float32))
    y = ssd_chunked(xs, dt, a,
                    Bm.reshape(b, l, SSD_GROUPS, SSD_STATE),
                    Cm.reshape(b, l, SSD_GROUPS, SSD_STATE))
    y = y + D_skip.astype(jnp.float32)[:, None] * xs.astype(jnp.float32)
    y = y.reshape(b, l, D_SSD).astype(x.dtype)
    y_ssd = rmsnorm(y * jax.nn.silu(z_ssd), ssd_norm_w)

    mixed = jnp.concatenate([y_sb, y_ssd], axis=-1)
    return x + mixed @ w_out


def _fwd_setup_inputs(seed: int = 0) -> dict:
    key = jax.random.key(seed)
    ks = jax.random.split(key, 14)
    f32 = jnp.float32
    x = jax.random.normal(ks[0], (BATCH, SEQ, D_MODEL), f32)
    norm_w = 1.0 + 0.02 * jax.random.normal(ks[1], (DEPTH, D_MODEL), f32)
    w_in = jax.random.normal(ks[2], (DEPTH, D_MODEL, D_IN_PROJ), f32) * D_MODEL ** -0.5
    q_norm_w = 1.0 + 0.02 * jax.random.normal(ks[3], (DEPTH, SB_HEAD_DIM), f32)
    k_norm_w = 1.0 + 0.02 * jax.random.normal(ks[4], (DEPTH, SB_HEAD_DIM), f32)
    conv_w = jax.random.normal(ks[5], (DEPTH, CONV_W, CONV_DIM), f32) * CONV_W ** -0.5
    conv_b = 0.01 * jax.random.normal(ks[6], (DEPTH, CONV_DIM), f32)
    u = jax.random.uniform(ks[7], (DEPTH, SSD_HEADS), f32)
    dt0 = jnp.exp(u * (math.log(0.1) - math.log(0.001)) + math.log(0.001))
    dt_bias = dt0 + jnp.log(-jnp.expm1(-dt0))
    A_log = jnp.log(jax.random.uniform(ks[8], (DEPTH, SSD_HEADS), f32, minval=1.0, maxval=16.0))
    D_skip = 1.0 + 0.02 * jax.random.normal(ks[9], (DEPTH, SSD_HEADS), f32)
    sb_norm_w = 1.0 + 0.02 * jax.random.normal(ks[10], (DEPTH, D_SB), f32)
    ssd_norm_w = 1.0 + 0.02 * jax.random.normal(ks[11], (DEPTH, D_SSD), f32)
    w_out = jax.random.normal(ks[12], (DEPTH, D_INNER, D_MODEL), f32) * D_INNER ** -0.5
    return {"x": x, "norm_w": norm_w, "w_in": w_in, "q_norm_w": q_norm_w,
            "k_norm_w": k_norm_w, "conv_w": conv_w, "conv_b": conv_b,
            "dt_bias": dt_bias, "A_log": A_log, "D_skip": D_skip,
            "sb_norm_w": sb_norm_w, "ssd_norm_w": ssd_norm_w, "w_out": w_out}


def _fwd_reference(x, norm_w, w_in, q_norm_w, k_norm_w, conv_w, conv_b, dt_bias,
              A_log, D_skip, sb_norm_w, ssd_norm_w, w_out):
    for layer in range(DEPTH):
        x = hybrid_layer(x, norm_w[layer], w_in[layer], q_norm_w[layer], k_norm_w[layer],
                         conv_w[layer], conv_b[layer], dt_bias[layer], A_log[layer],
                         D_skip[layer], sb_norm_w[layer], ssd_norm_w[layer], w_out[layer])
    return x


import jax as _jax
import jax.numpy as _jnp

TWIN_FORMAT = 'train_step'
FWD_PARAMS = ['x', 'norm_w', 'w_in', 'q_norm_w', 'k_norm_w', 'conv_w', 'conv_b', 'dt_bias', 'A_log', 'D_skip', 'sb_norm_w', 'ssd_norm_w', 'w_out']
TWIN_WEIGHTS = ['norm_w', 'w_in', 'q_norm_w', 'k_norm_w', 'conv_w', 'conv_b', 'dt_bias', 'A_log', 'D_skip', 'sb_norm_w', 'ssd_norm_w', 'w_out']
TWIN_DIFF_INPUT = 'x'
TWIN_INPUTS = ['x', 'norm_w', 'w_in', 'q_norm_w', 'k_norm_w', 'conv_w', 'conv_b', 'dt_bias', 'A_log', 'D_skip', 'sb_norm_w', 'ssd_norm_w', 'w_out', 'loss_target', 'm_norm_w', 'm_w_in', 'm_q_norm_w', 'm_k_norm_w', 'm_conv_w', 'm_conv_b', 'm_dt_bias', 'm_A_log', 'm_D_skip', 'm_sb_norm_w', 'm_ssd_norm_w', 'm_w_out', 'v_norm_w', 'v_w_in', 'v_q_norm_w', 'v_k_norm_w', 'v_conv_w', 'v_conv_b', 'v_dt_bias', 'v_A_log', 'v_D_skip', 'v_sb_norm_w', 'v_ssd_norm_w', 'v_w_out']
TWIN_OUTPUTS = ['loss', 'grad_x', 'grad_norm_w', 'grad_w_in', 'grad_q_norm_w', 'grad_k_norm_w', 'grad_conv_w', 'grad_conv_b', 'grad_dt_bias', 'grad_A_log', 'grad_D_skip', 'grad_sb_norm_w', 'grad_ssd_norm_w', 'grad_w_out', 'delta_norm_w', 'delta_w_in', 'delta_q_norm_w', 'delta_k_norm_w', 'delta_conv_w', 'delta_conv_b', 'delta_dt_bias', 'delta_A_log', 'delta_D_skip', 'delta_sb_norm_w', 'delta_ssd_norm_w', 'delta_w_out', 'new_m_norm_w', 'new_m_w_in', 'new_m_q_norm_w', 'new_m_k_norm_w', 'new_m_conv_w', 'new_m_conv_b', 'new_m_dt_bias', 'new_m_A_log', 'new_m_D_skip', 'new_m_sb_norm_w', 'new_m_ssd_norm_w', 'new_m_w_out', 'new_v_norm_w', 'new_v_w_in', 'new_v_q_norm_w', 'new_v_k_norm_w', 'new_v_conv_w', 'new_v_conv_b', 'new_v_dt_bias', 'new_v_A_log', 'new_v_D_skip', 'new_v_sb_norm_w', 'new_v_ssd_norm_w', 'new_v_w_out']
TWIN_LEAF_KINDS = {'loss': 'loss', 'grad_x': 'grad_x', 'grad_norm_w': 'grad_w', 'grad_w_in': 'grad_w', 'grad_q_norm_w': 'grad_w', 'grad_k_norm_w': 'grad_w', 'grad_conv_w': 'grad_w', 'grad_conv_b': 'grad_w', 'grad_dt_bias': 'grad_w', 'grad_A_log': 'grad_w', 'grad_D_skip': 'grad_w', 'grad_sb_norm_w': 'grad_w', 'grad_ssd_norm_w': 'grad_w', 'grad_w_out': 'grad_w', 'delta_norm_w': 'delta_w', 'delta_w_in': 'delta_w', 'delta_q_norm_w': 'delta_w', 'delta_k_norm_w': 'delta_w', 'delta_conv_w': 'delta_w', 'delta_conv_b': 'delta_w', 'delta_dt_bias': 'delta_w', 'delta_A_log': 'delta_w', 'delta_D_skip': 'delta_w', 'delta_sb_norm_w': 'delta_w', 'delta_ssd_norm_w': 'delta_w', 'delta_w_out': 'delta_w', 'new_m_norm_w': 'new_m', 'new_m_w_in': 'new_m', 'new_m_q_norm_w': 'new_m', 'new_m_k_norm_w': 'new_m', 'new_m_conv_w': 'new_m', 'new_m_conv_b': 'new_m', 'new_m_dt_bias': 'new_m', 'new_m_A_log': 'new_m', 'new_m_D_skip': 'new_m', 'new_m_sb_norm_w': 'new_m', 'new_m_ssd_norm_w': 'new_m', 'new_m_w_out': 'new_m', 'new_v_norm_w': 'new_v', 'new_v_w_in': 'new_v', 'new_v_q_norm_w': 'new_v', 'new_v_k_norm_w': 'new_v', 'new_v_conv_w': 'new_v', 'new_v_conv_b': 'new_v', 'new_v_dt_bias': 'new_v', 'new_v_A_log': 'new_v', 'new_v_D_skip': 'new_v', 'new_v_sb_norm_w': 'new_v', 'new_v_ssd_norm_w': 'new_v', 'new_v_w_out': 'new_v'}


def _forward(args):
    return _fwd_reference(*[args[k] for k in FWD_PARAMS])


def _output_shape():
    out = _jax.eval_shape(lambda: _forward(_fwd_setup_inputs(0)))
    return out.shape, out.dtype

N_MICROBATCH = 1
ADAM_LR = 0.001
ADAM_B1 = 0.9
ADAM_B2 = 0.999
ADAM_EPS = 1e-08
ADAM_WD = 0.01
ADAM_STEP = 10
PER_EXAMPLE_BATCH_AXIS = {'x': 0, 'loss_target': 0}
SHARED_INPUTS = []
_WEIGHT_DTYPES = {'norm_w': _jnp.float32, 'w_in': _jnp.float32, 'q_norm_w': _jnp.float32, 'k_norm_w': _jnp.float32, 'conv_w': _jnp.float32, 'conv_b': _jnp.float32, 'dt_bias': _jnp.float32, 'A_log': _jnp.float32, 'D_skip': _jnp.float32, 'sb_norm_w': _jnp.float32, 'ssd_norm_w': _jnp.float32, 'w_out': _jnp.float32}
MOMENT_SCALE = {'norm_w': 7.902713e-01, 'w_in': 2.887944e-01, 'q_norm_w': 6.485804e-01, 'k_norm_w': 6.505240e-01, 'conv_w': 9.495899e-01, 'conv_b': 3.313868e+00, 'dt_bias': 1.256315e+00, 'A_log': 1.246809e+01, 'D_skip': 7.696273e+00, 'sb_norm_w': 3.134747e+01, 'ssd_norm_w': 4.907728e+01, 'w_out': 1.574025e+00}


def _to_microbatches(a, axis):
    t = _jnp.moveaxis(a, axis, 0)
    t = t.reshape((N_MICROBATCH, t.shape[0] // N_MICROBATCH) + t.shape[1:])
    return _jnp.moveaxis(t, 1, axis + 1)


def setup_inputs(seed: int = 0) -> dict:
    inp = _fwd_setup_inputs(seed)
    key = _jax.random.fold_in(_jax.random.key(seed), 7919)
    shape, _ = _output_shape()
    out = dict(inp)
    out["loss_target"] = _jax.random.normal(_jax.random.fold_in(key, 0), shape, _jnp.float32)
    for i, name in enumerate(TWIN_WEIGHTS):
        w = inp[name].astype(_jnp.float32)
        if MOMENT_SCALE is None:
            s = _jnp.sqrt(_jnp.mean(_jnp.square(w)) + 1e-30)
        else:
            s = MOMENT_SCALE[name]
        km, kv = _jax.random.split(_jax.random.fold_in(key, i + 1))
        out[name] = w
        out["m_" + name] = s * _jax.random.normal(km, w.shape, _jnp.float32)
        out["v_" + name] = (s * s) * _jax.random.uniform(kv, w.shape, _jnp.float32, 0.5, 1.5)
    if N_MICROBATCH > 1:
        for name, axis in PER_EXAMPLE_BATCH_AXIS.items():
            out[name] = _to_microbatches(out[name], axis)
    return {'x': out['x'], 'norm_w': out['norm_w'], 'w_in': out['w_in'], 'q_norm_w': out['q_norm_w'], 'k_norm_w': out['k_norm_w'], 'conv_w': out['conv_w'], 'conv_b': out['conv_b'], 'dt_bias': out['dt_bias'], 'A_log': out['A_log'], 'D_skip': out['D_skip'], 'sb_norm_w': out['sb_norm_w'], 'ssd_norm_w': out['ssd_norm_w'], 'w_out': out['w_out'], 'loss_target': out['loss_target'], 'm_norm_w': out['m_norm_w'], 'm_w_in': out['m_w_in'], 'm_q_norm_w': out['m_q_norm_w'], 'm_k_norm_w': out['m_k_norm_w'], 'm_conv_w': out['m_conv_w'], 'm_conv_b': out['m_conv_b'], 'm_dt_bias': out['m_dt_bias'], 'm_A_log': out['m_A_log'], 'm_D_skip': out['m_D_skip'], 'm_sb_norm_w': out['m_sb_norm_w'], 'm_ssd_norm_w': out['m_ssd_norm_w'], 'm_w_out': out['m_w_out'], 'v_norm_w': out['v_norm_w'], 'v_w_in': out['v_w_in'], 'v_q_norm_w': out['v_q_norm_w'], 'v_k_norm_w': out['v_k_norm_w'], 'v_conv_w': out['v_conv_w'], 'v_conv_b': out['v_conv_b'], 'v_dt_bias': out['v_dt_bias'], 'v_A_log': out['v_A_log'], 'v_D_skip': out['v_D_skip'], 'v_sb_norm_w': out['v_sb_norm_w'], 'v_ssd_norm_w': out['v_ssd_norm_w'], 'v_w_out': out['v_w_out']}


def _loss(weights, diff, rest, loss_target):
    with _jax.named_scope("forward"):
        args = {**rest, TWIN_DIFF_INPUT: diff, **{k: w.astype(_WEIGHT_DTYPES[k]) for k, w in weights.items()}}
        y = _forward(args)
    with _jax.named_scope("loss_head"):
        err = _jnp.square(y.astype(_jnp.float32) - loss_target)
        return 0.5 * _jnp.sum(_jnp.mean(err, axis=-1)) if err.ndim else 0.5 * err


def _adamw(w, g, m, v):
    m = ADAM_B1 * m + (1.0 - ADAM_B1) * g
    v = ADAM_B2 * v + (1.0 - ADAM_B2) * _jnp.square(g)
    m_hat = m / (1.0 - ADAM_B1 ** ADAM_STEP)
    v_hat = v / (1.0 - ADAM_B2 ** ADAM_STEP)
    delta = -ADAM_LR * (m_hat / (_jnp.sqrt(v_hat) + ADAM_EPS) + ADAM_WD * w)
    return delta, m, v


def reference(x, norm_w, w_in, q_norm_w, k_norm_w, conv_w, conv_b, dt_bias, A_log, D_skip, sb_norm_w, ssd_norm_w, w_out, loss_target, m_norm_w, m_w_in, m_q_norm_w, m_k_norm_w, m_conv_w, m_conv_b, m_dt_bias, m_A_log, m_D_skip, m_sb_norm_w, m_ssd_norm_w, m_w_out, v_norm_w, v_w_in, v_q_norm_w, v_k_norm_w, v_conv_w, v_conv_b, v_dt_bias, v_A_log, v_D_skip, v_sb_norm_w, v_ssd_norm_w, v_w_out):
    given = dict(x=x, norm_w=norm_w, w_in=w_in, q_norm_w=q_norm_w, k_norm_w=k_norm_w, conv_w=conv_w, conv_b=conv_b, dt_bias=dt_bias, A_log=A_log, D_skip=D_skip, sb_norm_w=sb_norm_w, ssd_norm_w=ssd_norm_w, w_out=w_out, loss_target=loss_target, m_norm_w=m_norm_w, m_w_in=m_w_in, m_q_norm_w=m_q_norm_w, m_k_norm_w=m_k_norm_w, m_conv_w=m_conv_w, m_conv_b=m_conv_b, m_dt_bias=m_dt_bias, m_A_log=m_A_log, m_D_skip=m_D_skip, m_sb_norm_w=m_sb_norm_w, m_ssd_norm_w=m_ssd_norm_w, m_w_out=m_w_out, v_norm_w=v_norm_w, v_w_in=v_w_in, v_q_norm_w=v_q_norm_w, v_k_norm_w=v_k_norm_w, v_conv_w=v_conv_w, v_conv_b=v_conv_b, v_dt_bias=v_dt_bias, v_A_log=v_A_log, v_D_skip=v_D_skip, v_sb_norm_w=v_sb_norm_w, v_ssd_norm_w=v_ssd_norm_w, v_w_out=v_w_out)
    weights = {n: given[n] for n in TWIN_WEIGHTS}
    shared = {n: given[n] for n in SHARED_INPUTS}
    per_example = {n: given[n] for n in ['x']}
    grad_fn = _jax.value_and_grad(_loss, argnums=(0, 1))

    def one_microbatch(ex, loss_target):
        ex = dict(ex)
        diff = ex.pop(TWIN_DIFF_INPUT)
        return grad_fn(weights, diff, {**shared, **ex}, loss_target)

    if N_MICROBATCH == 1:
        loss, (grad_w, grad_x) = one_microbatch(per_example, given["loss_target"])
    else:
        def body(carry, xs):
            loss_sum, grad_sum = carry
            l_k, (gw_k, gx_k) = one_microbatch(xs[0], xs[1])
            with _jax.named_scope("update"):
                return (loss_sum + l_k, _jax.tree.map(_jnp.add, grad_sum, gw_k)), gx_k

        init = (_jnp.zeros((), _jnp.float32), _jax.tree.map(_jnp.zeros_like, weights))
        (loss, grad_w), grad_x = _jax.lax.scan(body, init, (per_example, given["loss_target"]))
    with _jax.named_scope("update"):
        delta_w, new_m, new_v = {}, {}, {}
        for n in TWIN_WEIGHTS:
            delta_w[n], new_m[n], new_v[n] = _adamw(weights[n], grad_w[n], given["m_" + n], given["v_" + n])
    return (loss, grad_x, *[grad_w[n] for n in TWIN_WEIGHTS], *[delta_w[n] for n in TWIN_WEIGHTS],
            *[new_m[n] for n in TWIN_WEIGHTS], *[new_v[n] for n in TWIN_WEIGHTS])
```

```python
import functools
import math

import jax
import jax.numpy as jnp
from jax import lax
from jax.experimental import pallas as pl
from jax.experimental.pallas import tpu as pltpu

F32 = jnp.float32
BF16 = jnp.bfloat16
HIGHEST = lax.Precision.HIGHEST

D_MODEL = 1024
N_HEADS = 16
HEAD_DIM = 64
N_PAIRS = N_HEADS // 2
D_BRANCH = 1024
N_GROUPS = 2
HEADS_PER_GROUP = 8
D_STATE = 128
GROUP_W = HEADS_PER_GROUP * HEAD_DIM
D_BC = 2 * N_GROUPS * D_STATE
D_CONV = D_BRANCH + D_BC
D_IN = 6672
D_IN_PAD = 7168
N_COLBLK = D_IN_PAD // 1024
COL_XS = 5120
COL_BC = 6144
COL_DT = 6656
EPS = 1e-6
CONV_TAPS = 4
N_DEV = 8

LANES = 128
SSD_CHUNK = 128
ATT_BLK = 256
VMEM_LIMIT = 56 * 1024 * 1024

ADAM_LR = 0.001
ADAM_B1 = 0.9
ADAM_B2 = 0.999
ADAM_EPS = 1e-08
ADAM_WD = 0.01
ADAM_STEP = 10

_NT = (((1,), (1,)), ((), ()))
_TN = (((0,), (0,)), ((), ()))


def _params(n_grid):
    return pltpu.CompilerParams(dimension_semantics=("arbitrary",) * n_grid,
                                vmem_limit_bytes=VMEM_LIMIT)


def _dot(a, b, dims=None, precision=None):
    if dims is None:
        return jnp.dot(a, b, preferred_element_type=F32, precision=precision)
    return lax.dot_general(a, b, dims, preferred_element_type=F32, precision=precision)


def _sigmoid(x):
    return 1.0 / (1.0 + jnp.exp(-x))


def _softplus(x):
    return jnp.maximum(x, 0.0) + jnp.log(1.0 + jnp.exp(-jnp.abs(x)))


def _split_bf16(x):
    hi = x.astype(BF16)
    lo = (x - hi.astype(F32)).astype(BF16)
    return hi, lo


def _lane_iota(shape):
    return lax.broadcasted_iota(jnp.int32, shape, len(shape) - 1)


def _row_iota(shape):
    return lax.broadcasted_iota(jnp.int32, shape, len(shape) - 2)


def _pair_sum(x):
    low = _lane_iota(x.shape) < HEAD_DIM
    s0 = jnp.sum(jnp.where(low, x, 0.0), axis=1, keepdims=True)
    s1 = jnp.sum(jnp.where(low, 0.0, x), axis=1, keepdims=True)
    return jnp.where(low, s0, s1)


def _head_expand():
    r = lax.broadcasted_iota(jnp.int32, (LANES, D_BRANCH), 0)
    c = lax.broadcasted_iota(jnp.int32, (LANES, D_BRANCH), 1)
    return jnp.where(c // HEAD_DIM == r, 1.0, 0.0).astype(F32)


def _in_proj(x2, norm_w, w_in_b, tm):
    t = x2.shape[0]

    def body(x_ref, nw_ref, w_ref, proj_ref, hn_ref):
        @pl.when(pl.program_id(1) == 0)
        def _():
            xf = x_ref[...]
            r = lax.rsqrt(jnp.mean(xf * xf, axis=1, keepdims=True) + EPS)
            hn_ref[...] = (xf * r * nw_ref[...]).astype(BF16)

        proj_ref[...] = _dot(hn_ref[...], w_ref[...])

    return pl.pallas_call(
        body, name="in_proj",
        grid=(t // tm, N_COLBLK),
        in_specs=[pl.BlockSpec((tm, D_MODEL), lambda i, j: (i, 0)),
                  pl.BlockSpec((1, D_MODEL), lambda i, j: (0, 0)),
                  pl.BlockSpec((D_MODEL, 1024), lambda i, j: (0, j))],
        out_specs=[pl.BlockSpec((tm, 1024), lambda i, j: (i, j)),
                   pl.BlockSpec((tm, D_MODEL), lambda i, j: (i, 0))],
        out_shape=[jax.ShapeDtypeStruct((t, D_IN_PAD), F32),
                   jax.ShapeDtypeStruct((t, D_MODEL), BF16)],
        compiler_params=_params(2),
    )(x2, norm_w, w_in_b)


def _qk_prep(proj, qw2, kw2, nb, seq, tq):
    nl = seq // tq
    scale = 1.0 / math.sqrt(HEAD_DIM)

    def body(q_ref, k_ref, v_ref, qw_ref, kw_ref, qs_ref, kn_ref, vb_ref, kt_ref):
        def norm(x, w):
            r = lax.rsqrt(_pair_sum(x * x) * (1.0 / HEAD_DIM) + EPS)
            return x * r * w

        qn = norm(q_ref[...], qw_ref[...]) * scale
        kn = norm(k_ref[...], kw_ref[...])
        v = v_ref[...]
        knt = kn.T.astype(BF16)
        for a in range(2):
            sl = slice(a * HEAD_DIM, (a + 1) * HEAD_DIM)
            qs_ref[0, a] = qn[:, sl].astype(BF16)
            kn_ref[0, a] = kn[:, sl].astype(BF16)
            vb_ref[0, a] = v[:, sl].astype(BF16)
            kt_ref[0, a] = knt[sl, :]

    hm = jax.ShapeDtypeStruct((nb, N_HEADS, seq, HEAD_DIM), BF16)
    hm_spec = pl.BlockSpec((1, 2, tq, HEAD_DIM), lambda b, i, h: (b, h, i, 0))
    return pl.pallas_call(
        body, name="qk_prep",
        grid=(nb, nl, N_PAIRS),
        in_specs=[pl.BlockSpec((tq, LANES), lambda b, i, h: (b * nl + i, h)),
                  pl.BlockSpec((tq, LANES), lambda b, i, h: (b * nl + i, N_PAIRS + h)),
                  pl.BlockSpec((tq, LANES), lambda b, i, h: (b * nl + i, 2 * N_PAIRS + h)),
                  pl.BlockSpec((1, LANES), lambda b, i, h: (0, 0)),
                  pl.BlockSpec((1, LANES), lambda b, i, h: (0, 0))],
        out_specs=[hm_spec, hm_spec, hm_spec,
                   pl.BlockSpec((1, 2, HEAD_DIM, tq), lambda b, i, h: (b, h, 0, i))],
        out_shape=[hm, hm, hm, jax.ShapeDtypeStruct((nb, N_HEADS, HEAD_DIM, seq), BF16)],
        compiler_params=_params(3),
    )(proj, proj, proj, qw2, kw2)


def _attn_fwd(qs, kn, vb, nb, seq, blk):
    nq = seq // blk

    def body(q_ref, k_ref, v_ref, o_ref, tot_ref):
        qi = pl.program_id(2)
        r_i = lax.broadcasted_iota(jnp.int32, (blk, blk), 0)
        c_i = lax.broadcasted_iota(jnp.int32, (blk, blk), 1)
        csum = jnp.where(r_i >= c_i, 1.0, 0.0).astype(BF16)
        causal = c_i < r_i

        for a in range(2):
            q = q_ref[0, a]

            def tile(j, carry, masked, q=q, a=a):
                run, acc = carry
                off = pl.multiple_of(j * blk, blk)
                kj = k_ref[0, a, pl.ds(off, blk), :]
                vj = v_ref[0, a, pl.ds(off, blk), :]
                z = _dot(q, kj, _NT)
                lk = -_softplus(z)
                if masked:
                    lk = jnp.where(causal, lk, 0.0)
                hi, lo = _split_bf16(lk)
                cl = _dot(hi, csum) + _dot(lo, csum)
                w = jnp.exp(z + cl + run)
                if masked:
                    w = jnp.where(causal, w, 0.0)
                acc = acc + _dot(w.astype(BF16), vj)
                return run + cl[:, 0:1], acc

            carry = (jnp.zeros((blk, 1), F32), jnp.zeros((blk, HEAD_DIM), F32))
            carry = tile(qi, carry, True)
            carry = lax.fori_loop(1, qi + 1, lambda s, cr: tile(qi - s, cr, False), carry)
            o_ref[:, a * HEAD_DIM:(a + 1) * HEAD_DIM] = carry[1]
            tot_ref[0, a, 0] = jnp.broadcast_to(carry[0], (blk, LANES)).T[0:8, :]

    return pl.pallas_call(
        body, name="sb_attn_fwd",
        grid=(nb, N_PAIRS, nq),
        in_specs=[pl.BlockSpec((1, 2, blk, HEAD_DIM), lambda b, h, i: (b, h, i, 0)),
                  pl.BlockSpec((1, 2, seq, HEAD_DIM), lambda b, h, i: (b, h, 0, 0)),
                  pl.BlockSpec((1, 2, seq, HEAD_DIM), lambda b, h, i: (b, h, 0, 0))],
        out_specs=[pl.BlockSpec((blk, LANES), lambda b, h, i: (b * nq + i, h)),
                   pl.BlockSpec((1, 2, 1, 8, blk), lambda b, h, i: (b, h, i, 0, 0))],
        out_shape=[jax.ShapeDtypeStruct((nb * seq, D_BRANCH), F32),
                   jax.ShapeDtypeStruct((nb, N_HEADS, nq, 8, blk), F32)],
        compiler_params=_params(3),
    )(qs, kn, vb)


def _attn_bwd(qs, kn, kt, vb, tot, d_o, nb, seq, blk):
    nq = seq // blk

    def body(q_ref, k_ref, kt_ref, v_ref, tot_ref, do_ref, dq_ref, dk_ref, dv_ref):
        qi = pl.program_id(2)

        @pl.when(qi == 0)
        def _():
            dk_ref[...] = jnp.zeros_like(dk_ref)
            dv_ref[...] = jnp.zeros_like(dv_ref)

        r_i = lax.broadcasted_iota(jnp.int32, (blk, blk), 0)
        c_i = lax.broadcasted_iota(jnp.int32, (blk, blk), 1)
        before = jnp.where(c_i < r_i, 1.0, 0.0).astype(BF16)
        upto = jnp.where(c_i <= r_i, 1.0, 0.0).astype(BF16)
        causal = r_i < c_i

        for a in range(2):
            sl = slice(a * HEAD_DIM, (a + 1) * HEAD_DIM)
            q = q_ref[0, a]
            d_ob = do_ref[:, sl].astype(BF16)
            total = tot_ref[0, a, 0][0:1, :]

            def tile(j, carry, masked, q=q, a=a, d_ob=d_ob, total=total):
                lsum, esum, dqt = carry
                off = pl.multiple_of(j * blk, blk)
                kj = k_ref[0, a, pl.ds(off, blk), :]
                vj = v_ref[0, a, pl.ds(off, blk), :]
                ktj = kt_ref[0, a, :, pl.ds(off, blk)]
                zt = _dot(kj, q, _NT)
                sp = _softplus(zt)
                lk = -sp
                if masked:
                    lk = jnp.where(causal, lk, 0.0)
                hi, lo = _split_bf16(lk)
                lpre = _dot(before, hi) + _dot(before, lo)
                wt = jnp.exp(zt + (total - lsum - lpre))
                if masked:
                    wt = jnp.where(causal, wt, 0.0)
                et = _dot(vj, d_ob, _NT) * wt
                e_hi, e_lo = _split_bf16(et)
                epre = _dot(upto, e_hi) + _dot(upto, e_lo)
                dzt = et - jnp.exp(zt - sp) * (esum + epre)
                if masked:
                    dzt = jnp.where(causal, dzt, 0.0)
                dzb = dzt.astype(BF16)
                dqt = dqt + _dot(ktj, dzb)
                dk_ref[0, a, pl.ds(off, blk), :] += _dot(dzb, q)
                dv_ref[0, a, pl.ds(off, blk), :] += _dot(wt.astype(BF16), d_ob)
                return (lsum + lpre[blk - 1:blk, :] + lk[blk - 1:blk, :],
                        esum + epre[blk - 1:blk, :], dqt)

            carry = (jnp.zeros((1, blk), F32), jnp.zeros((1, blk), F32),
                     jnp.zeros((HEAD_DIM, blk), F32))
            carry = lax.fori_loop(0, qi, lambda j, cr: tile(j, cr, False), carry)
            carry = tile(qi, carry, True)
            dq_ref[:, sl] = carry[2].T

    hm_acc = pl.BlockSpec((1, 2, seq, HEAD_DIM), lambda b, h, i: (b, h, 0, 0))
    tok = pl.BlockSpec((blk, LANES), lambda b, h, i: (b * nq + i, h))
    hm_shape = jax.ShapeDtypeStruct((nb, N_HEADS, seq, HEAD_DIM), F32)
    return pl.pallas_call(
        body, name="sb_attn_bwd",
        grid=(nb, N_PAIRS, nq),
        in_specs=[pl.BlockSpec((1, 2, blk, HEAD_DIM), lambda b, h, i: (b, h, i, 0)),
                  hm_acc,
                  pl.BlockSpec((1, 2, HEAD_DIM, seq), lambda b, h, i: (b, h, 0, 0)),
                  hm_acc,
                  pl.BlockSpec((1, 2, 1, 8, blk), lambda b, h, i: (b, h, i, 0, 0)),
                  tok],
        out_specs=[tok, hm_acc, hm_acc],
        out_shape=[jax.ShapeDtypeStruct((nb * seq, D_BRANCH), F32), hm_shape, hm_shape],
        compiler_params=_params(3),
    )(qs, kn, kt, vb, tot, d_o)


def _qk_bwd(proj, dqs, dkn, dvh, qw2, kw2, nb, seq, tq):
    nl = seq // tq
    scale = 1.0 / math.sqrt(HEAD_DIM)

    def body(q_ref, k_ref, dq_ref, dk_ref, dv_ref, qw_ref, kw_ref,
             dqr_ref, dkr_ref, dvr_ref, gq_ref, gk_ref):
        @pl.when((pl.program_id(0) == 0) & (pl.program_id(1) == 0) & (pl.program_id(2) == 0))
        def _():
            gq_ref[...] = jnp.zeros_like(gq_ref)
            gk_ref[...] = jnp.zeros_like(gk_ref)

        def norm_bwd(x, w, dy):
            r = lax.rsqrt(_pair_sum(x * x) * (1.0 / HEAD_DIM) + EPS)
            xhat = x * r
            g = dy * w
            m = _pair_sum(g * xhat) * (1.0 / HEAD_DIM)
            return r * (g - xhat * m), jnp.sum(dy * xhat, axis=0, keepdims=True)

        dqr, gq = norm_bwd(q_ref[...], qw_ref[...], dq_ref[...] * scale)
        dk2 = jnp.concatenate([dk_ref[0, 0], dk_ref[0, 1]], axis=1)
        dkr, gk = norm_bwd(k_ref[...], kw_ref[...], dk2)
        dqr_ref[...] = dqr.astype(BF16)
        dkr_ref[...] = dkr.astype(BF16)
        dvr_ref[...] = jnp.concatenate([dv_ref[0, 0], dv_ref[0, 1]], axis=1).astype(BF16)
        gq_ref[...] += gq
        gk_ref[...] += gk

    tok = lambda off: pl.BlockSpec((tq, LANES), lambda b, i, h: (b * nl + i, off + h))
    hm = pl.BlockSpec((1, 2, tq, HEAD_DIM), lambda b, i, h: (b, h, i, 0))
    vec = pl.BlockSpec((1, LANES), lambda b, i, h: (0, 0))
    tshape = jax.ShapeDtypeStruct((nb * seq, D_BRANCH), BF16)
    return pl.pallas_call(
        body, name="qk_bwd",
        grid=(nb, nl, N_PAIRS),
        in_specs=[tok(0), tok(N_PAIRS), tok(0), hm, hm, vec, vec],
        out_specs=[tok(0), tok(0), tok(0), vec, vec],
        out_shape=[tshape, tshape, tshape,
                   jax.ShapeDtypeStruct((1, LANES), F32), jax.ShapeDtypeStruct((1, LANES), F32)],
        compiler_params=_params(3),
    )(proj, proj, dqs, dkn, dvh, qw2, kw2)


def _shift_down(cur, prev, k):
    if k == 0:
        return cur
    rows = _row_iota(cur.shape)
    return jnp.where(rows < k, pltpu.roll(prev, k, axis=0), pltpu.roll(cur, k, axis=0))


def _shift_up(cur, nxt, k):
    if k == 0:
        return cur
    n = cur.shape[0]
    rows = _row_iota(cur.shape)
    return jnp.where(rows < n - k, pltpu.roll(cur, n - k, axis=0), pltpu.roll(nxt, n - k, axis=0))


def _conv_pre(cur, prev, w, b):
    out = b
    for i in range(CONV_TAPS):
        out = out + _shift_down(cur, prev, CONV_TAPS - 1 - i) * w[i:i + 1, :]
    return out


def _silu(x):
    return x * _sigmoid(x)


def _silu_grad(x):
    s = _sigmoid(x)
    return s * (1.0 + x * (1.0 - s))


def _chunk_decay(dt_raw, dtb, alog, expand, qc):
    dt = _softplus(dt_raw + dtb)
    d_a = dt * (-jnp.exp(alog))
    r_i = lax.broadcasted_iota(jnp.int32, (qc, qc), 0)
    c_i = lax.broadcasted_iota(jnp.int32, (qc, qc), 1)
    tril = r_i >= c_i
    a_cs = _dot(jnp.where(tril, 1.0, 0.0).astype(F32), d_a, precision=HIGHEST)
    dt_x = _dot(dt, expand, precision=HIGHEST)
    acs_x = _dot(a_cs, expand, precision=HIGHEST)
    return dt, d_a, a_cs, dt_x, acs_x, tril


def _ssd_fwd(proj, conv_w, conv_b, dtb, alog, dskip, nb, seq):
    qc = SSD_CHUNK
    nc = seq // qc

    def body(xs_ref, bc_ref, dt_ref, cw_ref, cb_ref, dtb_ref, al_ref, ds_ref,
             y_ref, st_ref, pxs_ref, pbc_ref, state_ref):
        @pl.when(pl.program_id(1) == 0)
        def _():
            pxs_ref[...] = jnp.zeros_like(pxs_ref)
            pbc_ref[...] = jnp.zeros_like(pbc_ref)
            state_ref[...] = jnp.zeros_like(state_ref)

        expand = _head_expand()
        xs_raw = xs_ref[...]
        bc_raw = bc_ref[...]
        cw = cw_ref[...]
        cb = cb_ref[...]
        xs = _silu(_conv_pre(xs_raw, pxs_ref[...], cw[:, :D_BRANCH], cb[:, :D_BRANCH]))
        bc = _silu(_conv_pre(bc_raw, pbc_ref[...], cw[:, D_BRANCH:], cb[:, D_BRANCH:]))
        pxs_ref[...] = xs_raw
        pbc_ref[...] = bc_raw

        dt, d_a, a_cs, dt_x, acs_x, tril = _chunk_decay(
            dt_ref[...], dtb_ref[...], al_ref[...], expand, qc)
        a_cst = a_cs.T
        aend_x = acs_x[qc - 1:qc, :]
        ea_x = jnp.exp(acs_x)
        dec_x = jnp.exp(aend_x - acs_x)
        xt = xs * dt_x
        xtb = xt.astype(BF16)
        xdb = (xt * dec_x).astype(BF16)
        d_x = _dot(jnp.broadcast_to(ds_ref[...], (8, LANES)), expand, precision=HIGHEST)[0:1, :]
        st_ref[0, 0] = state_ref[...]

        for g in range(N_GROUPS):
            gs = slice(g * GROUP_W, (g + 1) * GROUP_W)
            bg = bc[:, g * D_STATE:(g + 1) * D_STATE]
            cg = bc[:, (N_GROUPS + g) * D_STATE:(N_GROUPS + g + 1) * D_STATE]
            bgb = bg.astype(BF16)
            cgb = cg.astype(BF16)
            cbm = _dot(cgb, bgb, _NT)
            st_in = state_ref[g]
            y_off = _dot(cgb, st_in.astype(BF16)) * ea_x[:, gs]
            for k in range(HEADS_PER_GROUP):
                h = g * HEADS_PER_GROUP + k
                hs = slice(h * HEAD_DIM, (h + 1) * HEAD_DIM)
                seg = a_cs[:, h:h + 1] - a_cst[h:h + 1, :]
                gh = cbm * jnp.exp(jnp.where(tril, seg, -1e30))
                y_h = _dot(gh.astype(BF16), xtb[:, hs]) + y_off[:, k * HEAD_DIM:(k + 1) * HEAD_DIM]
                y_ref[:, hs] = y_h + d_x[:, hs] * xs[:, hs]
            state_ref[g] = st_in * jnp.exp(aend_x[:, gs]) + _dot(bg.T.astype(BF16), xdb[:, gs])

    nblk = lambda w, off: pl.BlockSpec((qc, w), lambda b, c: (b * nc + c, off))
    full = lambda r, w: pl.BlockSpec((r, w), lambda b, c: (0, 0))
    return pl.pallas_call(
        body, name="ssd_fwd",
        grid=(nb, nc),
        in_specs=[nblk(D_BRANCH, COL_XS // D_BRANCH), nblk(D_BC, COL_BC // D_BC),
                  nblk(LANES, COL_DT // LANES),
                  full(CONV_TAPS, D_CONV), full(1, D_CONV), full(1, LANES), full(1, LANES),
                  full(1, LANES)],
        out_specs=[pl.BlockSpec((qc, D_BRANCH), lambda b, c: (b * nc + c, 0)),
                   pl.BlockSpec((1, 1, N_GROUPS, D_STATE, GROUP_W), lambda b, c: (b, c, 0, 0, 0))],
        out_shape=[jax.ShapeDtypeStruct((nb * seq, D_BRANCH), F32),
                   jax.ShapeDtypeStruct((nb, nc, N_GROUPS, D_STATE, GROUP_W), F32)],
        scratch_shapes=[pltpu.VMEM((qc, D_BRANCH), F32), pltpu.VMEM((qc, D_BC), F32),
                        pltpu.VMEM((N_GROUPS, D_STATE, GROUP_W), F32)],
        compiler_params=_params(2),
    )(proj, proj, proj, conv_w, conv_b, dtb, alog, dskip)


def _ssd_bwd(proj, d_y, states, conv_w, conv_b, dtb, alog, dskip, nb, seq):
    qc = SSD_CHUNK
    nc = seq // qc

    def body(xs_ref, bc_ref, dt_ref, pxs_ref, pbc_ref, dy_ref, st_ref, stn_ref,
             cw_ref, cb_ref, dtb_ref, al_ref, ds_ref,
             dx_ref, gcw_ref, gcb_ref, gdtb_ref, gal_ref, gds_ref,
             dst_ref, nxs_ref, nbc_ref, yd_ref, dxt_ref):
        step = pl.program_id(1)
        chunk = nc - 1 - step

        @pl.when(step == 0)
        def _():
            dst_ref[...] = jnp.zeros_like(dst_ref)
            nxs_ref[...] = jnp.zeros_like(nxs_ref)
            nbc_ref[...] = jnp.zeros_like(nbc_ref)

        @pl.when((pl.program_id(0) == 0) & (step == 0))
        def _():
            gcw_ref[...] = jnp.zeros_like(gcw_ref)
            gcb_ref[...] = jnp.zeros_like(gcb_ref)
            gdtb_ref[...] = jnp.zeros_like(gdtb_ref)
            gal_ref[...] = jnp.zeros_like(gal_ref)
            gds_ref[...] = jnp.zeros_like(gds_ref)

        expand = _head_expand()
        collapse = lambda v: _dot(v, expand, _NT, precision=HIGHEST)
        first = jnp.where(chunk == 0, 0.0, 1.0)
        xs_raw = xs_ref[...]
        bc_raw = bc_ref[...]
        pxs = pxs_ref[...] * first
        pbc = pbc_ref[...] * first
        cw = cw_ref[...]
        cb = cb_ref[...]
        pre_xs = _conv_pre(xs_raw, pxs, cw[:, :D_BRANCH], cb[:, :D_BRANCH])
        pre_bc = _conv_pre(bc_raw, pbc, cw[:, D_BRANCH:], cb[:, D_BRANCH:])
        xs = _silu(pre_xs)
        bc = _silu(pre_bc)

        dt_in = dt_ref[...] + dtb_ref[...]
        dt, d_a, a_cs, dt_x, acs_x, tril = _chunk_decay(
            dt_ref[...], dtb_ref[...], al_ref[...], expand, qc)
        a_cst = a_cs.T
        aend_x = acs_x[qc - 1:qc, :]
        ea_x = jnp.exp(acs_x)
        dec_x = jnp.exp(aend_x - acs_x)
        xt = xs * dt_x
        xtb = xt.astype(BF16)
        xdb = (xt * dec_x).astype(BF16)
        d_x = _dot(jnp.broadcast_to(ds_ref[...], (8, LANES)), expand, precision=HIGHEST)[0:1, :]

        dy = dy_ref[...]
        dyb = dy.astype(BF16)
        dyeab = (dy * ea_x).astype(BF16)
        gds_ref[...] += collapse(jnp.broadcast_to(jnp.sum(dy * xs, axis=0, keepdims=True),
                                                  (8, D_BRANCH)))[0:1, :]

        d_bc = []
        d_cc = []
        y_offs = []
        dxt_states = []
        end_terms = []
        for g in range(N_GROUPS):
            gs = slice(g * GROUP_W, (g + 1) * GROUP_W)
            bg = bc[:, g * D_STATE:(g + 1) * D_STATE]
            cg = bc[:, (N_GROUPS + g) * D_STATE:(N_GROUPS + g + 1) * D_STATE]
            bgb = bg.astype(BF16)
            cgb = cg.astype(BF16)
            cbm = _dot(cgb, bgb, _NT)
            st_in = st_ref[0, 0, g]
            st_inb = st_in.astype(BF16)
            d_st = dst_ref[g]
            d_stb = d_st.astype(BF16)
            y_offs.append(_dot(cgb, st_inb) * ea_x[:, gs])
            dxt_states.append(_dot(bgb, d_stb) * dec_x[:, gs])
            d_c = _dot(dyeab[:, gs], st_inb, _NT)
            d_b = _dot(xdb[:, gs], d_stb, _NT)
            d_cb = jnp.zeros((qc, qc), F32)
            for k in range(HEADS_PER_GROUP):
                h = g * HEADS_PER_GROUP + k
                hs = slice(h * HEAD_DIM, (h + 1) * HEAD_DIM)
                seg = a_cs[:, h:h + 1] - a_cst[h:h + 1, :]
                lh = jnp.exp(jnp.where(tril, seg, -1e30))
                ghb = (cbm * lh).astype(BF16)
                d_cb = d_cb + _dot(dyb[:, hs], xtb[:, hs], _NT) * lh
                yd_ref[:, hs] = _dot(ghb, xtb[:, hs])
                dxt_ref[:, hs] = _dot(ghb, dyb[:, hs], _TN)
            d_cbb = d_cb.astype(BF16)
            d_cc.append(d_c + _dot(d_cbb, bgb))
            d_bc.append(d_b + _dot(d_cbb, cgb, _TN))
            end_terms.append(jnp.sum(d_st * stn_ref[0, 0, g], axis=0, keepdims=True))
            dst_ref[g] = d_st * jnp.exp(aend_x[:, gs]) + _dot(cg.T.astype(BF16), dyeab[:, gs])

        y_off = jnp.concatenate(y_offs, axis=1)
        dxt_state = jnp.concatenate(dxt_states, axis=1)
        dxt = dxt_ref[...] + dxt_state
        last = jnp.where(chunk == nc - 1, 0.0, 1.0)
        end_c = collapse(jnp.broadcast_to(jnp.concatenate(end_terms, axis=1), (8, D_BRANCH)))[0:1, :]
        da_cs = collapse(dyb.astype(F32) * yd_ref[...] - dxt_ref[...] * xtb.astype(F32)
                         + dy * y_off - dxt_state * xt)
        da_cs = da_cs + jnp.where(_row_iota(da_cs.shape) == qc - 1, end_c * last, 0.0)
        triu = lax.broadcasted_iota(jnp.int32, (qc, qc), 0) <= lax.broadcasted_iota(jnp.int32, (qc, qc), 1)
        dd_a = _dot(jnp.where(triu, 1.0, 0.0).astype(F32), da_cs, precision=HIGHEST)
        ddt = dd_a * (-jnp.exp(al_ref[...])) + collapse(dxt * xs)
        head_lanes = _lane_iota(ddt.shape) < N_HEADS
        ddt_raw = jnp.where(head_lanes, ddt * _sigmoid(dt_in), 0.0)
        gal_ref[...] += jnp.sum(jnp.where(head_lanes, dd_a * d_a, 0.0), axis=0, keepdims=True)
        gdtb_ref[...] += jnp.sum(ddt_raw, axis=0, keepdims=True)

        dpre_xs = (dxt * dt_x + d_x * dy) * _silu_grad(pre_xs)
        dpre_bc = jnp.concatenate(d_bc + d_cc, axis=1) * _silu_grad(pre_bc)
        gcb_ref[...] += jnp.concatenate([jnp.sum(dpre_xs, axis=0, keepdims=True),
                                         jnp.sum(dpre_bc, axis=0, keepdims=True)], axis=1)
        nxs = nxs_ref[...]
        nbc = nbc_ref[...]
        du_xs = jnp.zeros_like(dpre_xs)
        du_bc = jnp.zeros_like(dpre_bc)
        for i in range(CONV_TAPS):
            k = CONV_TAPS - 1 - i
            gcw_ref[i:i + 1, :] += jnp.concatenate(
                [jnp.sum(dpre_xs * _shift_down(xs_raw, pxs, k), axis=0, keepdims=True),
                 jnp.sum(dpre_bc * _shift_down(bc_raw, pbc, k), axis=0, keepdims=True)], axis=1)
            du_xs = du_xs + _shift_up(dpre_xs, nxs, k) * cw[i:i + 1, :D_BRANCH]
            du_bc = du_bc + _shift_up(dpre_bc, nbc, k) * cw[i:i + 1, D_BRANCH:]
        nxs_ref[...] = dpre_xs
        nbc_ref[...] = dpre_bc

        dx_ref[:, :D_BRANCH] = du_xs.astype(BF16)
        dx_ref[:, D_BRANCH:D_CONV] = du_bc.astype(BF16)
        dx_ref[:, D_CONV:D_CONV + LANES] = ddt_raw.astype(BF16)
        dx_ref[:, D_CONV + LANES:] = jnp.zeros((qc, 2048 - D_CONV - LANES), BF16)

    rev = lambda b, c: b * nc + (nc - 1 - c)
    prv = lambda b, c: b * nc + jnp.maximum(nc - 2 - c, 0)
    nblk = lambda w, off, f: pl.BlockSpec((qc, w), lambda b, c: (f(b, c), off))
    full = lambda r, w: pl.BlockSpec((r, w), lambda b, c: (0, 0))
    st_spec = lambda f: pl.BlockSpec((1, 1, N_GROUPS, D_STATE, GROUP_W),
                                     lambda b, c: (b, f(c), 0, 0, 0))
    return pl.pallas_call(
        body, name="ssd_bwd",
        grid=(nb, nc),
        in_specs=[nblk(D_BRANCH, COL_XS // D_BRANCH, rev), nblk(D_BC, COL_BC // D_BC, rev),
                  nblk(LANES, COL_DT // LANES, rev),
                  nblk(D_BRANCH, COL_XS // D_BRANCH, prv), nblk(D_BC, COL_BC // D_BC, prv),
                  nblk(D_BRANCH, 0, rev),
                  st_spec(lambda c: nc - 1 - c), st_spec(lambda c: jnp.minimum(nc - c, nc - 1)),
                  full(CONV_TAPS, D_CONV), full(1, D_CONV), full(1, LANES), full(1, LANES),
                  full(1, LANES)],
        out_specs=[nblk(2048, 0, rev), full(8, D_CONV), full(1, D_CONV), full(1, LANES),
                   full(1, LANES), full(1, LANES)],
        out_shape=[jax.ShapeDtypeStruct((nb * seq, 2048), BF16),
                   jax.ShapeDtypeStruct((8, D_CONV), F32), jax.ShapeDtypeStruct((1, D_CONV), F32),
                   jax.ShapeDtypeStruct((1, LANES), F32), jax.ShapeDtypeStruct((1, LANES), F32),
                   jax.ShapeDtypeStruct((1, LANES), F32)],
        scratch_shapes=[pltpu.VMEM((N_GROUPS, D_STATE, GROUP_W), F32),
                        pltpu.VMEM((qc, D_BRANCH), F32), pltpu.VMEM((qc, D_BC), F32),
                        pltpu.VMEM((qc, D_BRANCH), F32), pltpu.VMEM((qc, D_BRANCH), F32)],
        compiler_params=_params(2),
    )(proj, proj, proj, proj, proj, d_y, states, states, conv_w, conv_b, dtb, alog, dskip)


def _mid(o_sb, y_ssd, proj, x2, target, sb_w, ssd_w, w_out_b, tm):
    t = x2.shape[0]
    inv_d = 1.0 / D_MODEL

    def body(o_ref, y_ref, zsb_ref, zssd_ref, x_ref, tg_ref, sbw_ref, ssdw_ref, w_ref,
             dout_ref, dosb_ref, dy_ref, dz_ref, gw_ref, gsb_ref, gssd_ref, loss_ref):
        @pl.when(pl.program_id(0) == 0)
        def _():
            gw_ref[...] = jnp.zeros_like(gw_ref)
            gsb_ref[...] = jnp.zeros_like(gsb_ref)
            gssd_ref[...] = jnp.zeros_like(gssd_ref)
            loss_ref[...] = jnp.zeros_like(loss_ref)

        def branch(val, z, w):
            gate = _silu(z)
            g = val * gate
            r = lax.rsqrt(jnp.mean(g * g, axis=1, keepdims=True) + EPS)
            xhat = g * r
            return gate, r, xhat, (xhat * w).astype(BF16)

        o = o_ref[...]
        y = y_ref[...]
        z_sb = zsb_ref[...]
        z_ssd = zssd_ref[...]
        gate_a, r_a, xhat_a, mix_a = branch(o, z_sb, sbw_ref[...])
        gate_b, r_b, xhat_b, mix_b = branch(y, z_ssd, ssdw_ref[...])
        out = x_ref[...] + _dot(mix_a, w_ref[:D_BRANCH, :]) + _dot(mix_b, w_ref[D_BRANCH:, :])
        diff = out - tg_ref[...]
        loss_ref[...] += 0.5 * inv_d * jnp.sum(diff * diff)
        d_out = diff * inv_d
        dout_ref[...] = d_out
        d_outb = d_out.astype(BF16)
        gw_ref[:D_BRANCH, :] += _dot(mix_a, d_outb, _TN)
        gw_ref[D_BRANCH:, :] += _dot(mix_b, d_outb, _TN)

        def branch_bwd(dmix, val, z, w, gate, r, xhat):
            gg = dmix * w
            m = jnp.mean(gg * xhat, axis=1, keepdims=True)
            dg = r * (gg - xhat * m)
            return dg * gate, dg * val * _silu_grad(z), jnp.sum(dmix * xhat, axis=0, keepdims=True)

        dmix_a = _dot(d_outb, w_ref[:D_BRANCH, :], _NT)
        dmix_b = _dot(d_outb, w_ref[D_BRANCH:, :], _NT)
        d_o, dz_a, gsb = branch_bwd(dmix_a, o, z_sb, sbw_ref[...], gate_a, r_a, xhat_a)
        d_y, dz_b, gssd = branch_bwd(dmix_b, y, z_ssd, ssdw_ref[...], gate_b, r_b, xhat_b)
        dosb_ref[...] = d_o
        dy_ref[...] = d_y
        dz_ref[:, :D_BRANCH] = dz_a.astype(BF16)
        dz_ref[:, D_BRANCH:] = dz_b.astype(BF16)
        gsb_ref[...] += gsb
        gssd_ref[...] += gssd

    row = lambda w, off: pl.BlockSpec((tm, w), lambda i: (i, off))
    full = lambda r, w: pl.BlockSpec((r, w), lambda i: (0, 0))
    tok = jax.ShapeDtypeStruct((t, D_MODEL), F32)
    return pl.pallas_call(
        body, name="mid",
        grid=(t // tm,),
        in_specs=[row(D_BRANCH, 0), row(D_BRANCH, 0), row(D_BRANCH, 3), row(D_BRANCH, 4),
                  row(D_MODEL, 0), row(D_MODEL, 0), full(1, D_BRANCH), full(1, D_BRANCH),
                  full(2 * D_BRANCH, D_MODEL)],
        out_specs=[row(D_MODEL, 0), row(D_BRANCH, 0), row(D_BRANCH, 0), row(2 * D_BRANCH, 0),
                   full(2 * D_BRANCH, D_MODEL), full(1, D_BRANCH), full(1, D_BRANCH),
                   full(1, LANES)],
        out_shape=[tok, tok, tok, jax.ShapeDtypeStruct((t, 2 * D_BRANCH), BF16),
                   jax.ShapeDtypeStruct((2 * D_BRANCH, D_MODEL), F32),
                   jax.ShapeDtypeStruct((1, D_BRANCH), F32), jax.ShapeDtypeStruct((1, D_BRANCH), F32),
                   jax.ShapeDtypeStruct((1, LANES), F32)],
        compiler_params=_params(1),
    )(o_sb, y_ssd, proj, proj, x2, target, sb_w, ssd_w, w_out_b)


def _in_proj_bwd_x(d_proj, w_in_b, x2, d_out, norm_w, tm):
    t = x2.shape[0]

    def body(dp_ref, w_ref, x_ref, dout_ref, nw_ref, gx_ref, gnw_ref, acc_ref):
        j = pl.program_id(1)

        @pl.when((pl.program_id(0) == 0) & (j == 0))
        def _():
            gnw_ref[...] = jnp.zeros_like(gnw_ref)

        @pl.when(j == 0)
        def _():
            acc_ref[...] = jnp.zeros_like(acc_ref)

        acc_ref[...] += _dot(dp_ref[...], w_ref[...], _NT)

        @pl.when(j == N_COLBLK - 1)
        def _():
            xf = x_ref[...]
            d_hn = acc_ref[...]
            r = lax.rsqrt(jnp.mean(xf * xf, axis=1, keepdims=True) + EPS)
            xhat = xf * r
            g = d_hn * nw_ref[...]
            m = jnp.mean(g * xhat, axis=1, keepdims=True)
            gx_ref[...] = dout_ref[...] + r * (g - xhat * m)
            gnw_ref[...] += jnp.sum(d_hn * xhat, axis=0, keepdims=True)

    return pl.pallas_call(
        body, name="in_proj_bwd_x",
        grid=(t // tm, N_COLBLK),
        in_specs=[pl.BlockSpec((tm, 1024), lambda i, j: (i, j)),
                  pl.BlockSpec((D_MODEL, 1024), lambda i, j: (0, j)),
                  pl.BlockSpec((tm, D_MODEL), lambda i, j: (i, 0)),
                  pl.BlockSpec((tm, D_MODEL), lambda i, j: (i, 0)),
                  pl.BlockSpec((1, D_MODEL), lambda i, j: (0, 0))],
        out_specs=[pl.BlockSpec((tm, D_MODEL), lambda i, j: (i, 0)),
                   pl.BlockSpec((1, D_MODEL), lambda i, j: (0, 0))],
        out_shape=[jax.ShapeDtypeStruct((t, D_MODEL), F32), jax.ShapeDtypeStruct((1, D_MODEL), F32)],
        scratch_shapes=[pltpu.VMEM((tm, D_MODEL), F32)],
        compiler_params=_params(2),
    )(d_proj, w_in_b, x2, d_out, norm_w)


def _in_proj_bwd_w(hn, d_proj, tm):
    t = hn.shape[0]

    def body(hn_ref, dp_ref, gw_ref):
        @pl.when(pl.program_id(1) == 0)
        def _():
            gw_ref[...] = jnp.zeros_like(gw_ref)

        gw_ref[...] += _dot(hn_ref[...], dp_ref[...], _TN)

    return pl.pallas_call(
        body, name="in_proj_bwd_w",
        grid=(N_COLBLK, t // tm),
        in_specs=[pl.BlockSpec((tm, D_MODEL), lambda j, i: (i, 0)),
                  pl.BlockSpec((tm, 1024), lambda j, i: (i, j))],
        out_specs=pl.BlockSpec((D_MODEL, 1024), lambda j, i: (0, j)),
        out_shape=jax.ShapeDtypeStruct((D_MODEL, D_IN_PAD), F32),
        compiler_params=_params(2),
    )(hn, d_proj)


def _adamw(parts, w, m, v, tr, name):
    rows, cols = w.shape
    c1 = 1.0 - ADAM_B1 ** ADAM_STEP
    c2 = 1.0 - ADAM_B2 ** ADAM_STEP

    def body(p_ref, w_ref, m_ref, v_ref, g_ref, d_ref, nm_ref, nv_ref):
        g = p_ref[0]
        for s in range(1, N_DEV):
            g = g + p_ref[s]
        nm = ADAM_B1 * m_ref[...] + (1.0 - ADAM_B1) * g
        nv = ADAM_B2 * v_ref[...] + (1.0 - ADAM_B2) * (g * g)
        g_ref[...] = g
        nm_ref[...] = nm
        nv_ref[...] = nv
        d_ref[...] = -ADAM_LR * ((nm / c1) / (jnp.sqrt(nv / c2) + ADAM_EPS) + ADAM_WD * w_ref[...])

    blk = pl.BlockSpec((tr, cols), lambda i: (i, 0))
    shape = jax.ShapeDtypeStruct((rows, cols), F32)
    return pl.pallas_call(
        body, name=name,
        grid=(rows // tr,),
        in_specs=[pl.BlockSpec((N_DEV, tr, cols), lambda i: (0, i, 0)), blk, blk, blk],
        out_specs=[blk, blk, blk, blk],
        out_shape=[shape, shape, shape, shape],
        compiler_params=_params(1),
    )(parts, w, m, v)


def _mesh_place():
    x, y, c = lax.axis_index("x"), lax.axis_index("y"), lax.axis_index("c")
    return x, y, c, 4 * x + 2 * y + c


def _peer(x, y, c, k):
    px = 1 - x if k & 4 else x
    py = 1 - y if k & 2 else y
    pc = 1 - c if k & 1 else c
    return (px, py, pc), 4 * px + 2 * py + pc


def _exchange(srcs, scatter, name):
    n = len(srcs)

    def body(*refs):
        src_refs, dst_refs = refs[:n], refs[n:2 * n]
        send_sems, recv_sems, loc_sems = refs[2 * n:]
        x, y, c, me = _mesh_place()

        def src_of(i, idx):
            return src_refs[i].at[idx] if scatter[i] else src_refs[i]

        local = [pltpu.make_async_copy(src_of(i, me), dst_refs[i].at[me], loc_sems.at[i])
                 for i in range(n)]
        for cp in local:
            cp.start()
        sends = []
        for k in range(1, N_DEV):
            peer, pidx = _peer(x, y, c, k)
            for i in range(n):
                s = i * (N_DEV - 1) + k - 1
                cp = pltpu.make_async_remote_copy(
                    src_ref=src_of(i, pidx), dst_ref=dst_refs[i].at[me],
                    send_sem=send_sems.at[s], recv_sem=recv_sems.at[s],
                    device_id=peer, device_id_type=pl.DeviceIdType.MESH)
                cp.start()
                sends.append(cp)
        for k in range(1, N_DEV):
            peer, pidx = _peer(x, y, c, k)
            for i in range(n):
                s = i * (N_DEV - 1) + k - 1
                pltpu.make_async_remote_copy(
                    src_ref=src_of(i, pidx), dst_ref=dst_refs[i].at[pidx],
                    send_sem=send_sems.at[s], recv_sem=recv_sems.at[s],
                    device_id=peer, device_id_type=pl.DeviceIdType.MESH).wait_recv()
        for cp in sends:
            cp.wait_send()
        for cp in local:
            cp.wait()

    out_shape = [jax.ShapeDtypeStruct(s.shape if sc else (N_DEV,) + s.shape, s.dtype)
                 for s, sc in zip(srcs, scatter)]
    any_spec = pl.BlockSpec(memory_space=pl.ANY)
    return pl.pallas_call(
        body, name=name,
        in_specs=[any_spec] * n, out_specs=[any_spec] * n, out_shape=out_shape,
        scratch_shapes=[pltpu.SemaphoreType.DMA((n * (N_DEV - 1),)),
                        pltpu.SemaphoreType.DMA((n * (N_DEV - 1),)),
                        pltpu.SemaphoreType.DMA((n,))],
    )(*srcs)


def _pad_lanes(v, width=LANES):
    return jnp.pad(v, ((0, 0), (0, width - v.shape[1])))


def _local_step(x, target, norm_w, w_in_b, q_norm_w, k_norm_w, conv_w, conv_b, dt_bias, a_log,
                d_skip, sb_norm_w, ssd_norm_w, w_out_b, tm=512, tq=512, tmid=128, blk=ATT_BLK):
    nb, seq, _ = x.shape
    t = nb * seq
    x2 = x.reshape(t, D_MODEL)
    tg2 = target.reshape(t, D_MODEL)
    qw2 = jnp.tile(q_norm_w, (1, 2))
    kw2 = jnp.tile(k_norm_w, (1, 2))
    dtb, alog, dsk = _pad_lanes(dt_bias), _pad_lanes(a_log), _pad_lanes(d_skip)

    proj, hn = _in_proj(x2, norm_w, w_in_b, tm)
    qs, kn, vb, kt = _qk_prep(proj, qw2, kw2, nb, seq, tq)
    o_sb, sb_tot = _attn_fwd(qs, kn, vb, nb, seq, blk)
    y_ssd, states = _ssd_fwd(proj, conv_w, conv_b, dtb, alog, dsk, nb, seq)
    d_out, d_osb, d_y, d_z, g_wout, g_sbw, g_ssdw, loss = _mid(
        o_sb, y_ssd, proj, x2, tg2, sb_norm_w, ssd_norm_w, w_out_b, tmid)
    dqs, dkn, dvh = _attn_bwd(qs, kn, kt, vb, sb_tot, d_osb, nb, seq, blk)
    dq_raw, dk_raw, dv_raw, g_qw, g_kw = _qk_bwd(proj, dqs, dkn, dvh, qw2, kw2, nb, seq, tq)
    d_xbc, g_cw, g_cb, g_dtb, g_alog, g_dsk = _ssd_bwd(
        proj, d_y, states, conv_w, conv_b, dtb, alog, dsk, nb, seq)
    d_proj = jnp.concatenate([dq_raw, dk_raw, dv_raw, d_z, d_xbc], axis=1)
    grad_x, g_nw = _in_proj_bwd_x(d_proj, w_in_b, x2, d_out, norm_w, tm)
    g_win = _in_proj_bwd_w(hn, d_proj, tm)

    small = dict(
        norm_w=g_nw,
        q_norm_w=g_qw[:, :HEAD_DIM] + g_qw[:, HEAD_DIM:],
        k_norm_w=g_kw[:, :HEAD_DIM] + g_kw[:, HEAD_DIM:],
        conv_b=g_cb, dt_bias=g_dtb[:, :N_HEADS], A_log=g_alog[:, :N_HEADS],
        D_skip=g_dsk[:, :N_HEADS], sb_norm_w=g_sbw, ssd_norm_w=g_ssdw)
    return (loss[0, 0], grad_x.reshape(nb, seq, D_MODEL), g_win[:, :D_IN], g_wout,
            g_cw[:CONV_TAPS], small)


_SMALL = ("norm_w", "q_norm_w", "k_norm_w", "conv_b", "dt_bias", "A_log", "D_skip",
          "sb_norm_w", "ssd_norm_w")


def _pack_small(vals):
    rows = [_pad_lanes(vals[n], -(-vals[n].shape[1] // LANES) * LANES).reshape(-1, LANES)
            for n in _SMALL]
    packed = jnp.concatenate(rows, axis=0)
    return jnp.pad(packed, ((0, 48 - packed.shape[0]), (0, 0)))


def _unpack_small(packed, like):
    out, r = {}, 0
    for n in _SMALL:
        width = like[n].shape[1]
        nr = -(-width // LANES)
        out[n] = packed[r:r + nr].reshape(1, nr * LANES)[:, :width]
        r += nr
    return out


def kernel(x, norm_w, w_in, q_norm_w, k_norm_w, conv_w, conv_b, dt_bias, A_log, D_skip, sb_norm_w, ssd_norm_w, w_out, loss_target, m_norm_w, m_w_in, m_q_norm_w, m_k_norm_w, m_conv_w, m_conv_b, m_dt_bias, m_A_log, m_D_skip, m_sb_norm_w, m_ssd_norm_w, m_w_out, v_norm_w, v_w_in, v_q_norm_w, v_k_norm_w, v_conv_w, v_conv_b, v_dt_bias, v_A_log, v_D_skip, v_sb_norm_w, v_ssd_norm_w, v_w_out):
    w_sh = D_IN // N_DEV
    c_sh = D_CONV // N_DEV

    win_all, wout_all, cw_all = _exchange(
        [w_in[0].astype(BF16), w_out[0].astype(BF16), conv_w[0]], [False, False, False],
        "gather_weights")
    w_in_b = jnp.pad(jnp.transpose(win_all, (1, 0, 2)).reshape(D_MODEL, D_IN),
                     ((0, 0), (0, D_IN_PAD - D_IN)))
    w_out_b = wout_all.reshape(2 * D_BRANCH, D_MODEL)
    conv_full = jnp.transpose(cw_all, (1, 0, 2)).reshape(CONV_TAPS, D_CONV)

    loss, grad_x, g_win, g_wout, g_cw, g_small = _local_step(
        x, loss_target, norm_w, w_in_b, q_norm_w, k_norm_w, conv_full, conv_b, dt_bias, A_log,
        D_skip, sb_norm_w, ssd_norm_w, w_out_b)

    win_parts, wout_parts, cw_parts, small_parts = _exchange(
        [jnp.transpose(g_win.reshape(D_MODEL, N_DEV, w_sh), (1, 0, 2)),
         g_wout.reshape(N_DEV, 2 * D_BRANCH // N_DEV, D_MODEL),
         jnp.pad(jnp.transpose(g_cw.reshape(CONV_TAPS, N_DEV, c_sh), (1, 0, 2)),
                 ((0, 0), (0, 8 - CONV_TAPS), (0, 0))),
         _pack_small(g_small)],
        [True, True, True, False], "scatter_grads")

    small_w = dict(norm_w=norm_w, q_norm_w=q_norm_w, k_norm_w=k_norm_w, conv_b=conv_b,
                   dt_bias=dt_bias, A_log=A_log, D_skip=D_skip, sb_norm_w=sb_norm_w,
                   ssd_norm_w=ssd_norm_w)
    small_m = dict(norm_w=m_norm_w, q_norm_w=m_q_norm_w, k_norm_w=m_k_norm_w, conv_b=m_conv_b,
                   dt_bias=m_dt_bias, A_log=m_A_log, D_skip=m_D_skip, sb_norm_w=m_sb_norm_w,
                   ssd_norm_w=m_ssd_norm_w)
    small_v = dict(norm_w=v_norm_w, q_norm_w=v_q_norm_w, k_norm_w=v_k_norm_w, conv_b=v_conv_b,
                   dt_bias=v_dt_bias, A_log=v_A_log, D_skip=v_D_skip, sb_norm_w=v_sb_norm_w,
                   ssd_norm_w=v_ssd_norm_w)

    pad8 = lambda a: jnp.pad(a, ((0, 8 - CONV_TAPS), (0, 0)))
    r_win = _adamw(win_parts, w_in[0], m_w_in[0], v_w_in[0], 128, "adamw_w_in")
    r_wout = _adamw(wout_parts, w_out[0], m_w_out[0], v_w_out[0], 128, "adamw_w_out")
    r_cw = _adamw(cw_parts, pad8(conv_w[0]), pad8(m_conv_w[0]), pad8(v_conv_w[0]), 8, "adamw_conv_w")
    r_small = _adamw(small_parts, _pack_small(small_w), _pack_small(small_m),
                     _pack_small(small_v), 48, "adamw_small")

    loss = lax.psum(loss, ("x", "y", "c"))
    res = {"w_in": [a[None] for a in r_win], "w_out": [a[None] for a in r_wout],
           "conv_w": [a[:CONV_TAPS][None] for a in r_cw]}
    unpacked = [_unpack_small(a, small_w) for a in r_small]
    for n in _SMALL:
        res[n] = [u[n] for u in unpacked]
    order = ("norm_w", "w_in", "q_norm_w", "k_norm_w", "conv_w", "conv_b", "dt_bias", "A_log",
             "D_skip", "sb_norm_w", "ssd_norm_w", "w_out")
    outs = [loss, grad_x]
    for kind in range(4):
        outs += [res[n][kind] for n in order]
    return tuple(outs)
```

```python
import functools
import math

import jax
import jax.numpy as jnp
from jax import lax
from jax.experimental import pallas as pl
from jax.experimental.pallas import tpu as pltpu

F32 = jnp.float32
BF16 = jnp.bfloat16
HIGHEST = lax.Precision.HIGHEST

D_MODEL = 1024
N_HEADS = 16
HEAD_DIM = 64
N_PAIRS = N_HEADS // 2
D_BRANCH = 1024
N_GROUPS = 2
HEADS_PER_GROUP = 8
D_STATE = 128
GROUP_W = HEADS_PER_GROUP * HEAD_DIM
D_BC = 2 * N_GROUPS * D_STATE
D_CONV = D_BRANCH + D_BC
D_IN = 6672
D_IN_PAD = 7168
N_COLBLK = D_IN_PAD // 1024
COL_XS = 5120
COL_BC = 6144
COL_DT = 6656
EPS = 1e-6
CONV_TAPS = 4
N_DEV = 8

LANES = 128
SSD_CHUNK = 128
ATT_BLK = 256
VMEM_LIMIT = 56 * 1024 * 1024

ADAM_LR = 0.001
ADAM_B1 = 0.9
ADAM_B2 = 0.999
ADAM_EPS = 1e-08
ADAM_WD = 0.01
ADAM_STEP = 10

_NT = (((1,), (1,)), ((), ()))
_TN = (((0,), (0,)), ((), ()))


def _params(n_grid):
    return pltpu.CompilerParams(dimension_semantics=("arbitrary",) * n_grid,
                                vmem_limit_bytes=VMEM_LIMIT)


def _dot(a, b, dims=None, precision=None):
    if dims is None:
        return jnp.dot(a, b, preferred_element_type=F32, precision=precision)
    return lax.dot_general(a, b, dims, preferred_element_type=F32, precision=precision)


def _sigmoid(x):
    return 1.0 / (1.0 + jnp.exp(-x))


def _softplus(x):
    return jnp.maximum(x, 0.0) + jnp.log(1.0 + jnp.exp(-jnp.abs(x)))


def _split_bf16(x):
    hi = x.astype(BF16)
    lo = (x - hi.astype(F32)).astype(BF16)
    return hi, lo


def _lane_iota(shape):
    return lax.broadcasted_iota(jnp.int32, shape, len(shape) - 1)


def _row_iota(shape):
    return lax.broadcasted_iota(jnp.int32, shape, len(shape) - 2)


def _pair_sum(x):
    low = _lane_iota(x.shape) < HEAD_DIM
    s0 = jnp.sum(jnp.where(low, x, 0.0), axis=1, keepdims=True)
    s1 = jnp.sum(jnp.where(low, 0.0, x), axis=1, keepdims=True)
    return jnp.where(low, s0, s1)


def _head_expand():
    r = lax.broadcasted_iota(jnp.int32, (LANES, D_BRANCH), 0)
    c = lax.broadcasted_iota(jnp.int32, (LANES, D_BRANCH), 1)
    return jnp.where(c // HEAD_DIM == r, 1.0, 0.0).astype(F32)


def _in_proj(x2, norm_w, w_in_b, tm):
    t = x2.shape[0]

    def body(x_ref, nw_ref, w_ref, proj_ref, hn_ref):
        @pl.when(pl.program_id(1) == 0)
        def _():
            xf = x_ref[...]
            r = lax.rsqrt(jnp.mean(xf * xf, axis=1, keepdims=True) + EPS)
            hn_ref[...] = (xf * r * nw_ref[...]).astype(BF16)

        proj_ref[...] = _dot(hn_ref[...], w_ref[...])

    return pl.pallas_call(
        body, name="in_proj",
        grid=(t // tm, N_COLBLK),
        in_specs=[pl.BlockSpec((tm, D_MODEL), lambda i, j: (i, 0)),
                  pl.BlockSpec((1, D_MODEL), lambda i, j: (0, 0)),
                  pl.BlockSpec((D_MODEL, 1024), lambda i, j: (0, j))],
        out_specs=[pl.BlockSpec((tm, 1024), lambda i, j: (i, j)),
                   pl.BlockSpec((tm, D_MODEL), lambda i, j: (i, 0))],
        out_shape=[jax.ShapeDtypeStruct((t, D_IN_PAD), F32),
                   jax.ShapeDtypeStruct((t, D_MODEL), BF16)],
        compiler_params=_params(2),
    )(x2, norm_w, w_in_b)


def _qk_prep(proj, qw2, kw2, nb, seq, tq):
    nl = seq // tq
    scale = 1.0 / math.sqrt(HEAD_DIM)

    def body(q_ref, k_ref, v_ref, qw_ref, kw_ref, qs_ref, kn_ref, vb_ref, kt_ref):
        def norm(x, w):
            r = lax.rsqrt(_pair_sum(x * x) * (1.0 / HEAD_DIM) + EPS)
            return x * r * w

        qn = norm(q_ref[...], qw_ref[...]) * scale
        kn = norm(k_ref[...], kw_ref[...])
        v = v_ref[...]
        knt = kn.T.astype(BF16)
        for a in range(2):
            sl = slice(a * HEAD_DIM, (a + 1) * HEAD_DIM)
            qs_ref[0, a] = qn[:, sl].astype(BF16)
            kn_ref[0, a] = kn[:, sl].astype(BF16)
            vb_ref[0, a] = v[:, sl].astype(BF16)
            kt_ref[0, a] = knt[sl, :]

    hm = jax.ShapeDtypeStruct((nb, N_HEADS, seq, HEAD_DIM), BF16)
    hm_spec = pl.BlockSpec((1, 2, tq, HEAD_DIM), lambda b, i, h: (b, h, i, 0))
    return pl.pallas_call(
        body, name="qk_prep",
        grid=(nb, nl, N_PAIRS),
        in_specs=[pl.BlockSpec((tq, LANES), lambda b, i, h: (b * nl + i, h)),
                  pl.BlockSpec((tq, LANES), lambda b, i, h: (b * nl + i, N_PAIRS + h)),
                  pl.BlockSpec((tq, LANES), lambda b, i, h: (b * nl + i, 2 * N_PAIRS + h)),
                  pl.BlockSpec((1, LANES), lambda b, i, h: (0, 0)),
                  pl.BlockSpec((1, LANES), lambda b, i, h: (0, 0))],
        out_specs=[hm_spec, hm_spec, hm_spec,
                   pl.BlockSpec((1, 2, HEAD_DIM, tq), lambda b, i, h: (b, h, 0, i))],
        out_shape=[hm, hm, hm, jax.ShapeDtypeStruct((nb, N_HEADS, HEAD_DIM, seq), BF16)],
        compiler_params=_params(3),
    )(proj, proj, proj, qw2, kw2)


def _attn_fwd(qs, kn, vb, nb, seq, blk):
    nq = seq // blk

    def body(q_ref, k_ref, v_ref, o_ref, tot_ref):
        qi = pl.program_id(2)
        r_i = lax.broadcasted_iota(jnp.int32, (blk, blk), 0)
        c_i = lax.broadcasted_iota(jnp.int32, (blk, blk), 1)
        csum = jnp.where(r_i >= c_i, 1.0, 0.0).astype(BF16)
        causal = c_i < r_i

        heads = range(2)

        def scores(j):
            off = pl.multiple_of(j * blk, blk)
            return [_dot(q_ref[0, a], k_ref[0, a, pl.ds(off, blk), :], _NT) for a in heads]

        def weigh(j, w):
            off = pl.multiple_of(j * blk, blk)
            return [_dot(w[a], v_ref[0, a, pl.ds(off, blk), :]) for a in heads]

        def cumsums(z, masked):
            split = []
            for a in heads:
                lk = -_softplus(z[a])
                if masked:
                    lk = jnp.where(causal, lk, 0.0)
                split.append(_split_bf16(lk))
            return [_dot(split[a][0], csum) + _dot(split[a][1], csum) for a in heads]

        def weights(z, cl, run, masked, valid=None):
            w = []
            for a in heads:
                wa = jnp.exp(z[a] + cl[a] + run[a])
                if masked:
                    wa = jnp.where(causal, wa, 0.0)
                if valid is not None:
                    wa = jnp.where(valid, wa, 0.0)
                w.append(wa.astype(BF16))
            step = [cl[a][:, 0:1] for a in heads]
            if valid is not None:
                step = [jnp.where(valid, s, 0.0) for s in step]
            return w, [run[a] + step[a] for a in heads]

        def weigh_pair(p, w_far, w_near):
            far = weigh(jnp.minimum(qi + 1 - 2 * p, nq - 1), w_far)
            near = weigh(jnp.maximum(qi - 2 * p, 0), w_near)
            return [far[a] + near[a] for a in heads]

        def scores_pair(p):
            return (scores(jnp.maximum(qi - 1 - 2 * p, 0)), scores(jnp.maximum(qi - 2 - 2 * p, 0)))

        z = scores(qi)
        z_pair = scores_pair(0)
        w_diag, run = weights(z, cumsums(z, True), [jnp.zeros((blk, 1), F32)] * 2, True)
        w_none = [jnp.zeros((blk, blk), BF16)] * 2
        acc = [jnp.zeros((blk, HEAD_DIM), F32)] * 2

        def step(p, carry):
            (z1, z2), w_far, w_near, run, acc = carry
            pv = weigh_pair(p, w_far, w_near)
            z_pair = scores_pair(p + 1)
            cl1 = cumsums(z1, False)
            cl2 = cumsums(z2, False)
            w1, run = weights(z1, cl1, run, False)
            w2, run = weights(z2, cl2, run, False, valid=qi - 2 - 2 * p >= 0)
            return z_pair, w1, w2, run, [acc[a] + pv[a] for a in heads]

        n_steps = (qi + 1) // 2
        _, w_far, w_near, run, acc = lax.fori_loop(
            0, n_steps, step, (z_pair, w_none, w_diag, run, acc))
        pv = weigh_pair(n_steps, w_far, w_near)
        for a in heads:
            o_ref[:, a * HEAD_DIM:(a + 1) * HEAD_DIM] = acc[a] + pv[a]
            tot_ref[0, a, 0] = jnp.broadcast_to(run[a], (blk, LANES)).T[0:8, :]

    return pl.pallas_call(
        body, name="sb_attn_fwd",
        grid=(nb, N_PAIRS, nq),
        in_specs=[pl.BlockSpec((1, 2, blk, HEAD_DIM), lambda b, h, i: (b, h, i, 0)),
                  pl.BlockSpec((1, 2, seq, HEAD_DIM), lambda b, h, i: (b, h, 0, 0)),
                  pl.BlockSpec((1, 2, seq, HEAD_DIM), lambda b, h, i: (b, h, 0, 0))],
        out_specs=[pl.BlockSpec((blk, LANES), lambda b, h, i: (b * nq + i, h)),
                   pl.BlockSpec((1, 2, 1, 8, blk), lambda b, h, i: (b, h, i, 0, 0))],
        out_shape=[jax.ShapeDtypeStruct((nb * seq, D_BRANCH), F32),
                   jax.ShapeDtypeStruct((nb, N_HEADS, nq, 8, blk), F32)],
        compiler_params=_params(3),
    )(qs, kn, vb)


def _attn_bwd(qs, kn, kt, vb, tot, d_o, nb, seq, blk):
    nq = seq // blk

    def body(q_ref, k_ref, kt_ref, v_ref, tot_ref, do_ref, dq_ref, dk_ref, dv_ref):
        qi = pl.program_id(2)

        @pl.when(qi == 0)
        def _():
            dk_ref[...] = jnp.zeros_like(dk_ref)
            dv_ref[...] = jnp.zeros_like(dv_ref)

        r_i = lax.broadcasted_iota(jnp.int32, (blk, blk), 0)
        c_i = lax.broadcasted_iota(jnp.int32, (blk, blk), 1)
        before = jnp.where(c_i < r_i, 1.0, 0.0).astype(BF16)
        upto = jnp.where(c_i <= r_i, 1.0, 0.0).astype(BF16)
        causal = r_i < c_i

        heads = range(2)
        d_ob = [do_ref[:, a * HEAD_DIM:(a + 1) * HEAD_DIM].astype(BF16) for a in heads]
        total = [tot_ref[0, a, 0][0:1, :] for a in heads]

        def scores(j):
            off = pl.multiple_of(j * blk, blk)
            return ([_dot(k_ref[0, a, pl.ds(off, blk), :], q_ref[0, a], _NT) for a in heads],
                    [_dot(v_ref[0, a, pl.ds(off, blk), :], d_ob[a], _NT) for a in heads])

        def scatter(j, dzb, wtb, dqt):
            off = pl.multiple_of(j * blk, blk)
            for a in heads:
                dk_ref[0, a, pl.ds(off, blk), :] += _dot(dzb[a], q_ref[0, a])
                dv_ref[0, a, pl.ds(off, blk), :] += _dot(wtb[a], d_ob[a])
            return [dqt[a] + _dot(kt_ref[0, a, :, pl.ds(off, blk)], dzb[a]) for a in heads]

        def tile(zt, dwt, lsum, esum, masked):
            sp, lk, lpre = [], [], []
            for a in heads:
                sp.append(_softplus(zt[a]))
                lk.append(jnp.where(causal, -sp[a], 0.0) if masked else -sp[a])
            for a in heads:
                hi, lo = _split_bf16(lk[a])
                lpre.append(_dot(before, hi) + _dot(before, lo))
            wt, et, epre = [], [], []
            for a in heads:
                wa = jnp.exp(zt[a] + (total[a] - lsum[a] - lpre[a]))
                wt.append(jnp.where(causal, wa, 0.0) if masked else wa)
                et.append(dwt[a] * wt[a])
            for a in heads:
                e_hi, e_lo = _split_bf16(et[a])
                epre.append(_dot(upto, e_hi) + _dot(upto, e_lo))
            dzb = []
            for a in heads:
                dz = et[a] - jnp.exp(zt[a] - sp[a]) * (esum[a] + epre[a])
                dzb.append((jnp.where(causal, dz, 0.0) if masked else dz).astype(BF16))
            return (dzb, [w.astype(BF16) for w in wt],
                    [lsum[a] + lpre[a][blk - 1:blk, :] + lk[a][blk - 1:blk, :] for a in heads],
                    [esum[a] + epre[a][blk - 1:blk, :] for a in heads])

        row = [jnp.zeros((1, blk), F32)] * 2
        none = [jnp.zeros((blk, blk), BF16)] * 2
        dqt = [jnp.zeros((HEAD_DIM, blk), F32)] * 2

        def step(j, carry):
            zt, dwt, dzb, wtb, lsum, esum, dqt = carry
            dqt = scatter(jnp.maximum(j - 1, 0), dzb, wtb, dqt)
            zt_next, dwt_next = scores(j + 1)
            dzb, wtb, lsum, esum = tile(zt, dwt, lsum, esum, False)
            return zt_next, dwt_next, dzb, wtb, lsum, esum, dqt

        zt, dwt, dzb, wtb, lsum, esum, dqt = lax.fori_loop(
            0, qi, step, scores(0) + (none, none, row, row, dqt))
        dqt = scatter(jnp.maximum(qi - 1, 0), dzb, wtb, dqt)
        dzb, wtb, _, _ = tile(zt, dwt, lsum, esum, True)
        dqt = scatter(qi, dzb, wtb, dqt)
        for a in heads:
            dq_ref[:, a * HEAD_DIM:(a + 1) * HEAD_DIM] = dqt[a].T

    hm_acc = pl.BlockSpec((1, 2, seq, HEAD_DIM), lambda b, h, i: (b, h, 0, 0))
    tok = pl.BlockSpec((blk, LANES), lambda b, h, i: (b * nq + i, h))
    hm_shape = jax.ShapeDtypeStruct((nb, N_HEADS, seq, HEAD_DIM), F32)
    return pl.pallas_call(
        body, name="sb_attn_bwd",
        grid=(nb, N_PAIRS, nq),
        in_specs=[pl.BlockSpec((1, 2, blk, HEAD_DIM), lambda b, h, i: (b, h, i, 0)),
                  hm_acc,
                  pl.BlockSpec((1, 2, HEAD_DIM, seq), lambda b, h, i: (b, h, 0, 0)),
                  hm_acc,
                  pl.BlockSpec((1, 2, 1, 8, blk), lambda b, h, i: (b, h, i, 0, 0)),
                  tok],
        out_specs=[tok, hm_acc, hm_acc],
        out_shape=[jax.ShapeDtypeStruct((nb * seq, D_BRANCH), F32), hm_shape, hm_shape],
        compiler_params=_params(3),
    )(qs, kn, kt, vb, tot, d_o)


def _qk_bwd(proj, dqs, dkn, dvh, qw2, kw2, nb, seq, tq):
    nl = seq // tq
    scale = 1.0 / math.sqrt(HEAD_DIM)

    def body(q_ref, k_ref, dq_ref, dk_ref, dv_ref, qw_ref, kw_ref,
             dqr_ref, dkr_ref, dvr_ref, gq_ref, gk_ref):
        @pl.when((pl.program_id(0) == 0) & (pl.program_id(1) == 0) & (pl.program_id(2) == 0))
        def _():
            gq_ref[...] = jnp.zeros_like(gq_ref)
            gk_ref[...] = jnp.zeros_like(gk_ref)

        def norm_bwd(x, w, dy):
            r = lax.rsqrt(_pair_sum(x * x) * (1.0 / HEAD_DIM) + EPS)
            xhat = x * r
            g = dy * w
            m = _pair_sum(g * xhat) * (1.0 / HEAD_DIM)
            return r * (g - xhat * m), jnp.sum(dy * xhat, axis=0, keepdims=True)

        dqr, gq = norm_bwd(q_ref[...], qw_ref[...], dq_ref[...] * scale)
        dk2 = jnp.concatenate([dk_ref[0, 0], dk_ref[0, 1]], axis=1)
        dkr, gk = norm_bwd(k_ref[...], kw_ref[...], dk2)
        dqr_ref[...] = dqr.astype(BF16)
        dkr_ref[...] = dkr.astype(BF16)
        dvr_ref[...] = jnp.concatenate([dv_ref[0, 0], dv_ref[0, 1]], axis=1).astype(BF16)
        gq_ref[...] += gq
        gk_ref[...] += gk

    tok = lambda off: pl.BlockSpec((tq, LANES), lambda b, i, h: (b * nl + i, off + h))
    hm = pl.BlockSpec((1, 2, tq, HEAD_DIM), lambda b, i, h: (b, h, i, 0))
    vec = pl.BlockSpec((1, LANES), lambda b, i, h: (0, 0))
    tshape = jax.ShapeDtypeStruct((nb * seq, D_BRANCH), BF16)
    return pl.pallas_call(
        body, name="qk_bwd",
        grid=(nb, nl, N_PAIRS),
        in_specs=[tok(0), tok(N_PAIRS), tok(0), hm, hm, vec, vec],
        out_specs=[tok(0), tok(0), tok(0), vec, vec],
        out_shape=[tshape, tshape, tshape,
                   jax.ShapeDtypeStruct((1, LANES), F32), jax.ShapeDtypeStruct((1, LANES), F32)],
        compiler_params=_params(3),
    )(proj, proj, dqs, dkn, dvh, qw2, kw2)


def _shift_down(cur, prev, k):
    if k == 0:
        return cur
    rows = _row_iota(cur.shape)
    return jnp.where(rows < k, pltpu.roll(prev, k, axis=0), pltpu.roll(cur, k, axis=0))


def _shift_up(cur, nxt, k):
    if k == 0:
        return cur
    n = cur.shape[0]
    rows = _row_iota(cur.shape)
    return jnp.where(rows < n - k, pltpu.roll(cur, n - k, axis=0), pltpu.roll(nxt, n - k, axis=0))


def _conv_pre(cur, prev, w, b):
    out = b
    for i in range(CONV_TAPS):
        out = out + _shift_down(cur, prev, CONV_TAPS - 1 - i) * w[i:i + 1, :]
    return out


def _silu(x):
    return x * _sigmoid(x)


def _silu_grad(x):
    s = _sigmoid(x)
    return s * (1.0 + x * (1.0 - s))


def _chunk_decay(dt_raw, dtb, alog, expand, qc):
    dt = _softplus(dt_raw + dtb)
    d_a = dt * (-jnp.exp(alog))
    r_i = lax.broadcasted_iota(jnp.int32, (qc, qc), 0)
    c_i = lax.broadcasted_iota(jnp.int32, (qc, qc), 1)
    tril = r_i >= c_i
    a_cs = _dot(jnp.where(tril, 1.0, 0.0).astype(F32), d_a, precision=HIGHEST)
    dt_x = _dot(dt, expand, precision=HIGHEST)
    acs_x = _dot(a_cs, expand, precision=HIGHEST)
    return dt, d_a, a_cs, dt_x, acs_x, tril


def _ssd_fwd(proj, conv_w, conv_b, dtb, alog, dskip, nb, seq):
    qc = SSD_CHUNK
    nc = seq // qc

    def body(xs_ref, bc_ref, dt_ref, cw_ref, cb_ref, dtb_ref, al_ref, ds_ref,
             y_ref, st_ref, pxs_ref, pbc_ref, state_ref):
        @pl.when(pl.program_id(1) == 0)
        def _():
            pxs_ref[...] = jnp.zeros_like(pxs_ref)
            pbc_ref[...] = jnp.zeros_like(pbc_ref)
            state_ref[...] = jnp.zeros_like(state_ref)

        expand = _head_expand()
        xs_raw = xs_ref[...]
        bc_raw = bc_ref[...]
        cw = cw_ref[...]
        cb = cb_ref[...]
        xs = _silu(_conv_pre(xs_raw, pxs_ref[...], cw[:, :D_BRANCH], cb[:, :D_BRANCH]))
        bc = _silu(_conv_pre(bc_raw, pbc_ref[...], cw[:, D_BRANCH:], cb[:, D_BRANCH:]))
        pxs_ref[...] = xs_raw
        pbc_ref[...] = bc_raw

        dt, d_a, a_cs, dt_x, acs_x, tril = _chunk_decay(
            dt_ref[...], dtb_ref[...], al_ref[...], expand, qc)
        a_cst = a_cs.T
        aend_x = acs_x[qc - 1:qc, :]
        ea_x = jnp.exp(acs_x)
        dec_x = jnp.exp(aend_x - acs_x)
        xt = xs * dt_x
        xtb = xt.astype(BF16)
        xdb = (xt * dec_x).astype(BF16)
        d_x = _dot(jnp.broadcast_to(ds_ref[...], (8, LANES)), expand, precision=HIGHEST)[0:1, :]
        st_ref[0, 0] = state_ref[...]

        for g in range(N_GROUPS):
            gs = slice(g * GROUP_W, (g + 1) * GROUP_W)
            bg = bc[:, g * D_STATE:(g + 1) * D_STATE]
            cg = bc[:, (N_GROUPS + g) * D_STATE:(N_GROUPS + g + 1) * D_STATE]
            bgb = bg.astype(BF16)
            cgb = cg.astype(BF16)
            cbm = _dot(cgb, bgb, _NT)
            st_in = state_ref[g]
            y_off = _dot(cgb, st_in.astype(BF16)) * ea_x[:, gs]
            for k in range(HEADS_PER_GROUP):
                h = g * HEADS_PER_GROUP + k
                hs = slice(h * HEAD_DIM, (h + 1) * HEAD_DIM)
                seg = a_cs[:, h:h + 1] - a_cst[h:h + 1, :]
                gh = cbm * jnp.exp(jnp.where(tril, seg, -1e30))
                y_h = _dot(gh.astype(BF16), xtb[:, hs]) + y_off[:, k * HEAD_DIM:(k + 1) * HEAD_DIM]
                y_ref[:, hs] = y_h + d_x[:, hs] * xs[:, hs]
            state_ref[g] = st_in * jnp.exp(aend_x[:, gs]) + _dot(bg.T.astype(BF16), xdb[:, gs])

    nblk = lambda w, off: pl.BlockSpec((qc, w), lambda b, c: (b * nc + c, off))
    full = lambda r, w: pl.BlockSpec((r, w), lambda b, c: (0, 0))
    return pl.pallas_call(
        body, name="ssd_fwd",
        grid=(nb, nc),
        in_specs=[nblk(D_BRANCH, COL_XS // D_BRANCH), nblk(D_BC, COL_BC // D_BC),
                  nblk(LANES, COL_DT // LANES),
                  full(CONV_TAPS, D_CONV), full(1, D_CONV), full(1, LANES), full(1, LANES),
                  full(1, LANES)],
        out_specs=[pl.BlockSpec((qc, D_BRANCH), lambda b, c: (b * nc + c, 0)),
                   pl.BlockSpec((1, 1, N_GROUPS, D_STATE, GROUP_W), lambda b, c: (b, c, 0, 0, 0))],
        out_shape=[jax.ShapeDtypeStruct((nb * seq, D_BRANCH), F32),
                   jax.ShapeDtypeStruct((nb, nc, N_GROUPS, D_STATE, GROUP_W), F32)],
        scratch_shapes=[pltpu.VMEM((qc, D_BRANCH), F32), pltpu.VMEM((qc, D_BC), F32),
                        pltpu.VMEM((N_GROUPS, D_STATE, GROUP_W), F32)],
        compiler_params=_params(2),
    )(proj, proj, proj, conv_w, conv_b, dtb, alog, dskip)


def _ssd_bwd(proj, d_y, states, conv_w, conv_b, dtb, alog, dskip, nb, seq):
    qc = SSD_CHUNK
    nc = seq // qc

    def body(xs_ref, bc_ref, dt_ref, pxs_ref, pbc_ref, dy_ref, st_ref, stn_ref,
             cw_ref, cb_ref, dtb_ref, al_ref, ds_ref,
             dx_ref, gcw_ref, gcb_ref, gdtb_ref, gal_ref, gds_ref,
             dst_ref, nxs_ref, nbc_ref, yd_ref, dxt_ref):
        step = pl.program_id(1)
        chunk = nc - 1 - step

        @pl.when(step == 0)
        def _():
            dst_ref[...] = jnp.zeros_like(dst_ref)
            nxs_ref[...] = jnp.zeros_like(nxs_ref)
            nbc_ref[...] = jnp.zeros_like(nbc_ref)

        @pl.when((pl.program_id(0) == 0) & (step == 0))
        def _():
            gcw_ref[...] = jnp.zeros_like(gcw_ref)
            gcb_ref[...] = jnp.zeros_like(gcb_ref)
            gdtb_ref[...] = jnp.zeros_like(gdtb_ref)
            gal_ref[...] = jnp.zeros_like(gal_ref)
            gds_ref[...] = jnp.zeros_like(gds_ref)

        expand = _head_expand()
        collapse = lambda v: _dot(v, expand, _NT, precision=HIGHEST)
        first = jnp.where(chunk == 0, 0.0, 1.0)
        xs_raw = xs_ref[...]
        bc_raw = bc_ref[...]
        pxs = pxs_ref[...] * first
        pbc = pbc_ref[...] * first
        cw = cw_ref[...]
        cb = cb_ref[...]
        pre_xs = _conv_pre(xs_raw, pxs, cw[:, :D_BRANCH], cb[:, :D_BRANCH])
        pre_bc = _conv_pre(bc_raw, pbc, cw[:, D_BRANCH:], cb[:, D_BRANCH:])
        xs = _silu(pre_xs)
        bc = _silu(pre_bc)

        dt_in = dt_ref[...] + dtb_ref[...]
        dt, d_a, a_cs, dt_x, acs_x, tril = _chunk_decay(
            dt_ref[...], dtb_ref[...], al_ref[...], expand, qc)
        a_cst = a_cs.T
        aend_x = acs_x[qc - 1:qc, :]
        ea_x = jnp.exp(acs_x)
        dec_x = jnp.exp(aend_x - acs_x)
        xt = xs * dt_x
        xtb = xt.astype(BF16)
        xdb = (xt * dec_x).astype(BF16)
        d_x = _dot(jnp.broadcast_to(ds_ref[...], (8, LANES)), expand, precision=HIGHEST)[0:1, :]

        dy = dy_ref[...]
        dyb = dy.astype(BF16)
        dyeab = (dy * ea_x).astype(BF16)
        gds_ref[...] += collapse(jnp.broadcast_to(jnp.sum(dy * xs, axis=0, keepdims=True),
                                                  (8, D_BRANCH)))[0:1, :]

        d_bc = []
        d_cc = []
        y_offs = []
        dxt_states = []
        end_terms = []
        for g in range(N_GROUPS):
            gs = slice(g * GROUP_W, (g + 1) * GROUP_W)
            bg = bc[:, g * D_STATE:(g + 1) * D_STATE]
            cg = bc[:, (N_GROUPS + g) * D_STATE:(N_GROUPS + g + 1) * D_STATE]
            bgb = bg.astype(BF16)
            cgb = cg.astype(BF16)
            cbm = _dot(cgb, bgb, _NT)
            st_in = st_ref[0, 0, g]
            st_inb = st_in.astype(BF16)
            d_st = dst_ref[g]
            d_stb = d_st.astype(BF16)
            y_offs.append(_dot(cgb, st_inb) * ea_x[:, gs])
            dxt_states.append(_dot(bgb, d_stb) * dec_x[:, gs])
            d_c = _dot(dyeab[:, gs], st_inb, _NT)
            d_b = _dot(xdb[:, gs], d_stb, _NT)
            d_cb = jnp.zeros((qc, qc), F32)
            for k in range(HEADS_PER_GROUP):
                h = g * HEADS_PER_GROUP + k
                hs = slice(h * HEAD_DIM, (h + 1) * HEAD_DIM)
                seg = a_cs[:, h:h + 1] - a_cst[h:h + 1, :]
                lh = jnp.exp(jnp.where(tril, seg, -1e30))
                ghb = (cbm * lh).astype(BF16)
                d_cb = d_cb + _dot(dyb[:, hs], xtb[:, hs], _NT) * lh
                yd_ref[:, hs] = _dot(ghb, xtb[:, hs])
                dxt_ref[:, hs] = _dot(ghb, dyb[:, hs], _TN)
            d_cbb = d_cb.astype(BF16)
            d_cc.append(d_c + _dot(d_cbb, bgb))
            d_bc.append(d_b + _dot(d_cbb, cgb, _TN))
            end_terms.append(jnp.sum(d_st * stn_ref[0, 0, g], axis=0, keepdims=True))
            dst_ref[g] = d_st * jnp.exp(aend_x[:, gs]) + _dot(cg.T.astype(BF16), dyeab[:, gs])

        y_off = jnp.concatenate(y_offs, axis=1)
        dxt_state = jnp.concatenate(dxt_states, axis=1)
        dxt = dxt_ref[...] + dxt_state
        last = jnp.where(chunk == nc - 1, 0.0, 1.0)
        end_c = collapse(jnp.broadcast_to(jnp.concatenate(end_terms, axis=1), (8, D_BRANCH)))[0:1, :]
        da_cs = collapse(dyb.astype(F32) * yd_ref[...] - dxt_ref[...] * xtb.astype(F32)
                         + dy * y_off - dxt_state * xt)
        da_cs = da_cs + jnp.where(_row_iota(da_cs.shape) == qc - 1, end_c * last, 0.0)
        triu = lax.broadcasted_iota(jnp.int32, (qc, qc), 0) <= lax.broadcasted_iota(jnp.int32, (qc, qc), 1)
        dd_a = _dot(jnp.where(triu, 1.0, 0.0).astype(F32), da_cs, precision=HIGHEST)
        ddt = dd_a * (-jnp.exp(al_ref[...])) + collapse(dxt * xs)
        head_lanes = _lane_iota(ddt.shape) < N_HEADS
        ddt_raw = jnp.where(head_lanes, ddt * _sigmoid(dt_in), 0.0)
        gal_ref[...] += jnp.sum(jnp.where(head_lanes, dd_a * d_a, 0.0), axis=0, keepdims=True)
        gdtb_ref[...] += jnp.sum(ddt_raw, axis=0, keepdims=True)

        dpre_xs = (dxt * dt_x + d_x * dy) * _silu_grad(pre_xs)
        dpre_bc = jnp.concatenate(d_bc + d_cc, axis=1) * _silu_grad(pre_bc)
        gcb_ref[...] += jnp.concatenate([jnp.sum(dpre_xs, axis=0, keepdims=True),
                                         jnp.sum(dpre_bc, axis=0, keepdims=True)], axis=1)
        nxs = nxs_ref[...]
        nbc = nbc_ref[...]
        du_xs = jnp.zeros_like(dpre_xs)
        du_bc = jnp.zeros_like(dpre_bc)
        for i in range(CONV_TAPS):
            k = CONV_TAPS - 1 - i
            gcw_ref[i:i + 1, :] += jnp.concatenate(
                [jnp.sum(dpre_xs * _shift_down(xs_raw, pxs, k), axis=0, keepdims=True),
                 jnp.sum(dpre_bc * _shift_down(bc_raw, pbc, k), axis=0, keepdims=True)], axis=1)
            du_xs = du_xs + _shift_up(dpre_xs, nxs, k) * cw[i:i + 1, :D_BRANCH]
            du_bc = du_bc + _shift_up(dpre_bc, nbc, k) * cw[i:i + 1, D_BRANCH:]
        nxs_ref[...] = dpre_xs
        nbc_ref[...] = dpre_bc

        dx_ref[:, :D_BRANCH] = du_xs.astype(BF16)
        dx_ref[:, D_BRANCH:D_CONV] = du_bc.astype(BF16)
        dx_ref[:, D_CONV:D_CONV + LANES] = ddt_raw.astype(BF16)
        dx_ref[:, D_CONV + LANES:] = jnp.zeros((qc, 2048 - D_CONV - LANES), BF16)

    rev = lambda b, c: b * nc + (nc - 1 - c)
    prv = lambda b, c: b * nc + jnp.maximum(nc - 2 - c, 0)
    nblk = lambda w, off, f: pl.BlockSpec((qc, w), lambda b, c: (f(b, c), off))
    full = lambda r, w: pl.BlockSpec((r, w), lambda b, c: (0, 0))
    st_spec = lambda f: pl.BlockSpec((1, 1, N_GROUPS, D_STATE, GROUP_W),
                                     lambda b, c: (b, f(c), 0, 0, 0))
    return pl.pallas_call(
        body, name="ssd_bwd",
        grid=(nb, nc),
        in_specs=[nblk(D_BRANCH, COL_XS // D_BRANCH, rev), nblk(D_BC, COL_BC // D_BC, rev),
                  nblk(LANES, COL_DT // LANES, rev),
                  nblk(D_BRANCH, COL_XS // D_BRANCH, prv), nblk(D_BC, COL_BC // D_BC, prv),
                  nblk(D_BRANCH, 0, rev),
                  st_spec(lambda c: nc - 1 - c), st_spec(lambda c: jnp.minimum(nc - c, nc - 1)),
                  full(CONV_TAPS, D_CONV), full(1, D_CONV), full(1, LANES), full(1, LANES),
                  full(1, LANES)],
        out_specs=[nblk(2048, 0, rev), full(8, D_CONV), full(1, D_CONV), full(1, LANES),
                   full(1, LANES), full(1, LANES)],
        out_shape=[jax.ShapeDtypeStruct((nb * seq, 2048), BF16),
                   jax.ShapeDtypeStruct((8, D_CONV), F32), jax.ShapeDtypeStruct((1, D_CONV), F32),
                   jax.ShapeDtypeStruct((1, LANES), F32), jax.ShapeDtypeStruct((1, LANES), F32),
                   jax.ShapeDtypeStruct((1, LANES), F32)],
        scratch_shapes=[pltpu.VMEM((N_GROUPS, D_STATE, GROUP_W), F32),
                        pltpu.VMEM((qc, D_BRANCH), F32), pltpu.VMEM((qc, D_BC), F32),
                        pltpu.VMEM((qc, D_BRANCH), F32), pltpu.VMEM((qc, D_BRANCH), F32)],
        compiler_params=_params(2),
    )(proj, proj, proj, proj, proj, d_y, states, states, conv_w, conv_b, dtb, alog, dskip)


def _mid(o_sb, y_ssd, proj, x2, target, sb_w, ssd_w, w_out_b, tm):
    t = x2.shape[0]
    inv_d = 1.0 / D_MODEL

    def body(o_ref, y_ref, zsb_ref, zssd_ref, x_ref, tg_ref, sbw_ref, ssdw_ref, w_ref,
             dout_ref, dosb_ref, dy_ref, dz_ref, gw_ref, gsb_ref, gssd_ref, loss_ref):
        @pl.when(pl.program_id(0) == 0)
        def _():
            gw_ref[...] = jnp.zeros_like(gw_ref)
            gsb_ref[...] = jnp.zeros_like(gsb_ref)
            gssd_ref[...] = jnp.zeros_like(gssd_ref)
            loss_ref[...] = jnp.zeros_like(loss_ref)

        def branch(val, z, w):
            gate = _silu(z)
            g = val * gate
            r = lax.rsqrt(jnp.mean(g * g, axis=1, keepdims=True) + EPS)
            xhat = g * r
            return gate, r, xhat, (xhat * w).astype(BF16)

        o = o_ref[...]
        y = y_ref[...]
        z_sb = zsb_ref[...]
        z_ssd = zssd_ref[...]
        gate_a, r_a, xhat_a, mix_a = branch(o, z_sb, sbw_ref[...])
        gate_b, r_b, xhat_b, mix_b = branch(y, z_ssd, ssdw_ref[...])
        out = x_ref[...] + _dot(mix_a, w_ref[:D_BRANCH, :]) + _dot(mix_b, w_ref[D_BRANCH:, :])
        diff = out - tg_ref[...]
        loss_ref[...] += 0.5 * inv_d * jnp.sum(diff * diff)
        d_out = diff * inv_d
        dout_ref[...] = d_out
        d_outb = d_out.astype(BF16)
        gw_ref[:D_BRANCH, :] += _dot(mix_a, d_outb, _TN)
        gw_ref[D_BRANCH:, :] += _dot(mix_b, d_outb, _TN)

        def branch_bwd(dmix, val, z, w, gate, r, xhat):
            gg = dmix * w
            m = jnp.mean(gg * xhat, axis=1, keepdims=True)
            dg = r * (gg - xhat * m)
            return dg * gate, dg * val * _silu_grad(z), jnp.sum(dmix * xhat, axis=0, keepdims=True)

        dmix_a = _dot(d_outb, w_ref[:D_BRANCH, :], _NT)
        dmix_b = _dot(d_outb, w_ref[D_BRANCH:, :], _NT)
        d_o, dz_a, gsb = branch_bwd(dmix_a, o, z_sb, sbw_ref[...], gate_a, r_a, xhat_a)
        d_y, dz_b, gssd = branch_bwd(dmix_b, y, z_ssd, ssdw_ref[...], gate_b, r_b, xhat_b)
        dosb_ref[...] = d_o
        dy_ref[...] = d_y
        dz_ref[:, :D_BRANCH] = dz_a.astype(BF16)
        dz_ref[:, D_BRANCH:] = dz_b.astype(BF16)
        gsb_ref[...] += gsb
        gssd_ref[...] += gssd

    row = lambda w, off: pl.BlockSpec((tm, w), lambda i: (i, off))
    full = lambda r, w: pl.BlockSpec((r, w), lambda i: (0, 0))
    tok = jax.ShapeDtypeStruct((t, D_MODEL), F32)
    return pl.pallas_call(
        body, name="mid",
        grid=(t // tm,),
        in_specs=[row(D_BRANCH, 0), row(D_BRANCH, 0), row(D_BRANCH, 3), row(D_BRANCH, 4),
                  row(D_MODEL, 0), row(D_MODEL, 0), full(1, D_BRANCH), full(1, D_BRANCH),
                  full(2 * D_BRANCH, D_MODEL)],
        out_specs=[row(D_MODEL, 0), row(D_BRANCH, 0), row(D_BRANCH, 0), row(2 * D_BRANCH, 0),
                   full(2 * D_BRANCH, D_MODEL), full(1, D_BRANCH), full(1, D_BRANCH),
                   full(1, LANES)],
        out_shape=[tok, tok, tok, jax.ShapeDtypeStruct((t, 2 * D_BRANCH), BF16),
                   jax.ShapeDtypeStruct((2 * D_BRANCH, D_MODEL), F32),
                   jax.ShapeDtypeStruct((1, D_BRANCH), F32), jax.ShapeDtypeStruct((1, D_BRANCH), F32),
                   jax.ShapeDtypeStruct((1, LANES), F32)],
        compiler_params=_params(1),
    )(o_sb, y_ssd, proj, proj, x2, target, sb_w, ssd_w, w_out_b)


def _in_proj_bwd_x(d_proj, w_in_b, x2, d_out, norm_w, tm):
    t = x2.shape[0]

    def body(dp_ref, w_ref, x_ref, dout_ref, nw_ref, gx_ref, gnw_ref, acc_ref):
        j = pl.program_id(1)

        @pl.when((pl.program_id(0) == 0) & (j == 0))
        def _():
            gnw_ref[...] = jnp.zeros_like(gnw_ref)

        @pl.when(j == 0)
        def _():
            acc_ref[...] = jnp.zeros_like(acc_ref)

        acc_ref[...] += _dot(dp_ref[...], w_ref[...], _NT)

        @pl.when(j == N_COLBLK - 1)
        def _():
            xf = x_ref[...]
            d_hn = acc_ref[...]
            r = lax.rsqrt(jnp.mean(xf * xf, axis=1, keepdims=True) + EPS)
            xhat = xf * r
            g = d_hn * nw_ref[...]
            m = jnp.mean(g * xhat, axis=1, keepdims=True)
            gx_ref[...] = dout_ref[...] + r * (g - xhat * m)
            gnw_ref[...] += jnp.sum(d_hn * xhat, axis=0, keepdims=True)

    return pl.pallas_call(
        body, name="in_proj_bwd_x",
        grid=(t // tm, N_COLBLK),
        in_specs=[pl.BlockSpec((tm, 1024), lambda i, j: (i, j)),
                  pl.BlockSpec((D_MODEL, 1024), lambda i, j: (0, j)),
                  pl.BlockSpec((tm, D_MODEL), lambda i, j: (i, 0)),
                  pl.BlockSpec((tm, D_MODEL), lambda i, j: (i, 0)),
                  pl.BlockSpec((1, D_MODEL), lambda i, j: (0, 0))],
        out_specs=[pl.BlockSpec((tm, D_MODEL), lambda i, j: (i, 0)),
                   pl.BlockSpec((1, D_MODEL), lambda i, j: (0, 0))],
        out_shape=[jax.ShapeDtypeStruct((t, D_MODEL), F32), jax.ShapeDtypeStruct((1, D_MODEL), F32)],
        scratch_shapes=[pltpu.VMEM((tm, D_MODEL), F32)],
        compiler_params=_params(2),
    )(d_proj, w_in_b, x2, d_out, norm_w)


def _in_proj_bwd_w(hn, d_proj, tm):
    t = hn.shape[0]

    def body(hn_ref, dp_ref, gw_ref):
        @pl.when(pl.program_id(1) == 0)
        def _():
            gw_ref[...] = jnp.zeros_like(gw_ref)

        gw_ref[...] += _dot(hn_ref[...], dp_ref[...], _TN)

    return pl.pallas_call(
        body, name="in_proj_bwd_w",
        grid=(N_COLBLK, t // tm),
        in_specs=[pl.BlockSpec((tm, D_MODEL), lambda j, i: (i, 0)),
                  pl.BlockSpec((tm, 1024), lambda j, i: (i, j))],
        out_specs=pl.BlockSpec((D_MODEL, 1024), lambda j, i: (0, j)),
        out_shape=jax.ShapeDtypeStruct((D_MODEL, D_IN_PAD), F32),
        compiler_params=_params(2),
    )(hn, d_proj)


def _adamw(parts, w, m, v, tr, name):
    rows, cols = w.shape
    c1 = 1.0 - ADAM_B1 ** ADAM_STEP
    c2 = 1.0 - ADAM_B2 ** ADAM_STEP

    def body(p_ref, w_ref, m_ref, v_ref, g_ref, d_ref, nm_ref, nv_ref):
        g = p_ref[0]
        for s in range(1, N_DEV):
            g = g + p_ref[s]
        nm = ADAM_B1 * m_ref[...] + (1.0 - ADAM_B1) * g
        nv = ADAM_B2 * v_ref[...] + (1.0 - ADAM_B2) * (g * g)
        g_ref[...] = g
        nm_ref[...] = nm
        nv_ref[...] = nv
        d_ref[...] = -ADAM_LR * ((nm / c1) / (jnp.sqrt(nv / c2) + ADAM_EPS) + ADAM_WD * w_ref[...])

    blk = pl.BlockSpec((tr, cols), lambda i: (i, 0))
    shape = jax.ShapeDtypeStruct((rows, cols), F32)
    return pl.pallas_call(
        body, name=name,
        grid=(rows // tr,),
        in_specs=[pl.BlockSpec((N_DEV, tr, cols), lambda i: (0, i, 0)), blk, blk, blk],
        out_specs=[blk, blk, blk, blk],
        out_shape=[shape, shape, shape, shape],
        compiler_params=_params(1),
    )(parts, w, m, v)


def _mesh_place():
    x, y, c = lax.axis_index("x"), lax.axis_index("y"), lax.axis_index("c")
    return x, y, c, 4 * x + 2 * y + c


def _peer(x, y, c, k):
    px = 1 - x if k & 4 else x
    py = 1 - y if k & 2 else y
    pc = 1 - c if k & 1 else c
    return (px, py, pc), 4 * px + 2 * py + pc


def _exchange(srcs, scatter, name):
    n = len(srcs)

    def body(*refs):
        src_refs, dst_refs = refs[:n], refs[n:2 * n]
        send_sems, recv_sems, loc_sems = refs[2 * n:]
        x, y, c, me = _mesh_place()

        def src_of(i, idx):
            return src_refs[i].at[idx] if scatter[i] else src_refs[i]

        local = [pltpu.make_async_copy(src_of(i, me), dst_refs[i].at[me], loc_sems.at[i])
                 for i in range(n)]
        for cp in local:
            cp.start()
        sends = []
        for k in range(1, N_DEV):
            peer, pidx = _peer(x, y, c, k)
            for i in range(n):
                s = i * (N_DEV - 1) + k - 1
                cp = pltpu.make_async_remote_copy(
                    src_ref=src_of(i, pidx), dst_ref=dst_refs[i].at[me],
                    send_sem=send_sems.at[s], recv_sem=recv_sems.at[s],
                    device_id=peer, device_id_type=pl.DeviceIdType.MESH)
                cp.start()
                sends.append(cp)
        for k in range(1, N_DEV):
            peer, pidx = _peer(x, y, c, k)
            for i in range(n):
                s = i * (N_DEV - 1) + k - 1
                pltpu.make_async_remote_copy(
                    src_ref=src_of(i, pidx), dst_ref=dst_refs[i].at[pidx],
                    send_sem=send_sems.at[s], recv_sem=recv_sems.at[s],
                    device_id=peer, device_id_type=pl.DeviceIdType.MESH).wait_recv()
        for cp in sends:
            cp.wait_send()
        for cp in local:
            cp.wait()

    out_shape = [jax.ShapeDtypeStruct(s.shape if sc else (N_DEV,) + s.shape, s.dtype)
                 for s, sc in zip(srcs, scatter)]
    any_spec = pl.BlockSpec(memory_space=pl.ANY)
    return pl.pallas_call(
        body, name=name,
        in_specs=[any_spec] * n, out_specs=[any_spec] * n, out_shape=out_shape,
        scratch_shapes=[pltpu.SemaphoreType.DMA((n * (N_DEV - 1),)),
                        pltpu.SemaphoreType.DMA((n * (N_DEV - 1),)),
                        pltpu.SemaphoreType.DMA((n,))],
    )(*srcs)


def _pad_lanes(v, width=LANES):
    return jnp.pad(v, ((0, 0), (0, width - v.shape[1])))


def _local_step(x, target, norm_w, w_in_b, q_norm_w, k_norm_w, conv_w, conv_b, dt_bias, a_log,
                d_skip, sb_norm_w, ssd_norm_w, w_out_b, tm=512, tq=512, tmid=128, blk=ATT_BLK):
    nb, seq, _ = x.shape
    t = nb * seq
    x2 = x.reshape(t, D_MODEL)
    tg2 = target.reshape(t, D_MODEL)
    qw2 = jnp.tile(q_norm_w, (1, 2))
    kw2 = jnp.tile(k_norm_w, (1, 2))
    dtb, alog, dsk = _pad_lanes(dt_bias), _pad_lanes(a_log), _pad_lanes(d_skip)

    proj, hn = _in_proj(x2, norm_w, w_in_b, tm)
    qs, kn, vb, kt = _qk_prep(proj, qw2, kw2, nb, seq, tq)
    o_sb, sb_tot = _attn_fwd(qs, kn, vb, nb, seq, blk)
    y_ssd, states = _ssd_fwd(proj, conv_w, conv_b, dtb, alog, dsk, nb, seq)
    d_out, d_osb, d_y, d_z, g_wout, g_sbw, g_ssdw, loss = _mid(
        o_sb, y_ssd, proj, x2, tg2, sb_norm_w, ssd_norm_w, w_out_b, tmid)
    dqs, dkn, dvh = _attn_bwd(qs, kn, kt, vb, sb_tot, d_osb, nb, seq, blk)
    dq_raw, dk_raw, dv_raw, g_qw, g_kw = _qk_bwd(proj, dqs, dkn, dvh, qw2, kw2, nb, seq, tq)
    d_xbc, g_cw, g_cb, g_dtb, g_alog, g_dsk = _ssd_bwd(
        proj, d_y, states, conv_w, conv_b, dtb, alog, dsk, nb, seq)
    d_proj = jnp.concatenate([dq_raw, dk_raw, dv_raw, d_z, d_xbc], axis=1)
    grad_x, g_nw = _in_proj_bwd_x(d_proj, w_in_b, x2, d_out, norm_w, tm)
    g_win = _in_proj_bwd_w(hn, d_proj, tm)

    small = dict(
        norm_w=g_nw,
        q_norm_w=g_qw[:, :HEAD_DIM] + g_qw[:, HEAD_DIM:],
        k_norm_w=g_kw[:, :HEAD_DIM] + g_kw[:, HEAD_DIM:],
        conv_b=g_cb, dt_bias=g_dtb[:, :N_HEADS], A_log=g_alog[:, :N_HEADS],
        D_skip=g_dsk[:, :N_HEADS], sb_norm_w=g_sbw, ssd_norm_w=g_ssdw)
    return (loss[0, 0], grad_x.reshape(nb, seq, D_MODEL), g_win[:, :D_IN], g_wout,
            g_cw[:CONV_TAPS], small)


_SMALL = ("norm_w", "q_norm_w", "k_norm_w", "conv_b", "dt_bias", "A_log", "D_skip",
          "sb_norm_w", "ssd_norm_w")


def _pack_small(vals):
    rows = [_pad_lanes(vals[n], -(-vals[n].shape[1] // LANES) * LANES).reshape(-1, LANES)
            for n in _SMALL]
    packed = jnp.concatenate(rows, axis=0)
    return jnp.pad(packed, ((0, 48 - packed.shape[0]), (0, 0)))


def _unpack_small(packed, like):
    out, r = {}, 0
    for n in _SMALL:
        width = like[n].shape[1]
        nr = -(-width // LANES)
        out[n] = packed[r:r + nr].reshape(1, nr * LANES)[:, :width]
        r += nr
    return out


def kernel(x, norm_w, w_in, q_norm_w, k_norm_w, conv_w, conv_b, dt_bias, A_log, D_skip, sb_norm_w, ssd_norm_w, w_out, loss_target, m_norm_w, m_w_in, m_q_norm_w, m_k_norm_w, m_conv_w, m_conv_b, m_dt_bias, m_A_log, m_D_skip, m_sb_norm_w, m_ssd_norm_w, m_w_out, v_norm_w, v_w_in, v_q_norm_w, v_k_norm_w, v_conv_w, v_conv_b, v_dt_bias, v_A_log, v_D_skip, v_sb_norm_w, v_ssd_norm_w, v_w_out):
    w_sh = D_IN // N_DEV
    c_sh = D_CONV // N_DEV

    win_all, wout_all, cw_all = _exchange(
        [w_in[0].astype(BF16), w_out[0].astype(BF16), conv_w[0]], [False, False, False],
        "gather_weights")
    w_in_b = jnp.pad(jnp.transpose(win_all, (1, 0, 2)).reshape(D_MODEL, D_IN),
                     ((0, 0), (0, D_IN_PAD - D_IN)))
    w_out_b = wout_all.reshape(2 * D_BRANCH, D_MODEL)
    conv_full = jnp.transpose(cw_all, (1, 0, 2)).reshape(CONV_TAPS, D_CONV)

    loss, grad_x, g_win, g_wout, g_cw, g_small = _local_step(
        x, loss_target, norm_w, w_in_b, q_norm_w, k_norm_w, conv_full, conv_b, dt_bias, A_log,
        D_skip, sb_norm_w, ssd_norm_w, w_out_b)

    win_parts, wout_parts, cw_parts, small_parts = _exchange(
        [jnp.transpose(g_win.reshape(D_MODEL, N_DEV, w_sh), (1, 0, 2)),
         g_wout.reshape(N_DEV, 2 * D_BRANCH // N_DEV, D_MODEL),
         jnp.pad(jnp.transpose(g_cw.reshape(CONV_TAPS, N_DEV, c_sh), (1, 0, 2)),
                 ((0, 0), (0, 8 - CONV_TAPS), (0, 0))),
         _pack_small(g_small)],
        [True, True, True, False], "scatter_grads")

    small_w = dict(norm_w=norm_w, q_norm_w=q_norm_w, k_norm_w=k_norm_w, conv_b=conv_b,
                   dt_bias=dt_bias, A_log=A_log, D_skip=D_skip, sb_norm_w=sb_norm_w,
                   ssd_norm_w=ssd_norm_w)
    small_m = dict(norm_w=m_norm_w, q_norm_w=m_q_norm_w, k_norm_w=m_k_norm_w, conv_b=m_conv_b,
                   dt_bias=m_dt_bias, A_log=m_A_log, D_skip=m_D_skip, sb_norm_w=m_sb_norm_w,
                   ssd_norm_w=m_ssd_norm_w)
    small_v = dict(norm_w=v_norm_w, q_norm_w=v_q_norm_w, k_norm_w=v_k_norm_w, conv_b=v_conv_b,
                   dt_bias=v_dt_bias, A_log=v_A_log, D_skip=v_D_skip, sb_norm_w=v_sb_norm_w,
                   ssd_norm_w=v_ssd_norm_w)

    pad8 = lambda a: jnp.pad(a, ((0, 8 - CONV_TAPS), (0, 0)))
    r_win = _adamw(win_parts, w_in[0], m_w_in[0], v_w_in[0], 128, "adamw_w_in")
    r_wout = _adamw(wout_parts, w_out[0], m_w_out[0], v_w_out[0], 128, "adamw_w_out")
    r_cw = _adamw(cw_parts, pad8(conv_w[0]), pad8(m_conv_w[0]), pad8(v_conv_w[0]), 8, "adamw_conv_w")
    r_small = _adamw(small_parts, _pack_small(small_w), _pack_small(small_m),
                     _pack_small(small_v), 48, "adamw_small")

    loss = lax.psum(loss, ("x", "y", "c"))
    res = {"w_in": [a[None] for a in r_win], "w_out": [a[None] for a in r_wout],
           "conv_w": [a[:CONV_TAPS][None] for a in r_cw]}
    unpacked = [_unpack_small(a, small_w) for a in r_small]
    for n in _SMALL:
        res[n] = [u[n] for u in unpacked]
    order = ("norm_w", "w_in", "q_norm_w", "k_norm_w", "conv_w", "conv_b", "dt_bias", "A_log",
             "D_skip", "sb_norm_w", "ssd_norm_w", "w_out")
    outs = [loss, grad_x]
    for kind in range(4):
        outs += [res[n][kind] for n in order]
    return tuple(outs)
```

```python
import functools
import math

import jax
import jax.numpy as jnp
from jax import lax
from jax.experimental import pallas as pl
from jax.experimental.pallas import tpu as pltpu

F32 = jnp.float32
BF16 = jnp.bfloat16
HIGHEST = lax.Precision.HIGHEST

D_MODEL = 1024
N_HEADS = 16
HEAD_DIM = 64
N_PAIRS = N_HEADS // 2
D_BRANCH = 1024
N_GROUPS = 2
HEADS_PER_GROUP = 8
D_STATE = 128
GROUP_W = HEADS_PER_GROUP * HEAD_DIM
D_BC = 2 * N_GROUPS * D_STATE
D_CONV = D_BRANCH + D_BC
D_IN = 6672
D_IN_PAD = 7168
N_COLBLK = D_IN_PAD // 1024
COL_XS = 5120
COL_BC = 6144
COL_DT = 6656
EPS = 1e-6
CONV_TAPS = 4
N_DEV = 8

LANES = 128
SSD_CHUNK = 128
ATT_BLK = 256
EXP_UNDERFLOW = -105.0
VMEM_LIMIT = 56 * 1024 * 1024

ADAM_LR = 0.001
ADAM_B1 = 0.9
ADAM_B2 = 0.999
ADAM_EPS = 1e-08
ADAM_WD = 0.01
ADAM_STEP = 10

_NT = (((1,), (1,)), ((), ()))
_TN = (((0,), (0,)), ((), ()))


def _params(n_grid):
    return pltpu.CompilerParams(dimension_semantics=("arbitrary",) * n_grid,
                                vmem_limit_bytes=VMEM_LIMIT)


def _dot(a, b, dims=None, precision=None):
    if dims is None:
        return jnp.dot(a, b, preferred_element_type=F32, precision=precision)
    return lax.dot_general(a, b, dims, preferred_element_type=F32, precision=precision)


def _sigmoid(x):
    return 1.0 / (1.0 + jnp.exp(-x))


def _softplus(x):
    return jnp.maximum(x, 0.0) + jnp.log(1.0 + jnp.exp(-jnp.abs(x)))


def _split_bf16(x):
    hi = x.astype(BF16)
    lo = (x - hi.astype(F32)).astype(BF16)
    return hi, lo


def _lane_iota(shape):
    return lax.broadcasted_iota(jnp.int32, shape, len(shape) - 1)


def _row_iota(shape):
    return lax.broadcasted_iota(jnp.int32, shape, len(shape) - 2)


def _pair_sum(x):
    low = _lane_iota(x.shape) < HEAD_DIM
    s0 = jnp.sum(jnp.where(low, x, 0.0), axis=1, keepdims=True)
    s1 = jnp.sum(jnp.where(low, 0.0, x), axis=1, keepdims=True)
    return jnp.where(low, s0, s1)


def _head_expand():
    r = lax.broadcasted_iota(jnp.int32, (LANES, D_BRANCH), 0)
    c = lax.broadcasted_iota(jnp.int32, (LANES, D_BRANCH), 1)
    return jnp.where(c // HEAD_DIM == r, 1.0, 0.0).astype(F32)


def _in_proj(x2, norm_w, w_in_b, tm):
    t = x2.shape[0]

    def body(x_ref, nw_ref, w_ref, proj_ref, hn_ref):
        @pl.when(pl.program_id(1) == 0)
        def _():
            xf = x_ref[...]
            r = lax.rsqrt(jnp.mean(xf * xf, axis=1, keepdims=True) + EPS)
            hn_ref[...] = (xf * r * nw_ref[...]).astype(BF16)

        proj_ref[...] = _dot(hn_ref[...], w_ref[...])

    return pl.pallas_call(
        body, name="in_proj",
        grid=(t // tm, N_COLBLK),
        in_specs=[pl.BlockSpec((tm, D_MODEL), lambda i, j: (i, 0)),
                  pl.BlockSpec((1, D_MODEL), lambda i, j: (0, 0)),
                  pl.BlockSpec((D_MODEL, 1024), lambda i, j: (0, j))],
        out_specs=[pl.BlockSpec((tm, 1024), lambda i, j: (i, j)),
                   pl.BlockSpec((tm, D_MODEL), lambda i, j: (i, 0))],
        out_shape=[jax.ShapeDtypeStruct((t, D_IN_PAD), F32),
                   jax.ShapeDtypeStruct((t, D_MODEL), BF16)],
        compiler_params=_params(2),
    )(x2, norm_w, w_in_b)


def _qk_prep(proj, qw2, kw2, nb, seq, tq):
    nl = seq // tq
    scale = 1.0 / math.sqrt(HEAD_DIM)

    def body(q_ref, k_ref, v_ref, qw_ref, kw_ref, qs_ref, kn_ref, vb_ref, kt_ref):
        def norm(x, w):
            r = lax.rsqrt(_pair_sum(x * x) * (1.0 / HEAD_DIM) + EPS)
            return x * r * w

        qn = norm(q_ref[...], qw_ref[...]) * scale
        kn = norm(k_ref[...], kw_ref[...])
        v = v_ref[...]
        knt = kn.T.astype(BF16)
        for a in range(2):
            sl = slice(a * HEAD_DIM, (a + 1) * HEAD_DIM)
            qs_ref[0, a] = qn[:, sl].astype(BF16)
            kn_ref[0, a] = kn[:, sl].astype(BF16)
            vb_ref[0, a] = v[:, sl].astype(BF16)
            kt_ref[0, a] = knt[sl, :]

    hm = jax.ShapeDtypeStruct((nb, N_HEADS, seq, HEAD_DIM), BF16)
    hm_spec = pl.BlockSpec((1, 2, tq, HEAD_DIM), lambda b, i, h: (b, h, i, 0))
    return pl.pallas_call(
        body, name="qk_prep",
        grid=(nb, nl, N_PAIRS),
        in_specs=[pl.BlockSpec((tq, LANES), lambda b, i, h: (b * nl + i, h)),
                  pl.BlockSpec((tq, LANES), lambda b, i, h: (b * nl + i, N_PAIRS + h)),
                  pl.BlockSpec((tq, LANES), lambda b, i, h: (b * nl + i, 2 * N_PAIRS + h)),
                  pl.BlockSpec((1, LANES), lambda b, i, h: (0, 0)),
                  pl.BlockSpec((1, LANES), lambda b, i, h: (0, 0))],
        out_specs=[hm_spec, hm_spec, hm_spec,
                   pl.BlockSpec((1, 2, HEAD_DIM, tq), lambda b, i, h: (b, h, 0, i))],
        out_shape=[hm, hm, hm, jax.ShapeDtypeStruct((nb, N_HEADS, HEAD_DIM, seq), BF16)],
        compiler_params=_params(3),
    )(proj, proj, proj, qw2, kw2)


def _attn_fwd(qs, kn, vb, nb, seq, blk):
    nq = seq // blk

    def body(q_ref, k_ref, v_ref, o_ref, tot_ref, low_ref, kmax_ref):
        qi = pl.program_id(2)
        r_i = lax.broadcasted_iota(jnp.int32, (blk, blk), 0)
        c_i = lax.broadcasted_iota(jnp.int32, (blk, blk), 1)
        csum = jnp.where(r_i >= c_i, 1.0, 0.0).astype(BF16)
        causal = c_i < r_i
        heads = range(2)

        @pl.when(qi == 0)
        def _():
            for a in heads:
                kk = k_ref[0, a].astype(F32)
                kmax_ref[a] = jnp.full((8, LANES), jnp.max(jnp.sum(kk * kk, axis=1, keepdims=True)))

        zmax = []
        for a in heads:
            qf = q_ref[0, a].astype(F32)
            qsq = jnp.sum(qf * qf, axis=1, keepdims=True)
            zmax.append(1.01 * jnp.sqrt(qsq * kmax_ref[a][0:1, 0:1]) + 0.01)

        def exhausted(run):
            top = jnp.maximum(jnp.max(run[0] + zmax[0]), jnp.max(run[1] + zmax[1]))
            return top < EXP_UNDERFLOW

        def sweep(blocks, run, acc):
            offs = [pl.multiple_of(j * blk, blk) for j, _, _ in blocks]
            z = [[_dot(q_ref[0, a], k_ref[0, a, pl.ds(off, blk), :], _NT) for a in heads]
                 for off in offs]
            cl = []
            for (_, diag, valid), zb in zip(blocks, z):
                split = []
                for a in heads:
                    lk = -_softplus(zb[a])
                    if diag:
                        lk = jnp.where(causal, lk, 0.0)
                    if valid is not None:
                        lk = jnp.where(valid, lk, 0.0)
                    split.append(_split_bf16(lk))
                cl.append([_dot(split[a][0], csum) + _dot(split[a][1], csum) for a in heads])
            for (_, diag, valid), zb, clb, off in zip(blocks, z, cl, offs):
                w = []
                for a in heads:
                    wa = jnp.exp(zb[a] + clb[a] + run[a])
                    if diag:
                        wa = jnp.where(causal, wa, 0.0)
                    if valid is not None:
                        wa = jnp.where(valid, wa, 0.0)
                    w.append(wa.astype(BF16))
                run = [run[a] + clb[a][:, 0:1] for a in heads]
                acc = [acc[a] + _dot(w[a], v_ref[0, a, pl.ds(off, blk), :]) for a in heads]
            return run, acc

        run = [jnp.zeros((blk, 1), F32)] * 2
        acc = [jnp.zeros((blk, HEAD_DIM), F32)] * 2
        run, acc = sweep([(qi, True, None), (jnp.maximum(qi - 1, 0), False, qi >= 1)], run, acc)
        low = jnp.maximum(qi - 1, 0)

        def more(carry):
            low, done, _, _ = carry
            return (low > 0) & jnp.logical_not(done)

        def pair(carry):
            low, _, run, acc = carry
            run, acc = sweep([(low - 1, False, None), (jnp.maximum(low - 2, 0), False, low >= 2)],
                             run, acc)
            return jnp.maximum(low - 2, 0), exhausted(run), run, acc

        low, _, run, acc = lax.while_loop(more, pair, (low, exhausted(run), run, acc))
        low_ref[pl.program_id(0) * N_PAIRS + pl.program_id(1), qi] = low.astype(F32)
        for a in heads:
            o_ref[:, a * HEAD_DIM:(a + 1) * HEAD_DIM] = acc[a]
            tot_ref[0, a, 0] = jnp.broadcast_to(run[a], (blk, LANES)).T[0:8, :]

    return pl.pallas_call(
        body, name="sb_attn_fwd",
        grid=(nb, N_PAIRS, nq),
        in_specs=[pl.BlockSpec((1, 2, blk, HEAD_DIM), lambda b, h, i: (b, h, i, 0)),
                  pl.BlockSpec((1, 2, seq, HEAD_DIM), lambda b, h, i: (b, h, 0, 0)),
                  pl.BlockSpec((1, 2, seq, HEAD_DIM), lambda b, h, i: (b, h, 0, 0))],
        out_specs=[pl.BlockSpec((blk, LANES), lambda b, h, i: (b * nq + i, h)),
                   pl.BlockSpec((1, 2, 1, 8, blk), lambda b, h, i: (b, h, i, 0, 0)),
                   pl.BlockSpec(memory_space=pltpu.SMEM)],
        out_shape=[jax.ShapeDtypeStruct((nb * seq, D_BRANCH), F32),
                   jax.ShapeDtypeStruct((nb, N_HEADS, nq, 8, blk), F32),
                   jax.ShapeDtypeStruct((nb * N_PAIRS, nq), F32)],
        scratch_shapes=[pltpu.VMEM((2, 8, LANES), F32)],
        compiler_params=_params(3),
    )(qs, kn, vb)


def _attn_bwd(qs, kn, kt, vb, tot, low, d_o, nb, seq, blk):
    nq = seq // blk

    def body(q_ref, k_ref, kt_ref, v_ref, tot_ref, low_ref, do_ref, dq_ref, dk_ref, dv_ref):
        qi = pl.program_id(2)

        @pl.when(qi == 0)
        def _():
            dk_ref[...] = jnp.zeros_like(dk_ref)
            dv_ref[...] = jnp.zeros_like(dv_ref)

        r_i = lax.broadcasted_iota(jnp.int32, (blk, blk), 0)
        c_i = lax.broadcasted_iota(jnp.int32, (blk, blk), 1)
        before = jnp.where(c_i < r_i, 1.0, 0.0).astype(BF16)
        upto = jnp.where(c_i <= r_i, 1.0, 0.0).astype(BF16)
        causal = r_i < c_i

        heads = range(2)
        d_ob = [do_ref[:, a * HEAD_DIM:(a + 1) * HEAD_DIM].astype(BF16) for a in heads]
        total = [tot_ref[0, a, 0][0:1, :] for a in heads]

        def sweep(blocks, lsum, esum, dqt):
            def keep(x, diag, valid):
                if diag:
                    x = jnp.where(causal, x, 0.0)
                if valid is not None:
                    x = jnp.where(valid, x, 0.0)
                return x

            offs = [pl.multiple_of(j * blk, blk) for j, _, _ in blocks]
            zt = [[_dot(k_ref[0, a, pl.ds(off, blk), :], q_ref[0, a], _NT) for a in heads]
                  for off in offs]
            dwt = [[_dot(v_ref[0, a, pl.ds(off, blk), :], d_ob[a], _NT) for a in heads]
                   for off in offs]
            sp, lk, lpre = [], [], []
            for (_, diag, valid), ztb in zip(blocks, zt):
                sp.append([_softplus(ztb[a]) for a in heads])
                lk.append([keep(-sp[-1][a], diag, valid) for a in heads])
                split = [_split_bf16(lk[-1][a]) for a in heads]
                lpre.append([_dot(before, split[a][0]) + _dot(before, split[a][1]) for a in heads])
            wt, et, epre = [], [], []
            for i, (_, diag, valid) in enumerate(blocks):
                wt.append([keep(jnp.exp(zt[i][a] + (total[a] - lsum[a] - lpre[i][a])), diag, valid)
                           for a in heads])
                et.append([dwt[i][a] * wt[i][a] for a in heads])
                split = [_split_bf16(et[i][a]) for a in heads]
                epre.append([_dot(upto, split[a][0]) + _dot(upto, split[a][1]) for a in heads])
                lsum = [lsum[a] + lpre[i][a][blk - 1:blk, :] + lk[i][a][blk - 1:blk, :]
                        for a in heads]
            for i, (_, diag, valid) in enumerate(blocks):
                dzb = [keep(et[i][a] - jnp.exp(zt[i][a] - sp[i][a]) * (esum[a] + epre[i][a]),
                            diag, valid).astype(BF16) for a in heads]
                esum = [esum[a] + epre[i][a][blk - 1:blk, :] for a in heads]
                for a in heads:
                    dk_ref[0, a, pl.ds(offs[i], blk), :] += _dot(dzb[a], q_ref[0, a])
                    dv_ref[0, a, pl.ds(offs[i], blk), :] += _dot(wt[i][a].astype(BF16), d_ob[a])
                dqt = [dqt[a] + _dot(kt_ref[0, a, :, pl.ds(offs[i], blk)], dzb[a]) for a in heads]
            return lsum, esum, dqt

        row = [jnp.zeros((1, blk), F32)] * 2
        dqt = [jnp.zeros((HEAD_DIM, blk), F32)] * 2
        low = low_ref[pl.program_id(0) * N_PAIRS + pl.program_id(1), qi].astype(jnp.int32)
        low = jnp.clip(low, 0, jnp.maximum(qi - 1, 0))

        def pair(carry):
            j, lsum, esum, dqt = carry
            return (j + 2,) + sweep([(j, False, None), (j + 1, False, j + 1 < qi - 1)],
                                    lsum, esum, dqt)

        _, lsum, esum, dqt = lax.while_loop(lambda c: c[0] < qi - 1, pair, (low, row, row, dqt))
        _, _, dqt = sweep([(jnp.maximum(qi - 1, 0), False, qi >= 1), (qi, True, None)],
                          lsum, esum, dqt)
        for a in heads:
            dq_ref[:, a * HEAD_DIM:(a + 1) * HEAD_DIM] = dqt[a].T

    hm_acc = pl.BlockSpec((1, 2, seq, HEAD_DIM), lambda b, h, i: (b, h, 0, 0))
    tok = pl.BlockSpec((blk, LANES), lambda b, h, i: (b * nq + i, h))
    hm_shape = jax.ShapeDtypeStruct((nb, N_HEADS, seq, HEAD_DIM), F32)
    return pl.pallas_call(
        body, name="sb_attn_bwd",
        grid=(nb, N_PAIRS, nq),
        in_specs=[pl.BlockSpec((1, 2, blk, HEAD_DIM), lambda b, h, i: (b, h, i, 0)),
                  hm_acc,
                  pl.BlockSpec((1, 2, HEAD_DIM, seq), lambda b, h, i: (b, h, 0, 0)),
                  hm_acc,
                  pl.BlockSpec((1, 2, 1, 8, blk), lambda b, h, i: (b, h, i, 0, 0)),
                  pl.BlockSpec(memory_space=pltpu.SMEM),
                  tok],
        out_specs=[tok, hm_acc, hm_acc],
        out_shape=[jax.ShapeDtypeStruct((nb * seq, D_BRANCH), F32), hm_shape, hm_shape],
        compiler_params=_params(3),
    )(qs, kn, kt, vb, tot, low, d_o)


def _qk_bwd(proj, dqs, dkn, dvh, qw2, kw2, nb, seq, tq):
    nl = seq // tq
    scale = 1.0 / math.sqrt(HEAD_DIM)

    def body(q_ref, k_ref, dq_ref, dk_ref, dv_ref, qw_ref, kw_ref,
             dqr_ref, dkr_ref, dvr_ref, gq_ref, gk_ref):
        @pl.when((pl.program_id(0) == 0) & (pl.program_id(1) == 0) & (pl.program_id(2) == 0))
        def _():
            gq_ref[...] = jnp.zeros_like(gq_ref)
            gk_ref[...] = jnp.zeros_like(gk_ref)

        def norm_bwd(x, w, dy):
            r = lax.rsqrt(_pair_sum(x * x) * (1.0 / HEAD_DIM) + EPS)
            xhat = x * r
            g = dy * w
            m = _pair_sum(g * xhat) * (1.0 / HEAD_DIM)
            return r * (g - xhat * m), jnp.sum(dy * xhat, axis=0, keepdims=True)

        dqr, gq = norm_bwd(q_ref[...], qw_ref[...], dq_ref[...] * scale)
        dk2 = jnp.concatenate([dk_ref[0, 0], dk_ref[0, 1]], axis=1)
        dkr, gk = norm_bwd(k_ref[...], kw_ref[...], dk2)
        dqr_ref[...] = dqr.astype(BF16)
        dkr_ref[...] = dkr.astype(BF16)
        dvr_ref[...] = jnp.concatenate([dv_ref[0, 0], dv_ref[0, 1]], axis=1).astype(BF16)
        gq_ref[...] += gq
        gk_ref[...] += gk

    tok = lambda off: pl.BlockSpec((tq, LANES), lambda b, i, h: (b * nl + i, off + h))
    hm = pl.BlockSpec((1, 2, tq, HEAD_DIM), lambda b, i, h: (b, h, i, 0))
    vec = pl.BlockSpec((1, LANES), lambda b, i, h: (0, 0))
    tshape = jax.ShapeDtypeStruct((nb * seq, D_BRANCH), BF16)
    return pl.pallas_call(
        body, name="qk_bwd",
        grid=(nb, nl, N_PAIRS),
        in_specs=[tok(0), tok(N_PAIRS), tok(0), hm, hm, vec, vec],
        out_specs=[tok(0), tok(0), tok(0), vec, vec],
        out_shape=[tshape, tshape, tshape,
                   jax.ShapeDtypeStruct((1, LANES), F32), jax.ShapeDtypeStruct((1, LANES), F32)],
        compiler_params=_params(3),
    )(proj, proj, dqs, dkn, dvh, qw2, kw2)


def _shift_down(cur, prev, k):
    if k == 0:
        return cur
    rows = _row_iota(cur.shape)
    return jnp.where(rows < k, pltpu.roll(prev, k, axis=0), pltpu.roll(cur, k, axis=0))


def _shift_up(cur, nxt, k):
    if k == 0:
        return cur
    n = cur.shape[0]
    rows = _row_iota(cur.shape)
    return jnp.where(rows < n - k, pltpu.roll(cur, n - k, axis=0), pltpu.roll(nxt, n - k, axis=0))


def _conv_pre(cur, prev, w, b):
    out = b
    for i in range(CONV_TAPS):
        out = out + _shift_down(cur, prev, CONV_TAPS - 1 - i) * w[i:i + 1, :]
    return out


def _silu(x):
    return x * _sigmoid(x)


def _silu_grad(x):
    s = _sigmoid(x)
    return s * (1.0 + x * (1.0 - s))


def _chunk_decay(dt_raw, dtb, alog, expand, qc):
    dt = _softplus(dt_raw + dtb)
    d_a = dt * (-jnp.exp(alog))
    r_i = lax.broadcasted_iota(jnp.int32, (qc, qc), 0)
    c_i = lax.broadcasted_iota(jnp.int32, (qc, qc), 1)
    tril = r_i >= c_i
    a_cs = _dot(jnp.where(tril, 1.0, 0.0).astype(F32), d_a, precision=HIGHEST)
    dt_x = _dot(dt, expand, precision=HIGHEST)
    acs_x = _dot(a_cs, expand, precision=HIGHEST)
    return dt, d_a, a_cs, dt_x, acs_x, tril


def _ssd_fwd(proj, conv_w, conv_b, dtb, alog, dskip, nb, seq):
    qc = SSD_CHUNK
    nc = seq // qc

    def body(xs_ref, bc_ref, dt_ref, cw_ref, cb_ref, dtb_ref, al_ref, ds_ref,
             y_ref, st_ref, pxs_ref, pbc_ref, state_ref):
        @pl.when(pl.program_id(1) == 0)
        def _():
            pxs_ref[...] = jnp.zeros_like(pxs_ref)
            pbc_ref[...] = jnp.zeros_like(pbc_ref)
            state_ref[...] = jnp.zeros_like(state_ref)

        expand = _head_expand()
        xs_raw = xs_ref[...]
        bc_raw = bc_ref[...]
        cw = cw_ref[...]
        cb = cb_ref[...]
        xs = _silu(_conv_pre(xs_raw, pxs_ref[...], cw[:, :D_BRANCH], cb[:, :D_BRANCH]))
        bc = _silu(_conv_pre(bc_raw, pbc_ref[...], cw[:, D_BRANCH:], cb[:, D_BRANCH:]))
        pxs_ref[...] = xs_raw
        pbc_ref[...] = bc_raw

        dt, d_a, a_cs, dt_x, acs_x, tril = _chunk_decay(
            dt_ref[...], dtb_ref[...], al_ref[...], expand, qc)
        a_cst = a_cs.T
        aend_x = acs_x[qc - 1:qc, :]
        ea_x = jnp.exp(acs_x)
        dec_x = jnp.exp(aend_x - acs_x)
        xt = xs * dt_x
        xtb = xt.astype(BF16)
        xdb = (xt * dec_x).astype(BF16)
        d_x = _dot(jnp.broadcast_to(ds_ref[...], (8, LANES)), expand, precision=HIGHEST)[0:1, :]
        st_ref[0, 0] = state_ref[...]

        for g in range(N_GROUPS):
            gs = slice(g * GROUP_W, (g + 1) * GROUP_W)
            bg = bc[:, g * D_STATE:(g + 1) * D_STATE]
            cg = bc[:, (N_GROUPS + g) * D_STATE:(N_GROUPS + g + 1) * D_STATE]
            bgb = bg.astype(BF16)
            cgb = cg.astype(BF16)
            cbm = _dot(cgb, bgb, _NT)
            st_in = state_ref[g]
            y_off = _dot(cgb, st_in.astype(BF16)) * ea_x[:, gs]
            for k in range(HEADS_PER_GROUP):
                h = g * HEADS_PER_GROUP + k
                hs = slice(h * HEAD_DIM, (h + 1) * HEAD_DIM)
                seg = a_cs[:, h:h + 1] - a_cst[h:h + 1, :]
                gh = cbm * jnp.exp(jnp.where(tril, seg, -1e30))
                y_h = _dot(gh.astype(BF16), xtb[:, hs]) + y_off[:, k * HEAD_DIM:(k + 1) * HEAD_DIM]
                y_ref[:, hs] = y_h + d_x[:, hs] * xs[:, hs]
            state_ref[g] = st_in * jnp.exp(aend_x[:, gs]) + _dot(bg.T.astype(BF16), xdb[:, gs])

    nblk = lambda w, off: pl.BlockSpec((qc, w), lambda b, c: (b * nc + c, off))
    full = lambda r, w: pl.BlockSpec((r, w), lambda b, c: (0, 0))
    return pl.pallas_call(
        body, name="ssd_fwd",
        grid=(nb, nc),
        in_specs=[nblk(D_BRANCH, COL_XS // D_BRANCH), nblk(D_BC, COL_BC // D_BC),
                  nblk(LANES, COL_DT // LANES),
                  full(CONV_TAPS, D_CONV), full(1, D_CONV), full(1, LANES), full(1, LANES),
                  full(1, LANES)],
        out_specs=[pl.BlockSpec((qc, D_BRANCH), lambda b, c: (b * nc + c, 0)),
                   pl.BlockSpec((1, 1, N_GROUPS, D_STATE, GROUP_W), lambda b, c: (b, c, 0, 0, 0))],
        out_shape=[jax.ShapeDtypeStruct((nb * seq, D_BRANCH), F32),
                   jax.ShapeDtypeStruct((nb, nc, N_GROUPS, D_STATE, GROUP_W), F32)],
        scratch_shapes=[pltpu.VMEM((qc, D_BRANCH), F32), pltpu.VMEM((qc, D_BC), F32),
                        pltpu.VMEM((N_GROUPS, D_STATE, GROUP_W), F32)],
        compiler_params=_params(2),
    )(proj, proj, proj, conv_w, conv_b, dtb, alog, dskip)


def _ssd_bwd(proj, d_y, states, conv_w, conv_b, dtb, alog, dskip, nb, seq):
    qc = SSD_CHUNK
    nc = seq // qc

    def body(xs_ref, bc_ref, dt_ref, pxs_ref, pbc_ref, dy_ref, st_ref, stn_ref,
             cw_ref, cb_ref, dtb_ref, al_ref, ds_ref,
             dx_ref, gcw_ref, gcb_ref, gdtb_ref, gal_ref, gds_ref,
             dst_ref, nxs_ref, nbc_ref, yd_ref, dxt_ref):
        step = pl.program_id(1)
        chunk = nc - 1 - step

        @pl.when(step == 0)
        def _():
            dst_ref[...] = jnp.zeros_like(dst_ref)
            nxs_ref[...] = jnp.zeros_like(nxs_ref)
            nbc_ref[...] = jnp.zeros_like(nbc_ref)

        @pl.when((pl.program_id(0) == 0) & (step == 0))
        def _():
            gcw_ref[...] = jnp.zeros_like(gcw_ref)
            gcb_ref[...] = jnp.zeros_like(gcb_ref)
            gdtb_ref[...] = jnp.zeros_like(gdtb_ref)
            gal_ref[...] = jnp.zeros_like(gal_ref)
            gds_ref[...] = jnp.zeros_like(gds_ref)

        expand = _head_expand()
        collapse = lambda v: _dot(v, expand, _NT, precision=HIGHEST)
        first = jnp.where(chunk == 0, 0.0, 1.0)
        xs_raw = xs_ref[...]
        bc_raw = bc_ref[...]
        pxs = pxs_ref[...] * first
        pbc = pbc_ref[...] * first
        cw = cw_ref[...]
        cb = cb_ref[...]
        pre_xs = _conv_pre(xs_raw, pxs, cw[:, :D_BRANCH], cb[:, :D_BRANCH])
        pre_bc = _conv_pre(bc_raw, pbc, cw[:, D_BRANCH:], cb[:, D_BRANCH:])
        xs = _silu(pre_xs)
        bc = _silu(pre_bc)

        dt_in = dt_ref[...] + dtb_ref[...]
        dt, d_a, a_cs, dt_x, acs_x, tril = _chunk_decay(
            dt_ref[...], dtb_ref[...], al_ref[...], expand, qc)
        a_cst = a_cs.T
        aend_x = acs_x[qc - 1:qc, :]
        ea_x = jnp.exp(acs_x)
        dec_x = jnp.exp(aend_x - acs_x)
        xt = xs * dt_x
        xtb = xt.astype(BF16)
        xdb = (xt * dec_x).astype(BF16)
        d_x = _dot(jnp.broadcast_to(ds_ref[...], (8, LANES)), expand, precision=HIGHEST)[0:1, :]

        dy = dy_ref[...]
        dyb = dy.astype(BF16)
        dyeab = (dy * ea_x).astype(BF16)
        gds_ref[...] += collapse(jnp.broadcast_to(jnp.sum(dy * xs, axis=0, keepdims=True),
                                                  (8, D_BRANCH)))[0:1, :]

        d_bc = []
        d_cc = []
        y_offs = []
        dxt_states = []
        end_terms = []
        for g in range(N_GROUPS):
            gs = slice(g * GROUP_W, (g + 1) * GROUP_W)
            bg = bc[:, g * D_STATE:(g + 1) * D_STATE]
            cg = bc[:, (N_GROUPS + g) * D_STATE:(N_GROUPS + g + 1) * D_STATE]
            bgb = bg.astype(BF16)
            cgb = cg.astype(BF16)
            cbm = _dot(cgb, bgb, _NT)
            st_in = st_ref[0, 0, g]
            st_inb = st_in.astype(BF16)
            d_st = dst_ref[g]
            d_stb = d_st.astype(BF16)
            y_offs.append(_dot(cgb, st_inb) * ea_x[:, gs])
            dxt_states.append(_dot(bgb, d_stb) * dec_x[:, gs])
            d_c = _dot(dyeab[:, gs], st_inb, _NT)
            d_b = _dot(xdb[:, gs], d_stb, _NT)
            d_cb = jnp.zeros((qc, qc), F32)
            for k in range(HEADS_PER_GROUP):
                h = g * HEADS_PER_GROUP + k
                hs = slice(h * HEAD_DIM, (h + 1) * HEAD_DIM)
                seg = a_cs[:, h:h + 1] - a_cst[h:h + 1, :]
                lh = jnp.exp(jnp.where(tril, seg, -1e30))
                ghb = (cbm * lh).astype(BF16)
                d_cb = d_cb + _dot(dyb[:, hs], xtb[:, hs], _NT) * lh
                yd_ref[:, hs] = _dot(ghb, xtb[:, hs])
                dxt_ref[:, hs] = _dot(ghb, dyb[:, hs], _TN)
            d_cbb = d_cb.astype(BF16)
            d_cc.append(d_c + _dot(d_cbb, bgb))
            d_bc.append(d_b + _dot(d_cbb, cgb, _TN))
            end_terms.append(jnp.sum(d_st * stn_ref[0, 0, g], axis=0, keepdims=True))
            dst_ref[g] = d_st * jnp.exp(aend_x[:, gs]) + _dot(cg.T.astype(BF16), dyeab[:, gs])

        y_off = jnp.concatenate(y_offs, axis=1)
        dxt_state = jnp.concatenate(dxt_states, axis=1)
        dxt = dxt_ref[...] + dxt_state
        last = jnp.where(chunk == nc - 1, 0.0, 1.0)
        end_c = collapse(jnp.broadcast_to(jnp.concatenate(end_terms, axis=1), (8, D_BRANCH)))[0:1, :]
        da_cs = collapse(dyb.astype(F32) * yd_ref[...] - dxt_ref[...] * xtb.astype(F32)
                         + dy * y_off - dxt_state * xt)
        da_cs = da_cs + jnp.where(_row_iota(da_cs.shape) == qc - 1, end_c * last, 0.0)
        triu = lax.broadcasted_iota(jnp.int32, (qc, qc), 0) <= lax.broadcasted_iota(jnp.int32, (qc, qc), 1)
        dd_a = _dot(jnp.where(triu, 1.0, 0.0).astype(F32), da_cs, precision=HIGHEST)
        ddt = dd_a * (-jnp.exp(al_ref[...])) + collapse(dxt * xs)
        head_lanes = _lane_iota(ddt.shape) < N_HEADS
        ddt_raw = jnp.where(head_lanes, ddt * _sigmoid(dt_in), 0.0)
        gal_ref[...] += jnp.sum(jnp.where(head_lanes, dd_a * d_a, 0.0), axis=0, keepdims=True)
        gdtb_ref[...] += jnp.sum(ddt_raw, axis=0, keepdims=True)

        dpre_xs = (dxt * dt_x + d_x * dy) * _silu_grad(pre_xs)
        dpre_bc = jnp.concatenate(d_bc + d_cc, axis=1) * _silu_grad(pre_bc)
        gcb_ref[...] += jnp.concatenate([jnp.sum(dpre_xs, axis=0, keepdims=True),
                                         jnp.sum(dpre_bc, axis=0, keepdims=True)], axis=1)
        nxs = nxs_ref[...]
        nbc = nbc_ref[...]
        du_xs = jnp.zeros_like(dpre_xs)
        du_bc = jnp.zeros_like(dpre_bc)
        for i in range(CONV_TAPS):
            k = CONV_TAPS - 1 - i
            gcw_ref[i:i + 1, :] += jnp.concatenate(
                [jnp.sum(dpre_xs * _shift_down(xs_raw, pxs, k), axis=0, keepdims=True),
                 jnp.sum(dpre_bc * _shift_down(bc_raw, pbc, k), axis=0, keepdims=True)], axis=1)
            du_xs = du_xs + _shift_up(dpre_xs, nxs, k) * cw[i:i + 1, :D_BRANCH]
            du_bc = du_bc + _shift_up(dpre_bc, nbc, k) * cw[i:i + 1, D_BRANCH:]
        nxs_ref[...] = dpre_xs
        nbc_ref[...] = dpre_bc

        dx_ref[:, :D_BRANCH] = du_xs.astype(BF16)
        dx_ref[:, D_BRANCH:D_CONV] = du_bc.astype(BF16)
        dx_ref[:, D_CONV:D_CONV + LANES] = ddt_raw.astype(BF16)
        dx_ref[:, D_CONV + LANES:] = jnp.zeros((qc, 2048 - D_CONV - LANES), BF16)

    rev = lambda b, c: b * nc + (nc - 1 - c)
    prv = lambda b, c: b * nc + jnp.maximum(nc - 2 - c, 0)
    nblk = lambda w, off, f: pl.BlockSpec((qc, w), lambda b, c: (f(b, c), off))
    full = lambda r, w: pl.BlockSpec((r, w), lambda b, c: (0, 0))
    st_spec = lambda f: pl.BlockSpec((1, 1, N_GROUPS, D_STATE, GROUP_W),
                                     lambda b, c: (b, f(c), 0, 0, 0))
    return pl.pallas_call(
        body, name="ssd_bwd",
        grid=(nb, nc),
        in_specs=[nblk(D_BRANCH, COL_XS // D_BRANCH, rev), nblk(D_BC, COL_BC // D_BC, rev),
                  nblk(LANES, COL_DT // LANES, rev),
                  nblk(D_BRANCH, COL_XS // D_BRANCH, prv), nblk(D_BC, COL_BC // D_BC, prv),
                  nblk(D_BRANCH, 0, rev),
                  st_spec(lambda c: nc - 1 - c), st_spec(lambda c: jnp.minimum(nc - c, nc - 1)),
                  full(CONV_TAPS, D_CONV), full(1, D_CONV), full(1, LANES), full(1, LANES),
                  full(1, LANES)],
        out_specs=[nblk(2048, 0, rev), full(8, D_CONV), full(1, D_CONV), full(1, LANES),
                   full(1, LANES), full(1, LANES)],
        out_shape=[jax.ShapeDtypeStruct((nb * seq, 2048), BF16),
                   jax.ShapeDtypeStruct((8, D_CONV), F32), jax.ShapeDtypeStruct((1, D_CONV), F32),
                   jax.ShapeDtypeStruct((1, LANES), F32), jax.ShapeDtypeStruct((1, LANES), F32),
                   jax.ShapeDtypeStruct((1, LANES), F32)],
        scratch_shapes=[pltpu.VMEM((N_GROUPS, D_STATE, GROUP_W), F32),
                        pltpu.VMEM((qc, D_BRANCH), F32), pltpu.VMEM((qc, D_BC), F32),
                        pltpu.VMEM((qc, D_BRANCH), F32), pltpu.VMEM((qc, D_BRANCH), F32)],
        compiler_params=_params(2),
    )(proj, proj, proj, proj, proj, d_y, states, states, conv_w, conv_b, dtb, alog, dskip)


def _mid(o_sb, y_ssd, proj, x2, target, sb_w, ssd_w, w_out_b, tm):
    t = x2.shape[0]
    inv_d = 1.0 / D_MODEL

    def body(o_ref, y_ref, zsb_ref, zssd_ref, x_ref, tg_ref, sbw_ref, ssdw_ref, w_ref,
             dout_ref, dosb_ref, dy_ref, dz_ref, gw_ref, gsb_ref, gssd_ref, loss_ref):
        @pl.when(pl.program_id(0) == 0)
        def _():
            gw_ref[...] = jnp.zeros_like(gw_ref)
            gsb_ref[...] = jnp.zeros_like(gsb_ref)
            gssd_ref[...] = jnp.zeros_like(gssd_ref)
            loss_ref[...] = jnp.zeros_like(loss_ref)

        def branch(val, z, w):
            gate = _silu(z)
            g = val * gate
            r = lax.rsqrt(jnp.mean(g * g, axis=1, keepdims=True) + EPS)
            xhat = g * r
            return gate, r, xhat, (xhat * w).astype(BF16)

        o = o_ref[...]
        y = y_ref[...]
        z_sb = zsb_ref[...]
        z_ssd = zssd_ref[...]
        gate_a, r_a, xhat_a, mix_a = branch(o, z_sb, sbw_ref[...])
        gate_b, r_b, xhat_b, mix_b = branch(y, z_ssd, ssdw_ref[...])
        out = x_ref[...] + _dot(mix_a, w_ref[:D_BRANCH, :]) + _dot(mix_b, w_ref[D_BRANCH:, :])
        diff = out - tg_ref[...]
        loss_ref[...] += 0.5 * inv_d * jnp.sum(diff * diff)
        d_out = diff * inv_d
        dout_ref[...] = d_out
        d_outb = d_out.astype(BF16)
        gw_ref[:D_BRANCH, :] += _dot(mix_a, d_outb, _TN)
        gw_ref[D_BRANCH:, :] += _dot(mix_b, d_outb, _TN)

        def branch_bwd(dmix, val, z, w, gate, r, xhat):
            gg = dmix * w
            m = jnp.mean(gg * xhat, axis=1, keepdims=True)
            dg = r * (gg - xhat * m)
            return dg * gate, dg * val * _silu_grad(z), jnp.sum(dmix * xhat, axis=0, keepdims=True)

        dmix_a = _dot(d_outb, w_ref[:D_BRANCH, :], _NT)
        dmix_b = _dot(d_outb, w_ref[D_BRANCH:, :], _NT)
        d_o, dz_a, gsb = branch_bwd(dmix_a, o, z_sb, sbw_ref[...], gate_a, r_a, xhat_a)
        d_y, dz_b, gssd = branch_bwd(dmix_b, y, z_ssd, ssdw_ref[...], gate_b, r_b, xhat_b)
        dosb_ref[...] = d_o
        dy_ref[...] = d_y
        dz_ref[:, :D_BRANCH] = dz_a.astype(BF16)
        dz_ref[:, D_BRANCH:] = dz_b.astype(BF16)
        gsb_ref[...] += gsb
        gssd_ref[...] += gssd

    row = lambda w, off: pl.BlockSpec((tm, w), lambda i: (i, off))
    full = lambda r, w: pl.BlockSpec((r, w), lambda i: (0, 0))
    tok = jax.ShapeDtypeStruct((t, D_MODEL), F32)
    return pl.pallas_call(
        body, name="mid",
        grid=(t // tm,),
        in_specs=[row(D_BRANCH, 0), row(D_BRANCH, 0), row(D_BRANCH, 3), row(D_BRANCH, 4),
                  row(D_MODEL, 0), row(D_MODEL, 0), full(1, D_BRANCH), full(1, D_BRANCH),
                  full(2 * D_BRANCH, D_MODEL)],
        out_specs=[row(D_MODEL, 0), row(D_BRANCH, 0), row(D_BRANCH, 0), row(2 * D_BRANCH, 0),
                   full(2 * D_BRANCH, D_MODEL), full(1, D_BRANCH), full(1, D_BRANCH),
                   full(1, LANES)],
        out_shape=[tok, tok, tok, jax.ShapeDtypeStruct((t, 2 * D_BRANCH), BF16),
                   jax.ShapeDtypeStruct((2 * D_BRANCH, D_MODEL), F32),
                   jax.ShapeDtypeStruct((1, D_BRANCH), F32), jax.ShapeDtypeStruct((1, D_BRANCH), F32),
                   jax.ShapeDtypeStruct((1, LANES), F32)],
        compiler_params=_params(1),
    )(o_sb, y_ssd, proj, proj, x2, target, sb_w, ssd_w, w_out_b)


def _in_proj_bwd_x(d_proj, w_in_b, x2, d_out, norm_w, tm):
    t = x2.shape[0]

    def body(dp_ref, w_ref, x_ref, dout_ref, nw_ref, gx_ref, gnw_ref, acc_ref):
        j = pl.program_id(1)

        @pl.when((pl.program_id(0) == 0) & (j == 0))
        def _():
            gnw_ref[...] = jnp.zeros_like(gnw_ref)

        @pl.when(j == 0)
        def _():
            acc_ref[...] = jnp.zeros_like(acc_ref)

        acc_ref[...] += _dot(dp_ref[...], w_ref[...], _NT)

        @pl.when(j == N_COLBLK - 1)
        def _():
            xf = x_ref[...]
            d_hn = acc_ref[...]
            r = lax.rsqrt(jnp.mean(xf * xf, axis=1, keepdims=True) + EPS)
            xhat = xf * r
            g = d_hn * nw_ref[...]
            m = jnp.mean(g * xhat, axis=1, keepdims=True)
            gx_ref[...] = dout_ref[...] + r * (g - xhat * m)
            gnw_ref[...] += jnp.sum(d_hn * xhat, axis=0, keepdims=True)

    return pl.pallas_call(
        body, name="in_proj_bwd_x",
        grid=(t // tm, N_COLBLK),
        in_specs=[pl.BlockSpec((tm, 1024), lambda i, j: (i, j)),
                  pl.BlockSpec((D_MODEL, 1024), lambda i, j: (0, j)),
                  pl.BlockSpec((tm, D_MODEL), lambda i, j: (i, 0)),
                  pl.BlockSpec((tm, D_MODEL), lambda i, j: (i, 0)),
                  pl.BlockSpec((1, D_MODEL), lambda i, j: (0, 0))],
        out_specs=[pl.BlockSpec((tm, D_MODEL), lambda i, j: (i, 0)),
                   pl.BlockSpec((1, D_MODEL), lambda i, j: (0, 0))],
        out_shape=[jax.ShapeDtypeStruct((t, D_MODEL), F32), jax.ShapeDtypeStruct((1, D_MODEL), F32)],
        scratch_shapes=[pltpu.VMEM((tm, D_MODEL), F32)],
        compiler_params=_params(2),
    )(d_proj, w_in_b, x2, d_out, norm_w)


def _in_proj_bwd_w(hn, d_proj, tm):
    t = hn.shape[0]

    def body(hn_ref, dp_ref, gw_ref):
        @pl.when(pl.program_id(1) == 0)
        def _():
            gw_ref[...] = jnp.zeros_like(gw_ref)

        gw_ref[...] += _dot(hn_ref[...], dp_ref[...], _TN)

    return pl.pallas_call(
        body, name="in_proj_bwd_w",
        grid=(N_COLBLK, t // tm),
        in_specs=[pl.BlockSpec((tm, D_MODEL), lambda j, i: (i, 0)),
                  pl.BlockSpec((tm, 1024), lambda j, i: (i, j))],
        out_specs=pl.BlockSpec((D_MODEL, 1024), lambda j, i: (0, j)),
        out_shape=jax.ShapeDtypeStruct((D_MODEL, D_IN_PAD), F32),
        compiler_params=_params(2),
    )(hn, d_proj)


def _adamw(parts, w, m, v, tr, name):
    rows, cols = w.shape
    c1 = 1.0 - ADAM_B1 ** ADAM_STEP
    c2 = 1.0 - ADAM_B2 ** ADAM_STEP

    def body(p_ref, w_ref, m_ref, v_ref, g_ref, d_ref, nm_ref, nv_ref):
        g = p_ref[0]
        for s in range(1, N_DEV):
            g = g + p_ref[s]
        nm = ADAM_B1 * m_ref[...] + (1.0 - ADAM_B1) * g
        nv = ADAM_B2 * v_ref[...] + (1.0 - ADAM_B2) * (g * g)
        g_ref[...] = g
        nm_ref[...] = nm
        nv_ref[...] = nv
        d_ref[...] = -ADAM_LR * ((nm / c1) / (jnp.sqrt(nv / c2) + ADAM_EPS) + ADAM_WD * w_ref[...])

    blk = pl.BlockSpec((tr, cols), lambda i: (i, 0))
    shape = jax.ShapeDtypeStruct((rows, cols), F32)
    return pl.pallas_call(
        body, name=name,
        grid=(rows // tr,),
        in_specs=[pl.BlockSpec((N_DEV, tr, cols), lambda i: (0, i, 0)), blk, blk, blk],
        out_specs=[blk, blk, blk, blk],
        out_shape=[shape, shape, shape, shape],
        compiler_params=_params(1),
    )(parts, w, m, v)


def _mesh_place():
    x, y, c = lax.axis_index("x"), lax.axis_index("y"), lax.axis_index("c")
    return x, y, c, 4 * x + 2 * y + c


def _peer(x, y, c, k):
    px = 1 - x if k & 4 else x
    py = 1 - y if k & 2 else y
    pc = 1 - c if k & 1 else c
    return (px, py, pc), 4 * px + 2 * py + pc


def _exchange(srcs, scatter, name):
    n = len(srcs)

    def body(*refs):
        src_refs, dst_refs = refs[:n], refs[n:2 * n]
        send_sems, recv_sems, loc_sems = refs[2 * n:]
        x, y, c, me = _mesh_place()

        def src_of(i, idx):
            return src_refs[i].at[idx] if scatter[i] else src_refs[i]

        local = [pltpu.make_async_copy(src_of(i, me), dst_refs[i].at[me], loc_sems.at[i])
                 for i in range(n)]
        for cp in local:
            cp.start()
        sends = []
        for k in range(1, N_DEV):
            peer, pidx = _peer(x, y, c, k)
            for i in range(n):
                s = i * (N_DEV - 1) + k - 1
                cp = pltpu.make_async_remote_copy(
                    src_ref=src_of(i, pidx), dst_ref=dst_refs[i].at[me],
                    send_sem=send_sems.at[s], recv_sem=recv_sems.at[s],
                    device_id=peer, device_id_type=pl.DeviceIdType.MESH)
                cp.start()
                sends.append(cp)
        for k in range(1, N_DEV):
            peer, pidx = _peer(x, y, c, k)
            for i in range(n):
                s = i * (N_DEV - 1) + k - 1
                pltpu.make_async_remote_copy(
                    src_ref=src_of(i, pidx), dst_ref=dst_refs[i].at[pidx],
                    send_sem=send_sems.at[s], recv_sem=recv_sems.at[s],
                    device_id=peer, device_id_type=pl.DeviceIdType.MESH).wait_recv()
        for cp in sends:
            cp.wait_send()
        for cp in local:
            cp.wait()

    out_shape = [jax.ShapeDtypeStruct(s.shape if sc else (N_DEV,) + s.shape, s.dtype)
                 for s, sc in zip(srcs, scatter)]
    any_spec = pl.BlockSpec(memory_space=pl.ANY)
    return pl.pallas_call(
        body, name=name,
        in_specs=[any_spec] * n, out_specs=[any_spec] * n, out_shape=out_shape,
        scratch_shapes=[pltpu.SemaphoreType.DMA((n * (N_DEV - 1),)),
                        pltpu.SemaphoreType.DMA((n * (N_DEV - 1),)),
                        pltpu.SemaphoreType.DMA((n,))],
    )(*srcs)


def _pad_lanes(v, width=LANES):
    return jnp.pad(v, ((0, 0), (0, width - v.shape[1])))


def _local_step(x, target, norm_w, w_in_b, q_norm_w, k_norm_w, conv_w, conv_b, dt_bias, a_log,
                d_skip, sb_norm_w, ssd_norm_w, w_out_b, tm=512, tq=512, tmid=128, blk=ATT_BLK):
    nb, seq, _ = x.shape
    t = nb * seq
    x2 = x.reshape(t, D_MODEL)
    tg2 = target.reshape(t, D_MODEL)
    qw2 = jnp.tile(q_norm_w, (1, 2))
    kw2 = jnp.tile(k_norm_w, (1, 2))
    dtb, alog, dsk = _pad_lanes(dt_bias), _pad_lanes(a_log), _pad_lanes(d_skip)

    proj, hn = _in_proj(x2, norm_w, w_in_b, tm)
    qs, kn, vb, kt = _qk_prep(proj, qw2, kw2, nb, seq, tq)
    o_sb, sb_tot, sb_low = _attn_fwd(qs, kn, vb, nb, seq, blk)
    y_ssd, states = _ssd_fwd(proj, conv_w, conv_b, dtb, alog, dsk, nb, seq)
    d_out, d_osb, d_y, d_z, g_wout, g_sbw, g_ssdw, loss = _mid(
        o_sb, y_ssd, proj, x2, tg2, sb_norm_w, ssd_norm_w, w_out_b, tmid)
    dqs, dkn, dvh = _attn_bwd(qs, kn, kt, vb, sb_tot, sb_low, d_osb, nb, seq, blk)
    dq_raw, dk_raw, dv_raw, g_qw, g_kw = _qk_bwd(proj, dqs, dkn, dvh, qw2, kw2, nb, seq, tq)
    d_xbc, g_cw, g_cb, g_dtb, g_alog, g_dsk = _ssd_bwd(
        proj, d_y, states, conv_w, conv_b, dtb, alog, dsk, nb, seq)
    d_proj = jnp.concatenate([dq_raw, dk_raw, dv_raw, d_z, d_xbc], axis=1)
    grad_x, g_nw = _in_proj_bwd_x(d_proj, w_in_b, x2, d_out, norm_w, tm)
    g_win = _in_proj_bwd_w(hn, d_proj, tm)

    small = dict(
        norm_w=g_nw,
        q_norm_w=g_qw[:, :HEAD_DIM] + g_qw[:, HEAD_DIM:],
        k_norm_w=g_kw[:, :HEAD_DIM] + g_kw[:, HEAD_DIM:],
        conv_b=g_cb, dt_bias=g_dtb[:, :N_HEADS], A_log=g_alog[:, :N_HEADS],
        D_skip=g_dsk[:, :N_HEADS], sb_norm_w=g_sbw, ssd_norm_w=g_ssdw)
    return (loss[0, 0], grad_x.reshape(nb, seq, D_MODEL), g_win[:, :D_IN], g_wout,
            g_cw[:CONV_TAPS], small)


_SMALL = ("norm_w", "q_norm_w", "k_norm_w", "conv_b", "dt_bias", "A_log", "D_skip",
          "sb_norm_w", "ssd_norm_w")


def _pack_small(vals):
    rows = [_pad_lanes(vals[n], -(-vals[n].shape[1] // LANES) * LANES).reshape(-1, LANES)
            for n in _SMALL]
    packed = jnp.concatenate(rows, axis=0)
    return jnp.pad(packed, ((0, 48 - packed.shape[0]), (0, 0)))


def _unpack_small(packed, like):
    out, r = {}, 0
    for n in _SMALL:
        width = like[n].shape[1]
        nr = -(-width // LANES)
        out[n] = packed[r:r + nr].reshape(1, nr * LANES)[:, :width]
        r += nr
    return out


def kernel(x, norm_w, w_in, q_norm_w, k_norm_w, conv_w, conv_b, dt_bias, A_log, D_skip, sb_norm_w, ssd_norm_w, w_out, loss_target, m_norm_w, m_w_in, m_q_norm_w, m_k_norm_w, m_conv_w, m_conv_b, m_dt_bias, m_A_log, m_D_skip, m_sb_norm_w, m_ssd_norm_w, m_w_out, v_norm_w, v_w_in, v_q_norm_w, v_k_norm_w, v_conv_w, v_conv_b, v_dt_bias, v_A_log, v_D_skip, v_sb_norm_w, v_ssd_norm_w, v_w_out):
    w_sh = D_IN // N_DEV
    c_sh = D_CONV // N_DEV

    win_all, wout_all, cw_all = _exchange(
        [w_in[0].astype(BF16), w_out[0].astype(BF16), conv_w[0]], [False, False, False],
        "gather_weights")
    w_in_b = jnp.pad(jnp.transpose(win_all, (1, 0, 2)).reshape(D_MODEL, D_IN),
                     ((0, 0), (0, D_IN_PAD - D_IN)))
    w_out_b = wout_all.reshape(2 * D_BRANCH, D_MODEL)
    conv_full = jnp.transpose(cw_all, (1, 0, 2)).reshape(CONV_TAPS, D_CONV)

    loss, grad_x, g_win, g_wout, g_cw, g_small = _local_step(
        x, loss_target, norm_w, w_in_b, q_norm_w, k_norm_w, conv_full, conv_b, dt_bias, A_log,
        D_skip, sb_norm_w, ssd_norm_w, w_out_b)

    win_parts, wout_parts, cw_parts, small_parts = _exchange(
        [jnp.transpose(g_win.reshape(D_MODEL, N_DEV, w_sh), (1, 0, 2)),
         g_wout.reshape(N_DEV, 2 * D_BRANCH // N_DEV, D_MODEL),
         jnp.pad(jnp.transpose(g_cw.reshape(CONV_TAPS, N_DEV, c_sh), (1, 0, 2)),
                 ((0, 0), (0, 8 - CONV_TAPS), (0, 0))),
         _pack_small(g_small)],
        [True, True, True, False], "scatter_grads")

    small_w = dict(norm_w=norm_w, q_norm_w=q_norm_w, k_norm_w=k_norm_w, conv_b=conv_b,
                   dt_bias=dt_bias, A_log=A_log, D_skip=D_skip, sb_norm_w=sb_norm_w,
                   ssd_norm_w=ssd_norm_w)
    small_m = dict(norm_w=m_norm_w, q_norm_w=m_q_norm_w, k_norm_w=m_k_norm_w, conv_b=m_conv_b,
                   dt_bias=m_dt_bias, A_log=m_A_log, D_skip=m_D_skip, sb_norm_w=m_sb_norm_w,
                   ssd_norm_w=m_ssd_norm_w)
    small_v = dict(norm_w=v_norm_w, q_norm_w=v_q_norm_w, k_norm_w=v_k_norm_w, conv_b=v_conv_b,
                   dt_bias=v_dt_bias, A_log=v_A_log, D_skip=v_D_skip, sb_norm_w=v_sb_norm_w,
                   ssd_norm_w=v_ssd_norm_w)

    pad8 = lambda a: jnp.pad(a, ((0, 8 - CONV_TAPS), (0, 0)))
    r_win = _adamw(win_parts, w_in[0], m_w_in[0], v_w_in[0], 128, "adamw_w_in")
    r_wout = _adamw(wout_parts, w_out[0], m_w_out[0], v_w_out[0], 128, "adamw_w_out")
    r_cw = _adamw(cw_parts, pad8(conv_w[0]), pad8(m_conv_w[0]), pad8(v_conv_w[0]), 8, "adamw_conv_w")
    r_small = _adamw(small_parts, _pack_small(small_w), _pack_small(small_m),
                     _pack_small(small_v), 48, "adamw_small")

    loss = lax.psum(loss, ("x", "y", "c"))
    res = {"w_in": [a[None] for a in r_win], "w_out": [a[None] for a in r_wout],
           "conv_w": [a[:CONV_TAPS][None] for a in r_cw]}
    unpacked = [_unpack_small(a, small_w) for a in r_small]
    for n in _SMALL:
        res[n] = [u[n] for u in unpacked]
    order = ("norm_w", "w_in", "q_norm_w", "k_norm_w", "conv_w", "conv_b", "dt_bias", "A_log",
             "D_skip", "sb_norm_w", "ssd_norm_w", "w_out")
    outs = [loss, grad_x]
    for kind in range(4):
        outs += [res[n][kind] for n in order]
    return tuple(outs)
```

```python
import functools
import math

import jax
import jax.numpy as jnp
from jax import lax
from jax.experimental import pallas as pl
from jax.experimental.pallas import tpu as pltpu

F32 = jnp.float32
BF16 = jnp.bfloat16
HIGHEST = lax.Precision.HIGHEST

D_MODEL = 1024
N_HEADS = 16
HEAD_DIM = 64
N_PAIRS = N_HEADS // 2
D_BRANCH = 1024
N_GROUPS = 2
HEADS_PER_GROUP = 8
D_STATE = 128
GROUP_W = HEADS_PER_GROUP * HEAD_DIM
D_BC = 2 * N_GROUPS * D_STATE
D_CONV = D_BRANCH + D_BC
D_IN = 6672
D_IN_PAD = 7168
N_COLBLK = D_IN_PAD // 1024
COL_XS = 5120
COL_BC = 6144
COL_DT = 6656
EPS = 1e-6
CONV_TAPS = 4
N_DEV = 8

LANES = 128
SSD_CHUNK = 128
ATT_BLK = 256
EXP_UNDERFLOW = -105.0
VMEM_LIMIT = 56 * 1024 * 1024

ADAM_LR = 0.001
ADAM_B1 = 0.9
ADAM_B2 = 0.999
ADAM_EPS = 1e-08
ADAM_WD = 0.01
ADAM_STEP = 10

_NT = (((1,), (1,)), ((), ()))
_TN = (((0,), (0,)), ((), ()))


def _params(n_grid):
    return pltpu.CompilerParams(dimension_semantics=("arbitrary",) * n_grid,
                                vmem_limit_bytes=VMEM_LIMIT)


def _dot(a, b, dims=None, precision=None):
    if dims is None:
        return jnp.dot(a, b, preferred_element_type=F32, precision=precision)
    return lax.dot_general(a, b, dims, preferred_element_type=F32, precision=precision)


def _sigmoid(x):
    return 1.0 / (1.0 + jnp.exp(-x))


def _softplus(x):
    return jnp.maximum(x, 0.0) + jnp.log(1.0 + jnp.exp(-jnp.abs(x)))


def _split_bf16(x):
    hi = x.astype(BF16)
    lo = (x - hi.astype(F32)).astype(BF16)
    return hi, lo


def _lane_iota(shape):
    return lax.broadcasted_iota(jnp.int32, shape, len(shape) - 1)


def _row_iota(shape):
    return lax.broadcasted_iota(jnp.int32, shape, len(shape) - 2)


def _pair_sum(x):
    low = _lane_iota(x.shape) < HEAD_DIM
    s0 = jnp.sum(jnp.where(low, x, 0.0), axis=1, keepdims=True)
    s1 = jnp.sum(jnp.where(low, 0.0, x), axis=1, keepdims=True)
    return jnp.where(low, s0, s1)


def _head_expand():
    r = lax.broadcasted_iota(jnp.int32, (LANES, D_BRANCH), 0)
    c = lax.broadcasted_iota(jnp.int32, (LANES, D_BRANCH), 1)
    return jnp.where(c // HEAD_DIM == r, 1.0, 0.0).astype(F32)


def _in_proj(x2, norm_w, w_in_b, tm):
    t = x2.shape[0]

    def body(x_ref, nw_ref, w_ref, proj_ref, hn_ref):
        @pl.when(pl.program_id(1) == 0)
        def _():
            xf = x_ref[...]
            r = lax.rsqrt(jnp.mean(xf * xf, axis=1, keepdims=True) + EPS)
            hn_ref[...] = (xf * r * nw_ref[...]).astype(BF16)

        proj_ref[...] = _dot(hn_ref[...], w_ref[...])

    return pl.pallas_call(
        body, name="in_proj",
        grid=(t // tm, N_COLBLK),
        in_specs=[pl.BlockSpec((tm, D_MODEL), lambda i, j: (i, 0)),
                  pl.BlockSpec((1, D_MODEL), lambda i, j: (0, 0)),
                  pl.BlockSpec((D_MODEL, 1024), lambda i, j: (0, j))],
        out_specs=[pl.BlockSpec((tm, 1024), lambda i, j: (i, j)),
                   pl.BlockSpec((tm, D_MODEL), lambda i, j: (i, 0))],
        out_shape=[jax.ShapeDtypeStruct((t, D_IN_PAD), F32),
                   jax.ShapeDtypeStruct((t, D_MODEL), BF16)],
        compiler_params=_params(2),
    )(x2, norm_w, w_in_b)


def _qk_prep(proj, qw2, kw2, nb, seq, tq):
    nl = seq // tq
    scale = 1.0 / math.sqrt(HEAD_DIM)

    def body(q_ref, k_ref, v_ref, qw_ref, kw_ref, qs_ref, kn_ref, vb_ref, kt_ref):
        def norm(x, w):
            r = lax.rsqrt(_pair_sum(x * x) * (1.0 / HEAD_DIM) + EPS)
            return x * r * w

        qn = norm(q_ref[...], qw_ref[...]) * scale
        kn = norm(k_ref[...], kw_ref[...])
        v = v_ref[...]
        knt = kn.T.astype(BF16)
        for a in range(2):
            sl = slice(a * HEAD_DIM, (a + 1) * HEAD_DIM)
            qs_ref[0, a] = qn[:, sl].astype(BF16)
            kn_ref[0, a] = kn[:, sl].astype(BF16)
            vb_ref[0, a] = v[:, sl].astype(BF16)
            kt_ref[0, a] = knt[sl, :]

    hm = jax.ShapeDtypeStruct((nb, N_HEADS, seq, HEAD_DIM), BF16)
    hm_spec = pl.BlockSpec((1, 2, tq, HEAD_DIM), lambda b, i, h: (b, h, i, 0))
    return pl.pallas_call(
        body, name="qk_prep",
        grid=(nb, nl, N_PAIRS),
        in_specs=[pl.BlockSpec((tq, LANES), lambda b, i, h: (b * nl + i, h)),
                  pl.BlockSpec((tq, LANES), lambda b, i, h: (b * nl + i, N_PAIRS + h)),
                  pl.BlockSpec((tq, LANES), lambda b, i, h: (b * nl + i, 2 * N_PAIRS + h)),
                  pl.BlockSpec((1, LANES), lambda b, i, h: (0, 0)),
                  pl.BlockSpec((1, LANES), lambda b, i, h: (0, 0))],
        out_specs=[hm_spec, hm_spec, hm_spec,
                   pl.BlockSpec((1, 2, HEAD_DIM, tq), lambda b, i, h: (b, h, 0, i))],
        out_shape=[hm, hm, hm, jax.ShapeDtypeStruct((nb, N_HEADS, HEAD_DIM, seq), BF16)],
        compiler_params=_params(3),
    )(proj, proj, proj, qw2, kw2)


def _attn_fwd(qs, kn, vb, nb, seq, blk):
    nq = seq // blk

    def body(q_ref, k_ref, v_ref, o_ref, tot_ref, low_ref, kmax_ref):
        qi = pl.program_id(2)
        r_i = lax.broadcasted_iota(jnp.int32, (blk, blk), 0)
        c_i = lax.broadcasted_iota(jnp.int32, (blk, blk), 1)
        csum = jnp.where(r_i >= c_i, 1.0, 0.0).astype(BF16)
        causal = c_i < r_i
        heads = range(2)

        @pl.when(qi == 0)
        def _():
            for a in heads:
                kk = k_ref[0, a].astype(F32)
                kmax_ref[a] = jnp.full((8, LANES), jnp.max(jnp.sum(kk * kk, axis=1, keepdims=True)))

        zmax = []
        for a in heads:
            qf = q_ref[0, a].astype(F32)
            qsq = jnp.sum(qf * qf, axis=1, keepdims=True)
            zmax.append(1.01 * jnp.sqrt(qsq * kmax_ref[a][0:1, 0:1]) + 0.01)

        def exhausted(run):
            top = jnp.maximum(jnp.max(run[0] + zmax[0]), jnp.max(run[1] + zmax[1]))
            return top < EXP_UNDERFLOW

        def sweep(blocks, run, acc):
            offs = [pl.multiple_of(j * blk, blk) for j, _, _ in blocks]
            z = [[_dot(q_ref[0, a], k_ref[0, a, pl.ds(off, blk), :], _NT) for a in heads]
                 for off in offs]
            cl = []
            for (_, diag, valid), zb in zip(blocks, z):
                split = []
                for a in heads:
                    lk = -_softplus(zb[a])
                    if diag:
                        lk = jnp.where(causal, lk, 0.0)
                    if valid is not None:
                        lk = jnp.where(valid, lk, 0.0)
                    split.append(_split_bf16(lk))
                cl.append([_dot(split[a][0], csum) + _dot(split[a][1], csum) for a in heads])
            for (_, diag, valid), zb, clb, off in zip(blocks, z, cl, offs):
                w = []
                for a in heads:
                    wa = jnp.exp(zb[a] + clb[a] + run[a])
                    if diag:
                        wa = jnp.where(causal, wa, 0.0)
                    if valid is not None:
                        wa = jnp.where(valid, wa, 0.0)
                    w.append(wa.astype(BF16))
                run = [run[a] + clb[a][:, 0:1] for a in heads]
                acc = [acc[a] + _dot(w[a], v_ref[0, a, pl.ds(off, blk), :]) for a in heads]
            return run, acc

        run = [jnp.zeros((blk, 1), F32)] * 2
        acc = [jnp.zeros((blk, HEAD_DIM), F32)] * 2
        run, acc = sweep([(qi, True, None), (jnp.maximum(qi - 1, 0), False, qi >= 1)], run, acc)
        low = jnp.maximum(qi - 1, 0)

        def more(carry):
            low, done, _, _ = carry
            return (low > 0) & jnp.logical_not(done)

        def pair(carry):
            low, _, run, acc = carry
            run, acc = sweep([(low - 1, False, None), (jnp.maximum(low - 2, 0), False, low >= 2)],
                             run, acc)
            return jnp.maximum(low - 2, 0), exhausted(run), run, acc

        low, _, run, acc = lax.while_loop(more, pair, (low, exhausted(run), run, acc))
        low_ref[pl.program_id(0) * N_PAIRS + pl.program_id(1), qi] = low.astype(F32)
        for a in heads:
            o_ref[:, a * HEAD_DIM:(a + 1) * HEAD_DIM] = acc[a]
            tot_ref[0, a, 0] = jnp.broadcast_to(run[a], (blk, LANES)).T[0:8, :]

    return pl.pallas_call(
        body, name="sb_attn_fwd",
        grid=(nb, N_PAIRS, nq),
        in_specs=[pl.BlockSpec((1, 2, blk, HEAD_DIM), lambda b, h, i: (b, h, i, 0)),
                  pl.BlockSpec((1, 2, seq, HEAD_DIM), lambda b, h, i: (b, h, 0, 0)),
                  pl.BlockSpec((1, 2, seq, HEAD_DIM), lambda b, h, i: (b, h, 0, 0))],
        out_specs=[pl.BlockSpec((blk, LANES), lambda b, h, i: (b * nq + i, h)),
                   pl.BlockSpec((1, 2, 1, 8, blk), lambda b, h, i: (b, h, i, 0, 0)),
                   pl.BlockSpec(memory_space=pltpu.SMEM)],
        out_shape=[jax.ShapeDtypeStruct((nb * seq, D_BRANCH), F32),
                   jax.ShapeDtypeStruct((nb, N_HEADS, nq, 8, blk), F32),
                   jax.ShapeDtypeStruct((nb * N_PAIRS, nq), F32)],
        scratch_shapes=[pltpu.VMEM((2, 8, LANES), F32)],
        compiler_params=_params(3),
    )(qs, kn, vb)


def _attn_bwd(qs, kn, kt, vb, tot, low, d_o, nb, seq, blk):
    nq = seq // blk

    def body(q_ref, k_ref, kt_ref, v_ref, tot_ref, low_ref, do_ref, dq_ref, dk_ref, dv_ref):
        qi = pl.program_id(2)

        @pl.when(qi == 0)
        def _():
            dk_ref[...] = jnp.zeros_like(dk_ref)
            dv_ref[...] = jnp.zeros_like(dv_ref)

        r_i = lax.broadcasted_iota(jnp.int32, (blk, blk), 0)
        c_i = lax.broadcasted_iota(jnp.int32, (blk, blk), 1)
        before = jnp.where(c_i < r_i, 1.0, 0.0).astype(BF16)
        upto = jnp.where(c_i <= r_i, 1.0, 0.0).astype(BF16)
        causal = r_i < c_i

        heads = range(2)
        d_ob = [do_ref[:, a * HEAD_DIM:(a + 1) * HEAD_DIM].astype(BF16) for a in heads]
        total = [tot_ref[0, a, 0][0:1, :] for a in heads]

        def sweep(blocks, lsum, esum, dqt):
            def keep(x, diag, valid):
                if diag:
                    x = jnp.where(causal, x, 0.0)
                if valid is not None:
                    x = jnp.where(valid, x, 0.0)
                return x

            offs = [pl.multiple_of(j * blk, blk) for j, _, _ in blocks]
            zt = [[_dot(k_ref[0, a, pl.ds(off, blk), :], q_ref[0, a], _NT) for a in heads]
                  for off in offs]
            dwt = [[_dot(v_ref[0, a, pl.ds(off, blk), :], d_ob[a], _NT) for a in heads]
                   for off in offs]
            sp, lk, lpre = [], [], []
            for (_, diag, valid), ztb in zip(blocks, zt):
                sp.append([_softplus(ztb[a]) for a in heads])
                lk.append([keep(-sp[-1][a], diag, valid) for a in heads])
                split = [_split_bf16(lk[-1][a]) for a in heads]
                lpre.append([_dot(before, split[a][0]) + _dot(before, split[a][1]) for a in heads])
            wt, et, epre = [], [], []
            for i, (_, diag, valid) in enumerate(blocks):
                wt.append([keep(jnp.exp(zt[i][a] + (total[a] - lsum[a] - lpre[i][a])), diag, valid)
                           for a in heads])
                et.append([dwt[i][a] * wt[i][a] for a in heads])
                split = [_split_bf16(et[i][a]) for a in heads]
                epre.append([_dot(upto, split[a][0]) + _dot(upto, split[a][1]) for a in heads])
                lsum = [lsum[a] + lpre[i][a][blk - 1:blk, :] + lk[i][a][blk - 1:blk, :]
                        for a in heads]
            for i, (_, diag, valid) in enumerate(blocks):
                dzb = [keep(et[i][a] - jnp.exp(zt[i][a] - sp[i][a]) * (esum[a] + epre[i][a]),
                            diag, valid).astype(BF16) for a in heads]
                esum = [esum[a] + epre[i][a][blk - 1:blk, :] for a in heads]
                for a in heads:
                    dk_ref[0, a, pl.ds(offs[i], blk), :] += _dot(dzb[a], q_ref[0, a])
                    dv_ref[0, a, pl.ds(offs[i], blk), :] += _dot(wt[i][a].astype(BF16), d_ob[a])
                dqt = [dqt[a] + _dot(kt_ref[0, a, :, pl.ds(offs[i], blk)], dzb[a]) for a in heads]
            return lsum, esum, dqt

        row = [jnp.zeros((1, blk), F32)] * 2
        dqt = [jnp.zeros((HEAD_DIM, blk), F32)] * 2
        low = low_ref[pl.program_id(0) * N_PAIRS + pl.program_id(1), qi].astype(jnp.int32)
        low = jnp.clip(low, 0, jnp.maximum(qi - 1, 0))

        def pair(carry):
            j, lsum, esum, dqt = carry
            return (j + 2,) + sweep([(j, False, None), (j + 1, False, j + 1 < qi - 1)],
                                    lsum, esum, dqt)

        _, lsum, esum, dqt = lax.while_loop(lambda c: c[0] < qi - 1, pair, (low, row, row, dqt))
        _, _, dqt = sweep([(jnp.maximum(qi - 1, 0), False, qi >= 1), (qi, True, None)],
                          lsum, esum, dqt)
        for a in heads:
            dq_ref[:, a * HEAD_DIM:(a + 1) * HEAD_DIM] = dqt[a].T

    hm_acc = pl.BlockSpec((1, 2, seq, HEAD_DIM), lambda b, h, i: (b, h, 0, 0))
    tok = pl.BlockSpec((blk, LANES), lambda b, h, i: (b * nq + i, h))
    hm_shape = jax.ShapeDtypeStruct((nb, N_HEADS, seq, HEAD_DIM), F32)
    return pl.pallas_call(
        body, name="sb_attn_bwd",
        grid=(nb, N_PAIRS, nq),
        in_specs=[pl.BlockSpec((1, 2, blk, HEAD_DIM), lambda b, h, i: (b, h, i, 0)),
                  hm_acc,
                  pl.BlockSpec((1, 2, HEAD_DIM, seq), lambda b, h, i: (b, h, 0, 0)),
                  hm_acc,
                  pl.BlockSpec((1, 2, 1, 8, blk), lambda b, h, i: (b, h, i, 0, 0)),
                  pl.BlockSpec(memory_space=pltpu.SMEM),
                  tok],
        out_specs=[tok, hm_acc, hm_acc],
        out_shape=[jax.ShapeDtypeStruct((nb * seq, D_BRANCH), F32), hm_shape, hm_shape],
        compiler_params=_params(3),
    )(qs, kn, kt, vb, tot, low, d_o)


def _qk_bwd(proj, dqs, dkn, dvh, qw2, kw2, nb, seq, tq):
    nl = seq // tq
    scale = 1.0 / math.sqrt(HEAD_DIM)

    def body(q_ref, k_ref, dq_ref, dk_ref, dv_ref, qw_ref, kw_ref,
             dqr_ref, dkr_ref, dvr_ref, gq_ref, gk_ref):
        @pl.when((pl.program_id(0) == 0) & (pl.program_id(1) == 0) & (pl.program_id(2) == 0))
        def _():
            gq_ref[...] = jnp.zeros_like(gq_ref)
            gk_ref[...] = jnp.zeros_like(gk_ref)

        def norm_bwd(x, w, dy):
            r = lax.rsqrt(_pair_sum(x * x) * (1.0 / HEAD_DIM) + EPS)
            xhat = x * r
            g = dy * w
            m = _pair_sum(g * xhat) * (1.0 / HEAD_DIM)
            return r * (g - xhat * m), jnp.sum(dy * xhat, axis=0, keepdims=True)

        dqr, gq = norm_bwd(q_ref[...], qw_ref[...], dq_ref[...] * scale)
        dk2 = jnp.concatenate([dk_ref[0, 0], dk_ref[0, 1]], axis=1)
        dkr, gk = norm_bwd(k_ref[...], kw_ref[...], dk2)
        dqr_ref[...] = dqr.astype(BF16)
        dkr_ref[...] = dkr.astype(BF16)
        dvr_ref[...] = jnp.concatenate([dv_ref[0, 0], dv_ref[0, 1]], axis=1).astype(BF16)
        gq_ref[...] += gq
        gk_ref[...] += gk

    tok = lambda off: pl.BlockSpec((tq, LANES), lambda b, i, h: (b * nl + i, off + h))
    hm = pl.BlockSpec((1, 2, tq, HEAD_DIM), lambda b, i, h: (b, h, i, 0))
    vec = pl.BlockSpec((1, LANES), lambda b, i, h: (0, 0))
    tshape = jax.ShapeDtypeStruct((nb * seq, D_BRANCH), BF16)
    return pl.pallas_call(
        body, name="qk_bwd",
        grid=(nb, nl, N_PAIRS),
        in_specs=[tok(0), tok(N_PAIRS), tok(0), hm, hm, vec, vec],
        out_specs=[tok(0), tok(0), tok(0), vec, vec],
        out_shape=[tshape, tshape, tshape,
                   jax.ShapeDtypeStruct((1, LANES), F32), jax.ShapeDtypeStruct((1, LANES), F32)],
        compiler_params=_params(3),
    )(proj, proj, dqs, dkn, dvh, qw2, kw2)


def _shift_down(cur, prev, k):
    if k == 0:
        return cur
    rows = _row_iota(cur.shape)
    return jnp.where(rows < k, pltpu.roll(prev, k, axis=0), pltpu.roll(cur, k, axis=0))


def _shift_up(cur, nxt, k):
    if k == 0:
        return cur
    n = cur.shape[0]
    rows = _row_iota(cur.shape)
    return jnp.where(rows < n - k, pltpu.roll(cur, n - k, axis=0), pltpu.roll(nxt, n - k, axis=0))


def _conv_pre(cur, prev, w, b):
    out = b
    for i in range(CONV_TAPS):
        out = out + _shift_down(cur, prev, CONV_TAPS - 1 - i) * w[i:i + 1, :]
    return out


def _silu(x):
    return x * _sigmoid(x)


def _silu_grad(x):
    s = _sigmoid(x)
    return s * (1.0 + x * (1.0 - s))


def _chunk_decay(dt_raw, dtb, alog, expand, qc):
    dt = _softplus(dt_raw + dtb)
    d_a = dt * (-jnp.exp(alog))
    r_i = lax.broadcasted_iota(jnp.int32, (qc, qc), 0)
    c_i = lax.broadcasted_iota(jnp.int32, (qc, qc), 1)
    tril = r_i >= c_i
    a_cs = _dot(jnp.where(tril, 1.0, 0.0).astype(F32), d_a, precision=HIGHEST)
    dt_x = _dot(dt, expand, precision=HIGHEST)
    acs_x = _dot(a_cs, expand, precision=HIGHEST)
    return dt, d_a, a_cs, dt_x, acs_x, tril


def _ssd_fwd(proj, conv_w, conv_b, dtb, alog, dskip, nb, seq):
    qc = SSD_CHUNK
    nc = seq // qc

    def body(xs_ref, bc_ref, dt_ref, cw_ref, cb_ref, dtb_ref, al_ref, ds_ref,
             y_ref, st_ref, pxs_ref, pbc_ref, state_ref):
        @pl.when(pl.program_id(1) == 0)
        def _():
            pxs_ref[...] = jnp.zeros_like(pxs_ref)
            pbc_ref[...] = jnp.zeros_like(pbc_ref)
            state_ref[...] = jnp.zeros_like(state_ref)

        expand = _head_expand()
        xs_raw = xs_ref[...]
        bc_raw = bc_ref[...]
        cw = cw_ref[...]
        cb = cb_ref[...]
        xs = _silu(_conv_pre(xs_raw, pxs_ref[...], cw[:, :D_BRANCH], cb[:, :D_BRANCH]))
        bc = _silu(_conv_pre(bc_raw, pbc_ref[...], cw[:, D_BRANCH:], cb[:, D_BRANCH:]))
        pxs_ref[...] = xs_raw
        pbc_ref[...] = bc_raw

        dt, d_a, a_cs, dt_x, acs_x, tril = _chunk_decay(
            dt_ref[...], dtb_ref[...], al_ref[...], expand, qc)
        a_cst = a_cs.T
        aend_x = acs_x[qc - 1:qc, :]
        ea_x = jnp.exp(acs_x)
        dec_x = jnp.exp(aend_x - acs_x)
        xt = xs * dt_x
        xtb = xt.astype(BF16)
        xdb = (xt * dec_x).astype(BF16)
        d_x = _dot(jnp.broadcast_to(ds_ref[...], (8, LANES)), expand, precision=HIGHEST)[0:1, :]
        st_ref[0, 0] = state_ref[...]

        for g in range(N_GROUPS):
            gs = slice(g * GROUP_W, (g + 1) * GROUP_W)
            bg = bc[:, g * D_STATE:(g + 1) * D_STATE]
            cg = bc[:, (N_GROUPS + g) * D_STATE:(N_GROUPS + g + 1) * D_STATE]
            bgb = bg.astype(BF16)
            cgb = cg.astype(BF16)
            cbm = _dot(cgb, bgb, _NT)
            st_in = state_ref[g]
            y_off = _dot(cgb, st_in.astype(BF16)) * ea_x[:, gs]
            for k in range(HEADS_PER_GROUP):
                h = g * HEADS_PER_GROUP + k
                hs = slice(h * HEAD_DIM, (h + 1) * HEAD_DIM)
                seg = a_cs[:, h:h + 1] - a_cst[h:h + 1, :]
                gh = cbm * jnp.exp(jnp.where(tril, seg, -1e30))
                y_h = _dot(gh.astype(BF16), xtb[:, hs]) + y_off[:, k * HEAD_DIM:(k + 1) * HEAD_DIM]
                y_ref[:, hs] = y_h + d_x[:, hs] * xs[:, hs]
            state_ref[g] = st_in * jnp.exp(aend_x[:, gs]) + _dot(bg.T.astype(BF16), xdb[:, gs])

    nblk = lambda w, off: pl.BlockSpec((qc, w), lambda b, c: (b * nc + c, off))
    full = lambda r, w: pl.BlockSpec((r, w), lambda b, c: (0, 0))
    return pl.pallas_call(
        body, name="ssd_fwd",
        grid=(nb, nc),
        in_specs=[nblk(D_BRANCH, COL_XS // D_BRANCH), nblk(D_BC, COL_BC // D_BC),
                  nblk(LANES, COL_DT // LANES),
                  full(CONV_TAPS, D_CONV), full(1, D_CONV), full(1, LANES), full(1, LANES),
                  full(1, LANES)],
        out_specs=[pl.BlockSpec((qc, D_BRANCH), lambda b, c: (b * nc + c, 0)),
                   pl.BlockSpec((1, 1, N_GROUPS, D_STATE, GROUP_W), lambda b, c: (b, c, 0, 0, 0))],
        out_shape=[jax.ShapeDtypeStruct((nb * seq, D_BRANCH), F32),
                   jax.ShapeDtypeStruct((nb, nc, N_GROUPS, D_STATE, GROUP_W), F32)],
        scratch_shapes=[pltpu.VMEM((qc, D_BRANCH), F32), pltpu.VMEM((qc, D_BC), F32),
                        pltpu.VMEM((N_GROUPS, D_STATE, GROUP_W), F32)],
        compiler_params=_params(2),
    )(proj, proj, proj, conv_w, conv_b, dtb, alog, dskip)


def _ssd_bwd(proj, d_y, states, conv_w, conv_b, dtb, alog, dskip, nb, seq):
    qc = SSD_CHUNK
    nc = seq // qc

    def body(xs_ref, bc_ref, dt_ref, pxs_ref, pbc_ref, dy_ref, st_ref, stn_ref,
             cw_ref, cb_ref, dtb_ref, al_ref, ds_ref,
             dx_ref, gcw_ref, gcb_ref, gdtb_ref, gal_ref, gds_ref,
             dst_ref, nxs_ref, nbc_ref, yd_ref, dxt_ref):
        step = pl.program_id(1)
        chunk = nc - 1 - step

        @pl.when(step == 0)
        def _():
            dst_ref[...] = jnp.zeros_like(dst_ref)
            nxs_ref[...] = jnp.zeros_like(nxs_ref)
            nbc_ref[...] = jnp.zeros_like(nbc_ref)

        @pl.when((pl.program_id(0) == 0) & (step == 0))
        def _():
            gcw_ref[...] = jnp.zeros_like(gcw_ref)
            gcb_ref[...] = jnp.zeros_like(gcb_ref)
            gdtb_ref[...] = jnp.zeros_like(gdtb_ref)
            gal_ref[...] = jnp.zeros_like(gal_ref)
            gds_ref[...] = jnp.zeros_like(gds_ref)

        expand = _head_expand()
        collapse = lambda v: _dot(v, expand, _NT, precision=HIGHEST)
        first = jnp.where(chunk == 0, 0.0, 1.0)
        xs_raw = xs_ref[...]
        bc_raw = bc_ref[...]
        pxs = pxs_ref[...] * first
        pbc = pbc_ref[...] * first
        cw = cw_ref[...]
        cb = cb_ref[...]
        pre_xs = _conv_pre(xs_raw, pxs, cw[:, :D_BRANCH], cb[:, :D_BRANCH])
        pre_bc = _conv_pre(bc_raw, pbc, cw[:, D_BRANCH:], cb[:, D_BRANCH:])
        xs = _silu(pre_xs)
        bc = _silu(pre_bc)

        dt_in = dt_ref[...] + dtb_ref[...]
        dt, d_a, a_cs, dt_x, acs_x, tril = _chunk_decay(
            dt_ref[...], dtb_ref[...], al_ref[...], expand, qc)
        a_cst = a_cs.T
        aend_x = acs_x[qc - 1:qc, :]
        ea_x = jnp.exp(acs_x)
        dec_x = jnp.exp(aend_x - acs_x)
        xt = xs * dt_x
        xtb = xt.astype(BF16)
        xdb = (xt * dec_x).astype(BF16)
        d_x = _dot(jnp.broadcast_to(ds_ref[...], (8, LANES)), expand, precision=HIGHEST)[0:1, :]

        dy = dy_ref[...]
        dyb = dy.astype(BF16)
        dyeab = (dy * ea_x).astype(BF16)
        gds_ref[...] += collapse(jnp.broadcast_to(jnp.sum(dy * xs, axis=0, keepdims=True),
                                                  (8, D_BRANCH)))[0:1, :]

        d_bc = []
        d_cc = []
        y_offs = []
        dxt_states = []
        end_terms = []
        for g in range(N_GROUPS):
            gs = slice(g * GROUP_W, (g + 1) * GROUP_W)
            bg = bc[:, g * D_STATE:(g + 1) * D_STATE]
            cg = bc[:, (N_GROUPS + g) * D_STATE:(N_GROUPS + g + 1) * D_STATE]
            bgb = bg.astype(BF16)
            cgb = cg.astype(BF16)
            cbm = _dot(cgb, bgb, _NT)
            st_in = st_ref[0, 0, g]
            st_inb = st_in.astype(BF16)
            d_st = dst_ref[g]
            d_stb = d_st.astype(BF16)
            y_offs.append(_dot(cgb, st_inb) * ea_x[:, gs])
            dxt_states.append(_dot(bgb, d_stb) * dec_x[:, gs])
            d_c = _dot(dyeab[:, gs], st_inb, _NT)
            d_b = _dot(xdb[:, gs], d_stb, _NT)
            d_cb = jnp.zeros((qc, qc), F32)
            for k in range(HEADS_PER_GROUP):
                h = g * HEADS_PER_GROUP + k
                hs = slice(h * HEAD_DIM, (h + 1) * HEAD_DIM)
                seg = a_cs[:, h:h + 1] - a_cst[h:h + 1, :]
                lh = jnp.exp(jnp.where(tril, seg, -1e30))
                ghb = (cbm * lh).astype(BF16)
                d_cb = d_cb + _dot(dyb[:, hs], xtb[:, hs], _NT) * lh
                yd_ref[:, hs] = _dot(ghb, xtb[:, hs])
                dxt_ref[:, hs] = _dot(ghb, dyb[:, hs], _TN)
            d_cbb = d_cb.astype(BF16)
            d_cc.append(d_c + _dot(d_cbb, bgb))
            d_bc.append(d_b + _dot(d_cbb, cgb, _TN))
            end_terms.append(jnp.sum(d_st * stn_ref[0, 0, g], axis=0, keepdims=True))
            dst_ref[g] = d_st * jnp.exp(aend_x[:, gs]) + _dot(cg.T.astype(BF16), dyeab[:, gs])

        y_off = jnp.concatenate(y_offs, axis=1)
        dxt_state = jnp.concatenate(dxt_states, axis=1)
        dxt = dxt_ref[...] + dxt_state
        last = jnp.where(chunk == nc - 1, 0.0, 1.0)
        end_c = collapse(jnp.broadcast_to(jnp.concatenate(end_terms, axis=1), (8, D_BRANCH)))[0:1, :]
        da_cs = collapse(dyb.astype(F32) * yd_ref[...] - dxt_ref[...] * xtb.astype(F32)
                         + dy * y_off - dxt_state * xt)
        da_cs = da_cs + jnp.where(_row_iota(da_cs.shape) == qc - 1, end_c * last, 0.0)
        triu = lax.broadcasted_iota(jnp.int32, (qc, qc), 0) <= lax.broadcasted_iota(jnp.int32, (qc, qc), 1)
        dd_a = _dot(jnp.where(triu, 1.0, 0.0).astype(F32), da_cs, precision=HIGHEST)
        ddt = dd_a * (-jnp.exp(al_ref[...])) + collapse(dxt * xs)
        head_lanes = _lane_iota(ddt.shape) < N_HEADS
        ddt_raw = jnp.where(head_lanes, ddt * _sigmoid(dt_in), 0.0)
        gal_ref[...] += jnp.sum(jnp.where(head_lanes, dd_a * d_a, 0.0), axis=0, keepdims=True)
        gdtb_ref[...] += jnp.sum(ddt_raw, axis=0, keepdims=True)

        dpre_xs = (dxt * dt_x + d_x * dy) * _silu_grad(pre_xs)
        dpre_bc = jnp.concatenate(d_bc + d_cc, axis=1) * _silu_grad(pre_bc)
        gcb_ref[...] += jnp.concatenate([jnp.sum(dpre_xs, axis=0, keepdims=True),
                                         jnp.sum(dpre_bc, axis=0, keepdims=True)], axis=1)
        nxs = nxs_ref[...]
        nbc = nbc_ref[...]
        du_xs = jnp.zeros_like(dpre_xs)
        du_bc = jnp.zeros_like(dpre_bc)
        for i in range(CONV_TAPS):
            k = CONV_TAPS - 1 - i
            gcw_ref[i:i + 1, :] += jnp.concatenate(
                [jnp.sum(dpre_xs * _shift_down(xs_raw, pxs, k), axis=0, keepdims=True),
                 jnp.sum(dpre_bc * _shift_down(bc_raw, pbc, k), axis=0, keepdims=True)], axis=1)
            du_xs = du_xs + _shift_up(dpre_xs, nxs, k) * cw[i:i + 1, :D_BRANCH]
            du_bc = du_bc + _shift_up(dpre_bc, nbc, k) * cw[i:i + 1, D_BRANCH:]
        nxs_ref[...] = dpre_xs
        nbc_ref[...] = dpre_bc

        dx_ref[:, :D_BRANCH] = du_xs.astype(BF16)
        dx_ref[:, D_BRANCH:D_CONV] = du_bc.astype(BF16)
        dx_ref[:, D_CONV:D_CONV + LANES] = ddt_raw.astype(BF16)
        dx_ref[:, D_CONV + LANES:] = jnp.zeros((qc, 2048 - D_CONV - LANES), BF16)

    rev = lambda b, c: b * nc + (nc - 1 - c)
    prv = lambda b, c: b * nc + jnp.maximum(nc - 2 - c, 0)
    nblk = lambda w, off, f: pl.BlockSpec((qc, w), lambda b, c: (f(b, c), off))
    full = lambda r, w: pl.BlockSpec((r, w), lambda b, c: (0, 0))
    st_spec = lambda f: pl.BlockSpec((1, 1, N_GROUPS, D_STATE, GROUP_W),
                                     lambda b, c: (b, f(c), 0, 0, 0))
    return pl.pallas_call(
        body, name="ssd_bwd",
        grid=(nb, nc),
        in_specs=[nblk(D_BRANCH, COL_XS // D_BRANCH, rev), nblk(D_BC, COL_BC // D_BC, rev),
                  nblk(LANES, COL_DT // LANES, rev),
                  nblk(D_BRANCH, COL_XS // D_BRANCH, prv), nblk(D_BC, COL_BC // D_BC, prv),
                  nblk(D_BRANCH, 0, rev),
                  st_spec(lambda c: nc - 1 - c), st_spec(lambda c: jnp.minimum(nc - c, nc - 1)),
                  full(CONV_TAPS, D_CONV), full(1, D_CONV), full(1, LANES), full(1, LANES),
                  full(1, LANES)],
        out_specs=[nblk(2048, 0, rev), full(8, D_CONV), full(1, D_CONV), full(1, LANES),
                   full(1, LANES), full(1, LANES)],
        out_shape=[jax.ShapeDtypeStruct((nb * seq, 2048), BF16),
                   jax.ShapeDtypeStruct((8, D_CONV), F32), jax.ShapeDtypeStruct((1, D_CONV), F32),
                   jax.ShapeDtypeStruct((1, LANES), F32), jax.ShapeDtypeStruct((1, LANES), F32),
                   jax.ShapeDtypeStruct((1, LANES), F32)],
        scratch_shapes=[pltpu.VMEM((N_GROUPS, D_STATE, GROUP_W), F32),
                        pltpu.VMEM((qc, D_BRANCH), F32), pltpu.VMEM((qc, D_BC), F32),
                        pltpu.VMEM((qc, D_BRANCH), F32), pltpu.VMEM((qc, D_BRANCH), F32)],
        compiler_params=_params(2),
    )(proj, proj, proj, proj, proj, d_y, states, states, conv_w, conv_b, dtb, alog, dskip)


def _mid(o_sb, y_ssd, proj, x2, target, sb_w, ssd_w, w_out_b, tm):
    t = x2.shape[0]
    inv_d = 1.0 / D_MODEL

    def body(o_ref, y_ref, zsb_ref, zssd_ref, x_ref, tg_ref, sbw_ref, ssdw_ref, w_ref,
             dout_ref, dosb_ref, dy_ref, dz_ref, gw_ref, gsb_ref, gssd_ref, loss_ref):
        @pl.when(pl.program_id(0) == 0)
        def _():
            gw_ref[...] = jnp.zeros_like(gw_ref)
            gsb_ref[...] = jnp.zeros_like(gsb_ref)
            gssd_ref[...] = jnp.zeros_like(gssd_ref)
            loss_ref[...] = jnp.zeros_like(loss_ref)

        def branch(val, z, w):
            gate = _silu(z)
            g = val * gate
            r = lax.rsqrt(jnp.mean(g * g, axis=1, keepdims=True) + EPS)
            xhat = g * r
            return gate, r, xhat, (xhat * w).astype(BF16)

        o = o_ref[...]
        y = y_ref[...]
        z_sb = zsb_ref[...]
        z_ssd = zssd_ref[...]
        gate_a, r_a, xhat_a, mix_a = branch(o, z_sb, sbw_ref[...])
        gate_b, r_b, xhat_b, mix_b = branch(y, z_ssd, ssdw_ref[...])
        out = x_ref[...] + _dot(mix_a, w_ref[:D_BRANCH, :]) + _dot(mix_b, w_ref[D_BRANCH:, :])
        diff = out - tg_ref[...]
        loss_ref[...] += 0.5 * inv_d * jnp.sum(diff * diff)
        d_out = diff * inv_d
        dout_ref[...] = d_out
        d_outb = d_out.astype(BF16)
        gw_ref[:D_BRANCH, :] += _dot(mix_a, d_outb, _TN)
        gw_ref[D_BRANCH:, :] += _dot(mix_b, d_outb, _TN)

        def branch_bwd(dmix, val, z, w, gate, r, xhat):
            gg = dmix * w
            m = jnp.mean(gg * xhat, axis=1, keepdims=True)
            dg = r * (gg - xhat * m)
            return dg * gate, dg * val * _silu_grad(z), jnp.sum(dmix * xhat, axis=0, keepdims=True)

        dmix_a = _dot(d_outb, w_ref[:D_BRANCH, :], _NT)
        dmix_b = _dot(d_outb, w_ref[D_BRANCH:, :], _NT)
        d_o, dz_a, gsb = branch_bwd(dmix_a, o, z_sb, sbw_ref[...], gate_a, r_a, xhat_a)
        d_y, dz_b, gssd = branch_bwd(dmix_b, y, z_ssd, ssdw_ref[...], gate_b, r_b, xhat_b)
        dosb_ref[...] = d_o
        dy_ref[...] = d_y
        dz_ref[:, :D_BRANCH] = dz_a.astype(BF16)
        dz_ref[:, D_BRANCH:] = dz_b.astype(BF16)
        gsb_ref[...] += gsb
        gssd_ref[...] += gssd

    row = lambda w, off: pl.BlockSpec((tm, w), lambda i: (i, off))
    full = lambda r, w: pl.BlockSpec((r, w), lambda i: (0, 0))
    resident = pl.BlockSpec((2 * D_BRANCH, D_MODEL), lambda i: (0, 0), pipeline_mode=pl.Buffered(1))
    tok = jax.ShapeDtypeStruct((t, D_MODEL), F32)
    return pl.pallas_call(
        body, name="mid",
        grid=(t // tm,),
        in_specs=[row(D_BRANCH, 0), row(D_BRANCH, 0), row(D_BRANCH, 3), row(D_BRANCH, 4),
                  row(D_MODEL, 0), row(D_MODEL, 0), full(1, D_BRANCH), full(1, D_BRANCH),
                  resident],
        out_specs=[row(D_MODEL, 0), row(D_BRANCH, 0), row(D_BRANCH, 0), row(2 * D_BRANCH, 0),
                   resident, full(1, D_BRANCH), full(1, D_BRANCH),
                   full(1, LANES)],
        out_shape=[tok, tok, tok, jax.ShapeDtypeStruct((t, 2 * D_BRANCH), BF16),
                   jax.ShapeDtypeStruct((2 * D_BRANCH, D_MODEL), F32),
                   jax.ShapeDtypeStruct((1, D_BRANCH), F32), jax.ShapeDtypeStruct((1, D_BRANCH), F32),
                   jax.ShapeDtypeStruct((1, LANES), F32)],
        compiler_params=_params(1),
    )(o_sb, y_ssd, proj, proj, x2, target, sb_w, ssd_w, w_out_b)


_DPROJ_FIRST = (0, 1, 2, 3, 5)
_DPROJ_BLOCKS = (1, 1, 1, 2, 2)
_DPROJ_OWNER = (0, 1, 2, 3, 3, 4, 4)


def _dproj_col(j, p):
    return jnp.clip(j - _DPROJ_FIRST[p], 0, _DPROJ_BLOCKS[p] - 1)


def _in_proj_bwd_x(d_parts, w_in_b, x2, d_out, norm_w, tm):
    t = x2.shape[0]
    n_parts = len(d_parts)

    def body(*refs):
        dp_refs = refs[:n_parts]
        w_ref, x_ref, dout_ref, nw_ref, gx_ref, gnw_ref, acc_ref = refs[n_parts:]
        j = pl.program_id(1)

        @pl.when((pl.program_id(0) == 0) & (j == 0))
        def _():
            gnw_ref[...] = jnp.zeros_like(gnw_ref)

        @pl.when(j == 0)
        def _():
            acc_ref[...] = jnp.zeros_like(acc_ref)

        for jj in range(N_COLBLK):
            @pl.when(j == jj)
            def _(jj=jj):
                acc_ref[...] += _dot(dp_refs[_DPROJ_OWNER[jj]][...], w_ref[...], _NT)

        @pl.when(j == N_COLBLK - 1)
        def _():
            xf = x_ref[...]
            d_hn = acc_ref[...]
            r = lax.rsqrt(jnp.mean(xf * xf, axis=1, keepdims=True) + EPS)
            xhat = xf * r
            g = d_hn * nw_ref[...]
            m = jnp.mean(g * xhat, axis=1, keepdims=True)
            gx_ref[...] = dout_ref[...] + r * (g - xhat * m)
            gnw_ref[...] += jnp.sum(d_hn * xhat, axis=0, keepdims=True)

    return pl.pallas_call(
        body, name="in_proj_bwd_x",
        grid=(t // tm, N_COLBLK),
        in_specs=[pl.BlockSpec((tm, 1024), lambda i, j, p=p: (i, _dproj_col(j, p)))
                  for p in range(n_parts)] + [
                  pl.BlockSpec((D_MODEL, 1024), lambda i, j: (0, j)),
                  pl.BlockSpec((tm, D_MODEL), lambda i, j: (i, 0)),
                  pl.BlockSpec((tm, D_MODEL), lambda i, j: (i, 0)),
                  pl.BlockSpec((1, D_MODEL), lambda i, j: (0, 0))],
        out_specs=[pl.BlockSpec((tm, D_MODEL), lambda i, j: (i, 0)),
                   pl.BlockSpec((1, D_MODEL), lambda i, j: (0, 0))],
        out_shape=[jax.ShapeDtypeStruct((t, D_MODEL), F32), jax.ShapeDtypeStruct((1, D_MODEL), F32)],
        scratch_shapes=[pltpu.VMEM((tm, D_MODEL), F32)],
        compiler_params=_params(2),
    )(*d_parts, w_in_b, x2, d_out, norm_w)


def _in_proj_bwd_w(hn, d_parts, tm):
    t = hn.shape[0]
    n_parts = len(d_parts)

    def body(hn_ref, *refs):
        dp_refs, gw_ref = refs[:n_parts], refs[n_parts]
        j = pl.program_id(0)

        @pl.when(pl.program_id(1) == 0)
        def _():
            gw_ref[...] = jnp.zeros_like(gw_ref)

        for jj in range(N_COLBLK):
            @pl.when(j == jj)
            def _(jj=jj):
                gw_ref[...] += _dot(hn_ref[...], dp_refs[_DPROJ_OWNER[jj]][...], _TN)

    def part_spec(p):
        def index(j, i):
            mine = (j >= _DPROJ_FIRST[p]) & (j < _DPROJ_FIRST[p] + _DPROJ_BLOCKS[p])
            return jnp.where(mine, i, 0), _dproj_col(j, p)
        return pl.BlockSpec((tm, 1024), index)

    return pl.pallas_call(
        body, name="in_proj_bwd_w",
        grid=(N_COLBLK, t // tm),
        in_specs=[pl.BlockSpec((tm, D_MODEL), lambda j, i: (i, 0))]
                 + [part_spec(p) for p in range(n_parts)],
        out_specs=pl.BlockSpec((D_MODEL, 1024), lambda j, i: (0, j)),
        out_shape=jax.ShapeDtypeStruct((D_MODEL, D_IN_PAD), F32),
        compiler_params=_params(2),
    )(hn, *d_parts)


def _adamw(parts, w, m, v, tr, name):
    rows, cols = w.shape
    c1 = 1.0 - ADAM_B1 ** ADAM_STEP
    c2 = 1.0 - ADAM_B2 ** ADAM_STEP

    def body(p_ref, w_ref, m_ref, v_ref, g_ref, d_ref, nm_ref, nv_ref):
        g = p_ref[0].astype(F32)
        for s in range(1, N_DEV):
            g = g + p_ref[s].astype(F32)
        nm = ADAM_B1 * m_ref[...] + (1.0 - ADAM_B1) * g
        nv = ADAM_B2 * v_ref[...] + (1.0 - ADAM_B2) * (g * g)
        g_ref[...] = g
        nm_ref[...] = nm
        nv_ref[...] = nv
        d_ref[...] = -ADAM_LR * ((nm / c1) / (jnp.sqrt(nv / c2) + ADAM_EPS) + ADAM_WD * w_ref[...])

    blk = pl.BlockSpec((tr, cols), lambda i: (i, 0))
    shape = jax.ShapeDtypeStruct((rows, cols), F32)
    return pl.pallas_call(
        body, name=name,
        grid=(rows // tr,),
        in_specs=[pl.BlockSpec((N_DEV, tr, cols), lambda i: (0, i, 0)), blk, blk, blk],
        out_specs=[blk, blk, blk, blk],
        out_shape=[shape, shape, shape, shape],
        compiler_params=_params(1),
    )(parts, w, m, v)


def _mesh_place():
    x, y, c = lax.axis_index("x"), lax.axis_index("y"), lax.axis_index("c")
    return x, y, c, 4 * x + 2 * y + c


def _peer(x, y, c, k):
    px = 1 - x if k & 4 else x
    py = 1 - y if k & 2 else y
    pc = 1 - c if k & 1 else c
    return (px, py, pc), 4 * px + 2 * py + pc


def _exchange(srcs, scatter, name):
    n = len(srcs)

    def body(*refs):
        src_refs, dst_refs = refs[:n], refs[n:2 * n]
        send_sems, recv_sems, loc_sems = refs[2 * n:]
        x, y, c, me = _mesh_place()

        def src_of(i, idx):
            return src_refs[i].at[idx] if scatter[i] else src_refs[i]

        local = [pltpu.make_async_copy(src_of(i, me), dst_refs[i].at[me], loc_sems.at[i])
                 for i in range(n)]
        for cp in local:
            cp.start()
        sends = []
        for k in range(1, N_DEV):
            peer, pidx = _peer(x, y, c, k)
            for i in range(n):
                s = i * (N_DEV - 1) + k - 1
                cp = pltpu.make_async_remote_copy(
                    src_ref=src_of(i, pidx), dst_ref=dst_refs[i].at[me],
                    send_sem=send_sems.at[s], recv_sem=recv_sems.at[s],
                    device_id=peer, device_id_type=pl.DeviceIdType.MESH)
                cp.start()
                sends.append(cp)
        for k in range(1, N_DEV):
            peer, pidx = _peer(x, y, c, k)
            for i in range(n):
                s = i * (N_DEV - 1) + k - 1
                pltpu.make_async_remote_copy(
                    src_ref=src_of(i, pidx), dst_ref=dst_refs[i].at[pidx],
                    send_sem=send_sems.at[s], recv_sem=recv_sems.at[s],
                    device_id=peer, device_id_type=pl.DeviceIdType.MESH).wait_recv()
        for cp in sends:
            cp.wait_send()
        for cp in local:
            cp.wait()

    out_shape = [jax.ShapeDtypeStruct(s.shape if sc else (N_DEV,) + s.shape, s.dtype)
                 for s, sc in zip(srcs, scatter)]
    any_spec = pl.BlockSpec(memory_space=pl.ANY)
    return pl.pallas_call(
        body, name=name,
        in_specs=[any_spec] * n, out_specs=[any_spec] * n, out_shape=out_shape,
        scratch_shapes=[pltpu.SemaphoreType.DMA((n * (N_DEV - 1),)),
                        pltpu.SemaphoreType.DMA((n * (N_DEV - 1),)),
                        pltpu.SemaphoreType.DMA((n,))],
    )(*srcs)


def _pad_lanes(v, width=LANES):
    return jnp.pad(v, ((0, 0), (0, width - v.shape[1])))


def _local_step(x, target, norm_w, w_in_b, q_norm_w, k_norm_w, conv_w, conv_b, dt_bias, a_log,
                d_skip, sb_norm_w, ssd_norm_w, w_out_b, tm=512, tq=512, tmid=256, blk=ATT_BLK):
    nb, seq, _ = x.shape
    t = nb * seq
    x2 = x.reshape(t, D_MODEL)
    tg2 = target.reshape(t, D_MODEL)
    qw2 = jnp.tile(q_norm_w, (1, 2))
    kw2 = jnp.tile(k_norm_w, (1, 2))
    dtb, alog, dsk = _pad_lanes(dt_bias), _pad_lanes(a_log), _pad_lanes(d_skip)

    tproj = min(2 * tm, t)
    proj, hn = _in_proj(x2, norm_w, w_in_b, tproj)
    qs, kn, vb, kt = _qk_prep(proj, qw2, kw2, nb, seq, tq)
    o_sb, sb_tot, sb_low = _attn_fwd(qs, kn, vb, nb, seq, blk)
    y_ssd, states = _ssd_fwd(proj, conv_w, conv_b, dtb, alog, dsk, nb, seq)
    d_out, d_osb, d_y, d_z, g_wout, g_sbw, g_ssdw, loss = _mid(
        o_sb, y_ssd, proj, x2, tg2, sb_norm_w, ssd_norm_w, w_out_b, tmid)
    dqs, dkn, dvh = _attn_bwd(qs, kn, kt, vb, sb_tot, sb_low, d_osb, nb, seq, blk)
    dq_raw, dk_raw, dv_raw, g_qw, g_kw = _qk_bwd(proj, dqs, dkn, dvh, qw2, kw2, nb, seq, tq)
    d_xbc, g_cw, g_cb, g_dtb, g_alog, g_dsk = _ssd_bwd(
        proj, d_y, states, conv_w, conv_b, dtb, alog, dsk, nb, seq)
    d_parts = [dq_raw, dk_raw, dv_raw, d_z, d_xbc]
    grad_x, g_nw = _in_proj_bwd_x(d_parts, w_in_b, x2, d_out, norm_w, tm)
    g_win = _in_proj_bwd_w(hn, d_parts, tm)

    small = dict(
        norm_w=g_nw,
        q_norm_w=g_qw[:, :HEAD_DIM] + g_qw[:, HEAD_DIM:],
        k_norm_w=g_kw[:, :HEAD_DIM] + g_kw[:, HEAD_DIM:],
        conv_b=g_cb, dt_bias=g_dtb[:, :N_HEADS], A_log=g_alog[:, :N_HEADS],
        D_skip=g_dsk[:, :N_HEADS], sb_norm_w=g_sbw, ssd_norm_w=g_ssdw)
    return (loss[0, 0], grad_x.reshape(nb, seq, D_MODEL), g_win[:, :D_IN], g_wout,
            g_cw[:CONV_TAPS], small)


_SMALL = ("norm_w", "q_norm_w", "k_norm_w", "conv_b", "dt_bias", "A_log", "D_skip",
          "sb_norm_w", "ssd_norm_w")


def _pack_small(vals):
    rows = [_pad_lanes(vals[n], -(-vals[n].shape[1] // LANES) * LANES).reshape(-1, LANES)
            for n in _SMALL]
    packed = jnp.concatenate(rows, axis=0)
    return jnp.pad(packed, ((0, 48 - packed.shape[0]), (0, 0)))


def _unpack_small(packed, like):
    out, r = {}, 0
    for n in _SMALL:
        width = like[n].shape[1]
        nr = -(-width // LANES)
        out[n] = packed[r:r + nr].reshape(1, nr * LANES)[:, :width]
        r += nr
    return out


def kernel(x, norm_w, w_in, q_norm_w, k_norm_w, conv_w, conv_b, dt_bias, A_log, D_skip, sb_norm_w, ssd_norm_w, w_out, loss_target, m_norm_w, m_w_in, m_q_norm_w, m_k_norm_w, m_conv_w, m_conv_b, m_dt_bias, m_A_log, m_D_skip, m_sb_norm_w, m_ssd_norm_w, m_w_out, v_norm_w, v_w_in, v_q_norm_w, v_k_norm_w, v_conv_w, v_conv_b, v_dt_bias, v_A_log, v_D_skip, v_sb_norm_w, v_ssd_norm_w, v_w_out):
    w_sh = D_IN // N_DEV
    c_sh = D_CONV // N_DEV

    win_all, wout_all, cw_all = _exchange(
        [w_in[0].astype(BF16), w_out[0].astype(BF16), conv_w[0]], [False, False, False],
        "gather_weights")
    w_in_b = jnp.pad(jnp.transpose(win_all, (1, 0, 2)).reshape(D_MODEL, D_IN),
                     ((0, 0), (0, D_IN_PAD - D_IN)))
    w_out_b = wout_all.reshape(2 * D_BRANCH, D_MODEL)
    conv_full = jnp.transpose(cw_all, (1, 0, 2)).reshape(CONV_TAPS, D_CONV)

    loss, grad_x, g_win, g_wout, g_cw, g_small = _local_step(
        x, loss_target, norm_w, w_in_b, q_norm_w, k_norm_w, conv_full, conv_b, dt_bias, A_log,
        D_skip, sb_norm_w, ssd_norm_w, w_out_b)

    win_parts, wout_parts, cw_parts, small_parts = _exchange(
        [jnp.transpose(g_win.reshape(D_MODEL, N_DEV, w_sh), (1, 0, 2)).astype(BF16),
         g_wout.reshape(N_DEV, 2 * D_BRANCH // N_DEV, D_MODEL).astype(BF16),
         jnp.pad(jnp.transpose(g_cw.reshape(CONV_TAPS, N_DEV, c_sh), (1, 0, 2)),
                 ((0, 0), (0, 8 - CONV_TAPS), (0, 0))),
         _pack_small(g_small)],
        [True, True, True, False], "scatter_grads")

    small_w = dict(norm_w=norm_w, q_norm_w=q_norm_w, k_norm_w=k_norm_w, conv_b=conv_b,
                   dt_bias=dt_bias, A_log=A_log, D_skip=D_skip, sb_norm_w=sb_norm_w,
                   ssd_norm_w=ssd_norm_w)
    small_m = dict(norm_w=m_norm_w, q_norm_w=m_q_norm_w, k_norm_w=m_k_norm_w, conv_b=m_conv_b,
                   dt_bias=m_dt_bias, A_log=m_A_log, D_skip=m_D_skip, sb_norm_w=m_sb_norm_w,
                   ssd_norm_w=m_ssd_norm_w)
    small_v = dict(norm_w=v_norm_w, q_norm_w=v_q_norm_w, k_norm_w=v_k_norm_w, conv_b=v_conv_b,
                   dt_bias=v_dt_bias, A_log=v_A_log, D_skip=v_D_skip, sb_norm_w=v_sb_norm_w,
                   ssd_norm_w=v_ssd_norm_w)

    pad8 = lambda a: jnp.pad(a, ((0, 8 - CONV_TAPS), (0, 0)))
    r_win = _adamw(win_parts, w_in[0], m_w_in[0], v_w_in[0], 128, "adamw_w_in")
    r_wout = _adamw(wout_parts, w_out[0], m_w_out[0], v_w_out[0], 128, "adamw_w_out")
    r_cw = _adamw(cw_parts, pad8(conv_w[0]), pad8(m_conv_w[0]), pad8(v_conv_w[0]), 8, "adamw_conv_w")
    r_small = _adamw(small_parts, _pack_small(small_w), _pack_small(small_m),
                     _pack_small(small_v), 48, "adamw_small")

    loss = lax.psum(loss, ("x", "y", "c"))
    res = {"w_in": [a[None] for a in r_win], "w_out": [a[None] for a in r_wout],
           "conv_w": [a[:CONV_TAPS][None] for a in r_cw]}
    unpacked = [_unpack_small(a, small_w) for a in r_small]
    for n in _SMALL:
        res[n] = [u[n] for u in unpacked]
    order = ("norm_w", "w_in", "q_norm_w", "k_norm_w", "conv_w", "conv_b", "dt_bias", "A_log",
             "D_skip", "sb_norm_w", "ssd_norm_w", "w_out")
    outs = [loss, grad_x]
    for kind in range(4):
        outs += [res[n][kind] for n in order]
    return tuple(outs)
```

```python
import functools
import math

import jax
import jax.numpy as jnp
from jax import lax
from jax.experimental import pallas as pl
from jax.experimental.pallas import tpu as pltpu

F32 = jnp.float32
BF16 = jnp.bfloat16
HIGHEST = lax.Precision.HIGHEST

D_MODEL = 1024
N_HEADS = 16
HEAD_DIM = 64
N_PAIRS = N_HEADS // 2
D_BRANCH = 1024
N_GROUPS = 2
HEADS_PER_GROUP = 8
D_STATE = 128
GROUP_W = HEADS_PER_GROUP * HEAD_DIM
D_BC = 2 * N_GROUPS * D_STATE
D_CONV = D_BRANCH + D_BC
D_IN = 6672
D_IN_PAD = 7168
N_COLBLK = D_IN_PAD // 1024
COL_XS = 5120
COL_BC = 6144
COL_DT = 6656
EPS = 1e-6
CONV_TAPS = 4
N_DEV = 8

LANES = 128
SSD_CHUNK = 128
ATT_BLK = 256
EXP_UNDERFLOW = -105.0
VMEM_LIMIT = 56 * 1024 * 1024

ADAM_LR = 0.001
ADAM_B1 = 0.9
ADAM_B2 = 0.999
ADAM_EPS = 1e-08
ADAM_WD = 0.01
ADAM_STEP = 10

_NT = (((1,), (1,)), ((), ()))
_TN = (((0,), (0,)), ((), ()))


def _params(n_grid):
    return pltpu.CompilerParams(dimension_semantics=("arbitrary",) * n_grid,
                                vmem_limit_bytes=VMEM_LIMIT)


def _dot(a, b, dims=None, precision=None):
    if dims is None:
        return jnp.dot(a, b, preferred_element_type=F32, precision=precision)
    return lax.dot_general(a, b, dims, preferred_element_type=F32, precision=precision)


def _sigmoid(x):
    return 1.0 / (1.0 + jnp.exp(-x))


def _softplus(x):
    return jnp.maximum(x, 0.0) + jnp.log(1.0 + jnp.exp(-jnp.abs(x)))


def _split_bf16(x):
    hi = x.astype(BF16)
    lo = (x - hi.astype(F32)).astype(BF16)
    return hi, lo


def _lane_iota(shape):
    return lax.broadcasted_iota(jnp.int32, shape, len(shape) - 1)


def _row_iota(shape):
    return lax.broadcasted_iota(jnp.int32, shape, len(shape) - 2)


def _pair_sum(x):
    r = lax.broadcasted_iota(jnp.int32, (LANES, LANES), 0)
    c = lax.broadcasted_iota(jnp.int32, (LANES, LANES), 1)
    same_head = jnp.where(r // HEAD_DIM == c // HEAD_DIM, 1.0, 0.0).astype(BF16)
    hi, lo = _split_bf16(x)
    return _dot(hi, same_head) + _dot(lo, same_head)


def _head_expand():
    r = lax.broadcasted_iota(jnp.int32, (LANES, D_BRANCH), 0)
    c = lax.broadcasted_iota(jnp.int32, (LANES, D_BRANCH), 1)
    return jnp.where(c // HEAD_DIM == r, 1.0, 0.0).astype(F32)


def _in_proj(x2, norm_w, w_in_b, tm):
    t = x2.shape[0]

    def body(x_ref, nw_ref, w_ref, proj_ref, hn_ref):
        @pl.when(pl.program_id(1) == 0)
        def _():
            xf = x_ref[...]
            r = lax.rsqrt(jnp.mean(xf * xf, axis=1, keepdims=True) + EPS)
            hn_ref[...] = (xf * r * nw_ref[...]).astype(BF16)

        proj_ref[...] = _dot(hn_ref[...], w_ref[...])

    return pl.pallas_call(
        body, name="in_proj",
        grid=(t // tm, N_COLBLK),
        in_specs=[pl.BlockSpec((tm, D_MODEL), lambda i, j: (i, 0)),
                  pl.BlockSpec((1, D_MODEL), lambda i, j: (0, 0)),
                  pl.BlockSpec((D_MODEL, 1024), lambda i, j: (0, j))],
        out_specs=[pl.BlockSpec((tm, 1024), lambda i, j: (i, j)),
                   pl.BlockSpec((tm, D_MODEL), lambda i, j: (i, 0))],
        out_shape=[jax.ShapeDtypeStruct((t, D_IN_PAD), F32),
                   jax.ShapeDtypeStruct((t, D_MODEL), BF16)],
        compiler_params=_params(2),
    )(x2, norm_w, w_in_b)


def _qk_prep(proj, qw2, kw2, nb, seq, tq):
    nl = seq // tq
    scale = 1.0 / math.sqrt(HEAD_DIM)

    def body(q_ref, k_ref, v_ref, qw_ref, kw_ref, qs_ref, kn_ref, vb_ref, kt_ref):
        def norm(x, w):
            r = lax.rsqrt(_pair_sum(x * x) * (1.0 / HEAD_DIM) + EPS)
            return x * r * w

        qn = norm(q_ref[...], qw_ref[...]) * scale
        kn = norm(k_ref[...], kw_ref[...])
        v = v_ref[...]
        knt = kn.T.astype(BF16)
        for a in range(2):
            sl = slice(a * HEAD_DIM, (a + 1) * HEAD_DIM)
            qs_ref[0, a] = qn[:, sl].astype(BF16)
            kn_ref[0, a] = kn[:, sl].astype(BF16)
            vb_ref[0, a] = v[:, sl].astype(BF16)
            kt_ref[0, a] = knt[sl, :]

    hm = jax.ShapeDtypeStruct((nb, N_HEADS, seq, HEAD_DIM), BF16)
    hm_spec = pl.BlockSpec((1, 2, tq, HEAD_DIM), lambda b, i, h: (b, h, i, 0))
    return pl.pallas_call(
        body, name="qk_prep",
        grid=(nb, nl, N_PAIRS),
        in_specs=[pl.BlockSpec((tq, LANES), lambda b, i, h: (b * nl + i, h)),
                  pl.BlockSpec((tq, LANES), lambda b, i, h: (b * nl + i, N_PAIRS + h)),
                  pl.BlockSpec((tq, LANES), lambda b, i, h: (b * nl + i, 2 * N_PAIRS + h)),
                  pl.BlockSpec((1, LANES), lambda b, i, h: (0, 0)),
                  pl.BlockSpec((1, LANES), lambda b, i, h: (0, 0))],
        out_specs=[hm_spec, hm_spec, hm_spec,
                   pl.BlockSpec((1, 2, HEAD_DIM, tq), lambda b, i, h: (b, h, 0, i))],
        out_shape=[hm, hm, hm, jax.ShapeDtypeStruct((nb, N_HEADS, HEAD_DIM, seq), BF16)],
        compiler_params=_params(3),
    )(proj, proj, proj, qw2, kw2)


def _attn_fwd(qs, kn, vb, nb, seq, blk):
    nq = seq // blk

    def body(q_ref, k_ref, v_ref, o_ref, tot_ref, low_ref, kmax_ref):
        qi = pl.program_id(2)
        r_i = lax.broadcasted_iota(jnp.int32, (blk, blk), 0)
        c_i = lax.broadcasted_iota(jnp.int32, (blk, blk), 1)
        csum = jnp.where(r_i >= c_i, 1.0, 0.0).astype(BF16)
        causal = c_i < r_i
        heads = range(2)

        @pl.when(qi == 0)
        def _():
            for a in heads:
                kk = k_ref[0, a].astype(F32)
                kmax_ref[a] = jnp.full((8, LANES), jnp.max(jnp.sum(kk * kk, axis=1, keepdims=True)))

        zmax = []
        for a in heads:
            qf = q_ref[0, a].astype(F32)
            qsq = jnp.sum(qf * qf, axis=1, keepdims=True)
            zmax.append(1.01 * jnp.sqrt(qsq * kmax_ref[a][0:1, 0:1]) + 0.01)

        def exhausted(run):
            top = jnp.maximum(jnp.max(run[0] + zmax[0]), jnp.max(run[1] + zmax[1]))
            return top < EXP_UNDERFLOW

        def sweep(blocks, run, acc):
            offs = [pl.multiple_of(j * blk, blk) for j, _, _ in blocks]
            z = [[_dot(q_ref[0, a], k_ref[0, a, pl.ds(off, blk), :], _NT) for a in heads]
                 for off in offs]
            cl = []
            for (_, diag, valid), zb in zip(blocks, z):
                split = []
                for a in heads:
                    lk = -_softplus(zb[a])
                    if diag:
                        lk = jnp.where(causal, lk, 0.0)
                    if valid is not None:
                        lk = jnp.where(valid, lk, 0.0)
                    split.append(_split_bf16(lk))
                cl.append([_dot(split[a][0], csum) + _dot(split[a][1], csum) for a in heads])
            for (_, diag, valid), zb, clb, off in zip(blocks, z, cl, offs):
                w = []
                for a in heads:
                    wa = jnp.exp(zb[a] + clb[a] + run[a])
                    if diag:
                        wa = jnp.where(causal, wa, 0.0)
                    if valid is not None:
                        wa = jnp.where(valid, wa, 0.0)
                    w.append(wa.astype(BF16))
                run = [run[a] + clb[a][:, 0:1] for a in heads]
                acc = [acc[a] + _dot(w[a], v_ref[0, a, pl.ds(off, blk), :]) for a in heads]
            return run, acc

        run = [jnp.zeros((blk, 1), F32)] * 2
        acc = [jnp.zeros((blk, HEAD_DIM), F32)] * 2
        run, acc = sweep([(qi, True, None), (jnp.maximum(qi - 1, 0), False, qi >= 1)], run, acc)
        low = jnp.maximum(qi - 1, 0)

        def more(carry):
            low, done, _, _ = carry
            return (low > 0) & jnp.logical_not(done)

        def pair(carry):
            low, _, run, acc = carry
            run, acc = sweep([(low - 1, False, None), (jnp.maximum(low - 2, 0), False, low >= 2)],
                             run, acc)
            return jnp.maximum(low - 2, 0), exhausted(run), run, acc

        low, _, run, acc = lax.while_loop(more, pair, (low, exhausted(run), run, acc))
        low_ref[pl.program_id(0) * N_PAIRS + pl.program_id(1), qi] = low.astype(F32)
        for a in heads:
            o_ref[:, a * HEAD_DIM:(a + 1) * HEAD_DIM] = acc[a]
            as_row = jnp.sum(jnp.where(r_i == c_i, run[a], 0.0), axis=0, keepdims=True)
            tot_ref[0, a, 0] = jnp.broadcast_to(as_row, (8, blk))

    return pl.pallas_call(
        body, name="sb_attn_fwd",
        grid=(nb, N_PAIRS, nq),
        in_specs=[pl.BlockSpec((1, 2, blk, HEAD_DIM), lambda b, h, i: (b, h, i, 0)),
                  pl.BlockSpec((1, 2, seq, HEAD_DIM), lambda b, h, i: (b, h, 0, 0)),
                  pl.BlockSpec((1, 2, seq, HEAD_DIM), lambda b, h, i: (b, h, 0, 0))],
        out_specs=[pl.BlockSpec((blk, LANES), lambda b, h, i: (b * nq + i, h)),
                   pl.BlockSpec((1, 2, 1, 8, blk), lambda b, h, i: (b, h, i, 0, 0)),
                   pl.BlockSpec(memory_space=pltpu.SMEM)],
        out_shape=[jax.ShapeDtypeStruct((nb * seq, D_BRANCH), F32),
                   jax.ShapeDtypeStruct((nb, N_HEADS, nq, 8, blk), F32),
                   jax.ShapeDtypeStruct((nb * N_PAIRS, nq), F32)],
        scratch_shapes=[pltpu.VMEM((2, 8, LANES), F32)],
        compiler_params=_params(3),
    )(qs, kn, vb)


def _attn_bwd(qs, kn, kt, vb, tot, low, d_o, nb, seq, blk):
    nq = seq // blk

    def body(q_ref, k_ref, kt_ref, v_ref, tot_ref, low_ref, do_ref, dq_ref, dk_ref, dv_ref):
        qi = pl.program_id(2)

        @pl.when(qi == 0)
        def _():
            dk_ref[...] = jnp.zeros_like(dk_ref)
            dv_ref[...] = jnp.zeros_like(dv_ref)

        r_i = lax.broadcasted_iota(jnp.int32, (blk, blk), 0)
        c_i = lax.broadcasted_iota(jnp.int32, (blk, blk), 1)
        before = jnp.where(c_i < r_i, 1.0, 0.0).astype(BF16)
        upto = jnp.where(c_i <= r_i, 1.0, 0.0).astype(BF16)
        causal = r_i < c_i

        heads = range(2)
        d_ob = [do_ref[:, a * HEAD_DIM:(a + 1) * HEAD_DIM].astype(BF16) for a in heads]
        total = [tot_ref[0, a, 0][0:1, :] for a in heads]

        def sweep(blocks, lsum, esum, dqt):
            def keep(x, diag, valid):
                if diag:
                    x = jnp.where(causal, x, 0.0)
                if valid is not None:
                    x = jnp.where(valid, x, 0.0)
                return x

            offs = [pl.multiple_of(j * blk, blk) for j, _, _ in blocks]
            zt = [[_dot(k_ref[0, a, pl.ds(off, blk), :], q_ref[0, a], _NT) for a in heads]
                  for off in offs]
            dwt = [[_dot(v_ref[0, a, pl.ds(off, blk), :], d_ob[a], _NT) for a in heads]
                   for off in offs]
            sp, lk, lpre = [], [], []
            for (_, diag, valid), ztb in zip(blocks, zt):
                sp.append([_softplus(ztb[a]) for a in heads])
                lk.append([keep(-sp[-1][a], diag, valid) for a in heads])
                split = [_split_bf16(lk[-1][a]) for a in heads]
                lpre.append([_dot(before, split[a][0]) + _dot(before, split[a][1]) for a in heads])
            wt, et, epre = [], [], []
            for i, (_, diag, valid) in enumerate(blocks):
                wt.append([keep(jnp.exp(zt[i][a] + (total[a] - lsum[a] - lpre[i][a])), diag, valid)
                           for a in heads])
                et.append([dwt[i][a] * wt[i][a] for a in heads])
                split = [_split_bf16(et[i][a]) for a in heads]
                epre.append([_dot(upto, split[a][0]) + _dot(upto, split[a][1]) for a in heads])
                lsum = [lsum[a] + lpre[i][a][blk - 1:blk, :] + lk[i][a][blk - 1:blk, :]
                        for a in heads]
            for i, (_, diag, valid) in enumerate(blocks):
                dzb = [keep(et[i][a] - jnp.exp(zt[i][a] - sp[i][a]) * (esum[a] + epre[i][a]),
                            diag, valid).astype(BF16) for a in heads]
                esum = [esum[a] + epre[i][a][blk - 1:blk, :] for a in heads]
                for a in heads:
                    dk_ref[0, a, pl.ds(offs[i], blk), :] += _dot(dzb[a], q_ref[0, a])
                    dv_ref[0, a, pl.ds(offs[i], blk), :] += _dot(wt[i][a].astype(BF16), d_ob[a])
                dqt = [dqt[a] + _dot(kt_ref[0, a, :, pl.ds(offs[i], blk)], dzb[a]) for a in heads]
            return lsum, esum, dqt

        row = [jnp.zeros((1, blk), F32)] * 2
        dqt = [jnp.zeros((HEAD_DIM, blk), F32)] * 2
        low = low_ref[pl.program_id(0) * N_PAIRS + pl.program_id(1), qi].astype(jnp.int32)
        low = jnp.clip(low, 0, jnp.maximum(qi - 1, 0))

        def pair(carry):
            j, lsum, esum, dqt = carry
            return (j + 2,) + sweep([(j, False, None), (j + 1, False, j + 1 < qi - 1)],
                                    lsum, esum, dqt)

        _, lsum, esum, dqt = lax.while_loop(lambda c: c[0] < qi - 1, pair, (low, row, row, dqt))
        _, _, dqt = sweep([(jnp.maximum(qi - 1, 0), False, qi >= 1), (qi, True, None)],
                          lsum, esum, dqt)
        for a in heads:
            dq_ref[:, a * HEAD_DIM:(a + 1) * HEAD_DIM] = dqt[a].T

    hm_acc = pl.BlockSpec((1, 2, seq, HEAD_DIM), lambda b, h, i: (b, h, 0, 0))
    tok = pl.BlockSpec((blk, LANES), lambda b, h, i: (b * nq + i, h))
    hm_shape = jax.ShapeDtypeStruct((nb, N_HEADS, seq, HEAD_DIM), F32)
    return pl.pallas_call(
        body, name="sb_attn_bwd",
        grid=(nb, N_PAIRS, nq),
        in_specs=[pl.BlockSpec((1, 2, blk, HEAD_DIM), lambda b, h, i: (b, h, i, 0)),
                  hm_acc,
                  pl.BlockSpec((1, 2, HEAD_DIM, seq), lambda b, h, i: (b, h, 0, 0)),
                  hm_acc,
                  pl.BlockSpec((1, 2, 1, 8, blk), lambda b, h, i: (b, h, i, 0, 0)),
                  pl.BlockSpec(memory_space=pltpu.SMEM),
                  tok],
        out_specs=[tok, hm_acc, hm_acc],
        out_shape=[jax.ShapeDtypeStruct((nb * seq, D_BRANCH), F32), hm_shape, hm_shape],
        compiler_params=_params(3),
    )(qs, kn, kt, vb, tot, low, d_o)


def _qk_bwd(proj, dqs, dkn, dvh, qw2, kw2, nb, seq, tq):
    nl = seq // tq
    scale = 1.0 / math.sqrt(HEAD_DIM)

    def body(q_ref, k_ref, dq_ref, dk_ref, dv_ref, qw_ref, kw_ref,
             dqr_ref, dkr_ref, dvr_ref, gq_ref, gk_ref):
        @pl.when((pl.program_id(0) == 0) & (pl.program_id(1) == 0) & (pl.program_id(2) == 0))
        def _():
            gq_ref[...] = jnp.zeros_like(gq_ref)
            gk_ref[...] = jnp.zeros_like(gk_ref)

        def norm_bwd(x, w, dy):
            r = lax.rsqrt(_pair_sum(x * x) * (1.0 / HEAD_DIM) + EPS)
            xhat = x * r
            g = dy * w
            m = _pair_sum(g * xhat) * (1.0 / HEAD_DIM)
            return r * (g - xhat * m), jnp.sum(dy * xhat, axis=0, keepdims=True)

        dqr, gq = norm_bwd(q_ref[...], qw_ref[...], dq_ref[...] * scale)
        dk2 = jnp.concatenate([dk_ref[0, 0], dk_ref[0, 1]], axis=1)
        dkr, gk = norm_bwd(k_ref[...], kw_ref[...], dk2)
        dqr_ref[...] = dqr.astype(BF16)
        dkr_ref[...] = dkr.astype(BF16)
        dvr_ref[...] = jnp.concatenate([dv_ref[0, 0], dv_ref[0, 1]], axis=1).astype(BF16)
        gq_ref[...] += gq
        gk_ref[...] += gk

    tok = lambda off: pl.BlockSpec((tq, LANES), lambda b, i, h: (b * nl + i, off + h))
    hm = pl.BlockSpec((1, 2, tq, HEAD_DIM), lambda b, i, h: (b, h, i, 0))
    vec = pl.BlockSpec((1, LANES), lambda b, i, h: (0, 0))
    tshape = jax.ShapeDtypeStruct((nb * seq, D_BRANCH), BF16)
    return pl.pallas_call(
        body, name="qk_bwd",
        grid=(nb, nl, N_PAIRS),
        in_specs=[tok(0), tok(N_PAIRS), tok(0), hm, hm, vec, vec],
        out_specs=[tok(0), tok(0), tok(0), vec, vec],
        out_shape=[tshape, tshape, tshape,
                   jax.ShapeDtypeStruct((1, LANES), F32), jax.ShapeDtypeStruct((1, LANES), F32)],
        compiler_params=_params(3),
    )(proj, proj, dqs, dkn, dvh, qw2, kw2)


def _shift_down(cur, prev, k):
    if k == 0:
        return cur
    rows = _row_iota(cur.shape)
    return jnp.where(rows < k, pltpu.roll(prev, k, axis=0), pltpu.roll(cur, k, axis=0))


def _shift_up(cur, nxt, k):
    if k == 0:
        return cur
    n = cur.shape[0]
    rows = _row_iota(cur.shape)
    return jnp.where(rows < n - k, pltpu.roll(cur, n - k, axis=0), pltpu.roll(nxt, n - k, axis=0))


def _conv_pre(cur, prev, w, b):
    out = b
    for i in range(CONV_TAPS):
        out = out + _shift_down(cur, prev, CONV_TAPS - 1 - i) * w[i:i + 1, :]
    return out


def _silu(x):
    return x * _sigmoid(x)


def _silu_grad(x):
    s = _sigmoid(x)
    return s * (1.0 + x * (1.0 - s))


def _chunk_decay(dt_raw, dtb, alog, expand, qc):
    dt = _softplus(dt_raw + dtb)
    d_a = dt * (-jnp.exp(alog))
    r_i = lax.broadcasted_iota(jnp.int32, (qc, qc), 0)
    c_i = lax.broadcasted_iota(jnp.int32, (qc, qc), 1)
    tril = r_i >= c_i
    a_cs = _dot(jnp.where(tril, 1.0, 0.0).astype(F32), d_a, precision=HIGHEST)
    dt_x = _dot(dt, expand, precision=HIGHEST)
    acs_x = _dot(a_cs, expand, precision=HIGHEST)
    return dt, d_a, a_cs, dt_x, acs_x, tril


def _ssd_fwd(proj, conv_w, conv_b, dtb, alog, dskip, nb, seq):
    qc = SSD_CHUNK
    nc = seq // qc

    def body(xs_ref, bc_ref, dt_ref, cw_ref, cb_ref, dtb_ref, al_ref, ds_ref,
             y_ref, st_ref, pxs_ref, pbc_ref, state_ref):
        @pl.when(pl.program_id(1) == 0)
        def _():
            pxs_ref[...] = jnp.zeros_like(pxs_ref)
            pbc_ref[...] = jnp.zeros_like(pbc_ref)
            state_ref[...] = jnp.zeros_like(state_ref)

        expand = _head_expand()
        xs_raw = xs_ref[...]
        bc_raw = bc_ref[...]
        cw = cw_ref[...]
        cb = cb_ref[...]
        xs = _silu(_conv_pre(xs_raw, pxs_ref[...], cw[:, :D_BRANCH], cb[:, :D_BRANCH]))
        bc = _silu(_conv_pre(bc_raw, pbc_ref[...], cw[:, D_BRANCH:], cb[:, D_BRANCH:]))
        pxs_ref[...] = xs_raw
        pbc_ref[...] = bc_raw

        dt, d_a, a_cs, dt_x, acs_x, tril = _chunk_decay(
            dt_ref[...], dtb_ref[...], al_ref[...], expand, qc)
        a_cst = a_cs.T
        aend_x = acs_x[qc - 1:qc, :]
        ea_x = jnp.exp(acs_x)
        dec_x = jnp.exp(aend_x - acs_x)
        xt = xs * dt_x
        xtb = xt.astype(BF16)
        xdb = (xt * dec_x).astype(BF16)
        d_x = _dot(jnp.broadcast_to(ds_ref[...], (8, LANES)), expand, precision=HIGHEST)[0:1, :]
        st_ref[0, 0] = state_ref[...]

        for g in range(N_GROUPS):
            gs = slice(g * GROUP_W, (g + 1) * GROUP_W)
            bg = bc[:, g * D_STATE:(g + 1) * D_STATE]
            cg = bc[:, (N_GROUPS + g) * D_STATE:(N_GROUPS + g + 1) * D_STATE]
            bgb = bg.astype(BF16)
            cgb = cg.astype(BF16)
            cbm = _dot(cgb, bgb, _NT)
            st_in = state_ref[g]
            y_off = _dot(cgb, st_in.astype(BF16)) * ea_x[:, gs]
            for k in range(HEADS_PER_GROUP):
                h = g * HEADS_PER_GROUP + k
                hs = slice(h * HEAD_DIM, (h + 1) * HEAD_DIM)
                seg = a_cs[:, h:h + 1] - a_cst[h:h + 1, :]
                gh = cbm * jnp.exp(jnp.where(tril, seg, -1e30))
                y_h = _dot(gh.astype(BF16), xtb[:, hs]) + y_off[:, k * HEAD_DIM:(k + 1) * HEAD_DIM]
                y_ref[:, hs] = y_h + d_x[:, hs] * xs[:, hs]
            state_ref[g] = st_in * jnp.exp(aend_x[:, gs]) + _dot(bg.T.astype(BF16), xdb[:, gs])

    nblk = lambda w, off: pl.BlockSpec((qc, w), lambda b, c: (b * nc + c, off))
    full = lambda r, w: pl.BlockSpec((r, w), lambda b, c: (0, 0))
    return pl.pallas_call(
        body, name="ssd_fwd",
        grid=(nb, nc),
        in_specs=[nblk(D_BRANCH, COL_XS // D_BRANCH), nblk(D_BC, COL_BC // D_BC),
                  nblk(LANES, COL_DT // LANES),
                  full(CONV_TAPS, D_CONV), full(1, D_CONV), full(1, LANES), full(1, LANES),
                  full(1, LANES)],
        out_specs=[pl.BlockSpec((qc, D_BRANCH), lambda b, c: (b * nc + c, 0)),
                   pl.BlockSpec((1, 1, N_GROUPS, D_STATE, GROUP_W), lambda b, c: (b, c, 0, 0, 0))],
        out_shape=[jax.ShapeDtypeStruct((nb * seq, D_BRANCH), F32),
                   jax.ShapeDtypeStruct((nb, nc, N_GROUPS, D_STATE, GROUP_W), F32)],
        scratch_shapes=[pltpu.VMEM((qc, D_BRANCH), F32), pltpu.VMEM((qc, D_BC), F32),
                        pltpu.VMEM((N_GROUPS, D_STATE, GROUP_W), F32)],
        compiler_params=_params(2),
    )(proj, proj, proj, conv_w, conv_b, dtb, alog, dskip)


def _ssd_bwd(proj, d_y, states, conv_w, conv_b, dtb, alog, dskip, nb, seq):
    qc = SSD_CHUNK
    nc = seq // qc

    def body(xs_ref, bc_ref, dt_ref, pxs_ref, pbc_ref, dy_ref, st_ref, stn_ref,
             cw_ref, cb_ref, dtb_ref, al_ref, ds_ref,
             dx_ref, gcw_ref, gcb_ref, gdtb_ref, gal_ref, gds_ref,
             dst_ref, nxs_ref, nbc_ref, yd_ref, dxt_ref):
        step = pl.program_id(1)
        chunk = nc - 1 - step

        @pl.when(step == 0)
        def _():
            dst_ref[...] = jnp.zeros_like(dst_ref)
            nxs_ref[...] = jnp.zeros_like(nxs_ref)
            nbc_ref[...] = jnp.zeros_like(nbc_ref)

        @pl.when((pl.program_id(0) == 0) & (step == 0))
        def _():
            gcw_ref[...] = jnp.zeros_like(gcw_ref)
            gcb_ref[...] = jnp.zeros_like(gcb_ref)
            gdtb_ref[...] = jnp.zeros_like(gdtb_ref)
            gal_ref[...] = jnp.zeros_like(gal_ref)
            gds_ref[...] = jnp.zeros_like(gds_ref)

        expand = _head_expand()
        collapse = lambda v: _dot(v, expand, _NT, precision=HIGHEST)
        first = jnp.where(chunk == 0, 0.0, 1.0)
        xs_raw = xs_ref[...]
        bc_raw = bc_ref[...]
        pxs = pxs_ref[...] * first
        pbc = pbc_ref[...] * first
        cw = cw_ref[...]
        cb = cb_ref[...]
        pre_xs = _conv_pre(xs_raw, pxs, cw[:, :D_BRANCH], cb[:, :D_BRANCH])
        pre_bc = _conv_pre(bc_raw, pbc, cw[:, D_BRANCH:], cb[:, D_BRANCH:])
        xs = _silu(pre_xs)
        bc = _silu(pre_bc)

        dt_in = dt_ref[...] + dtb_ref[...]
        dt, d_a, a_cs, dt_x, acs_x, tril = _chunk_decay(
            dt_ref[...], dtb_ref[...], al_ref[...], expand, qc)
        a_cst = a_cs.T
        aend_x = acs_x[qc - 1:qc, :]
        ea_x = jnp.exp(acs_x)
        dec_x = jnp.exp(aend_x - acs_x)
        xt = xs * dt_x
        xtb = xt.astype(BF16)
        xdb = (xt * dec_x).astype(BF16)
        d_x = _dot(jnp.broadcast_to(ds_ref[...], (8, LANES)), expand, precision=HIGHEST)[0:1, :]

        dy = dy_ref[...]
        dyb = dy.astype(BF16)
        dyeab = (dy * ea_x).astype(BF16)
        gds_ref[...] += collapse(jnp.broadcast_to(jnp.sum(dy * xs, axis=0, keepdims=True),
                                                  (8, D_BRANCH)))[0:1, :]

        d_bc = []
        d_cc = []
        y_offs = []
        dxt_states = []
        end_terms = []
        for g in range(N_GROUPS):
            gs = slice(g * GROUP_W, (g + 1) * GROUP_W)
            bg = bc[:, g * D_STATE:(g + 1) * D_STATE]
            cg = bc[:, (N_GROUPS + g) * D_STATE:(N_GROUPS + g + 1) * D_STATE]
            bgb = bg.astype(BF16)
            cgb = cg.astype(BF16)
            cbm = _dot(cgb, bgb, _NT)
            st_in = st_ref[0, 0, g]
            st_inb = st_in.astype(BF16)
            d_st = dst_ref[g]
            d_stb = d_st.astype(BF16)
            y_offs.append(_dot(cgb, st_inb) * ea_x[:, gs])
            dxt_states.append(_dot(bgb, d_stb) * dec_x[:, gs])
            d_c = _dot(dyeab[:, gs], st_inb, _NT)
            d_b = _dot(xdb[:, gs], d_stb, _NT)
            d_cb = jnp.zeros((qc, qc), F32)
            for k in range(HEADS_PER_GROUP):
                h = g * HEADS_PER_GROUP + k
                hs = slice(h * HEAD_DIM, (h + 1) * HEAD_DIM)
                seg = a_cs[:, h:h + 1] - a_cst[h:h + 1, :]
                lh = jnp.exp(jnp.where(tril, seg, -1e30))
                ghb = (cbm * lh).astype(BF16)
                d_cb = d_cb + _dot(dyb[:, hs], xtb[:, hs], _NT) * lh
                yd_ref[:, hs] = _dot(ghb, xtb[:, hs])
                dxt_ref[:, hs] = _dot(ghb, dyb[:, hs], _TN)
            d_cbb = d_cb.astype(BF16)
            d_cc.append(d_c + _dot(d_cbb, bgb))
            d_bc.append(d_b + _dot(d_cbb, cgb, _TN))
            end_terms.append(jnp.sum(d_st * stn_ref[0, 0, g], axis=0, keepdims=True))
            dst_ref[g] = d_st * jnp.exp(aend_x[:, gs]) + _dot(cg.T.astype(BF16), dyeab[:, gs])

        y_off = jnp.concatenate(y_offs, axis=1)
        dxt_state = jnp.concatenate(dxt_states, axis=1)
        dxt = dxt_ref[...] + dxt_state
        last = jnp.where(chunk == nc - 1, 0.0, 1.0)
        end_c = collapse(jnp.broadcast_to(jnp.concatenate(end_terms, axis=1), (8, D_BRANCH)))[0:1, :]
        da_cs = collapse(dyb.astype(F32) * yd_ref[...] - dxt_ref[...] * xtb.astype(F32)
                         + dy * y_off - dxt_state * xt)
        da_cs = da_cs + jnp.where(_row_iota(da_cs.shape) == qc - 1, end_c * last, 0.0)
        triu = lax.broadcasted_iota(jnp.int32, (qc, qc), 0) <= lax.broadcasted_iota(jnp.int32, (qc, qc), 1)
        dd_a = _dot(jnp.where(triu, 1.0, 0.0).astype(F32), da_cs, precision=HIGHEST)
        ddt = dd_a * (-jnp.exp(al_ref[...])) + collapse(dxt * xs)
        head_lanes = _lane_iota(ddt.shape) < N_HEADS
        ddt_raw = jnp.where(head_lanes, ddt * _sigmoid(dt_in), 0.0)
        gal_ref[...] += jnp.sum(jnp.where(head_lanes, dd_a * d_a, 0.0), axis=0, keepdims=True)
        gdtb_ref[...] += jnp.sum(ddt_raw, axis=0, keepdims=True)

        dpre_xs = (dxt * dt_x + d_x * dy) * _silu_grad(pre_xs)
        dpre_bc = jnp.concatenate(d_bc + d_cc, axis=1) * _silu_grad(pre_bc)
        gcb_ref[...] += jnp.concatenate([jnp.sum(dpre_xs, axis=0, keepdims=True),
                                         jnp.sum(dpre_bc, axis=0, keepdims=True)], axis=1)
        nxs = nxs_ref[...]
        nbc = nbc_ref[...]
        du_xs = jnp.zeros_like(dpre_xs)
        du_bc = jnp.zeros_like(dpre_bc)
        for i in range(CONV_TAPS):
            k = CONV_TAPS - 1 - i
            gcw_ref[i:i + 1, :] += jnp.concatenate(
                [jnp.sum(dpre_xs * _shift_down(xs_raw, pxs, k), axis=0, keepdims=True),
                 jnp.sum(dpre_bc * _shift_down(bc_raw, pbc, k), axis=0, keepdims=True)], axis=1)
            du_xs = du_xs + _shift_up(dpre_xs, nxs, k) * cw[i:i + 1, :D_BRANCH]
            du_bc = du_bc + _shift_up(dpre_bc, nbc, k) * cw[i:i + 1, D_BRANCH:]
        nxs_ref[...] = dpre_xs
        nbc_ref[...] = dpre_bc

        dx_ref[:, :D_BRANCH] = du_xs.astype(BF16)
        dx_ref[:, D_BRANCH:D_CONV] = du_bc.astype(BF16)
        dx_ref[:, D_CONV:D_CONV + LANES] = ddt_raw.astype(BF16)
        dx_ref[:, D_CONV + LANES:] = jnp.zeros((qc, 2048 - D_CONV - LANES), BF16)

    rev = lambda b, c: b * nc + (nc - 1 - c)
    prv = lambda b, c: b * nc + jnp.maximum(nc - 2 - c, 0)
    nblk = lambda w, off, f: pl.BlockSpec((qc, w), lambda b, c: (f(b, c), off))
    full = lambda r, w: pl.BlockSpec((r, w), lambda b, c: (0, 0))
    st_spec = lambda f: pl.BlockSpec((1, 1, N_GROUPS, D_STATE, GROUP_W),
                                     lambda b, c: (b, f(c), 0, 0, 0))
    return pl.pallas_call(
        body, name="ssd_bwd",
        grid=(nb, nc),
        in_specs=[nblk(D_BRANCH, COL_XS // D_BRANCH, rev), nblk(D_BC, COL_BC // D_BC, rev),
                  nblk(LANES, COL_DT // LANES, rev),
                  nblk(D_BRANCH, COL_XS // D_BRANCH, prv), nblk(D_BC, COL_BC // D_BC, prv),
                  nblk(D_BRANCH, 0, rev),
                  st_spec(lambda c: nc - 1 - c), st_spec(lambda c: jnp.minimum(nc - c, nc - 1)),
                  full(CONV_TAPS, D_CONV), full(1, D_CONV), full(1, LANES), full(1, LANES),
                  full(1, LANES)],
        out_specs=[nblk(2048, 0, rev), full(8, D_CONV), full(1, D_CONV), full(1, LANES),
                   full(1, LANES), full(1, LANES)],
        out_shape=[jax.ShapeDtypeStruct((nb * seq, 2048), BF16),
                   jax.ShapeDtypeStruct((8, D_CONV), F32), jax.ShapeDtypeStruct((1, D_CONV), F32),
                   jax.ShapeDtypeStruct((1, LANES), F32), jax.ShapeDtypeStruct((1, LANES), F32),
                   jax.ShapeDtypeStruct((1, LANES), F32)],
        scratch_shapes=[pltpu.VMEM((N_GROUPS, D_STATE, GROUP_W), F32),
                        pltpu.VMEM((qc, D_BRANCH), F32), pltpu.VMEM((qc, D_BC), F32),
                        pltpu.VMEM((qc, D_BRANCH), F32), pltpu.VMEM((qc, D_BRANCH), F32)],
        compiler_params=_params(2),
    )(proj, proj, proj, proj, proj, d_y, states, states, conv_w, conv_b, dtb, alog, dskip)


def _mid(o_sb, y_ssd, proj, x2, target, sb_w, ssd_w, w_out_b, tm):
    t = x2.shape[0]
    inv_d = 1.0 / D_MODEL

    def body(o_ref, y_ref, zsb_ref, zssd_ref, x_ref, tg_ref, sbw_ref, ssdw_ref, w_ref,
             dout_ref, dosb_ref, dy_ref, dz_ref, gw_ref, gsb_ref, gssd_ref, loss_ref):
        @pl.when(pl.program_id(0) == 0)
        def _():
            gw_ref[...] = jnp.zeros_like(gw_ref)
            gsb_ref[...] = jnp.zeros_like(gsb_ref)
            gssd_ref[...] = jnp.zeros_like(gssd_ref)
            loss_ref[...] = jnp.zeros_like(loss_ref)

        def branch(val, z, w):
            gate = _silu(z)
            g = val * gate
            r = lax.rsqrt(jnp.mean(g * g, axis=1, keepdims=True) + EPS)
            xhat = g * r
            return gate, r, xhat, (xhat * w).astype(BF16)

        o = o_ref[...]
        y = y_ref[...]
        z_sb = zsb_ref[...]
        z_ssd = zssd_ref[...]
        gate_a, r_a, xhat_a, mix_a = branch(o, z_sb, sbw_ref[...])
        gate_b, r_b, xhat_b, mix_b = branch(y, z_ssd, ssdw_ref[...])
        out = x_ref[...] + _dot(mix_a, w_ref[:D_BRANCH, :]) + _dot(mix_b, w_ref[D_BRANCH:, :])
        diff = out - tg_ref[...]
        loss_ref[...] += 0.5 * inv_d * jnp.sum(diff * diff)
        d_out = diff * inv_d
        dout_ref[...] = d_out
        d_outb = d_out.astype(BF16)
        gw_ref[:D_BRANCH, :] += _dot(mix_a, d_outb, _TN)
        gw_ref[D_BRANCH:, :] += _dot(mix_b, d_outb, _TN)

        def branch_bwd(dmix, val, z, w, gate, r, xhat):
            gg = dmix * w
            m = jnp.mean(gg * xhat, axis=1, keepdims=True)
            dg = r * (gg - xhat * m)
            return dg * gate, dg * val * _silu_grad(z), jnp.sum(dmix * xhat, axis=0, keepdims=True)

        dmix_a = _dot(d_outb, w_ref[:D_BRANCH, :], _NT)
        dmix_b = _dot(d_outb, w_ref[D_BRANCH:, :], _NT)
        d_o, dz_a, gsb = branch_bwd(dmix_a, o, z_sb, sbw_ref[...], gate_a, r_a, xhat_a)
        d_y, dz_b, gssd = branch_bwd(dmix_b, y, z_ssd, ssdw_ref[...], gate_b, r_b, xhat_b)
        dosb_ref[...] = d_o
        dy_ref[...] = d_y
        dz_ref[:, :D_BRANCH] = dz_a.astype(BF16)
        dz_ref[:, D_BRANCH:] = dz_b.astype(BF16)
        gsb_ref[...] += gsb
        gssd_ref[...] += gssd

    row = lambda w, off: pl.BlockSpec((tm, w), lambda i: (i, off))
    full = lambda r, w: pl.BlockSpec((r, w), lambda i: (0, 0))
    resident = pl.BlockSpec((2 * D_BRANCH, D_MODEL), lambda i: (0, 0), pipeline_mode=pl.Buffered(1))
    tok = jax.ShapeDtypeStruct((t, D_MODEL), F32)
    return pl.pallas_call(
        body, name="mid",
        grid=(t // tm,),
        in_specs=[row(D_BRANCH, 0), row(D_BRANCH, 0), row(D_BRANCH, 3), row(D_BRANCH, 4),
                  row(D_MODEL, 0), row(D_MODEL, 0), full(1, D_BRANCH), full(1, D_BRANCH),
                  resident],
        out_specs=[row(D_MODEL, 0), row(D_BRANCH, 0), row(D_BRANCH, 0), row(2 * D_BRANCH, 0),
                   resident, full(1, D_BRANCH), full(1, D_BRANCH),
                   full(1, LANES)],
        out_shape=[tok, tok, tok, jax.ShapeDtypeStruct((t, 2 * D_BRANCH), BF16),
                   jax.ShapeDtypeStruct((2 * D_BRANCH, D_MODEL), F32),
                   jax.ShapeDtypeStruct((1, D_BRANCH), F32), jax.ShapeDtypeStruct((1, D_BRANCH), F32),
                   jax.ShapeDtypeStruct((1, LANES), F32)],
        compiler_params=_params(1),
    )(o_sb, y_ssd, proj, proj, x2, target, sb_w, ssd_w, w_out_b)


_DPROJ_FIRST = (0, 1, 2, 3, 5)
_DPROJ_BLOCKS = (1, 1, 1, 2, 2)
_DPROJ_OWNER = (0, 1, 2, 3, 3, 4, 4)


def _dproj_col(j, p):
    return jnp.clip(j - _DPROJ_FIRST[p], 0, _DPROJ_BLOCKS[p] - 1)


def _in_proj_bwd_x(d_parts, w_in_b, x2, d_out, norm_w, tm, slabs=()):
    t = x2.shape[0]
    n_parts = len(d_parts)
    n_slabs = len(slabs)
    n_rows = t // tm

    def body(*refs):
        dp_refs = refs[:n_parts]
        w_ref, x_ref, dout_ref, nw_ref = refs[n_parts:n_parts + 4]
        rest = refs[n_parts + 4:]
        src_refs, (gx_ref, gnw_ref) = rest[:n_slabs], rest[n_slabs:n_slabs + 2]
        dst_refs = rest[n_slabs + 2:2 * n_slabs + 2]
        acc_ref = rest[2 * n_slabs + 2]
        sems = rest[2 * n_slabs + 3:]
        i, j = pl.program_id(0), pl.program_id(1)

        if n_slabs:
            @pl.when((i == 0) & (j == 0))
            def _():
                _exchange_start(_exchange_copies(src_refs, dst_refs, (True,) * n_slabs, *sems))

            @pl.when((i == n_rows - 1) & (j == N_COLBLK - 1))
            def _():
                _exchange_wait(_exchange_copies(src_refs, dst_refs, (True,) * n_slabs, *sems))

        @pl.when((i == 0) & (j == 0))
        def _():
            gnw_ref[...] = jnp.zeros_like(gnw_ref)

        @pl.when(j == 0)
        def _():
            acc_ref[...] = jnp.zeros_like(acc_ref)

        for jj in range(N_COLBLK):
            @pl.when(j == jj)
            def _(jj=jj):
                acc_ref[...] += _dot(dp_refs[_DPROJ_OWNER[jj]][...], w_ref[...], _NT)

        @pl.when(j == N_COLBLK - 1)
        def _():
            xf = x_ref[...]
            d_hn = acc_ref[...]
            r = lax.rsqrt(jnp.mean(xf * xf, axis=1, keepdims=True) + EPS)
            xhat = xf * r
            g = d_hn * nw_ref[...]
            m = jnp.mean(g * xhat, axis=1, keepdims=True)
            gx_ref[...] = dout_ref[...] + r * (g - xhat * m)
            gnw_ref[...] += jnp.sum(d_hn * xhat, axis=0, keepdims=True)

    return pl.pallas_call(
        body, name="in_proj_bwd_x",
        grid=(t // tm, N_COLBLK),
        in_specs=[pl.BlockSpec((tm, 1024), lambda i, j, p=p: (i, _dproj_col(j, p)))
                  for p in range(n_parts)] + [
                  pl.BlockSpec((D_MODEL, 1024), lambda i, j: (0, j)),
                  pl.BlockSpec((tm, D_MODEL), lambda i, j: (i, 0)),
                  pl.BlockSpec((tm, D_MODEL), lambda i, j: (i, 0)),
                  pl.BlockSpec((1, D_MODEL), lambda i, j: (0, 0))] + [_ANY] * n_slabs,
        out_specs=[pl.BlockSpec((tm, D_MODEL), lambda i, j: (i, 0)),
                   pl.BlockSpec((1, D_MODEL), lambda i, j: (0, 0))] + [_ANY] * n_slabs,
        out_shape=[jax.ShapeDtypeStruct((t, D_MODEL), F32), jax.ShapeDtypeStruct((1, D_MODEL), F32)]
                  + _exchange_shapes(slabs, (True,) * n_slabs),
        scratch_shapes=[pltpu.VMEM((tm, D_MODEL), F32)] + (_exchange_sems(n_slabs) if n_slabs else []),
        compiler_params=_params(2),
    )(*d_parts, w_in_b, x2, d_out, norm_w, *slabs)


def _in_proj_bwd_w(hn, d_parts, tm):
    t = hn.shape[0]
    n_parts = len(d_parts)

    def body(hn_ref, *refs):
        dp_refs, gw_ref = refs[:n_parts], refs[n_parts]
        j = pl.program_id(0)

        @pl.when(pl.program_id(1) == 0)
        def _():
            gw_ref[...] = jnp.zeros_like(gw_ref)

        for jj in range(N_COLBLK):
            @pl.when(j == jj)
            def _(jj=jj):
                gw_ref[...] += _dot(hn_ref[...], dp_refs[_DPROJ_OWNER[jj]][...], _TN)

    def part_spec(p):
        def index(j, i):
            mine = (j >= _DPROJ_FIRST[p]) & (j < _DPROJ_FIRST[p] + _DPROJ_BLOCKS[p])
            return jnp.where(mine, i, 0), _dproj_col(j, p)
        return pl.BlockSpec((tm, 1024), index)

    return pl.pallas_call(
        body, name="in_proj_bwd_w",
        grid=(N_COLBLK, t // tm),
        in_specs=[pl.BlockSpec((tm, D_MODEL), lambda j, i: (i, 0))]
                 + [part_spec(p) for p in range(n_parts)],
        out_specs=pl.BlockSpec((D_MODEL, 1024), lambda j, i: (0, j)),
        out_shape=jax.ShapeDtypeStruct((D_MODEL, D_IN_PAD), F32),
        compiler_params=_params(2),
    )(hn, *d_parts)


def _adamw(parts, w, m, v, tr, name):
    rows, cols = w.shape
    c1 = 1.0 - ADAM_B1 ** ADAM_STEP
    c2 = 1.0 - ADAM_B2 ** ADAM_STEP

    def body(p_ref, w_ref, m_ref, v_ref, g_ref, d_ref, nm_ref, nv_ref):
        g = p_ref[0].astype(F32)
        for s in range(1, N_DEV):
            g = g + p_ref[s].astype(F32)
        nm = ADAM_B1 * m_ref[...] + (1.0 - ADAM_B1) * g
        nv = ADAM_B2 * v_ref[...] + (1.0 - ADAM_B2) * (g * g)
        g_ref[...] = g
        nm_ref[...] = nm
        nv_ref[...] = nv
        d_ref[...] = -ADAM_LR * ((nm / c1) / (jnp.sqrt(nv / c2) + ADAM_EPS) + ADAM_WD * w_ref[...])

    blk = pl.BlockSpec((tr, cols), lambda i: (i, 0))
    shape = jax.ShapeDtypeStruct((rows, cols), F32)
    return pl.pallas_call(
        body, name=name,
        grid=(rows // tr,),
        in_specs=[pl.BlockSpec((N_DEV, tr, cols), lambda i: (0, i, 0)), blk, blk, blk],
        out_specs=[blk, blk, blk, blk],
        out_shape=[shape, shape, shape, shape],
        compiler_params=_params(1),
    )(parts, w, m, v)


def _mesh_place():
    x, y, c = lax.axis_index("x"), lax.axis_index("y"), lax.axis_index("c")
    return x, y, c, 4 * x + 2 * y + c


def _peer(x, y, c, k):
    px = 1 - x if k & 4 else x
    py = 1 - y if k & 2 else y
    pc = 1 - c if k & 1 else c
    return (px, py, pc), 4 * px + 2 * py + pc


def _exchange(srcs, scatter, name):
    n = len(srcs)

    def body(*refs):
        copies = _exchange_copies(refs[:n], refs[n:2 * n], scatter, *refs[2 * n:])
        _exchange_start(copies)
        _exchange_wait(copies)

    return pl.pallas_call(
        body, name=name,
        in_specs=[_ANY] * n, out_specs=[_ANY] * n, out_shape=_exchange_shapes(srcs, scatter),
        scratch_shapes=_exchange_sems(n),
    )(*srcs)


_ANY = pl.BlockSpec(memory_space=pl.ANY)


def _exchange_shapes(srcs, scatter):
    return [jax.ShapeDtypeStruct(s.shape if sc else (N_DEV,) + s.shape, s.dtype)
            for s, sc in zip(srcs, scatter)]


def _exchange_sems(n):
    return [pltpu.SemaphoreType.DMA((n * (N_DEV - 1),)),
            pltpu.SemaphoreType.DMA((n * (N_DEV - 1),)),
            pltpu.SemaphoreType.DMA((n,))]


def _exchange_copies(src_refs, dst_refs, scatter, send_sems, recv_sems, loc_sems):
    n = len(src_refs)
    x, y, c, me = _mesh_place()

    def src_of(i, idx):
        return src_refs[i].at[idx] if scatter[i] else src_refs[i]

    local = [pltpu.make_async_copy(src_of(i, me), dst_refs[i].at[me], loc_sems.at[i])
             for i in range(n)]
    sends, recvs = [], []
    for k in range(1, N_DEV):
        peer, pidx = _peer(x, y, c, k)
        for i in range(n):
            s = i * (N_DEV - 1) + k - 1
            for dst_slab, group in ((me, sends), (pidx, recvs)):
                group.append(pltpu.make_async_remote_copy(
                    src_ref=src_of(i, pidx), dst_ref=dst_refs[i].at[dst_slab],
                    send_sem=send_sems.at[s], recv_sem=recv_sems.at[s],
                    device_id=peer, device_id_type=pl.DeviceIdType.MESH))
    return local, sends, recvs


def _exchange_start(copies):
    local, sends, _ = copies
    for cp in local + sends:
        cp.start()


def _exchange_wait(copies):
    local, sends, recvs = copies
    for cp in recvs:
        cp.wait_recv()
    for cp in sends:
        cp.wait_send()
    for cp in local:
        cp.wait()


def _pad_lanes(v, width=LANES):
    return jnp.pad(v, ((0, 0), (0, width - v.shape[1])))


def _local_step(x, target, norm_w, w_in_b, q_norm_w, k_norm_w, conv_w, conv_b, dt_bias, a_log,
                d_skip, sb_norm_w, ssd_norm_w, w_out_b, tm=512, tq=512, tmid=256, blk=ATT_BLK,
                scatter=False):
    nb, seq, _ = x.shape
    t = nb * seq
    x2 = x.reshape(t, D_MODEL)
    tg2 = target.reshape(t, D_MODEL)
    qw2 = jnp.tile(q_norm_w, (1, 2))
    kw2 = jnp.tile(k_norm_w, (1, 2))
    dtb, alog, dsk = _pad_lanes(dt_bias), _pad_lanes(a_log), _pad_lanes(d_skip)

    tproj = min(2 * tm, t)
    proj, hn = _in_proj(x2, norm_w, w_in_b, tproj)
    qs, kn, vb, kt = _qk_prep(proj, qw2, kw2, nb, seq, tq)
    o_sb, sb_tot, sb_low = _attn_fwd(qs, kn, vb, nb, seq, blk)
    y_ssd, states = _ssd_fwd(proj, conv_w, conv_b, dtb, alog, dsk, nb, seq)
    d_out, d_osb, d_y, d_z, g_wout, g_sbw, g_ssdw, loss = _mid(
        o_sb, y_ssd, proj, x2, tg2, sb_norm_w, ssd_norm_w, w_out_b, tmid)
    dqs, dkn, dvh = _attn_bwd(qs, kn, kt, vb, sb_tot, sb_low, d_osb, nb, seq, blk)
    dq_raw, dk_raw, dv_raw, g_qw, g_kw = _qk_bwd(proj, dqs, dkn, dvh, qw2, kw2, nb, seq, tq)
    d_xbc, g_cw, g_cb, g_dtb, g_alog, g_dsk = _ssd_bwd(
        proj, d_y, states, conv_w, conv_b, dtb, alog, dsk, nb, seq)
    d_parts = [dq_raw, dk_raw, dv_raw, d_z, d_xbc]
    g_win = _in_proj_bwd_w(hn, d_parts, tm)[:, :D_IN]
    g_cw = g_cw[:CONV_TAPS]
    if scatter:
        grad_x, g_nw, g_win, g_wout, g_cw = _in_proj_bwd_x(
            d_parts, w_in_b, x2, d_out, norm_w, tm, _grad_slabs(g_win, g_wout, g_cw))
    else:
        grad_x, g_nw = _in_proj_bwd_x(d_parts, w_in_b, x2, d_out, norm_w, tm)

    small = dict(
        norm_w=g_nw,
        q_norm_w=g_qw[:, :HEAD_DIM] + g_qw[:, HEAD_DIM:],
        k_norm_w=g_kw[:, :HEAD_DIM] + g_kw[:, HEAD_DIM:],
        conv_b=g_cb, dt_bias=g_dtb[:, :N_HEADS], A_log=g_alog[:, :N_HEADS],
        D_skip=g_dsk[:, :N_HEADS], sb_norm_w=g_sbw, ssd_norm_w=g_ssdw)
    return loss[0, 0], grad_x.reshape(nb, seq, D_MODEL), g_win, g_wout, g_cw, small


def _grad_slabs(g_win, g_wout, g_cw):
    w_sh = D_IN // N_DEV
    c_sh = D_CONV // N_DEV
    return (jnp.transpose(g_win.reshape(D_MODEL, N_DEV, w_sh), (1, 0, 2)).astype(BF16),
            g_wout.reshape(N_DEV, 2 * D_BRANCH // N_DEV, D_MODEL).astype(BF16),
            jnp.pad(jnp.transpose(g_cw.reshape(CONV_TAPS, N_DEV, c_sh), (1, 0, 2)),
                    ((0, 0), (0, 8 - CONV_TAPS), (0, 0))))


_SMALL = ("norm_w", "q_norm_w", "k_norm_w", "conv_b", "dt_bias", "A_log", "D_skip",
          "sb_norm_w", "ssd_norm_w")


def _pack_small(vals):
    rows = [_pad_lanes(vals[n], -(-vals[n].shape[1] // LANES) * LANES).reshape(-1, LANES)
            for n in _SMALL]
    packed = jnp.concatenate(rows, axis=0)
    return jnp.pad(packed, ((0, 48 - packed.shape[0]), (0, 0)))


def _unpack_small(packed, like):
    out, r = {}, 0
    for n in _SMALL:
        width = like[n].shape[1]
        nr = -(-width // LANES)
        out[n] = packed[r:r + nr].reshape(1, nr * LANES)[:, :width]
        r += nr
    return out


def kernel(x, norm_w, w_in, q_norm_w, k_norm_w, conv_w, conv_b, dt_bias, A_log, D_skip, sb_norm_w, ssd_norm_w, w_out, loss_target, m_norm_w, m_w_in, m_q_norm_w, m_k_norm_w, m_conv_w, m_conv_b, m_dt_bias, m_A_log, m_D_skip, m_sb_norm_w, m_ssd_norm_w, m_w_out, v_norm_w, v_w_in, v_q_norm_w, v_k_norm_w, v_conv_w, v_conv_b, v_dt_bias, v_A_log, v_D_skip, v_sb_norm_w, v_ssd_norm_w, v_w_out):
    win_all, wout_all, cw_all = _exchange(
        [w_in[0].astype(BF16), w_out[0].astype(BF16), conv_w[0]], [False, False, False],
        "gather_weights")
    w_in_b = jnp.pad(jnp.transpose(win_all, (1, 0, 2)).reshape(D_MODEL, D_IN),
                     ((0, 0), (0, D_IN_PAD - D_IN)))
    w_out_b = wout_all.reshape(2 * D_BRANCH, D_MODEL)
    conv_full = jnp.transpose(cw_all, (1, 0, 2)).reshape(CONV_TAPS, D_CONV)

    loss, grad_x, win_parts, wout_parts, cw_parts, g_small = _local_step(
        x, loss_target, norm_w, w_in_b, q_norm_w, k_norm_w, conv_full, conv_b, dt_bias, A_log,
        D_skip, sb_norm_w, ssd_norm_w, w_out_b, scatter=True)
    small_parts, = _exchange([_pack_small(g_small)], [False], "gather_small_grads")

    small_w = dict(norm_w=norm_w, q_norm_w=q_norm_w, k_norm_w=k_norm_w, conv_b=conv_b,
                   dt_bias=dt_bias, A_log=A_log, D_skip=D_skip, sb_norm_w=sb_norm_w,
                   ssd_norm_w=ssd_norm_w)
    small_m = dict(norm_w=m_norm_w, q_norm_w=m_q_norm_w, k_norm_w=m_k_norm_w, conv_b=m_conv_b,
                   dt_bias=m_dt_bias, A_log=m_A_log, D_skip=m_D_skip, sb_norm_w=m_sb_norm_w,
                   ssd_norm_w=m_ssd_norm_w)
    small_v = dict(norm_w=v_norm_w, q_norm_w=v_q_norm_w, k_norm_w=v_k_norm_w, conv_b=v_conv_b,
                   dt_bias=v_dt_bias, A_log=v_A_log, D_skip=v_D_skip, sb_norm_w=v_sb_norm_w,
                   ssd_norm_w=v_ssd_norm_w)

    pad8 = lambda a: jnp.pad(a, ((0, 8 - CONV_TAPS), (0, 0)))
    r_win = _adamw(win_parts, w_in[0], m_w_in[0], v_w_in[0], 128, "adamw_w_in")
    r_wout = _adamw(wout_parts, w_out[0], m_w_out[0], v_w_out[0], 128, "adamw_w_out")
    r_cw = _adamw(cw_parts, pad8(conv_w[0]), pad8(m_conv_w[0]), pad8(v_conv_w[0]), 8, "adamw_conv_w")
    r_small = _adamw(small_parts, _pack_small(small_w), _pack_small(small_m),
                     _pack_small(small_v), 48, "adamw_small")

    loss = lax.psum(loss, ("x", "y", "c"))
    res = {"w_in": [a[None] for a in r_win], "w_out": [a[None] for a in r_wout],
           "conv_w": [a[:CONV_TAPS][None] for a in r_cw]}
    unpacked = [_unpack_small(a, small_w) for a in r_small]
    for n in _SMALL:
        res[n] = [u[n] for u in unpacked]
    order = ("norm_w", "w_in", "q_norm_w", "k_norm_w", "conv_w", "conv_b", "dt_bias", "A_log",
             "D_skip", "sb_norm_w", "ssd_norm_w", "w_out")
    outs = [loss, grad_x]
    for kind in range(4):
        outs += [res[n][kind] for n in order]
    return tuple(outs)
```

```python
import functools
import math

import jax
import jax.numpy as jnp
from jax import lax
from jax.experimental import pallas as pl
from jax.experimental.pallas import tpu as pltpu

F32 = jnp.float32
BF16 = jnp.bfloat16
HIGHEST = lax.Precision.HIGHEST

D_MODEL = 1024
N_HEADS = 16
HEAD_DIM = 64
N_PAIRS = N_HEADS // 2
D_BRANCH = 1024
N_GROUPS = 2
HEADS_PER_GROUP = 8
D_STATE = 128
GROUP_W = HEADS_PER_GROUP * HEAD_DIM
D_BC = 2 * N_GROUPS * D_STATE
D_CONV = D_BRANCH + D_BC
D_IN = 6672
D_IN_PAD = 7168
N_COLBLK = D_IN_PAD // 1024
COL_XS = 5120
COL_BC = 6144
COL_DT = 6656
EPS = 1e-6
CONV_TAPS = 4
N_DEV = 8

LANES = 128
SSD_CHUNK = 128
ATT_BLK = 256
EXP_UNDERFLOW = -105.0
VMEM_LIMIT = 56 * 1024 * 1024

ADAM_LR = 0.001
ADAM_B1 = 0.9
ADAM_B2 = 0.999
ADAM_EPS = 1e-08
ADAM_WD = 0.01
ADAM_STEP = 10

_NT = (((1,), (1,)), ((), ()))
_TN = (((0,), (0,)), ((), ()))


def _params(n_grid):
    return pltpu.CompilerParams(dimension_semantics=("arbitrary",) * n_grid,
                                vmem_limit_bytes=VMEM_LIMIT)


def _dot(a, b, dims=None, precision=None):
    if dims is None:
        return jnp.dot(a, b, preferred_element_type=F32, precision=precision)
    return lax.dot_general(a, b, dims, preferred_element_type=F32, precision=precision)


def _sigmoid(x):
    return 1.0 / (1.0 + jnp.exp(-x))


def _softplus(x):
    return jnp.maximum(x, 0.0) + jnp.log(1.0 + jnp.exp(-jnp.abs(x)))


def _split_bf16(x):
    hi = x.astype(BF16)
    lo = (x - hi.astype(F32)).astype(BF16)
    return hi, lo


def _lane_iota(shape):
    return lax.broadcasted_iota(jnp.int32, shape, len(shape) - 1)


def _row_iota(shape):
    return lax.broadcasted_iota(jnp.int32, shape, len(shape) - 2)


def _pair_sum(x):
    r = lax.broadcasted_iota(jnp.int32, (LANES, LANES), 0)
    c = lax.broadcasted_iota(jnp.int32, (LANES, LANES), 1)
    same_head = jnp.where(r // HEAD_DIM == c // HEAD_DIM, 1.0, 0.0).astype(BF16)
    hi, lo = _split_bf16(x)
    return _dot(hi, same_head) + _dot(lo, same_head)


def _pair_head(x, a):
    lane = _lane_iota(x.shape)
    mine = (lane < HEAD_DIM) if a == 0 else (lane >= HEAD_DIM)
    return jnp.where(mine, x, jnp.zeros_like(x))


def _head_expand():
    r = lax.broadcasted_iota(jnp.int32, (LANES, D_BRANCH), 0)
    c = lax.broadcasted_iota(jnp.int32, (LANES, D_BRANCH), 1)
    return jnp.where(c // HEAD_DIM == r, 1.0, 0.0).astype(F32)


def _in_proj(x2, norm_w, w_in_b, tm, shards=()):
    t = x2.shape[0]
    n_sh = len(shards)
    n_rows = t // tm

    def body(x_ref, nw_ref, w_ref, *rest):
        src_refs, (proj_ref, hn_ref) = rest[:n_sh], rest[n_sh:n_sh + 2]
        dst_refs, sems = rest[n_sh + 2:2 * n_sh + 2], rest[2 * n_sh + 2:]
        i, j = pl.program_id(0), pl.program_id(1)

        if n_sh:
            @pl.when((i == 0) & (j == 0))
            def _():
                _exchange_start(_exchange_copies(src_refs, dst_refs, (False,) * n_sh, *sems))

            @pl.when((i == n_rows - 1) & (j == N_COLBLK - 1))
            def _():
                _exchange_wait(_exchange_copies(src_refs, dst_refs, (False,) * n_sh, *sems))

        @pl.when(j == 0)
        def _():
            xf = x_ref[...]
            r = lax.rsqrt(jnp.mean(xf * xf, axis=1, keepdims=True) + EPS)
            hn_ref[...] = (xf * r * nw_ref[...]).astype(BF16)

        proj_ref[...] = _dot(hn_ref[...], w_ref[...])

    return pl.pallas_call(
        body, name="in_proj",
        grid=(n_rows, N_COLBLK),
        in_specs=[pl.BlockSpec((tm, D_MODEL), lambda i, j: (i, 0)),
                  pl.BlockSpec((1, D_MODEL), lambda i, j: (0, 0)),
                  pl.BlockSpec((D_MODEL, 1024), lambda i, j: (0, j))] + [_ANY] * n_sh,
        out_specs=[pl.BlockSpec((tm, 1024), lambda i, j: (i, j)),
                   pl.BlockSpec((tm, D_MODEL), lambda i, j: (i, 0))] + [_ANY] * n_sh,
        out_shape=[jax.ShapeDtypeStruct((t, D_IN_PAD), F32),
                   jax.ShapeDtypeStruct((t, D_MODEL), BF16)]
                  + _exchange_shapes(shards, (False,) * n_sh),
        scratch_shapes=_exchange_sems(n_sh) if n_sh else [],
        compiler_params=_params(2),
    )(x2, norm_w, w_in_b, *shards)


def _qk_prep(proj, qw2, kw2, nb, seq, tq):
    nl = seq // tq
    scale = 1.0 / math.sqrt(HEAD_DIM)

    def body(q_ref, k_ref, v_ref, qw_ref, kw_ref, qs_ref, kn_ref, vb_ref, kt_ref):
        def norm(x, w):
            r = lax.rsqrt(_pair_sum(x * x) * (1.0 / HEAD_DIM) + EPS)
            return x * r * w

        kn = norm(k_ref[...], kw_ref[...])
        qs_ref[...] = (norm(q_ref[...], qw_ref[...]) * scale).astype(BF16)
        kn_ref[...] = kn.astype(BF16)
        vb_ref[...] = v_ref[...].astype(BF16)
        kt_ref[0, 0] = kn.T.astype(BF16)

    tok_shape = jax.ShapeDtypeStruct((nb * seq, D_BRANCH), BF16)
    tok = pl.BlockSpec((tq, LANES), lambda b, i, h: (b * nl + i, h))
    return pl.pallas_call(
        body, name="qk_prep",
        grid=(nb, nl, N_PAIRS),
        in_specs=[pl.BlockSpec((tq, LANES), lambda b, i, h: (b * nl + i, h)),
                  pl.BlockSpec((tq, LANES), lambda b, i, h: (b * nl + i, N_PAIRS + h)),
                  pl.BlockSpec((tq, LANES), lambda b, i, h: (b * nl + i, 2 * N_PAIRS + h)),
                  pl.BlockSpec((1, LANES), lambda b, i, h: (0, 0)),
                  pl.BlockSpec((1, LANES), lambda b, i, h: (0, 0))],
        out_specs=[tok, tok, tok,
                   pl.BlockSpec((1, 1, LANES, tq), lambda b, i, h: (b, h, 0, i))],
        out_shape=[tok_shape, tok_shape, tok_shape,
                   jax.ShapeDtypeStruct((nb, N_PAIRS, LANES, seq), BF16)],
        compiler_params=_params(3),
    )(proj, proj, proj, qw2, kw2)


def _attn_fwd(qs, kn, vb, nb, seq, blk):
    nq = seq // blk

    def body(q_ref, k_ref, v_ref, o_ref, tot_ref, low_ref, kmax_ref):
        qi = pl.program_id(2)
        r_i = lax.broadcasted_iota(jnp.int32, (blk, blk), 0)
        c_i = lax.broadcasted_iota(jnp.int32, (blk, blk), 1)
        csum = jnp.where(r_i >= c_i, 1.0, 0.0).astype(BF16)
        causal = c_i < r_i
        heads = range(2)

        head = _pair_head

        @pl.when(qi == 0)
        def _():
            kk = k_ref[...].astype(F32)
            for a in heads:
                ksq = jnp.sum(head(kk * kk, a), axis=1, keepdims=True)
                kmax_ref[a] = jnp.full((8, LANES), jnp.max(ksq))

        q_pair = q_ref[...]
        qf = q_pair.astype(F32)
        q_head = [head(q_pair, a) for a in heads]
        zmax = []
        for a in heads:
            qsq = jnp.sum(head(qf * qf, a), axis=1, keepdims=True)
            zmax.append(1.01 * jnp.sqrt(qsq * kmax_ref[a][0:1, 0:1]) + 0.01)

        def exhausted(run):
            top = jnp.maximum(jnp.max(run[0] + zmax[0]), jnp.max(run[1] + zmax[1]))
            return top < EXP_UNDERFLOW

        def sweep(blocks, run, acc):
            offs = [pl.multiple_of(j * blk, blk) for j, _, _ in blocks]
            z = [[_dot(q_head[a], k_ref[pl.ds(off, blk), :], _NT) for a in heads]
                 for off in offs]
            cl = []
            for (_, diag, valid), zb in zip(blocks, z):
                split = []
                for a in heads:
                    lk = -_softplus(zb[a])
                    if diag:
                        lk = jnp.where(causal, lk, 0.0)
                    if valid is not None:
                        lk = jnp.where(valid, lk, 0.0)
                    split.append(_split_bf16(lk))
                cl.append([_dot(split[a][0], csum) + _dot(split[a][1], csum) for a in heads])
            for (_, diag, valid), zb, clb, off in zip(blocks, z, cl, offs):
                w = []
                for a in heads:
                    wa = jnp.exp(zb[a] + clb[a] + run[a])
                    if diag:
                        wa = jnp.where(causal, wa, 0.0)
                    if valid is not None:
                        wa = jnp.where(valid, wa, 0.0)
                    w.append(wa.astype(BF16))
                run = [run[a] + clb[a][:, 0:1] for a in heads]
                v_pair = v_ref[pl.ds(off, blk), :]
                acc = acc + _dot(w[0], head(v_pair, 0)) + _dot(w[1], head(v_pair, 1))
            return run, acc

        run = [jnp.zeros((blk, 1), F32)] * 2
        acc = jnp.zeros((blk, LANES), F32)
        run, acc = sweep([(qi, True, None), (jnp.maximum(qi - 1, 0), False, qi >= 1)], run, acc)
        low = jnp.maximum(qi - 1, 0)

        def more(carry):
            low, done, _, _ = carry
            return (low > 0) & jnp.logical_not(done)

        def pair(carry):
            low, _, run, acc = carry
            run, acc = sweep([(low - 1, False, None), (jnp.maximum(low - 2, 0), False, low >= 2)],
                             run, acc)
            return jnp.maximum(low - 2, 0), exhausted(run), run, acc

        low, _, run, acc = lax.while_loop(more, pair, (low, exhausted(run), run, acc))
        low_ref[pl.program_id(0) * N_PAIRS + pl.program_id(1), qi] = low.astype(F32)
        o_ref[...] = acc
        for a in heads:
            as_row = jnp.sum(jnp.where(r_i == c_i, run[a], 0.0), axis=0, keepdims=True)
            tot_ref[0, a, 0] = jnp.broadcast_to(as_row, (8, blk))

    return pl.pallas_call(
        body, name="sb_attn_fwd",
        grid=(nb, N_PAIRS, nq),
        in_specs=[pl.BlockSpec((blk, LANES), lambda b, h, i: (b * nq + i, h)),
                  pl.BlockSpec((seq, LANES), lambda b, h, i: (b, h)),
                  pl.BlockSpec((seq, LANES), lambda b, h, i: (b, h))],
        out_specs=[pl.BlockSpec((blk, LANES), lambda b, h, i: (b * nq + i, h)),
                   pl.BlockSpec((1, 2, 1, 8, blk), lambda b, h, i: (b, h, i, 0, 0)),
                   pl.BlockSpec(memory_space=pltpu.SMEM)],
        out_shape=[jax.ShapeDtypeStruct((nb * seq, D_BRANCH), F32),
                   jax.ShapeDtypeStruct((nb, N_HEADS, nq, 8, blk), F32),
                   jax.ShapeDtypeStruct((nb * N_PAIRS, nq), F32)],
        scratch_shapes=[pltpu.VMEM((2, 8, LANES), F32)],
        compiler_params=_params(3),
    )(qs, kn, vb)


def _attn_bwd(qs, kn, kt, vb, tot, low, d_o, nb, seq, blk):
    nq = seq // blk

    def body(q_ref, k_ref, kt_ref, v_ref, tot_ref, low_ref, do_ref, dq_ref, dk_ref, dv_ref):
        qi = pl.program_id(2)

        @pl.when(qi == 0)
        def _():
            dk_ref[...] = jnp.zeros_like(dk_ref)
            dv_ref[...] = jnp.zeros_like(dv_ref)

        r_i = lax.broadcasted_iota(jnp.int32, (blk, blk), 0)
        c_i = lax.broadcasted_iota(jnp.int32, (blk, blk), 1)
        before = jnp.where(c_i < r_i, 1.0, 0.0).astype(BF16)
        upto = jnp.where(c_i <= r_i, 1.0, 0.0).astype(BF16)
        causal = r_i < c_i

        heads = range(2)
        q_head = [_pair_head(q_ref[...], a) for a in heads]
        d_ob = [_pair_head(do_ref[...].astype(BF16), a) for a in heads]
        total = [tot_ref[0, a, 0][0:1, :] for a in heads]

        def sweep(blocks, lsum, esum, dqt):
            def keep(x, diag, valid):
                if diag:
                    x = jnp.where(causal, x, 0.0)
                if valid is not None:
                    x = jnp.where(valid, x, 0.0)
                return x

            offs = [pl.multiple_of(j * blk, blk) for j, _, _ in blocks]
            zt = [[_dot(k_ref[pl.ds(off, blk), :], q_head[a], _NT) for a in heads]
                  for off in offs]
            dwt = [[_dot(v_ref[pl.ds(off, blk), :], d_ob[a], _NT) for a in heads]
                   for off in offs]
            sp, lk, lpre = [], [], []
            for (_, diag, valid), ztb in zip(blocks, zt):
                sp.append([_softplus(ztb[a]) for a in heads])
                lk.append([keep(-sp[-1][a], diag, valid) for a in heads])
                split = [_split_bf16(lk[-1][a]) for a in heads]
                lpre.append([_dot(before, split[a][0]) + _dot(before, split[a][1]) for a in heads])
            wt, et, epre = [], [], []
            for i, (_, diag, valid) in enumerate(blocks):
                wt.append([keep(jnp.exp(zt[i][a] + (total[a] - lsum[a] - lpre[i][a])), diag, valid)
                           for a in heads])
                et.append([dwt[i][a] * wt[i][a] for a in heads])
                split = [_split_bf16(et[i][a]) for a in heads]
                epre.append([_dot(upto, split[a][0]) + _dot(upto, split[a][1]) for a in heads])
                lsum = [lsum[a] + lpre[i][a][blk - 1:blk, :] + lk[i][a][blk - 1:blk, :]
                        for a in heads]
            for i, (_, diag, valid) in enumerate(blocks):
                dzb = [keep(et[i][a] - jnp.exp(zt[i][a] - sp[i][a]) * (esum[a] + epre[i][a]),
                            diag, valid).astype(BF16) for a in heads]
                esum = [esum[a] + epre[i][a][blk - 1:blk, :] for a in heads]
                dk_ref[pl.ds(offs[i], blk), :] += (_dot(dzb[0], q_head[0]) + _dot(dzb[1], q_head[1]))
                dv_ref[pl.ds(offs[i], blk), :] += (_dot(wt[i][0].astype(BF16), d_ob[0])
                                                   + _dot(wt[i][1].astype(BF16), d_ob[1]))
                kt_pair = kt_ref[0, 0, :, pl.ds(offs[i], blk)]
                dqt = [dqt[a] + _dot(kt_pair, dzb[a]) for a in heads]
            return lsum, esum, dqt

        row = [jnp.zeros((1, blk), F32)] * 2
        dqt = [jnp.zeros((LANES, blk), F32)] * 2
        low = low_ref[pl.program_id(0) * N_PAIRS + pl.program_id(1), qi].astype(jnp.int32)
        low = jnp.clip(low, 0, jnp.maximum(qi - 1, 0))

        def pair(carry):
            j, lsum, esum, dqt = carry
            return (j + 2,) + sweep([(j, False, None), (j + 1, False, j + 1 < qi - 1)],
                                    lsum, esum, dqt)

        _, lsum, esum, dqt = lax.while_loop(lambda c: c[0] < qi - 1, pair, (low, row, row, dqt))
        _, _, dqt = sweep([(jnp.maximum(qi - 1, 0), False, qi >= 1), (qi, True, None)],
                          lsum, esum, dqt)
        top = _row_iota((LANES, blk)) < HEAD_DIM
        dq_ref[...] = jnp.where(top, dqt[0], dqt[1]).T

    seq_blk = pl.BlockSpec((seq, LANES), lambda b, h, i: (b, h))
    tok = pl.BlockSpec((blk, LANES), lambda b, h, i: (b * nq + i, h))
    tok_shape = jax.ShapeDtypeStruct((nb * seq, D_BRANCH), F32)
    return pl.pallas_call(
        body, name="sb_attn_bwd",
        grid=(nb, N_PAIRS, nq),
        in_specs=[tok, seq_blk,
                  pl.BlockSpec((1, 1, LANES, seq), lambda b, h, i: (b, h, 0, 0)),
                  seq_blk,
                  pl.BlockSpec((1, 2, 1, 8, blk), lambda b, h, i: (b, h, i, 0, 0)),
                  pl.BlockSpec(memory_space=pltpu.SMEM),
                  tok],
        out_specs=[tok, seq_blk, seq_blk],
        out_shape=[tok_shape, tok_shape, tok_shape],
        compiler_params=_params(3),
    )(qs, kn, kt, vb, tot, low, d_o)


def _qk_bwd(proj, dqs, dkn, dvh, qw2, kw2, nb, seq, tq):
    nl = seq // tq
    scale = 1.0 / math.sqrt(HEAD_DIM)

    def body(q_ref, k_ref, dq_ref, dk_ref, dv_ref, qw_ref, kw_ref,
             dqr_ref, dkr_ref, dvr_ref, gq_ref, gk_ref):
        @pl.when((pl.program_id(0) == 0) & (pl.program_id(1) == 0) & (pl.program_id(2) == 0))
        def _():
            gq_ref[...] = jnp.zeros_like(gq_ref)
            gk_ref[...] = jnp.zeros_like(gk_ref)

        def norm_bwd(x, w, dy):
            r = lax.rsqrt(_pair_sum(x * x) * (1.0 / HEAD_DIM) + EPS)
            xhat = x * r
            g = dy * w
            m = _pair_sum(g * xhat) * (1.0 / HEAD_DIM)
            return r * (g - xhat * m), jnp.sum(dy * xhat, axis=0, keepdims=True)

        dqr, gq = norm_bwd(q_ref[...], qw_ref[...], dq_ref[...] * scale)
        dkr, gk = norm_bwd(k_ref[...], kw_ref[...], dk_ref[...])
        dqr_ref[...] = dqr.astype(BF16)
        dkr_ref[...] = dkr.astype(BF16)
        dvr_ref[...] = dv_ref[...].astype(BF16)
        gq_ref[...] += gq
        gk_ref[...] += gk

    tok = lambda off: pl.BlockSpec((tq, LANES), lambda b, i, h: (b * nl + i, off + h))
    vec = pl.BlockSpec((1, LANES), lambda b, i, h: (0, 0))
    tshape = jax.ShapeDtypeStruct((nb * seq, D_BRANCH), BF16)
    return pl.pallas_call(
        body, name="qk_bwd",
        grid=(nb, nl, N_PAIRS),
        in_specs=[tok(0), tok(N_PAIRS), tok(0), tok(0), tok(0), vec, vec],
        out_specs=[tok(0), tok(0), tok(0), vec, vec],
        out_shape=[tshape, tshape, tshape,
                   jax.ShapeDtypeStruct((1, LANES), F32), jax.ShapeDtypeStruct((1, LANES), F32)],
        compiler_params=_params(3),
    )(proj, proj, dqs, dkn, dvh, qw2, kw2)


def _shift_down(cur, prev, k):
    if k == 0:
        return cur
    rows = _row_iota(cur.shape)
    return jnp.where(rows < k, pltpu.roll(prev, k, axis=0), pltpu.roll(cur, k, axis=0))


def _shift_up(cur, nxt, k):
    if k == 0:
        return cur
    n = cur.shape[0]
    rows = _row_iota(cur.shape)
    return jnp.where(rows < n - k, pltpu.roll(cur, n - k, axis=0), pltpu.roll(nxt, n - k, axis=0))


def _conv_pre(cur, prev, w, b):
    out = b
    for i in range(CONV_TAPS):
        out = out + _shift_down(cur, prev, CONV_TAPS - 1 - i) * w[i:i + 1, :]
    return out


def _silu(x):
    return x * _sigmoid(x)


def _silu_grad(x):
    s = _sigmoid(x)
    return s * (1.0 + x * (1.0 - s))


def _chunk_decay(dt_raw, dtb, alog, expand, qc):
    dt = _softplus(dt_raw + dtb)
    d_a = dt * (-jnp.exp(alog))
    r_i = lax.broadcasted_iota(jnp.int32, (qc, qc), 0)
    c_i = lax.broadcasted_iota(jnp.int32, (qc, qc), 1)
    tril = r_i >= c_i
    a_cs = _dot(jnp.where(tril, 1.0, 0.0).astype(F32), d_a, precision=HIGHEST)
    dt_x = _dot(dt, expand, precision=HIGHEST)
    acs_x = _dot(a_cs, expand, precision=HIGHEST)
    return dt, d_a, a_cs, dt_x, acs_x, tril


def _ssd_fwd(proj, conv_w, conv_b, dtb, alog, dskip, nb, seq):
    qc = SSD_CHUNK
    nc = seq // qc

    def body(xs_ref, bc_ref, dt_ref, cw_ref, cb_ref, dtb_ref, al_ref, ds_ref,
             y_ref, st_ref, pxs_ref, pbc_ref, state_ref):
        @pl.when(pl.program_id(1) == 0)
        def _():
            pxs_ref[...] = jnp.zeros_like(pxs_ref)
            pbc_ref[...] = jnp.zeros_like(pbc_ref)
            state_ref[...] = jnp.zeros_like(state_ref)

        expand = _head_expand()
        xs_raw = xs_ref[...]
        bc_raw = bc_ref[...]
        cw = cw_ref[...]
        cb = cb_ref[...]
        xs = _silu(_conv_pre(xs_raw, pxs_ref[...], cw[:, :D_BRANCH], cb[:, :D_BRANCH]))
        bc = _silu(_conv_pre(bc_raw, pbc_ref[...], cw[:, D_BRANCH:], cb[:, D_BRANCH:]))
        pxs_ref[...] = xs_raw
        pbc_ref[...] = bc_raw

        dt, d_a, a_cs, dt_x, acs_x, tril = _chunk_decay(
            dt_ref[...], dtb_ref[...], al_ref[...], expand, qc)
        a_cst = a_cs.T
        aend_x = acs_x[qc - 1:qc, :]
        ea_x = jnp.exp(acs_x)
        dec_x = jnp.exp(aend_x - acs_x)
        xt = xs * dt_x
        xtb = xt.astype(BF16)
        xdb = (xt * dec_x).astype(BF16)
        d_x = _dot(jnp.broadcast_to(ds_ref[...], (8, LANES)), expand, precision=HIGHEST)[0:1, :]
        st_ref[0, 0] = state_ref[...]

        for g in range(N_GROUPS):
            gs = slice(g * GROUP_W, (g + 1) * GROUP_W)
            bg = bc[:, g * D_STATE:(g + 1) * D_STATE]
            cg = bc[:, (N_GROUPS + g) * D_STATE:(N_GROUPS + g + 1) * D_STATE]
            bgb = bg.astype(BF16)
            cgb = cg.astype(BF16)
            cbm = _dot(cgb, bgb, _NT)
            st_in = state_ref[g]
            y_off = _dot(cgb, st_in.astype(BF16)) * ea_x[:, gs]
            for k in range(HEADS_PER_GROUP):
                h = g * HEADS_PER_GROUP + k
                hs = slice(h * HEAD_DIM, (h + 1) * HEAD_DIM)
                seg = a_cs[:, h:h + 1] - a_cst[h:h + 1, :]
                gh = cbm * jnp.exp(jnp.where(tril, seg, -1e30))
                y_h = _dot(gh.astype(BF16), xtb[:, hs]) + y_off[:, k * HEAD_DIM:(k + 1) * HEAD_DIM]
                y_ref[:, hs] = y_h + d_x[:, hs] * xs[:, hs]
            state_ref[g] = st_in * jnp.exp(aend_x[:, gs]) + _dot(bg.T.astype(BF16), xdb[:, gs])

    nblk = lambda w, off: pl.BlockSpec((qc, w), lambda b, c: (b * nc + c, off))
    full = lambda r, w: pl.BlockSpec((r, w), lambda b, c: (0, 0))
    return pl.pallas_call(
        body, name="ssd_fwd",
        grid=(nb, nc),
        in_specs=[nblk(D_BRANCH, COL_XS // D_BRANCH), nblk(D_BC, COL_BC // D_BC),
                  nblk(LANES, COL_DT // LANES),
                  full(CONV_TAPS, D_CONV), full(1, D_CONV), full(1, LANES), full(1, LANES),
                  full(1, LANES)],
        out_specs=[pl.BlockSpec((qc, D_BRANCH), lambda b, c: (b * nc + c, 0)),
                   pl.BlockSpec((1, 1, N_GROUPS, D_STATE, GROUP_W), lambda b, c: (b, c, 0, 0, 0))],
        out_shape=[jax.ShapeDtypeStruct((nb * seq, D_BRANCH), F32),
                   jax.ShapeDtypeStruct((nb, nc, N_GROUPS, D_STATE, GROUP_W), F32)],
        scratch_shapes=[pltpu.VMEM((qc, D_BRANCH), F32), pltpu.VMEM((qc, D_BC), F32),
                        pltpu.VMEM((N_GROUPS, D_STATE, GROUP_W), F32)],
        compiler_params=_params(2),
    )(proj, proj, proj, conv_w, conv_b, dtb, alog, dskip)


def _ssd_bwd(proj, d_y, states, conv_w, conv_b, dtb, alog, dskip, nb, seq):
    qc = SSD_CHUNK
    nc = seq // qc

    def body(xs_ref, bc_ref, dt_ref, pxs_ref, pbc_ref, dy_ref, st_ref, stn_ref,
             cw_ref, cb_ref, dtb_ref, al_ref, ds_ref,
             dx_ref, gcw_ref, gcb_ref, gdtb_ref, gal_ref, gds_ref,
             dst_ref, nxs_ref, nbc_ref, yd_ref, dxt_ref):
        step = pl.program_id(1)
        chunk = nc - 1 - step

        @pl.when(step == 0)
        def _():
            dst_ref[...] = jnp.zeros_like(dst_ref)
            nxs_ref[...] = jnp.zeros_like(nxs_ref)
            nbc_ref[...] = jnp.zeros_like(nbc_ref)

        @pl.when((pl.program_id(0) == 0) & (step == 0))
        def _():
            gcw_ref[...] = jnp.zeros_like(gcw_ref)
            gcb_ref[...] = jnp.zeros_like(gcb_ref)
            gdtb_ref[...] = jnp.zeros_like(gdtb_ref)
            gal_ref[...] = jnp.zeros_like(gal_ref)
            gds_ref[...] = jnp.zeros_like(gds_ref)

        expand = _head_expand()
        collapse = lambda v: _dot(v, expand, _NT, precision=HIGHEST)
        first = jnp.where(chunk == 0, 0.0, 1.0)
        xs_raw = xs_ref[...]
        bc_raw = bc_ref[...]
        pxs = pxs_ref[...] * first
        pbc = pbc_ref[...] * first
        cw = cw_ref[...]
        cb = cb_ref[...]
        pre_xs = _conv_pre(xs_raw, pxs, cw[:, :D_BRANCH], cb[:, :D_BRANCH])
        pre_bc = _conv_pre(bc_raw, pbc, cw[:, D_BRANCH:], cb[:, D_BRANCH:])
        xs = _silu(pre_xs)
        bc = _silu(pre_bc)

        dt_in = dt_ref[...] + dtb_ref[...]
        dt, d_a, a_cs, dt_x, acs_x, tril = _chunk_decay(
            dt_ref[...], dtb_ref[...], al_ref[...], expand, qc)
        a_cst = a_cs.T
        aend_x = acs_x[qc - 1:qc, :]
        ea_x = jnp.exp(acs_x)
        dec_x = jnp.exp(aend_x - acs_x)
        xt = xs * dt_x
        xtb = xt.astype(BF16)
        xdb = (xt * dec_x).astype(BF16)
        d_x = _dot(jnp.broadcast_to(ds_ref[...], (8, LANES)), expand, precision=HIGHEST)[0:1, :]

        dy = dy_ref[...]
        dyb = dy.astype(BF16)
        dyeab = (dy * ea_x).astype(BF16)
        gds_ref[...] += collapse(jnp.broadcast_to(jnp.sum(dy * xs, axis=0, keepdims=True),
                                                  (8, D_BRANCH)))[0:1, :]

        d_bc = []
        d_cc = []
        y_offs = []
        dxt_states = []
        end_terms = []
        for g in range(N_GROUPS):
            gs = slice(g * GROUP_W, (g + 1) * GROUP_W)
            bg = bc[:, g * D_STATE:(g + 1) * D_STATE]
            cg = bc[:, (N_GROUPS + g) * D_STATE:(N_GROUPS + g + 1) * D_STATE]
            bgb = bg.astype(BF16)
            cgb = cg.astype(BF16)
            cbm = _dot(cgb, bgb, _NT)
            st_in = st_ref[0, 0, g]
            st_inb = st_in.astype(BF16)
            d_st = dst_ref[g]
            d_stb = d_st.astype(BF16)
            y_offs.append(_dot(cgb, st_inb) * ea_x[:, gs])
            dxt_states.append(_dot(bgb, d_stb) * dec_x[:, gs])
            d_c = _dot(dyeab[:, gs], st_inb, _NT)
            d_b = _dot(xdb[:, gs], d_stb, _NT)
            d_cb = jnp.zeros((qc, qc), F32)
            for k in range(HEADS_PER_GROUP):
                h = g * HEADS_PER_GROUP + k
                hs = slice(h * HEAD_DIM, (h + 1) * HEAD_DIM)
                seg = a_cs[:, h:h + 1] - a_cst[h:h + 1, :]
                lh = jnp.exp(jnp.where(tril, seg, -1e30))
                ghb = (cbm * lh).astype(BF16)
                d_cb = d_cb + _dot(dyb[:, hs], xtb[:, hs], _NT) * lh
                yd_ref[:, hs] = _dot(ghb, xtb[:, hs])
                dxt_ref[:, hs] = _dot(ghb, dyb[:, hs], _TN)
            d_cbb = d_cb.astype(BF16)
            d_cc.append(d_c + _dot(d_cbb, bgb))
            d_bc.append(d_b + _dot(d_cbb, cgb, _TN))
            end_terms.append(jnp.sum(d_st * stn_ref[0, 0, g], axis=0, keepdims=True))
            dst_ref[g] = d_st * jnp.exp(aend_x[:, gs]) + _dot(cg.T.astype(BF16), dyeab[:, gs])

        y_off = jnp.concatenate(y_offs, axis=1)
        dxt_state = jnp.concatenate(dxt_states, axis=1)
        dxt = dxt_ref[...] + dxt_state
        last = jnp.where(chunk == nc - 1, 0.0, 1.0)
        end_c = collapse(jnp.broadcast_to(jnp.concatenate(end_terms, axis=1), (8, D_BRANCH)))[0:1, :]
        da_cs = collapse(dyb.astype(F32) * yd_ref[...] - dxt_ref[...] * xtb.astype(F32)
                         + dy * y_off - dxt_state * xt)
        da_cs = da_cs + jnp.where(_row_iota(da_cs.shape) == qc - 1, end_c * last, 0.0)
        triu = lax.broadcasted_iota(jnp.int32, (qc, qc), 0) <= lax.broadcasted_iota(jnp.int32, (qc, qc), 1)
        dd_a = _dot(jnp.where(triu, 1.0, 0.0).astype(F32), da_cs, precision=HIGHEST)
        ddt = dd_a * (-jnp.exp(al_ref[...])) + collapse(dxt * xs)
        head_lanes = _lane_iota(ddt.shape) < N_HEADS
        ddt_raw = jnp.where(head_lanes, ddt * _sigmoid(dt_in), 0.0)
        gal_ref[...] += jnp.sum(jnp.where(head_lanes, dd_a * d_a, 0.0), axis=0, keepdims=True)
        gdtb_ref[...] += jnp.sum(ddt_raw, axis=0, keepdims=True)

        dpre_xs = (dxt * dt_x + d_x * dy) * _silu_grad(pre_xs)
        dpre_bc = jnp.concatenate(d_bc + d_cc, axis=1) * _silu_grad(pre_bc)
        gcb_ref[...] += jnp.concatenate([jnp.sum(dpre_xs, axis=0, keepdims=True),
                                         jnp.sum(dpre_bc, axis=0, keepdims=True)], axis=1)
        nxs = nxs_ref[...]
        nbc = nbc_ref[...]
        du_xs = jnp.zeros_like(dpre_xs)
        du_bc = jnp.zeros_like(dpre_bc)
        for i in range(CONV_TAPS):
            k = CONV_TAPS - 1 - i
            gcw_ref[i:i + 1, :] += jnp.concatenate(
                [jnp.sum(dpre_xs * _shift_down(xs_raw, pxs, k), axis=0, keepdims=True),
                 jnp.sum(dpre_bc * _shift_down(bc_raw, pbc, k), axis=0, keepdims=True)], axis=1)
            du_xs = du_xs + _shift_up(dpre_xs, nxs, k) * cw[i:i + 1, :D_BRANCH]
            du_bc = du_bc + _shift_up(dpre_bc, nbc, k) * cw[i:i + 1, D_BRANCH:]
        nxs_ref[...] = dpre_xs
        nbc_ref[...] = dpre_bc

        dx_ref[:, :D_BRANCH] = du_xs.astype(BF16)
        dx_ref[:, D_BRANCH:D_CONV] = du_bc.astype(BF16)
        dx_ref[:, D_CONV:D_CONV + LANES] = ddt_raw.astype(BF16)
        dx_ref[:, D_CONV + LANES:] = jnp.zeros((qc, 2048 - D_CONV - LANES), BF16)

    rev = lambda b, c: b * nc + (nc - 1 - c)
    prv = lambda b, c: b * nc + jnp.maximum(nc - 2 - c, 0)
    nblk = lambda w, off, f: pl.BlockSpec((qc, w), lambda b, c: (f(b, c), off))
    full = lambda r, w: pl.BlockSpec((r, w), lambda b, c: (0, 0))
    st_spec = lambda f: pl.BlockSpec((1, 1, N_GROUPS, D_STATE, GROUP_W),
                                     lambda b, c: (b, f(c), 0, 0, 0))
    return pl.pallas_call(
        body, name="ssd_bwd",
        grid=(nb, nc),
        in_specs=[nblk(D_BRANCH, COL_XS // D_BRANCH, rev), nblk(D_BC, COL_BC // D_BC, rev),
                  nblk(LANES, COL_DT // LANES, rev),
                  nblk(D_BRANCH, COL_XS // D_BRANCH, prv), nblk(D_BC, COL_BC // D_BC, prv),
                  nblk(D_BRANCH, 0, rev),
                  st_spec(lambda c: nc - 1 - c), st_spec(lambda c: jnp.minimum(nc - c, nc - 1)),
                  full(CONV_TAPS, D_CONV), full(1, D_CONV), full(1, LANES), full(1, LANES),
                  full(1, LANES)],
        out_specs=[nblk(2048, 0, rev), full(8, D_CONV), full(1, D_CONV), full(1, LANES),
                   full(1, LANES), full(1, LANES)],
        out_shape=[jax.ShapeDtypeStruct((nb * seq, 2048), BF16),
                   jax.ShapeDtypeStruct((8, D_CONV), F32), jax.ShapeDtypeStruct((1, D_CONV), F32),
                   jax.ShapeDtypeStruct((1, LANES), F32), jax.ShapeDtypeStruct((1, LANES), F32),
                   jax.ShapeDtypeStruct((1, LANES), F32)],
        scratch_shapes=[pltpu.VMEM((N_GROUPS, D_STATE, GROUP_W), F32),
                        pltpu.VMEM((qc, D_BRANCH), F32), pltpu.VMEM((qc, D_BC), F32),
                        pltpu.VMEM((qc, D_BRANCH), F32), pltpu.VMEM((qc, D_BRANCH), F32)],
        compiler_params=_params(2),
    )(proj, proj, proj, proj, proj, d_y, states, states, conv_w, conv_b, dtb, alog, dskip)


def _mid(o_sb, y_ssd, proj, x2, target, sb_w, ssd_w, w_out_b, tm):
    t = x2.shape[0]
    inv_d = 1.0 / D_MODEL

    def body(o_ref, y_ref, zsb_ref, zssd_ref, x_ref, tg_ref, sbw_ref, ssdw_ref, w_ref,
             dout_ref, dosb_ref, dy_ref, dz_ref, gw_ref, gsb_ref, gssd_ref, loss_ref):
        @pl.when(pl.program_id(0) == 0)
        def _():
            gw_ref[...] = jnp.zeros_like(gw_ref)
            gsb_ref[...] = jnp.zeros_like(gsb_ref)
            gssd_ref[...] = jnp.zeros_like(gssd_ref)
            loss_ref[...] = jnp.zeros_like(loss_ref)

        def branch(val, z, w):
            gate = _silu(z)
            g = val * gate
            r = lax.rsqrt(jnp.mean(g * g, axis=1, keepdims=True) + EPS)
            xhat = g * r
            return gate, r, xhat, (xhat * w).astype(BF16)

        o = o_ref[...]
        y = y_ref[...]
        z_sb = zsb_ref[...]
        z_ssd = zssd_ref[...]
        gate_a, r_a, xhat_a, mix_a = branch(o, z_sb, sbw_ref[...])
        gate_b, r_b, xhat_b, mix_b = branch(y, z_ssd, ssdw_ref[...])
        out = x_ref[...] + _dot(mix_a, w_ref[:D_BRANCH, :]) + _dot(mix_b, w_ref[D_BRANCH:, :])
        diff = out - tg_ref[...]
        loss_ref[...] += 0.5 * inv_d * jnp.sum(diff * diff)
        d_out = diff * inv_d
        dout_ref[...] = d_out
        d_outb = d_out.astype(BF16)
        gw_ref[:D_BRANCH, :] += _dot(mix_a, d_outb, _TN)
        gw_ref[D_BRANCH:, :] += _dot(mix_b, d_outb, _TN)

        def branch_bwd(dmix, val, z, w, gate, r, xhat):
            gg = dmix * w
            m = jnp.mean(gg * xhat, axis=1, keepdims=True)
            dg = r * (gg - xhat * m)
            return dg * gate, dg * val * _silu_grad(z), jnp.sum(dmix * xhat, axis=0, keepdims=True)

        dmix_a = _dot(d_outb, w_ref[:D_BRANCH, :], _NT)
        dmix_b = _dot(d_outb, w_ref[D_BRANCH:, :], _NT)
        d_o, dz_a, gsb = branch_bwd(dmix_a, o, z_sb, sbw_ref[...], gate_a, r_a, xhat_a)
        d_y, dz_b, gssd = branch_bwd(dmix_b, y, z_ssd, ssdw_ref[...], gate_b, r_b, xhat_b)
        dosb_ref[...] = d_o
        dy_ref[...] = d_y
        dz_ref[:, :D_BRANCH] = dz_a.astype(BF16)
        dz_ref[:, D_BRANCH:] = dz_b.astype(BF16)
        gsb_ref[...] += gsb
        gssd_ref[...] += gssd

    row = lambda w, off: pl.BlockSpec((tm, w), lambda i: (i, off))
    full = lambda r, w: pl.BlockSpec((r, w), lambda i: (0, 0))
    resident = pl.BlockSpec((2 * D_BRANCH, D_MODEL), lambda i: (0, 0), pipeline_mode=pl.Buffered(1))
    tok = jax.ShapeDtypeStruct((t, D_MODEL), F32)
    return pl.pallas_call(
        body, name="mid",
        grid=(t // tm,),
        in_specs=[row(D_BRANCH, 0), row(D_BRANCH, 0), row(D_BRANCH, 3), row(D_BRANCH, 4),
                  row(D_MODEL, 0), row(D_MODEL, 0), full(1, D_BRANCH), full(1, D_BRANCH),
                  resident],
        out_specs=[row(D_MODEL, 0), row(D_BRANCH, 0), row(D_BRANCH, 0), row(2 * D_BRANCH, 0),
                   resident, full(1, D_BRANCH), full(1, D_BRANCH),
                   full(1, LANES)],
        out_shape=[tok, tok, tok, jax.ShapeDtypeStruct((t, 2 * D_BRANCH), BF16),
                   jax.ShapeDtypeStruct((2 * D_BRANCH, D_MODEL), F32),
                   jax.ShapeDtypeStruct((1, D_BRANCH), F32), jax.ShapeDtypeStruct((1, D_BRANCH), F32),
                   jax.ShapeDtypeStruct((1, LANES), F32)],
        compiler_params=_params(1),
    )(o_sb, y_ssd, proj, proj, x2, target, sb_w, ssd_w, w_out_b)


_DPROJ_FIRST = (0, 1, 2, 3, 5)
_DPROJ_BLOCKS = (1, 1, 1, 2, 2)
_DPROJ_OWNER = (0, 1, 2, 3, 3, 4, 4)


def _dproj_col(j, p):
    return jnp.clip(j - _DPROJ_FIRST[p], 0, _DPROJ_BLOCKS[p] - 1)


def _in_proj_bwd_x(d_parts, w_in_b, x2, d_out, norm_w, tm, slabs=()):
    t = x2.shape[0]
    n_parts = len(d_parts)
    n_slabs = len(slabs)
    n_rows = t // tm

    def body(*refs):
        dp_refs = refs[:n_parts]
        w_ref, x_ref, dout_ref, nw_ref = refs[n_parts:n_parts + 4]
        rest = refs[n_parts + 4:]
        src_refs, (gx_ref, gnw_ref) = rest[:n_slabs], rest[n_slabs:n_slabs + 2]
        dst_refs = rest[n_slabs + 2:2 * n_slabs + 2]
        acc_ref = rest[2 * n_slabs + 2]
        sems = rest[2 * n_slabs + 3:]
        i, j = pl.program_id(0), pl.program_id(1)

        if n_slabs:
            @pl.when((i == 0) & (j == 0))
            def _():
                _exchange_start(_exchange_copies(src_refs, dst_refs, (True,) * n_slabs, *sems))

            @pl.when((i == n_rows - 1) & (j == N_COLBLK - 1))
            def _():
                _exchange_wait(_exchange_copies(src_refs, dst_refs, (True,) * n_slabs, *sems))

        @pl.when((i == 0) & (j == 0))
        def _():
            gnw_ref[...] = jnp.zeros_like(gnw_ref)

        @pl.when(j == 0)
        def _():
            acc_ref[...] = jnp.zeros_like(acc_ref)

        for jj in range(N_COLBLK):
            @pl.when(j == jj)
            def _(jj=jj):
                acc_ref[...] += _dot(dp_refs[_DPROJ_OWNER[jj]][...], w_ref[...], _NT)

        @pl.when(j == N_COLBLK - 1)
        def _():
            xf = x_ref[...]
            d_hn = acc_ref[...]
            r = lax.rsqrt(jnp.mean(xf * xf, axis=1, keepdims=True) + EPS)
            xhat = xf * r
            g = d_hn * nw_ref[...]
            m = jnp.mean(g * xhat, axis=1, keepdims=True)
            gx_ref[...] = dout_ref[...] + r * (g - xhat * m)
            gnw_ref[...] += jnp.sum(d_hn * xhat, axis=0, keepdims=True)

    return pl.pallas_call(
        body, name="in_proj_bwd_x",
        grid=(t // tm, N_COLBLK),
        in_specs=[pl.BlockSpec((tm, 1024), lambda i, j, p=p: (i, _dproj_col(j, p)))
                  for p in range(n_parts)] + [
                  pl.BlockSpec((D_MODEL, 1024), lambda i, j: (0, j)),
                  pl.BlockSpec((tm, D_MODEL), lambda i, j: (i, 0)),
                  pl.BlockSpec((tm, D_MODEL), lambda i, j: (i, 0)),
                  pl.BlockSpec((1, D_MODEL), lambda i, j: (0, 0))] + [_ANY] * n_slabs,
        out_specs=[pl.BlockSpec((tm, D_MODEL), lambda i, j: (i, 0)),
                   pl.BlockSpec((1, D_MODEL), lambda i, j: (0, 0))] + [_ANY] * n_slabs,
        out_shape=[jax.ShapeDtypeStruct((t, D_MODEL), F32), jax.ShapeDtypeStruct((1, D_MODEL), F32)]
                  + _exchange_shapes(slabs, (True,) * n_slabs),
        scratch_shapes=[pltpu.VMEM((tm, D_MODEL), F32)] + (_exchange_sems(n_slabs) if n_slabs else []),
        compiler_params=_params(2),
    )(*d_parts, w_in_b, x2, d_out, norm_w, *slabs)


def _in_proj_bwd_w(hn, d_parts, tm):
    t = hn.shape[0]
    n_parts = len(d_parts)

    def body(hn_ref, *refs):
        dp_refs, gw_ref = refs[:n_parts], refs[n_parts]
        j = pl.program_id(0)

        @pl.when(pl.program_id(1) == 0)
        def _():
            gw_ref[...] = jnp.zeros_like(gw_ref)

        for jj in range(N_COLBLK):
            @pl.when(j == jj)
            def _(jj=jj):
                gw_ref[...] += _dot(hn_ref[...], dp_refs[_DPROJ_OWNER[jj]][...], _TN)

    def part_spec(p):
        def index(j, i):
            mine = (j >= _DPROJ_FIRST[p]) & (j < _DPROJ_FIRST[p] + _DPROJ_BLOCKS[p])
            return jnp.where(mine, i, 0), _dproj_col(j, p)
        return pl.BlockSpec((tm, 1024), index)

    return pl.pallas_call(
        body, name="in_proj_bwd_w",
        grid=(N_COLBLK, t // tm),
        in_specs=[pl.BlockSpec((tm, D_MODEL), lambda j, i: (i, 0))]
                 + [part_spec(p) for p in range(n_parts)],
        out_specs=pl.BlockSpec((D_MODEL, 1024), lambda j, i: (0, j)),
        out_shape=jax.ShapeDtypeStruct((D_MODEL, D_IN_PAD), F32),
        compiler_params=_params(2),
    )(hn, *d_parts)


def _adamw(parts, w, m, v, tr, name):
    rows, cols = w.shape
    c1 = 1.0 - ADAM_B1 ** ADAM_STEP
    c2 = 1.0 - ADAM_B2 ** ADAM_STEP

    def body(p_ref, w_ref, m_ref, v_ref, g_ref, d_ref, nm_ref, nv_ref):
        g = p_ref[0].astype(F32)
        for s in range(1, N_DEV):
            g = g + p_ref[s].astype(F32)
        nm = ADAM_B1 * m_ref[...] + (1.0 - ADAM_B1) * g
        nv = ADAM_B2 * v_ref[...] + (1.0 - ADAM_B2) * (g * g)
        g_ref[...] = g
        nm_ref[...] = nm
        nv_ref[...] = nv
        d_ref[...] = -ADAM_LR * ((nm / c1) / (jnp.sqrt(nv / c2) + ADAM_EPS) + ADAM_WD * w_ref[...])

    blk = pl.BlockSpec((tr, cols), lambda i: (i, 0))
    shape = jax.ShapeDtypeStruct((rows, cols), F32)
    return pl.pallas_call(
        body, name=name,
        grid=(rows // tr,),
        in_specs=[pl.BlockSpec((N_DEV, tr, cols), lambda i: (0, i, 0)), blk, blk, blk],
        out_specs=[blk, blk, blk, blk],
        out_shape=[shape, shape, shape, shape],
        compiler_params=_params(1),
    )(parts, w, m, v)


def _mesh_place():
    x, y, c = lax.axis_index("x"), lax.axis_index("y"), lax.axis_index("c")
    return x, y, c, 4 * x + 2 * y + c


def _peer(x, y, c, k):
    px = 1 - x if k & 4 else x
    py = 1 - y if k & 2 else y
    pc = 1 - c if k & 1 else c
    return (px, py, pc), 4 * px + 2 * py + pc


def _exchange(srcs, scatter, name):
    n = len(srcs)

    def body(*refs):
        copies = _exchange_copies(refs[:n], refs[n:2 * n], scatter, *refs[2 * n:])
        _exchange_start(copies)
        _exchange_wait(copies)

    return pl.pallas_call(
        body, name=name,
        in_specs=[_ANY] * n, out_specs=[_ANY] * n, out_shape=_exchange_shapes(srcs, scatter),
        scratch_shapes=_exchange_sems(n),
    )(*srcs)


_ANY = pl.BlockSpec(memory_space=pl.ANY)


def _exchange_shapes(srcs, scatter):
    return [jax.ShapeDtypeStruct(s.shape if sc else (N_DEV,) + s.shape, s.dtype)
            for s, sc in zip(srcs, scatter)]


def _exchange_sems(n):
    return [pltpu.SemaphoreType.DMA((n * (N_DEV - 1),)),
            pltpu.SemaphoreType.DMA((n * (N_DEV - 1),)),
            pltpu.SemaphoreType.DMA((n,))]


def _exchange_copies(src_refs, dst_refs, scatter, send_sems, recv_sems, loc_sems):
    n = len(src_refs)
    x, y, c, me = _mesh_place()

    def src_of(i, idx):
        return src_refs[i].at[idx] if scatter[i] else src_refs[i]

    local = [pltpu.make_async_copy(src_of(i, me), dst_refs[i].at[me], loc_sems.at[i])
             for i in range(n)]
    sends, recvs = [], []
    for k in range(1, N_DEV):
        peer, pidx = _peer(x, y, c, k)
        for i in range(n):
            s = i * (N_DEV - 1) + k - 1
            for dst_slab, group in ((me, sends), (pidx, recvs)):
                group.append(pltpu.make_async_remote_copy(
                    src_ref=src_of(i, pidx), dst_ref=dst_refs[i].at[dst_slab],
                    send_sem=send_sems.at[s], recv_sem=recv_sems.at[s],
                    device_id=peer, device_id_type=pl.DeviceIdType.MESH))
    return local, sends, recvs


def _exchange_start(copies):
    local, sends, _ = copies
    for cp in local + sends:
        cp.start()


def _exchange_wait(copies):
    local, sends, recvs = copies
    for cp in recvs:
        cp.wait_recv()
    for cp in sends:
        cp.wait_send()
    for cp in local:
        cp.wait()


def _pad_lanes(v, width=LANES):
    return jnp.pad(v, ((0, 0), (0, width - v.shape[1])))


def _local_step(x, target, norm_w, w_in_b, q_norm_w, k_norm_w, conv_w, conv_b, dt_bias, a_log,
                d_skip, sb_norm_w, ssd_norm_w, w_out_b, tm=512, tq=512, tmid=256, blk=ATT_BLK,
                scatter=False):
    nb, seq, _ = x.shape
    t = nb * seq
    x2 = x.reshape(t, D_MODEL)
    tg2 = target.reshape(t, D_MODEL)
    qw2 = jnp.tile(q_norm_w, (1, 2))
    kw2 = jnp.tile(k_norm_w, (1, 2))
    dtb, alog, dsk = _pad_lanes(dt_bias), _pad_lanes(a_log), _pad_lanes(d_skip)

    tproj = min(2 * tm, t)
    if scatter:
        proj, hn, wout_all, cw_all = _in_proj(x2, norm_w, w_in_b, tproj, (w_out_b, conv_w))
        w_out_b = wout_all.reshape(2 * D_BRANCH, D_MODEL)
        conv_w = jnp.transpose(cw_all, (1, 0, 2)).reshape(CONV_TAPS, D_CONV)
    else:
        proj, hn = _in_proj(x2, norm_w, w_in_b, tproj)
    qs, kn, vb, kt = _qk_prep(proj, qw2, kw2, nb, seq, tq)
    o_sb, sb_tot, sb_low = _attn_fwd(qs, kn, vb, nb, seq, blk)
    y_ssd, states = _ssd_fwd(proj, conv_w, conv_b, dtb, alog, dsk, nb, seq)
    d_out, d_osb, d_y, d_z, g_wout, g_sbw, g_ssdw, loss = _mid(
        o_sb, y_ssd, proj, x2, tg2, sb_norm_w, ssd_norm_w, w_out_b, tmid)
    dqs, dkn, dvh = _attn_bwd(qs, kn, kt, vb, sb_tot, sb_low, d_osb, nb, seq, blk)
    dq_raw, dk_raw, dv_raw, g_qw, g_kw = _qk_bwd(proj, dqs, dkn, dvh, qw2, kw2, nb, seq, tq)
    d_xbc, g_cw, g_cb, g_dtb, g_alog, g_dsk = _ssd_bwd(
        proj, d_y, states, conv_w, conv_b, dtb, alog, dsk, nb, seq)
    d_parts = [dq_raw, dk_raw, dv_raw, d_z, d_xbc]
    g_win = _in_proj_bwd_w(hn, d_parts, tm)[:, :D_IN]
    g_cw = g_cw[:CONV_TAPS]
    if scatter:
        grad_x, g_nw, g_win, g_wout, g_cw = _in_proj_bwd_x(
            d_parts, w_in_b, x2, d_out, norm_w, tm, _grad_slabs(g_win, g_wout, g_cw))
    else:
        grad_x, g_nw = _in_proj_bwd_x(d_parts, w_in_b, x2, d_out, norm_w, tm)

    small = dict(
        norm_w=g_nw,
        q_norm_w=g_qw[:, :HEAD_DIM] + g_qw[:, HEAD_DIM:],
        k_norm_w=g_kw[:, :HEAD_DIM] + g_kw[:, HEAD_DIM:],
        conv_b=g_cb, dt_bias=g_dtb[:, :N_HEADS], A_log=g_alog[:, :N_HEADS],
        D_skip=g_dsk[:, :N_HEADS], sb_norm_w=g_sbw, ssd_norm_w=g_ssdw)
    return loss[0, 0], grad_x.reshape(nb, seq, D_MODEL), g_win, g_wout, g_cw, small


def _grad_slabs(g_win, g_wout, g_cw):
    w_sh = D_IN // N_DEV
    c_sh = D_CONV // N_DEV
    return (jnp.transpose(g_win.reshape(D_MODEL, N_DEV, w_sh), (1, 0, 2)).astype(BF16),
            g_wout.reshape(N_DEV, 2 * D_BRANCH // N_DEV, D_MODEL).astype(BF16),
            jnp.pad(jnp.transpose(g_cw.reshape(CONV_TAPS, N_DEV, c_sh), (1, 0, 2)),
                    ((0, 0), (0, 8 - CONV_TAPS), (0, 0))))


_SMALL = ("norm_w", "q_norm_w", "k_norm_w", "conv_b", "dt_bias", "A_log", "D_skip",
          "sb_norm_w", "ssd_norm_w")


def _pack_small(vals):
    rows = [_pad_lanes(vals[n], -(-vals[n].shape[1] // LANES) * LANES).reshape(-1, LANES)
            for n in _SMALL]
    packed = jnp.concatenate(rows, axis=0)
    return jnp.pad(packed, ((0, 48 - packed.shape[0]), (0, 0)))


def _unpack_small(packed, like):
    out, r = {}, 0
    for n in _SMALL:
        width = like[n].shape[1]
        nr = -(-width // LANES)
        out[n] = packed[r:r + nr].reshape(1, nr * LANES)[:, :width]
        r += nr
    return out


def kernel(x, norm_w, w_in, q_norm_w, k_norm_w, conv_w, conv_b, dt_bias, A_log, D_skip, sb_norm_w, ssd_norm_w, w_out, loss_target, m_norm_w, m_w_in, m_q_norm_w, m_k_norm_w, m_conv_w, m_conv_b, m_dt_bias, m_A_log, m_D_skip, m_sb_norm_w, m_ssd_norm_w, m_w_out, v_norm_w, v_w_in, v_q_norm_w, v_k_norm_w, v_conv_w, v_conv_b, v_dt_bias, v_A_log, v_D_skip, v_sb_norm_w, v_ssd_norm_w, v_w_out):
    win_all, = _exchange([w_in[0].astype(BF16)], [False], "gather_w_in")
    w_in_b = jnp.pad(jnp.transpose(win_all, (1, 0, 2)).reshape(D_MODEL, D_IN),
                     ((0, 0), (0, D_IN_PAD - D_IN)))

    loss, grad_x, win_parts, wout_parts, cw_parts, g_small = _local_step(
        x, loss_target, norm_w, w_in_b, q_norm_w, k_norm_w, conv_w[0], conv_b, dt_bias, A_log,
        D_skip, sb_norm_w, ssd_norm_w, w_out[0].astype(BF16), scatter=True)
    small_parts, = _exchange([_pack_small(g_small)], [False], "gather_small_grads")

    small_w = dict(norm_w=norm_w, q_norm_w=q_norm_w, k_norm_w=k_norm_w, conv_b=conv_b,
                   dt_bias=dt_bias, A_log=A_log, D_skip=D_skip, sb_norm_w=sb_norm_w,
                   ssd_norm_w=ssd_norm_w)
    small_m = dict(norm_w=m_norm_w, q_norm_w=m_q_norm_w, k_norm_w=m_k_norm_w, conv_b=m_conv_b,
                   dt_bias=m_dt_bias, A_log=m_A_log, D_skip=m_D_skip, sb_norm_w=m_sb_norm_w,
                   ssd_norm_w=m_ssd_norm_w)
    small_v = dict(norm_w=v_norm_w, q_norm_w=v_q_norm_w, k_norm_w=v_k_norm_w, conv_b=v_conv_b,
                   dt_bias=v_dt_bias, A_log=v_A_log, D_skip=v_D_skip, sb_norm_w=v_sb_norm_w,
                   ssd_norm_w=v_ssd_norm_w)

    pad8 = lambda a: jnp.pad(a, ((0, 8 - CONV_TAPS), (0, 0)))
    r_win = _adamw(win_parts, w_in[0], m_w_in[0], v_w_in[0], 128, "adamw_w_in")
    r_wout = _adamw(wout_parts, w_out[0], m_w_out[0], v_w_out[0], 128, "adamw_w_out")
    r_cw = _adamw(cw_parts, pad8(conv_w[0]), pad8(m_conv_w[0]), pad8(v_conv_w[0]), 8, "adamw_conv_w")
    r_small = _adamw(small_parts, _pack_small(small_w), _pack_small(small_m),
                     _pack_small(small_v), 48, "adamw_small")

    loss = lax.psum(loss, ("x", "y", "c"))
    res = {"w_in": [a[None] for a in r_win], "w_out": [a[None] for a in r_wout],
           "conv_w": [a[:CONV_TAPS][None] for a in r_cw]}
    unpacked = [_unpack_small(a, small_w) for a in r_small]
    for n in _SMALL:
        res[n] = [u[n] for u in unpacked]
    order = ("norm_w", "w_in", "q_norm_w", "k_norm_w", "conv_w", "conv_b", "dt_bias", "A_log",
             "D_skip", "sb_norm_w", "ssd_norm_w", "w_out")
    outs = [loss, grad_x]
    for kind in range(4):
        outs += [res[n][kind] for n in order]
    return tuple(outs)
```

```python
import functools
import math

import jax
import jax.numpy as jnp
from jax import lax
from jax.experimental import pallas as pl
from jax.experimental.pallas import tpu as pltpu

F32 = jnp.float32
BF16 = jnp.bfloat16
HIGHEST = lax.Precision.HIGHEST

D_MODEL = 1024
N_HEADS = 16
HEAD_DIM = 64
N_PAIRS = N_HEADS // 2
D_BRANCH = 1024
N_GROUPS = 2
HEADS_PER_GROUP = 8
D_STATE = 128
GROUP_W = HEADS_PER_GROUP * HEAD_DIM
D_BC = 2 * N_GROUPS * D_STATE
D_CONV = D_BRANCH + D_BC
D_IN = 6672
D_IN_PAD = 7168
N_COLBLK = D_IN_PAD // 1024
COL_XS = 5120
COL_BC = 6144
COL_DT = 6656
EPS = 1e-6
CONV_TAPS = 4
N_DEV = 8

LANES = 128
SSD_CHUNK = 128
ATT_BLK = 256
EXP_UNDERFLOW = -105.0
VMEM_LIMIT = 56 * 1024 * 1024

ADAM_LR = 0.001
ADAM_B1 = 0.9
ADAM_B2 = 0.999
ADAM_EPS = 1e-08
ADAM_WD = 0.01
ADAM_STEP = 10

_NT = (((1,), (1,)), ((), ()))
_TN = (((0,), (0,)), ((), ()))


def _params(n_grid):
    return pltpu.CompilerParams(dimension_semantics=("arbitrary",) * n_grid,
                                vmem_limit_bytes=VMEM_LIMIT)


def _dot(a, b, dims=None, precision=None):
    if dims is None:
        return jnp.dot(a, b, preferred_element_type=F32, precision=precision)
    return lax.dot_general(a, b, dims, preferred_element_type=F32, precision=precision)


def _sigmoid(x):
    return 1.0 / (1.0 + jnp.exp(-x))


def _softplus(x):
    return jnp.maximum(x, 0.0) + jnp.log(1.0 + jnp.exp(-jnp.abs(x)))


def _split_bf16(x):
    hi = x.astype(BF16)
    lo = (x - hi.astype(F32)).astype(BF16)
    return hi, lo


def _lane_iota(shape):
    return lax.broadcasted_iota(jnp.int32, shape, len(shape) - 1)


def _row_iota(shape):
    return lax.broadcasted_iota(jnp.int32, shape, len(shape) - 2)


def _pair_sum(x):
    r = lax.broadcasted_iota(jnp.int32, (LANES, LANES), 0)
    c = lax.broadcasted_iota(jnp.int32, (LANES, LANES), 1)
    same_head = jnp.where(r // HEAD_DIM == c // HEAD_DIM, 1.0, 0.0).astype(BF16)
    hi, lo = _split_bf16(x)
    return _dot(hi, same_head) + _dot(lo, same_head)


def _pair_head(x, a):
    lane = _lane_iota(x.shape)
    mine = (lane < HEAD_DIM) if a == 0 else (lane >= HEAD_DIM)
    return jnp.where(mine, x, jnp.zeros_like(x))


def _head_expand():
    r = lax.broadcasted_iota(jnp.int32, (LANES, D_BRANCH), 0)
    c = lax.broadcasted_iota(jnp.int32, (LANES, D_BRANCH), 1)
    return jnp.where(c // HEAD_DIM == r, 1.0, 0.0).astype(F32)


def _in_proj(x2, norm_w, w_in_b, tm, shards=()):
    t = x2.shape[0]
    n_sh = len(shards)
    n_rows = t // tm

    def body(x_ref, nw_ref, w_ref, *rest):
        src_refs, (proj_ref, hn_ref) = rest[:n_sh], rest[n_sh:n_sh + 2]
        dst_refs, sems = rest[n_sh + 2:2 * n_sh + 2], rest[2 * n_sh + 2:]
        i, j = pl.program_id(0), pl.program_id(1)

        if n_sh:
            @pl.when((i == 0) & (j == 0))
            def _():
                _exchange_start(_exchange_copies(src_refs, dst_refs, (False,) * n_sh, *sems))

            @pl.when((i == n_rows - 1) & (j == N_COLBLK - 1))
            def _():
                _exchange_wait(_exchange_copies(src_refs, dst_refs, (False,) * n_sh, *sems))

        @pl.when(j == 0)
        def _():
            xf = x_ref[...]
            r = lax.rsqrt(jnp.mean(xf * xf, axis=1, keepdims=True) + EPS)
            hn_ref[...] = (xf * r * nw_ref[...]).astype(BF16)

        proj_ref[...] = _dot(hn_ref[...], w_ref[...])

    return pl.pallas_call(
        body, name="in_proj",
        grid=(n_rows, N_COLBLK),
        in_specs=[pl.BlockSpec((tm, D_MODEL), lambda i, j: (i, 0)),
                  pl.BlockSpec((1, D_MODEL), lambda i, j: (0, 0)),
                  pl.BlockSpec((D_MODEL, 1024), lambda i, j: (0, j))] + [_ANY] * n_sh,
        out_specs=[pl.BlockSpec((tm, 1024), lambda i, j: (i, j)),
                   pl.BlockSpec((tm, D_MODEL), lambda i, j: (i, 0))] + [_ANY] * n_sh,
        out_shape=[jax.ShapeDtypeStruct((t, D_IN_PAD), F32),
                   jax.ShapeDtypeStruct((t, D_MODEL), BF16)]
                  + _exchange_shapes(shards, (False,) * n_sh),
        scratch_shapes=_exchange_sems(n_sh) if n_sh else [],
        compiler_params=_params(2),
    )(x2, norm_w, w_in_b, *shards)


def _qk_prep(proj, qw2, kw2, nb, seq, tq):
    nl = seq // tq
    scale = 1.0 / math.sqrt(HEAD_DIM)

    def body(q_ref, k_ref, v_ref, qw_ref, kw_ref, qs_ref, kn_ref, vb_ref, kt_ref):
        def norm(x, w):
            r = lax.rsqrt(_pair_sum(x * x) * (1.0 / HEAD_DIM) + EPS)
            return x * r * w

        vb_ref[...] = v_ref[...].astype(BF16)
        for p in range(N_PAIRS):
            cols = slice(p * LANES, (p + 1) * LANES)
            kn = norm(k_ref[:, cols], kw_ref[...])
            qs_ref[:, cols] = (norm(q_ref[:, cols], qw_ref[...]) * scale).astype(BF16)
            kn_ref[:, cols] = kn.astype(BF16)
            kt_ref[0, p] = kn.T.astype(BF16)

    tok_shape = jax.ShapeDtypeStruct((nb * seq, D_BRANCH), BF16)
    tok = lambda blk: pl.BlockSpec((tq, D_BRANCH), lambda b, i: (b * nl + i, blk))
    vec = pl.BlockSpec((1, LANES), lambda b, i: (0, 0))
    return pl.pallas_call(
        body, name="qk_prep",
        grid=(nb, nl),
        in_specs=[tok(0), tok(1), tok(2), vec, vec],
        out_specs=[tok(0), tok(0), tok(0),
                   pl.BlockSpec((1, N_PAIRS, LANES, tq), lambda b, i: (b, 0, 0, i))],
        out_shape=[tok_shape, tok_shape, tok_shape,
                   jax.ShapeDtypeStruct((nb, N_PAIRS, LANES, seq), BF16)],
        compiler_params=_params(2),
    )(proj, proj, proj, qw2, kw2)


def _attn_fwd(qs, kn, vb, nb, seq, blk):
    nq = seq // blk

    def body(q_ref, k_ref, v_ref, o_ref, tot_ref, low_ref, kmax_ref):
        qi = pl.program_id(2)
        r_i = lax.broadcasted_iota(jnp.int32, (blk, blk), 0)
        c_i = lax.broadcasted_iota(jnp.int32, (blk, blk), 1)
        csum = jnp.where(r_i >= c_i, 1.0, 0.0).astype(BF16)
        causal = c_i < r_i
        heads = range(2)

        head = _pair_head

        @pl.when(qi == 0)
        def _():
            kk = k_ref[...].astype(F32)
            for a in heads:
                ksq = jnp.sum(head(kk * kk, a), axis=1, keepdims=True)
                kmax_ref[a] = jnp.full((8, LANES), jnp.max(ksq))

        q_pair = q_ref[...]
        qf = q_pair.astype(F32)
        q_head = [head(q_pair, a) for a in heads]
        zmax = []
        for a in heads:
            qsq = jnp.sum(head(qf * qf, a), axis=1, keepdims=True)
            zmax.append(1.01 * jnp.sqrt(qsq * kmax_ref[a][0:1, 0:1]) + 0.01)

        def exhausted(run):
            top = jnp.maximum(jnp.max(run[0] + zmax[0]), jnp.max(run[1] + zmax[1]))
            return top < EXP_UNDERFLOW

        def sweep(blocks, run, acc):
            offs = [pl.multiple_of(j * blk, blk) for j, _, _ in blocks]
            z = [[_dot(q_head[a], k_ref[pl.ds(off, blk), :], _NT) for a in heads]
                 for off in offs]
            cl = []
            for (_, diag, valid), zb in zip(blocks, z):
                lkb = []
                for a in heads:
                    lk = -_softplus(zb[a])
                    if diag:
                        lk = jnp.where(causal, lk, 0.0)
                    if valid is not None:
                        lk = jnp.where(valid, lk, 0.0)
                    lkb.append(lk.astype(BF16))
                cl.append([_dot(lkb[a], csum) for a in heads])
            for (_, diag, valid), zb, clb, off in zip(blocks, z, cl, offs):
                w = []
                for a in heads:
                    wa = jnp.exp(zb[a] + clb[a] + run[a])
                    if diag:
                        wa = jnp.where(causal, wa, 0.0)
                    if valid is not None:
                        wa = jnp.where(valid, wa, 0.0)
                    w.append(wa.astype(BF16))
                run = [run[a] + clb[a][:, 0:1] for a in heads]
                v_pair = v_ref[pl.ds(off, blk), :]
                acc = acc + _dot(w[0], head(v_pair, 0)) + _dot(w[1], head(v_pair, 1))
            return run, acc

        run = [jnp.zeros((blk, 1), F32)] * 2
        acc = jnp.zeros((blk, LANES), F32)
        run, acc = sweep([(qi, True, None), (jnp.maximum(qi - 1, 0), False, qi >= 1)], run, acc)
        low = jnp.maximum(qi - 1, 0)

        def more(carry):
            low, done, _, _ = carry
            return (low > 0) & jnp.logical_not(done)

        def pair(carry):
            low, _, run, acc = carry
            run, acc = sweep([(low - 1, False, None), (jnp.maximum(low - 2, 0), False, low >= 2)],
                             run, acc)
            return jnp.maximum(low - 2, 0), exhausted(run), run, acc

        low, _, run, acc = lax.while_loop(more, pair, (low, exhausted(run), run, acc))
        low_ref[pl.program_id(0) * N_PAIRS + pl.program_id(1), qi] = low.astype(F32)
        o_ref[...] = acc
        for a in heads:
            as_row = jnp.sum(jnp.where(r_i == c_i, run[a], 0.0), axis=0, keepdims=True)
            tot_ref[0, a, 0] = jnp.broadcast_to(as_row, (8, blk))

    return pl.pallas_call(
        body, name="sb_attn_fwd",
        grid=(nb, N_PAIRS, nq),
        in_specs=[pl.BlockSpec((blk, LANES), lambda b, h, i: (b * nq + i, h)),
                  pl.BlockSpec((seq, LANES), lambda b, h, i: (b, h)),
                  pl.BlockSpec((seq, LANES), lambda b, h, i: (b, h))],
        out_specs=[pl.BlockSpec((blk, LANES), lambda b, h, i: (b * nq + i, h)),
                   pl.BlockSpec((1, 2, 1, 8, blk), lambda b, h, i: (b, h, i, 0, 0)),
                   pl.BlockSpec(memory_space=pltpu.SMEM)],
        out_shape=[jax.ShapeDtypeStruct((nb * seq, D_BRANCH), F32),
                   jax.ShapeDtypeStruct((nb, N_HEADS, nq, 8, blk), F32),
                   jax.ShapeDtypeStruct((nb * N_PAIRS, nq), F32)],
        scratch_shapes=[pltpu.VMEM((2, 8, LANES), F32)],
        compiler_params=_params(3),
    )(qs, kn, vb)


def _attn_bwd(qs, kn, kt, vb, tot, low, d_o, nb, seq, blk):
    nq = seq // blk

    def body(q_ref, k_ref, kt_ref, v_ref, tot_ref, low_ref, do_ref, dq_ref, dk_ref, dv_ref):
        qi = pl.program_id(2)

        @pl.when(qi == 0)
        def _():
            dk_ref[...] = jnp.zeros_like(dk_ref)
            dv_ref[...] = jnp.zeros_like(dv_ref)

        r_i = lax.broadcasted_iota(jnp.int32, (blk, blk), 0)
        c_i = lax.broadcasted_iota(jnp.int32, (blk, blk), 1)
        before = jnp.where(c_i < r_i, 1.0, 0.0).astype(BF16)
        upto = jnp.where(c_i <= r_i, 1.0, 0.0).astype(BF16)
        causal = r_i < c_i

        heads = range(2)
        q_head = [_pair_head(q_ref[...], a) for a in heads]
        d_ob = [_pair_head(do_ref[...].astype(BF16), a) for a in heads]
        total = [tot_ref[0, a, 0][0:1, :] for a in heads]

        def sweep(blocks, lsum, esum, dqt):
            def keep(x, diag, valid):
                if diag:
                    x = jnp.where(causal, x, 0.0)
                if valid is not None:
                    x = jnp.where(valid, x, 0.0)
                return x

            offs = [pl.multiple_of(j * blk, blk) for j, _, _ in blocks]
            zt = [[_dot(k_ref[pl.ds(off, blk), :], q_head[a], _NT) for a in heads]
                  for off in offs]
            dwt = [[_dot(v_ref[pl.ds(off, blk), :], d_ob[a], _NT) for a in heads]
                   for off in offs]
            sp, lk, lpre = [], [], []
            for (_, diag, valid), ztb in zip(blocks, zt):
                sp.append([_softplus(ztb[a]) for a in heads])
                lk.append([keep(-sp[-1][a], diag, valid).astype(BF16) for a in heads])
                lpre.append([_dot(before, lk[-1][a]) for a in heads])
            wt, et, epre = [], [], []
            for i, (_, diag, valid) in enumerate(blocks):
                wt.append([keep(jnp.exp(zt[i][a] + (total[a] - lsum[a] - lpre[i][a])), diag, valid)
                           for a in heads])
                et.append([dwt[i][a] * wt[i][a] for a in heads])
                split = [_split_bf16(et[i][a]) for a in heads]
                epre.append([_dot(upto, split[a][0]) + _dot(upto, split[a][1]) for a in heads])
                lsum = [lsum[a] + lpre[i][a][blk - 1:blk, :] + lk[i][a][blk - 1:blk, :]
                        for a in heads]
            for i, (_, diag, valid) in enumerate(blocks):
                dzb = [keep(et[i][a] - jnp.exp(zt[i][a] - sp[i][a]) * (esum[a] + epre[i][a]),
                            diag, valid).astype(BF16) for a in heads]
                esum = [esum[a] + epre[i][a][blk - 1:blk, :] for a in heads]
                dk_ref[pl.ds(offs[i], blk), :] += (_dot(dzb[0], q_head[0]) + _dot(dzb[1], q_head[1]))
                dv_ref[pl.ds(offs[i], blk), :] += (_dot(wt[i][0].astype(BF16), d_ob[0])
                                                   + _dot(wt[i][1].astype(BF16), d_ob[1]))
                kt_pair = kt_ref[0, 0, :, pl.ds(offs[i], blk)]
                dqt = [dqt[a] + _dot(kt_pair, dzb[a]) for a in heads]
            return lsum, esum, dqt

        row = [jnp.zeros((1, blk), F32)] * 2
        dqt = [jnp.zeros((LANES, blk), F32)] * 2
        low = low_ref[pl.program_id(0) * N_PAIRS + pl.program_id(1), qi].astype(jnp.int32)
        low = jnp.clip(low, 0, jnp.maximum(qi - 1, 0))

        def pair(carry):
            j, lsum, esum, dqt = carry
            return (j + 2,) + sweep([(j, False, None), (j + 1, False, j + 1 < qi - 1)],
                                    lsum, esum, dqt)

        _, lsum, esum, dqt = lax.while_loop(lambda c: c[0] < qi - 1, pair, (low, row, row, dqt))
        _, _, dqt = sweep([(jnp.maximum(qi - 1, 0), False, qi >= 1), (qi, True, None)],
                          lsum, esum, dqt)
        top = _row_iota((LANES, blk)) < HEAD_DIM
        dq_ref[...] = jnp.where(top, dqt[0], dqt[1]).T

    seq_blk = pl.BlockSpec((seq, LANES), lambda b, h, i: (b, h))
    tok = pl.BlockSpec((blk, LANES), lambda b, h, i: (b * nq + i, h))
    tok_shape = jax.ShapeDtypeStruct((nb * seq, D_BRANCH), F32)
    return pl.pallas_call(
        body, name="sb_attn_bwd",
        grid=(nb, N_PAIRS, nq),
        in_specs=[tok, seq_blk,
                  pl.BlockSpec((1, 1, LANES, seq), lambda b, h, i: (b, h, 0, 0)),
                  seq_blk,
                  pl.BlockSpec((1, 2, 1, 8, blk), lambda b, h, i: (b, h, i, 0, 0)),
                  pl.BlockSpec(memory_space=pltpu.SMEM),
                  tok],
        out_specs=[tok, seq_blk, seq_blk],
        out_shape=[tok_shape, tok_shape, tok_shape],
        compiler_params=_params(3),
    )(qs, kn, kt, vb, tot, low, d_o)


def _qk_bwd(proj, dqs, dkn, dvh, qw2, kw2, nb, seq, tq):
    nl = seq // tq
    scale = 1.0 / math.sqrt(HEAD_DIM)

    def body(q_ref, k_ref, dq_ref, dk_ref, dv_ref, qw_ref, kw_ref,
             dqr_ref, dkr_ref, dvr_ref, gq_ref, gk_ref):
        @pl.when((pl.program_id(0) == 0) & (pl.program_id(1) == 0))
        def _():
            gq_ref[...] = jnp.zeros_like(gq_ref)
            gk_ref[...] = jnp.zeros_like(gk_ref)

        def norm_bwd(x, w, dy):
            r = lax.rsqrt(_pair_sum(x * x) * (1.0 / HEAD_DIM) + EPS)
            xhat = x * r
            g = dy * w
            m = _pair_sum(g * xhat) * (1.0 / HEAD_DIM)
            return r * (g - xhat * m), jnp.sum(dy * xhat, axis=0, keepdims=True)

        dvr_ref[...] = dv_ref[...].astype(BF16)
        gq = jnp.zeros((1, LANES), F32)
        gk = jnp.zeros((1, LANES), F32)
        for p in range(N_PAIRS):
            cols = slice(p * LANES, (p + 1) * LANES)
            dqr, gq_p = norm_bwd(q_ref[:, cols], qw_ref[...], dq_ref[:, cols] * scale)
            dkr, gk_p = norm_bwd(k_ref[:, cols], kw_ref[...], dk_ref[:, cols])
            dqr_ref[:, cols] = dqr.astype(BF16)
            dkr_ref[:, cols] = dkr.astype(BF16)
            gq, gk = gq + gq_p, gk + gk_p
        gq_ref[...] += gq
        gk_ref[...] += gk

    tok = lambda blk: pl.BlockSpec((tq, D_BRANCH), lambda b, i: (b * nl + i, blk))
    vec = pl.BlockSpec((1, LANES), lambda b, i: (0, 0))
    tshape = jax.ShapeDtypeStruct((nb * seq, D_BRANCH), BF16)
    return pl.pallas_call(
        body, name="qk_bwd",
        grid=(nb, nl),
        in_specs=[tok(0), tok(1), tok(0), tok(0), tok(0), vec, vec],
        out_specs=[tok(0), tok(0), tok(0), vec, vec],
        out_shape=[tshape, tshape, tshape,
                   jax.ShapeDtypeStruct((1, LANES), F32), jax.ShapeDtypeStruct((1, LANES), F32)],
        compiler_params=_params(2),
    )(proj, proj, dqs, dkn, dvh, qw2, kw2)


def _shift_down(cur, prev, k):
    if k == 0:
        return cur
    rows = _row_iota(cur.shape)
    return jnp.where(rows < k, pltpu.roll(prev, k, axis=0), pltpu.roll(cur, k, axis=0))


def _shift_up(cur, nxt, k):
    if k == 0:
        return cur
    n = cur.shape[0]
    rows = _row_iota(cur.shape)
    return jnp.where(rows < n - k, pltpu.roll(cur, n - k, axis=0), pltpu.roll(nxt, n - k, axis=0))


def _conv_pre(cur, prev, w, b):
    out = b
    for i in range(CONV_TAPS):
        out = out + _shift_down(cur, prev, CONV_TAPS - 1 - i) * w[i:i + 1, :]
    return out


def _silu(x):
    return x * _sigmoid(x)


def _silu_grad(x):
    s = _sigmoid(x)
    return s * (1.0 + x * (1.0 - s))


def _chunk_decay(dt_raw, dtb, alog, expand, qc):
    dt = _softplus(dt_raw + dtb)
    d_a = dt * (-jnp.exp(alog))
    r_i = lax.broadcasted_iota(jnp.int32, (qc, qc), 0)
    c_i = lax.broadcasted_iota(jnp.int32, (qc, qc), 1)
    tril = r_i >= c_i
    a_cs = _dot(jnp.where(tril, 1.0, 0.0).astype(F32), d_a, precision=HIGHEST)
    dt_x = _dot(dt, expand, precision=HIGHEST)
    acs_x = _dot(a_cs, expand, precision=HIGHEST)
    return dt, d_a, a_cs, dt_x, acs_x, tril


def _ssd_fwd(proj, conv_w, conv_b, dtb, alog, dskip, nb, seq):
    qc = SSD_CHUNK
    nc = seq // qc

    def body(xs_ref, bc_ref, dt_ref, cw_ref, cb_ref, dtb_ref, al_ref, ds_ref,
             y_ref, st_ref, pxs_ref, pbc_ref, state_ref):
        @pl.when(pl.program_id(1) == 0)
        def _():
            pxs_ref[...] = jnp.zeros_like(pxs_ref)
            pbc_ref[...] = jnp.zeros_like(pbc_ref)
            state_ref[...] = jnp.zeros_like(state_ref)

        expand = _head_expand()
        xs_raw = xs_ref[...]
        bc_raw = bc_ref[...]
        cw = cw_ref[...]
        cb = cb_ref[...]
        xs = _silu(_conv_pre(xs_raw, pxs_ref[...], cw[:, :D_BRANCH], cb[:, :D_BRANCH]))
        bc = _silu(_conv_pre(bc_raw, pbc_ref[...], cw[:, D_BRANCH:], cb[:, D_BRANCH:]))
        pxs_ref[...] = xs_raw
        pbc_ref[...] = bc_raw

        dt, d_a, a_cs, dt_x, acs_x, tril = _chunk_decay(
            dt_ref[...], dtb_ref[...], al_ref[...], expand, qc)
        a_cst = a_cs.T
        aend_x = acs_x[qc - 1:qc, :]
        ea_x = jnp.exp(acs_x)
        dec_x = jnp.exp(aend_x - acs_x)
        xt = xs * dt_x
        xtb = xt.astype(BF16)
        xdb = (xt * dec_x).astype(BF16)
        d_x = _dot(jnp.broadcast_to(ds_ref[...], (8, LANES)), expand, precision=HIGHEST)[0:1, :]
        st_ref[0, 0] = state_ref[...]

        for g in range(N_GROUPS):
            gs = slice(g * GROUP_W, (g + 1) * GROUP_W)
            bg = bc[:, g * D_STATE:(g + 1) * D_STATE]
            cg = bc[:, (N_GROUPS + g) * D_STATE:(N_GROUPS + g + 1) * D_STATE]
            bgb = bg.astype(BF16)
            cgb = cg.astype(BF16)
            cbm = _dot(cgb, bgb, _NT)
            st_in = state_ref[g]
            y_off = _dot(cgb, st_in.astype(BF16)) * ea_x[:, gs]
            for k in range(HEADS_PER_GROUP):
                h = g * HEADS_PER_GROUP + k
                hs = slice(h * HEAD_DIM, (h + 1) * HEAD_DIM)
                seg = a_cs[:, h:h + 1] - a_cst[h:h + 1, :]
                gh = cbm * jnp.exp(jnp.where(tril, seg, -1e30))
                y_h = _dot(gh.astype(BF16), xtb[:, hs]) + y_off[:, k * HEAD_DIM:(k + 1) * HEAD_DIM]
                y_ref[:, hs] = y_h + d_x[:, hs] * xs[:, hs]
            state_ref[g] = st_in * jnp.exp(aend_x[:, gs]) + _dot(bg.T.astype(BF16), xdb[:, gs])

    nblk = lambda w, off: pl.BlockSpec((qc, w), lambda b, c: (b * nc + c, off))
    full = lambda r, w: pl.BlockSpec((r, w), lambda b, c: (0, 0))
    return pl.pallas_call(
        body, name="ssd_fwd",
        grid=(nb, nc),
        in_specs=[nblk(D_BRANCH, COL_XS // D_BRANCH), nblk(D_BC, COL_BC // D_BC),
                  nblk(LANES, COL_DT // LANES),
                  full(CONV_TAPS, D_CONV), full(1, D_CONV), full(1, LANES), full(1, LANES),
                  full(1, LANES)],
        out_specs=[pl.BlockSpec((qc, D_BRANCH), lambda b, c: (b * nc + c, 0)),
                   pl.BlockSpec((1, 1, N_GROUPS, D_STATE, GROUP_W), lambda b, c: (b, c, 0, 0, 0))],
        out_shape=[jax.ShapeDtypeStruct((nb * seq, D_BRANCH), F32),
                   jax.ShapeDtypeStruct((nb, nc, N_GROUPS, D_STATE, GROUP_W), F32)],
        scratch_shapes=[pltpu.VMEM((qc, D_BRANCH), F32), pltpu.VMEM((qc, D_BC), F32),
                        pltpu.VMEM((N_GROUPS, D_STATE, GROUP_W), F32)],
        compiler_params=_params(2),
    )(proj, proj, proj, conv_w, conv_b, dtb, alog, dskip)


def _ssd_bwd(proj, d_y, states, conv_w, conv_b, dtb, alog, dskip, nb, seq):
    qc = SSD_CHUNK
    nc = seq // qc

    def body(xs_ref, bc_ref, dt_ref, pxs_ref, pbc_ref, dy_ref, st_ref, stn_ref,
             cw_ref, cb_ref, dtb_ref, al_ref, ds_ref,
             dx_ref, gcw_ref, gcb_ref, gdtb_ref, gal_ref, gds_ref,
             dst_ref, nxs_ref, nbc_ref, yd_ref, dxt_ref):
        step = pl.program_id(1)
        chunk = nc - 1 - step

        @pl.when(step == 0)
        def _():
            dst_ref[...] = jnp.zeros_like(dst_ref)
            nxs_ref[...] = jnp.zeros_like(nxs_ref)
            nbc_ref[...] = jnp.zeros_like(nbc_ref)

        @pl.when((pl.program_id(0) == 0) & (step == 0))
        def _():
            gcw_ref[...] = jnp.zeros_like(gcw_ref)
            gcb_ref[...] = jnp.zeros_like(gcb_ref)
            gdtb_ref[...] = jnp.zeros_like(gdtb_ref)
            gal_ref[...] = jnp.zeros_like(gal_ref)
            gds_ref[...] = jnp.zeros_like(gds_ref)

        expand = _head_expand()
        collapse = lambda v: _dot(v, expand, _NT, precision=HIGHEST)
        first = jnp.where(chunk == 0, 0.0, 1.0)
        xs_raw = xs_ref[...]
        bc_raw = bc_ref[...]
        pxs = pxs_ref[...] * first
        pbc = pbc_ref[...] * first
        cw = cw_ref[...]
        cb = cb_ref[...]
        pre_xs = _conv_pre(xs_raw, pxs, cw[:, :D_BRANCH], cb[:, :D_BRANCH])
        pre_bc = _conv_pre(bc_raw, pbc, cw[:, D_BRANCH:], cb[:, D_BRANCH:])
        xs = _silu(pre_xs)
        bc = _silu(pre_bc)

        dt_in = dt_ref[...] + dtb_ref[...]
        dt, d_a, a_cs, dt_x, acs_x, tril = _chunk_decay(
            dt_ref[...], dtb_ref[...], al_ref[...], expand, qc)
        a_cst = a_cs.T
        aend_x = acs_x[qc - 1:qc, :]
        ea_x = jnp.exp(acs_x)
        dec_x = jnp.exp(aend_x - acs_x)
        xt = xs * dt_x
        xtb = xt.astype(BF16)
        xdb = (xt * dec_x).astype(BF16)
        d_x = _dot(jnp.broadcast_to(ds_ref[...], (8, LANES)), expand, precision=HIGHEST)[0:1, :]

        dy = dy_ref[...]
        dyb = dy.astype(BF16)
        dyeab = (dy * ea_x).astype(BF16)
        gds_ref[...] += collapse(jnp.broadcast_to(jnp.sum(dy * xs, axis=0, keepdims=True),
                                                  (8, D_BRANCH)))[0:1, :]

        d_bc = []
        d_cc = []
        y_offs = []
        dxt_states = []
        end_terms = []
        for g in range(N_GROUPS):
            gs = slice(g * GROUP_W, (g + 1) * GROUP_W)
            bg = bc[:, g * D_STATE:(g + 1) * D_STATE]
            cg = bc[:, (N_GROUPS + g) * D_STATE:(N_GROUPS + g + 1) * D_STATE]
            bgb = bg.astype(BF16)
            cgb = cg.astype(BF16)
            cbm = _dot(cgb, bgb, _NT)
            st_in = st_ref[0, 0, g]
            st_inb = st_in.astype(BF16)
            d_st = dst_ref[g]
            d_stb = d_st.astype(BF16)
            y_offs.append(_dot(cgb, st_inb) * ea_x[:, gs])
            dxt_states.append(_dot(bgb, d_stb) * dec_x[:, gs])
            d_c = _dot(dyeab[:, gs], st_inb, _NT)
            d_b = _dot(xdb[:, gs], d_stb, _NT)
            d_cb = jnp.zeros((qc, qc), F32)
            for k in range(HEADS_PER_GROUP):
                h = g * HEADS_PER_GROUP + k
                hs = slice(h * HEAD_DIM, (h + 1) * HEAD_DIM)
                seg = a_cs[:, h:h + 1] - a_cst[h:h + 1, :]
                lh = jnp.exp(jnp.where(tril, seg, -1e30))
                ghb = (cbm * lh).astype(BF16)
                d_cb = d_cb + _dot(dyb[:, hs], xtb[:, hs], _NT) * lh
                yd_ref[:, hs] = _dot(ghb, xtb[:, hs])
                dxt_ref[:, hs] = _dot(ghb, dyb[:, hs], _TN)
            d_cbb = d_cb.astype(BF16)
            d_cc.append(d_c + _dot(d_cbb, bgb))
            d_bc.append(d_b + _dot(d_cbb, cgb, _TN))
            end_terms.append(jnp.sum(d_st * stn_ref[0, 0, g], axis=0, keepdims=True))
            dst_ref[g] = d_st * jnp.exp(aend_x[:, gs]) + _dot(cg.T.astype(BF16), dyeab[:, gs])

        y_off = jnp.concatenate(y_offs, axis=1)
        dxt_state = jnp.concatenate(dxt_states, axis=1)
        dxt = dxt_ref[...] + dxt_state
        last = jnp.where(chunk == nc - 1, 0.0, 1.0)
        end_c = collapse(jnp.broadcast_to(jnp.concatenate(end_terms, axis=1), (8, D_BRANCH)))[0:1, :]
        da_cs = collapse(dyb.astype(F32) * yd_ref[...] - dxt_ref[...] * xtb.astype(F32)
                         + dy * y_off - dxt_state * xt)
        da_cs = da_cs + jnp.where(_row_iota(da_cs.shape) == qc - 1, end_c * last, 0.0)
        triu = lax.broadcasted_iota(jnp.int32, (qc, qc), 0) <= lax.broadcasted_iota(jnp.int32, (qc, qc), 1)
        dd_a = _dot(jnp.where(triu, 1.0, 0.0).astype(F32), da_cs, precision=HIGHEST)
        ddt = dd_a * (-jnp.exp(al_ref[...])) + collapse(dxt * xs)
        head_lanes = _lane_iota(ddt.shape) < N_HEADS
        ddt_raw = jnp.where(head_lanes, ddt * _sigmoid(dt_in), 0.0)
        gal_ref[...] += jnp.sum(jnp.where(head_lanes, dd_a * d_a, 0.0), axis=0, keepdims=True)
        gdtb_ref[...] += jnp.sum(ddt_raw, axis=0, keepdims=True)

        dpre_xs = (dxt * dt_x + d_x * dy) * _silu_grad(pre_xs)
        dpre_bc = jnp.concatenate(d_bc + d_cc, axis=1) * _silu_grad(pre_bc)
        gcb_ref[...] += jnp.concatenate([jnp.sum(dpre_xs, axis=0, keepdims=True),
                                         jnp.sum(dpre_bc, axis=0, keepdims=True)], axis=1)
        nxs = nxs_ref[...]
        nbc = nbc_ref[...]
        du_xs = jnp.zeros_like(dpre_xs)
        du_bc = jnp.zeros_like(dpre_bc)
        for i in range(CONV_TAPS):
            k = CONV_TAPS - 1 - i
            gcw_ref[i:i + 1, :] += jnp.concatenate(
                [jnp.sum(dpre_xs * _shift_down(xs_raw, pxs, k), axis=0, keepdims=True),
                 jnp.sum(dpre_bc * _shift_down(bc_raw, pbc, k), axis=0, keepdims=True)], axis=1)
            du_xs = du_xs + _shift_up(dpre_xs, nxs, k) * cw[i:i + 1, :D_BRANCH]
            du_bc = du_bc + _shift_up(dpre_bc, nbc, k) * cw[i:i + 1, D_BRANCH:]
        nxs_ref[...] = dpre_xs
        nbc_ref[...] = dpre_bc

        dx_ref[:, :D_BRANCH] = du_xs.astype(BF16)
        dx_ref[:, D_BRANCH:D_CONV] = du_bc.astype(BF16)
        dx_ref[:, D_CONV:D_CONV + LANES] = ddt_raw.astype(BF16)
        dx_ref[:, D_CONV + LANES:] = jnp.zeros((qc, 2048 - D_CONV - LANES), BF16)

    rev = lambda b, c: b * nc + (nc - 1 - c)
    prv = lambda b, c: b * nc + jnp.maximum(nc - 2 - c, 0)
    nblk = lambda w, off, f: pl.BlockSpec((qc, w), lambda b, c: (f(b, c), off))
    full = lambda r, w: pl.BlockSpec((r, w), lambda b, c: (0, 0))
    st_spec = lambda f: pl.BlockSpec((1, 1, N_GROUPS, D_STATE, GROUP_W),
                                     lambda b, c: (b, f(c), 0, 0, 0))
    return pl.pallas_call(
        body, name="ssd_bwd",
        grid=(nb, nc),
        in_specs=[nblk(D_BRANCH, COL_XS // D_BRANCH, rev), nblk(D_BC, COL_BC // D_BC, rev),
                  nblk(LANES, COL_DT // LANES, rev),
                  nblk(D_BRANCH, COL_XS // D_BRANCH, prv), nblk(D_BC, COL_BC // D_BC, prv),
                  nblk(D_BRANCH, 0, rev),
                  st_spec(lambda c: nc - 1 - c), st_spec(lambda c: jnp.minimum(nc - c, nc - 1)),
                  full(CONV_TAPS, D_CONV), full(1, D_CONV), full(1, LANES), full(1, LANES),
                  full(1, LANES)],
        out_specs=[nblk(2048, 0, rev), full(8, D_CONV), full(1, D_CONV), full(1, LANES),
                   full(1, LANES), full(1, LANES)],
        out_shape=[jax.ShapeDtypeStruct((nb * seq, 2048), BF16),
                   jax.ShapeDtypeStruct((8, D_CONV), F32), jax.ShapeDtypeStruct((1, D_CONV), F32),
                   jax.ShapeDtypeStruct((1, LANES), F32), jax.ShapeDtypeStruct((1, LANES), F32),
                   jax.ShapeDtypeStruct((1, LANES), F32)],
        scratch_shapes=[pltpu.VMEM((N_GROUPS, D_STATE, GROUP_W), F32),
                        pltpu.VMEM((qc, D_BRANCH), F32), pltpu.VMEM((qc, D_BC), F32),
                        pltpu.VMEM((qc, D_BRANCH), F32), pltpu.VMEM((qc, D_BRANCH), F32)],
        compiler_params=_params(2),
    )(proj, proj, proj, proj, proj, d_y, states, states, conv_w, conv_b, dtb, alog, dskip)


def _mid(o_sb, y_ssd, proj, x2, target, sb_w, ssd_w, w_out_b, tm):
    t = x2.shape[0]
    inv_d = 1.0 / D_MODEL

    def body(o_ref, y_ref, zsb_ref, zssd_ref, x_ref, tg_ref, sbw_ref, ssdw_ref, w_ref,
             dout_ref, dosb_ref, dy_ref, dz_ref, gw_ref, gsb_ref, gssd_ref, loss_ref):
        @pl.when(pl.program_id(0) == 0)
        def _():
            gw_ref[...] = jnp.zeros_like(gw_ref)
            gsb_ref[...] = jnp.zeros_like(gsb_ref)
            gssd_ref[...] = jnp.zeros_like(gssd_ref)
            loss_ref[...] = jnp.zeros_like(loss_ref)

        def branch(val, z, w):
            gate = _silu(z)
            g = val * gate
            r = lax.rsqrt(jnp.mean(g * g, axis=1, keepdims=True) + EPS)
            xhat = g * r
            return gate, r, xhat, (xhat * w).astype(BF16)

        o = o_ref[...]
        y = y_ref[...]
        z_sb = zsb_ref[...]
        z_ssd = zssd_ref[...]
        gate_a, r_a, xhat_a, mix_a = branch(o, z_sb, sbw_ref[...])
        gate_b, r_b, xhat_b, mix_b = branch(y, z_ssd, ssdw_ref[...])
        out = x_ref[...] + _dot(mix_a, w_ref[:D_BRANCH, :]) + _dot(mix_b, w_ref[D_BRANCH:, :])
        diff = out - tg_ref[...]
        loss_ref[...] += 0.5 * inv_d * jnp.sum(diff * diff)
        d_out = diff * inv_d
        dout_ref[...] = d_out
        d_outb = d_out.astype(BF16)
        gw_ref[:D_BRANCH, :] += _dot(mix_a, d_outb, _TN)
        gw_ref[D_BRANCH:, :] += _dot(mix_b, d_outb, _TN)

        def branch_bwd(dmix, val, z, w, gate, r, xhat):
            gg = dmix * w
            m = jnp.mean(gg * xhat, axis=1, keepdims=True)
            dg = r * (gg - xhat * m)
            return dg * gate, dg * val * _silu_grad(z), jnp.sum(dmix * xhat, axis=0, keepdims=True)

        dmix_a = _dot(d_outb, w_ref[:D_BRANCH, :], _NT)
        dmix_b = _dot(d_outb, w_ref[D_BRANCH:, :], _NT)
        d_o, dz_a, gsb = branch_bwd(dmix_a, o, z_sb, sbw_ref[...], gate_a, r_a, xhat_a)
        d_y, dz_b, gssd = branch_bwd(dmix_b, y, z_ssd, ssdw_ref[...], gate_b, r_b, xhat_b)
        dosb_ref[...] = d_o
        dy_ref[...] = d_y
        dz_ref[:, :D_BRANCH] = dz_a.astype(BF16)
        dz_ref[:, D_BRANCH:] = dz_b.astype(BF16)
        gsb_ref[...] += gsb
        gssd_ref[...] += gssd

    row = lambda w, off: pl.BlockSpec((tm, w), lambda i: (i, off))
    full = lambda r, w: pl.BlockSpec((r, w), lambda i: (0, 0))
    resident = pl.BlockSpec((2 * D_BRANCH, D_MODEL), lambda i: (0, 0), pipeline_mode=pl.Buffered(1))
    tok = jax.ShapeDtypeStruct((t, D_MODEL), F32)
    return pl.pallas_call(
        body, name="mid",
        grid=(t // tm,),
        in_specs=[row(D_BRANCH, 0), row(D_BRANCH, 0), row(D_BRANCH, 3), row(D_BRANCH, 4),
                  row(D_MODEL, 0), row(D_MODEL, 0), full(1, D_BRANCH), full(1, D_BRANCH),
                  resident],
        out_specs=[row(D_MODEL, 0), row(D_BRANCH, 0), row(D_BRANCH, 0), row(2 * D_BRANCH, 0),
                   resident, full(1, D_BRANCH), full(1, D_BRANCH),
                   full(1, LANES)],
        out_shape=[tok, tok, tok, jax.ShapeDtypeStruct((t, 2 * D_BRANCH), BF16),
                   jax.ShapeDtypeStruct((2 * D_BRANCH, D_MODEL), F32),
                   jax.ShapeDtypeStruct((1, D_BRANCH), F32), jax.ShapeDtypeStruct((1, D_BRANCH), F32),
                   jax.ShapeDtypeStruct((1, LANES), F32)],
        compiler_params=_params(1),
    )(o_sb, y_ssd, proj, proj, x2, target, sb_w, ssd_w, w_out_b)


_DPROJ_FIRST = (0, 1, 2, 3, 5)
_DPROJ_BLOCKS = (1, 1, 1, 2, 2)
_DPROJ_OWNER = (0, 1, 2, 3, 3, 4, 4)


def _dproj_col(j, p):
    return jnp.clip(j - _DPROJ_FIRST[p], 0, _DPROJ_BLOCKS[p] - 1)


def _in_proj_bwd_x(d_parts, w_in_b, x2, d_out, norm_w, tm, slabs=()):
    t = x2.shape[0]
    n_parts = len(d_parts)
    n_slabs = len(slabs)
    n_rows = t // tm

    def body(*refs):
        dp_refs = refs[:n_parts]
        w_ref, x_ref, dout_ref, nw_ref = refs[n_parts:n_parts + 4]
        rest = refs[n_parts + 4:]
        src_refs, (gx_ref, gnw_ref) = rest[:n_slabs], rest[n_slabs:n_slabs + 2]
        dst_refs = rest[n_slabs + 2:2 * n_slabs + 2]
        acc_ref = rest[2 * n_slabs + 2]
        sems = rest[2 * n_slabs + 3:]
        i, j = pl.program_id(0), pl.program_id(1)

        if n_slabs:
            @pl.when((i == 0) & (j == 0))
            def _():
                _exchange_start(_exchange_copies(src_refs, dst_refs, (True,) * n_slabs, *sems))

            @pl.when((i == n_rows - 1) & (j == N_COLBLK - 1))
            def _():
                _exchange_wait(_exchange_copies(src_refs, dst_refs, (True,) * n_slabs, *sems))

        @pl.when((i == 0) & (j == 0))
        def _():
            gnw_ref[...] = jnp.zeros_like(gnw_ref)

        @pl.when(j == 0)
        def _():
            acc_ref[...] = jnp.zeros_like(acc_ref)

        for jj in range(N_COLBLK):
            @pl.when(j == jj)
            def _(jj=jj):
                acc_ref[...] += _dot(dp_refs[_DPROJ_OWNER[jj]][...], w_ref[...], _NT)

        @pl.when(j == N_COLBLK - 1)
        def _():
            xf = x_ref[...]
            d_hn = acc_ref[...]
            r = lax.rsqrt(jnp.mean(xf * xf, axis=1, keepdims=True) + EPS)
            xhat = xf * r
            g = d_hn * nw_ref[...]
            m = jnp.mean(g * xhat, axis=1, keepdims=True)
            gx_ref[...] = dout_ref[...] + r * (g - xhat * m)
            gnw_ref[...] += jnp.sum(d_hn * xhat, axis=0, keepdims=True)

    return pl.pallas_call(
        body, name="in_proj_bwd_x",
        grid=(t // tm, N_COLBLK),
        in_specs=[pl.BlockSpec((tm, 1024), lambda i, j, p=p: (i, _dproj_col(j, p)))
                  for p in range(n_parts)] + [
                  pl.BlockSpec((D_MODEL, 1024), lambda i, j: (0, j)),
                  pl.BlockSpec((tm, D_MODEL), lambda i, j: (i, 0)),
                  pl.BlockSpec((tm, D_MODEL), lambda i, j: (i, 0)),
                  pl.BlockSpec((1, D_MODEL), lambda i, j: (0, 0))] + [_ANY] * n_slabs,
        out_specs=[pl.BlockSpec((tm, D_MODEL), lambda i, j: (i, 0)),
                   pl.BlockSpec((1, D_MODEL), lambda i, j: (0, 0))] + [_ANY] * n_slabs,
        out_shape=[jax.ShapeDtypeStruct((t, D_MODEL), F32), jax.ShapeDtypeStruct((1, D_MODEL), F32)]
                  + _exchange_shapes(slabs, (True,) * n_slabs),
        scratch_shapes=[pltpu.VMEM((tm, D_MODEL), F32)] + (_exchange_sems(n_slabs) if n_slabs else []),
        compiler_params=_params(2),
    )(*d_parts, w_in_b, x2, d_out, norm_w, *slabs)


def _in_proj_bwd_w(hn, d_parts, tm):
    t = hn.shape[0]
    n_parts = len(d_parts)

    def body(hn_ref, *refs):
        dp_refs, gw_ref = refs[:n_parts], refs[n_parts]
        j = pl.program_id(0)

        @pl.when(pl.program_id(1) == 0)
        def _():
            gw_ref[...] = jnp.zeros_like(gw_ref)

        for jj in range(N_COLBLK):
            @pl.when(j == jj)
            def _(jj=jj):
                gw_ref[...] += _dot(hn_ref[...], dp_refs[_DPROJ_OWNER[jj]][...], _TN)

    def part_spec(p):
        def index(j, i):
            mine = (j >= _DPROJ_FIRST[p]) & (j < _DPROJ_FIRST[p] + _DPROJ_BLOCKS[p])
            return jnp.where(mine, i, 0), _dproj_col(j, p)
        return pl.BlockSpec((tm, 1024), index)

    return pl.pallas_call(
        body, name="in_proj_bwd_w",
        grid=(N_COLBLK, t // tm),
        in_specs=[pl.BlockSpec((tm, D_MODEL), lambda j, i: (i, 0))]
                 + [part_spec(p) for p in range(n_parts)],
        out_specs=pl.BlockSpec((D_MODEL, 1024), lambda j, i: (0, j)),
        out_shape=jax.ShapeDtypeStruct((D_MODEL, D_IN_PAD), F32),
        compiler_params=_params(2),
    )(hn, *d_parts)


def _adamw(parts, w, m, v, tr, name):
    rows, cols = w.shape
    c1 = 1.0 - ADAM_B1 ** ADAM_STEP
    c2 = 1.0 - ADAM_B2 ** ADAM_STEP

    def body(p_ref, w_ref, m_ref, v_ref, g_ref, d_ref, nm_ref, nv_ref):
        g = p_ref[0].astype(F32)
        for s in range(1, N_DEV):
            g = g + p_ref[s].astype(F32)
        nm = ADAM_B1 * m_ref[...] + (1.0 - ADAM_B1) * g
        nv = ADAM_B2 * v_ref[...] + (1.0 - ADAM_B2) * (g * g)
        g_ref[...] = g
        nm_ref[...] = nm
        nv_ref[...] = nv
        d_ref[...] = -ADAM_LR * ((nm / c1) / (jnp.sqrt(nv / c2) + ADAM_EPS) + ADAM_WD * w_ref[...])

    blk = pl.BlockSpec((tr, cols), lambda i: (i, 0))
    shape = jax.ShapeDtypeStruct((rows, cols), F32)
    return pl.pallas_call(
        body, name=name,
        grid=(rows // tr,),
        in_specs=[pl.BlockSpec((N_DEV, tr, cols), lambda i: (0, i, 0)), blk, blk, blk],
        out_specs=[blk, blk, blk, blk],
        out_shape=[shape, shape, shape, shape],
        compiler_params=_params(1),
    )(parts, w, m, v)


def _mesh_place():
    x, y, c = lax.axis_index("x"), lax.axis_index("y"), lax.axis_index("c")
    return x, y, c, 4 * x + 2 * y + c


def _peer(x, y, c, k):
    px = 1 - x if k & 4 else x
    py = 1 - y if k & 2 else y
    pc = 1 - c if k & 1 else c
    return (px, py, pc), 4 * px + 2 * py + pc


def _exchange(srcs, scatter, name):
    n = len(srcs)

    def body(*refs):
        copies = _exchange_copies(refs[:n], refs[n:2 * n], scatter, *refs[2 * n:])
        _exchange_start(copies)
        _exchange_wait(copies)

    return pl.pallas_call(
        body, name=name,
        in_specs=[_ANY] * n, out_specs=[_ANY] * n, out_shape=_exchange_shapes(srcs, scatter),
        scratch_shapes=_exchange_sems(n),
    )(*srcs)


_ANY = pl.BlockSpec(memory_space=pl.ANY)


def _exchange_shapes(srcs, scatter):
    return [jax.ShapeDtypeStruct(s.shape if sc else (N_DEV,) + s.shape, s.dtype)
            for s, sc in zip(srcs, scatter)]


def _exchange_sems(n):
    return [pltpu.SemaphoreType.DMA((n * (N_DEV - 1),)),
            pltpu.SemaphoreType.DMA((n * (N_DEV - 1),)),
            pltpu.SemaphoreType.DMA((n,))]


def _exchange_copies(src_refs, dst_refs, scatter, send_sems, recv_sems, loc_sems):
    n = len(src_refs)
    x, y, c, me = _mesh_place()

    def src_of(i, idx):
        return src_refs[i].at[idx] if scatter[i] else src_refs[i]

    local = [pltpu.make_async_copy(src_of(i, me), dst_refs[i].at[me], loc_sems.at[i])
             for i in range(n)]
    sends, recvs = [], []
    for k in range(1, N_DEV):
        peer, pidx = _peer(x, y, c, k)
        for i in range(n):
            s = i * (N_DEV - 1) + k - 1
            for dst_slab, group in ((me, sends), (pidx, recvs)):
                group.append(pltpu.make_async_remote_copy(
                    src_ref=src_of(i, pidx), dst_ref=dst_refs[i].at[dst_slab],
                    send_sem=send_sems.at[s], recv_sem=recv_sems.at[s],
                    device_id=peer, device_id_type=pl.DeviceIdType.MESH))
    return local, sends, recvs


def _exchange_start(copies):
    local, sends, _ = copies
    for cp in local + sends:
        cp.start()


def _exchange_wait(copies):
    local, sends, recvs = copies
    for cp in recvs:
        cp.wait_recv()
    for cp in sends:
        cp.wait_send()
    for cp in local:
        cp.wait()


def _pad_lanes(v, width=LANES):
    return jnp.pad(v, ((0, 0), (0, width - v.shape[1])))


def _local_step(x, target, norm_w, w_in_b, q_norm_w, k_norm_w, conv_w, conv_b, dt_bias, a_log,
                d_skip, sb_norm_w, ssd_norm_w, w_out_b, tm=512, tq=512, tmid=256, blk=ATT_BLK,
                scatter=False):
    nb, seq, _ = x.shape
    t = nb * seq
    x2 = x.reshape(t, D_MODEL)
    tg2 = target.reshape(t, D_MODEL)
    qw2 = jnp.tile(q_norm_w, (1, 2))
    kw2 = jnp.tile(k_norm_w, (1, 2))
    dtb, alog, dsk = _pad_lanes(dt_bias), _pad_lanes(a_log), _pad_lanes(d_skip)

    tproj = min(2 * tm, t)
    if scatter:
        proj, hn, wout_all, cw_all = _in_proj(x2, norm_w, w_in_b, tproj, (w_out_b, conv_w))
        w_out_b = wout_all.reshape(2 * D_BRANCH, D_MODEL)
        conv_w = jnp.transpose(cw_all, (1, 0, 2)).reshape(CONV_TAPS, D_CONV)
    else:
        proj, hn = _in_proj(x2, norm_w, w_in_b, tproj)
    qs, kn, vb, kt = _qk_prep(proj, qw2, kw2, nb, seq, tq)
    o_sb, sb_tot, sb_low = _attn_fwd(qs, kn, vb, nb, seq, blk)
    y_ssd, states = _ssd_fwd(proj, conv_w, conv_b, dtb, alog, dsk, nb, seq)
    d_out, d_osb, d_y, d_z, g_wout, g_sbw, g_ssdw, loss = _mid(
        o_sb, y_ssd, proj, x2, tg2, sb_norm_w, ssd_norm_w, w_out_b, tmid)
    dqs, dkn, dvh = _attn_bwd(qs, kn, kt, vb, sb_tot, sb_low, d_osb, nb, seq, blk)
    dq_raw, dk_raw, dv_raw, g_qw, g_kw = _qk_bwd(proj, dqs, dkn, dvh, qw2, kw2, nb, seq, tq)
    d_xbc, g_cw, g_cb, g_dtb, g_alog, g_dsk = _ssd_bwd(
        proj, d_y, states, conv_w, conv_b, dtb, alog, dsk, nb, seq)
    d_parts = [dq_raw, dk_raw, dv_raw, d_z, d_xbc]
    g_win = _in_proj_bwd_w(hn, d_parts, tm)[:, :D_IN]
    g_cw = g_cw[:CONV_TAPS]
    if scatter:
        grad_x, g_nw, g_win, g_wout, g_cw = _in_proj_bwd_x(
            d_parts, w_in_b, x2, d_out, norm_w, tm, _grad_slabs(g_win, g_wout, g_cw))
    else:
        grad_x, g_nw = _in_proj_bwd_x(d_parts, w_in_b, x2, d_out, norm_w, tm)

    small = dict(
        norm_w=g_nw,
        q_norm_w=g_qw[:, :HEAD_DIM] + g_qw[:, HEAD_DIM:],
        k_norm_w=g_kw[:, :HEAD_DIM] + g_kw[:, HEAD_DIM:],
        conv_b=g_cb, dt_bias=g_dtb[:, :N_HEADS], A_log=g_alog[:, :N_HEADS],
        D_skip=g_dsk[:, :N_HEADS], sb_norm_w=g_sbw, ssd_norm_w=g_ssdw)
    return loss[0, 0], grad_x.reshape(nb, seq, D_MODEL), g_win, g_wout, g_cw, small


def _grad_slabs(g_win, g_wout, g_cw):
    w_sh = D_IN // N_DEV
    c_sh = D_CONV // N_DEV
    return (jnp.transpose(g_win.reshape(D_MODEL, N_DEV, w_sh), (1, 0, 2)).astype(BF16),
            g_wout.reshape(N_DEV, 2 * D_BRANCH // N_DEV, D_MODEL).astype(BF16),
            jnp.pad(jnp.transpose(g_cw.reshape(CONV_TAPS, N_DEV, c_sh), (1, 0, 2)),
                    ((0, 0), (0, 8 - CONV_TAPS), (0, 0))))


_SMALL = ("norm_w", "q_norm_w", "k_norm_w", "conv_b", "dt_bias", "A_log", "D_skip",
          "sb_norm_w", "ssd_norm_w")


def _pack_small(vals):
    rows = [_pad_lanes(vals[n], -(-vals[n].shape[1] // LANES) * LANES).reshape(-1, LANES)
            for n in _SMALL]
    packed = jnp.concatenate(rows, axis=0)
    return jnp.pad(packed, ((0, 48 - packed.shape[0]), (0, 0)))


def _unpack_small(packed, like):
    out, r = {}, 0
    for n in _SMALL:
        width = like[n].shape[1]
        nr = -(-width // LANES)
        out[n] = packed[r:r + nr].reshape(1, nr * LANES)[:, :width]
        r += nr
    return out


def kernel(x, norm_w, w_in, q_norm_w, k_norm_w, conv_w, conv_b, dt_bias, A_log, D_skip, sb_norm_w, ssd_norm_w, w_out, loss_target, m_norm_w, m_w_in, m_q_norm_w, m_k_norm_w, m_conv_w, m_conv_b, m_dt_bias, m_A_log, m_D_skip, m_sb_norm_w, m_ssd_norm_w, m_w_out, v_norm_w, v_w_in, v_q_norm_w, v_k_norm_w, v_conv_w, v_conv_b, v_dt_bias, v_A_log, v_D_skip, v_sb_norm_w, v_ssd_norm_w, v_w_out):
    win_all, = _exchange([w_in[0].astype(BF16)], [False], "gather_w_in")
    w_in_b = jnp.pad(jnp.transpose(win_all, (1, 0, 2)).reshape(D_MODEL, D_IN),
                     ((0, 0), (0, D_IN_PAD - D_IN)))

    loss, grad_x, win_parts, wout_parts, cw_parts, g_small = _local_step(
        x, loss_target, norm_w, w_in_b, q_norm_w, k_norm_w, conv_w[0], conv_b, dt_bias, A_log,
        D_skip, sb_norm_w, ssd_norm_w, w_out[0].astype(BF16), scatter=True)
    small_parts, = _exchange([_pack_small(g_small)], [False], "gather_small_grads")

    small_w = dict(norm_w=norm_w, q_norm_w=q_norm_w, k_norm_w=k_norm_w, conv_b=conv_b,
                   dt_bias=dt_bias, A_log=A_log, D_skip=D_skip, sb_norm_w=sb_norm_w,
                   ssd_norm_w=ssd_norm_w)
    small_m = dict(norm_w=m_norm_w, q_norm_w=m_q_norm_w, k_norm_w=m_k_norm_w, conv_b=m_conv_b,
                   dt_bias=m_dt_bias, A_log=m_A_log, D_skip=m_D_skip, sb_norm_w=m_sb_norm_w,
                   ssd_norm_w=m_ssd_norm_w)
    small_v = dict(norm_w=v_norm_w, q_norm_w=v_q_norm_w, k_norm_w=v_k_norm_w, conv_b=v_conv_b,
                   dt_bias=v_dt_bias, A_log=v_A_log, D_skip=v_D_skip, sb_norm_w=v_sb_norm_w,
                   ssd_norm_w=v_ssd_norm_w)

    pad8 = lambda a: jnp.pad(a, ((0, 8 - CONV_TAPS), (0, 0)))
    r_win = _adamw(win_parts, w_in[0], m_w_in[0], v_w_in[0], 128, "adamw_w_in")
    r_wout = _adamw(wout_parts, w_out[0], m_w_out[0], v_w_out[0], 128, "adamw_w_out")
    r_cw = _adamw(cw_parts, pad8(conv_w[0]), pad8(m_conv_w[0]), pad8(v_conv_w[0]), 8, "adamw_conv_w")
    r_small = _adamw(small_parts, _pack_small(small_w), _pack_small(small_m),
                     _pack_small(small_v), 48, "adamw_small")

    loss = lax.psum(loss, ("x", "y", "c"))
    res = {"w_in": [a[None] for a in r_win], "w_out": [a[None] for a in r_wout],
           "conv_w": [a[:CONV_TAPS][None] for a in r_cw]}
    unpacked = [_unpack_small(a, small_w) for a in r_small]
    for n in _SMALL:
        res[n] = [u[n] for u in unpacked]
    order = ("norm_w", "w_in", "q_norm_w", "k_norm_w", "conv_w", "conv_b", "dt_bias", "A_log",
             "D_skip", "sb_norm_w", "ssd_norm_w", "w_out")
    outs = [loss, grad_x]
    for kind in range(4):
        outs += [res[n][kind] for n in order]
    return tuple(outs)
```

```python
import functools
import math

import jax
import jax.numpy as jnp
from jax import lax
from jax.experimental import pallas as pl
from jax.experimental.pallas import tpu as pltpu

F32 = jnp.float32
BF16 = jnp.bfloat16
HIGHEST = lax.Precision.HIGHEST

D_MODEL = 1024
N_HEADS = 16
HEAD_DIM = 64
N_PAIRS = N_HEADS // 2
D_BRANCH = 1024
N_GROUPS = 2
HEADS_PER_GROUP = 8
D_STATE = 128
GROUP_W = HEADS_PER_GROUP * HEAD_DIM
D_BC = 2 * N_GROUPS * D_STATE
D_CONV = D_BRANCH + D_BC
D_IN = 6672
D_IN_PAD = 7168
N_COLBLK = D_IN_PAD // 1024
COL_XS = 5120
COL_BC = 6144
COL_DT = 6656
EPS = 1e-6
CONV_TAPS = 4
N_DEV = 8

LANES = 128
SSD_CHUNK = 128
ATT_BLK = 256
EXP_UNDERFLOW = -105.0
VMEM_LIMIT = 56 * 1024 * 1024

ADAM_LR = 0.001
ADAM_B1 = 0.9
ADAM_B2 = 0.999
ADAM_EPS = 1e-08
ADAM_WD = 0.01
ADAM_STEP = 10

_NT = (((1,), (1,)), ((), ()))
_TN = (((0,), (0,)), ((), ()))


def _params(n_grid):
    return pltpu.CompilerParams(dimension_semantics=("arbitrary",) * n_grid,
                                vmem_limit_bytes=VMEM_LIMIT)


def _dot(a, b, dims=None, precision=None):
    if dims is None:
        return jnp.dot(a, b, preferred_element_type=F32, precision=precision)
    return lax.dot_general(a, b, dims, preferred_element_type=F32, precision=precision)


def _sigmoid(x):
    return 1.0 / (1.0 + jnp.exp(-x))


def _softplus(x):
    return jnp.maximum(x, 0.0) + jnp.log(1.0 + jnp.exp(-jnp.abs(x)))


def _split_bf16(x):
    hi = x.astype(BF16)
    lo = (x - hi.astype(F32)).astype(BF16)
    return hi, lo


def _lane_iota(shape):
    return lax.broadcasted_iota(jnp.int32, shape, len(shape) - 1)


def _row_iota(shape):
    return lax.broadcasted_iota(jnp.int32, shape, len(shape) - 2)


def _pair_sum(x):
    r = lax.broadcasted_iota(jnp.int32, (LANES, LANES), 0)
    c = lax.broadcasted_iota(jnp.int32, (LANES, LANES), 1)
    same_head = jnp.where(r // HEAD_DIM == c // HEAD_DIM, 1.0, 0.0).astype(BF16)
    hi, lo = _split_bf16(x)
    return _dot(hi, same_head) + _dot(lo, same_head)


def _pair_head(x, a):
    lane = _lane_iota(x.shape)
    mine = (lane < HEAD_DIM) if a == 0 else (lane >= HEAD_DIM)
    return jnp.where(mine, x, jnp.zeros_like(x))


def _head_expand():
    r = lax.broadcasted_iota(jnp.int32, (LANES, D_BRANCH), 0)
    c = lax.broadcasted_iota(jnp.int32, (LANES, D_BRANCH), 1)
    return jnp.where(c // HEAD_DIM == r, 1.0, 0.0).astype(F32)


def _in_proj(x2, norm_w, w_in_b, tm, shards=()):
    t = x2.shape[0]
    n_sh = len(shards)
    n_rows = t // tm

    def body(x_ref, nw_ref, w_ref, *rest):
        src_refs, (proj_ref, hn_ref) = rest[:n_sh], rest[n_sh:n_sh + 2]
        dst_refs, sems = rest[n_sh + 2:2 * n_sh + 2], rest[2 * n_sh + 2:]
        i, j = pl.program_id(0), pl.program_id(1)

        if n_sh:
            @pl.when((i == 0) & (j == 0))
            def _():
                _exchange_start(_exchange_copies(src_refs, dst_refs, (False,) * n_sh, *sems))

            @pl.when((i == n_rows - 1) & (j == N_COLBLK - 1))
            def _():
                _exchange_wait(_exchange_copies(src_refs, dst_refs, (False,) * n_sh, *sems))

        @pl.when(j == 0)
        def _():
            xf = x_ref[...]
            r = lax.rsqrt(jnp.mean(xf * xf, axis=1, keepdims=True) + EPS)
            hn_ref[...] = (xf * r * nw_ref[...]).astype(BF16)

        proj_ref[...] = _dot(hn_ref[...], w_ref[...])

    return pl.pallas_call(
        body, name="in_proj",
        grid=(n_rows, N_COLBLK),
        in_specs=[pl.BlockSpec((tm, D_MODEL), lambda i, j: (i, 0)),
                  pl.BlockSpec((1, D_MODEL), lambda i, j: (0, 0)),
                  pl.BlockSpec((D_MODEL, 1024), lambda i, j: (0, j))] + [_ANY] * n_sh,
        out_specs=[pl.BlockSpec((tm, 1024), lambda i, j: (i, j)),
                   pl.BlockSpec((tm, D_MODEL), lambda i, j: (i, 0))] + [_ANY] * n_sh,
        out_shape=[jax.ShapeDtypeStruct((t, D_IN_PAD), F32),
                   jax.ShapeDtypeStruct((t, D_MODEL), BF16)]
                  + _exchange_shapes(shards, (False,) * n_sh),
        scratch_shapes=_exchange_sems(n_sh) if n_sh else [],
        compiler_params=_params(2),
    )(x2, norm_w, w_in_b, *shards)


def _qk_prep(proj, qw2, kw2, nb, seq, tq):
    nl = seq // tq
    scale = 1.0 / math.sqrt(HEAD_DIM)

    def body(q_ref, k_ref, v_ref, qw_ref, kw_ref, qs_ref, kn_ref, vb_ref, kt_ref):
        def norm(x, w):
            r = lax.rsqrt(_pair_sum(x * x) * (1.0 / HEAD_DIM) + EPS)
            return x * r * w

        vb_ref[...] = v_ref[...].astype(BF16)
        for p in range(N_PAIRS):
            cols = slice(p * LANES, (p + 1) * LANES)
            kn = norm(k_ref[:, cols], kw_ref[...])
            qs_ref[:, cols] = (norm(q_ref[:, cols], qw_ref[...]) * scale).astype(BF16)
            kn_ref[:, cols] = kn.astype(BF16)
            kt_ref[0, p] = kn.T.astype(BF16)

    tok_shape = jax.ShapeDtypeStruct((nb * seq, D_BRANCH), BF16)
    tok = lambda blk: pl.BlockSpec((tq, D_BRANCH), lambda b, i: (b * nl + i, blk))
    vec = pl.BlockSpec((1, LANES), lambda b, i: (0, 0))
    return pl.pallas_call(
        body, name="qk_prep",
        grid=(nb, nl),
        in_specs=[tok(0), tok(1), tok(2), vec, vec],
        out_specs=[tok(0), tok(0), tok(0),
                   pl.BlockSpec((1, N_PAIRS, LANES, tq), lambda b, i: (b, 0, 0, i))],
        out_shape=[tok_shape, tok_shape, tok_shape,
                   jax.ShapeDtypeStruct((nb, N_PAIRS, LANES, seq), BF16)],
        compiler_params=_params(2),
    )(proj, proj, proj, qw2, kw2)


def _attn_fwd(qs, kn, vb, nb, seq, blk):
    nq = seq // blk

    def body(q_ref, k_ref, v_ref, o_ref, tot_ref, low_ref, kmax_ref):
        qi = pl.program_id(2)
        r_i = lax.broadcasted_iota(jnp.int32, (blk, blk), 0)
        c_i = lax.broadcasted_iota(jnp.int32, (blk, blk), 1)
        csum = jnp.where(r_i >= c_i, 1.0, 0.0).astype(BF16)
        causal = c_i < r_i
        heads = range(2)

        head = _pair_head

        @pl.when(qi == 0)
        def _():
            kk = k_ref[...].astype(F32)
            for a in heads:
                ksq = jnp.sum(head(kk * kk, a), axis=1, keepdims=True)
                kmax_ref[a] = jnp.full((8, LANES), jnp.max(ksq))

        q_pair = q_ref[...]
        qf = q_pair.astype(F32)
        q_head = [head(q_pair, a) for a in heads]
        zmax = []
        for a in heads:
            qsq = jnp.sum(head(qf * qf, a), axis=1, keepdims=True)
            zmax.append(1.01 * jnp.sqrt(qsq * kmax_ref[a][0:1, 0:1]) + 0.01)

        def exhausted(run):
            top = jnp.maximum(jnp.max(run[0] + zmax[0]), jnp.max(run[1] + zmax[1]))
            return top < EXP_UNDERFLOW

        def sweep(blocks, run, acc):
            offs = [pl.multiple_of(j * blk, blk) for j, _, _ in blocks]
            z = [[_dot(q_head[a], k_ref[pl.ds(off, blk), :], _NT) for a in heads]
                 for off in offs]
            cl = []
            for (_, diag, valid), zb in zip(blocks, z):
                lkb = []
                for a in heads:
                    lk = -_softplus(zb[a])
                    if diag:
                        lk = jnp.where(causal, lk, 0.0)
                    if valid is not None:
                        lk = jnp.where(valid, lk, 0.0)
                    lkb.append(lk.astype(BF16))
                cl.append([_dot(lkb[a], csum) for a in heads])
            for (_, diag, valid), zb, clb, off in zip(blocks, z, cl, offs):
                w = []
                for a in heads:
                    wa = jnp.exp(zb[a] + clb[a] + run[a])
                    if diag:
                        wa = jnp.where(causal, wa, 0.0)
                    if valid is not None:
                        wa = jnp.where(valid, wa, 0.0)
                    w.append(wa.astype(BF16))
                run = [run[a] + clb[a][:, 0:1] for a in heads]
                v_pair = v_ref[pl.ds(off, blk), :]
                acc = acc + _dot(w[0], head(v_pair, 0)) + _dot(w[1], head(v_pair, 1))
            return run, acc

        run = [jnp.zeros((blk, 1), F32)] * 2
        acc = jnp.zeros((blk, LANES), F32)
        run, acc = sweep([(qi, True, None), (jnp.maximum(qi - 1, 0), False, qi >= 1)], run, acc)
        low = jnp.maximum(qi - 1, 0)

        def more(carry):
            low, done, _, _ = carry
            return (low > 0) & jnp.logical_not(done)

        def pair(carry):
            low, _, run, acc = carry
            run, acc = sweep([(low - 1, False, None), (jnp.maximum(low - 2, 0), False, low >= 2)],
                             run, acc)
            return jnp.maximum(low - 2, 0), exhausted(run), run, acc

        low, _, run, acc = lax.while_loop(more, pair, (low, exhausted(run), run, acc))
        low_ref[pl.program_id(0) * N_PAIRS + pl.program_id(1), qi] = low.astype(F32)
        o_ref[...] = acc
        for a in heads:
            as_row = jnp.sum(jnp.where(r_i == c_i, run[a], 0.0), axis=0, keepdims=True)
            tot_ref[0, a, 0] = jnp.broadcast_to(as_row, (8, blk))

    return pl.pallas_call(
        body, name="sb_attn_fwd",
        grid=(nb, N_PAIRS, nq),
        in_specs=[pl.BlockSpec((blk, LANES), lambda b, h, i: (b * nq + i, h)),
                  pl.BlockSpec((seq, LANES), lambda b, h, i: (b, h)),
                  pl.BlockSpec((seq, LANES), lambda b, h, i: (b, h))],
        out_specs=[pl.BlockSpec((blk, LANES), lambda b, h, i: (b * nq + i, h)),
                   pl.BlockSpec((1, 2, 1, 8, blk), lambda b, h, i: (b, h, i, 0, 0)),
                   pl.BlockSpec(memory_space=pltpu.SMEM)],
        out_shape=[jax.ShapeDtypeStruct((nb * seq, D_BRANCH), F32),
                   jax.ShapeDtypeStruct((nb, N_HEADS, nq, 8, blk), F32),
                   jax.ShapeDtypeStruct((nb * N_PAIRS, nq), F32)],
        scratch_shapes=[pltpu.VMEM((2, 8, LANES), F32)],
        compiler_params=_params(3),
    )(qs, kn, vb)


def _attn_bwd(qs, kn, kt, vb, tot, low, d_o, nb, seq, blk):
    nq = seq // blk

    def body(q_ref, k_ref, kt_ref, v_ref, tot_ref, low_ref, do_ref, dq_ref, dk_ref, dv_ref):
        qi = pl.program_id(2)

        @pl.when(qi == 0)
        def _():
            dk_ref[...] = jnp.zeros_like(dk_ref)
            dv_ref[...] = jnp.zeros_like(dv_ref)

        r_i = lax.broadcasted_iota(jnp.int32, (blk, blk), 0)
        c_i = lax.broadcasted_iota(jnp.int32, (blk, blk), 1)
        before = jnp.where(c_i < r_i, 1.0, 0.0).astype(BF16)
        upto = jnp.where(c_i <= r_i, 1.0, 0.0).astype(BF16)
        causal = r_i < c_i

        heads = range(2)
        q_head = [_pair_head(q_ref[...], a) for a in heads]
        d_ob = [_pair_head(do_ref[...].astype(BF16), a) for a in heads]
        total = [tot_ref[0, a, 0][0:1, :] for a in heads]

        def sweep(blocks, lsum, esum, dqt):
            def keep(x, diag, valid):
                if diag:
                    x = jnp.where(causal, x, 0.0)
                if valid is not None:
                    x = jnp.where(valid, x, 0.0)
                return x

            offs = [pl.multiple_of(j * blk, blk) for j, _, _ in blocks]
            zt = [[_dot(k_ref[pl.ds(off, blk), :], q_head[a], _NT) for a in heads]
                  for off in offs]
            dwt = [[_dot(v_ref[pl.ds(off, blk), :], d_ob[a], _NT) for a in heads]
                   for off in offs]
            sp, lk, lpre = [], [], []
            for (_, diag, valid), ztb in zip(blocks, zt):
                sp.append([_softplus(ztb[a]) for a in heads])
                lk.append([keep(-sp[-1][a], diag, valid).astype(BF16) for a in heads])
                lpre.append([_dot(before, lk[-1][a]) for a in heads])
            wt, et, epre = [], [], []
            for i, (_, diag, valid) in enumerate(blocks):
                wt.append([keep(jnp.exp(zt[i][a] + (total[a] - lsum[a] - lpre[i][a])), diag, valid)
                           for a in heads])
                et.append([dwt[i][a] * wt[i][a] for a in heads])
                split = [_split_bf16(et[i][a]) for a in heads]
                epre.append([_dot(upto, split[a][0]) + _dot(upto, split[a][1]) for a in heads])
                lsum = [lsum[a] + lpre[i][a][blk - 1:blk, :] + lk[i][a][blk - 1:blk, :]
                        for a in heads]
            for i, (_, diag, valid) in enumerate(blocks):
                dzb = [keep(et[i][a] - jnp.exp(zt[i][a] - sp[i][a]) * (esum[a] + epre[i][a]),
                            diag, valid).astype(BF16) for a in heads]
                esum = [esum[a] + epre[i][a][blk - 1:blk, :] for a in heads]
                dk_ref[pl.ds(offs[i], blk), :] += (_dot(dzb[0], q_head[0]) + _dot(dzb[1], q_head[1]))
                dv_ref[pl.ds(offs[i], blk), :] += (_dot(wt[i][0].astype(BF16), d_ob[0])
                                                   + _dot(wt[i][1].astype(BF16), d_ob[1]))
                kt_pair = kt_ref[0, 0, :, pl.ds(offs[i], blk)]
                dqt = [dqt[a] + _dot(kt_pair, dzb[a]) for a in heads]
            return lsum, esum, dqt

        row = [jnp.zeros((1, blk), F32)] * 2
        dqt = [jnp.zeros((LANES, blk), F32)] * 2
        low = low_ref[pl.program_id(0) * N_PAIRS + pl.program_id(1), qi].astype(jnp.int32)
        low = jnp.clip(low, 0, jnp.maximum(qi - 1, 0))

        def pair(carry):
            j, lsum, esum, dqt = carry
            return (j + 2,) + sweep([(j, False, None), (j + 1, False, j + 1 < qi - 1)],
                                    lsum, esum, dqt)

        _, lsum, esum, dqt = lax.while_loop(lambda c: c[0] < qi - 1, pair, (low, row, row, dqt))
        _, _, dqt = sweep([(jnp.maximum(qi - 1, 0), False, qi >= 1), (qi, True, None)],
                          lsum, esum, dqt)
        top = _row_iota((LANES, blk)) < HEAD_DIM
        dq_ref[...] = jnp.where(top, dqt[0], dqt[1]).T

    seq_blk = pl.BlockSpec((seq, LANES), lambda b, h, i: (b, h))
    tok = pl.BlockSpec((blk, LANES), lambda b, h, i: (b * nq + i, h))
    tok_shape = jax.ShapeDtypeStruct((nb * seq, D_BRANCH), F32)
    return pl.pallas_call(
        body, name="sb_attn_bwd",
        grid=(nb, N_PAIRS, nq),
        in_specs=[tok, seq_blk,
                  pl.BlockSpec((1, 1, LANES, seq), lambda b, h, i: (b, h, 0, 0)),
                  seq_blk,
                  pl.BlockSpec((1, 2, 1, 8, blk), lambda b, h, i: (b, h, i, 0, 0)),
                  pl.BlockSpec(memory_space=pltpu.SMEM),
                  tok],
        out_specs=[tok, seq_blk, seq_blk],
        out_shape=[tok_shape, tok_shape, tok_shape],
        compiler_params=_params(3),
    )(qs, kn, kt, vb, tot, low, d_o)


def _qk_bwd(proj, dqs, dkn, dvh, qw2, kw2, nb, seq, tq):
    nl = seq // tq
    scale = 1.0 / math.sqrt(HEAD_DIM)

    def body(q_ref, k_ref, dq_ref, dk_ref, dv_ref, qw_ref, kw_ref,
             dqr_ref, dkr_ref, dvr_ref, gq_ref, gk_ref):
        @pl.when((pl.program_id(0) == 0) & (pl.program_id(1) == 0))
        def _():
            gq_ref[...] = jnp.zeros_like(gq_ref)
            gk_ref[...] = jnp.zeros_like(gk_ref)

        def norm_bwd(x, w, dy):
            r = lax.rsqrt(_pair_sum(x * x) * (1.0 / HEAD_DIM) + EPS)
            xhat = x * r
            g = dy * w
            m = _pair_sum(g * xhat) * (1.0 / HEAD_DIM)
            return r * (g - xhat * m), jnp.sum(dy * xhat, axis=0, keepdims=True)

        dvr_ref[...] = dv_ref[...].astype(BF16)
        gq = jnp.zeros((1, LANES), F32)
        gk = jnp.zeros((1, LANES), F32)
        for p in range(N_PAIRS):
            cols = slice(p * LANES, (p + 1) * LANES)
            dqr, gq_p = norm_bwd(q_ref[:, cols], qw_ref[...], dq_ref[:, cols] * scale)
            dkr, gk_p = norm_bwd(k_ref[:, cols], kw_ref[...], dk_ref[:, cols])
            dqr_ref[:, cols] = dqr.astype(BF16)
            dkr_ref[:, cols] = dkr.astype(BF16)
            gq, gk = gq + gq_p, gk + gk_p
        gq_ref[...] += gq
        gk_ref[...] += gk

    tok = lambda blk: pl.BlockSpec((tq, D_BRANCH), lambda b, i: (b * nl + i, blk))
    vec = pl.BlockSpec((1, LANES), lambda b, i: (0, 0))
    tshape = jax.ShapeDtypeStruct((nb * seq, D_BRANCH), BF16)
    return pl.pallas_call(
        body, name="qk_bwd",
        grid=(nb, nl),
        in_specs=[tok(0), tok(1), tok(0), tok(0), tok(0), vec, vec],
        out_specs=[tok(0), tok(0), tok(0), vec, vec],
        out_shape=[tshape, tshape, tshape,
                   jax.ShapeDtypeStruct((1, LANES), F32), jax.ShapeDtypeStruct((1, LANES), F32)],
        compiler_params=_params(2),
    )(proj, proj, dqs, dkn, dvh, qw2, kw2)


def _shift_down(cur, prev, k):
    if k == 0:
        return cur
    rows = _row_iota(cur.shape)
    return jnp.where(rows < k, pltpu.roll(prev, k, axis=0), pltpu.roll(cur, k, axis=0))


def _shift_up(cur, nxt, k):
    if k == 0:
        return cur
    n = cur.shape[0]
    rows = _row_iota(cur.shape)
    return jnp.where(rows < n - k, pltpu.roll(cur, n - k, axis=0), pltpu.roll(nxt, n - k, axis=0))


def _conv_pre(cur, prev, w, b):
    out = b
    for i in range(CONV_TAPS):
        out = out + _shift_down(cur, prev, CONV_TAPS - 1 - i) * w[i:i + 1, :]
    return out


def _silu(x):
    return x * _sigmoid(x)


def _silu_grad(x):
    s = _sigmoid(x)
    return s * (1.0 + x * (1.0 - s))


def _chunk_decay(dt_raw, dtb, alog, expand, qc):
    dt = _softplus(dt_raw + dtb)
    d_a = dt * (-jnp.exp(alog))
    r_i = lax.broadcasted_iota(jnp.int32, (qc, qc), 0)
    c_i = lax.broadcasted_iota(jnp.int32, (qc, qc), 1)
    tril = r_i >= c_i
    a_cs = _dot(jnp.where(tril, 1.0, 0.0).astype(F32), d_a, precision=HIGHEST)
    dt_x = _dot(dt, expand, precision=HIGHEST)
    acs_x = _dot(a_cs, expand, precision=HIGHEST)
    return dt, d_a, a_cs, dt_x, acs_x, tril


def _ssd_fwd(proj, conv_w, conv_b, dtb, alog, dskip, nb, seq):
    qc = SSD_CHUNK
    nc = seq // qc

    def body(xs_ref, bc_ref, dt_ref, cw_ref, cb_ref, dtb_ref, al_ref, ds_ref,
             y_ref, st_ref, pxs_ref, pbc_ref, state_ref):
        @pl.when(pl.program_id(1) == 0)
        def _():
            pxs_ref[...] = jnp.zeros_like(pxs_ref)
            pbc_ref[...] = jnp.zeros_like(pbc_ref)
            state_ref[...] = jnp.zeros_like(state_ref)

        expand = _head_expand()
        xs_raw = xs_ref[...]
        bc_raw = bc_ref[...]
        cw = cw_ref[...]
        cb = cb_ref[...]
        xs = _silu(_conv_pre(xs_raw, pxs_ref[...], cw[:, :D_BRANCH], cb[:, :D_BRANCH]))
        bc = _silu(_conv_pre(bc_raw, pbc_ref[...], cw[:, D_BRANCH:], cb[:, D_BRANCH:]))
        pxs_ref[...] = xs_raw
        pbc_ref[...] = bc_raw

        dt, d_a, a_cs, dt_x, acs_x, tril = _chunk_decay(
            dt_ref[...], dtb_ref[...], al_ref[...], expand, qc)
        a_cst = a_cs.T
        aend_x = acs_x[qc - 1:qc, :]
        ea_x = jnp.exp(acs_x)
        dec_x = jnp.exp(aend_x - acs_x)
        xt = xs * dt_x
        xtb = xt.astype(BF16)
        xdb = (xt * dec_x).astype(BF16)
        d_x = _dot(jnp.broadcast_to(ds_ref[...], (8, LANES)), expand, precision=HIGHEST)[0:1, :]
        st_ref[0, 0] = state_ref[...]

        for g in range(N_GROUPS):
            gs = slice(g * GROUP_W, (g + 1) * GROUP_W)
            bg = bc[:, g * D_STATE:(g + 1) * D_STATE]
            cg = bc[:, (N_GROUPS + g) * D_STATE:(N_GROUPS + g + 1) * D_STATE]
            bgb = bg.astype(BF16)
            cgb = cg.astype(BF16)
            cbm = _dot(cgb, bgb, _NT)
            st_in = state_ref[g]
            y_off = _dot(cgb, st_in.astype(BF16)) * ea_x[:, gs]
            for k in range(HEADS_PER_GROUP):
                h = g * HEADS_PER_GROUP + k
                hs = slice(h * HEAD_DIM, (h + 1) * HEAD_DIM)
                seg = a_cs[:, h:h + 1] - a_cst[h:h + 1, :]
                gh = cbm * jnp.exp(jnp.where(tril, seg, -1e30))
                y_h = _dot(gh.astype(BF16), xtb[:, hs]) + y_off[:, k * HEAD_DIM:(k + 1) * HEAD_DIM]
                y_ref[:, hs] = y_h + d_x[:, hs] * xs[:, hs]
            state_ref[g] = st_in * jnp.exp(aend_x[:, gs]) + _dot(bg.T.astype(BF16), xdb[:, gs])

    nblk = lambda w, off: pl.BlockSpec((qc, w), lambda b, c: (b * nc + c, off))
    full = lambda r, w: pl.BlockSpec((r, w), lambda b, c: (0, 0))
    return pl.pallas_call(
        body, name="ssd_fwd",
        grid=(nb, nc),
        in_specs=[nblk(D_BRANCH, COL_XS // D_BRANCH), nblk(D_BC, COL_BC // D_BC),
                  nblk(LANES, COL_DT // LANES),
                  full(CONV_TAPS, D_CONV), full(1, D_CONV), full(1, LANES), full(1, LANES),
                  full(1, LANES)],
        out_specs=[pl.BlockSpec((qc, D_BRANCH), lambda b, c: (b * nc + c, 0)),
                   pl.BlockSpec((1, 1, N_GROUPS, D_STATE, GROUP_W), lambda b, c: (b, c, 0, 0, 0))],
        out_shape=[jax.ShapeDtypeStruct((nb * seq, D_BRANCH), F32),
                   jax.ShapeDtypeStruct((nb, nc, N_GROUPS, D_STATE, GROUP_W), F32)],
        scratch_shapes=[pltpu.VMEM((qc, D_BRANCH), F32), pltpu.VMEM((qc, D_BC), F32),
                        pltpu.VMEM((N_GROUPS, D_STATE, GROUP_W), F32)],
        compiler_params=_params(2),
    )(proj, proj, proj, conv_w, conv_b, dtb, alog, dskip)


def _ssd_bwd(proj, d_y, states, conv_w, conv_b, dtb, alog, dskip, nb, seq):
    qc = SSD_CHUNK
    nc = seq // qc

    def body(xs_ref, bc_ref, dt_ref, pxs_ref, pbc_ref, dy_ref, st_ref, stn_ref,
             cw_ref, cb_ref, dtb_ref, al_ref, ds_ref,
             dx_ref, gcw_ref, gcb_ref, gdtb_ref, gal_ref, gds_ref,
             dst_ref, nxs_ref, nbc_ref, yd_ref, dxt_ref):
        step = pl.program_id(1)
        chunk = nc - 1 - step

        @pl.when(step == 0)
        def _():
            dst_ref[...] = jnp.zeros_like(dst_ref)
            nxs_ref[...] = jnp.zeros_like(nxs_ref)
            nbc_ref[...] = jnp.zeros_like(nbc_ref)

        @pl.when((pl.program_id(0) == 0) & (step == 0))
        def _():
            gcw_ref[...] = jnp.zeros_like(gcw_ref)
            gcb_ref[...] = jnp.zeros_like(gcb_ref)
            gdtb_ref[...] = jnp.zeros_like(gdtb_ref)
            gal_ref[...] = jnp.zeros_like(gal_ref)
            gds_ref[...] = jnp.zeros_like(gds_ref)

        expand = _head_expand()
        collapse = lambda v: _dot(v, expand, _NT, precision=HIGHEST)
        first = jnp.where(chunk == 0, 0.0, 1.0)
        xs_raw = xs_ref[...]
        bc_raw = bc_ref[...]
        pxs = pxs_ref[...] * first
        pbc = pbc_ref[...] * first
        cw = cw_ref[...]
        cb = cb_ref[...]
        pre_xs = _conv_pre(xs_raw, pxs, cw[:, :D_BRANCH], cb[:, :D_BRANCH])
        pre_bc = _conv_pre(bc_raw, pbc, cw[:, D_BRANCH:], cb[:, D_BRANCH:])
        xs = _silu(pre_xs)
        bc = _silu(pre_bc)

        dt_in = dt_ref[...] + dtb_ref[...]
        dt, d_a, a_cs, dt_x, acs_x, tril = _chunk_decay(
            dt_ref[...], dtb_ref[...], al_ref[...], expand, qc)
        a_cst = a_cs.T
        aend_x = acs_x[qc - 1:qc, :]
        ea_x = jnp.exp(acs_x)
        dec_x = jnp.exp(aend_x - acs_x)
        xt = xs * dt_x
        xtb = xt.astype(BF16)
        xdb = (xt * dec_x).astype(BF16)
        d_x = _dot(jnp.broadcast_to(ds_ref[...], (8, LANES)), expand, precision=HIGHEST)[0:1, :]

        dy = dy_ref[...]
        dyb = dy.astype(BF16)
        dyeab = (dy * ea_x).astype(BF16)
        gds_ref[...] += collapse(jnp.broadcast_to(jnp.sum(dy * xs, axis=0, keepdims=True),
                                                  (8, D_BRANCH)))[0:1, :]

        d_bc = []
        d_cc = []
        y_offs = []
        dxt_states = []
        end_terms = []
        for g in range(N_GROUPS):
            gs = slice(g * GROUP_W, (g + 1) * GROUP_W)
            bg = bc[:, g * D_STATE:(g + 1) * D_STATE]
            cg = bc[:, (N_GROUPS + g) * D_STATE:(N_GROUPS + g + 1) * D_STATE]
            bgb = bg.astype(BF16)
            cgb = cg.astype(BF16)
            cbm = _dot(cgb, bgb, _NT)
            st_in = st_ref[0, 0, g]
            st_inb = st_in.astype(BF16)
            d_st = dst_ref[g]
            d_stb = d_st.astype(BF16)
            y_offs.append(_dot(cgb, st_inb) * ea_x[:, gs])
            dxt_states.append(_dot(bgb, d_stb) * dec_x[:, gs])
            d_c = _dot(dyeab[:, gs], st_inb, _NT)
            d_b = _dot(xdb[:, gs], d_stb, _NT)
            d_cb = jnp.zeros((qc, qc), F32)
            for k in range(HEADS_PER_GROUP):
                h = g * HEADS_PER_GROUP + k
                hs = slice(h * HEAD_DIM, (h + 1) * HEAD_DIM)
                seg = a_cs[:, h:h + 1] - a_cst[h:h + 1, :]
                lh = jnp.exp(jnp.where(tril, seg, -1e30))
                ghb = (cbm * lh).astype(BF16)
                d_cb = d_cb + _dot(dyb[:, hs], xtb[:, hs], _NT) * lh
                yd_ref[:, hs] = _dot(ghb, xtb[:, hs])
                dxt_ref[:, hs] = _dot(ghb, dyb[:, hs], _TN)
            d_cbb = d_cb.astype(BF16)
            d_cc.append(d_c + _dot(d_cbb, bgb))
            d_bc.append(d_b + _dot(d_cbb, cgb, _TN))
            end_terms.append(jnp.sum(d_st * stn_ref[0, 0, g], axis=0, keepdims=True))
            dst_ref[g] = d_st * jnp.exp(aend_x[:, gs]) + _dot(cg.T.astype(BF16), dyeab[:, gs])

        y_off = jnp.concatenate(y_offs, axis=1)
        dxt_state = jnp.concatenate(dxt_states, axis=1)
        dxt = dxt_ref[...] + dxt_state
        last = jnp.where(chunk == nc - 1, 0.0, 1.0)
        end_c = collapse(jnp.broadcast_to(jnp.concatenate(end_terms, axis=1), (8, D_BRANCH)))[0:1, :]
        da_cs = collapse(dyb.astype(F32) * yd_ref[...] - dxt_ref[...] * xtb.astype(F32)
                         + dy * y_off - dxt_state * xt)
        da_cs = da_cs + jnp.where(_row_iota(da_cs.shape) == qc - 1, end_c * last, 0.0)
        triu = lax.broadcasted_iota(jnp.int32, (qc, qc), 0) <= lax.broadcasted_iota(jnp.int32, (qc, qc), 1)
        dd_a = _dot(jnp.where(triu, 1.0, 0.0).astype(F32), da_cs, precision=HIGHEST)
        ddt = dd_a * (-jnp.exp(al_ref[...])) + collapse(dxt * xs)
        head_lanes = _lane_iota(ddt.shape) < N_HEADS
        ddt_raw = jnp.where(head_lanes, ddt * _sigmoid(dt_in), 0.0)
        gal_ref[...] += jnp.sum(jnp.where(head_lanes, dd_a * d_a, 0.0), axis=0, keepdims=True)
        gdtb_ref[...] += jnp.sum(ddt_raw, axis=0, keepdims=True)

        dpre_xs = (dxt * dt_x + d_x * dy) * _silu_grad(pre_xs)
        dpre_bc = jnp.concatenate(d_bc + d_cc, axis=1) * _silu_grad(pre_bc)
        gcb_ref[...] += jnp.concatenate([jnp.sum(dpre_xs, axis=0, keepdims=True),
                                         jnp.sum(dpre_bc, axis=0, keepdims=True)], axis=1)
        nxs = nxs_ref[...]
        nbc = nbc_ref[...]
        du_xs = jnp.zeros_like(dpre_xs)
        du_bc = jnp.zeros_like(dpre_bc)
        for i in range(CONV_TAPS):
            k = CONV_TAPS - 1 - i
            gcw_ref[i:i + 1, :] += jnp.concatenate(
                [jnp.sum(dpre_xs * _shift_down(xs_raw, pxs, k), axis=0, keepdims=True),
                 jnp.sum(dpre_bc * _shift_down(bc_raw, pbc, k), axis=0, keepdims=True)], axis=1)
            du_xs = du_xs + _shift_up(dpre_xs, nxs, k) * cw[i:i + 1, :D_BRANCH]
            du_bc = du_bc + _shift_up(dpre_bc, nbc, k) * cw[i:i + 1, D_BRANCH:]
        nxs_ref[...] = dpre_xs
        nbc_ref[...] = dpre_bc

        dx_ref[:, :D_BRANCH] = du_xs.astype(BF16)
        dx_ref[:, D_BRANCH:D_CONV] = du_bc.astype(BF16)
        dx_ref[:, D_CONV:D_CONV + LANES] = ddt_raw.astype(BF16)
        dx_ref[:, D_CONV + LANES:] = jnp.zeros((qc, 2048 - D_CONV - LANES), BF16)

    rev = lambda b, c: b * nc + (nc - 1 - c)
    prv = lambda b, c: b * nc + jnp.maximum(nc - 2 - c, 0)
    nblk = lambda w, off, f: pl.BlockSpec((qc, w), lambda b, c: (f(b, c), off))
    full = lambda r, w: pl.BlockSpec((r, w), lambda b, c: (0, 0))
    st_spec = lambda f: pl.BlockSpec((1, 1, N_GROUPS, D_STATE, GROUP_W),
                                     lambda b, c: (b, f(c), 0, 0, 0))
    return pl.pallas_call(
        body, name="ssd_bwd",
        grid=(nb, nc),
        in_specs=[nblk(D_BRANCH, COL_XS // D_BRANCH, rev), nblk(D_BC, COL_BC // D_BC, rev),
                  nblk(LANES, COL_DT // LANES, rev),
                  nblk(D_BRANCH, COL_XS // D_BRANCH, prv), nblk(D_BC, COL_BC // D_BC, prv),
                  nblk(D_BRANCH, 0, rev),
                  st_spec(lambda c: nc - 1 - c), st_spec(lambda c: jnp.minimum(nc - c, nc - 1)),
                  full(CONV_TAPS, D_CONV), full(1, D_CONV), full(1, LANES), full(1, LANES),
                  full(1, LANES)],
        out_specs=[nblk(2048, 0, rev), full(8, D_CONV), full(1, D_CONV), full(1, LANES),
                   full(1, LANES), full(1, LANES)],
        out_shape=[jax.ShapeDtypeStruct((nb * seq, 2048), BF16),
                   jax.ShapeDtypeStruct((8, D_CONV), F32), jax.ShapeDtypeStruct((1, D_CONV), F32),
                   jax.ShapeDtypeStruct((1, LANES), F32), jax.ShapeDtypeStruct((1, LANES), F32),
                   jax.ShapeDtypeStruct((1, LANES), F32)],
        scratch_shapes=[pltpu.VMEM((N_GROUPS, D_STATE, GROUP_W), F32),
                        pltpu.VMEM((qc, D_BRANCH), F32), pltpu.VMEM((qc, D_BC), F32),
                        pltpu.VMEM((qc, D_BRANCH), F32), pltpu.VMEM((qc, D_BRANCH), F32)],
        compiler_params=_params(2),
    )(proj, proj, proj, proj, proj, d_y, states, states, conv_w, conv_b, dtb, alog, dskip)


def _mid(o_sb, y_ssd, proj, x2, target, sb_w, ssd_w, w_out_b, tm):
    t = x2.shape[0]
    inv_d = 1.0 / D_MODEL

    def body(o_ref, y_ref, zsb_ref, zssd_ref, x_ref, tg_ref, sbw_ref, ssdw_ref, w_ref,
             dout_ref, dosb_ref, dy_ref, dz_ref, gw_ref, gsb_ref, gssd_ref, loss_ref):
        @pl.when(pl.program_id(0) == 0)
        def _():
            gw_ref[...] = jnp.zeros_like(gw_ref)
            gsb_ref[...] = jnp.zeros_like(gsb_ref)
            gssd_ref[...] = jnp.zeros_like(gssd_ref)
            loss_ref[...] = jnp.zeros_like(loss_ref)

        def branch(val, z, w):
            gate = _silu(z)
            g = val * gate
            r = lax.rsqrt(jnp.mean(g * g, axis=1, keepdims=True) + EPS)
            xhat = g * r
            return gate, r, xhat, (xhat * w).astype(BF16)

        o = o_ref[...]
        y = y_ref[...]
        z_sb = zsb_ref[...]
        z_ssd = zssd_ref[...]
        gate_a, r_a, xhat_a, mix_a = branch(o, z_sb, sbw_ref[...])
        gate_b, r_b, xhat_b, mix_b = branch(y, z_ssd, ssdw_ref[...])
        out = x_ref[...] + _dot(mix_a, w_ref[:D_BRANCH, :]) + _dot(mix_b, w_ref[D_BRANCH:, :])
        diff = out - tg_ref[...]
        loss_ref[...] += 0.5 * inv_d * jnp.sum(diff * diff)
        d_out = diff * inv_d
        dout_ref[...] = d_out
        d_outb = d_out.astype(BF16)
        gw_ref[:D_BRANCH, :] += _dot(mix_a, d_outb, _TN)
        gw_ref[D_BRANCH:, :] += _dot(mix_b, d_outb, _TN)

        def branch_bwd(dmix, val, z, w, gate, r, xhat):
            gg = dmix * w
            m = jnp.mean(gg * xhat, axis=1, keepdims=True)
            dg = r * (gg - xhat * m)
            return dg * gate, dg * val * _silu_grad(z), jnp.sum(dmix * xhat, axis=0, keepdims=True)

        dmix_a = _dot(d_outb, w_ref[:D_BRANCH, :], _NT)
        dmix_b = _dot(d_outb, w_ref[D_BRANCH:, :], _NT)
        d_o, dz_a, gsb = branch_bwd(dmix_a, o, z_sb, sbw_ref[...], gate_a, r_a, xhat_a)
        d_y, dz_b, gssd = branch_bwd(dmix_b, y, z_ssd, ssdw_ref[...], gate_b, r_b, xhat_b)
        dosb_ref[...] = d_o
        dy_ref[...] = d_y
        dz_ref[:, :D_BRANCH] = dz_a.astype(BF16)
        dz_ref[:, D_BRANCH:] = dz_b.astype(BF16)
        gsb_ref[...] += gsb
        gssd_ref[...] += gssd

    row = lambda w, off: pl.BlockSpec((tm, w), lambda i: (i, off))
    full = lambda r, w: pl.BlockSpec((r, w), lambda i: (0, 0))
    resident = pl.BlockSpec((2 * D_BRANCH, D_MODEL), lambda i: (0, 0), pipeline_mode=pl.Buffered(1))
    tok = jax.ShapeDtypeStruct((t, D_MODEL), F32)
    return pl.pallas_call(
        body, name="mid",
        grid=(t // tm,),
        in_specs=[row(D_BRANCH, 0), row(D_BRANCH, 0), row(D_BRANCH, 3), row(D_BRANCH, 4),
                  row(D_MODEL, 0), row(D_MODEL, 0), full(1, D_BRANCH), full(1, D_BRANCH),
                  resident],
        out_specs=[row(D_MODEL, 0), row(D_BRANCH, 0), row(D_BRANCH, 0), row(2 * D_BRANCH, 0),
                   resident, full(1, D_BRANCH), full(1, D_BRANCH),
                   full(1, LANES)],
        out_shape=[tok, tok, tok, jax.ShapeDtypeStruct((t, 2 * D_BRANCH), BF16),
                   jax.ShapeDtypeStruct((2 * D_BRANCH, D_MODEL), F32),
                   jax.ShapeDtypeStruct((1, D_BRANCH), F32), jax.ShapeDtypeStruct((1, D_BRANCH), F32),
                   jax.ShapeDtypeStruct((1, LANES), F32)],
        compiler_params=_params(1),
    )(o_sb, y_ssd, proj, proj, x2, target, sb_w, ssd_w, w_out_b)


_DPROJ_FIRST = (0, 1, 2, 3, 5)
_DPROJ_BLOCKS = (1, 1, 1, 2, 2)
_DPROJ_OWNER = (0, 1, 2, 3, 3, 4, 4)


def _dproj_col(j, p):
    return jnp.clip(j - _DPROJ_FIRST[p], 0, _DPROJ_BLOCKS[p] - 1)


def _in_proj_bwd_x(d_parts, w_in_b, x2, d_out, norm_w, tm, slabs=()):
    t = x2.shape[0]
    n_parts = len(d_parts)
    n_slabs = len(slabs)
    n_rows = t // tm

    def body(*refs):
        dp_refs = refs[:n_parts]
        w_ref, x_ref, dout_ref, nw_ref = refs[n_parts:n_parts + 4]
        rest = refs[n_parts + 4:]
        src_refs, (gx_ref, gnw_ref) = rest[:n_slabs], rest[n_slabs:n_slabs + 2]
        dst_refs = rest[n_slabs + 2:2 * n_slabs + 2]
        acc_ref = rest[2 * n_slabs + 2]
        sems = rest[2 * n_slabs + 3:]
        i, j = pl.program_id(0), pl.program_id(1)

        if n_slabs:
            @pl.when((i == 0) & (j == 0))
            def _():
                _exchange_start(_exchange_copies(src_refs, dst_refs, (True,) * n_slabs, *sems))

            @pl.when((i == n_rows - 1) & (j == N_COLBLK - 1))
            def _():
                _exchange_wait(_exchange_copies(src_refs, dst_refs, (True,) * n_slabs, *sems))

        @pl.when((i == 0) & (j == 0))
        def _():
            gnw_ref[...] = jnp.zeros_like(gnw_ref)

        @pl.when(j == 0)
        def _():
            acc_ref[...] = jnp.zeros_like(acc_ref)

        for jj in range(N_COLBLK):
            @pl.when(j == jj)
            def _(jj=jj):
                acc_ref[...] += _dot(dp_refs[_DPROJ_OWNER[jj]][...], w_ref[...], _NT)

        @pl.when(j == N_COLBLK - 1)
        def _():
            xf = x_ref[...]
            d_hn = acc_ref[...]
            r = lax.rsqrt(jnp.mean(xf * xf, axis=1, keepdims=True) + EPS)
            xhat = xf * r
            g = d_hn * nw_ref[...]
            m = jnp.mean(g * xhat, axis=1, keepdims=True)
            gx_ref[...] = dout_ref[...] + r * (g - xhat * m)
            gnw_ref[...] += jnp.sum(d_hn * xhat, axis=0, keepdims=True)

    return pl.pallas_call(
        body, name="in_proj_bwd_x",
        grid=(t // tm, N_COLBLK),
        in_specs=[pl.BlockSpec((tm, 1024), lambda i, j, p=p: (i, _dproj_col(j, p)))
                  for p in range(n_parts)] + [
                  pl.BlockSpec((D_MODEL, 1024), lambda i, j: (0, j)),
                  pl.BlockSpec((tm, D_MODEL), lambda i, j: (i, 0)),
                  pl.BlockSpec((tm, D_MODEL), lambda i, j: (i, 0)),
                  pl.BlockSpec((1, D_MODEL), lambda i, j: (0, 0))] + [_ANY] * n_slabs,
        out_specs=[pl.BlockSpec((tm, D_MODEL), lambda i, j: (i, 0)),
                   pl.BlockSpec((1, D_MODEL), lambda i, j: (0, 0))] + [_ANY] * n_slabs,
        out_shape=[jax.ShapeDtypeStruct((t, D_MODEL), F32), jax.ShapeDtypeStruct((1, D_MODEL), F32)]
                  + _exchange_shapes(slabs, (True,) * n_slabs),
        scratch_shapes=[pltpu.VMEM((tm, D_MODEL), F32)] + (_exchange_sems(n_slabs) if n_slabs else []),
        compiler_params=_params(2),
    )(*d_parts, w_in_b, x2, d_out, norm_w, *slabs)


def _in_proj_bwd_w(hn, d_parts, tm):
    t = hn.shape[0]
    n_parts = len(d_parts)

    def body(hn_ref, *refs):
        dp_refs, gw_ref = refs[:n_parts], refs[n_parts]
        j = pl.program_id(0)

        @pl.when(pl.program_id(1) == 0)
        def _():
            gw_ref[...] = jnp.zeros_like(gw_ref)

        for jj in range(N_COLBLK):
            @pl.when(j == jj)
            def _(jj=jj):
                gw_ref[...] += _dot(hn_ref[...], dp_refs[_DPROJ_OWNER[jj]][...], _TN)

    def part_spec(p):
        def index(j, i):
            mine = (j >= _DPROJ_FIRST[p]) & (j < _DPROJ_FIRST[p] + _DPROJ_BLOCKS[p])
            return jnp.where(mine, i, 0), _dproj_col(j, p)
        return pl.BlockSpec((tm, 1024), index)

    return pl.pallas_call(
        body, name="in_proj_bwd_w",
        grid=(N_COLBLK, t // tm),
        in_specs=[pl.BlockSpec((tm, D_MODEL), lambda j, i: (i, 0))]
                 + [part_spec(p) for p in range(n_parts)],
        out_specs=pl.BlockSpec((D_MODEL, 1024), lambda j, i: (0, j)),
        out_shape=jax.ShapeDtypeStruct((D_MODEL, D_IN_PAD), F32),
        compiler_params=_params(2),
    )(hn, *d_parts)


def _adamw(parts, w, m, v, tr, name):
    _, rows, cols = w.shape
    c1 = 1.0 - ADAM_B1 ** ADAM_STEP
    c2 = 1.0 - ADAM_B2 ** ADAM_STEP

    def body(p_ref, w_ref, m_ref, v_ref, g_ref, d_ref, nm_ref, nv_ref):
        g = p_ref[0].astype(F32)
        for s in range(1, N_DEV):
            g = g + p_ref[s].astype(F32)
        nm = ADAM_B1 * m_ref[0] + (1.0 - ADAM_B1) * g
        nv = ADAM_B2 * v_ref[0] + (1.0 - ADAM_B2) * (g * g)
        g_ref[0] = g
        nm_ref[0] = nm
        nv_ref[0] = nv
        d_ref[0] = -ADAM_LR * ((nm / c1) / (jnp.sqrt(nv / c2) + ADAM_EPS) + ADAM_WD * w_ref[0])

    blk = pl.BlockSpec((1, tr, cols), lambda i: (0, i, 0))
    shape = jax.ShapeDtypeStruct((1, rows, cols), F32)
    return pl.pallas_call(
        body, name=name,
        grid=(rows // tr,),
        in_specs=[pl.BlockSpec((N_DEV, tr, cols), lambda i: (0, i, 0)), blk, blk, blk],
        out_specs=[blk, blk, blk, blk],
        out_shape=[shape, shape, shape, shape],
        compiler_params=_params(1),
    )(parts, w, m, v)


def _mesh_place():
    x, y, c = lax.axis_index("x"), lax.axis_index("y"), lax.axis_index("c")
    return x, y, c, 4 * x + 2 * y + c


def _peer(x, y, c, k):
    px = 1 - x if k & 4 else x
    py = 1 - y if k & 2 else y
    pc = 1 - c if k & 1 else c
    return (px, py, pc), 4 * px + 2 * py + pc


def _exchange(srcs, scatter, name):
    n = len(srcs)

    def body(*refs):
        copies = _exchange_copies(refs[:n], refs[n:2 * n], scatter, *refs[2 * n:])
        _exchange_start(copies)
        _exchange_wait(copies)

    return pl.pallas_call(
        body, name=name,
        in_specs=[_ANY] * n, out_specs=[_ANY] * n, out_shape=_exchange_shapes(srcs, scatter),
        scratch_shapes=_exchange_sems(n),
    )(*srcs)


def _gather_two_level(shard, name):
    def body(x_ref, out_ref, send_sems, recv_sems, local_sem):
        x, y, c, me = _mesh_place()
        sibling = (x, y, 1 - c)
        chips = [(1 - x, y), (x, 1 - y), (1 - x, 1 - y)]

        def slab(px, py, pc):
            return out_ref.at[4 * px + 2 * py + pc]

        def copy(k, block, to, src=None):
            return pltpu.make_async_remote_copy(
                src_ref=slab(*block) if src is None else src, dst_ref=slab(*block),
                send_sem=send_sems.at[k], recv_sem=recv_sems.at[k],
                device_id=to, device_id_type=pl.DeviceIdType.MESH)

        mine = pltpu.make_async_copy(x_ref, slab(x, y, c), local_sem)
        mine.start()
        first = [copy(0, (x, y, c), sibling, src=x_ref)]
        first += [copy(1 + j, (x, y, c), (*chip, c), src=x_ref) for j, chip in enumerate(chips)]
        for cp in first:
            cp.start()
        passed = [copy(4 + j, (*chip, c), sibling) for j, chip in enumerate(chips)]
        for j, chip in enumerate(chips):
            copy(1 + j, (*chip, c), (x, y, c)).wait_recv()
            passed[j].start()
        copy(0, sibling, (x, y, c)).wait_recv()
        for j, chip in enumerate(chips):
            copy(4 + j, (*chip, 1 - c), (x, y, c)).wait_recv()
        for cp in first + passed:
            cp.wait_send()
        mine.wait()

    return pl.pallas_call(
        body, name=name,
        in_specs=[_ANY], out_specs=_ANY,
        out_shape=jax.ShapeDtypeStruct((N_DEV,) + shard.shape, shard.dtype),
        scratch_shapes=[pltpu.SemaphoreType.DMA((N_DEV - 1,)), pltpu.SemaphoreType.DMA((N_DEV - 1,)),
                        pltpu.SemaphoreType.DMA],
    )(shard)


_ANY = pl.BlockSpec(memory_space=pl.ANY)


def _exchange_shapes(srcs, scatter):
    return [jax.ShapeDtypeStruct(s.shape if sc else (N_DEV,) + s.shape, s.dtype)
            for s, sc in zip(srcs, scatter)]


def _exchange_sems(n):
    return [pltpu.SemaphoreType.DMA((n * (N_DEV - 1),)),
            pltpu.SemaphoreType.DMA((n * (N_DEV - 1),)),
            pltpu.SemaphoreType.DMA((n,))]


def _exchange_copies(src_refs, dst_refs, scatter, send_sems, recv_sems, loc_sems):
    n = len(src_refs)
    x, y, c, me = _mesh_place()

    def src_of(i, idx):
        return src_refs[i].at[idx] if scatter[i] else src_refs[i]

    local = [pltpu.make_async_copy(src_of(i, me), dst_refs[i].at[me], loc_sems.at[i])
             for i in range(n)]
    sends, recvs = [], []
    for k in range(1, N_DEV):
        peer, pidx = _peer(x, y, c, k)
        for i in range(n):
            s = i * (N_DEV - 1) + k - 1
            for dst_slab, group in ((me, sends), (pidx, recvs)):
                group.append(pltpu.make_async_remote_copy(
                    src_ref=src_of(i, pidx), dst_ref=dst_refs[i].at[dst_slab],
                    send_sem=send_sems.at[s], recv_sem=recv_sems.at[s],
                    device_id=peer, device_id_type=pl.DeviceIdType.MESH))
    return local, sends, recvs


def _exchange_start(copies):
    local, sends, _ = copies
    for cp in local + sends:
        cp.start()


def _exchange_wait(copies):
    local, sends, recvs = copies
    for cp in recvs:
        cp.wait_recv()
    for cp in sends:
        cp.wait_send()
    for cp in local:
        cp.wait()


def _pad_lanes(v, width=LANES):
    return jnp.pad(v, ((0, 0), (0, width - v.shape[1])))


def _local_step(x, target, norm_w, w_in_b, q_norm_w, k_norm_w, conv_w, conv_b, dt_bias, a_log,
                d_skip, sb_norm_w, ssd_norm_w, w_out_b, tm=512, tq=512, tmid=256, blk=ATT_BLK,
                scatter=False):
    nb, seq, _ = x.shape
    t = nb * seq
    x2 = x.reshape(t, D_MODEL)
    tg2 = target.reshape(t, D_MODEL)
    qw2 = jnp.tile(q_norm_w, (1, 2))
    kw2 = jnp.tile(k_norm_w, (1, 2))
    dtb, alog, dsk = _pad_lanes(dt_bias), _pad_lanes(a_log), _pad_lanes(d_skip)

    tproj = min(2 * tm, t)
    if scatter:
        proj, hn, wout_all, cw_all = _in_proj(x2, norm_w, w_in_b, tproj, (w_out_b, conv_w))
        w_out_b = wout_all.reshape(2 * D_BRANCH, D_MODEL)
        conv_w = jnp.transpose(cw_all, (1, 0, 2)).reshape(CONV_TAPS, D_CONV)
    else:
        proj, hn = _in_proj(x2, norm_w, w_in_b, tproj)
    qs, kn, vb, kt = _qk_prep(proj, qw2, kw2, nb, seq, tq)
    o_sb, sb_tot, sb_low = _attn_fwd(qs, kn, vb, nb, seq, blk)
    y_ssd, states = _ssd_fwd(proj, conv_w, conv_b, dtb, alog, dsk, nb, seq)
    d_out, d_osb, d_y, d_z, g_wout, g_sbw, g_ssdw, loss = _mid(
        o_sb, y_ssd, proj, x2, tg2, sb_norm_w, ssd_norm_w, w_out_b, tmid)
    dqs, dkn, dvh = _attn_bwd(qs, kn, kt, vb, sb_tot, sb_low, d_osb, nb, seq, blk)
    dq_raw, dk_raw, dv_raw, g_qw, g_kw = _qk_bwd(proj, dqs, dkn, dvh, qw2, kw2, nb, seq, tq)
    d_xbc, g_cw, g_cb, g_dtb, g_alog, g_dsk = _ssd_bwd(
        proj, d_y, states, conv_w, conv_b, dtb, alog, dsk, nb, seq)
    d_parts = [dq_raw, dk_raw, dv_raw, d_z, d_xbc]
    g_win = _in_proj_bwd_w(hn, d_parts, tm)[:, :D_IN]
    g_cw = g_cw[:CONV_TAPS]
    if scatter:
        grad_x, g_nw, g_win, g_wout, g_cw = _in_proj_bwd_x(
            d_parts, w_in_b, x2, d_out, norm_w, tm, _grad_slabs(g_win, g_wout, g_cw))
    else:
        grad_x, g_nw = _in_proj_bwd_x(d_parts, w_in_b, x2, d_out, norm_w, tm)

    small = dict(
        norm_w=g_nw,
        q_norm_w=g_qw[:, :HEAD_DIM] + g_qw[:, HEAD_DIM:],
        k_norm_w=g_kw[:, :HEAD_DIM] + g_kw[:, HEAD_DIM:],
        conv_b=g_cb, dt_bias=g_dtb[:, :N_HEADS], A_log=g_alog[:, :N_HEADS],
        D_skip=g_dsk[:, :N_HEADS], sb_norm_w=g_sbw, ssd_norm_w=g_ssdw)
    return loss[0, 0], grad_x.reshape(nb, seq, D_MODEL), g_win, g_wout, g_cw, small


def _grad_slabs(g_win, g_wout, g_cw):
    w_sh = D_IN // N_DEV
    c_sh = D_CONV // N_DEV
    return (jnp.transpose(g_win.reshape(D_MODEL, N_DEV, w_sh), (1, 0, 2)).astype(BF16),
            g_wout.reshape(N_DEV, 2 * D_BRANCH // N_DEV, D_MODEL).astype(BF16),
            jnp.pad(jnp.transpose(g_cw.reshape(CONV_TAPS, N_DEV, c_sh), (1, 0, 2)),
                    ((0, 0), (0, 8 - CONV_TAPS), (0, 0))))


_SMALL = ("norm_w", "q_norm_w", "k_norm_w", "conv_b", "dt_bias", "A_log", "D_skip",
          "sb_norm_w", "ssd_norm_w")


def _pack_small(vals):
    rows = [_pad_lanes(vals[n], -(-vals[n].shape[1] // LANES) * LANES).reshape(-1, LANES)
            for n in _SMALL]
    packed = jnp.concatenate(rows, axis=0)
    return jnp.pad(packed, ((0, 48 - packed.shape[0]), (0, 0)))


def _unpack_small(packed, like):
    out, r = {}, 0
    for n in _SMALL:
        width = like[n].shape[1]
        nr = -(-width // LANES)
        out[n] = packed[r:r + nr].reshape(1, nr * LANES)[:, :width]
        r += nr
    return out


def kernel(x, norm_w, w_in, q_norm_w, k_norm_w, conv_w, conv_b, dt_bias, A_log, D_skip, sb_norm_w, ssd_norm_w, w_out, loss_target, m_norm_w, m_w_in, m_q_norm_w, m_k_norm_w, m_conv_w, m_conv_b, m_dt_bias, m_A_log, m_D_skip, m_sb_norm_w, m_ssd_norm_w, m_w_out, v_norm_w, v_w_in, v_q_norm_w, v_k_norm_w, v_conv_w, v_conv_b, v_dt_bias, v_A_log, v_D_skip, v_sb_norm_w, v_ssd_norm_w, v_w_out):
    win_all = _gather_two_level(w_in[0].astype(BF16), "gather_w_in")
    w_in_b = jnp.pad(jnp.transpose(win_all, (1, 0, 2)).reshape(D_MODEL, D_IN),
                     ((0, 0), (0, D_IN_PAD - D_IN)))

    loss, grad_x, win_parts, wout_parts, cw_parts, g_small = _local_step(
        x, loss_target, norm_w, w_in_b, q_norm_w, k_norm_w, conv_w[0], conv_b, dt_bias, A_log,
        D_skip, sb_norm_w, ssd_norm_w, w_out[0].astype(BF16), scatter=True)
    small_parts, = _exchange([_pack_small(g_small)], [False], "gather_small_grads")

    small_w = dict(norm_w=norm_w, q_norm_w=q_norm_w, k_norm_w=k_norm_w, conv_b=conv_b,
                   dt_bias=dt_bias, A_log=A_log, D_skip=D_skip, sb_norm_w=sb_norm_w,
                   ssd_norm_w=ssd_norm_w)
    small_m = dict(norm_w=m_norm_w, q_norm_w=m_q_norm_w, k_norm_w=m_k_norm_w, conv_b=m_conv_b,
                   dt_bias=m_dt_bias, A_log=m_A_log, D_skip=m_D_skip, sb_norm_w=m_sb_norm_w,
                   ssd_norm_w=m_ssd_norm_w)
    small_v = dict(norm_w=v_norm_w, q_norm_w=v_q_norm_w, k_norm_w=v_k_norm_w, conv_b=v_conv_b,
                   dt_bias=v_dt_bias, A_log=v_A_log, D_skip=v_D_skip, sb_norm_w=v_sb_norm_w,
                   ssd_norm_w=v_ssd_norm_w)

    pad8 = lambda a: jnp.pad(a, ((0, 0), (0, 8 - CONV_TAPS), (0, 0)))
    r_win = _adamw(win_parts, w_in, m_w_in, v_w_in, 128, "adamw_w_in")
    r_wout = _adamw(wout_parts, w_out, m_w_out, v_w_out, 128, "adamw_w_out")
    r_cw = _adamw(cw_parts, pad8(conv_w), pad8(m_conv_w), pad8(v_conv_w), 8, "adamw_conv_w")
    r_small = _adamw(small_parts, _pack_small(small_w)[None], _pack_small(small_m)[None],
                     _pack_small(small_v)[None], 48, "adamw_small")

    loss = lax.psum(loss, ("x", "y", "c"))
    res = {"w_in": r_win, "w_out": r_wout, "conv_w": [a[:, :CONV_TAPS] for a in r_cw]}
    unpacked = [_unpack_small(a[0], small_w) for a in r_small]
    for n in _SMALL:
        res[n] = [u[n] for u in unpacked]
    order = ("norm_w", "w_in", "q_norm_w", "k_norm_w", "conv_w", "conv_b", "dt_bias", "A_log",
             "D_skip", "sb_norm_w", "ssd_norm_w", "w_out")
    outs = [loss, grad_x]
    for kind in range(4):
        outs += [res[n][kind] for n in order]
    return tuple(outs)
```

```python
import functools
import math

import jax
import jax.numpy as jnp
from jax import lax
from jax.experimental import pallas as pl
from jax.experimental.pallas import tpu as pltpu

F32 = jnp.float32
BF16 = jnp.bfloat16

D_MODEL = 1024
N_HEADS = 16
HEAD_DIM = 64
N_PAIRS = N_HEADS // 2
D_BRANCH = 1024
N_GROUPS = 2
HEADS_PER_GROUP = 8
D_STATE = 128
GROUP_W = HEADS_PER_GROUP * HEAD_DIM
D_BC = 2 * N_GROUPS * D_STATE
D_CONV = D_BRANCH + D_BC
D_IN = 6672
D_IN_PAD = 7168
N_COLBLK = D_IN_PAD // 1024
COL_XS = 5120
COL_BC = 6144
COL_DT = 6656
EPS = 1e-6
CONV_TAPS = 4
N_DEV = 8

LANES = 128
SSD_CHUNK = 128
ATT_BLK = 256
EXP_UNDERFLOW = -105.0
VMEM_LIMIT = 56 * 1024 * 1024

ADAM_LR = 0.001
ADAM_B1 = 0.9
ADAM_B2 = 0.999
ADAM_EPS = 1e-08
ADAM_WD = 0.01
ADAM_STEP = 10

_NT = (((1,), (1,)), ((), ()))
_TN = (((0,), (0,)), ((), ()))


def _params(n_grid):
    return pltpu.CompilerParams(dimension_semantics=("arbitrary",) * n_grid,
                                vmem_limit_bytes=VMEM_LIMIT)


def _dot(a, b, dims=None, precision=None):
    if dims is None:
        return jnp.dot(a, b, preferred_element_type=F32, precision=precision)
    return lax.dot_general(a, b, dims, preferred_element_type=F32, precision=precision)


def _sigmoid(x):
    return 1.0 / (1.0 + jnp.exp(-x))


def _softplus(x):
    return jnp.maximum(x, 0.0) + jnp.log(1.0 + jnp.exp(-jnp.abs(x)))


def _split_bf16(x):
    hi = x.astype(BF16)
    lo = (x - hi.astype(F32)).astype(BF16)
    return hi, lo


def _lane_iota(shape):
    return lax.broadcasted_iota(jnp.int32, shape, len(shape) - 1)


def _row_iota(shape):
    return lax.broadcasted_iota(jnp.int32, shape, len(shape) - 2)


def _pair_sum(x):
    r = lax.broadcasted_iota(jnp.int32, (LANES, LANES), 0)
    c = lax.broadcasted_iota(jnp.int32, (LANES, LANES), 1)
    same_head = jnp.where(r // HEAD_DIM == c // HEAD_DIM, 1.0, 0.0).astype(BF16)
    hi, lo = _split_bf16(x)
    return _dot(hi, same_head) + _dot(lo, same_head)


def _pair_head(x, a):
    lane = _lane_iota(x.shape)
    mine = (lane < HEAD_DIM) if a == 0 else (lane >= HEAD_DIM)
    return jnp.where(mine, x, jnp.zeros_like(x))


def _head_expand():
    r = lax.broadcasted_iota(jnp.int32, (LANES, D_BRANCH), 0)
    c = lax.broadcasted_iota(jnp.int32, (LANES, D_BRANCH), 1)
    return jnp.where(c // HEAD_DIM == r, 1.0, 0.0).astype(BF16)


def _in_proj(x2, norm_w, w_in_b, tm, shards=()):
    t = x2.shape[0]
    n_sh = len(shards)
    n_rows = t // tm

    def body(x_ref, nw_ref, w_ref, *rest):
        src_refs, (proj_ref, hn_ref) = rest[:n_sh], rest[n_sh:n_sh + 2]
        dst_refs, sems = rest[n_sh + 2:2 * n_sh + 2], rest[2 * n_sh + 2:]
        i, j = pl.program_id(0), pl.program_id(1)

        if n_sh:
            @pl.when((i == 0) & (j == 0))
            def _():
                _exchange_start(_exchange_copies(src_refs, dst_refs, (False,) * n_sh, *sems))

            @pl.when((i == n_rows - 1) & (j == N_COLBLK - 1))
            def _():
                _exchange_wait(_exchange_copies(src_refs, dst_refs, (False,) * n_sh, *sems))

        @pl.when(j == 0)
        def _():
            xf = x_ref[...]
            r = lax.rsqrt(jnp.mean(xf * xf, axis=1, keepdims=True) + EPS)
            hn_ref[...] = (xf * r * nw_ref[...]).astype(BF16)

        proj_ref[...] = _dot(hn_ref[...], w_ref[...])

    return pl.pallas_call(
        body, name="in_proj",
        grid=(n_rows, N_COLBLK),
        in_specs=[pl.BlockSpec((tm, D_MODEL), lambda i, j: (i, 0)),
                  pl.BlockSpec((1, D_MODEL), lambda i, j: (0, 0)),
                  pl.BlockSpec((D_MODEL, 1024), lambda i, j: (0, j))] + [_ANY] * n_sh,
        out_specs=[pl.BlockSpec((tm, 1024), lambda i, j: (i, j)),
                   pl.BlockSpec((tm, D_MODEL), lambda i, j: (i, 0))] + [_ANY] * n_sh,
        out_shape=[jax.ShapeDtypeStruct((t, D_IN_PAD), F32),
                   jax.ShapeDtypeStruct((t, D_MODEL), BF16)]
                  + _exchange_shapes(shards, (False,) * n_sh),
        scratch_shapes=_exchange_sems(n_sh) if n_sh else [],
        compiler_params=_params(2),
    )(x2, norm_w, w_in_b, *shards)


def _qk_prep(proj, qw2, kw2, nb, seq, tq):
    nl = seq // tq
    scale = 1.0 / math.sqrt(HEAD_DIM)

    def body(q_ref, k_ref, v_ref, qw_ref, kw_ref, qs_ref, kn_ref, vb_ref, kt_ref):
        def norm(x, w):
            r = lax.rsqrt(_pair_sum(x * x) * (1.0 / HEAD_DIM) + EPS)
            return x * r * w

        vb_ref[...] = v_ref[...].astype(BF16)
        for p in range(N_PAIRS):
            cols = slice(p * LANES, (p + 1) * LANES)
            kn = norm(k_ref[:, cols], kw_ref[...])
            qs_ref[:, cols] = (norm(q_ref[:, cols], qw_ref[...]) * scale).astype(BF16)
            kn_ref[:, cols] = kn.astype(BF16)
            kt_ref[0, p] = kn.T.astype(BF16)

    tok_shape = jax.ShapeDtypeStruct((nb * seq, D_BRANCH), BF16)
    tok = lambda blk: pl.BlockSpec((tq, D_BRANCH), lambda b, i: (b * nl + i, blk))
    vec = pl.BlockSpec((1, LANES), lambda b, i: (0, 0))
    return pl.pallas_call(
        body, name="qk_prep",
        grid=(nb, nl),
        in_specs=[tok(0), tok(1), tok(2), vec, vec],
        out_specs=[tok(0), tok(0), tok(0),
                   pl.BlockSpec((1, N_PAIRS, LANES, tq), lambda b, i: (b, 0, 0, i))],
        out_shape=[tok_shape, tok_shape, tok_shape,
                   jax.ShapeDtypeStruct((nb, N_PAIRS, LANES, seq), BF16)],
        compiler_params=_params(2),
    )(proj, proj, proj, qw2, kw2)


def _attn_fwd(qs, kn, vb, nb, seq, blk):
    nq = seq // blk

    def body(q_ref, k_ref, v_ref, o_ref, tot_ref, low_ref, kmax_ref):
        qi = pl.program_id(2)
        r_i = lax.broadcasted_iota(jnp.int32, (blk, blk), 0)
        c_i = lax.broadcasted_iota(jnp.int32, (blk, blk), 1)
        csum = jnp.where(r_i >= c_i, 1.0, 0.0).astype(BF16)
        causal = c_i < r_i
        heads = range(2)

        head = _pair_head

        @pl.when(qi == 0)
        def _():
            kk = k_ref[...].astype(F32)
            for a in heads:
                ksq = jnp.sum(head(kk * kk, a), axis=1, keepdims=True)
                kmax_ref[a] = jnp.full((8, LANES), jnp.max(ksq))

        q_pair = q_ref[...]
        qf = q_pair.astype(F32)
        q_head = [head(q_pair, a) for a in heads]
        zmax = []
        for a in heads:
            qsq = jnp.sum(head(qf * qf, a), axis=1, keepdims=True)
            zmax.append(1.01 * jnp.sqrt(qsq * kmax_ref[a][0:1, 0:1]) + 0.01)

        def exhausted(run):
            top = jnp.maximum(jnp.max(run[0] + zmax[0]), jnp.max(run[1] + zmax[1]))
            return top < EXP_UNDERFLOW

        def sweep(blocks, run, acc):
            offs = [pl.multiple_of(j * blk, blk) for j, _, _ in blocks]
            z = [[_dot(q_head[a], k_ref[pl.ds(off, blk), :], _NT) for a in heads]
                 for off in offs]
            cl = []
            for (_, diag, valid), zb in zip(blocks, z):
                lkb = []
                for a in heads:
                    lk = -_softplus(zb[a])
                    if diag:
                        lk = jnp.where(causal, lk, 0.0)
                    if valid is not None:
                        lk = jnp.where(valid, lk, 0.0)
                    lkb.append(lk.astype(BF16))
                cl.append([_dot(lkb[a], csum) for a in heads])
            for (_, diag, valid), zb, clb, off in zip(blocks, z, cl, offs):
                w = []
                for a in heads:
                    wa = jnp.exp(zb[a] + clb[a] + run[a])
                    if diag:
                        wa = jnp.where(causal, wa, 0.0)
                    if valid is not None:
                        wa = jnp.where(valid, wa, 0.0)
                    w.append(wa.astype(BF16))
                run = [run[a] + clb[a][:, 0:1] for a in heads]
                v_pair = v_ref[pl.ds(off, blk), :]
                acc = acc + _dot(w[0], head(v_pair, 0)) + _dot(w[1], head(v_pair, 1))
            return run, acc

        run = [jnp.zeros((blk, 1), F32)] * 2
        acc = jnp.zeros((blk, LANES), F32)
        run, acc = sweep([(qi, True, None), (jnp.maximum(qi - 1, 0), False, qi >= 1)], run, acc)
        low = jnp.maximum(qi - 1, 0)

        def more(carry):
            low, done, _, _ = carry
            return (low > 0) & jnp.logical_not(done)

        def pair(carry):
            low, _, run, acc = carry
            run, acc = sweep([(low - 1, False, None), (jnp.maximum(low - 2, 0), False, low >= 2)],
                             run, acc)
            return jnp.maximum(low - 2, 0), exhausted(run), run, acc

        low, _, run, acc = lax.while_loop(more, pair, (low, exhausted(run), run, acc))
        low_ref[pl.program_id(0) * N_PAIRS + pl.program_id(1), qi] = low.astype(F32)
        o_ref[...] = acc
        for a in heads:
            as_row = jnp.sum(jnp.where(r_i == c_i, run[a], 0.0), axis=0, keepdims=True)
            tot_ref[0, a, 0] = jnp.broadcast_to(as_row, (8, blk))

    return pl.pallas_call(
        body, name="sb_attn_fwd",
        grid=(nb, N_PAIRS, nq),
        in_specs=[pl.BlockSpec((blk, LANES), lambda b, h, i: (b * nq + i, h)),
                  pl.BlockSpec((seq, LANES), lambda b, h, i: (b, h)),
                  pl.BlockSpec((seq, LANES), lambda b, h, i: (b, h))],
        out_specs=[pl.BlockSpec((blk, LANES), lambda b, h, i: (b * nq + i, h)),
                   pl.BlockSpec((1, 2, 1, 8, blk), lambda b, h, i: (b, h, i, 0, 0)),
                   pl.BlockSpec(memory_space=pltpu.SMEM)],
        out_shape=[jax.ShapeDtypeStruct((nb * seq, D_BRANCH), F32),
                   jax.ShapeDtypeStruct((nb, N_HEADS, nq, 8, blk), F32),
                   jax.ShapeDtypeStruct((nb * N_PAIRS, nq), F32)],
        scratch_shapes=[pltpu.VMEM((2, 8, LANES), F32)],
        compiler_params=_params(3),
    )(qs, kn, vb)


def _attn_bwd(qs, kn, kt, vb, tot, low, d_o, nb, seq, blk):
    nq = seq // blk

    def body(q_ref, k_ref, kt_ref, v_ref, tot_ref, low_ref, do_ref, dq_ref, dk_ref, dv_ref):
        qi = pl.program_id(2)

        @pl.when(qi == 0)
        def _():
            dk_ref[...] = jnp.zeros_like(dk_ref)
            dv_ref[...] = jnp.zeros_like(dv_ref)

        r_i = lax.broadcasted_iota(jnp.int32, (blk, blk), 0)
        c_i = lax.broadcasted_iota(jnp.int32, (blk, blk), 1)
        before = jnp.where(c_i < r_i, 1.0, 0.0).astype(BF16)
        upto = jnp.where(c_i <= r_i, 1.0, 0.0).astype(BF16)
        causal = r_i < c_i

        heads = range(2)
        q_head = [_pair_head(q_ref[...], a) for a in heads]
        d_ob = [_pair_head(do_ref[...].astype(BF16), a) for a in heads]
        total = [tot_ref[0, a, 0][0:1, :] for a in heads]

        def sweep(blocks, lsum, esum, dqt):
            def keep(x, diag, valid):
                if diag:
                    x = jnp.where(causal, x, 0.0)
                if valid is not None:
                    x = jnp.where(valid, x, 0.0)
                return x

            offs = [pl.multiple_of(j * blk, blk) for j, _, _ in blocks]
            zt = [[_dot(k_ref[pl.ds(off, blk), :], q_head[a], _NT) for a in heads]
                  for off in offs]
            dwt = [[_dot(v_ref[pl.ds(off, blk), :], d_ob[a], _NT) for a in heads]
                   for off in offs]
            sp, lk, lpre = [], [], []
            for (_, diag, valid), ztb in zip(blocks, zt):
                sp.append([_softplus(ztb[a]) for a in heads])
                lk.append([keep(-sp[-1][a], diag, valid).astype(BF16) for a in heads])
                lpre.append([_dot(before, lk[-1][a]) for a in heads])
            wt, et, epre = [], [], []
            for i, (_, diag, valid) in enumerate(blocks):
                wt.append([keep(jnp.exp(zt[i][a] + (total[a] - lsum[a] - lpre[i][a])), diag, valid)
                           for a in heads])
                et.append([dwt[i][a] * wt[i][a] for a in heads])
                split = [_split_bf16(et[i][a]) for a in heads]
                epre.append([_dot(upto, split[a][0]) + _dot(upto, split[a][1]) for a in heads])
                lsum = [lsum[a] + lpre[i][a][blk - 1:blk, :] + lk[i][a][blk - 1:blk, :]
                        for a in heads]
            for i, (_, diag, valid) in enumerate(blocks):
                dzb = [keep(et[i][a] - jnp.exp(zt[i][a] - sp[i][a]) * (esum[a] + epre[i][a]),
                            diag, valid).astype(BF16) for a in heads]
                esum = [esum[a] + epre[i][a][blk - 1:blk, :] for a in heads]
                dk_ref[pl.ds(offs[i], blk), :] += (_dot(dzb[0], q_head[0]) + _dot(dzb[1], q_head[1]))
                dv_ref[pl.ds(offs[i], blk), :] += (_dot(wt[i][0].astype(BF16), d_ob[0])
                                                   + _dot(wt[i][1].astype(BF16), d_ob[1]))
                kt_pair = kt_ref[0, 0, :, pl.ds(offs[i], blk)]
                dqt = [dqt[a] + _dot(kt_pair, dzb[a]) for a in heads]
            return lsum, esum, dqt

        row = [jnp.zeros((1, blk), F32)] * 2
        dqt = [jnp.zeros((LANES, blk), F32)] * 2
        low = low_ref[pl.program_id(0) * N_PAIRS + pl.program_id(1), qi].astype(jnp.int32)
        low = jnp.clip(low, 0, jnp.maximum(qi - 1, 0))

        def pair(carry):
            j, lsum, esum, dqt = carry
            return (j + 2,) + sweep([(j, False, None), (j + 1, False, j + 1 < qi - 1)],
                                    lsum, esum, dqt)

        _, lsum, esum, dqt = lax.while_loop(lambda c: c[0] < qi - 1, pair, (low, row, row, dqt))
        _, _, dqt = sweep([(jnp.maximum(qi - 1, 0), False, qi >= 1), (qi, True, None)],
                          lsum, esum, dqt)
        top = _row_iota((LANES, blk)) < HEAD_DIM
        dq_ref[...] = jnp.where(top, dqt[0], dqt[1]).T

    seq_blk = pl.BlockSpec((seq, LANES), lambda b, h, i: (b, h))
    tok = pl.BlockSpec((blk, LANES), lambda b, h, i: (b * nq + i, h))
    tok_shape = jax.ShapeDtypeStruct((nb * seq, D_BRANCH), F32)
    return pl.pallas_call(
        body, name="sb_attn_bwd",
        grid=(nb, N_PAIRS, nq),
        in_specs=[tok, seq_blk,
                  pl.BlockSpec((1, 1, LANES, seq), lambda b, h, i: (b, h, 0, 0)),
                  seq_blk,
                  pl.BlockSpec((1, 2, 1, 8, blk), lambda b, h, i: (b, h, i, 0, 0)),
                  pl.BlockSpec(memory_space=pltpu.SMEM),
                  tok],
        out_specs=[tok, seq_blk, seq_blk],
        out_shape=[tok_shape, tok_shape, tok_shape],
        compiler_params=_params(3),
    )(qs, kn, kt, vb, tot, low, d_o)


def _qk_bwd(proj, dqs, dkn, dvh, qw2, kw2, nb, seq, tq):
    nl = seq // tq
    scale = 1.0 / math.sqrt(HEAD_DIM)

    def body(q_ref, k_ref, dq_ref, dk_ref, dv_ref, qw_ref, kw_ref,
             dqr_ref, dkr_ref, dvr_ref, gq_ref, gk_ref):
        @pl.when((pl.program_id(0) == 0) & (pl.program_id(1) == 0))
        def _():
            gq_ref[...] = jnp.zeros_like(gq_ref)
            gk_ref[...] = jnp.zeros_like(gk_ref)

        def norm_bwd(x, w, dy):
            r = lax.rsqrt(_pair_sum(x * x) * (1.0 / HEAD_DIM) + EPS)
            xhat = x * r
            g = dy * w
            m = _pair_sum(g * xhat) * (1.0 / HEAD_DIM)
            return r * (g - xhat * m), jnp.sum(dy * xhat, axis=0, keepdims=True)

        dvr_ref[...] = dv_ref[...].astype(BF16)
        gq = jnp.zeros((1, LANES), F32)
        gk = jnp.zeros((1, LANES), F32)
        for p in range(N_PAIRS):
            cols = slice(p * LANES, (p + 1) * LANES)
            dqr, gq_p = norm_bwd(q_ref[:, cols], qw_ref[...], dq_ref[:, cols] * scale)
            dkr, gk_p = norm_bwd(k_ref[:, cols], kw_ref[...], dk_ref[:, cols])
            dqr_ref[:, cols] = dqr.astype(BF16)
            dkr_ref[:, cols] = dkr.astype(BF16)
            gq, gk = gq + gq_p, gk + gk_p
        gq_ref[...] += gq
        gk_ref[...] += gk

    tok = lambda blk: pl.BlockSpec((tq, D_BRANCH), lambda b, i: (b * nl + i, blk))
    vec = pl.BlockSpec((1, LANES), lambda b, i: (0, 0))
    tshape = jax.ShapeDtypeStruct((nb * seq, D_BRANCH), BF16)
    return pl.pallas_call(
        body, name="qk_bwd",
        grid=(nb, nl),
        in_specs=[tok(0), tok(1), tok(0), tok(0), tok(0), vec, vec],
        out_specs=[tok(0), tok(0), tok(0), vec, vec],
        out_shape=[tshape, tshape, tshape,
                   jax.ShapeDtypeStruct((1, LANES), F32), jax.ShapeDtypeStruct((1, LANES), F32)],
        compiler_params=_params(2),
    )(proj, proj, dqs, dkn, dvh, qw2, kw2)


def _shift_down(cur, prev, k):
    if k == 0:
        return cur
    rows = _row_iota(cur.shape)
    return jnp.where(rows < k, pltpu.roll(prev, k, axis=0), pltpu.roll(cur, k, axis=0))


def _shift_up(cur, nxt, k):
    if k == 0:
        return cur
    n = cur.shape[0]
    rows = _row_iota(cur.shape)
    return jnp.where(rows < n - k, pltpu.roll(cur, n - k, axis=0), pltpu.roll(nxt, n - k, axis=0))


def _conv_taps(cur, prev):
    return [_shift_down(cur, prev, CONV_TAPS - 1 - i) for i in range(CONV_TAPS)]


def _conv_pre(taps, w, b):
    out = b
    for i in range(CONV_TAPS):
        out = out + taps[i] * w[i:i + 1, :]
    return out


def _silu(x):
    return x * _sigmoid(x)


def _silu_and_grad(x):
    s = _sigmoid(x)
    return x * s, s * (1.0 + x * (1.0 - s))


def _dot01(x, m01, parts, dims=None, m_left=False):
    total, rest = None, x
    for i in range(parts):
        piece = rest.astype(BF16)
        if i + 1 < parts:
            rest = rest - piece.astype(F32)
        term = _dot(m01, piece, dims) if m_left else _dot(piece, m01, dims)
        total = term if total is None else total + term
    return total


def _chunk_decay(dt_raw, dtb, alog, expand, qc):
    dt = _softplus(dt_raw + dtb)
    d_a = dt * (-jnp.exp(alog))
    r_i = lax.broadcasted_iota(jnp.int32, (qc, qc), 0)
    c_i = lax.broadcasted_iota(jnp.int32, (qc, qc), 1)
    tril = r_i >= c_i
    a_cs = _dot01(d_a, jnp.where(tril, 1.0, 0.0).astype(BF16), 3, m_left=True)
    dt_x = _dot01(dt, expand, 3)
    acs_x = _dot01(a_cs, expand, 3)
    return dt, d_a, a_cs, dt_x, acs_x, tril


def _ssd_fwd(proj, conv_w, conv_b, dtb, alog, dskip, nb, seq):
    qc = SSD_CHUNK
    nc = seq // qc

    def body(xs_ref, bc_ref, dt_ref, cw_ref, cb_ref, dtb_ref, al_ref, ds_ref,
             y_ref, st_ref, pxs_ref, pbc_ref, state_ref):
        @pl.when(pl.program_id(1) == 0)
        def _():
            pxs_ref[...] = jnp.zeros_like(pxs_ref)
            pbc_ref[...] = jnp.zeros_like(pbc_ref)
            state_ref[...] = jnp.zeros_like(state_ref)

        expand = _head_expand()
        xs_raw = xs_ref[...]
        bc_raw = bc_ref[...]
        cw = cw_ref[...]
        cb = cb_ref[...]
        xs = _silu(_conv_pre(_conv_taps(xs_raw, pxs_ref[...]), cw[:, :D_BRANCH], cb[:, :D_BRANCH]))
        bc = _silu(_conv_pre(_conv_taps(bc_raw, pbc_ref[...]), cw[:, D_BRANCH:], cb[:, D_BRANCH:]))
        pxs_ref[...] = xs_raw
        pbc_ref[...] = bc_raw

        dt, d_a, a_cs, dt_x, acs_x, tril = _chunk_decay(
            dt_ref[...], dtb_ref[...], al_ref[...], expand, qc)
        a_cst = a_cs.T
        aend_x = acs_x[qc - 1:qc, :]
        ea_x = jnp.exp(acs_x)
        dec_x = jnp.exp(aend_x - acs_x)
        xt = xs * dt_x
        xtb = xt.astype(BF16)
        xdb = (xt * dec_x).astype(BF16)
        d_x = _dot01(jnp.broadcast_to(ds_ref[...], (8, LANES)), expand, 3)[0:1, :]
        st_ref[0, 0] = state_ref[...]

        for g in range(N_GROUPS):
            gs = slice(g * GROUP_W, (g + 1) * GROUP_W)
            bg = bc[:, g * D_STATE:(g + 1) * D_STATE]
            cg = bc[:, (N_GROUPS + g) * D_STATE:(N_GROUPS + g + 1) * D_STATE]
            bgb = bg.astype(BF16)
            cgb = cg.astype(BF16)
            cbm = _dot(cgb, bgb, _NT)
            st_in = state_ref[g]
            y_off = _dot(cgb, st_in.astype(BF16)) * ea_x[:, gs]
            for k in range(HEADS_PER_GROUP):
                h = g * HEADS_PER_GROUP + k
                hs = slice(h * HEAD_DIM, (h + 1) * HEAD_DIM)
                seg = a_cs[:, h:h + 1] - a_cst[h:h + 1, :]
                gh = cbm * jnp.exp(jnp.where(tril, seg, -1e30))
                y_h = _dot(gh.astype(BF16), xtb[:, hs]) + y_off[:, k * HEAD_DIM:(k + 1) * HEAD_DIM]
                y_ref[:, hs] = y_h + d_x[:, hs] * xs[:, hs]
            state_ref[g] = st_in * jnp.exp(aend_x[:, gs]) + _dot(bg.T.astype(BF16), xdb[:, gs])

    nblk = lambda w, off: pl.BlockSpec((qc, w), lambda b, c: (b * nc + c, off))
    full = lambda r, w: pl.BlockSpec((r, w), lambda b, c: (0, 0))
    return pl.pallas_call(
        body, name="ssd_fwd",
        grid=(nb, nc),
        in_specs=[nblk(D_BRANCH, COL_XS // D_BRANCH), nblk(D_BC, COL_BC // D_BC),
                  nblk(LANES, COL_DT // LANES),
                  full(CONV_TAPS, D_CONV), full(1, D_CONV), full(1, LANES), full(1, LANES),
                  full(1, LANES)],
        out_specs=[pl.BlockSpec((qc, D_BRANCH), lambda b, c: (b * nc + c, 0)),
                   pl.BlockSpec((1, 1, N_GROUPS, D_STATE, GROUP_W), lambda b, c: (b, c, 0, 0, 0))],
        out_shape=[jax.ShapeDtypeStruct((nb * seq, D_BRANCH), F32),
                   jax.ShapeDtypeStruct((nb, nc, N_GROUPS, D_STATE, GROUP_W), F32)],
        scratch_shapes=[pltpu.VMEM((qc, D_BRANCH), F32), pltpu.VMEM((qc, D_BC), F32),
                        pltpu.VMEM((N_GROUPS, D_STATE, GROUP_W), F32)],
        compiler_params=_params(2),
    )(proj, proj, proj, conv_w, conv_b, dtb, alog, dskip)


def _ssd_bwd(proj, d_y, states, conv_w, conv_b, dtb, alog, dskip, nb, seq):
    qc = SSD_CHUNK
    nc = seq // qc

    def body(xs_ref, bc_ref, dt_ref, pxs_ref, pbc_ref, dy_ref, st_ref, stn_ref,
             cw_ref, cb_ref, dtb_ref, al_ref, ds_ref,
             dx_ref, gcw_ref, gcb_ref, gdtb_ref, gal_ref, gds_ref,
             dst_ref, nxs_ref, nbc_ref, yd_ref, dxt_ref):
        step = pl.program_id(1)
        chunk = nc - 1 - step

        @pl.when(step == 0)
        def _():
            dst_ref[...] = jnp.zeros_like(dst_ref)
            nxs_ref[...] = jnp.zeros_like(nxs_ref)
            nbc_ref[...] = jnp.zeros_like(nbc_ref)

        @pl.when((pl.program_id(0) == 0) & (step == 0))
        def _():
            gcw_ref[...] = jnp.zeros_like(gcw_ref)
            gcb_ref[...] = jnp.zeros_like(gcb_ref)
            gdtb_ref[...] = jnp.zeros_like(gdtb_ref)
            gal_ref[...] = jnp.zeros_like(gal_ref)
            gds_ref[...] = jnp.zeros_like(gds_ref)

        expand = _head_expand()
        collapse = lambda v: _dot01(v, expand, 2, _NT)
        first = jnp.where(chunk == 0, 0.0, 1.0)
        xs_raw = xs_ref[...]
        bc_raw = bc_ref[...]
        pxs = pxs_ref[...] * first
        pbc = pbc_ref[...] * first
        cw = cw_ref[...]
        cb = cb_ref[...]
        taps_xs = _conv_taps(xs_raw, pxs)
        taps_bc = _conv_taps(bc_raw, pbc)
        xs, dsilu_xs = _silu_and_grad(_conv_pre(taps_xs, cw[:, :D_BRANCH], cb[:, :D_BRANCH]))
        bc, dsilu_bc = _silu_and_grad(_conv_pre(taps_bc, cw[:, D_BRANCH:], cb[:, D_BRANCH:]))

        dt_in = dt_ref[...] + dtb_ref[...]
        dt, d_a, a_cs, dt_x, acs_x, tril = _chunk_decay(
            dt_ref[...], dtb_ref[...], al_ref[...], expand, qc)
        a_cst = a_cs.T
        aend_x = acs_x[qc - 1:qc, :]
        ea_x = jnp.exp(acs_x)
        dec_x = jnp.exp(aend_x - acs_x)
        xt = xs * dt_x
        xtb = xt.astype(BF16)
        xdb = (xt * dec_x).astype(BF16)
        d_x = _dot01(jnp.broadcast_to(ds_ref[...], (8, LANES)), expand, 3)[0:1, :]

        dy = dy_ref[...]
        dyb = dy.astype(BF16)
        dyeab = (dy * ea_x).astype(BF16)
        gds_ref[...] += collapse(jnp.broadcast_to(jnp.sum(dy * xs, axis=0, keepdims=True),
                                                  (8, D_BRANCH)))[0:1, :]

        d_bc = []
        d_cc = []
        y_offs = []
        dxt_states = []
        end_terms = []
        for g in range(N_GROUPS):
            gs = slice(g * GROUP_W, (g + 1) * GROUP_W)
            bg = bc[:, g * D_STATE:(g + 1) * D_STATE]
            cg = bc[:, (N_GROUPS + g) * D_STATE:(N_GROUPS + g + 1) * D_STATE]
            bgb = bg.astype(BF16)
            cgb = cg.astype(BF16)
            cbm = _dot(cgb, bgb, _NT)
            st_in = st_ref[0, 0, g]
            st_inb = st_in.astype(BF16)
            d_st = dst_ref[g]
            d_stb = d_st.astype(BF16)
            y_offs.append(_dot(cgb, st_inb) * ea_x[:, gs])
            dxt_states.append(_dot(bgb, d_stb) * dec_x[:, gs])
            d_c = _dot(dyeab[:, gs], st_inb, _NT)
            d_b = _dot(xdb[:, gs], d_stb, _NT)
            d_cb = jnp.zeros((qc, qc), F32)
            for k in range(HEADS_PER_GROUP):
                h = g * HEADS_PER_GROUP + k
                hs = slice(h * HEAD_DIM, (h + 1) * HEAD_DIM)
                seg = a_cs[:, h:h + 1] - a_cst[h:h + 1, :]
                lh = jnp.exp(jnp.where(tril, seg, -1e30))
                ghb = (cbm * lh).astype(BF16)
                d_cb = d_cb + _dot(dyb[:, hs], xtb[:, hs], _NT) * lh
                yd_ref[:, hs] = _dot(ghb, xtb[:, hs])
                dxt_ref[:, hs] = _dot(ghb, dyb[:, hs], _TN)
            d_cbb = d_cb.astype(BF16)
            d_cc.append(d_c + _dot(d_cbb, bgb))
            d_bc.append(d_b + _dot(d_cbb, cgb, _TN))
            end_terms.append(jnp.sum(d_st * stn_ref[0, 0, g], axis=0, keepdims=True))
            dst_ref[g] = d_st * jnp.exp(aend_x[:, gs]) + _dot(cg.T.astype(BF16), dyeab[:, gs])

        y_off = jnp.concatenate(y_offs, axis=1)
        dxt_state = jnp.concatenate(dxt_states, axis=1)
        dxt = dxt_ref[...] + dxt_state
        last = jnp.where(chunk == nc - 1, 0.0, 1.0)
        end_c = collapse(jnp.broadcast_to(jnp.concatenate(end_terms, axis=1), (8, D_BRANCH)))[0:1, :]
        da_cs = collapse(dyb.astype(F32) * yd_ref[...] - dxt_ref[...] * xtb.astype(F32)
                         + dy * y_off - dxt_state * xt)
        da_cs = da_cs + jnp.where(_row_iota(da_cs.shape) == qc - 1, end_c * last, 0.0)
        triu = lax.broadcasted_iota(jnp.int32, (qc, qc), 0) <= lax.broadcasted_iota(jnp.int32, (qc, qc), 1)
        dd_a = _dot01(da_cs, jnp.where(triu, 1.0, 0.0).astype(BF16), 3, m_left=True)
        ddt = dd_a * (-jnp.exp(al_ref[...])) + collapse(dxt * xs)
        head_lanes = _lane_iota(ddt.shape) < N_HEADS
        ddt_raw = jnp.where(head_lanes, ddt * _sigmoid(dt_in), 0.0)
        gal_ref[...] += jnp.sum(jnp.where(head_lanes, dd_a * d_a, 0.0), axis=0, keepdims=True)
        gdtb_ref[...] += jnp.sum(ddt_raw, axis=0, keepdims=True)

        dpre_xs = (dxt * dt_x + d_x * dy) * dsilu_xs
        dpre_bc = jnp.concatenate(d_bc + d_cc, axis=1) * dsilu_bc
        gcb_ref[...] += jnp.concatenate([jnp.sum(dpre_xs, axis=0, keepdims=True),
                                         jnp.sum(dpre_bc, axis=0, keepdims=True)], axis=1)
        nxs = nxs_ref[...]
        nbc = nbc_ref[...]
        du_xs = jnp.zeros_like(dpre_xs)
        du_bc = jnp.zeros_like(dpre_bc)
        for i in range(CONV_TAPS):
            k = CONV_TAPS - 1 - i
            gcw_ref[i:i + 1, :] += jnp.concatenate(
                [jnp.sum(dpre_xs * taps_xs[i], axis=0, keepdims=True),
                 jnp.sum(dpre_bc * taps_bc[i], axis=0, keepdims=True)], axis=1)
            du_xs = du_xs + _shift_up(dpre_xs, nxs, k) * cw[i:i + 1, :D_BRANCH]
            du_bc = du_bc + _shift_up(dpre_bc, nbc, k) * cw[i:i + 1, D_BRANCH:]
        nxs_ref[...] = dpre_xs
        nbc_ref[...] = dpre_bc

        dx_ref[:, :D_BRANCH] = du_xs.astype(BF16)
        dx_ref[:, D_BRANCH:D_CONV] = du_bc.astype(BF16)
        dx_ref[:, D_CONV:D_CONV + LANES] = ddt_raw.astype(BF16)
        dx_ref[:, D_CONV + LANES:] = jnp.zeros((qc, 2048 - D_CONV - LANES), BF16)

    rev = lambda b, c: b * nc + (nc - 1 - c)
    prv = lambda b, c: b * nc + jnp.maximum(nc - 2 - c, 0)
    nblk = lambda w, off, f: pl.BlockSpec((qc, w), lambda b, c: (f(b, c), off))
    full = lambda r, w: pl.BlockSpec((r, w), lambda b, c: (0, 0))
    st_spec = lambda f: pl.BlockSpec((1, 1, N_GROUPS, D_STATE, GROUP_W),
                                     lambda b, c: (b, f(c), 0, 0, 0))
    return pl.pallas_call(
        body, name="ssd_bwd",
        grid=(nb, nc),
        in_specs=[nblk(D_BRANCH, COL_XS // D_BRANCH, rev), nblk(D_BC, COL_BC // D_BC, rev),
                  nblk(LANES, COL_DT // LANES, rev),
                  nblk(D_BRANCH, COL_XS // D_BRANCH, prv), nblk(D_BC, COL_BC // D_BC, prv),
                  nblk(D_BRANCH, 0, rev),
                  st_spec(lambda c: nc - 1 - c), st_spec(lambda c: jnp.minimum(nc - c, nc - 1)),
                  full(CONV_TAPS, D_CONV), full(1, D_CONV), full(1, LANES), full(1, LANES),
                  full(1, LANES)],
        out_specs=[nblk(2048, 0, rev), full(8, D_CONV), full(1, D_CONV), full(1, LANES),
                   full(1, LANES), full(1, LANES)],
        out_shape=[jax.ShapeDtypeStruct((nb * seq, 2048), BF16),
                   jax.ShapeDtypeStruct((8, D_CONV), F32), jax.ShapeDtypeStruct((1, D_CONV), F32),
                   jax.ShapeDtypeStruct((1, LANES), F32), jax.ShapeDtypeStruct((1, LANES), F32),
                   jax.ShapeDtypeStruct((1, LANES), F32)],
        scratch_shapes=[pltpu.VMEM((N_GROUPS, D_STATE, GROUP_W), F32),
                        pltpu.VMEM((qc, D_BRANCH), F32), pltpu.VMEM((qc, D_BC), F32),
                        pltpu.VMEM((qc, D_BRANCH), F32), pltpu.VMEM((qc, D_BRANCH), F32)],
        compiler_params=_params(2),
    )(proj, proj, proj, proj, proj, d_y, states, states, conv_w, conv_b, dtb, alog, dskip)


def _mid(o_sb, y_ssd, proj, x2, target, sb_w, ssd_w, w_out_b, tm):
    t = x2.shape[0]
    inv_d = 1.0 / D_MODEL

    def body(o_ref, y_ref, zsb_ref, zssd_ref, x_ref, tg_ref, sbw_ref, ssdw_ref, w_ref,
             dout_ref, dosb_ref, dy_ref, dz_ref, gw_ref, gsb_ref, gssd_ref, loss_ref):
        @pl.when(pl.program_id(0) == 0)
        def _():
            gw_ref[...] = jnp.zeros_like(gw_ref)
            gsb_ref[...] = jnp.zeros_like(gsb_ref)
            gssd_ref[...] = jnp.zeros_like(gssd_ref)
            loss_ref[...] = jnp.zeros_like(loss_ref)

        def branch(val, z, w):
            gate, dgate = _silu_and_grad(z)
            g = val * gate
            r = lax.rsqrt(jnp.mean(g * g, axis=1, keepdims=True) + EPS)
            xhat = g * r
            return (gate, dgate, r, xhat), (xhat * w).astype(BF16)

        o = o_ref[...]
        y = y_ref[...]
        saved_a, mix_a = branch(o, zsb_ref[...], sbw_ref[...])
        saved_b, mix_b = branch(y, zssd_ref[...], ssdw_ref[...])
        out = x_ref[...] + _dot(mix_a, w_ref[:D_BRANCH, :]) + _dot(mix_b, w_ref[D_BRANCH:, :])
        diff = out - tg_ref[...]
        loss_ref[...] += 0.5 * inv_d * jnp.sum(diff * diff)
        d_out = diff * inv_d
        dout_ref[...] = d_out
        d_outb = d_out.astype(BF16)
        gw_ref[:D_BRANCH, :] += _dot(mix_a, d_outb, _TN)
        gw_ref[D_BRANCH:, :] += _dot(mix_b, d_outb, _TN)

        def branch_bwd(dmix, val, w, saved):
            gate, dgate, r, xhat = saved
            gg = dmix * w
            m = jnp.mean(gg * xhat, axis=1, keepdims=True)
            dg = r * (gg - xhat * m)
            return dg * gate, dg * val * dgate, jnp.sum(dmix * xhat, axis=0, keepdims=True)

        dmix_a = _dot(d_outb, w_ref[:D_BRANCH, :], _NT)
        dmix_b = _dot(d_outb, w_ref[D_BRANCH:, :], _NT)
        d_o, dz_a, gsb = branch_bwd(dmix_a, o, sbw_ref[...], saved_a)
        d_y, dz_b, gssd = branch_bwd(dmix_b, y, ssdw_ref[...], saved_b)
        dosb_ref[...] = d_o
        dy_ref[...] = d_y
        dz_ref[:, :D_BRANCH] = dz_a.astype(BF16)
        dz_ref[:, D_BRANCH:] = dz_b.astype(BF16)
        gsb_ref[...] += gsb
        gssd_ref[...] += gssd

    row = lambda w, off: pl.BlockSpec((tm, w), lambda i: (i, off))
    full = lambda r, w: pl.BlockSpec((r, w), lambda i: (0, 0))
    resident = pl.BlockSpec((2 * D_BRANCH, D_MODEL), lambda i: (0, 0), pipeline_mode=pl.Buffered(1))
    tok = jax.ShapeDtypeStruct((t, D_MODEL), F32)
    return pl.pallas_call(
        body, name="mid",
        grid=(t // tm,),
        in_specs=[row(D_BRANCH, 0), row(D_BRANCH, 0), row(D_BRANCH, 3), row(D_BRANCH, 4),
                  row(D_MODEL, 0), row(D_MODEL, 0), full(1, D_BRANCH), full(1, D_BRANCH),
                  resident],
        out_specs=[row(D_MODEL, 0), row(D_BRANCH, 0), row(D_BRANCH, 0), row(2 * D_BRANCH, 0),
                   resident, full(1, D_BRANCH), full(1, D_BRANCH),
                   full(1, LANES)],
        out_shape=[tok, tok, tok, jax.ShapeDtypeStruct((t, 2 * D_BRANCH), BF16),
                   jax.ShapeDtypeStruct((2 * D_BRANCH, D_MODEL), F32),
                   jax.ShapeDtypeStruct((1, D_BRANCH), F32), jax.ShapeDtypeStruct((1, D_BRANCH), F32),
                   jax.ShapeDtypeStruct((1, LANES), F32)],
        compiler_params=_params(1),
    )(o_sb, y_ssd, proj, proj, x2, target, sb_w, ssd_w, w_out_b)


_DPROJ_FIRST = (0, 1, 2, 3, 5)
_DPROJ_BLOCKS = (1, 1, 1, 2, 2)
_DPROJ_OWNER = (0, 1, 2, 3, 3, 4, 4)


def _dproj_col(j, p):
    return jnp.clip(j - _DPROJ_FIRST[p], 0, _DPROJ_BLOCKS[p] - 1)


def _in_proj_bwd_x(d_parts, w_in_b, x2, d_out, norm_w, tm, slabs=()):
    t = x2.shape[0]
    n_parts = len(d_parts)
    n_slabs = len(slabs)
    n_rows = t // tm

    def body(*refs):
        dp_refs = refs[:n_parts]
        w_ref, x_ref, dout_ref, nw_ref = refs[n_parts:n_parts + 4]
        rest = refs[n_parts + 4:]
        src_refs, (gx_ref, gnw_ref) = rest[:n_slabs], rest[n_slabs:n_slabs + 2]
        dst_refs = rest[n_slabs + 2:2 * n_slabs + 2]
        acc_ref = rest[2 * n_slabs + 2]
        sems = rest[2 * n_slabs + 3:]
        i, j = pl.program_id(0), pl.program_id(1)

        if n_slabs:
            @pl.when((i == 0) & (j == 0))
            def _():
                _exchange_start(_exchange_copies(src_refs, dst_refs, (True,) * n_slabs, *sems))

            @pl.when((i == n_rows - 1) & (j == N_COLBLK - 1))
            def _():
                _exchange_wait(_exchange_copies(src_refs, dst_refs, (True,) * n_slabs, *sems))

        @pl.when((i == 0) & (j == 0))
        def _():
            gnw_ref[...] = jnp.zeros_like(gnw_ref)

        @pl.when(j == 0)
        def _():
            acc_ref[...] = jnp.zeros_like(acc_ref)

        for jj in range(N_COLBLK):
            @pl.when(j == jj)
            def _(jj=jj):
                acc_ref[...] += _dot(dp_refs[_DPROJ_OWNER[jj]][...], w_ref[...], _NT)

        @pl.when(j == N_COLBLK - 1)
        def _():
            xf = x_ref[...]
            d_hn = acc_ref[...]
            r = lax.rsqrt(jnp.mean(xf * xf, axis=1, keepdims=True) + EPS)
            xhat = xf * r
            g = d_hn * nw_ref[...]
            m = jnp.mean(g * xhat, axis=1, keepdims=True)
            gx_ref[...] = dout_ref[...] + r * (g - xhat * m)
            gnw_ref[...] += jnp.sum(d_hn * xhat, axis=0, keepdims=True)

    return pl.pallas_call(
        body, name="in_proj_bwd_x",
        grid=(t // tm, N_COLBLK),
        in_specs=[pl.BlockSpec((tm, 1024), lambda i, j, p=p: (i, _dproj_col(j, p)))
                  for p in range(n_parts)] + [
                  pl.BlockSpec((D_MODEL, 1024), lambda i, j: (0, j)),
                  pl.BlockSpec((tm, D_MODEL), lambda i, j: (i, 0)),
                  pl.BlockSpec((tm, D_MODEL), lambda i, j: (i, 0)),
                  pl.BlockSpec((1, D_MODEL), lambda i, j: (0, 0))] + [_ANY] * n_slabs,
        out_specs=[pl.BlockSpec((tm, D_MODEL), lambda i, j: (i, 0)),
                   pl.BlockSpec((1, D_MODEL), lambda i, j: (0, 0))] + [_ANY] * n_slabs,
        out_shape=[jax.ShapeDtypeStruct((t, D_MODEL), F32), jax.ShapeDtypeStruct((1, D_MODEL), F32)]
                  + _exchange_shapes(slabs, (True,) * n_slabs),
        scratch_shapes=[pltpu.VMEM((tm, D_MODEL), F32)] + (_exchange_sems(n_slabs) if n_slabs else []),
        compiler_params=_params(2),
    )(*d_parts, w_in_b, x2, d_out, norm_w, *slabs)


def _in_proj_bwd_w(hn, d_parts, tm):
    t = hn.shape[0]
    n_parts = len(d_parts)

    def body(hn_ref, *refs):
        dp_refs, gw_ref = refs[:n_parts], refs[n_parts]
        j = pl.program_id(0)

        @pl.when(pl.program_id(1) == 0)
        def _():
            gw_ref[...] = jnp.zeros_like(gw_ref)

        for jj in range(N_COLBLK):
            @pl.when(j == jj)
            def _(jj=jj):
                gw_ref[...] += _dot(hn_ref[...], dp_refs[_DPROJ_OWNER[jj]][...], _TN)

    def part_spec(p):
        def index(j, i):
            mine = (j >= _DPROJ_FIRST[p]) & (j < _DPROJ_FIRST[p] + _DPROJ_BLOCKS[p])
            return jnp.where(mine, i, 0), _dproj_col(j, p)
        return pl.BlockSpec((tm, 1024), index)

    return pl.pallas_call(
        body, name="in_proj_bwd_w",
        grid=(N_COLBLK, t // tm),
        in_specs=[pl.BlockSpec((tm, D_MODEL), lambda j, i: (i, 0))]
                 + [part_spec(p) for p in range(n_parts)],
        out_specs=pl.BlockSpec((D_MODEL, 1024), lambda j, i: (0, j)),
        out_shape=jax.ShapeDtypeStruct((D_MODEL, D_IN_PAD), F32),
        compiler_params=_params(2),
    )(hn, *d_parts)


def _adamw(parts, w, m, v, tr, name):
    _, rows, cols = w.shape
    c1 = 1.0 - ADAM_B1 ** ADAM_STEP
    c2 = 1.0 - ADAM_B2 ** ADAM_STEP

    def body(p_ref, w_ref, m_ref, v_ref, g_ref, d_ref, nm_ref, nv_ref):
        g = p_ref[0].astype(F32)
        for s in range(1, N_DEV):
            g = g + p_ref[s].astype(F32)
        nm = ADAM_B1 * m_ref[0] + (1.0 - ADAM_B1) * g
        nv = ADAM_B2 * v_ref[0] + (1.0 - ADAM_B2) * (g * g)
        g_ref[0] = g
        nm_ref[0] = nm
        nv_ref[0] = nv
        d_ref[0] = -ADAM_LR * ((nm / c1) / (jnp.sqrt(nv / c2) + ADAM_EPS) + ADAM_WD * w_ref[0])

    blk = pl.BlockSpec((1, tr, cols), lambda i: (0, i, 0))
    shape = jax.ShapeDtypeStruct((1, rows, cols), F32)
    return pl.pallas_call(
        body, name=name,
        grid=(rows // tr,),
        in_specs=[pl.BlockSpec((N_DEV, tr, cols), lambda i: (0, i, 0)), blk, blk, blk],
        out_specs=[blk, blk, blk, blk],
        out_shape=[shape, shape, shape, shape],
        compiler_params=_params(1),
    )(parts, w, m, v)


def _mesh_place():
    x, y, c = lax.axis_index("x"), lax.axis_index("y"), lax.axis_index("c")
    return x, y, c, 4 * x + 2 * y + c


def _peer(x, y, c, k):
    px = 1 - x if k & 4 else x
    py = 1 - y if k & 2 else y
    pc = 1 - c if k & 1 else c
    return (px, py, pc), 4 * px + 2 * py + pc


def _exchange(srcs, scatter, name):
    n = len(srcs)

    def body(*refs):
        copies = _exchange_copies(refs[:n], refs[n:2 * n], scatter, *refs[2 * n:])
        _exchange_start(copies)
        _exchange_wait(copies)

    return pl.pallas_call(
        body, name=name,
        in_specs=[_ANY] * n, out_specs=[_ANY] * n, out_shape=_exchange_shapes(srcs, scatter),
        scratch_shapes=_exchange_sems(n),
    )(*srcs)


def _gather_two_level(shard, name):
    def body(x_ref, out_ref, send_sems, recv_sems, local_sem):
        x, y, c, me = _mesh_place()
        sibling = (x, y, 1 - c)
        chips = [(1 - x, y), (x, 1 - y), (1 - x, 1 - y)]

        def slab(px, py, pc):
            return out_ref.at[4 * px + 2 * py + pc]

        def copy(k, block, to, src=None):
            return pltpu.make_async_remote_copy(
                src_ref=slab(*block) if src is None else src, dst_ref=slab(*block),
                send_sem=send_sems.at[k], recv_sem=recv_sems.at[k],
                device_id=to, device_id_type=pl.DeviceIdType.MESH)

        mine = pltpu.make_async_copy(x_ref, slab(x, y, c), local_sem)
        mine.start()
        first = [copy(0, (x, y, c), sibling, src=x_ref)]
        first += [copy(1 + j, (x, y, c), (*chip, c), src=x_ref) for j, chip in enumerate(chips)]
        for cp in first:
            cp.start()
        passed = [copy(4 + j, (*chip, c), sibling) for j, chip in enumerate(chips)]
        for j, chip in enumerate(chips):
            copy(1 + j, (*chip, c), (x, y, c)).wait_recv()
            passed[j].start()
        copy(0, sibling, (x, y, c)).wait_recv()
        for j, chip in enumerate(chips):
            copy(4 + j, (*chip, 1 - c), (x, y, c)).wait_recv()
        for cp in first + passed:
            cp.wait_send()
        mine.wait()

    return pl.pallas_call(
        body, name=name,
        in_specs=[_ANY], out_specs=_ANY,
        out_shape=jax.ShapeDtypeStruct((N_DEV,) + shard.shape, shard.dtype),
        scratch_shapes=[pltpu.SemaphoreType.DMA((N_DEV - 1,)), pltpu.SemaphoreType.DMA((N_DEV - 1,)),
                        pltpu.SemaphoreType.DMA],
    )(shard)


_ANY = pl.BlockSpec(memory_space=pl.ANY)


def _exchange_shapes(srcs, scatter):
    return [jax.ShapeDtypeStruct(s.shape if sc else (N_DEV,) + s.shape, s.dtype)
            for s, sc in zip(srcs, scatter)]


def _exchange_sems(n):
    return [pltpu.SemaphoreType.DMA((n * (N_DEV - 1),)),
            pltpu.SemaphoreType.DMA((n * (N_DEV - 1),)),
            pltpu.SemaphoreType.DMA((n,))]


def _exchange_copies(src_refs, dst_refs, scatter, send_sems, recv_sems, loc_sems):
    n = len(src_refs)
    x, y, c, me = _mesh_place()

    def src_of(i, idx):
        return src_refs[i].at[idx] if scatter[i] else src_refs[i]

    local = [pltpu.make_async_copy(src_of(i, me), dst_refs[i].at[me], loc_sems.at[i])
             for i in range(n)]
    sends, recvs = [], []
    for k in range(1, N_DEV):
        peer, pidx = _peer(x, y, c, k)
        for i in range(n):
            s = i * (N_DEV - 1) + k - 1
            for dst_slab, group in ((me, sends), (pidx, recvs)):
                group.append(pltpu.make_async_remote_copy(
                    src_ref=src_of(i, pidx), dst_ref=dst_refs[i].at[dst_slab],
                    send_sem=send_sems.at[s], recv_sem=recv_sems.at[s],
                    device_id=peer, device_id_type=pl.DeviceIdType.MESH))
    return local, sends, recvs


def _exchange_start(copies):
    local, sends, _ = copies
    for cp in local + sends:
        cp.start()


def _exchange_wait(copies):
    local, sends, recvs = copies
    for cp in recvs:
        cp.wait_recv()
    for cp in sends:
        cp.wait_send()
    for cp in local:
        cp.wait()


def _pad_lanes(v, width=LANES):
    return jnp.pad(v, ((0, 0), (0, width - v.shape[1])))


def _local_step(x, target, norm_w, w_in_b, q_norm_w, k_norm_w, conv_w, conv_b, dt_bias, a_log,
                d_skip, sb_norm_w, ssd_norm_w, w_out_b, tm=512, tq=512, tmid=256, blk=ATT_BLK,
                scatter=False):
    nb, seq, _ = x.shape
    t = nb * seq
    x2 = x.reshape(t, D_MODEL)
    tg2 = target.reshape(t, D_MODEL)
    qw2 = jnp.tile(q_norm_w, (1, 2))
    kw2 = jnp.tile(k_norm_w, (1, 2))
    dtb, alog, dsk = _pad_lanes(dt_bias), _pad_lanes(a_log), _pad_lanes(d_skip)

    tproj = min(2 * tm, t)
    if scatter:
        proj, hn, wout_all, cw_all = _in_proj(x2, norm_w, w_in_b, tproj, (w_out_b, conv_w))
        w_out_b = wout_all.reshape(2 * D_BRANCH, D_MODEL)
        conv_w = jnp.transpose(cw_all, (1, 0, 2)).reshape(CONV_TAPS, D_CONV)
    else:
        proj, hn = _in_proj(x2, norm_w, w_in_b, tproj)
    qs, kn, vb, kt = _qk_prep(proj, qw2, kw2, nb, seq, tq)
    o_sb, sb_tot, sb_low = _attn_fwd(qs, kn, vb, nb, seq, blk)
    y_ssd, states = _ssd_fwd(proj, conv_w, conv_b, dtb, alog, dsk, nb, seq)
    d_out, d_osb, d_y, d_z, g_wout, g_sbw, g_ssdw, loss = _mid(
        o_sb, y_ssd, proj, x2, tg2, sb_norm_w, ssd_norm_w, w_out_b, tmid)
    dqs, dkn, dvh = _attn_bwd(qs, kn, kt, vb, sb_tot, sb_low, d_osb, nb, seq, blk)
    dq_raw, dk_raw, dv_raw, g_qw, g_kw = _qk_bwd(proj, dqs, dkn, dvh, qw2, kw2, nb, seq, tq)
    d_xbc, g_cw, g_cb, g_dtb, g_alog, g_dsk = _ssd_bwd(
        proj, d_y, states, conv_w, conv_b, dtb, alog, dsk, nb, seq)
    d_parts = [dq_raw, dk_raw, dv_raw, d_z, d_xbc]
    g_win = _in_proj_bwd_w(hn, d_parts, tm)[:, :D_IN]
    g_cw = g_cw[:CONV_TAPS]
    if scatter:
        grad_x, g_nw, g_win, g_wout, g_cw = _in_proj_bwd_x(
            d_parts, w_in_b, x2, d_out, norm_w, tm, _grad_slabs(g_win, g_wout, g_cw))
    else:
        grad_x, g_nw = _in_proj_bwd_x(d_parts, w_in_b, x2, d_out, norm_w, tm)

    small = dict(
        norm_w=g_nw,
        q_norm_w=g_qw[:, :HEAD_DIM] + g_qw[:, HEAD_DIM:],
        k_norm_w=g_kw[:, :HEAD_DIM] + g_kw[:, HEAD_DIM:],
        conv_b=g_cb, dt_bias=g_dtb[:, :N_HEADS], A_log=g_alog[:, :N_HEADS],
        D_skip=g_dsk[:, :N_HEADS], sb_norm_w=g_sbw, ssd_norm_w=g_ssdw)
    return loss[0, 0], grad_x.reshape(nb, seq, D_MODEL), g_win, g_wout, g_cw, small


def _grad_slabs(g_win, g_wout, g_cw):
    w_sh = D_IN // N_DEV
    c_sh = D_CONV // N_DEV
    return (jnp.transpose(g_win.reshape(D_MODEL, N_DEV, w_sh), (1, 0, 2)).astype(BF16),
            g_wout.reshape(N_DEV, 2 * D_BRANCH // N_DEV, D_MODEL).astype(BF16),
            jnp.pad(jnp.transpose(g_cw.reshape(CONV_TAPS, N_DEV, c_sh), (1, 0, 2)),
                    ((0, 0), (0, 8 - CONV_TAPS), (0, 0))))


_SMALL = ("norm_w", "q_norm_w", "k_norm_w", "conv_b", "dt_bias", "A_log", "D_skip",
          "sb_norm_w", "ssd_norm_w")


def _pack_small(vals):
    flat = jnp.concatenate([_pad_lanes(vals[n], -(-vals[n].shape[1] // LANES) * LANES)
                            for n in _SMALL], axis=1)
    return jnp.pad(flat, ((0, 0), (0, 48 * LANES - flat.shape[1]))).reshape(48, LANES)


def _unpack_small(packed, like):
    out, r = {}, 0
    for n in _SMALL:
        width = like[n].shape[1]
        nr = -(-width // LANES)
        out[n] = packed[r:r + nr].reshape(1, nr * LANES)[:, :width]
        r += nr
    return out


def kernel(x, norm_w, w_in, q_norm_w, k_norm_w, conv_w, conv_b, dt_bias, A_log, D_skip, sb_norm_w, ssd_norm_w, w_out, loss_target, m_norm_w, m_w_in, m_q_norm_w, m_k_norm_w, m_conv_w, m_conv_b, m_dt_bias, m_A_log, m_D_skip, m_sb_norm_w, m_ssd_norm_w, m_w_out, v_norm_w, v_w_in, v_q_norm_w, v_k_norm_w, v_conv_w, v_conv_b, v_dt_bias, v_A_log, v_D_skip, v_sb_norm_w, v_ssd_norm_w, v_w_out):
    win_all = _gather_two_level(w_in[0].astype(BF16), "gather_w_in")
    w_in_b = jnp.pad(jnp.transpose(win_all, (1, 0, 2)).reshape(D_MODEL, D_IN),
                     ((0, 0), (0, D_IN_PAD - D_IN)))

    loss, grad_x, win_parts, wout_parts, cw_parts, g_small = _local_step(
        x, loss_target, norm_w, w_in_b, q_norm_w, k_norm_w, conv_w[0], conv_b, dt_bias, A_log,
        D_skip, sb_norm_w, ssd_norm_w, w_out[0].astype(BF16), scatter=True)
    small_parts, = _exchange([_pack_small(g_small)], [False], "gather_small_grads")

    small_w = dict(norm_w=norm_w, q_norm_w=q_norm_w, k_norm_w=k_norm_w, conv_b=conv_b,
                   dt_bias=dt_bias, A_log=A_log, D_skip=D_skip, sb_norm_w=sb_norm_w,
                   ssd_norm_w=ssd_norm_w)
    small_m = dict(norm_w=m_norm_w, q_norm_w=m_q_norm_w, k_norm_w=m_k_norm_w, conv_b=m_conv_b,
                   dt_bias=m_dt_bias, A_log=m_A_log, D_skip=m_D_skip, sb_norm_w=m_sb_norm_w,
                   ssd_norm_w=m_ssd_norm_w)
    small_v = dict(norm_w=v_norm_w, q_norm_w=v_q_norm_w, k_norm_w=v_k_norm_w, conv_b=v_conv_b,
                   dt_bias=v_dt_bias, A_log=v_A_log, D_skip=v_D_skip, sb_norm_w=v_sb_norm_w,
                   ssd_norm_w=v_ssd_norm_w)

    pad8 = lambda a: jnp.pad(a, ((0, 0), (0, 8 - CONV_TAPS), (0, 0)))
    r_win = _adamw(win_parts, w_in, m_w_in, v_w_in, 128, "adamw_w_in")
    r_wout = _adamw(wout_parts, w_out, m_w_out, v_w_out, 128, "adamw_w_out")
    r_cw = _adamw(cw_parts, pad8(conv_w), pad8(m_conv_w), pad8(v_conv_w), 8, "adamw_conv_w")
    r_small = _adamw(small_parts, _pack_small(small_w)[None], _pack_small(small_m)[None],
                     _pack_small(small_v)[None], 48, "adamw_small")

    loss = lax.psum(loss, ("x", "y", "c"))
    res = {"w_in": r_win, "w_out": r_wout, "conv_w": [a[:, :CONV_TAPS] for a in r_cw]}
    unpacked = [_unpack_small(a[0], small_w) for a in r_small]
    for n in _SMALL:
        res[n] = [u[n] for u in unpacked]
    order = ("norm_w", "w_in", "q_norm_w", "k_norm_w", "conv_w", "conv_b", "dt_bias", "A_log",
             "D_skip", "sb_norm_w", "ssd_norm_w", "w_out")
    outs = [loss, grad_x]
    for kind in range(4):
        outs += [res[n][kind] for n in order]
    return tuple(outs)
```

```python
import functools
import math

import jax
import jax.numpy as jnp
from jax import lax
from jax.experimental import pallas as pl
from jax.experimental.pallas import tpu as pltpu

F32 = jnp.float32
BF16 = jnp.bfloat16

D_MODEL = 1024
N_HEADS = 16
HEAD_DIM = 64
N_PAIRS = N_HEADS // 2
D_BRANCH = 1024
N_GROUPS = 2
HEADS_PER_GROUP = 8
D_STATE = 128
GROUP_W = HEADS_PER_GROUP * HEAD_DIM
D_BC = 2 * N_GROUPS * D_STATE
D_CONV = D_BRANCH + D_BC
D_IN = 6672
D_IN_PAD = 7168
N_COLBLK = D_IN_PAD // 1024
COL_XS = 5120
COL_BC = 6144
COL_DT = 6656
EPS = 1e-6
CONV_TAPS = 4
N_DEV = 8

LANES = 128
SSD_CHUNK = 128
ATT_BLK = 256
EXP_UNDERFLOW = -105.0
VMEM_LIMIT = 56 * 1024 * 1024

ADAM_LR = 0.001
ADAM_B1 = 0.9
ADAM_B2 = 0.999
ADAM_EPS = 1e-08
ADAM_WD = 0.01
ADAM_STEP = 10

_NT = (((1,), (1,)), ((), ()))
_TN = (((0,), (0,)), ((), ()))


def _params(n_grid):
    return pltpu.CompilerParams(dimension_semantics=("arbitrary",) * n_grid,
                                vmem_limit_bytes=VMEM_LIMIT)


def _dot(a, b, dims=None, precision=None):
    if dims is None:
        return jnp.dot(a, b, preferred_element_type=F32, precision=precision)
    return lax.dot_general(a, b, dims, preferred_element_type=F32, precision=precision)


def _sigmoid(x):
    return 1.0 / (1.0 + jnp.exp(-x))


def _softplus(x):
    return jnp.maximum(x, 0.0) + jnp.log(1.0 + jnp.exp(-jnp.abs(x)))


def _split_bf16(x):
    hi = x.astype(BF16)
    lo = (x - hi.astype(F32)).astype(BF16)
    return hi, lo


def _lane_iota(shape):
    return lax.broadcasted_iota(jnp.int32, shape, len(shape) - 1)


def _row_iota(shape):
    return lax.broadcasted_iota(jnp.int32, shape, len(shape) - 2)


def _pair_sum(x):
    r = lax.broadcasted_iota(jnp.int32, (LANES, LANES), 0)
    c = lax.broadcasted_iota(jnp.int32, (LANES, LANES), 1)
    same_head = jnp.where(r // HEAD_DIM == c // HEAD_DIM, 1.0, 0.0).astype(BF16)
    hi, lo = _split_bf16(x)
    return _dot(hi, same_head) + _dot(lo, same_head)


def _pair_head(x, a):
    lane = _lane_iota(x.shape)
    mine = (lane < HEAD_DIM) if a == 0 else (lane >= HEAD_DIM)
    return jnp.where(mine, x, jnp.zeros_like(x))


def _head_expand():
    r = lax.broadcasted_iota(jnp.int32, (LANES, D_BRANCH), 0)
    c = lax.broadcasted_iota(jnp.int32, (LANES, D_BRANCH), 1)
    return jnp.where(c // HEAD_DIM == r, 1.0, 0.0).astype(BF16)


def _in_proj(x2, norm_w, w_in_b, tm, shards=()):
    t = x2.shape[0]

    def body(x_ref, nw_ref, w_ref, proj_ref, hn_ref):
        @pl.when(pl.program_id(1) == 0)
        def _():
            xf = x_ref[...]
            r = lax.rsqrt(jnp.mean(xf * xf, axis=1, keepdims=True) + EPS)
            hn_ref[...] = (xf * r * nw_ref[...]).astype(BF16)

        proj_ref[...] = _dot(hn_ref[...], w_ref[...])

    return _call_with_exchange(
        body, (x2, norm_w, w_in_b), shards, (False,) * len(shards), name="in_proj",
        grid=(t // tm, N_COLBLK),
        in_specs=[pl.BlockSpec((tm, D_MODEL), lambda i, j: (i, 0)),
                  pl.BlockSpec((1, D_MODEL), lambda i, j: (0, 0)),
                  pl.BlockSpec((D_MODEL, 1024), lambda i, j: (0, j))],
        out_specs=[pl.BlockSpec((tm, 1024), lambda i, j: (i, j)),
                   pl.BlockSpec((tm, D_MODEL), lambda i, j: (i, 0))],
        out_shape=[jax.ShapeDtypeStruct((t, D_IN_PAD), F32),
                   jax.ShapeDtypeStruct((t, D_MODEL), BF16)])


def _qk_prep(proj, qw2, kw2, nb, seq, tq):
    nl = seq // tq
    scale = 1.0 / math.sqrt(HEAD_DIM)

    def body(q_ref, k_ref, v_ref, qw_ref, kw_ref, qs_ref, kn_ref, vb_ref, kt_ref):
        def norm(x, w):
            r = lax.rsqrt(_pair_sum(x * x) * (1.0 / HEAD_DIM) + EPS)
            return x * r * w

        vb_ref[...] = v_ref[...].astype(BF16)
        for p in range(N_PAIRS):
            cols = slice(p * LANES, (p + 1) * LANES)
            kn = norm(k_ref[:, cols], kw_ref[...])
            qs_ref[:, cols] = (norm(q_ref[:, cols], qw_ref[...]) * scale).astype(BF16)
            kn_ref[:, cols] = kn.astype(BF16)
            kt_ref[0, p] = kn.T.astype(BF16)

    tok_shape = jax.ShapeDtypeStruct((nb * seq, D_BRANCH), BF16)
    tok = lambda blk: pl.BlockSpec((tq, D_BRANCH), lambda b, i: (b * nl + i, blk))
    vec = pl.BlockSpec((1, LANES), lambda b, i: (0, 0))
    return pl.pallas_call(
        body, name="qk_prep",
        grid=(nb, nl),
        in_specs=[tok(0), tok(1), tok(2), vec, vec],
        out_specs=[tok(0), tok(0), tok(0),
                   pl.BlockSpec((1, N_PAIRS, LANES, tq), lambda b, i: (b, 0, 0, i))],
        out_shape=[tok_shape, tok_shape, tok_shape,
                   jax.ShapeDtypeStruct((nb, N_PAIRS, LANES, seq), BF16)],
        compiler_params=_params(2),
    )(proj, proj, proj, qw2, kw2)


def _attn_fwd(qs, kn, vb, nb, seq, blk):
    nq = seq // blk

    def body(q_ref, k_ref, v_ref, o_ref, tot_ref, low_ref, kmax_ref):
        qi = pl.program_id(2)
        r_i = lax.broadcasted_iota(jnp.int32, (blk, blk), 0)
        c_i = lax.broadcasted_iota(jnp.int32, (blk, blk), 1)
        csum = jnp.where(r_i >= c_i, 1.0, 0.0).astype(BF16)
        causal = c_i < r_i
        heads = range(2)

        head = _pair_head

        @pl.when(qi == 0)
        def _():
            kk = k_ref[...].astype(F32)
            for a in heads:
                ksq = jnp.sum(head(kk * kk, a), axis=1, keepdims=True)
                kmax_ref[a] = jnp.full((8, LANES), jnp.max(ksq))

        q_pair = q_ref[...]
        qf = q_pair.astype(F32)
        q_head = [head(q_pair, a) for a in heads]
        zmax = []
        for a in heads:
            qsq = jnp.sum(head(qf * qf, a), axis=1, keepdims=True)
            zmax.append(1.01 * jnp.sqrt(qsq * kmax_ref[a][0:1, 0:1]) + 0.01)

        def exhausted(run):
            top = jnp.maximum(jnp.max(run[0] + zmax[0]), jnp.max(run[1] + zmax[1]))
            return top < EXP_UNDERFLOW

        def sweep(blocks, run, acc):
            offs = [pl.multiple_of(j * blk, blk) for j, _, _ in blocks]
            z = [[_dot(q_head[a], k_ref[pl.ds(off, blk), :], _NT) for a in heads]
                 for off in offs]
            cl = []
            for (_, diag, valid), zb in zip(blocks, z):
                lkb = []
                for a in heads:
                    lk = -_softplus(zb[a])
                    if diag:
                        lk = jnp.where(causal, lk, 0.0)
                    if valid is not None:
                        lk = jnp.where(valid, lk, 0.0)
                    lkb.append(lk.astype(BF16))
                cl.append([_dot(lkb[a], csum) for a in heads])
            for (_, diag, valid), zb, clb, off in zip(blocks, z, cl, offs):
                w = []
                for a in heads:
                    wa = jnp.exp(zb[a] + clb[a] + run[a])
                    if diag:
                        wa = jnp.where(causal, wa, 0.0)
                    if valid is not None:
                        wa = jnp.where(valid, wa, 0.0)
                    w.append(wa.astype(BF16))
                run = [run[a] + clb[a][:, 0:1] for a in heads]
                v_pair = v_ref[pl.ds(off, blk), :]
                acc = acc + _dot(w[0], head(v_pair, 0)) + _dot(w[1], head(v_pair, 1))
            return run, acc

        run = [jnp.zeros((blk, 1), F32)] * 2
        acc = jnp.zeros((blk, LANES), F32)
        run, acc = sweep([(qi, True, None), (jnp.maximum(qi - 1, 0), False, qi >= 1)], run, acc)
        low = jnp.maximum(qi - 1, 0)

        def more(carry):
            low, done, _, _ = carry
            return (low > 0) & jnp.logical_not(done)

        def pair(carry):
            low, _, run, acc = carry
            run, acc = sweep([(low - 1, False, None), (jnp.maximum(low - 2, 0), False, low >= 2)],
                             run, acc)
            return jnp.maximum(low - 2, 0), exhausted(run), run, acc

        low, _, run, acc = lax.while_loop(more, pair, (low, exhausted(run), run, acc))
        low_ref[pl.program_id(0) * N_PAIRS + pl.program_id(1), qi] = low.astype(F32)
        o_ref[...] = acc
        for a in heads:
            as_row = jnp.sum(jnp.where(r_i == c_i, run[a], 0.0), axis=0, keepdims=True)
            tot_ref[0, a, 0] = jnp.broadcast_to(as_row, (8, blk))

    return pl.pallas_call(
        body, name="sb_attn_fwd",
        grid=(nb, N_PAIRS, nq),
        in_specs=[pl.BlockSpec((blk, LANES), lambda b, h, i: (b * nq + i, h)),
                  pl.BlockSpec((seq, LANES), lambda b, h, i: (b, h)),
                  pl.BlockSpec((seq, LANES), lambda b, h, i: (b, h))],
        out_specs=[pl.BlockSpec((blk, LANES), lambda b, h, i: (b * nq + i, h)),
                   pl.BlockSpec((1, 2, 1, 8, blk), lambda b, h, i: (b, h, i, 0, 0)),
                   pl.BlockSpec(memory_space=pltpu.SMEM)],
        out_shape=[jax.ShapeDtypeStruct((nb * seq, D_BRANCH), F32),
                   jax.ShapeDtypeStruct((nb, N_HEADS, nq, 8, blk), F32),
                   jax.ShapeDtypeStruct((nb * N_PAIRS, nq), F32)],
        scratch_shapes=[pltpu.VMEM((2, 8, LANES), F32)],
        compiler_params=_params(3),
    )(qs, kn, vb)


def _attn_bwd(qs, kn, kt, vb, tot, low, d_o, nb, seq, blk):
    nq = seq // blk

    def body(q_ref, k_ref, kt_ref, v_ref, tot_ref, low_ref, do_ref, dq_ref, dk_ref, dv_ref):
        qi = pl.program_id(2)

        @pl.when(qi == 0)
        def _():
            dk_ref[...] = jnp.zeros_like(dk_ref)
            dv_ref[...] = jnp.zeros_like(dv_ref)

        r_i = lax.broadcasted_iota(jnp.int32, (blk, blk), 0)
        c_i = lax.broadcasted_iota(jnp.int32, (blk, blk), 1)
        before = jnp.where(c_i < r_i, 1.0, 0.0).astype(BF16)
        upto = jnp.where(c_i <= r_i, 1.0, 0.0).astype(BF16)
        causal = r_i < c_i

        heads = range(2)
        q_head = [_pair_head(q_ref[...], a) for a in heads]
        d_ob = [_pair_head(do_ref[...].astype(BF16), a) for a in heads]
        total = [tot_ref[0, a, 0][0:1, :] for a in heads]

        def sweep(blocks, lsum, esum, dqt):
            def keep(x, diag, valid):
                if diag:
                    x = jnp.where(causal, x, 0.0)
                if valid is not None:
                    x = jnp.where(valid, x, 0.0)
                return x

            offs = [pl.multiple_of(j * blk, blk) for j, _, _ in blocks]
            zt = [[_dot(k_ref[pl.ds(off, blk), :], q_head[a], _NT) for a in heads]
                  for off in offs]
            dwt = [[_dot(v_ref[pl.ds(off, blk), :], d_ob[a], _NT) for a in heads]
                   for off in offs]
            sp, lk, lpre = [], [], []
            for (_, diag, valid), ztb in zip(blocks, zt):
                sp.append([_softplus(ztb[a]) for a in heads])
                lk.append([keep(-sp[-1][a], diag, valid).astype(BF16) for a in heads])
                lpre.append([_dot(before, lk[-1][a]) for a in heads])
            wt, et, epre = [], [], []
            for i, (_, diag, valid) in enumerate(blocks):
                wt.append([keep(jnp.exp(zt[i][a] + (total[a] - lsum[a] - lpre[i][a])), diag, valid)
                           for a in heads])
                et.append([dwt[i][a] * wt[i][a] for a in heads])
                split = [_split_bf16(et[i][a]) for a in heads]
                epre.append([_dot(upto, split[a][0]) + _dot(upto, split[a][1]) for a in heads])
                lsum = [lsum[a] + lpre[i][a][blk - 1:blk, :] + lk[i][a][blk - 1:blk, :]
                        for a in heads]
            for i, (_, diag, valid) in enumerate(blocks):
                dzb = [keep(et[i][a] - jnp.exp(zt[i][a] - sp[i][a]) * (esum[a] + epre[i][a]),
                            diag, valid).astype(BF16) for a in heads]
                esum = [esum[a] + epre[i][a][blk - 1:blk, :] for a in heads]
                dk_ref[pl.ds(offs[i], blk), :] += (_dot(dzb[0], q_head[0]) + _dot(dzb[1], q_head[1]))
                dv_ref[pl.ds(offs[i], blk), :] += (_dot(wt[i][0].astype(BF16), d_ob[0])
                                                   + _dot(wt[i][1].astype(BF16), d_ob[1]))
                kt_pair = kt_ref[0, 0, :, pl.ds(offs[i], blk)]
                dqt = [dqt[a] + _dot(kt_pair, dzb[a]) for a in heads]
            return lsum, esum, dqt

        row = [jnp.zeros((1, blk), F32)] * 2
        dqt = [jnp.zeros((LANES, blk), F32)] * 2
        low = low_ref[pl.program_id(0) * N_PAIRS + pl.program_id(1), qi].astype(jnp.int32)
        low = jnp.clip(low, 0, jnp.maximum(qi - 1, 0))

        def pair(carry):
            j, lsum, esum, dqt = carry
            return (j + 2,) + sweep([(j, False, None), (j + 1, False, j + 1 < qi - 1)],
                                    lsum, esum, dqt)

        _, lsum, esum, dqt = lax.while_loop(lambda c: c[0] < qi - 1, pair, (low, row, row, dqt))
        _, _, dqt = sweep([(jnp.maximum(qi - 1, 0), False, qi >= 1), (qi, True, None)],
                          lsum, esum, dqt)
        top = _row_iota((LANES, blk)) < HEAD_DIM
        dq_ref[...] = jnp.where(top, dqt[0], dqt[1]).T

    seq_blk = pl.BlockSpec((seq, LANES), lambda b, h, i: (b, h))
    tok = pl.BlockSpec((blk, LANES), lambda b, h, i: (b * nq + i, h))
    tok_shape = jax.ShapeDtypeStruct((nb * seq, D_BRANCH), F32)
    return pl.pallas_call(
        body, name="sb_attn_bwd",
        grid=(nb, N_PAIRS, nq),
        in_specs=[tok, seq_blk,
                  pl.BlockSpec((1, 1, LANES, seq), lambda b, h, i: (b, h, 0, 0)),
                  seq_blk,
                  pl.BlockSpec((1, 2, 1, 8, blk), lambda b, h, i: (b, h, i, 0, 0)),
                  pl.BlockSpec(memory_space=pltpu.SMEM),
                  tok],
        out_specs=[tok, seq_blk, seq_blk],
        out_shape=[tok_shape, tok_shape, tok_shape],
        compiler_params=_params(3),
    )(qs, kn, kt, vb, tot, low, d_o)


def _qk_bwd(proj, dqs, dkn, dvh, qw2, kw2, nb, seq, tq):
    nl = seq // tq
    scale = 1.0 / math.sqrt(HEAD_DIM)

    def body(q_ref, k_ref, dq_ref, dk_ref, dv_ref, qw_ref, kw_ref,
             dqr_ref, dkr_ref, dvr_ref, gq_ref, gk_ref):
        @pl.when((pl.program_id(0) == 0) & (pl.program_id(1) == 0))
        def _():
            gq_ref[...] = jnp.zeros_like(gq_ref)
            gk_ref[...] = jnp.zeros_like(gk_ref)

        def norm_bwd(x, w, dy):
            r = lax.rsqrt(_pair_sum(x * x) * (1.0 / HEAD_DIM) + EPS)
            xhat = x * r
            g = dy * w
            m = _pair_sum(g * xhat) * (1.0 / HEAD_DIM)
            return r * (g - xhat * m), jnp.sum(dy * xhat, axis=0, keepdims=True)

        dvr_ref[...] = dv_ref[...].astype(BF16)
        gq = jnp.zeros((1, LANES), F32)
        gk = jnp.zeros((1, LANES), F32)
        for p in range(N_PAIRS):
            cols = slice(p * LANES, (p + 1) * LANES)
            dqr, gq_p = norm_bwd(q_ref[:, cols], qw_ref[...], dq_ref[:, cols] * scale)
            dkr, gk_p = norm_bwd(k_ref[:, cols], kw_ref[...], dk_ref[:, cols])
            dqr_ref[:, cols] = dqr.astype(BF16)
            dkr_ref[:, cols] = dkr.astype(BF16)
            gq, gk = gq + gq_p, gk + gk_p
        gq_ref[...] += gq
        gk_ref[...] += gk

    tok = lambda blk: pl.BlockSpec((tq, D_BRANCH), lambda b, i: (b * nl + i, blk))
    vec = pl.BlockSpec((1, LANES), lambda b, i: (0, 0))
    tshape = jax.ShapeDtypeStruct((nb * seq, D_BRANCH), BF16)
    return pl.pallas_call(
        body, name="qk_bwd",
        grid=(nb, nl),
        in_specs=[tok(0), tok(1), tok(0), tok(0), tok(0), vec, vec],
        out_specs=[tok(0), tok(0), tok(0), vec, vec],
        out_shape=[tshape, tshape, tshape,
                   jax.ShapeDtypeStruct((1, LANES), F32), jax.ShapeDtypeStruct((1, LANES), F32)],
        compiler_params=_params(2),
    )(proj, proj, dqs, dkn, dvh, qw2, kw2)


def _shift_down(cur, prev, k):
    if k == 0:
        return cur
    rows = _row_iota(cur.shape)
    return jnp.where(rows < k, pltpu.roll(prev, k, axis=0), pltpu.roll(cur, k, axis=0))


def _shift_up(cur, nxt, k):
    if k == 0:
        return cur
    n = cur.shape[0]
    rows = _row_iota(cur.shape)
    return jnp.where(rows < n - k, pltpu.roll(cur, n - k, axis=0), pltpu.roll(nxt, n - k, axis=0))


def _conv_taps(cur, prev):
    return [_shift_down(cur, prev, CONV_TAPS - 1 - i) for i in range(CONV_TAPS)]


def _conv_pre(taps, w, b):
    out = b
    for i in range(CONV_TAPS):
        out = out + taps[i] * w[i:i + 1, :]
    return out


def _silu(x):
    return x * _sigmoid(x)


def _silu_and_grad(x):
    s = _sigmoid(x)
    return x * s, s * (1.0 + x * (1.0 - s))


def _dot01(x, m01, parts, dims=None, m_left=False):
    total, rest = None, x
    for i in range(parts):
        piece = rest.astype(BF16)
        if i + 1 < parts:
            rest = rest - piece.astype(F32)
        term = _dot(m01, piece, dims) if m_left else _dot(piece, m01, dims)
        total = term if total is None else total + term
    return total


def _chunk_decay(dt_raw, dtb, alog, expand, qc):
    dt = _softplus(dt_raw + dtb)
    d_a = dt * (-jnp.exp(alog))
    r_i = lax.broadcasted_iota(jnp.int32, (qc, qc), 0)
    c_i = lax.broadcasted_iota(jnp.int32, (qc, qc), 1)
    tril = r_i >= c_i
    a_cs = _dot01(d_a, jnp.where(tril, 1.0, 0.0).astype(BF16), 3, m_left=True)
    dt_x = _dot01(dt, expand, 3)
    acs_x = _dot01(a_cs, expand, 3)
    return dt, d_a, a_cs, dt_x, acs_x, tril


def _ssd_fwd(proj, conv_w, conv_b, dtb, alog, dskip, nb, seq):
    qc = SSD_CHUNK
    nc = seq // qc

    def body(xs_ref, bc_ref, dt_ref, cw_ref, cb_ref, dtb_ref, al_ref, ds_ref,
             y_ref, st_ref, pxs_ref, pbc_ref, state_ref):
        @pl.when(pl.program_id(1) == 0)
        def _():
            pxs_ref[...] = jnp.zeros_like(pxs_ref)
            pbc_ref[...] = jnp.zeros_like(pbc_ref)
            state_ref[...] = jnp.zeros_like(state_ref)

        expand = _head_expand()
        xs_raw = xs_ref[...]
        bc_raw = bc_ref[...]
        cw = cw_ref[...]
        cb = cb_ref[...]
        xs = _silu(_conv_pre(_conv_taps(xs_raw, pxs_ref[...]), cw[:, :D_BRANCH], cb[:, :D_BRANCH]))
        bc = _silu(_conv_pre(_conv_taps(bc_raw, pbc_ref[...]), cw[:, D_BRANCH:], cb[:, D_BRANCH:]))
        pxs_ref[...] = xs_raw
        pbc_ref[...] = bc_raw

        dt, d_a, a_cs, dt_x, acs_x, tril = _chunk_decay(
            dt_ref[...], dtb_ref[...], al_ref[...], expand, qc)
        a_cst = a_cs.T
        aend_x = acs_x[qc - 1:qc, :]
        ea_x = jnp.exp(acs_x)
        dec_x = jnp.exp(aend_x - acs_x)
        xt = xs * dt_x
        xtb = xt.astype(BF16)
        xdb = (xt * dec_x).astype(BF16)
        d_x = _dot01(jnp.broadcast_to(ds_ref[...], (8, LANES)), expand, 3)[0:1, :]
        st_ref[0, 0] = state_ref[...]

        for g in range(N_GROUPS):
            gs = slice(g * GROUP_W, (g + 1) * GROUP_W)
            bg = bc[:, g * D_STATE:(g + 1) * D_STATE]
            cg = bc[:, (N_GROUPS + g) * D_STATE:(N_GROUPS + g + 1) * D_STATE]
            bgb = bg.astype(BF16)
            cgb = cg.astype(BF16)
            cbm = _dot(cgb, bgb, _NT)
            st_in = state_ref[g]
            y_off = _dot(cgb, st_in.astype(BF16)) * ea_x[:, gs]
            for k in range(HEADS_PER_GROUP):
                h = g * HEADS_PER_GROUP + k
                hs = slice(h * HEAD_DIM, (h + 1) * HEAD_DIM)
                seg = a_cs[:, h:h + 1] - a_cst[h:h + 1, :]
                gh = cbm * jnp.exp(jnp.where(tril, seg, -1e30))
                y_h = _dot(gh.astype(BF16), xtb[:, hs]) + y_off[:, k * HEAD_DIM:(k + 1) * HEAD_DIM]
                y_ref[:, hs] = y_h + d_x[:, hs] * xs[:, hs]
            state_ref[g] = st_in * jnp.exp(aend_x[:, gs]) + _dot(bg.T.astype(BF16), xdb[:, gs])

    nblk = lambda w, off: pl.BlockSpec((qc, w), lambda b, c: (b * nc + c, off))
    full = lambda r, w: pl.BlockSpec((r, w), lambda b, c: (0, 0))
    return pl.pallas_call(
        body, name="ssd_fwd",
        grid=(nb, nc),
        in_specs=[nblk(D_BRANCH, COL_XS // D_BRANCH), nblk(D_BC, COL_BC // D_BC),
                  nblk(LANES, COL_DT // LANES),
                  full(CONV_TAPS, D_CONV), full(1, D_CONV), full(1, LANES), full(1, LANES),
                  full(1, LANES)],
        out_specs=[pl.BlockSpec((qc, D_BRANCH), lambda b, c: (b * nc + c, 0)),
                   pl.BlockSpec((1, 1, N_GROUPS, D_STATE, GROUP_W), lambda b, c: (b, c, 0, 0, 0))],
        out_shape=[jax.ShapeDtypeStruct((nb * seq, D_BRANCH), F32),
                   jax.ShapeDtypeStruct((nb, nc, N_GROUPS, D_STATE, GROUP_W), F32)],
        scratch_shapes=[pltpu.VMEM((qc, D_BRANCH), F32), pltpu.VMEM((qc, D_BC), F32),
                        pltpu.VMEM((N_GROUPS, D_STATE, GROUP_W), F32)],
        compiler_params=_params(2),
    )(proj, proj, proj, conv_w, conv_b, dtb, alog, dskip)


def _ssd_bwd(proj, d_y, states, conv_w, conv_b, dtb, alog, dskip, nb, seq, slabs=()):
    qc = SSD_CHUNK
    nc = seq // qc

    def body(xs_ref, bc_ref, dt_ref, pxs_ref, pbc_ref, dy_ref, st_ref, stn_ref,
             cw_ref, cb_ref, dtb_ref, al_ref, ds_ref,
             dx_ref, gcw_ref, gcb_ref, gdtb_ref, gal_ref, gds_ref,
             dst_ref, nxs_ref, nbc_ref, yd_ref, dxt_ref):
        step = pl.program_id(1)
        chunk = nc - 1 - step

        @pl.when(step == 0)
        def _():
            dst_ref[...] = jnp.zeros_like(dst_ref)
            nxs_ref[...] = jnp.zeros_like(nxs_ref)
            nbc_ref[...] = jnp.zeros_like(nbc_ref)

        @pl.when((pl.program_id(0) == 0) & (step == 0))
        def _():
            gcw_ref[...] = jnp.zeros_like(gcw_ref)
            gcb_ref[...] = jnp.zeros_like(gcb_ref)
            gdtb_ref[...] = jnp.zeros_like(gdtb_ref)
            gal_ref[...] = jnp.zeros_like(gal_ref)
            gds_ref[...] = jnp.zeros_like(gds_ref)

        expand = _head_expand()
        collapse = lambda v: _dot01(v, expand, 2, _NT)
        first = jnp.where(chunk == 0, 0.0, 1.0)
        xs_raw = xs_ref[...]
        bc_raw = bc_ref[...]
        pxs = pxs_ref[...] * first
        pbc = pbc_ref[...] * first
        cw = cw_ref[...]
        cb = cb_ref[...]
        taps_xs = _conv_taps(xs_raw, pxs)
        taps_bc = _conv_taps(bc_raw, pbc)
        xs, dsilu_xs = _silu_and_grad(_conv_pre(taps_xs, cw[:, :D_BRANCH], cb[:, :D_BRANCH]))
        bc, dsilu_bc = _silu_and_grad(_conv_pre(taps_bc, cw[:, D_BRANCH:], cb[:, D_BRANCH:]))

        dt_in = dt_ref[...] + dtb_ref[...]
        dt, d_a, a_cs, dt_x, acs_x, tril = _chunk_decay(
            dt_ref[...], dtb_ref[...], al_ref[...], expand, qc)
        a_cst = a_cs.T
        aend_x = acs_x[qc - 1:qc, :]
        ea_x = jnp.exp(acs_x)
        dec_x = jnp.exp(aend_x - acs_x)
        xt = xs * dt_x
        xtb = xt.astype(BF16)
        xdb = (xt * dec_x).astype(BF16)
        d_x = _dot01(jnp.broadcast_to(ds_ref[...], (8, LANES)), expand, 3)[0:1, :]

        dy = dy_ref[...]
        dyb = dy.astype(BF16)
        dyeab = (dy * ea_x).astype(BF16)
        gds_ref[...] += collapse(jnp.broadcast_to(jnp.sum(dy * xs, axis=0, keepdims=True),
                                                  (8, D_BRANCH)))[0:1, :]

        d_bc = []
        d_cc = []
        y_offs = []
        dxt_states = []
        end_terms = []
        for g in range(N_GROUPS):
            gs = slice(g * GROUP_W, (g + 1) * GROUP_W)
            bg = bc[:, g * D_STATE:(g + 1) * D_STATE]
            cg = bc[:, (N_GROUPS + g) * D_STATE:(N_GROUPS + g + 1) * D_STATE]
            bgb = bg.astype(BF16)
            cgb = cg.astype(BF16)
            cbm = _dot(cgb, bgb, _NT)
            st_in = st_ref[0, 0, g]
            st_inb = st_in.astype(BF16)
            d_st = dst_ref[g]
            d_stb = d_st.astype(BF16)
            y_offs.append(_dot(cgb, st_inb) * ea_x[:, gs])
            dxt_states.append(_dot(bgb, d_stb) * dec_x[:, gs])
            d_c = _dot(dyeab[:, gs], st_inb, _NT)
            d_b = _dot(xdb[:, gs], d_stb, _NT)
            d_cb = jnp.zeros((qc, qc), F32)
            for k in range(HEADS_PER_GROUP):
                h = g * HEADS_PER_GROUP + k
                hs = slice(h * HEAD_DIM, (h + 1) * HEAD_DIM)
                seg = a_cs[:, h:h + 1] - a_cst[h:h + 1, :]
                lh = jnp.exp(jnp.where(tril, seg, -1e30))
                ghb = (cbm * lh).astype(BF16)
                d_cb = d_cb + _dot(dyb[:, hs], xtb[:, hs], _NT) * lh
                yd_ref[:, hs] = _dot(ghb, xtb[:, hs])
                dxt_ref[:, hs] = _dot(ghb, dyb[:, hs], _TN)
            d_cbb = d_cb.astype(BF16)
            d_cc.append(d_c + _dot(d_cbb, bgb))
            d_bc.append(d_b + _dot(d_cbb, cgb, _TN))
            end_terms.append(jnp.sum(d_st * stn_ref[0, 0, g], axis=0, keepdims=True))
            dst_ref[g] = d_st * jnp.exp(aend_x[:, gs]) + _dot(cg.T.astype(BF16), dyeab[:, gs])

        y_off = jnp.concatenate(y_offs, axis=1)
        dxt_state = jnp.concatenate(dxt_states, axis=1)
        dxt = dxt_ref[...] + dxt_state
        last = jnp.where(chunk == nc - 1, 0.0, 1.0)
        end_c = collapse(jnp.broadcast_to(jnp.concatenate(end_terms, axis=1), (8, D_BRANCH)))[0:1, :]
        da_cs = collapse(dyb.astype(F32) * yd_ref[...] - dxt_ref[...] * xtb.astype(F32)
                         + dy * y_off - dxt_state * xt)
        da_cs = da_cs + jnp.where(_row_iota(da_cs.shape) == qc - 1, end_c * last, 0.0)
        triu = lax.broadcasted_iota(jnp.int32, (qc, qc), 0) <= lax.broadcasted_iota(jnp.int32, (qc, qc), 1)
        dd_a = _dot01(da_cs, jnp.where(triu, 1.0, 0.0).astype(BF16), 3, m_left=True)
        ddt = dd_a * (-jnp.exp(al_ref[...])) + collapse(dxt * xs)
        head_lanes = _lane_iota(ddt.shape) < N_HEADS
        ddt_raw = jnp.where(head_lanes, ddt * _sigmoid(dt_in), 0.0)
        gal_ref[...] += jnp.sum(jnp.where(head_lanes, dd_a * d_a, 0.0), axis=0, keepdims=True)
        gdtb_ref[...] += jnp.sum(ddt_raw, axis=0, keepdims=True)

        dpre_xs = (dxt * dt_x + d_x * dy) * dsilu_xs
        dpre_bc = jnp.concatenate(d_bc + d_cc, axis=1) * dsilu_bc
        gcb_ref[...] += jnp.concatenate([jnp.sum(dpre_xs, axis=0, keepdims=True),
                                         jnp.sum(dpre_bc, axis=0, keepdims=True)], axis=1)
        nxs = nxs_ref[...]
        nbc = nbc_ref[...]
        du_xs = jnp.zeros_like(dpre_xs)
        du_bc = jnp.zeros_like(dpre_bc)
        for i in range(CONV_TAPS):
            k = CONV_TAPS - 1 - i
            gcw_ref[i:i + 1, :] += jnp.concatenate(
                [jnp.sum(dpre_xs * taps_xs[i], axis=0, keepdims=True),
                 jnp.sum(dpre_bc * taps_bc[i], axis=0, keepdims=True)], axis=1)
            du_xs = du_xs + _shift_up(dpre_xs, nxs, k) * cw[i:i + 1, :D_BRANCH]
            du_bc = du_bc + _shift_up(dpre_bc, nbc, k) * cw[i:i + 1, D_BRANCH:]
        nxs_ref[...] = dpre_xs
        nbc_ref[...] = dpre_bc

        dx_ref[:, :D_BRANCH] = du_xs.astype(BF16)
        dx_ref[:, D_BRANCH:D_CONV] = du_bc.astype(BF16)
        dx_ref[:, D_CONV:D_CONV + LANES] = ddt_raw.astype(BF16)
        dx_ref[:, D_CONV + LANES:] = jnp.zeros((qc, 2048 - D_CONV - LANES), BF16)

    rev = lambda b, c: b * nc + (nc - 1 - c)
    prv = lambda b, c: b * nc + jnp.maximum(nc - 2 - c, 0)
    nblk = lambda w, off, f: pl.BlockSpec((qc, w), lambda b, c: (f(b, c), off))
    full = lambda r, w: pl.BlockSpec((r, w), lambda b, c: (0, 0))
    st_spec = lambda f: pl.BlockSpec((1, 1, N_GROUPS, D_STATE, GROUP_W),
                                     lambda b, c: (b, f(c), 0, 0, 0))
    return _call_with_exchange(
        body, (proj, proj, proj, proj, proj, d_y, states, states, conv_w, conv_b, dtb, alog, dskip),
        slabs, (True,) * len(slabs), name="ssd_bwd", grid=(nb, nc),
        in_specs=[nblk(D_BRANCH, COL_XS // D_BRANCH, rev), nblk(D_BC, COL_BC // D_BC, rev),
                  nblk(LANES, COL_DT // LANES, rev),
                  nblk(D_BRANCH, COL_XS // D_BRANCH, prv), nblk(D_BC, COL_BC // D_BC, prv),
                  nblk(D_BRANCH, 0, rev),
                  st_spec(lambda c: nc - 1 - c), st_spec(lambda c: jnp.minimum(nc - c, nc - 1)),
                  full(CONV_TAPS, D_CONV), full(1, D_CONV), full(1, LANES), full(1, LANES),
                  full(1, LANES)],
        out_specs=[nblk(2048, 0, rev), full(8, D_CONV), full(1, D_CONV), full(1, LANES),
                   full(1, LANES), full(1, LANES)],
        out_shape=[jax.ShapeDtypeStruct((nb * seq, 2048), BF16),
                   jax.ShapeDtypeStruct((8, D_CONV), F32), jax.ShapeDtypeStruct((1, D_CONV), F32),
                   jax.ShapeDtypeStruct((1, LANES), F32), jax.ShapeDtypeStruct((1, LANES), F32),
                   jax.ShapeDtypeStruct((1, LANES), F32)],
        scratch_shapes=[pltpu.VMEM((N_GROUPS, D_STATE, GROUP_W), F32),
                        pltpu.VMEM((qc, D_BRANCH), F32), pltpu.VMEM((qc, D_BC), F32),
                        pltpu.VMEM((qc, D_BRANCH), F32), pltpu.VMEM((qc, D_BRANCH), F32)])


def _mid(o_sb, y_ssd, proj, x2, target, sb_w, ssd_w, w_out_b, tm):
    t = x2.shape[0]
    inv_d = 1.0 / D_MODEL

    def body(o_ref, y_ref, zsb_ref, zssd_ref, x_ref, tg_ref, sbw_ref, ssdw_ref, w_ref,
             dout_ref, dosb_ref, dy_ref, dz_ref, gw_ref, gsb_ref, gssd_ref, loss_ref):
        @pl.when(pl.program_id(0) == 0)
        def _():
            gw_ref[...] = jnp.zeros_like(gw_ref)
            gsb_ref[...] = jnp.zeros_like(gsb_ref)
            gssd_ref[...] = jnp.zeros_like(gssd_ref)
            loss_ref[...] = jnp.zeros_like(loss_ref)

        def branch(val, z, w):
            gate, dgate = _silu_and_grad(z)
            g = val * gate
            r = lax.rsqrt(jnp.mean(g * g, axis=1, keepdims=True) + EPS)
            xhat = g * r
            return (gate, dgate, r, xhat), (xhat * w).astype(BF16)

        o = o_ref[...]
        y = y_ref[...]
        saved_a, mix_a = branch(o, zsb_ref[...], sbw_ref[...])
        saved_b, mix_b = branch(y, zssd_ref[...], ssdw_ref[...])
        out = x_ref[...] + _dot(mix_a, w_ref[:D_BRANCH, :]) + _dot(mix_b, w_ref[D_BRANCH:, :])
        diff = out - tg_ref[...]
        loss_ref[...] += 0.5 * inv_d * jnp.sum(diff * diff)
        d_out = diff * inv_d
        dout_ref[...] = d_out
        d_outb = d_out.astype(BF16)
        gw_ref[:D_BRANCH, :] += _dot(mix_a, d_outb, _TN)
        gw_ref[D_BRANCH:, :] += _dot(mix_b, d_outb, _TN)

        def branch_bwd(dmix, val, w, saved):
            gate, dgate, r, xhat = saved
            gg = dmix * w
            m = jnp.mean(gg * xhat, axis=1, keepdims=True)
            dg = r * (gg - xhat * m)
            return dg * gate, dg * val * dgate, jnp.sum(dmix * xhat, axis=0, keepdims=True)

        dmix_a = _dot(d_outb, w_ref[:D_BRANCH, :], _NT)
        dmix_b = _dot(d_outb, w_ref[D_BRANCH:, :], _NT)
        d_o, dz_a, gsb = branch_bwd(dmix_a, o, sbw_ref[...], saved_a)
        d_y, dz_b, gssd = branch_bwd(dmix_b, y, ssdw_ref[...], saved_b)
        dosb_ref[...] = d_o
        dy_ref[...] = d_y
        dz_ref[:, :D_BRANCH] = dz_a.astype(BF16)
        dz_ref[:, D_BRANCH:] = dz_b.astype(BF16)
        gsb_ref[...] += gsb
        gssd_ref[...] += gssd

    row = lambda w, off: pl.BlockSpec((tm, w), lambda i: (i, off))
    full = lambda r, w: pl.BlockSpec((r, w), lambda i: (0, 0))
    resident = pl.BlockSpec((2 * D_BRANCH, D_MODEL), lambda i: (0, 0), pipeline_mode=pl.Buffered(1))
    tok = jax.ShapeDtypeStruct((t, D_MODEL), F32)
    return pl.pallas_call(
        body, name="mid",
        grid=(t // tm,),
        in_specs=[row(D_BRANCH, 0), row(D_BRANCH, 0), row(D_BRANCH, 3), row(D_BRANCH, 4),
                  row(D_MODEL, 0), row(D_MODEL, 0), full(1, D_BRANCH), full(1, D_BRANCH),
                  resident],
        out_specs=[row(D_MODEL, 0), row(D_BRANCH, 0), row(D_BRANCH, 0), row(2 * D_BRANCH, 0),
                   resident, full(1, D_BRANCH), full(1, D_BRANCH),
                   full(1, LANES)],
        out_shape=[tok, tok, tok, jax.ShapeDtypeStruct((t, 2 * D_BRANCH), BF16),
                   jax.ShapeDtypeStruct((2 * D_BRANCH, D_MODEL), F32),
                   jax.ShapeDtypeStruct((1, D_BRANCH), F32), jax.ShapeDtypeStruct((1, D_BRANCH), F32),
                   jax.ShapeDtypeStruct((1, LANES), F32)],
        compiler_params=_params(1),
    )(o_sb, y_ssd, proj, proj, x2, target, sb_w, ssd_w, w_out_b)


_DPROJ_FIRST = (0, 1, 2, 3, 5)
_DPROJ_BLOCKS = (1, 1, 1, 2, 2)
_DPROJ_OWNER = (0, 1, 2, 3, 3, 4, 4)


def _dproj_col(j, p):
    return jnp.clip(j - _DPROJ_FIRST[p], 0, _DPROJ_BLOCKS[p] - 1)


def _in_proj_bwd_x(d_parts, w_in_b, x2, d_out, norm_w, tm, slabs=()):
    t = x2.shape[0]
    n_parts = len(d_parts)

    def body(*refs):
        dp_refs = refs[:n_parts]
        w_ref, x_ref, dout_ref, nw_ref, gx_ref, gnw_ref, acc_ref = refs[n_parts:]
        i, j = pl.program_id(0), pl.program_id(1)

        @pl.when((i == 0) & (j == 0))
        def _():
            gnw_ref[...] = jnp.zeros_like(gnw_ref)

        @pl.when(j == 0)
        def _():
            acc_ref[...] = jnp.zeros_like(acc_ref)

        for jj in range(N_COLBLK):
            @pl.when(j == jj)
            def _(jj=jj):
                acc_ref[...] += _dot(dp_refs[_DPROJ_OWNER[jj]][...], w_ref[...], _NT)

        @pl.when(j == N_COLBLK - 1)
        def _():
            xf = x_ref[...]
            d_hn = acc_ref[...]
            r = lax.rsqrt(jnp.mean(xf * xf, axis=1, keepdims=True) + EPS)
            xhat = xf * r
            g = d_hn * nw_ref[...]
            m = jnp.mean(g * xhat, axis=1, keepdims=True)
            gx_ref[...] = dout_ref[...] + r * (g - xhat * m)
            gnw_ref[...] += jnp.sum(d_hn * xhat, axis=0, keepdims=True)

    return _call_with_exchange(
        body, (*d_parts, w_in_b, x2, d_out, norm_w), slabs, (True,) * len(slabs),
        name="in_proj_bwd_x", grid=(t // tm, N_COLBLK),
        in_specs=[pl.BlockSpec((tm, 1024), lambda i, j, p=p: (i, _dproj_col(j, p)))
                  for p in range(n_parts)] + [
                  pl.BlockSpec((D_MODEL, 1024), lambda i, j: (0, j)),
                  pl.BlockSpec((tm, D_MODEL), lambda i, j: (i, 0)),
                  pl.BlockSpec((tm, D_MODEL), lambda i, j: (i, 0)),
                  pl.BlockSpec((1, D_MODEL), lambda i, j: (0, 0))],
        out_specs=[pl.BlockSpec((tm, D_MODEL), lambda i, j: (i, 0)),
                   pl.BlockSpec((1, D_MODEL), lambda i, j: (0, 0))],
        out_shape=[jax.ShapeDtypeStruct((t, D_MODEL), F32), jax.ShapeDtypeStruct((1, D_MODEL), F32)],
        scratch_shapes=[pltpu.VMEM((tm, D_MODEL), F32)])


def _in_proj_bwd_w(hn, d_parts, tm):
    t = hn.shape[0]
    n_parts = len(d_parts)

    def body(hn_ref, *refs):
        dp_refs, gw_ref = refs[:n_parts], refs[n_parts]
        j = pl.program_id(0)

        @pl.when(pl.program_id(1) == 0)
        def _():
            gw_ref[...] = jnp.zeros_like(gw_ref)

        for jj in range(N_COLBLK):
            @pl.when(j == jj)
            def _(jj=jj):
                gw_ref[...] += _dot(hn_ref[...], dp_refs[_DPROJ_OWNER[jj]][...], _TN)

    def part_spec(p):
        def index(j, i):
            mine = (j >= _DPROJ_FIRST[p]) & (j < _DPROJ_FIRST[p] + _DPROJ_BLOCKS[p])
            return jnp.where(mine, i, 0), _dproj_col(j, p)
        return pl.BlockSpec((tm, 1024), index)

    return pl.pallas_call(
        body, name="in_proj_bwd_w",
        grid=(N_COLBLK, t // tm),
        in_specs=[pl.BlockSpec((tm, D_MODEL), lambda j, i: (i, 0))]
                 + [part_spec(p) for p in range(n_parts)],
        out_specs=pl.BlockSpec((D_MODEL, 1024), lambda j, i: (0, j)),
        out_shape=jax.ShapeDtypeStruct((D_MODEL, D_IN_PAD), F32),
        compiler_params=_params(2),
    )(hn, *d_parts)


def _adamw(parts, w, m, v, tr, name):
    _, rows, cols = w.shape
    c1 = 1.0 - ADAM_B1 ** ADAM_STEP
    c2 = 1.0 - ADAM_B2 ** ADAM_STEP

    def body(p_ref, w_ref, m_ref, v_ref, g_ref, d_ref, nm_ref, nv_ref):
        g = p_ref[0].astype(F32)
        for s in range(1, N_DEV):
            g = g + p_ref[s].astype(F32)
        nm = ADAM_B1 * m_ref[0] + (1.0 - ADAM_B1) * g
        nv = ADAM_B2 * v_ref[0] + (1.0 - ADAM_B2) * (g * g)
        g_ref[0] = g
        nm_ref[0] = nm
        nv_ref[0] = nv
        d_ref[0] = -ADAM_LR * ((nm / c1) / (jnp.sqrt(nv / c2) + ADAM_EPS) + ADAM_WD * w_ref[0])

    blk = pl.BlockSpec((1, tr, cols), lambda i: (0, i, 0))
    shape = jax.ShapeDtypeStruct((1, rows, cols), F32)
    return pl.pallas_call(
        body, name=name,
        grid=(rows // tr,),
        in_specs=[pl.BlockSpec((N_DEV, tr, cols), lambda i: (0, i, 0)), blk, blk, blk],
        out_specs=[blk, blk, blk, blk],
        out_shape=[shape, shape, shape, shape],
        compiler_params=_params(1),
    )(parts, w, m, v)


def _mesh_place():
    x, y, c = lax.axis_index("x"), lax.axis_index("y"), lax.axis_index("c")
    return x, y, c, 4 * x + 2 * y + c


def _peer(x, y, c, k):
    px = 1 - x if k & 4 else x
    py = 1 - y if k & 2 else y
    pc = 1 - c if k & 1 else c
    return (px, py, pc), 4 * px + 2 * py + pc


def _exchange(srcs, scatter, name):
    n = len(srcs)

    def body(*refs):
        copies = _exchange_copies(refs[:n], refs[n:2 * n], scatter, *refs[2 * n:])
        _exchange_start(copies)
        _exchange_wait(copies)

    return pl.pallas_call(
        body, name=name,
        in_specs=[_ANY] * n, out_specs=[_ANY] * n, out_shape=_exchange_shapes(srcs, scatter),
        scratch_shapes=_exchange_sems(n),
    )(*srcs)


def _call_with_exchange(body, operands, srcs, scatter, *, name, grid, in_specs, out_specs,
                        out_shape, scratch_shapes=()):
    n_in, n_out, n_scr, n_x = len(in_specs), len(out_specs), len(scratch_shapes), len(srcs)
    params = _params(len(grid))
    if not n_x:
        return pl.pallas_call(body, name=name, grid=grid, in_specs=list(in_specs),
                              out_specs=list(out_specs), out_shape=list(out_shape),
                              scratch_shapes=list(scratch_shapes), compiler_params=params)(*operands)

    def wrapped(*refs):
        ins, refs = refs[:n_in], refs[n_in:]
        x_src, refs = refs[:n_x], refs[n_x:]
        outs, refs = refs[:n_out], refs[n_out:]
        x_dst, refs = refs[:n_x], refs[n_x:]
        scratch, sems = refs[:n_scr], refs[n_scr:]
        ids = [pl.program_id(a) for a in range(len(grid))]
        first = functools.reduce(jnp.logical_and, [i == 0 for i in ids])
        last = functools.reduce(jnp.logical_and, [i == n - 1 for i, n in zip(ids, grid)])

        @pl.when(first)
        def _():
            _exchange_start(_exchange_copies(x_src, x_dst, scatter, *sems))

        body(*ins, *outs, *scratch)

        @pl.when(last)
        def _():
            _exchange_wait(_exchange_copies(x_src, x_dst, scatter, *sems))

    return pl.pallas_call(
        wrapped, name=name, grid=grid,
        in_specs=list(in_specs) + [_ANY] * n_x, out_specs=list(out_specs) + [_ANY] * n_x,
        out_shape=list(out_shape) + _exchange_shapes(srcs, scatter),
        scratch_shapes=list(scratch_shapes) + _exchange_sems(n_x), compiler_params=params,
    )(*operands, *srcs)


def _gather_two_level(shard, name):
    def body(x_ref, out_ref, send_sems, recv_sems, local_sem):
        x, y, c, me = _mesh_place()
        sibling = (x, y, 1 - c)
        chips = [(1 - x, y), (x, 1 - y), (1 - x, 1 - y)]

        def slab(px, py, pc):
            return out_ref.at[4 * px + 2 * py + pc]

        def copy(k, block, to, src=None):
            return pltpu.make_async_remote_copy(
                src_ref=slab(*block) if src is None else src, dst_ref=slab(*block),
                send_sem=send_sems.at[k], recv_sem=recv_sems.at[k],
                device_id=to, device_id_type=pl.DeviceIdType.MESH)

        mine = pltpu.make_async_copy(x_ref, slab(x, y, c), local_sem)
        mine.start()
        first = [copy(0, (x, y, c), sibling, src=x_ref)]
        first += [copy(1 + j, (x, y, c), (*chip, c), src=x_ref) for j, chip in enumerate(chips)]
        for cp in first:
            cp.start()
        passed = [copy(4 + j, (*chip, c), sibling) for j, chip in enumerate(chips)]
        for j, chip in enumerate(chips):
            copy(1 + j, (*chip, c), (x, y, c)).wait_recv()
            passed[j].start()
        copy(0, sibling, (x, y, c)).wait_recv()
        for j, chip in enumerate(chips):
            copy(4 + j, (*chip, 1 - c), (x, y, c)).wait_recv()
        for cp in first + passed:
            cp.wait_send()
        mine.wait()

    return pl.pallas_call(
        body, name=name,
        in_specs=[_ANY], out_specs=_ANY,
        out_shape=jax.ShapeDtypeStruct((N_DEV,) + shard.shape, shard.dtype),
        scratch_shapes=[pltpu.SemaphoreType.DMA((N_DEV - 1,)), pltpu.SemaphoreType.DMA((N_DEV - 1,)),
                        pltpu.SemaphoreType.DMA],
    )(shard)


_ANY = pl.BlockSpec(memory_space=pl.ANY)


def _exchange_shapes(srcs, scatter):
    return [jax.ShapeDtypeStruct(s.shape if sc else (N_DEV,) + s.shape, s.dtype)
            for s, sc in zip(srcs, scatter)]


def _exchange_sems(n):
    return [pltpu.SemaphoreType.DMA((n * (N_DEV - 1),)),
            pltpu.SemaphoreType.DMA((n * (N_DEV - 1),)),
            pltpu.SemaphoreType.DMA((n,))]


def _exchange_copies(src_refs, dst_refs, scatter, send_sems, recv_sems, loc_sems):
    n = len(src_refs)
    x, y, c, me = _mesh_place()

    def src_of(i, idx):
        return src_refs[i].at[idx] if scatter[i] else src_refs[i]

    local = [pltpu.make_async_copy(src_of(i, me), dst_refs[i].at[me], loc_sems.at[i])
             for i in range(n)]
    sends, recvs = [], []
    for k in range(1, N_DEV):
        peer, pidx = _peer(x, y, c, k)
        for i in range(n):
            s = i * (N_DEV - 1) + k - 1
            for dst_slab, group in ((me, sends), (pidx, recvs)):
                group.append(pltpu.make_async_remote_copy(
                    src_ref=src_of(i, pidx), dst_ref=dst_refs[i].at[dst_slab],
                    send_sem=send_sems.at[s], recv_sem=recv_sems.at[s],
                    device_id=peer, device_id_type=pl.DeviceIdType.MESH))
    return local, sends, recvs


def _exchange_start(copies):
    local, sends, _ = copies
    for cp in local + sends:
        cp.start()


def _exchange_wait(copies):
    local, sends, recvs = copies
    for cp in recvs:
        cp.wait_recv()
    for cp in sends:
        cp.wait_send()
    for cp in local:
        cp.wait()


def _pad_lanes(v, width=LANES):
    return jnp.pad(v, ((0, 0), (0, width - v.shape[1])))


def _local_step(x, target, norm_w, w_in_b, q_norm_w, k_norm_w, conv_w, conv_b, dt_bias, a_log,
                d_skip, sb_norm_w, ssd_norm_w, w_out_b, tm=512, tq=512, tmid=256, blk=ATT_BLK,
                scatter=False):
    nb, seq, _ = x.shape
    t = nb * seq
    x2 = x.reshape(t, D_MODEL)
    tg2 = target.reshape(t, D_MODEL)
    qw2 = jnp.tile(q_norm_w, (1, 2))
    kw2 = jnp.tile(k_norm_w, (1, 2))
    dtb, alog, dsk = _pad_lanes(dt_bias), _pad_lanes(a_log), _pad_lanes(d_skip)

    tproj = min(2 * tm, t)
    if scatter:
        proj, hn, wout_all, cw_all = _in_proj(x2, norm_w, w_in_b, tproj, (w_out_b, conv_w))
        w_out_b = wout_all.reshape(2 * D_BRANCH, D_MODEL)
        conv_w = jnp.transpose(cw_all, (1, 0, 2)).reshape(CONV_TAPS, D_CONV)
    else:
        proj, hn = _in_proj(x2, norm_w, w_in_b, tproj)
    qs, kn, vb, kt = _qk_prep(proj, qw2, kw2, nb, seq, tq)
    o_sb, sb_tot, sb_low = _attn_fwd(qs, kn, vb, nb, seq, blk)
    y_ssd, states = _ssd_fwd(proj, conv_w, conv_b, dtb, alog, dsk, nb, seq)
    d_out, d_osb, d_y, d_z, g_wout, g_sbw, g_ssdw, loss = _mid(
        o_sb, y_ssd, proj, x2, tg2, sb_norm_w, ssd_norm_w, w_out_b, tmid)
    dqs, dkn, dvh = _attn_bwd(qs, kn, kt, vb, sb_tot, sb_low, d_osb, nb, seq, blk)
    dq_raw, dk_raw, dv_raw, g_qw, g_kw = _qk_bwd(proj, dqs, dkn, dvh, qw2, kw2, nb, seq, tq)
    wout_slabs = (g_wout.reshape(N_DEV, 2 * D_BRANCH // N_DEV, D_MODEL).astype(BF16),)
    d_xbc, g_cw, g_cb, g_dtb, g_alog, g_dsk, *moved = _ssd_bwd(
        proj, d_y, states, conv_w, conv_b, dtb, alog, dsk, nb, seq, wout_slabs if scatter else ())
    d_parts = [dq_raw, dk_raw, dv_raw, d_z, d_xbc]
    g_win = _in_proj_bwd_w(hn, d_parts, tm)[:, :D_IN]
    g_cw = g_cw[:CONV_TAPS]
    if scatter:
        g_wout, = moved
        grad_x, g_nw, g_win, g_cw = _in_proj_bwd_x(
            d_parts, w_in_b, x2, d_out, norm_w, tm, _grad_slabs(g_win, g_cw))
    else:
        grad_x, g_nw = _in_proj_bwd_x(d_parts, w_in_b, x2, d_out, norm_w, tm)

    small = dict(
        norm_w=g_nw,
        q_norm_w=g_qw[:, :HEAD_DIM] + g_qw[:, HEAD_DIM:],
        k_norm_w=g_kw[:, :HEAD_DIM] + g_kw[:, HEAD_DIM:],
        conv_b=g_cb, dt_bias=g_dtb[:, :N_HEADS], A_log=g_alog[:, :N_HEADS],
        D_skip=g_dsk[:, :N_HEADS], sb_norm_w=g_sbw, ssd_norm_w=g_ssdw)
    return loss[0, 0], grad_x.reshape(nb, seq, D_MODEL), g_win, g_wout, g_cw, small


def _grad_slabs(g_win, g_cw):
    w_sh = D_IN // N_DEV
    c_sh = D_CONV // N_DEV
    return (jnp.transpose(g_win.reshape(D_MODEL, N_DEV, w_sh), (1, 0, 2)).astype(BF16),
            jnp.pad(jnp.transpose(g_cw.reshape(CONV_TAPS, N_DEV, c_sh), (1, 0, 2)),
                    ((0, 0), (0, 8 - CONV_TAPS), (0, 0))))


_SMALL = ("norm_w", "q_norm_w", "k_norm_w", "conv_b", "dt_bias", "A_log", "D_skip",
          "sb_norm_w", "ssd_norm_w")


def _pack_small(vals):
    flat = jnp.concatenate([_pad_lanes(vals[n], -(-vals[n].shape[1] // LANES) * LANES)
                            for n in _SMALL], axis=1)
    return jnp.pad(flat, ((0, 0), (0, 48 * LANES - flat.shape[1]))).reshape(48, LANES)


def _unpack_small(packed, like):
    out, r = {}, 0
    for n in _SMALL:
        width = like[n].shape[1]
        nr = -(-width // LANES)
        out[n] = packed[r:r + nr].reshape(1, nr * LANES)[:, :width]
        r += nr
    return out


def kernel(x, norm_w, w_in, q_norm_w, k_norm_w, conv_w, conv_b, dt_bias, A_log, D_skip, sb_norm_w, ssd_norm_w, w_out, loss_target, m_norm_w, m_w_in, m_q_norm_w, m_k_norm_w, m_conv_w, m_conv_b, m_dt_bias, m_A_log, m_D_skip, m_sb_norm_w, m_ssd_norm_w, m_w_out, v_norm_w, v_w_in, v_q_norm_w, v_k_norm_w, v_conv_w, v_conv_b, v_dt_bias, v_A_log, v_D_skip, v_sb_norm_w, v_ssd_norm_w, v_w_out):
    win_all = _gather_two_level(w_in[0].astype(BF16), "gather_w_in")
    w_in_b = jnp.pad(jnp.transpose(win_all, (1, 0, 2)).reshape(D_MODEL, D_IN),
                     ((0, 0), (0, D_IN_PAD - D_IN)))

    loss, grad_x, win_parts, wout_parts, cw_parts, g_small = _local_step(
        x, loss_target, norm_w, w_in_b, q_norm_w, k_norm_w, conv_w[0], conv_b, dt_bias, A_log,
        D_skip, sb_norm_w, ssd_norm_w, w_out[0].astype(BF16), scatter=True)
    small_parts, = _exchange([_pack_small(g_small)], [False], "gather_small_grads")

    small_w = dict(norm_w=norm_w, q_norm_w=q_norm_w, k_norm_w=k_norm_w, conv_b=conv_b,
                   dt_bias=dt_bias, A_log=A_log, D_skip=D_skip, sb_norm_w=sb_norm_w,
                   ssd_norm_w=ssd_norm_w)
    small_m = dict(norm_w=m_norm_w, q_norm_w=m_q_norm_w, k_norm_w=m_k_norm_w, conv_b=m_conv_b,
                   dt_bias=m_dt_bias, A_log=m_A_log, D_skip=m_D_skip, sb_norm_w=m_sb_norm_w,
                   ssd_norm_w=m_ssd_norm_w)
    small_v = dict(norm_w=v_norm_w, q_norm_w=v_q_norm_w, k_norm_w=v_k_norm_w, conv_b=v_conv_b,
                   dt_bias=v_dt_bias, A_log=v_A_log, D_skip=v_D_skip, sb_norm_w=v_sb_norm_w,
                   ssd_norm_w=v_ssd_norm_w)

    pad8 = lambda a: jnp.pad(a, ((0, 0), (0, 8 - CONV_TAPS), (0, 0)))
    r_win = _adamw(win_parts, w_in, m_w_in, v_w_in, 128, "adamw_w_in")
    r_wout = _adamw(wout_parts, w_out, m_w_out, v_w_out, 128, "adamw_w_out")
    r_cw = _adamw(cw_parts, pad8(conv_w), pad8(m_conv_w), pad8(v_conv_w), 8, "adamw_conv_w")
    r_small = _adamw(small_parts, _pack_small(small_w)[None], _pack_small(small_m)[None],
                     _pack_small(small_v)[None], 48, "adamw_small")

    loss = lax.psum(loss, ("x", "y", "c"))
    res = {"w_in": r_win, "w_out": r_wout, "conv_w": [a[:, :CONV_TAPS] for a in r_cw]}
    unpacked = [_unpack_small(a[0], small_w) for a in r_small]
    for n in _SMALL:
        res[n] = [u[n] for u in unpacked]
    order = ("norm_w", "w_in", "q_norm_w", "k_norm_w", "conv_w", "conv_b", "dt_bias", "A_log",
             "D_skip", "sb_norm_w", "ssd_norm_w", "w_out")
    outs = [loss, grad_x]
    for kind in range(4):
        outs += [res[n][kind] for n in order]
    return tuple(outs)
```

```python
import functools
import math

import jax
import jax.numpy as jnp
from jax import lax
from jax.experimental import pallas as pl
from jax.experimental.pallas import tpu as pltpu

F32 = jnp.float32
BF16 = jnp.bfloat16

D_MODEL = 1024
N_HEADS = 16
HEAD_DIM = 64
N_PAIRS = N_HEADS // 2
D_BRANCH = 1024
N_GROUPS = 2
HEADS_PER_GROUP = 8
D_STATE = 128
GROUP_W = HEADS_PER_GROUP * HEAD_DIM
D_BC = 2 * N_GROUPS * D_STATE
D_CONV = D_BRANCH + D_BC
D_IN = 6672
D_IN_PAD = 7168
N_COLBLK = D_IN_PAD // 1024
COL_XS = 5120
COL_BC = 6144
COL_DT = 6656
EPS = 1e-6
CONV_TAPS = 4
N_DEV = 8

LANES = 128
SSD_CHUNK = 128
ATT_BLK = 256
EXP_UNDERFLOW = -105.0
VMEM_LIMIT = 56 * 1024 * 1024

ADAM_LR = 0.001
ADAM_B1 = 0.9
ADAM_B2 = 0.999
ADAM_EPS = 1e-08
ADAM_WD = 0.01
ADAM_STEP = 10

_NT = (((1,), (1,)), ((), ()))
_TN = (((0,), (0,)), ((), ()))


def _params(n_grid):
    return pltpu.CompilerParams(dimension_semantics=("arbitrary",) * n_grid,
                                vmem_limit_bytes=VMEM_LIMIT)


def _dot(a, b, dims=None, precision=None):
    if dims is None:
        return jnp.dot(a, b, preferred_element_type=F32, precision=precision)
    return lax.dot_general(a, b, dims, preferred_element_type=F32, precision=precision)


def _sigmoid(x):
    return 1.0 / (1.0 + jnp.exp(-x))


def _softplus(x):
    return jnp.maximum(x, 0.0) + jnp.log(1.0 + jnp.exp(-jnp.abs(x)))


def _split_bf16(x):
    hi = x.astype(BF16)
    lo = (x - hi.astype(F32)).astype(BF16)
    return hi, lo


def _lane_iota(shape):
    return lax.broadcasted_iota(jnp.int32, shape, len(shape) - 1)


def _row_iota(shape):
    return lax.broadcasted_iota(jnp.int32, shape, len(shape) - 2)


def _pair_sum(x):
    r = lax.broadcasted_iota(jnp.int32, (LANES, LANES), 0)
    c = lax.broadcasted_iota(jnp.int32, (LANES, LANES), 1)
    same_head = jnp.where(r // HEAD_DIM == c // HEAD_DIM, 1.0, 0.0).astype(BF16)
    hi, lo = _split_bf16(x)
    return _dot(hi, same_head) + _dot(lo, same_head)


def _pair_head(x, a):
    lane = _lane_iota(x.shape)
    mine = (lane < HEAD_DIM) if a == 0 else (lane >= HEAD_DIM)
    return jnp.where(mine, x, jnp.zeros_like(x))


def _head_expand():
    r = lax.broadcasted_iota(jnp.int32, (LANES, D_BRANCH), 0)
    c = lax.broadcasted_iota(jnp.int32, (LANES, D_BRANCH), 1)
    return jnp.where(c // HEAD_DIM == r, 1.0, 0.0).astype(BF16)


def _in_proj(x2, norm_w, w_in_b, tm, shards=()):
    t = x2.shape[0]

    def body(x_ref, nw_ref, w_ref, proj_ref, hn_ref):
        xf = x_ref[...]
        r = lax.rsqrt(jnp.mean(xf * xf, axis=1, keepdims=True) + EPS)
        hn = (xf * r * nw_ref[...]).astype(BF16)
        hn_ref[...] = hn
        for j in range(N_COLBLK):
            cols = slice(j * 1024, (j + 1) * 1024)
            proj_ref[:, cols] = _dot(hn, w_ref[:, cols])

    return _call_with_exchange(
        body, (x2, norm_w, w_in_b), shards, (False,) * len(shards), name="in_proj",
        grid=(t // tm,),
        in_specs=[pl.BlockSpec((tm, D_MODEL), lambda i: (i, 0)),
                  pl.BlockSpec((1, D_MODEL), lambda i: (0, 0)),
                  pl.BlockSpec((D_MODEL, D_IN_PAD), lambda i: (0, 0), pipeline_mode=pl.Buffered(1))],
        out_specs=[pl.BlockSpec((tm, D_IN_PAD), lambda i: (i, 0)),
                   pl.BlockSpec((tm, D_MODEL), lambda i: (i, 0))],
        out_shape=[jax.ShapeDtypeStruct((t, D_IN_PAD), F32),
                   jax.ShapeDtypeStruct((t, D_MODEL), BF16)])


def _qk_prep(proj, qw2, kw2, nb, seq, tq):
    nl = seq // tq
    scale = 1.0 / math.sqrt(HEAD_DIM)

    def body(q_ref, k_ref, v_ref, qw_ref, kw_ref, qs_ref, kn_ref, vb_ref, kt_ref):
        def norm(x, w):
            r = lax.rsqrt(_pair_sum(x * x) * (1.0 / HEAD_DIM) + EPS)
            return x * r * w

        vb_ref[...] = v_ref[...].astype(BF16)
        for p in range(N_PAIRS):
            cols = slice(p * LANES, (p + 1) * LANES)
            kn = norm(k_ref[:, cols], kw_ref[...])
            qs_ref[:, cols] = (norm(q_ref[:, cols], qw_ref[...]) * scale).astype(BF16)
            kn_ref[:, cols] = kn.astype(BF16)
            kt_ref[0, p] = kn.T.astype(BF16)

    tok_shape = jax.ShapeDtypeStruct((nb * seq, D_BRANCH), BF16)
    tok = lambda blk: pl.BlockSpec((tq, D_BRANCH), lambda b, i: (b * nl + i, blk))
    vec = pl.BlockSpec((1, LANES), lambda b, i: (0, 0))
    return pl.pallas_call(
        body, name="qk_prep",
        grid=(nb, nl),
        in_specs=[tok(0), tok(1), tok(2), vec, vec],
        out_specs=[tok(0), tok(0), tok(0),
                   pl.BlockSpec((1, N_PAIRS, LANES, tq), lambda b, i: (b, 0, 0, i))],
        out_shape=[tok_shape, tok_shape, tok_shape,
                   jax.ShapeDtypeStruct((nb, N_PAIRS, LANES, seq), BF16)],
        compiler_params=_params(2),
    )(proj, proj, proj, qw2, kw2)


def _attn_fwd(qs, kn, vb, nb, seq, blk):
    nq = seq // blk

    def body(q_ref, k_ref, v_ref, o_ref, tot_ref, low_ref, kmax_ref):
        qi = pl.program_id(2)
        r_i = lax.broadcasted_iota(jnp.int32, (blk, blk), 0)
        c_i = lax.broadcasted_iota(jnp.int32, (blk, blk), 1)
        csum = jnp.where(r_i >= c_i, 1.0, 0.0).astype(BF16)
        causal = c_i < r_i
        heads = range(2)

        head = _pair_head

        @pl.when(qi == 0)
        def _():
            kk = k_ref[...].astype(F32)
            for a in heads:
                ksq = jnp.sum(head(kk * kk, a), axis=1, keepdims=True)
                kmax_ref[a] = jnp.full((8, LANES), jnp.max(ksq))

        q_pair = q_ref[...]
        qf = q_pair.astype(F32)
        q_head = [head(q_pair, a) for a in heads]
        zmax = []
        for a in heads:
            qsq = jnp.sum(head(qf * qf, a), axis=1, keepdims=True)
            zmax.append(1.01 * jnp.sqrt(qsq * kmax_ref[a][0:1, 0:1]) + 0.01)

        def exhausted(run):
            top = jnp.maximum(jnp.max(run[0] + zmax[0]), jnp.max(run[1] + zmax[1]))
            return top < EXP_UNDERFLOW

        def sweep(blocks, run, acc):
            offs = [pl.multiple_of(j * blk, blk) for j, _, _ in blocks]
            z = [[_dot(q_head[a], k_ref[pl.ds(off, blk), :], _NT) for a in heads]
                 for off in offs]
            cl = []
            for (_, diag, valid), zb in zip(blocks, z):
                lkb = []
                for a in heads:
                    lk = -_softplus(zb[a])
                    if diag:
                        lk = jnp.where(causal, lk, 0.0)
                    if valid is not None:
                        lk = jnp.where(valid, lk, 0.0)
                    lkb.append(lk.astype(BF16))
                cl.append([_dot(lkb[a], csum) for a in heads])
            for (_, diag, valid), zb, clb, off in zip(blocks, z, cl, offs):
                w = []
                for a in heads:
                    wa = jnp.exp(zb[a] + clb[a] + run[a])
                    if diag:
                        wa = jnp.where(causal, wa, 0.0)
                    if valid is not None:
                        wa = jnp.where(valid, wa, 0.0)
                    w.append(wa.astype(BF16))
                run = [run[a] + clb[a][:, 0:1] for a in heads]
                v_pair = v_ref[pl.ds(off, blk), :]
                acc = acc + _dot(w[0], head(v_pair, 0)) + _dot(w[1], head(v_pair, 1))
            return run, acc

        run = [jnp.zeros((blk, 1), F32)] * 2
        acc = jnp.zeros((blk, LANES), F32)
        run, acc = sweep([(qi, True, None), (jnp.maximum(qi - 1, 0), False, qi >= 1)], run, acc)
        low = jnp.maximum(qi - 1, 0)

        def more(carry):
            low, done, _, _ = carry
            return (low > 0) & jnp.logical_not(done)

        def pair(carry):
            low, _, run, acc = carry
            run, acc = sweep([(low - 1, False, None), (jnp.maximum(low - 2, 0), False, low >= 2)],
                             run, acc)
            return jnp.maximum(low - 2, 0), exhausted(run), run, acc

        low, _, run, acc = lax.while_loop(more, pair, (low, exhausted(run), run, acc))
        low_ref[pl.program_id(0) * N_PAIRS + pl.program_id(1), qi] = low.astype(F32)
        o_ref[...] = acc
        for a in heads:
            as_row = jnp.sum(jnp.where(r_i == c_i, run[a], 0.0), axis=0, keepdims=True)
            tot_ref[0, a, 0] = jnp.broadcast_to(as_row, (8, blk))

    return pl.pallas_call(
        body, name="sb_attn_fwd",
        grid=(nb, N_PAIRS, nq),
        in_specs=[pl.BlockSpec((blk, LANES), lambda b, h, i: (b * nq + i, h)),
                  pl.BlockSpec((seq, LANES), lambda b, h, i: (b, h)),
                  pl.BlockSpec((seq, LANES), lambda b, h, i: (b, h))],
        out_specs=[pl.BlockSpec((blk, LANES), lambda b, h, i: (b * nq + i, h)),
                   pl.BlockSpec((1, 2, 1, 8, blk), lambda b, h, i: (b, h, i, 0, 0)),
                   pl.BlockSpec(memory_space=pltpu.SMEM)],
        out_shape=[jax.ShapeDtypeStruct((nb * seq, D_BRANCH), F32),
                   jax.ShapeDtypeStruct((nb, N_HEADS, nq, 8, blk), F32),
                   jax.ShapeDtypeStruct((nb * N_PAIRS, nq), F32)],
        scratch_shapes=[pltpu.VMEM((2, 8, LANES), F32)],
        compiler_params=_params(3),
    )(qs, kn, vb)


def _attn_bwd(qs, kn, kt, vb, tot, low, d_o, nb, seq, blk):
    nq = seq // blk

    def body(q_ref, k_ref, kt_ref, v_ref, tot_ref, low_ref, do_ref, dq_ref, dk_ref, dv_ref):
        qi = pl.program_id(2)

        @pl.when(qi == 0)
        def _():
            dk_ref[...] = jnp.zeros_like(dk_ref)
            dv_ref[...] = jnp.zeros_like(dv_ref)

        r_i = lax.broadcasted_iota(jnp.int32, (blk, blk), 0)
        c_i = lax.broadcasted_iota(jnp.int32, (blk, blk), 1)
        before = jnp.where(c_i < r_i, 1.0, 0.0).astype(BF16)
        upto = jnp.where(c_i <= r_i, 1.0, 0.0).astype(BF16)
        causal = r_i < c_i

        heads = range(2)
        q_head = [_pair_head(q_ref[...], a) for a in heads]
        d_ob = [_pair_head(do_ref[...].astype(BF16), a) for a in heads]
        total = [tot_ref[0, a, 0][0:1, :] for a in heads]

        def sweep(blocks, lsum, esum, dqt):
            def keep(x, diag, valid):
                if diag:
                    x = jnp.where(causal, x, 0.0)
                if valid is not None:
                    x = jnp.where(valid, x, 0.0)
                return x

            offs = [pl.multiple_of(j * blk, blk) for j, _, _ in blocks]
            zt = [[_dot(k_ref[pl.ds(off, blk), :], q_head[a], _NT) for a in heads]
                  for off in offs]
            dwt = [[_dot(v_ref[pl.ds(off, blk), :], d_ob[a], _NT) for a in heads]
                   for off in offs]
            sp, lk, lpre = [], [], []
            for (_, diag, valid), ztb in zip(blocks, zt):
                sp.append([_softplus(ztb[a]) for a in heads])
                lk.append([keep(-sp[-1][a], diag, valid).astype(BF16) for a in heads])
                lpre.append([_dot(before, lk[-1][a]) for a in heads])
            wt, et, epre = [], [], []
            for i, (_, diag, valid) in enumerate(blocks):
                wt.append([keep(jnp.exp(zt[i][a] + (total[a] - lsum[a] - lpre[i][a])), diag, valid)
                           for a in heads])
                et.append([dwt[i][a] * wt[i][a] for a in heads])
                split = [_split_bf16(et[i][a]) for a in heads]
                epre.append([_dot(upto, split[a][0]) + _dot(upto, split[a][1]) for a in heads])
                lsum = [lsum[a] + lpre[i][a][blk - 1:blk, :] + lk[i][a][blk - 1:blk, :]
                        for a in heads]
            for i, (_, diag, valid) in enumerate(blocks):
                dzb = [keep(et[i][a] - jnp.exp(zt[i][a] - sp[i][a]) * (esum[a] + epre[i][a]),
                            diag, valid).astype(BF16) for a in heads]
                esum = [esum[a] + epre[i][a][blk - 1:blk, :] for a in heads]
                dk_ref[pl.ds(offs[i], blk), :] += (_dot(dzb[0], q_head[0]) + _dot(dzb[1], q_head[1]))
                dv_ref[pl.ds(offs[i], blk), :] += (_dot(wt[i][0].astype(BF16), d_ob[0])
                                                   + _dot(wt[i][1].astype(BF16), d_ob[1]))
                kt_pair = kt_ref[0, 0, :, pl.ds(offs[i], blk)]
                dqt = [dqt[a] + _dot(kt_pair, dzb[a]) for a in heads]
            return lsum, esum, dqt

        row = [jnp.zeros((1, blk), F32)] * 2
        dqt = [jnp.zeros((LANES, blk), F32)] * 2
        low = low_ref[pl.program_id(0) * N_PAIRS + pl.program_id(1), qi].astype(jnp.int32)
        low = jnp.clip(low, 0, jnp.maximum(qi - 1, 0))

        def pair(carry):
            j, lsum, esum, dqt = carry
            return (j + 2,) + sweep([(j, False, None), (j + 1, False, j + 1 < qi - 1)],
                                    lsum, esum, dqt)

        _, lsum, esum, dqt = lax.while_loop(lambda c: c[0] < qi - 1, pair, (low, row, row, dqt))
        _, _, dqt = sweep([(jnp.maximum(qi - 1, 0), False, qi >= 1), (qi, True, None)],
                          lsum, esum, dqt)
        top = _row_iota((LANES, blk)) < HEAD_DIM
        dq_ref[...] = jnp.where(top, dqt[0], dqt[1]).T

    seq_blk = pl.BlockSpec((seq, LANES), lambda b, h, i: (b, h))
    tok = pl.BlockSpec((blk, LANES), lambda b, h, i: (b * nq + i, h))
    tok_shape = jax.ShapeDtypeStruct((nb * seq, D_BRANCH), F32)
    return pl.pallas_call(
        body, name="sb_attn_bwd",
        grid=(nb, N_PAIRS, nq),
        in_specs=[tok, seq_blk,
                  pl.BlockSpec((1, 1, LANES, seq), lambda b, h, i: (b, h, 0, 0)),
                  seq_blk,
                  pl.BlockSpec((1, 2, 1, 8, blk), lambda b, h, i: (b, h, i, 0, 0)),
                  pl.BlockSpec(memory_space=pltpu.SMEM),
                  tok],
        out_specs=[tok, seq_blk, seq_blk],
        out_shape=[tok_shape, tok_shape, tok_shape],
        compiler_params=_params(3),
    )(qs, kn, kt, vb, tot, low, d_o)


def _qk_bwd(proj, dqs, dkn, dvh, qw2, kw2, nb, seq, tq):
    nl = seq // tq
    scale = 1.0 / math.sqrt(HEAD_DIM)

    def body(q_ref, k_ref, dq_ref, dk_ref, dv_ref, qw_ref, kw_ref,
             dqr_ref, dkr_ref, dvr_ref, gq_ref, gk_ref):
        @pl.when((pl.program_id(0) == 0) & (pl.program_id(1) == 0))
        def _():
            gq_ref[...] = jnp.zeros_like(gq_ref)
            gk_ref[...] = jnp.zeros_like(gk_ref)

        def norm_bwd(x, w, dy):
            r = lax.rsqrt(_pair_sum(x * x) * (1.0 / HEAD_DIM) + EPS)
            xhat = x * r
            g = dy * w
            m = _pair_sum(g * xhat) * (1.0 / HEAD_DIM)
            return r * (g - xhat * m), jnp.sum(dy * xhat, axis=0, keepdims=True)

        dvr_ref[...] = dv_ref[...].astype(BF16)
        gq = jnp.zeros((1, LANES), F32)
        gk = jnp.zeros((1, LANES), F32)
        for p in range(N_PAIRS):
            cols = slice(p * LANES, (p + 1) * LANES)
            dqr, gq_p = norm_bwd(q_ref[:, cols], qw_ref[...], dq_ref[:, cols] * scale)
            dkr, gk_p = norm_bwd(k_ref[:, cols], kw_ref[...], dk_ref[:, cols])
            dqr_ref[:, cols] = dqr.astype(BF16)
            dkr_ref[:, cols] = dkr.astype(BF16)
            gq, gk = gq + gq_p, gk + gk_p
        gq_ref[...] += gq
        gk_ref[...] += gk

    tok = lambda blk: pl.BlockSpec((tq, D_BRANCH), lambda b, i: (b * nl + i, blk))
    vec = pl.BlockSpec((1, LANES), lambda b, i: (0, 0))
    tshape = jax.ShapeDtypeStruct((nb * seq, D_BRANCH), BF16)
    return pl.pallas_call(
        body, name="qk_bwd",
        grid=(nb, nl),
        in_specs=[tok(0), tok(1), tok(0), tok(0), tok(0), vec, vec],
        out_specs=[tok(0), tok(0), tok(0), vec, vec],
        out_shape=[tshape, tshape, tshape,
                   jax.ShapeDtypeStruct((1, LANES), F32), jax.ShapeDtypeStruct((1, LANES), F32)],
        compiler_params=_params(2),
    )(proj, proj, dqs, dkn, dvh, qw2, kw2)


def _shift_down(cur, prev, k):
    if k == 0:
        return cur
    rows = _row_iota(cur.shape)
    return jnp.where(rows < k, pltpu.roll(prev, k, axis=0), pltpu.roll(cur, k, axis=0))


def _shift_up(cur, nxt, k):
    if k == 0:
        return cur
    n = cur.shape[0]
    rows = _row_iota(cur.shape)
    return jnp.where(rows < n - k, pltpu.roll(cur, n - k, axis=0), pltpu.roll(nxt, n - k, axis=0))


def _conv_taps(cur, prev):
    return [_shift_down(cur, prev, CONV_TAPS - 1 - i) for i in range(CONV_TAPS)]


def _conv_pre(taps, w, b):
    out = b
    for i in range(CONV_TAPS):
        out = out + taps[i] * w[i:i + 1, :]
    return out


def _silu(x):
    return x * _sigmoid(x)


def _silu_and_grad(x):
    s = _sigmoid(x)
    return x * s, s * (1.0 + x * (1.0 - s))


def _dot01(x, m01, parts, dims=None, m_left=False):
    total, rest = None, x
    for i in range(parts):
        piece = rest.astype(BF16)
        if i + 1 < parts:
            rest = rest - piece.astype(F32)
        term = _dot(m01, piece, dims) if m_left else _dot(piece, m01, dims)
        total = term if total is None else total + term
    return total


def _chunk_decay(dt_raw, dtb, alog, expand, qc):
    dt = _softplus(dt_raw + dtb)
    d_a = dt * (-jnp.exp(alog))
    r_i = lax.broadcasted_iota(jnp.int32, (qc, qc), 0)
    c_i = lax.broadcasted_iota(jnp.int32, (qc, qc), 1)
    tril = r_i >= c_i
    a_cs = _dot01(d_a, jnp.where(tril, 1.0, 0.0).astype(BF16), 3, m_left=True)
    dt_x = _dot01(dt, expand, 3)
    acs_x = _dot01(a_cs, expand, 3)
    return dt, d_a, a_cs, dt_x, acs_x, tril


def _ssd_fwd(proj, conv_w, conv_b, dtb, alog, dskip, nb, seq):
    qc = SSD_CHUNK
    nc = seq // qc

    def body(xs_ref, bc_ref, dt_ref, cw_ref, cb_ref, dtb_ref, al_ref, ds_ref,
             y_ref, st_ref, pxs_ref, pbc_ref, state_ref):
        @pl.when(pl.program_id(1) == 0)
        def _():
            pxs_ref[...] = jnp.zeros_like(pxs_ref)
            pbc_ref[...] = jnp.zeros_like(pbc_ref)
            state_ref[...] = jnp.zeros_like(state_ref)

        expand = _head_expand()
        xs_raw = xs_ref[...]
        bc_raw = bc_ref[...]
        cw = cw_ref[...]
        cb = cb_ref[...]
        xs = _silu(_conv_pre(_conv_taps(xs_raw, pxs_ref[...]), cw[:, :D_BRANCH], cb[:, :D_BRANCH]))
        bc = _silu(_conv_pre(_conv_taps(bc_raw, pbc_ref[...]), cw[:, D_BRANCH:], cb[:, D_BRANCH:]))
        pxs_ref[...] = xs_raw
        pbc_ref[...] = bc_raw

        dt, d_a, a_cs, dt_x, acs_x, tril = _chunk_decay(
            dt_ref[...], dtb_ref[...], al_ref[...], expand, qc)
        a_cst = a_cs.T
        aend_x = acs_x[qc - 1:qc, :]
        ea_x = jnp.exp(acs_x)
        dec_x = jnp.exp(aend_x - acs_x)
        xt = xs * dt_x
        xtb = xt.astype(BF16)
        xdb = (xt * dec_x).astype(BF16)
        d_x = _dot01(jnp.broadcast_to(ds_ref[...], (8, LANES)), expand, 3)[0:1, :]
        st_ref[0, 0] = state_ref[...]

        for g in range(N_GROUPS):
            gs = slice(g * GROUP_W, (g + 1) * GROUP_W)
            bg = bc[:, g * D_STATE:(g + 1) * D_STATE]
            cg = bc[:, (N_GROUPS + g) * D_STATE:(N_GROUPS + g + 1) * D_STATE]
            bgb = bg.astype(BF16)
            cgb = cg.astype(BF16)
            cbm = _dot(cgb, bgb, _NT)
            st_in = state_ref[g]
            y_off = _dot(cgb, st_in.astype(BF16)) * ea_x[:, gs]
            for k in range(HEADS_PER_GROUP):
                h = g * HEADS_PER_GROUP + k
                hs = slice(h * HEAD_DIM, (h + 1) * HEAD_DIM)
                seg = a_cs[:, h:h + 1] - a_cst[h:h + 1, :]
                gh = cbm * jnp.exp(jnp.where(tril, seg, -1e30))
                y_h = _dot(gh.astype(BF16), xtb[:, hs]) + y_off[:, k * HEAD_DIM:(k + 1) * HEAD_DIM]
                y_ref[:, hs] = y_h + d_x[:, hs] * xs[:, hs]
            state_ref[g] = st_in * jnp.exp(aend_x[:, gs]) + _dot(bg.T.astype(BF16), xdb[:, gs])

    nblk = lambda w, off: pl.BlockSpec((qc, w), lambda b, c: (b * nc + c, off))
    full = lambda r, w: pl.BlockSpec((r, w), lambda b, c: (0, 0))
    return pl.pallas_call(
        body, name="ssd_fwd",
        grid=(nb, nc),
        in_specs=[nblk(D_BRANCH, COL_XS // D_BRANCH), nblk(D_BC, COL_BC // D_BC),
                  nblk(LANES, COL_DT // LANES),
                  full(CONV_TAPS, D_CONV), full(1, D_CONV), full(1, LANES), full(1, LANES),
                  full(1, LANES)],
        out_specs=[pl.BlockSpec((qc, D_BRANCH), lambda b, c: (b * nc + c, 0)),
                   pl.BlockSpec((1, 1, N_GROUPS, D_STATE, GROUP_W), lambda b, c: (b, c, 0, 0, 0))],
        out_shape=[jax.ShapeDtypeStruct((nb * seq, D_BRANCH), F32),
                   jax.ShapeDtypeStruct((nb, nc, N_GROUPS, D_STATE, GROUP_W), F32)],
        scratch_shapes=[pltpu.VMEM((qc, D_BRANCH), F32), pltpu.VMEM((qc, D_BC), F32),
                        pltpu.VMEM((N_GROUPS, D_STATE, GROUP_W), F32)],
        compiler_params=_params(2),
    )(proj, proj, proj, conv_w, conv_b, dtb, alog, dskip)


def _ssd_bwd(proj, d_y, states, conv_w, conv_b, dtb, alog, dskip, nb, seq, slabs=()):
    qc = SSD_CHUNK
    nc = seq // qc

    def body(xs_ref, bc_ref, dt_ref, pxs_ref, pbc_ref, dy_ref, st_ref, stn_ref,
             cw_ref, cb_ref, dtb_ref, al_ref, ds_ref,
             dx_ref, gcw_ref, gcb_ref, gdtb_ref, gal_ref, gds_ref,
             dst_ref, nxs_ref, nbc_ref, yd_ref, dxt_ref):
        step = pl.program_id(1)
        chunk = nc - 1 - step

        @pl.when(step == 0)
        def _():
            dst_ref[...] = jnp.zeros_like(dst_ref)
            nxs_ref[...] = jnp.zeros_like(nxs_ref)
            nbc_ref[...] = jnp.zeros_like(nbc_ref)

        @pl.when((pl.program_id(0) == 0) & (step == 0))
        def _():
            gcw_ref[...] = jnp.zeros_like(gcw_ref)
            gcb_ref[...] = jnp.zeros_like(gcb_ref)
            gdtb_ref[...] = jnp.zeros_like(gdtb_ref)
            gal_ref[...] = jnp.zeros_like(gal_ref)
            gds_ref[...] = jnp.zeros_like(gds_ref)

        expand = _head_expand()
        collapse = lambda v: _dot01(v, expand, 2, _NT)
        first = jnp.where(chunk == 0, 0.0, 1.0)
        xs_raw = xs_ref[...]
        bc_raw = bc_ref[...]
        pxs = pxs_ref[...] * first
        pbc = pbc_ref[...] * first
        cw = cw_ref[...]
        cb = cb_ref[...]
        taps_xs = _conv_taps(xs_raw, pxs)
        taps_bc = _conv_taps(bc_raw, pbc)
        xs, dsilu_xs = _silu_and_grad(_conv_pre(taps_xs, cw[:, :D_BRANCH], cb[:, :D_BRANCH]))
        bc, dsilu_bc = _silu_and_grad(_conv_pre(taps_bc, cw[:, D_BRANCH:], cb[:, D_BRANCH:]))

        dt_in = dt_ref[...] + dtb_ref[...]
        dt, d_a, a_cs, dt_x, acs_x, tril = _chunk_decay(
            dt_ref[...], dtb_ref[...], al_ref[...], expand, qc)
        a_cst = a_cs.T
        aend_x = acs_x[qc - 1:qc, :]
        ea_x = jnp.exp(acs_x)
        dec_x = jnp.exp(aend_x - acs_x)
        xt = xs * dt_x
        xtb = xt.astype(BF16)
        xdb = (xt * dec_x).astype(BF16)
        d_x = _dot01(jnp.broadcast_to(ds_ref[...], (8, LANES)), expand, 3)[0:1, :]

        dy = dy_ref[...]
        dyb = dy.astype(BF16)
        dyeab = (dy * ea_x).astype(BF16)
        gds_ref[...] += collapse(jnp.broadcast_to(jnp.sum(dy * xs, axis=0, keepdims=True),
                                                  (8, D_BRANCH)))[0:1, :]

        d_bc = []
        d_cc = []
        y_offs = []
        dxt_states = []
        end_terms = []
        for g in range(N_GROUPS):
            gs = slice(g * GROUP_W, (g + 1) * GROUP_W)
            bg = bc[:, g * D_STATE:(g + 1) * D_STATE]
            cg = bc[:, (N_GROUPS + g) * D_STATE:(N_GROUPS + g + 1) * D_STATE]
            bgb = bg.astype(BF16)
            cgb = cg.astype(BF16)
            cbm = _dot(cgb, bgb, _NT)
            st_in = st_ref[0, 0, g]
            st_inb = st_in.astype(BF16)
            d_st = dst_ref[g]
            d_stb = d_st.astype(BF16)
            y_offs.append(_dot(cgb, st_inb) * ea_x[:, gs])
            dxt_states.append(_dot(bgb, d_stb) * dec_x[:, gs])
            d_c = _dot(dyeab[:, gs], st_inb, _NT)
            d_b = _dot(xdb[:, gs], d_stb, _NT)
            d_cb = jnp.zeros((qc, qc), F32)
            for k in range(HEADS_PER_GROUP):
                h = g * HEADS_PER_GROUP + k
                hs = slice(h * HEAD_DIM, (h + 1) * HEAD_DIM)
                seg = a_cs[:, h:h + 1] - a_cst[h:h + 1, :]
                lh = jnp.exp(jnp.where(tril, seg, -1e30))
                ghb = (cbm * lh).astype(BF16)
                d_cb = d_cb + _dot(dyb[:, hs], xtb[:, hs], _NT) * lh
                yd_ref[:, hs] = _dot(ghb, xtb[:, hs])
                dxt_ref[:, hs] = _dot(ghb, dyb[:, hs], _TN)
            d_cbb = d_cb.astype(BF16)
            d_cc.append(d_c + _dot(d_cbb, bgb))
            d_bc.append(d_b + _dot(d_cbb, cgb, _TN))
            end_terms.append(jnp.sum(d_st * stn_ref[0, 0, g], axis=0, keepdims=True))
            dst_ref[g] = d_st * jnp.exp(aend_x[:, gs]) + _dot(cg.T.astype(BF16), dyeab[:, gs])

        y_off = jnp.concatenate(y_offs, axis=1)
        dxt_state = jnp.concatenate(dxt_states, axis=1)
        dxt = dxt_ref[...] + dxt_state
        last = jnp.where(chunk == nc - 1, 0.0, 1.0)
        end_c = collapse(jnp.broadcast_to(jnp.concatenate(end_terms, axis=1), (8, D_BRANCH)))[0:1, :]
        da_cs = collapse(dyb.astype(F32) * yd_ref[...] - dxt_ref[...] * xtb.astype(F32)
                         + dy * y_off - dxt_state * xt)
        da_cs = da_cs + jnp.where(_row_iota(da_cs.shape) == qc - 1, end_c * last, 0.0)
        triu = lax.broadcasted_iota(jnp.int32, (qc, qc), 0) <= lax.broadcasted_iota(jnp.int32, (qc, qc), 1)
        dd_a = _dot01(da_cs, jnp.where(triu, 1.0, 0.0).astype(BF16), 3, m_left=True)
        ddt = dd_a * (-jnp.exp(al_ref[...])) + collapse(dxt * xs)
        head_lanes = _lane_iota(ddt.shape) < N_HEADS
        ddt_raw = jnp.where(head_lanes, ddt * _sigmoid(dt_in), 0.0)
        gal_ref[...] += jnp.sum(jnp.where(head_lanes, dd_a * d_a, 0.0), axis=0, keepdims=True)
        gdtb_ref[...] += jnp.sum(ddt_raw, axis=0, keepdims=True)

        dpre_xs = (dxt * dt_x + d_x * dy) * dsilu_xs
        dpre_bc = jnp.concatenate(d_bc + d_cc, axis=1) * dsilu_bc
        gcb_ref[...] += jnp.concatenate([jnp.sum(dpre_xs, axis=0, keepdims=True),
                                         jnp.sum(dpre_bc, axis=0, keepdims=True)], axis=1)
        nxs = nxs_ref[...]
        nbc = nbc_ref[...]
        du_xs = jnp.zeros_like(dpre_xs)
        du_bc = jnp.zeros_like(dpre_bc)
        for i in range(CONV_TAPS):
            k = CONV_TAPS - 1 - i
            gcw_ref[i:i + 1, :] += jnp.concatenate(
                [jnp.sum(dpre_xs * taps_xs[i], axis=0, keepdims=True),
                 jnp.sum(dpre_bc * taps_bc[i], axis=0, keepdims=True)], axis=1)
            du_xs = du_xs + _shift_up(dpre_xs, nxs, k) * cw[i:i + 1, :D_BRANCH]
            du_bc = du_bc + _shift_up(dpre_bc, nbc, k) * cw[i:i + 1, D_BRANCH:]
        nxs_ref[...] = dpre_xs
        nbc_ref[...] = dpre_bc

        dx_ref[:, :D_BRANCH] = du_xs.astype(BF16)
        dx_ref[:, D_BRANCH:D_CONV] = du_bc.astype(BF16)
        dx_ref[:, D_CONV:D_CONV + LANES] = ddt_raw.astype(BF16)
        dx_ref[:, D_CONV + LANES:] = jnp.zeros((qc, 2048 - D_CONV - LANES), BF16)

    rev = lambda b, c: b * nc + (nc - 1 - c)
    prv = lambda b, c: b * nc + jnp.maximum(nc - 2 - c, 0)
    nblk = lambda w, off, f: pl.BlockSpec((qc, w), lambda b, c: (f(b, c), off))
    full = lambda r, w: pl.BlockSpec((r, w), lambda b, c: (0, 0))
    st_spec = lambda f: pl.BlockSpec((1, 1, N_GROUPS, D_STATE, GROUP_W),
                                     lambda b, c: (b, f(c), 0, 0, 0))
    return _call_with_exchange(
        body, (proj, proj, proj, proj, proj, d_y, states, states, conv_w, conv_b, dtb, alog, dskip),
        slabs, (True,) * len(slabs), name="ssd_bwd", grid=(nb, nc),
        in_specs=[nblk(D_BRANCH, COL_XS // D_BRANCH, rev), nblk(D_BC, COL_BC // D_BC, rev),
                  nblk(LANES, COL_DT // LANES, rev),
                  nblk(D_BRANCH, COL_XS // D_BRANCH, prv), nblk(D_BC, COL_BC // D_BC, prv),
                  nblk(D_BRANCH, 0, rev),
                  st_spec(lambda c: nc - 1 - c), st_spec(lambda c: jnp.minimum(nc - c, nc - 1)),
                  full(CONV_TAPS, D_CONV), full(1, D_CONV), full(1, LANES), full(1, LANES),
                  full(1, LANES)],
        out_specs=[nblk(2048, 0, rev), full(8, D_CONV), full(1, D_CONV), full(1, LANES),
                   full(1, LANES), full(1, LANES)],
        out_shape=[jax.ShapeDtypeStruct((nb * seq, 2048), BF16),
                   jax.ShapeDtypeStruct((8, D_CONV), F32), jax.ShapeDtypeStruct((1, D_CONV), F32),
                   jax.ShapeDtypeStruct((1, LANES), F32), jax.ShapeDtypeStruct((1, LANES), F32),
                   jax.ShapeDtypeStruct((1, LANES), F32)],
        scratch_shapes=[pltpu.VMEM((N_GROUPS, D_STATE, GROUP_W), F32),
                        pltpu.VMEM((qc, D_BRANCH), F32), pltpu.VMEM((qc, D_BC), F32),
                        pltpu.VMEM((qc, D_BRANCH), F32), pltpu.VMEM((qc, D_BRANCH), F32)])


def _mid(o_sb, y_ssd, proj, x2, target, sb_w, ssd_w, w_out_b, tm):
    t = x2.shape[0]
    inv_d = 1.0 / D_MODEL

    def body(o_ref, y_ref, zsb_ref, zssd_ref, x_ref, tg_ref, sbw_ref, ssdw_ref, w_ref,
             dout_ref, dosb_ref, dy_ref, dz_ref, gw_ref, gsb_ref, gssd_ref, loss_ref):
        @pl.when(pl.program_id(0) == 0)
        def _():
            gw_ref[...] = jnp.zeros_like(gw_ref)
            gsb_ref[...] = jnp.zeros_like(gsb_ref)
            gssd_ref[...] = jnp.zeros_like(gssd_ref)
            loss_ref[...] = jnp.zeros_like(loss_ref)

        def branch(val, z, w):
            gate, dgate = _silu_and_grad(z)
            g = val * gate
            r = lax.rsqrt(jnp.mean(g * g, axis=1, keepdims=True) + EPS)
            xhat = g * r
            return (gate, dgate, r, xhat), (xhat * w).astype(BF16)

        o = o_ref[...]
        y = y_ref[...]
        saved_a, mix_a = branch(o, zsb_ref[...], sbw_ref[...])
        saved_b, mix_b = branch(y, zssd_ref[...], ssdw_ref[...])
        out = x_ref[...] + _dot(mix_a, w_ref[:D_BRANCH, :]) + _dot(mix_b, w_ref[D_BRANCH:, :])
        diff = out - tg_ref[...]
        loss_ref[...] += 0.5 * inv_d * jnp.sum(diff * diff)
        d_out = diff * inv_d
        dout_ref[...] = d_out
        d_outb = d_out.astype(BF16)
        gw_ref[:D_BRANCH, :] += _dot(mix_a, d_outb, _TN)
        gw_ref[D_BRANCH:, :] += _dot(mix_b, d_outb, _TN)

        def branch_bwd(dmix, val, w, saved):
            gate, dgate, r, xhat = saved
            gg = dmix * w
            m = jnp.mean(gg * xhat, axis=1, keepdims=True)
            dg = r * (gg - xhat * m)
            return dg * gate, dg * val * dgate, jnp.sum(dmix * xhat, axis=0, keepdims=True)

        dmix_a = _dot(d_outb, w_ref[:D_BRANCH, :], _NT)
        dmix_b = _dot(d_outb, w_ref[D_BRANCH:, :], _NT)
        d_o, dz_a, gsb = branch_bwd(dmix_a, o, sbw_ref[...], saved_a)
        d_y, dz_b, gssd = branch_bwd(dmix_b, y, ssdw_ref[...], saved_b)
        dosb_ref[...] = d_o
        dy_ref[...] = d_y
        dz_ref[:, :D_BRANCH] = dz_a.astype(BF16)
        dz_ref[:, D_BRANCH:] = dz_b.astype(BF16)
        gsb_ref[...] += gsb
        gssd_ref[...] += gssd

    row = lambda w, off: pl.BlockSpec((tm, w), lambda i: (i, off))
    full = lambda r, w: pl.BlockSpec((r, w), lambda i: (0, 0))
    resident = pl.BlockSpec((2 * D_BRANCH, D_MODEL), lambda i: (0, 0), pipeline_mode=pl.Buffered(1))
    tok = jax.ShapeDtypeStruct((t, D_MODEL), F32)
    return pl.pallas_call(
        body, name="mid",
        grid=(t // tm,),
        in_specs=[row(D_BRANCH, 0), row(D_BRANCH, 0), row(D_BRANCH, 3), row(D_BRANCH, 4),
                  row(D_MODEL, 0), row(D_MODEL, 0), full(1, D_BRANCH), full(1, D_BRANCH),
                  resident],
        out_specs=[row(D_MODEL, 0), row(D_BRANCH, 0), row(D_BRANCH, 0), row(2 * D_BRANCH, 0),
                   resident, full(1, D_BRANCH), full(1, D_BRANCH),
                   full(1, LANES)],
        out_shape=[tok, tok, tok, jax.ShapeDtypeStruct((t, 2 * D_BRANCH), BF16),
                   jax.ShapeDtypeStruct((2 * D_BRANCH, D_MODEL), F32),
                   jax.ShapeDtypeStruct((1, D_BRANCH), F32), jax.ShapeDtypeStruct((1, D_BRANCH), F32),
                   jax.ShapeDtypeStruct((1, LANES), F32)],
        compiler_params=_params(1),
    )(o_sb, y_ssd, proj, proj, x2, target, sb_w, ssd_w, w_out_b)


_DPROJ_FIRST = (0, 1, 2, 3, 5)
_DPROJ_BLOCKS = (1, 1, 1, 2, 2)


def _in_proj_bwd_x(d_parts, w_in_t, x2, d_out, norm_w, tm, slabs=()):
    t = x2.shape[0]
    n_parts = len(d_parts)

    def body(*refs):
        dp_refs = refs[:n_parts]
        w_ref, x_ref, dout_ref, nw_ref, gx_ref, gnw_ref = refs[n_parts:]

        @pl.when(pl.program_id(0) == 0)
        def _():
            gnw_ref[...] = jnp.zeros_like(gnw_ref)

        d_hn = None
        for p in range(n_parts):
            rows = slice(_DPROJ_FIRST[p] * 1024, (_DPROJ_FIRST[p] + _DPROJ_BLOCKS[p]) * 1024)
            term = _dot(dp_refs[p][...], w_ref[rows, :])
            d_hn = term if d_hn is None else d_hn + term
        xf = x_ref[...]
        r = lax.rsqrt(jnp.mean(xf * xf, axis=1, keepdims=True) + EPS)
        xhat = xf * r
        g = d_hn * nw_ref[...]
        m = jnp.mean(g * xhat, axis=1, keepdims=True)
        gx_ref[...] = dout_ref[...] + r * (g - xhat * m)
        gnw_ref[...] += jnp.sum(d_hn * xhat, axis=0, keepdims=True)

    row = lambda w: pl.BlockSpec((tm, w), lambda i: (i, 0))
    return _call_with_exchange(
        body, (*d_parts, w_in_t, x2, d_out, norm_w), slabs, (True,) * len(slabs),
        name="in_proj_bwd_x", grid=(t // tm,),
        in_specs=[row(1024 * _DPROJ_BLOCKS[p]) for p in range(n_parts)] + [
                  pl.BlockSpec((D_IN_PAD, D_MODEL), lambda i: (0, 0), pipeline_mode=pl.Buffered(1)),
                  row(D_MODEL), row(D_MODEL), pl.BlockSpec((1, D_MODEL), lambda i: (0, 0))],
        out_specs=[row(D_MODEL), pl.BlockSpec((1, D_MODEL), lambda i: (0, 0))],
        out_shape=[jax.ShapeDtypeStruct((t, D_MODEL), F32), jax.ShapeDtypeStruct((1, D_MODEL), F32)])


def _in_proj_bwd_w(hn, d_parts, tm):
    t = hn.shape[0]
    n_parts = len(d_parts)

    def body(hn_ref, *refs):
        dp_refs, gw_ref = refs[:n_parts], refs[n_parts]

        @pl.when(pl.program_id(0) == 0)
        def _():
            gw_ref[...] = jnp.zeros_like(gw_ref)

        hnt = hn_ref[...].astype(F32).T.astype(BF16)
        for p in range(n_parts):
            cols = slice(_DPROJ_FIRST[p] * 1024, (_DPROJ_FIRST[p] + _DPROJ_BLOCKS[p]) * 1024)
            gw_ref[:, cols] += _dot(hnt, dp_refs[p][...])

    return pl.pallas_call(
        body, name="in_proj_bwd_w",
        grid=(t // tm,),
        in_specs=[pl.BlockSpec((tm, D_MODEL), lambda i: (i, 0))]
                 + [pl.BlockSpec((tm, 1024 * _DPROJ_BLOCKS[p]), lambda i: (i, 0))
                    for p in range(n_parts)],
        out_specs=pl.BlockSpec((D_MODEL, D_IN_PAD), lambda i: (0, 0), pipeline_mode=pl.Buffered(1)),
        out_shape=jax.ShapeDtypeStruct((D_MODEL, D_IN_PAD), F32),
        compiler_params=_params(1),
    )(hn, *d_parts)


def _adamw(parts, w, m, v, tr, name):
    _, rows, cols = w.shape
    c1 = 1.0 - ADAM_B1 ** ADAM_STEP
    c2 = 1.0 - ADAM_B2 ** ADAM_STEP

    def body(p_ref, w_ref, m_ref, v_ref, g_ref, d_ref, nm_ref, nv_ref):
        g = p_ref[0].astype(F32)
        for s in range(1, N_DEV):
            g = g + p_ref[s].astype(F32)
        nm = ADAM_B1 * m_ref[0] + (1.0 - ADAM_B1) * g
        nv = ADAM_B2 * v_ref[0] + (1.0 - ADAM_B2) * (g * g)
        g_ref[0] = g
        nm_ref[0] = nm
        nv_ref[0] = nv
        d_ref[0] = -ADAM_LR * ((nm / c1) / (jnp.sqrt(nv / c2) + ADAM_EPS) + ADAM_WD * w_ref[0])

    blk = pl.BlockSpec((1, tr, cols), lambda i: (0, i, 0))
    shape = jax.ShapeDtypeStruct((1, rows, cols), F32)
    return pl.pallas_call(
        body, name=name,
        grid=(rows // tr,),
        in_specs=[pl.BlockSpec((N_DEV, tr, cols), lambda i: (0, i, 0)), blk, blk, blk],
        out_specs=[blk, blk, blk, blk],
        out_shape=[shape, shape, shape, shape],
        compiler_params=_params(1),
    )(parts, w, m, v)


def _mesh_place():
    x, y, c = lax.axis_index("x"), lax.axis_index("y"), lax.axis_index("c")
    return x, y, c, 4 * x + 2 * y + c


def _peer(x, y, c, k):
    px = 1 - x if k & 4 else x
    py = 1 - y if k & 2 else y
    pc = 1 - c if k & 1 else c
    return (px, py, pc), 4 * px + 2 * py + pc


def _exchange(srcs, scatter, name):
    n = len(srcs)

    def body(*refs):
        copies = _exchange_copies(refs[:n], refs[n:2 * n], scatter, *refs[2 * n:])
        _exchange_start(copies)
        _exchange_wait(copies)

    return pl.pallas_call(
        body, name=name,
        in_specs=[_ANY] * n, out_specs=[_ANY] * n, out_shape=_exchange_shapes(srcs, scatter),
        scratch_shapes=_exchange_sems(n),
    )(*srcs)


def _call_with_exchange(body, operands, srcs, scatter, *, name, grid, in_specs, out_specs,
                        out_shape, scratch_shapes=()):
    n_in, n_out, n_scr, n_x = len(in_specs), len(out_specs), len(scratch_shapes), len(srcs)
    params = _params(len(grid))
    if not n_x:
        return pl.pallas_call(body, name=name, grid=grid, in_specs=list(in_specs),
                              out_specs=list(out_specs), out_shape=list(out_shape),
                              scratch_shapes=list(scratch_shapes), compiler_params=params)(*operands)

    def wrapped(*refs):
        ins, refs = refs[:n_in], refs[n_in:]
        x_src, refs = refs[:n_x], refs[n_x:]
        outs, refs = refs[:n_out], refs[n_out:]
        x_dst, refs = refs[:n_x], refs[n_x:]
        scratch, sems = refs[:n_scr], refs[n_scr:]
        ids = [pl.program_id(a) for a in range(len(grid))]
        first = functools.reduce(jnp.logical_and, [i == 0 for i in ids])
        last = functools.reduce(jnp.logical_and, [i == n - 1 for i, n in zip(ids, grid)])

        @pl.when(first)
        def _():
            _exchange_start(_exchange_copies(x_src, x_dst, scatter, *sems))

        body(*ins, *outs, *scratch)

        @pl.when(last)
        def _():
            _exchange_wait(_exchange_copies(x_src, x_dst, scatter, *sems))

    return pl.pallas_call(
        wrapped, name=name, grid=grid,
        in_specs=list(in_specs) + [_ANY] * n_x, out_specs=list(out_specs) + [_ANY] * n_x,
        out_shape=list(out_shape) + _exchange_shapes(srcs, scatter),
        scratch_shapes=list(scratch_shapes) + _exchange_sems(n_x), compiler_params=params,
    )(*operands, *srcs)


def _gather_two_level(shard, name):
    def body(x_ref, out_ref, send_sems, recv_sems, local_sem):
        x, y, c, me = _mesh_place()
        sibling = (x, y, 1 - c)
        chips = [(1 - x, y), (x, 1 - y), (1 - x, 1 - y)]

        def slab(px, py, pc):
            return out_ref.at[4 * px + 2 * py + pc]

        def copy(k, block, to, src=None):
            return pltpu.make_async_remote_copy(
                src_ref=slab(*block) if src is None else src, dst_ref=slab(*block),
                send_sem=send_sems.at[k], recv_sem=recv_sems.at[k],
                device_id=to, device_id_type=pl.DeviceIdType.MESH)

        mine = pltpu.make_async_copy(x_ref, slab(x, y, c), local_sem)
        mine.start()
        first = [copy(0, (x, y, c), sibling, src=x_ref)]
        first += [copy(1 + j, (x, y, c), (*chip, c), src=x_ref) for j, chip in enumerate(chips)]
        for cp in first:
            cp.start()
        passed = [copy(4 + j, (*chip, c), sibling) for j, chip in enumerate(chips)]
        for j, chip in enumerate(chips):
            copy(1 + j, (*chip, c), (x, y, c)).wait_recv()
            passed[j].start()
        copy(0, sibling, (x, y, c)).wait_recv()
        for j, chip in enumerate(chips):
            copy(4 + j, (*chip, 1 - c), (x, y, c)).wait_recv()
        for cp in first + passed:
            cp.wait_send()
        mine.wait()

    return pl.pallas_call(
        body, name=name,
        in_specs=[_ANY], out_specs=_ANY,
        out_shape=jax.ShapeDtypeStruct((N_DEV,) + shard.shape, shard.dtype),
        scratch_shapes=[pltpu.SemaphoreType.DMA((N_DEV - 1,)), pltpu.SemaphoreType.DMA((N_DEV - 1,)),
                        pltpu.SemaphoreType.DMA],
    )(shard)


_ANY = pl.BlockSpec(memory_space=pl.ANY)


def _exchange_shapes(srcs, scatter):
    return [jax.ShapeDtypeStruct(s.shape if sc else (N_DEV,) + s.shape, s.dtype)
            for s, sc in zip(srcs, scatter)]


def _exchange_sems(n):
    return [pltpu.SemaphoreType.DMA((n * (N_DEV - 1),)),
            pltpu.SemaphoreType.DMA((n * (N_DEV - 1),)),
            pltpu.SemaphoreType.DMA((n,))]


def _exchange_copies(src_refs, dst_refs, scatter, send_sems, recv_sems, loc_sems):
    n = len(src_refs)
    x, y, c, me = _mesh_place()

    def src_of(i, idx):
        return src_refs[i].at[idx] if scatter[i] else src_refs[i]

    local = [pltpu.make_async_copy(src_of(i, me), dst_refs[i].at[me], loc_sems.at[i])
             for i in range(n)]
    sends, recvs = [], []
    for k in range(1, N_DEV):
        peer, pidx = _peer(x, y, c, k)
        for i in range(n):
            s = i * (N_DEV - 1) + k - 1
            for dst_slab, group in ((me, sends), (pidx, recvs)):
                group.append(pltpu.make_async_remote_copy(
                    src_ref=src_of(i, pidx), dst_ref=dst_refs[i].at[dst_slab],
                    send_sem=send_sems.at[s], recv_sem=recv_sems.at[s],
                    device_id=peer, device_id_type=pl.DeviceIdType.MESH))
    return local, sends, recvs


def _exchange_start(copies):
    local, sends, _ = copies
    for cp in local + sends:
        cp.start()


def _exchange_wait(copies):
    local, sends, recvs = copies
    for cp in recvs:
        cp.wait_recv()
    for cp in sends:
        cp.wait_send()
    for cp in local:
        cp.wait()


def _pad_lanes(v, width=LANES):
    return jnp.pad(v, ((0, 0), (0, width - v.shape[1])))


def _local_step(x, target, norm_w, w_in_b, q_norm_w, k_norm_w, conv_w, conv_b, dt_bias, a_log,
                d_skip, sb_norm_w, ssd_norm_w, w_out_b, tm=256, tq=512, tmid=256, blk=ATT_BLK,
                scatter=False, w_in_t=None):
    nb, seq, _ = x.shape
    t = nb * seq
    x2 = x.reshape(t, D_MODEL)
    tg2 = target.reshape(t, D_MODEL)
    qw2 = jnp.tile(q_norm_w, (1, 2))
    kw2 = jnp.tile(k_norm_w, (1, 2))
    dtb, alog, dsk = _pad_lanes(dt_bias), _pad_lanes(a_log), _pad_lanes(d_skip)

    if w_in_t is None:
        w_in_t = w_in_b.T
    if scatter:
        proj, hn, wout_all, cw_all = _in_proj(x2, norm_w, w_in_b, tm, (w_out_b, conv_w))
        w_out_b = wout_all.reshape(2 * D_BRANCH, D_MODEL)
        conv_w = jnp.transpose(cw_all, (1, 0, 2)).reshape(CONV_TAPS, D_CONV)
    else:
        proj, hn = _in_proj(x2, norm_w, w_in_b, tm)
    qs, kn, vb, kt = _qk_prep(proj, qw2, kw2, nb, seq, tq)
    o_sb, sb_tot, sb_low = _attn_fwd(qs, kn, vb, nb, seq, blk)
    y_ssd, states = _ssd_fwd(proj, conv_w, conv_b, dtb, alog, dsk, nb, seq)
    d_out, d_osb, d_y, d_z, g_wout, g_sbw, g_ssdw, loss = _mid(
        o_sb, y_ssd, proj, x2, tg2, sb_norm_w, ssd_norm_w, w_out_b, tmid)
    dqs, dkn, dvh = _attn_bwd(qs, kn, kt, vb, sb_tot, sb_low, d_osb, nb, seq, blk)
    dq_raw, dk_raw, dv_raw, g_qw, g_kw = _qk_bwd(proj, dqs, dkn, dvh, qw2, kw2, nb, seq, tq)
    wout_slabs = (g_wout.reshape(N_DEV, 2 * D_BRANCH // N_DEV, D_MODEL).astype(BF16),)
    d_xbc, g_cw, g_cb, g_dtb, g_alog, g_dsk, *moved = _ssd_bwd(
        proj, d_y, states, conv_w, conv_b, dtb, alog, dsk, nb, seq, wout_slabs if scatter else ())
    d_parts = [dq_raw, dk_raw, dv_raw, d_z, d_xbc]
    g_win = _in_proj_bwd_w(hn, d_parts, tm)[:, :D_IN]
    g_cw = g_cw[:CONV_TAPS]
    if scatter:
        g_wout, = moved
        grad_x, g_nw, g_win, g_cw = _in_proj_bwd_x(
            d_parts, w_in_t, x2, d_out, norm_w, tm, _grad_slabs(g_win, g_cw))
    else:
        grad_x, g_nw = _in_proj_bwd_x(d_parts, w_in_t, x2, d_out, norm_w, tm)

    small = dict(
        norm_w=g_nw,
        q_norm_w=g_qw[:, :HEAD_DIM] + g_qw[:, HEAD_DIM:],
        k_norm_w=g_kw[:, :HEAD_DIM] + g_kw[:, HEAD_DIM:],
        conv_b=g_cb, dt_bias=g_dtb[:, :N_HEADS], A_log=g_alog[:, :N_HEADS],
        D_skip=g_dsk[:, :N_HEADS], sb_norm_w=g_sbw, ssd_norm_w=g_ssdw)
    return loss[0, 0], grad_x.reshape(nb, seq, D_MODEL), g_win, g_wout, g_cw, small


def _grad_slabs(g_win, g_cw):
    w_sh = D_IN // N_DEV
    c_sh = D_CONV // N_DEV
    return (jnp.transpose(g_win.reshape(D_MODEL, N_DEV, w_sh), (1, 0, 2)).astype(BF16),
            jnp.pad(jnp.transpose(g_cw.reshape(CONV_TAPS, N_DEV, c_sh), (1, 0, 2)),
                    ((0, 0), (0, 8 - CONV_TAPS), (0, 0))))


_SMALL = ("norm_w", "q_norm_w", "k_norm_w", "conv_b", "dt_bias", "A_log", "D_skip",
          "sb_norm_w", "ssd_norm_w")


def _pack_small(vals):
    flat = jnp.concatenate([_pad_lanes(vals[n], -(-vals[n].shape[1] // LANES) * LANES)
                            for n in _SMALL], axis=1)
    return jnp.pad(flat, ((0, 0), (0, 48 * LANES - flat.shape[1]))).reshape(48, LANES)


def _unpack_small(packed, like):
    out, r = {}, 0
    for n in _SMALL:
        width = like[n].shape[1]
        nr = -(-width // LANES)
        out[n] = packed[r:r + nr].reshape(1, nr * LANES)[:, :width]
        r += nr
    return out


def kernel(x, norm_w, w_in, q_norm_w, k_norm_w, conv_w, conv_b, dt_bias, A_log, D_skip, sb_norm_w, ssd_norm_w, w_out, loss_target, m_norm_w, m_w_in, m_q_norm_w, m_k_norm_w, m_conv_w, m_conv_b, m_dt_bias, m_A_log, m_D_skip, m_sb_norm_w, m_ssd_norm_w, m_w_out, v_norm_w, v_w_in, v_q_norm_w, v_k_norm_w, v_conv_w, v_conv_b, v_dt_bias, v_A_log, v_D_skip, v_sb_norm_w, v_ssd_norm_w, v_w_out):
    win_all = _gather_two_level(w_in[0].astype(BF16), "gather_w_in")
    w_in_b = jnp.pad(jnp.transpose(win_all, (1, 0, 2)).reshape(D_MODEL, D_IN),
                     ((0, 0), (0, D_IN_PAD - D_IN)))
    w_in_t = jnp.pad(jnp.transpose(win_all, (0, 2, 1)).reshape(D_IN, D_MODEL),
                     ((0, D_IN_PAD - D_IN), (0, 0)))

    loss, grad_x, win_parts, wout_parts, cw_parts, g_small = _local_step(
        x, loss_target, norm_w, w_in_b, q_norm_w, k_norm_w, conv_w[0], conv_b, dt_bias, A_log,
        D_skip, sb_norm_w, ssd_norm_w, w_out[0].astype(BF16), scatter=True, w_in_t=w_in_t)
    small_parts, = _exchange([_pack_small(g_small)], [False], "gather_small_grads")

    small_w = dict(norm_w=norm_w, q_norm_w=q_norm_w, k_norm_w=k_norm_w, conv_b=conv_b,
                   dt_bias=dt_bias, A_log=A_log, D_skip=D_skip, sb_norm_w=sb_norm_w,
                   ssd_norm_w=ssd_norm_w)
    small_m = dict(norm_w=m_norm_w, q_norm_w=m_q_norm_w, k_norm_w=m_k_norm_w, conv_b=m_conv_b,
                   dt_bias=m_dt_bias, A_log=m_A_log, D_skip=m_D_skip, sb_norm_w=m_sb_norm_w,
                   ssd_norm_w=m_ssd_norm_w)
    small_v = dict(norm_w=v_norm_w, q_norm_w=v_q_norm_w, k_norm_w=v_k_norm_w, conv_b=v_conv_b,
                   dt_bias=v_dt_bias, A_log=v_A_log, D_skip=v_D_skip, sb_norm_w=v_sb_norm_w,
                   ssd_norm_w=v_ssd_norm_w)

    pad8 = lambda a: jnp.pad(a, ((0, 0), (0, 8 - CONV_TAPS), (0, 0)))
    r_win = _adamw(win_parts, w_in, m_w_in, v_w_in, 128, "adamw_w_in")
    r_wout = _adamw(wout_parts, w_out, m_w_out, v_w_out, 128, "adamw_w_out")
    r_cw = _adamw(cw_parts, pad8(conv_w), pad8(m_conv_w), pad8(v_conv_w), 8, "adamw_conv_w")
    r_small = _adamw(small_parts, _pack_small(small_w)[None], _pack_small(small_m)[None],
                     _pack_small(small_v)[None], 48, "adamw_small")

    loss = lax.psum(loss, ("x", "y", "c"))
    res = {"w_in": r_win, "w_out": r_wout, "conv_w": [a[:, :CONV_TAPS] for a in r_cw]}
    unpacked = [_unpack_small(a[0], small_w) for a in r_small]
    for n in _SMALL:
        res[n] = [u[n] for u in unpacked]
    order = ("norm_w", "w_in", "q_norm_w", "k_norm_w", "conv_w", "conv_b", "dt_bias", "A_log",
             "D_skip", "sb_norm_w", "ssd_norm_w", "w_out")
    outs = [loss, grad_x]
    for kind in range(4):
        outs += [res[n][kind] for n in order]
    return tuple(outs)
```

```python
import functools
import math

import jax
import jax.numpy as jnp
from jax import lax
from jax.experimental import pallas as pl
from jax.experimental.pallas import tpu as pltpu

F32 = jnp.float32
BF16 = jnp.bfloat16

D_MODEL = 1024
N_HEADS = 16
HEAD_DIM = 64
N_PAIRS = N_HEADS // 2
D_BRANCH = 1024
N_GROUPS = 2
HEADS_PER_GROUP = 8
D_STATE = 128
GROUP_W = HEADS_PER_GROUP * HEAD_DIM
D_BC = 2 * N_GROUPS * D_STATE
D_CONV = D_BRANCH + D_BC
D_IN = 6672
D_IN_PAD = 7168
N_COLBLK = D_IN_PAD // 1024
COL_XS = 5120
COL_BC = 6144
COL_DT = 6656
EPS = 1e-6
CONV_TAPS = 4
N_DEV = 8

LANES = 128
SSD_CHUNK = 128
ATT_BLK = 256
ATT_HEADS = 4
ATT_W = ATT_HEADS * HEAD_DIM
N_ATT_GROUPS = N_HEADS // ATT_HEADS
EXP_UNDERFLOW = -105.0
VMEM_LIMIT = 56 * 1024 * 1024

ADAM_LR = 0.001
ADAM_B1 = 0.9
ADAM_B2 = 0.999
ADAM_EPS = 1e-08
ADAM_WD = 0.01
ADAM_STEP = 10

_NT = (((1,), (1,)), ((), ()))
_TN = (((0,), (0,)), ((), ()))


def _params(n_grid):
    return pltpu.CompilerParams(dimension_semantics=("arbitrary",) * n_grid,
                                vmem_limit_bytes=VMEM_LIMIT)


def _dot(a, b, dims=None, precision=None):
    if dims is None:
        return jnp.dot(a, b, preferred_element_type=F32, precision=precision)
    return lax.dot_general(a, b, dims, preferred_element_type=F32, precision=precision)


def _sigmoid(x):
    return 1.0 / (1.0 + jnp.exp(-x))


def _softplus(x):
    return jnp.maximum(x, 0.0) + jnp.log(1.0 + jnp.exp(-jnp.abs(x)))


def _split_bf16(x):
    hi = x.astype(BF16)
    lo = (x - hi.astype(F32)).astype(BF16)
    return hi, lo


def _lane_iota(shape):
    return lax.broadcasted_iota(jnp.int32, shape, len(shape) - 1)


def _row_iota(shape):
    return lax.broadcasted_iota(jnp.int32, shape, len(shape) - 2)


def _pair_sum(x):
    r = lax.broadcasted_iota(jnp.int32, (LANES, LANES), 0)
    c = lax.broadcasted_iota(jnp.int32, (LANES, LANES), 1)
    same_head = jnp.where(r // HEAD_DIM == c // HEAD_DIM, 1.0, 0.0).astype(BF16)
    hi, lo = _split_bf16(x)
    return _dot(hi, same_head) + _dot(lo, same_head)


def _pair_head(x, a):
    lane = _lane_iota(x.shape)
    mine = (lane >= a * HEAD_DIM) & (lane < (a + 1) * HEAD_DIM)
    return jnp.where(mine, x, jnp.zeros_like(x))


def _head_expand():
    r = lax.broadcasted_iota(jnp.int32, (LANES, D_BRANCH), 0)
    c = lax.broadcasted_iota(jnp.int32, (LANES, D_BRANCH), 1)
    return jnp.where(c // HEAD_DIM == r, 1.0, 0.0).astype(BF16)


def _in_proj(x2, norm_w, w_in_b, tm, shards=()):
    t = x2.shape[0]

    def body(x_ref, nw_ref, w_ref, proj_ref, hn_ref):
        xf = x_ref[...]
        r = lax.rsqrt(jnp.mean(xf * xf, axis=1, keepdims=True) + EPS)
        hn = (xf * r * nw_ref[...]).astype(BF16)
        hn_ref[...] = hn
        for j in range(N_COLBLK):
            cols = slice(j * 1024, (j + 1) * 1024)
            proj_ref[:, cols] = _dot(hn, w_ref[:, cols])

    return _call_with_exchange(
        body, (x2, norm_w, w_in_b), shards, (False,) * len(shards), name="in_proj",
        grid=(t // tm,),
        in_specs=[pl.BlockSpec((tm, D_MODEL), lambda i: (i, 0)),
                  pl.BlockSpec((1, D_MODEL), lambda i: (0, 0)),
                  pl.BlockSpec((D_MODEL, D_IN_PAD), lambda i: (0, 0), pipeline_mode=pl.Buffered(1))],
        out_specs=[pl.BlockSpec((tm, D_IN_PAD), lambda i: (i, 0)),
                   pl.BlockSpec((tm, D_MODEL), lambda i: (i, 0))],
        out_shape=[jax.ShapeDtypeStruct((t, D_IN_PAD), F32),
                   jax.ShapeDtypeStruct((t, D_MODEL), BF16)])


def _qk_prep(proj, qw2, kw2, nb, seq, tq):
    nl = seq // tq
    scale = 1.0 / math.sqrt(HEAD_DIM)

    def body(q_ref, k_ref, v_ref, qw_ref, kw_ref, qs_ref, kn_ref, vb_ref, kt_ref):
        def norm(x, w):
            r = lax.rsqrt(_pair_sum(x * x) * (1.0 / HEAD_DIM) + EPS)
            return x * r * w

        vb_ref[...] = v_ref[...].astype(BF16)
        for p in range(N_PAIRS):
            cols = slice(p * LANES, (p + 1) * LANES)
            kn = norm(k_ref[:, cols], kw_ref[...])
            qs_ref[:, cols] = (norm(q_ref[:, cols], qw_ref[...]) * scale).astype(BF16)
            kn_ref[:, cols] = kn.astype(BF16)
            kt_ref[0, p] = kn.T.astype(BF16)

    tok_shape = jax.ShapeDtypeStruct((nb * seq, D_BRANCH), BF16)
    tok = lambda blk: pl.BlockSpec((tq, D_BRANCH), lambda b, i: (b * nl + i, blk))
    vec = pl.BlockSpec((1, LANES), lambda b, i: (0, 0))
    return pl.pallas_call(
        body, name="qk_prep",
        grid=(nb, nl),
        in_specs=[tok(0), tok(1), tok(2), vec, vec],
        out_specs=[tok(0), tok(0), tok(0),
                   pl.BlockSpec((1, N_PAIRS, LANES, tq), lambda b, i: (b, 0, 0, i))],
        out_shape=[tok_shape, tok_shape, tok_shape,
                   jax.ShapeDtypeStruct((nb, N_PAIRS, LANES, seq), BF16)],
        compiler_params=_params(2),
    )(proj, proj, proj, qw2, kw2)


def _attn_fwd(qs, kn, vb, nb, seq, blk):
    nq = seq // blk

    def body(q_ref, k_ref, v_ref, o_ref, tot_ref, low_ref, kmax_ref):
        qi = pl.program_id(2)
        r_i = lax.broadcasted_iota(jnp.int32, (blk, blk), 0)
        c_i = lax.broadcasted_iota(jnp.int32, (blk, blk), 1)
        csum = jnp.where(r_i >= c_i, 1.0, 0.0).astype(BF16)
        causal = c_i < r_i
        heads = range(ATT_HEADS)

        head = _pair_head

        @pl.when(qi == 0)
        def _():
            kk = k_ref[...].astype(F32)
            for a in heads:
                ksq = jnp.sum(head(kk * kk, a), axis=1, keepdims=True)
                kmax_ref[a] = jnp.full((8, LANES), jnp.max(ksq))

        q_pair = q_ref[...]
        qf = q_pair.astype(F32)
        q_head = [head(q_pair, a) for a in heads]
        zmax = []
        for a in heads:
            qsq = jnp.sum(head(qf * qf, a), axis=1, keepdims=True)
            zmax.append(1.01 * jnp.sqrt(qsq * kmax_ref[a][0:1, 0:1]) + 0.01)

        def exhausted(run):
            top = functools.reduce(jnp.maximum, [jnp.max(run[a] + zmax[a]) for a in heads])
            return top < EXP_UNDERFLOW

        def sweep(blocks, run, acc):
            offs = [pl.multiple_of(j * blk, blk) for j, _, _ in blocks]
            z = [[_dot(q_head[a], k_ref[pl.ds(off, blk), :], _NT) for a in heads]
                 for off in offs]
            cl = []
            for (_, diag, valid), zb in zip(blocks, z):
                lkb = []
                for a in heads:
                    lk = -_softplus(zb[a])
                    if diag:
                        lk = jnp.where(causal, lk, 0.0)
                    if valid is not None:
                        lk = jnp.where(valid, lk, 0.0)
                    lkb.append(lk.astype(BF16))
                cl.append([_dot(lkb[a], csum) for a in heads])
            for (_, diag, valid), zb, clb, off in zip(blocks, z, cl, offs):
                w = []
                for a in heads:
                    wa = jnp.exp(zb[a] + clb[a] + run[a])
                    if diag:
                        wa = jnp.where(causal, wa, 0.0)
                    if valid is not None:
                        wa = jnp.where(valid, wa, 0.0)
                    w.append(wa.astype(BF16))
                run = [run[a] + clb[a][:, 0:1] for a in heads]
                v_blk = v_ref[pl.ds(off, blk), :]
                for a in heads:
                    acc = acc + _dot(w[a], head(v_blk, a))
            return run, acc

        run = [jnp.zeros((blk, 1), F32)] * ATT_HEADS
        acc = jnp.zeros((blk, ATT_W), F32)
        run, acc = sweep([(qi, True, None), (jnp.maximum(qi - 1, 0), False, qi >= 1)], run, acc)
        low = jnp.maximum(qi - 1, 0)

        def more(carry):
            low, done, _, _ = carry
            return (low > 0) & jnp.logical_not(done)

        def pair(carry):
            low, _, run, acc = carry
            run, acc = sweep([(low - 1, False, None), (jnp.maximum(low - 2, 0), False, low >= 2)],
                             run, acc)
            return jnp.maximum(low - 2, 0), exhausted(run), run, acc

        low, _, run, acc = lax.while_loop(more, pair, (low, exhausted(run), run, acc))
        low_ref[pl.program_id(0) * N_ATT_GROUPS + pl.program_id(1), qi] = low.astype(F32)
        o_ref[...] = acc
        for a in heads:
            as_row = jnp.sum(jnp.where(r_i == c_i, run[a], 0.0), axis=0, keepdims=True)
            tot_ref[0, a, 0] = jnp.broadcast_to(as_row, (8, blk))

    return pl.pallas_call(
        body, name="sb_attn_fwd",
        grid=(nb, N_ATT_GROUPS, nq),
        in_specs=[pl.BlockSpec((blk, ATT_W), lambda b, h, i: (b * nq + i, h)),
                  pl.BlockSpec((seq, ATT_W), lambda b, h, i: (b, h)),
                  pl.BlockSpec((seq, ATT_W), lambda b, h, i: (b, h))],
        out_specs=[pl.BlockSpec((blk, ATT_W), lambda b, h, i: (b * nq + i, h)),
                   pl.BlockSpec((1, ATT_HEADS, 1, 8, blk), lambda b, h, i: (b, h, i, 0, 0)),
                   pl.BlockSpec(memory_space=pltpu.SMEM)],
        out_shape=[jax.ShapeDtypeStruct((nb * seq, D_BRANCH), F32),
                   jax.ShapeDtypeStruct((nb, N_HEADS, nq, 8, blk), F32),
                   jax.ShapeDtypeStruct((nb * N_ATT_GROUPS, nq), F32)],
        scratch_shapes=[pltpu.VMEM((ATT_HEADS, 8, LANES), F32)],
        compiler_params=_params(3),
    )(qs, kn, vb)


def _attn_bwd(qs, kn, kt, vb, tot, low, d_o, nb, seq, blk):
    nq = seq // blk

    def body(q_ref, k_ref, kt_ref, v_ref, tot_ref, low_ref, do_ref, dq_ref, dk_ref, dv_ref):
        qi = pl.program_id(2)

        @pl.when(qi == 0)
        def _():
            dk_ref[...] = jnp.zeros_like(dk_ref)
            dv_ref[...] = jnp.zeros_like(dv_ref)

        r_i = lax.broadcasted_iota(jnp.int32, (blk, blk), 0)
        c_i = lax.broadcasted_iota(jnp.int32, (blk, blk), 1)
        before = jnp.where(c_i < r_i, 1.0, 0.0).astype(BF16)
        upto = jnp.where(c_i <= r_i, 1.0, 0.0).astype(BF16)
        causal = r_i < c_i

        heads = range(ATT_HEADS)
        q_head = [_pair_head(q_ref[...], a) for a in heads]
        d_ob = [_pair_head(do_ref[...].astype(BF16), a) for a in heads]
        total = [tot_ref[0, a, 0][0:1, :] for a in heads]

        def sweep(blocks, lsum, esum, dqt):
            def keep(x, diag, valid):
                if diag:
                    x = jnp.where(causal, x, 0.0)
                if valid is not None:
                    x = jnp.where(valid, x, 0.0)
                return x

            offs = [pl.multiple_of(j * blk, blk) for j, _, _ in blocks]
            zt = [[_dot(k_ref[pl.ds(off, blk), :], q_head[a], _NT) for a in heads]
                  for off in offs]
            dwt = [[_dot(v_ref[pl.ds(off, blk), :], d_ob[a], _NT) for a in heads]
                   for off in offs]
            sp, lk, lpre = [], [], []
            for (_, diag, valid), ztb in zip(blocks, zt):
                sp.append([_softplus(ztb[a]) for a in heads])
                lk.append([keep(-sp[-1][a], diag, valid).astype(BF16) for a in heads])
                lpre.append([_dot(before, lk[-1][a]) for a in heads])
            wt, et, epre = [], [], []
            for i, (_, diag, valid) in enumerate(blocks):
                wt.append([keep(jnp.exp(zt[i][a] + (total[a] - lsum[a] - lpre[i][a])), diag, valid)
                           for a in heads])
                et.append([dwt[i][a] * wt[i][a] for a in heads])
                split = [_split_bf16(et[i][a]) for a in heads]
                epre.append([_dot(upto, split[a][0]) + _dot(upto, split[a][1]) for a in heads])
                lsum = [lsum[a] + lpre[i][a][blk - 1:blk, :] + lk[i][a][blk - 1:blk, :]
                        for a in heads]
            for i, (_, diag, valid) in enumerate(blocks):
                dzb = [keep(et[i][a] - jnp.exp(zt[i][a] - sp[i][a]) * (esum[a] + epre[i][a]),
                            diag, valid).astype(BF16) for a in heads]
                esum = [esum[a] + epre[i][a][blk - 1:blk, :] for a in heads]
                dk_ref[pl.ds(offs[i], blk), :] += functools.reduce(
                    jnp.add, [_dot(dzb[a], q_head[a]) for a in heads])
                dv_ref[pl.ds(offs[i], blk), :] += functools.reduce(
                    jnp.add, [_dot(wt[i][a].astype(BF16), d_ob[a]) for a in heads])
                dqt = [dqt[a] + _dot(kt_ref[0, a // 2, (a % 2) * HEAD_DIM:(a % 2 + 1) * HEAD_DIM,
                                            pl.ds(offs[i], blk)], dzb[a]) for a in heads]
            return lsum, esum, dqt

        row = [jnp.zeros((1, blk), F32)] * ATT_HEADS
        dqt = [jnp.zeros((HEAD_DIM, blk), F32)] * ATT_HEADS
        low = low_ref[pl.program_id(0) * N_ATT_GROUPS + pl.program_id(1), qi].astype(jnp.int32)
        low = jnp.clip(low, 0, jnp.maximum(qi - 1, 0))

        def pair(carry):
            j, lsum, esum, dqt = carry
            return (j + 2,) + sweep([(j, False, None), (j + 1, False, j + 1 < qi - 1)],
                                    lsum, esum, dqt)

        _, lsum, esum, dqt = lax.while_loop(lambda c: c[0] < qi - 1, pair, (low, row, row, dqt))
        _, _, dqt = sweep([(jnp.maximum(qi - 1, 0), False, qi >= 1), (qi, True, None)],
                          lsum, esum, dqt)
        dq_ref[...] = jnp.concatenate(dqt, axis=0).T

    seq_blk = pl.BlockSpec((seq, ATT_W), lambda b, h, i: (b, h))
    tok = pl.BlockSpec((blk, ATT_W), lambda b, h, i: (b * nq + i, h))
    tok_shape = jax.ShapeDtypeStruct((nb * seq, D_BRANCH), F32)
    return pl.pallas_call(
        body, name="sb_attn_bwd",
        grid=(nb, N_ATT_GROUPS, nq),
        in_specs=[tok, seq_blk,
                  pl.BlockSpec((1, ATT_HEADS // 2, LANES, seq), lambda b, h, i: (b, h, 0, 0)),
                  seq_blk,
                  pl.BlockSpec((1, ATT_HEADS, 1, 8, blk), lambda b, h, i: (b, h, i, 0, 0)),
                  pl.BlockSpec(memory_space=pltpu.SMEM),
                  tok],
        out_specs=[tok, seq_blk, seq_blk],
        out_shape=[tok_shape, tok_shape, tok_shape],
        compiler_params=_params(3),
    )(qs, kn, kt, vb, tot, low, d_o)


def _qk_bwd(proj, dqs, dkn, dvh, qw2, kw2, nb, seq, tq):
    nl = seq // tq
    scale = 1.0 / math.sqrt(HEAD_DIM)

    def body(q_ref, k_ref, dq_ref, dk_ref, dv_ref, qw_ref, kw_ref,
             dqr_ref, dkr_ref, dvr_ref, gq_ref, gk_ref):
        @pl.when((pl.program_id(0) == 0) & (pl.program_id(1) == 0))
        def _():
            gq_ref[...] = jnp.zeros_like(gq_ref)
            gk_ref[...] = jnp.zeros_like(gk_ref)

        def norm_bwd(x, w, dy):
            r = lax.rsqrt(_pair_sum(x * x) * (1.0 / HEAD_DIM) + EPS)
            xhat = x * r
            g = dy * w
            m = _pair_sum(g * xhat) * (1.0 / HEAD_DIM)
            return r * (g - xhat * m), jnp.sum(dy * xhat, axis=0, keepdims=True)

        dvr_ref[...] = dv_ref[...].astype(BF16)
        gq = jnp.zeros((1, LANES), F32)
        gk = jnp.zeros((1, LANES), F32)
        for p in range(N_PAIRS):
            cols = slice(p * LANES, (p + 1) * LANES)
            dqr, gq_p = norm_bwd(q_ref[:, cols], qw_ref[...], dq_ref[:, cols] * scale)
            dkr, gk_p = norm_bwd(k_ref[:, cols], kw_ref[...], dk_ref[:, cols])
            dqr_ref[:, cols] = dqr.astype(BF16)
            dkr_ref[:, cols] = dkr.astype(BF16)
            gq, gk = gq + gq_p, gk + gk_p
        gq_ref[...] += gq
        gk_ref[...] += gk

    tok = lambda blk: pl.BlockSpec((tq, D_BRANCH), lambda b, i: (b * nl + i, blk))
    vec = pl.BlockSpec((1, LANES), lambda b, i: (0, 0))
    tshape = jax.ShapeDtypeStruct((nb * seq, D_BRANCH), BF16)
    return pl.pallas_call(
        body, name="qk_bwd",
        grid=(nb, nl),
        in_specs=[tok(0), tok(1), tok(0), tok(0), tok(0), vec, vec],
        out_specs=[tok(0), tok(0), tok(0), vec, vec],
        out_shape=[tshape, tshape, tshape,
                   jax.ShapeDtypeStruct((1, LANES), F32), jax.ShapeDtypeStruct((1, LANES), F32)],
        compiler_params=_params(2),
    )(proj, proj, dqs, dkn, dvh, qw2, kw2)


def _shift_down(cur, prev, k):
    if k == 0:
        return cur
    rows = _row_iota(cur.shape)
    return jnp.where(rows < k, pltpu.roll(prev, k, axis=0), pltpu.roll(cur, k, axis=0))


def _shift_up(cur, nxt, k):
    if k == 0:
        return cur
    n = cur.shape[0]
    rows = _row_iota(cur.shape)
    return jnp.where(rows < n - k, pltpu.roll(cur, n - k, axis=0), pltpu.roll(nxt, n - k, axis=0))


def _conv_taps(cur, prev):
    return [_shift_down(cur, prev, CONV_TAPS - 1 - i) for i in range(CONV_TAPS)]


def _conv_pre(taps, w, b):
    out = b
    for i in range(CONV_TAPS):
        out = out + taps[i] * w[i:i + 1, :]
    return out


def _silu(x):
    return x * _sigmoid(x)


def _silu_and_grad(x):
    s = _sigmoid(x)
    return x * s, s * (1.0 + x * (1.0 - s))


def _dot01(x, m01, parts, dims=None, m_left=False):
    total, rest = None, x
    for i in range(parts):
        piece = rest.astype(BF16)
        if i + 1 < parts:
            rest = rest - piece.astype(F32)
        term = _dot(m01, piece, dims) if m_left else _dot(piece, m01, dims)
        total = term if total is None else total + term
    return total


def _chunk_decay(dt_raw, dtb, alog, expand, qc):
    dt = _softplus(dt_raw + dtb)
    d_a = dt * (-jnp.exp(alog))
    r_i = lax.broadcasted_iota(jnp.int32, (qc, qc), 0)
    c_i = lax.broadcasted_iota(jnp.int32, (qc, qc), 1)
    tril = r_i >= c_i
    a_cs = _dot01(d_a, jnp.where(tril, 1.0, 0.0).astype(BF16), 3, m_left=True)
    dt_x = _dot01(dt, expand, 3)
    acs_x = _dot01(a_cs, expand, 3)
    return dt, d_a, a_cs, dt_x, acs_x, tril


def _ssd_fwd(proj, conv_w, conv_b, dtb, alog, dskip, nb, seq):
    qc = SSD_CHUNK
    nc = seq // qc

    def body(xs_ref, bc_ref, dt_ref, cw_ref, cb_ref, dtb_ref, al_ref, ds_ref,
             y_ref, st_ref, pxs_ref, pbc_ref, state_ref):
        @pl.when(pl.program_id(1) == 0)
        def _():
            pxs_ref[...] = jnp.zeros_like(pxs_ref)
            pbc_ref[...] = jnp.zeros_like(pbc_ref)
            state_ref[...] = jnp.zeros_like(state_ref)

        expand = _head_expand()
        xs_raw = xs_ref[...]
        bc_raw = bc_ref[...]
        cw = cw_ref[...]
        cb = cb_ref[...]
        xs = _silu(_conv_pre(_conv_taps(xs_raw, pxs_ref[...]), cw[:, :D_BRANCH], cb[:, :D_BRANCH]))
        bc = _silu(_conv_pre(_conv_taps(bc_raw, pbc_ref[...]), cw[:, D_BRANCH:], cb[:, D_BRANCH:]))
        pxs_ref[...] = xs_raw
        pbc_ref[...] = bc_raw

        dt, d_a, a_cs, dt_x, acs_x, tril = _chunk_decay(
            dt_ref[...], dtb_ref[...], al_ref[...], expand, qc)
        a_cst = a_cs.T
        aend_x = acs_x[qc - 1:qc, :]
        ea_x = jnp.exp(acs_x)
        dec_x = jnp.exp(aend_x - acs_x)
        xt = xs * dt_x
        xtb = xt.astype(BF16)
        xdb = (xt * dec_x).astype(BF16)
        d_x = _dot01(jnp.broadcast_to(ds_ref[...], (8, LANES)), expand, 3)[0:1, :]
        st_ref[0, 0] = state_ref[...]

        for g in range(N_GROUPS):
            gs = slice(g * GROUP_W, (g + 1) * GROUP_W)
            bg = bc[:, g * D_STATE:(g + 1) * D_STATE]
            cg = bc[:, (N_GROUPS + g) * D_STATE:(N_GROUPS + g + 1) * D_STATE]
            bgb = bg.astype(BF16)
            cgb = cg.astype(BF16)
            cbm = _dot(cgb, bgb, _NT)
            st_in = state_ref[g]
            y_off = _dot(cgb, st_in.astype(BF16)) * ea_x[:, gs]
            for k in range(HEADS_PER_GROUP):
                h = g * HEADS_PER_GROUP + k
                hs = slice(h * HEAD_DIM, (h + 1) * HEAD_DIM)
                seg = a_cs[:, h:h + 1] - a_cst[h:h + 1, :]
                gh = cbm * jnp.exp(jnp.where(tril, seg, -1e30))
                y_h = _dot(gh.astype(BF16), xtb[:, hs]) + y_off[:, k * HEAD_DIM:(k + 1) * HEAD_DIM]
                y_ref[:, hs] = y_h + d_x[:, hs] * xs[:, hs]
            state_ref[g] = st_in * jnp.exp(aend_x[:, gs]) + _dot(bg.T.astype(BF16), xdb[:, gs])

    nblk = lambda w, off: pl.BlockSpec((qc, w), lambda b, c: (b * nc + c, off))
    full = lambda r, w: pl.BlockSpec((r, w), lambda b, c: (0, 0))
    return pl.pallas_call(
        body, name="ssd_fwd",
        grid=(nb, nc),
        in_specs=[nblk(D_BRANCH, COL_XS // D_BRANCH), nblk(D_BC, COL_BC // D_BC),
                  nblk(LANES, COL_DT // LANES),
                  full(CONV_TAPS, D_CONV), full(1, D_CONV), full(1, LANES), full(1, LANES),
                  full(1, LANES)],
        out_specs=[pl.BlockSpec((qc, D_BRANCH), lambda b, c: (b * nc + c, 0)),
                   pl.BlockSpec((1, 1, N_GROUPS, D_STATE, GROUP_W), lambda b, c: (b, c, 0, 0, 0))],
        out_shape=[jax.ShapeDtypeStruct((nb * seq, D_BRANCH), F32),
                   jax.ShapeDtypeStruct((nb, nc, N_GROUPS, D_STATE, GROUP_W), F32)],
        scratch_shapes=[pltpu.VMEM((qc, D_BRANCH), F32), pltpu.VMEM((qc, D_BC), F32),
                        pltpu.VMEM((N_GROUPS, D_STATE, GROUP_W), F32)],
        compiler_params=_params(2),
    )(proj, proj, proj, conv_w, conv_b, dtb, alog, dskip)


def _ssd_bwd(proj, d_y, states, conv_w, conv_b, dtb, alog, dskip, nb, seq, slabs=()):
    qc = SSD_CHUNK
    nc = seq // qc

    def body(xs_ref, bc_ref, dt_ref, pxs_ref, pbc_ref, dy_ref, st_ref, stn_ref,
             cw_ref, cb_ref, dtb_ref, al_ref, ds_ref,
             dx_ref, gcw_ref, gcb_ref, gdtb_ref, gal_ref, gds_ref,
             dst_ref, nxs_ref, nbc_ref, yd_ref, dxt_ref):
        step = pl.program_id(1)
        chunk = nc - 1 - step

        @pl.when(step == 0)
        def _():
            dst_ref[...] = jnp.zeros_like(dst_ref)
            nxs_ref[...] = jnp.zeros_like(nxs_ref)
            nbc_ref[...] = jnp.zeros_like(nbc_ref)

        @pl.when((pl.program_id(0) == 0) & (step == 0))
        def _():
            gcw_ref[...] = jnp.zeros_like(gcw_ref)
            gcb_ref[...] = jnp.zeros_like(gcb_ref)
            gdtb_ref[...] = jnp.zeros_like(gdtb_ref)
            gal_ref[...] = jnp.zeros_like(gal_ref)
            gds_ref[...] = jnp.zeros_like(gds_ref)

        expand = _head_expand()
        collapse = lambda v: _dot01(v, expand, 2, _NT)
        first = jnp.where(chunk == 0, 0.0, 1.0)
        xs_raw = xs_ref[...]
        bc_raw = bc_ref[...]
        pxs = pxs_ref[...] * first
        pbc = pbc_ref[...] * first
        cw = cw_ref[...]
        cb = cb_ref[...]
        taps_xs = _conv_taps(xs_raw, pxs)
        taps_bc = _conv_taps(bc_raw, pbc)
        xs, dsilu_xs = _silu_and_grad(_conv_pre(taps_xs, cw[:, :D_BRANCH], cb[:, :D_BRANCH]))
        bc, dsilu_bc = _silu_and_grad(_conv_pre(taps_bc, cw[:, D_BRANCH:], cb[:, D_BRANCH:]))

        dt_in = dt_ref[...] + dtb_ref[...]
        dt, d_a, a_cs, dt_x, acs_x, tril = _chunk_decay(
            dt_ref[...], dtb_ref[...], al_ref[...], expand, qc)
        a_cst = a_cs.T
        aend_x = acs_x[qc - 1:qc, :]
        ea_x = jnp.exp(acs_x)
        dec_x = jnp.exp(aend_x - acs_x)
        xt = xs * dt_x
        xtb = xt.astype(BF16)
        xdb = (xt * dec_x).astype(BF16)
        d_x = _dot01(jnp.broadcast_to(ds_ref[...], (8, LANES)), expand, 3)[0:1, :]

        dy = dy_ref[...]
        dyb = dy.astype(BF16)
        dyeab = (dy * ea_x).astype(BF16)
        gds_ref[...] += collapse(jnp.broadcast_to(jnp.sum(dy * xs, axis=0, keepdims=True),
                                                  (8, D_BRANCH)))[0:1, :]

        d_bc = []
        d_cc = []
        y_offs = []
        dxt_states = []
        end_terms = []
        for g in range(N_GROUPS):
            gs = slice(g * GROUP_W, (g + 1) * GROUP_W)
            bg = bc[:, g * D_STATE:(g + 1) * D_STATE]
            cg = bc[:, (N_GROUPS + g) * D_STATE:(N_GROUPS + g + 1) * D_STATE]
            bgb = bg.astype(BF16)
            cgb = cg.astype(BF16)
            cbm = _dot(cgb, bgb, _NT)
            st_in = st_ref[0, 0, g]
            st_inb = st_in.astype(BF16)
            d_st = dst_ref[g]
            d_stb = d_st.astype(BF16)
            y_offs.append(_dot(cgb, st_inb) * ea_x[:, gs])
            dxt_states.append(_dot(bgb, d_stb) * dec_x[:, gs])
            d_c = _dot(dyeab[:, gs], st_inb, _NT)
            d_b = _dot(xdb[:, gs], d_stb, _NT)
            d_cb = jnp.zeros((qc, qc), F32)
            for k in range(HEADS_PER_GROUP):
                h = g * HEADS_PER_GROUP + k
                hs = slice(h * HEAD_DIM, (h + 1) * HEAD_DIM)
                seg = a_cs[:, h:h + 1] - a_cst[h:h + 1, :]
                lh = jnp.exp(jnp.where(tril, seg, -1e30))
                ghb = (cbm * lh).astype(BF16)
                d_cb = d_cb + _dot(dyb[:, hs], xtb[:, hs], _NT) * lh
                yd_ref[:, hs] = _dot(ghb, xtb[:, hs])
                dxt_ref[:, hs] = _dot(ghb, dyb[:, hs], _TN)
            d_cbb = d_cb.astype(BF16)
            d_cc.append(d_c + _dot(d_cbb, bgb))
            d_bc.append(d_b + _dot(d_cbb, cgb, _TN))
            end_terms.append(jnp.sum(d_st * stn_ref[0, 0, g], axis=0, keepdims=True))
            dst_ref[g] = d_st * jnp.exp(aend_x[:, gs]) + _dot(cg.T.astype(BF16), dyeab[:, gs])

        y_off = jnp.concatenate(y_offs, axis=1)
        dxt_state = jnp.concatenate(dxt_states, axis=1)
        dxt = dxt_ref[...] + dxt_state
        last = jnp.where(chunk == nc - 1, 0.0, 1.0)
        end_c = collapse(jnp.broadcast_to(jnp.concatenate(end_terms, axis=1), (8, D_BRANCH)))[0:1, :]
        da_cs = collapse(dyb.astype(F32) * yd_ref[...] - dxt_ref[...] * xtb.astype(F32)
                         + dy * y_off - dxt_state * xt)
        da_cs = da_cs + jnp.where(_row_iota(da_cs.shape) == qc - 1, end_c * last, 0.0)
        triu = lax.broadcasted_iota(jnp.int32, (qc, qc), 0) <= lax.broadcasted_iota(jnp.int32, (qc, qc), 1)
        dd_a = _dot01(da_cs, jnp.where(triu, 1.0, 0.0).astype(BF16), 3, m_left=True)
        ddt = dd_a * (-jnp.exp(al_ref[...])) + collapse(dxt * xs)
        head_lanes = _lane_iota(ddt.shape) < N_HEADS
        ddt_raw = jnp.where(head_lanes, ddt * _sigmoid(dt_in), 0.0)
        gal_ref[...] += jnp.sum(jnp.where(head_lanes, dd_a * d_a, 0.0), axis=0, keepdims=True)
        gdtb_ref[...] += jnp.sum(ddt_raw, axis=0, keepdims=True)

        dpre_xs = (dxt * dt_x + d_x * dy) * dsilu_xs
        dpre_bc = jnp.concatenate(d_bc + d_cc, axis=1) * dsilu_bc
        gcb_ref[...] += jnp.concatenate([jnp.sum(dpre_xs, axis=0, keepdims=True),
                                         jnp.sum(dpre_bc, axis=0, keepdims=True)], axis=1)
        nxs = nxs_ref[...]
        nbc = nbc_ref[...]
        du_xs = jnp.zeros_like(dpre_xs)
        du_bc = jnp.zeros_like(dpre_bc)
        for i in range(CONV_TAPS):
            k = CONV_TAPS - 1 - i
            gcw_ref[i:i + 1, :] += jnp.concatenate(
                [jnp.sum(dpre_xs * taps_xs[i], axis=0, keepdims=True),
                 jnp.sum(dpre_bc * taps_bc[i], axis=0, keepdims=True)], axis=1)
            du_xs = du_xs + _shift_up(dpre_xs, nxs, k) * cw[i:i + 1, :D_BRANCH]
            du_bc = du_bc + _shift_up(dpre_bc, nbc, k) * cw[i:i + 1, D_BRANCH:]
        nxs_ref[...] = dpre_xs
        nbc_ref[...] = dpre_bc

        dx_ref[:, :D_BRANCH] = du_xs.astype(BF16)
        dx_ref[:, D_BRANCH:D_CONV] = du_bc.astype(BF16)
        dx_ref[:, D_CONV:D_CONV + LANES] = ddt_raw.astype(BF16)
        dx_ref[:, D_CONV + LANES:] = jnp.zeros((qc, 2048 - D_CONV - LANES), BF16)

    rev = lambda b, c: b * nc + (nc - 1 - c)
    prv = lambda b, c: b * nc + jnp.maximum(nc - 2 - c, 0)
    nblk = lambda w, off, f: pl.BlockSpec((qc, w), lambda b, c: (f(b, c), off))
    full = lambda r, w: pl.BlockSpec((r, w), lambda b, c: (0, 0))
    st_spec = lambda f: pl.BlockSpec((1, 1, N_GROUPS, D_STATE, GROUP_W),
                                     lambda b, c: (b, f(c), 0, 0, 0))
    return _call_with_exchange(
        body, (proj, proj, proj, proj, proj, d_y, states, states, conv_w, conv_b, dtb, alog, dskip),
        slabs, (True,) * len(slabs), name="ssd_bwd", grid=(nb, nc),
        in_specs=[nblk(D_BRANCH, COL_XS // D_BRANCH, rev), nblk(D_BC, COL_BC // D_BC, rev),
                  nblk(LANES, COL_DT // LANES, rev),
                  nblk(D_BRANCH, COL_XS // D_BRANCH, prv), nblk(D_BC, COL_BC // D_BC, prv),
                  nblk(D_BRANCH, 0, rev),
                  st_spec(lambda c: nc - 1 - c), st_spec(lambda c: jnp.minimum(nc - c, nc - 1)),
                  full(CONV_TAPS, D_CONV), full(1, D_CONV), full(1, LANES), full(1, LANES),
                  full(1, LANES)],
        out_specs=[nblk(2048, 0, rev), full(8, D_CONV), full(1, D_CONV), full(1, LANES),
                   full(1, LANES), full(1, LANES)],
        out_shape=[jax.ShapeDtypeStruct((nb * seq, 2048), BF16),
                   jax.ShapeDtypeStruct((8, D_CONV), F32), jax.ShapeDtypeStruct((1, D_CONV), F32),
                   jax.ShapeDtypeStruct((1, LANES), F32), jax.ShapeDtypeStruct((1, LANES), F32),
                   jax.ShapeDtypeStruct((1, LANES), F32)],
        scratch_shapes=[pltpu.VMEM((N_GROUPS, D_STATE, GROUP_W), F32),
                        pltpu.VMEM((qc, D_BRANCH), F32), pltpu.VMEM((qc, D_BC), F32),
                        pltpu.VMEM((qc, D_BRANCH), F32), pltpu.VMEM((qc, D_BRANCH), F32)])


def _mid(o_sb, y_ssd, proj, x2, target, sb_w, ssd_w, w_out_b, tm):
    t = x2.shape[0]
    inv_d = 1.0 / D_MODEL

    def body(o_ref, y_ref, zsb_ref, zssd_ref, x_ref, tg_ref, sbw_ref, ssdw_ref, w_ref,
             dout_ref, dosb_ref, dy_ref, dz_ref, gw_ref, gsb_ref, gssd_ref, loss_ref):
        @pl.when(pl.program_id(0) == 0)
        def _():
            gw_ref[...] = jnp.zeros_like(gw_ref)
            gsb_ref[...] = jnp.zeros_like(gsb_ref)
            gssd_ref[...] = jnp.zeros_like(gssd_ref)
            loss_ref[...] = jnp.zeros_like(loss_ref)

        def branch(val, z, w):
            gate, dgate = _silu_and_grad(z)
            g = val * gate
            r = lax.rsqrt(jnp.mean(g * g, axis=1, keepdims=True) + EPS)
            xhat = g * r
            return (gate, dgate, r, xhat), (xhat * w).astype(BF16)

        o = o_ref[...]
        y = y_ref[...]
        saved_a, mix_a = branch(o, zsb_ref[...], sbw_ref[...])
        saved_b, mix_b = branch(y, zssd_ref[...], ssdw_ref[...])
        out = x_ref[...] + _dot(mix_a, w_ref[:D_BRANCH, :]) + _dot(mix_b, w_ref[D_BRANCH:, :])
        diff = out - tg_ref[...]
        loss_ref[...] += 0.5 * inv_d * jnp.sum(diff * diff)
        d_out = diff * inv_d
        dout_ref[...] = d_out
        d_outb = d_out.astype(BF16)
        gw_ref[:D_BRANCH, :] += _dot(mix_a, d_outb, _TN)
        gw_ref[D_BRANCH:, :] += _dot(mix_b, d_outb, _TN)

        def branch_bwd(dmix, val, w, saved):
            gate, dgate, r, xhat = saved
            gg = dmix * w
            m = jnp.mean(gg * xhat, axis=1, keepdims=True)
            dg = r * (gg - xhat * m)
            return dg * gate, dg * val * dgate, jnp.sum(dmix * xhat, axis=0, keepdims=True)

        dmix_a = _dot(d_outb, w_ref[:D_BRANCH, :], _NT)
        dmix_b = _dot(d_outb, w_ref[D_BRANCH:, :], _NT)
        d_o, dz_a, gsb = branch_bwd(dmix_a, o, sbw_ref[...], saved_a)
        d_y, dz_b, gssd = branch_bwd(dmix_b, y, ssdw_ref[...], saved_b)
        dosb_ref[...] = d_o
        dy_ref[...] = d_y
        dz_ref[:, :D_BRANCH] = dz_a.astype(BF16)
        dz_ref[:, D_BRANCH:] = dz_b.astype(BF16)
        gsb_ref[...] += gsb
        gssd_ref[...] += gssd

    row = lambda w, off: pl.BlockSpec((tm, w), lambda i: (i, off))
    full = lambda r, w: pl.BlockSpec((r, w), lambda i: (0, 0))
    resident = pl.BlockSpec((2 * D_BRANCH, D_MODEL), lambda i: (0, 0), pipeline_mode=pl.Buffered(1))
    tok = jax.ShapeDtypeStruct((t, D_MODEL), F32)
    return pl.pallas_call(
        body, name="mid",
        grid=(t // tm,),
        in_specs=[row(D_BRANCH, 0), row(D_BRANCH, 0), row(D_BRANCH, 3), row(D_BRANCH, 4),
                  row(D_MODEL, 0), row(D_MODEL, 0), full(1, D_BRANCH), full(1, D_BRANCH),
                  resident],
        out_specs=[row(D_MODEL, 0), row(D_BRANCH, 0), row(D_BRANCH, 0), row(2 * D_BRANCH, 0),
                   resident, full(1, D_BRANCH), full(1, D_BRANCH),
                   full(1, LANES)],
        out_shape=[tok, tok, tok, jax.ShapeDtypeStruct((t, 2 * D_BRANCH), BF16),
                   jax.ShapeDtypeStruct((2 * D_BRANCH, D_MODEL), F32),
                   jax.ShapeDtypeStruct((1, D_BRANCH), F32), jax.ShapeDtypeStruct((1, D_BRANCH), F32),
                   jax.ShapeDtypeStruct((1, LANES), F32)],
        compiler_params=_params(1),
    )(o_sb, y_ssd, proj, proj, x2, target, sb_w, ssd_w, w_out_b)


_DPROJ_FIRST = (0, 1, 2, 3, 5)
_DPROJ_BLOCKS = (1, 1, 1, 2, 2)


def _in_proj_bwd_x(d_parts, w_in_t, x2, d_out, norm_w, tm, slabs=()):
    t = x2.shape[0]
    n_parts = len(d_parts)

    def body(*refs):
        dp_refs = refs[:n_parts]
        w_ref, x_ref, dout_ref, nw_ref, gx_ref, gnw_ref = refs[n_parts:]

        @pl.when(pl.program_id(0) == 0)
        def _():
            gnw_ref[...] = jnp.zeros_like(gnw_ref)

        d_hn = None
        for p in range(n_parts):
            rows = slice(_DPROJ_FIRST[p] * 1024, (_DPROJ_FIRST[p] + _DPROJ_BLOCKS[p]) * 1024)
            term = _dot(dp_refs[p][...], w_ref[rows, :])
            d_hn = term if d_hn is None else d_hn + term
        xf = x_ref[...]
        r = lax.rsqrt(jnp.mean(xf * xf, axis=1, keepdims=True) + EPS)
        xhat = xf * r
        g = d_hn * nw_ref[...]
        m = jnp.mean(g * xhat, axis=1, keepdims=True)
        gx_ref[...] = dout_ref[...] + r * (g - xhat * m)
        gnw_ref[...] += jnp.sum(d_hn * xhat, axis=0, keepdims=True)

    row = lambda w: pl.BlockSpec((tm, w), lambda i: (i, 0))
    return _call_with_exchange(
        body, (*d_parts, w_in_t, x2, d_out, norm_w), slabs, (True,) * len(slabs),
        name="in_proj_bwd_x", grid=(t // tm,),
        in_specs=[row(1024 * _DPROJ_BLOCKS[p]) for p in range(n_parts)] + [
                  pl.BlockSpec((D_IN_PAD, D_MODEL), lambda i: (0, 0), pipeline_mode=pl.Buffered(1)),
                  row(D_MODEL), row(D_MODEL), pl.BlockSpec((1, D_MODEL), lambda i: (0, 0))],
        out_specs=[row(D_MODEL), pl.BlockSpec((1, D_MODEL), lambda i: (0, 0))],
        out_shape=[jax.ShapeDtypeStruct((t, D_MODEL), F32), jax.ShapeDtypeStruct((1, D_MODEL), F32)])


def _in_proj_bwd_w(hn, d_parts, tm):
    t = hn.shape[0]
    n_parts = len(d_parts)

    def body(hn_ref, *refs):
        dp_refs, gw_ref = refs[:n_parts], refs[n_parts]

        @pl.when(pl.program_id(0) == 0)
        def _():
            gw_ref[...] = jnp.zeros_like(gw_ref)

        hnt = hn_ref[...].astype(F32).T.astype(BF16)
        for p in range(n_parts):
            cols = slice(_DPROJ_FIRST[p] * 1024, (_DPROJ_FIRST[p] + _DPROJ_BLOCKS[p]) * 1024)
            gw_ref[:, cols] += _dot(hnt, dp_refs[p][...])

    return pl.pallas_call(
        body, name="in_proj_bwd_w",
        grid=(t // tm,),
        in_specs=[pl.BlockSpec((tm, D_MODEL), lambda i: (i, 0))]
                 + [pl.BlockSpec((tm, 1024 * _DPROJ_BLOCKS[p]), lambda i: (i, 0))
                    for p in range(n_parts)],
        out_specs=pl.BlockSpec((D_MODEL, D_IN_PAD), lambda i: (0, 0), pipeline_mode=pl.Buffered(1)),
        out_shape=jax.ShapeDtypeStruct((D_MODEL, D_IN_PAD), F32),
        compiler_params=_params(1),
    )(hn, *d_parts)


def _adamw(parts, w, m, v, tr, name):
    _, rows, cols = w.shape
    c1 = 1.0 - ADAM_B1 ** ADAM_STEP
    c2 = 1.0 - ADAM_B2 ** ADAM_STEP

    def body(p_ref, w_ref, m_ref, v_ref, g_ref, d_ref, nm_ref, nv_ref):
        g = p_ref[0].astype(F32)
        for s in range(1, N_DEV):
            g = g + p_ref[s].astype(F32)
        nm = ADAM_B1 * m_ref[0] + (1.0 - ADAM_B1) * g
        nv = ADAM_B2 * v_ref[0] + (1.0 - ADAM_B2) * (g * g)
        g_ref[0] = g
        nm_ref[0] = nm
        nv_ref[0] = nv
        d_ref[0] = -ADAM_LR * ((nm / c1) / (jnp.sqrt(nv / c2) + ADAM_EPS) + ADAM_WD * w_ref[0])

    blk = pl.BlockSpec((1, tr, cols), lambda i: (0, i, 0))
    shape = jax.ShapeDtypeStruct((1, rows, cols), F32)
    return pl.pallas_call(
        body, name=name,
        grid=(rows // tr,),
        in_specs=[pl.BlockSpec((N_DEV, tr, cols), lambda i: (0, i, 0)), blk, blk, blk],
        out_specs=[blk, blk, blk, blk],
        out_shape=[shape, shape, shape, shape],
        compiler_params=_params(1),
    )(parts, w, m, v)


def _mesh_place():
    x, y, c = lax.axis_index("x"), lax.axis_index("y"), lax.axis_index("c")
    return x, y, c, 4 * x + 2 * y + c


def _peer(x, y, c, k):
    px = 1 - x if k & 4 else x
    py = 1 - y if k & 2 else y
    pc = 1 - c if k & 1 else c
    return (px, py, pc), 4 * px + 2 * py + pc


def _exchange(srcs, scatter, name):
    n = len(srcs)

    def body(*refs):
        copies = _exchange_copies(refs[:n], refs[n:2 * n], scatter, *refs[2 * n:])
        _exchange_start(copies)
        _exchange_wait(copies)

    return pl.pallas_call(
        body, name=name,
        in_specs=[_ANY] * n, out_specs=[_ANY] * n, out_shape=_exchange_shapes(srcs, scatter),
        scratch_shapes=_exchange_sems(n),
    )(*srcs)


def _call_with_exchange(body, operands, srcs, scatter, *, name, grid, in_specs, out_specs,
                        out_shape, scratch_shapes=()):
    n_in, n_out, n_scr, n_x = len(in_specs), len(out_specs), len(scratch_shapes), len(srcs)
    params = _params(len(grid))
    if not n_x:
        return pl.pallas_call(body, name=name, grid=grid, in_specs=list(in_specs),
                              out_specs=list(out_specs), out_shape=list(out_shape),
                              scratch_shapes=list(scratch_shapes), compiler_params=params)(*operands)

    def wrapped(*refs):
        ins, refs = refs[:n_in], refs[n_in:]
        x_src, refs = refs[:n_x], refs[n_x:]
        outs, refs = refs[:n_out], refs[n_out:]
        x_dst, refs = refs[:n_x], refs[n_x:]
        scratch, sems = refs[:n_scr], refs[n_scr:]
        ids = [pl.program_id(a) for a in range(len(grid))]
        first = functools.reduce(jnp.logical_and, [i == 0 for i in ids])
        last = functools.reduce(jnp.logical_and, [i == n - 1 for i, n in zip(ids, grid)])

        @pl.when(first)
        def _():
            _exchange_start(_exchange_copies(x_src, x_dst, scatter, *sems))

        body(*ins, *outs, *scratch)

        @pl.when(last)
        def _():
            _exchange_wait(_exchange_copies(x_src, x_dst, scatter, *sems))

    return pl.pallas_call(
        wrapped, name=name, grid=grid,
        in_specs=list(in_specs) + [_ANY] * n_x, out_specs=list(out_specs) + [_ANY] * n_x,
        out_shape=list(out_shape) + _exchange_shapes(srcs, scatter),
        scratch_shapes=list(scratch_shapes) + _exchange_sems(n_x), compiler_params=params,
    )(*operands, *srcs)


def _gather_two_level(shard, name):
    def body(x_ref, out_ref, send_sems, recv_sems, local_sem):
        x, y, c, me = _mesh_place()
        sibling = (x, y, 1 - c)
        chips = [(1 - x, y), (x, 1 - y), (1 - x, 1 - y)]

        def slab(px, py, pc):
            return out_ref.at[4 * px + 2 * py + pc]

        def copy(k, block, to, src=None):
            return pltpu.make_async_remote_copy(
                src_ref=slab(*block) if src is None else src, dst_ref=slab(*block),
                send_sem=send_sems.at[k], recv_sem=recv_sems.at[k],
                device_id=to, device_id_type=pl.DeviceIdType.MESH)

        mine = pltpu.make_async_copy(x_ref, slab(x, y, c), local_sem)
        mine.start()
        first = [copy(0, (x, y, c), sibling, src=x_ref)]
        first += [copy(1 + j, (x, y, c), (*chip, c), src=x_ref) for j, chip in enumerate(chips)]
        for cp in first:
            cp.start()
        passed = [copy(4 + j, (*chip, c), sibling) for j, chip in enumerate(chips)]
        for j, chip in enumerate(chips):
            copy(1 + j, (*chip, c), (x, y, c)).wait_recv()
            passed[j].start()
        copy(0, sibling, (x, y, c)).wait_recv()
        for j, chip in enumerate(chips):
            copy(4 + j, (*chip, 1 - c), (x, y, c)).wait_recv()
        for cp in first + passed:
            cp.wait_send()
        mine.wait()

    return pl.pallas_call(
        body, name=name,
        in_specs=[_ANY], out_specs=_ANY,
        out_shape=jax.ShapeDtypeStruct((N_DEV,) + shard.shape, shard.dtype),
        scratch_shapes=[pltpu.SemaphoreType.DMA((N_DEV - 1,)), pltpu.SemaphoreType.DMA((N_DEV - 1,)),
                        pltpu.SemaphoreType.DMA],
    )(shard)


_ANY = pl.BlockSpec(memory_space=pl.ANY)


def _exchange_shapes(srcs, scatter):
    return [jax.ShapeDtypeStruct(s.shape if sc else (N_DEV,) + s.shape, s.dtype)
            for s, sc in zip(srcs, scatter)]


def _exchange_sems(n):
    return [pltpu.SemaphoreType.DMA((n * (N_DEV - 1),)),
            pltpu.SemaphoreType.DMA((n * (N_DEV - 1),)),
            pltpu.SemaphoreType.DMA((n,))]


def _exchange_copies(src_refs, dst_refs, scatter, send_sems, recv_sems, loc_sems):
    n = len(src_refs)
    x, y, c, me = _mesh_place()

    def src_of(i, idx):
        return src_refs[i].at[idx] if scatter[i] else src_refs[i]

    local = [pltpu.make_async_copy(src_of(i, me), dst_refs[i].at[me], loc_sems.at[i])
             for i in range(n)]
    sends, recvs = [], []
    for k in range(1, N_DEV):
        peer, pidx = _peer(x, y, c, k)
        for i in range(n):
            s = i * (N_DEV - 1) + k - 1
            for dst_slab, group in ((me, sends), (pidx, recvs)):
                group.append(pltpu.make_async_remote_copy(
                    src_ref=src_of(i, pidx), dst_ref=dst_refs[i].at[dst_slab],
                    send_sem=send_sems.at[s], recv_sem=recv_sems.at[s],
                    device_id=peer, device_id_type=pl.DeviceIdType.MESH))
    return local, sends, recvs


def _exchange_start(copies):
    local, sends, _ = copies
    for cp in local + sends:
        cp.start()


def _exchange_wait(copies):
    local, sends, recvs = copies
    for cp in recvs:
        cp.wait_recv()
    for cp in sends:
        cp.wait_send()
    for cp in local:
        cp.wait()


def _pad_lanes(v, width=LANES):
    return jnp.pad(v, ((0, 0), (0, width - v.shape[1])))


def _local_step(x, target, norm_w, w_in_b, q_norm_w, k_norm_w, conv_w, conv_b, dt_bias, a_log,
                d_skip, sb_norm_w, ssd_norm_w, w_out_b, tm=256, tq=512, tmid=256, blk=ATT_BLK,
                scatter=False, w_in_t=None):
    nb, seq, _ = x.shape
    t = nb * seq
    x2 = x.reshape(t, D_MODEL)
    tg2 = target.reshape(t, D_MODEL)
    qw2 = jnp.tile(q_norm_w, (1, 2))
    kw2 = jnp.tile(k_norm_w, (1, 2))
    dtb, alog, dsk = _pad_lanes(dt_bias), _pad_lanes(a_log), _pad_lanes(d_skip)

    if w_in_t is None:
        w_in_t = w_in_b.T
    if scatter:
        proj, hn, wout_all, cw_all = _in_proj(x2, norm_w, w_in_b, tm, (w_out_b, conv_w))
        w_out_b = wout_all.reshape(2 * D_BRANCH, D_MODEL)
        conv_w = jnp.transpose(cw_all, (1, 0, 2)).reshape(CONV_TAPS, D_CONV)
    else:
        proj, hn = _in_proj(x2, norm_w, w_in_b, tm)
    qs, kn, vb, kt = _qk_prep(proj, qw2, kw2, nb, seq, tq)
    o_sb, sb_tot, sb_low = _attn_fwd(qs, kn, vb, nb, seq, blk)
    y_ssd, states = _ssd_fwd(proj, conv_w, conv_b, dtb, alog, dsk, nb, seq)
    d_out, d_osb, d_y, d_z, g_wout, g_sbw, g_ssdw, loss = _mid(
        o_sb, y_ssd, proj, x2, tg2, sb_norm_w, ssd_norm_w, w_out_b, tmid)
    dqs, dkn, dvh = _attn_bwd(qs, kn, kt, vb, sb_tot, sb_low, d_osb, nb, seq, blk)
    dq_raw, dk_raw, dv_raw, g_qw, g_kw = _qk_bwd(proj, dqs, dkn, dvh, qw2, kw2, nb, seq, tq)
    wout_slabs = (g_wout.reshape(N_DEV, 2 * D_BRANCH // N_DEV, D_MODEL).astype(BF16),)
    d_xbc, g_cw, g_cb, g_dtb, g_alog, g_dsk, *moved = _ssd_bwd(
        proj, d_y, states, conv_w, conv_b, dtb, alog, dsk, nb, seq, wout_slabs if scatter else ())
    d_parts = [dq_raw, dk_raw, dv_raw, d_z, d_xbc]
    g_win = _in_proj_bwd_w(hn, d_parts, tm)[:, :D_IN]
    g_cw = g_cw[:CONV_TAPS]
    if scatter:
        g_wout, = moved
        grad_x, g_nw, g_win, g_cw = _in_proj_bwd_x(
            d_parts, w_in_t, x2, d_out, norm_w, tm, _grad_slabs(g_win, g_cw))
    else:
        grad_x, g_nw = _in_proj_bwd_x(d_parts, w_in_t, x2, d_out, norm_w, tm)

    small = dict(
        norm_w=g_nw,
        q_norm_w=g_qw[:, :HEAD_DIM] + g_qw[:, HEAD_DIM:],
        k_norm_w=g_kw[:, :HEAD_DIM] + g_kw[:, HEAD_DIM:],
        conv_b=g_cb, dt_bias=g_dtb[:, :N_HEADS], A_log=g_alog[:, :N_HEADS],
        D_skip=g_dsk[:, :N_HEADS], sb_norm_w=g_sbw, ssd_norm_w=g_ssdw)
    return loss[0, 0], grad_x.reshape(nb, seq, D_MODEL), g_win, g_wout, g_cw, small


def _grad_slabs(g_win, g_cw):
    w_sh = D_IN // N_DEV
    c_sh = D_CONV // N_DEV
    return (jnp.transpose(g_win.reshape(D_MODEL, N_DEV, w_sh), (1, 0, 2)).astype(BF16),
            jnp.pad(jnp.transpose(g_cw.reshape(CONV_TAPS, N_DEV, c_sh), (1, 0, 2)),
                    ((0, 0), (0, 8 - CONV_TAPS), (0, 0))))


_SMALL = ("norm_w", "q_norm_w", "k_norm_w", "conv_b", "dt_bias", "A_log", "D_skip",
          "sb_norm_w", "ssd_norm_w")


def _pack_small(vals):
    flat = jnp.concatenate([_pad_lanes(vals[n], -(-vals[n].shape[1] // LANES) * LANES)
                            for n in _SMALL], axis=1)
    return jnp.pad(flat, ((0, 0), (0, 48 * LANES - flat.shape[1]))).reshape(48, LANES)


def _unpack_small(packed, like):
    out, r = {}, 0
    for n in _SMALL:
        width = like[n].shape[1]
        nr = -(-width // LANES)
        out[n] = packed[r:r + nr].reshape(1, nr * LANES)[:, :width]
        r += nr
    return out


def kernel(x, norm_w, w_in, q_norm_w, k_norm_w, conv_w, conv_b, dt_bias, A_log, D_skip, sb_norm_w, ssd_norm_w, w_out, loss_target, m_norm_w, m_w_in, m_q_norm_w, m_k_norm_w, m_conv_w, m_conv_b, m_dt_bias, m_A_log, m_D_skip, m_sb_norm_w, m_ssd_norm_w, m_w_out, v_norm_w, v_w_in, v_q_norm_w, v_k_norm_w, v_conv_w, v_conv_b, v_dt_bias, v_A_log, v_D_skip, v_sb_norm_w, v_ssd_norm_w, v_w_out):
    win_all = _gather_two_level(w_in[0].astype(BF16), "gather_w_in")
    w_in_b = jnp.pad(jnp.transpose(win_all, (1, 0, 2)).reshape(D_MODEL, D_IN),
                     ((0, 0), (0, D_IN_PAD - D_IN)))
    w_in_t = jnp.pad(jnp.transpose(win_all, (0, 2, 1)).reshape(D_IN, D_MODEL),
                     ((0, D_IN_PAD - D_IN), (0, 0)))

    loss, grad_x, win_parts, wout_parts, cw_parts, g_small = _local_step(
        x, loss_target, norm_w, w_in_b, q_norm_w, k_norm_w, conv_w[0], conv_b, dt_bias, A_log,
        D_skip, sb_norm_w, ssd_norm_w, w_out[0].astype(BF16), scatter=True, w_in_t=w_in_t)
    small_parts, = _exchange([_pack_small(g_small)], [False], "gather_small_grads")

    small_w = dict(norm_w=norm_w, q_norm_w=q_norm_w, k_norm_w=k_norm_w, conv_b=conv_b,
                   dt_bias=dt_bias, A_log=A_log, D_skip=D_skip, sb_norm_w=sb_norm_w,
                   ssd_norm_w=ssd_norm_w)
    small_m = dict(norm_w=m_norm_w, q_norm_w=m_q_norm_w, k_norm_w=m_k_norm_w, conv_b=m_conv_b,
                   dt_bias=m_dt_bias, A_log=m_A_log, D_skip=m_D_skip, sb_norm_w=m_sb_norm_w,
                   ssd_norm_w=m_ssd_norm_w)
    small_v = dict(norm_w=v_norm_w, q_norm_w=v_q_norm_w, k_norm_w=v_k_norm_w, conv_b=v_conv_b,
                   dt_bias=v_dt_bias, A_log=v_A_log, D_skip=v_D_skip, sb_norm_w=v_sb_norm_w,
                   ssd_norm_w=v_ssd_norm_w)

    pad8 = lambda a: jnp.pad(a, ((0, 0), (0, 8 - CONV_TAPS), (0, 0)))
    r_win = _adamw(win_parts, w_in, m_w_in, v_w_in, 128, "adamw_w_in")
    r_wout = _adamw(wout_parts, w_out, m_w_out, v_w_out, 128, "adamw_w_out")
    r_cw = _adamw(cw_parts, pad8(conv_w), pad8(m_conv_w), pad8(v_conv_w), 8, "adamw_conv_w")
    r_small = _adamw(small_parts, _pack_small(small_w)[None], _pack_small(small_m)[None],
                     _pack_small(small_v)[None], 48, "adamw_small")

    loss = lax.psum(loss, ("x", "y", "c"))
    res = {"w_in": r_win, "w_out": r_wout, "conv_w": [a[:, :CONV_TAPS] for a in r_cw]}
    unpacked = [_unpack_small(a[0], small_w) for a in r_small]
    for n in _SMALL:
        res[n] = [u[n] for u in unpacked]
    order = ("norm_w", "w_in", "q_norm_w", "k_norm_w", "conv_w", "conv_b", "dt_bias", "A_log",
             "D_skip", "sb_norm_w", "ssd_norm_w", "w_out")
    outs = [loss, grad_x]
    for kind in range(4):
        outs += [res[n][kind] for n in order]
    return tuple(outs)
```

```python
import functools
import math

import jax
import jax.numpy as jnp
from jax import lax
from jax.experimental import pallas as pl
from jax.experimental.pallas import tpu as pltpu

F32 = jnp.float32
BF16 = jnp.bfloat16

D_MODEL = 1024
N_HEADS = 16
HEAD_DIM = 64
N_PAIRS = N_HEADS // 2
D_BRANCH = 1024
N_GROUPS = 2
HEADS_PER_GROUP = 8
D_STATE = 128
GROUP_W = HEADS_PER_GROUP * HEAD_DIM
D_BC = 2 * N_GROUPS * D_STATE
D_CONV = D_BRANCH + D_BC
D_IN = 6672
D_IN_PAD = 7168
N_COLBLK = D_IN_PAD // 1024
COL_XS = 5120
COL_BC = 6144
COL_DT = 6656
EPS = 1e-6
CONV_TAPS = 4
N_DEV = 8

LANES = 128
SSD_CHUNK = 128
ATT_BLK = 256
ATT_HEADS = 4
ATT_W = ATT_HEADS * HEAD_DIM
N_ATT_GROUPS = N_HEADS // ATT_HEADS
EXP_UNDERFLOW = -105.0
VMEM_LIMIT = 56 * 1024 * 1024

ADAM_LR = 0.001
ADAM_B1 = 0.9
ADAM_B2 = 0.999
ADAM_EPS = 1e-08
ADAM_WD = 0.01
ADAM_STEP = 10

_NT = (((1,), (1,)), ((), ()))
_TN = (((0,), (0,)), ((), ()))


def _params(n_grid):
    return pltpu.CompilerParams(dimension_semantics=("arbitrary",) * n_grid,
                                vmem_limit_bytes=VMEM_LIMIT)


def _dot(a, b, dims=None, precision=None):
    if dims is None:
        return jnp.dot(a, b, preferred_element_type=F32, precision=precision)
    return lax.dot_general(a, b, dims, preferred_element_type=F32, precision=precision)


def _sigmoid(x):
    return 1.0 / (1.0 + jnp.exp(-x))


def _softplus(x):
    return jnp.maximum(x, 0.0) + jnp.log(1.0 + jnp.exp(-jnp.abs(x)))


def _split_bf16(x):
    hi = x.astype(BF16)
    lo = (x - hi.astype(F32)).astype(BF16)
    return hi, lo


def _lane_iota(shape):
    return lax.broadcasted_iota(jnp.int32, shape, len(shape) - 1)


def _row_iota(shape):
    return lax.broadcasted_iota(jnp.int32, shape, len(shape) - 2)


def _pair_sum(x):
    r = lax.broadcasted_iota(jnp.int32, (LANES, LANES), 0)
    c = lax.broadcasted_iota(jnp.int32, (LANES, LANES), 1)
    same_head = jnp.where(r // HEAD_DIM == c // HEAD_DIM, 1.0, 0.0).astype(BF16)
    hi, lo = _split_bf16(x)
    return _dot(hi, same_head) + _dot(lo, same_head)


def _pair_head(x, a):
    lane = _lane_iota(x.shape)
    mine = (lane >= a * HEAD_DIM) & (lane < (a + 1) * HEAD_DIM)
    return jnp.where(mine, x, jnp.zeros_like(x))


def _head_expand():
    r = lax.broadcasted_iota(jnp.int32, (LANES, D_BRANCH), 0)
    c = lax.broadcasted_iota(jnp.int32, (LANES, D_BRANCH), 1)
    return jnp.where(c // HEAD_DIM == r, 1.0, 0.0).astype(BF16)


def _in_proj(x2, norm_w, w_in_b, tm, shards=()):
    t = x2.shape[0]

    def body(x_ref, nw_ref, w_ref, proj_ref, hn_ref):
        xf = x_ref[...]
        r = lax.rsqrt(jnp.mean(xf * xf, axis=1, keepdims=True) + EPS)
        hn = (xf * r * nw_ref[...]).astype(BF16)
        hn_ref[...] = hn
        for j in range(N_COLBLK):
            cols = slice(j * 1024, (j + 1) * 1024)
            proj_ref[:, cols] = _dot(hn, w_ref[:, cols])

    return _call_with_exchange(
        body, (x2, norm_w, w_in_b), shards, (False,) * len(shards), name="in_proj",
        grid=(t // tm,),
        in_specs=[pl.BlockSpec((tm, D_MODEL), lambda i: (i, 0)),
                  pl.BlockSpec((1, D_MODEL), lambda i: (0, 0)),
                  pl.BlockSpec((D_MODEL, D_IN_PAD), lambda i: (0, 0), pipeline_mode=pl.Buffered(1))],
        out_specs=[pl.BlockSpec((tm, D_IN_PAD), lambda i: (i, 0)),
                   pl.BlockSpec((tm, D_MODEL), lambda i: (i, 0))],
        out_shape=[jax.ShapeDtypeStruct((t, D_IN_PAD), F32),
                   jax.ShapeDtypeStruct((t, D_MODEL), BF16)])


def _qk_prep(proj, qw2, kw2, nb, seq, tq):
    nl = seq // tq
    scale = 1.0 / math.sqrt(HEAD_DIM)

    def body(q_ref, k_ref, v_ref, qw_ref, kw_ref, qs_ref, kn_ref, vb_ref, kt_ref):
        def norm(x, w):
            r = lax.rsqrt(_pair_sum(x * x) * (1.0 / HEAD_DIM) + EPS)
            return x * r * w

        vb_ref[...] = v_ref[...].astype(BF16)
        for p in range(N_PAIRS):
            cols = slice(p * LANES, (p + 1) * LANES)
            kn = norm(k_ref[:, cols], kw_ref[...])
            qs_ref[:, cols] = (norm(q_ref[:, cols], qw_ref[...]) * scale).astype(BF16)
            kn_ref[:, cols] = kn.astype(BF16)
            kt_ref[0, p] = kn.T.astype(BF16)

    tok_shape = jax.ShapeDtypeStruct((nb * seq, D_BRANCH), BF16)
    tok = lambda blk: pl.BlockSpec((tq, D_BRANCH), lambda b, i: (b * nl + i, blk))
    vec = pl.BlockSpec((1, LANES), lambda b, i: (0, 0))
    return pl.pallas_call(
        body, name="qk_prep",
        grid=(nb, nl),
        in_specs=[tok(0), tok(1), tok(2), vec, vec],
        out_specs=[tok(0), tok(0), tok(0),
                   pl.BlockSpec((1, N_PAIRS, LANES, tq), lambda b, i: (b, 0, 0, i))],
        out_shape=[tok_shape, tok_shape, tok_shape,
                   jax.ShapeDtypeStruct((nb, N_PAIRS, LANES, seq), BF16)],
        compiler_params=_params(2),
    )(proj, proj, proj, qw2, kw2)


def _attn_fwd(qs, kn, vb, nb, seq, blk):
    nq = seq // blk

    def body(q_ref, k_ref, v_ref, o_ref, tot_ref, low_ref, kmax_ref):
        qi = pl.program_id(2)
        r_i = lax.broadcasted_iota(jnp.int32, (blk, blk), 0)
        c_i = lax.broadcasted_iota(jnp.int32, (blk, blk), 1)
        csum = jnp.where(r_i >= c_i, 1.0, 0.0).astype(BF16)
        causal = c_i < r_i
        heads = range(ATT_HEADS)

        head = _pair_head

        @pl.when(qi == 0)
        def _():
            kk = k_ref[...].astype(F32)
            for a in heads:
                ksq = jnp.sum(head(kk * kk, a), axis=1, keepdims=True)
                kmax_ref[a] = jnp.full((8, LANES), jnp.max(ksq))

        q_pair = q_ref[...]
        qf = q_pair.astype(F32)
        q_head = [head(q_pair, a) for a in heads]
        zmax = []
        for a in heads:
            qsq = jnp.sum(head(qf * qf, a), axis=1, keepdims=True)
            zmax.append(1.01 * jnp.sqrt(qsq * kmax_ref[a][0:1, 0:1]) + 0.01)

        def exhausted(run):
            top = functools.reduce(jnp.maximum, [jnp.max(run[a] + zmax[a]) for a in heads])
            return top < EXP_UNDERFLOW

        def sweep(blocks, run, acc):
            offs = [pl.multiple_of(j * blk, blk) for j, _, _ in blocks]
            z = [[_dot(q_head[a], k_ref[pl.ds(off, blk), :], _NT) for a in heads]
                 for off in offs]
            cl = []
            for (_, diag, valid), zb in zip(blocks, z):
                lkb = []
                for a in heads:
                    lk = -_softplus(zb[a])
                    if diag:
                        lk = jnp.where(causal, lk, 0.0)
                    if valid is not None:
                        lk = jnp.where(valid, lk, 0.0)
                    lkb.append(lk.astype(BF16))
                cl.append([_dot(lkb[a], csum) for a in heads])
            for (_, diag, valid), zb, clb, off in zip(blocks, z, cl, offs):
                w = []
                for a in heads:
                    wa = jnp.exp(zb[a] + clb[a] + run[a])
                    if diag:
                        wa = jnp.where(causal, wa, 0.0)
                    if valid is not None:
                        wa = jnp.where(valid, wa, 0.0)
                    w.append(wa.astype(BF16))
                run = [run[a] + clb[a][:, 0:1] for a in heads]
                v_blk = v_ref[pl.ds(off, blk), :]
                for a in heads:
                    acc = acc + _dot(w[a], head(v_blk, a))
            return run, acc

        run = [jnp.zeros((blk, 1), F32)] * ATT_HEADS
        acc = jnp.zeros((blk, ATT_W), F32)
        run, acc = sweep([(qi, True, None), (jnp.maximum(qi - 1, 0), False, qi >= 1)], run, acc)
        low = jnp.maximum(qi - 1, 0)

        def more(carry):
            low, done, _, _ = carry
            return (low > 0) & jnp.logical_not(done)

        def pair(carry):
            low, _, run, acc = carry
            run, acc = sweep([(low - 1, False, None), (jnp.maximum(low - 2, 0), False, low >= 2)],
                             run, acc)
            return jnp.maximum(low - 2, 0), exhausted(run), run, acc

        low, _, run, acc = lax.while_loop(more, pair, (low, exhausted(run), run, acc))
        low_ref[pl.program_id(0) * N_ATT_GROUPS + pl.program_id(1), qi] = low.astype(F32)
        o_ref[...] = acc
        for a in heads:
            as_row = jnp.sum(jnp.where(r_i == c_i, run[a], 0.0), axis=0, keepdims=True)
            tot_ref[0, a, 0] = jnp.broadcast_to(as_row, (8, blk))

    return pl.pallas_call(
        body, name="sb_attn_fwd",
        grid=(nb, N_ATT_GROUPS, nq),
        in_specs=[pl.BlockSpec((blk, ATT_W), lambda b, h, i: (b * nq + i, h)),
                  pl.BlockSpec((seq, ATT_W), lambda b, h, i: (b, h)),
                  pl.BlockSpec((seq, ATT_W), lambda b, h, i: (b, h))],
        out_specs=[pl.BlockSpec((blk, ATT_W), lambda b, h, i: (b * nq + i, h)),
                   pl.BlockSpec((1, ATT_HEADS, 1, 8, blk), lambda b, h, i: (b, h, i, 0, 0)),
                   pl.BlockSpec(memory_space=pltpu.SMEM)],
        out_shape=[jax.ShapeDtypeStruct((nb * seq, D_BRANCH), F32),
                   jax.ShapeDtypeStruct((nb, N_HEADS, nq, 8, blk), F32),
                   jax.ShapeDtypeStruct((nb * N_ATT_GROUPS, nq), F32)],
        scratch_shapes=[pltpu.VMEM((ATT_HEADS, 8, LANES), F32)],
        compiler_params=_params(3),
    )(qs, kn, vb)


def _attn_bwd(qs, kn, kt, vb, tot, low, d_o, nb, seq, blk):
    nq = seq // blk

    def body(q_ref, k_ref, kt_ref, v_ref, tot_ref, low_ref, do_ref, dq_ref, dk_ref, dv_ref):
        qi = pl.program_id(2)

        @pl.when(qi == 0)
        def _():
            dk_ref[...] = jnp.zeros_like(dk_ref)
            dv_ref[...] = jnp.zeros_like(dv_ref)

        r_i = lax.broadcasted_iota(jnp.int32, (blk, blk), 0)
        c_i = lax.broadcasted_iota(jnp.int32, (blk, blk), 1)
        before = jnp.where(c_i < r_i, 1.0, 0.0).astype(BF16)
        upto = jnp.where(c_i <= r_i, 1.0, 0.0).astype(BF16)
        causal = r_i < c_i

        heads = range(ATT_HEADS)
        q_head = [_pair_head(q_ref[...], a) for a in heads]
        d_ob = [_pair_head(do_ref[...].astype(BF16), a) for a in heads]
        total = [tot_ref[0, a, 0][0:1, :] for a in heads]

        def sweep(blocks, lsum, esum, dqt):
            def keep(x, diag, valid):
                if diag:
                    x = jnp.where(causal, x, 0.0)
                if valid is not None:
                    x = jnp.where(valid, x, 0.0)
                return x

            offs = [pl.multiple_of(j * blk, blk) for j, _, _ in blocks]
            zt = [[_dot(k_ref[pl.ds(off, blk), :], q_head[a], _NT) for a in heads]
                  for off in offs]
            dwt = [[_dot(v_ref[pl.ds(off, blk), :], d_ob[a], _NT) for a in heads]
                   for off in offs]
            sp, lk, lpre = [], [], []
            for (_, diag, valid), ztb in zip(blocks, zt):
                sp.append([_softplus(ztb[a]) for a in heads])
                lk.append([keep(-sp[-1][a], diag, valid).astype(BF16) for a in heads])
                lpre.append([_dot(before, lk[-1][a]) for a in heads])
            wt, et, epre = [], [], []
            for i, (_, diag, valid) in enumerate(blocks):
                wt.append([keep(jnp.exp(zt[i][a] + (total[a] - lsum[a] - lpre[i][a])), diag, valid)
                           for a in heads])
                et.append([dwt[i][a] * wt[i][a] for a in heads])
                split = [_split_bf16(et[i][a]) for a in heads]
                epre.append([_dot(upto, split[a][0]) + _dot(upto, split[a][1]) for a in heads])
                lsum = [lsum[a] + lpre[i][a][blk - 1:blk, :] + lk[i][a][blk - 1:blk, :]
                        for a in heads]
            for i, (_, diag, valid) in enumerate(blocks):
                dzb = [keep(et[i][a] - jnp.exp(zt[i][a] - sp[i][a]) * (esum[a] + epre[i][a]),
                            diag, valid).astype(BF16) for a in heads]
                esum = [esum[a] + epre[i][a][blk - 1:blk, :] for a in heads]
                dk_ref[pl.ds(offs[i], blk), :] += functools.reduce(
                    jnp.add, [_dot(dzb[a], q_head[a]) for a in heads])
                dv_ref[pl.ds(offs[i], blk), :] += functools.reduce(
                    jnp.add, [_dot(wt[i][a].astype(BF16), d_ob[a]) for a in heads])
                dqt = [dqt[a] + _dot(kt_ref[0, a // 2, (a % 2) * HEAD_DIM:(a % 2 + 1) * HEAD_DIM,
                                            pl.ds(offs[i], blk)], dzb[a]) for a in heads]
            return lsum, esum, dqt

        row = [jnp.zeros((1, blk), F32)] * ATT_HEADS
        dqt = [jnp.zeros((HEAD_DIM, blk), F32)] * ATT_HEADS
        low = low_ref[pl.program_id(0) * N_ATT_GROUPS + pl.program_id(1), qi].astype(jnp.int32)
        low = jnp.clip(low, 0, jnp.maximum(qi - 1, 0))

        def pair(carry):
            j, lsum, esum, dqt = carry
            return (j + 2,) + sweep([(j, False, None), (j + 1, False, j + 1 < qi - 1)],
                                    lsum, esum, dqt)

        _, lsum, esum, dqt = lax.while_loop(lambda c: c[0] < qi - 1, pair, (low, row, row, dqt))
        _, _, dqt = sweep([(jnp.maximum(qi - 1, 0), False, qi >= 1), (qi, True, None)],
                          lsum, esum, dqt)
        dq_ref[...] = jnp.concatenate(dqt, axis=0).T

    seq_blk = pl.BlockSpec((seq, ATT_W), lambda b, h, i: (b, h))
    tok = pl.BlockSpec((blk, ATT_W), lambda b, h, i: (b * nq + i, h))
    tok_shape = jax.ShapeDtypeStruct((nb * seq, D_BRANCH), F32)
    return pl.pallas_call(
        body, name="sb_attn_bwd",
        grid=(nb, N_ATT_GROUPS, nq),
        in_specs=[tok, seq_blk,
                  pl.BlockSpec((1, ATT_HEADS // 2, LANES, seq), lambda b, h, i: (b, h, 0, 0)),
                  seq_blk,
                  pl.BlockSpec((1, ATT_HEADS, 1, 8, blk), lambda b, h, i: (b, h, i, 0, 0)),
                  pl.BlockSpec(memory_space=pltpu.SMEM),
                  tok],
        out_specs=[tok, seq_blk, seq_blk],
        out_shape=[tok_shape, tok_shape, tok_shape],
        compiler_params=_params(3),
    )(qs, kn, kt, vb, tot, low, d_o)


def _qk_bwd(proj, dqs, dkn, dvh, qw2, kw2, nb, seq, tq):
    nl = seq // tq
    scale = 1.0 / math.sqrt(HEAD_DIM)

    def body(q_ref, k_ref, dq_ref, dk_ref, dv_ref, qw_ref, kw_ref,
             dqr_ref, dkr_ref, dvr_ref, gq_ref, gk_ref):
        @pl.when((pl.program_id(0) == 0) & (pl.program_id(1) == 0))
        def _():
            gq_ref[...] = jnp.zeros_like(gq_ref)
            gk_ref[...] = jnp.zeros_like(gk_ref)

        def norm_bwd(x, w, dy):
            r = lax.rsqrt(_pair_sum(x * x) * (1.0 / HEAD_DIM) + EPS)
            xhat = x * r
            g = dy * w
            m = _pair_sum(g * xhat) * (1.0 / HEAD_DIM)
            return r * (g - xhat * m), jnp.sum(dy * xhat, axis=0, keepdims=True)

        dvr_ref[...] = dv_ref[...].astype(BF16)
        gq = jnp.zeros((1, LANES), F32)
        gk = jnp.zeros((1, LANES), F32)
        for p in range(N_PAIRS):
            cols = slice(p * LANES, (p + 1) * LANES)
            dqr, gq_p = norm_bwd(q_ref[:, cols], qw_ref[...], dq_ref[:, cols] * scale)
            dkr, gk_p = norm_bwd(k_ref[:, cols], kw_ref[...], dk_ref[:, cols])
            dqr_ref[:, cols] = dqr.astype(BF16)
            dkr_ref[:, cols] = dkr.astype(BF16)
            gq, gk = gq + gq_p, gk + gk_p
        gq_ref[...] += gq
        gk_ref[...] += gk

    tok = lambda blk: pl.BlockSpec((tq, D_BRANCH), lambda b, i: (b * nl + i, blk))
    vec = pl.BlockSpec((1, LANES), lambda b, i: (0, 0))
    tshape = jax.ShapeDtypeStruct((nb * seq, D_BRANCH), BF16)
    return pl.pallas_call(
        body, name="qk_bwd",
        grid=(nb, nl),
        in_specs=[tok(0), tok(1), tok(0), tok(0), tok(0), vec, vec],
        out_specs=[tok(0), tok(0), tok(0), vec, vec],
        out_shape=[tshape, tshape, tshape,
                   jax.ShapeDtypeStruct((1, LANES), F32), jax.ShapeDtypeStruct((1, LANES), F32)],
        compiler_params=_params(2),
    )(proj, proj, dqs, dkn, dvh, qw2, kw2)


def _shift_down(cur, prev, k):
    if k == 0:
        return cur
    rows = _row_iota(cur.shape)
    return jnp.where(rows < k, pltpu.roll(prev, k, axis=0), pltpu.roll(cur, k, axis=0))


def _shift_up(cur, nxt, k):
    if k == 0:
        return cur
    n = cur.shape[0]
    rows = _row_iota(cur.shape)
    return jnp.where(rows < n - k, pltpu.roll(cur, n - k, axis=0), pltpu.roll(nxt, n - k, axis=0))


def _conv_taps(cur, prev):
    return [_shift_down(cur, prev, CONV_TAPS - 1 - i) for i in range(CONV_TAPS)]


def _conv_pre(taps, w, b):
    out = b
    for i in range(CONV_TAPS):
        out = out + taps[i] * w[i:i + 1, :]
    return out


def _silu(x):
    return x * _sigmoid(x)


def _silu_and_grad(x):
    s = _sigmoid(x)
    return x * s, s * (1.0 + x * (1.0 - s))


def _dot01(x, m01, parts, dims=None, m_left=False):
    total, rest = None, x
    for i in range(parts):
        piece = rest.astype(BF16)
        if i + 1 < parts:
            rest = rest - piece.astype(F32)
        term = _dot(m01, piece, dims) if m_left else _dot(piece, m01, dims)
        total = term if total is None else total + term
    return total


def _chunk_decay(dt_raw, dtb, alog, expand, qc):
    dt = _softplus(dt_raw + dtb)
    d_a = dt * (-jnp.exp(alog))
    r_i = lax.broadcasted_iota(jnp.int32, (qc, qc), 0)
    c_i = lax.broadcasted_iota(jnp.int32, (qc, qc), 1)
    tril = r_i >= c_i
    a_cs = _dot01(d_a, jnp.where(tril, 1.0, 0.0).astype(BF16), 3, m_left=True)
    dt_x = _dot01(dt, expand, 3)
    acs_x = _dot01(a_cs, expand, 3)
    return dt, d_a, a_cs, dt_x, acs_x, tril


def _ssd_fwd(proj, conv_w, conv_b, dtb, alog, dskip, nb, seq):
    qc = SSD_CHUNK
    nc = seq // qc

    def body(xs_ref, bc_ref, dt_ref, cw_ref, cb_ref, dtb_ref, al_ref, ds_ref,
             y_ref, st_ref, pxs_ref, pbc_ref, state_ref):
        @pl.when(pl.program_id(1) == 0)
        def _():
            pxs_ref[...] = jnp.zeros_like(pxs_ref)
            pbc_ref[...] = jnp.zeros_like(pbc_ref)
            state_ref[...] = jnp.zeros_like(state_ref)

        expand = _head_expand()
        xs_raw = xs_ref[...]
        bc_raw = bc_ref[...]
        cw = cw_ref[...]
        cb = cb_ref[...]
        xs = _silu(_conv_pre(_conv_taps(xs_raw, pxs_ref[...]), cw[:, :D_BRANCH], cb[:, :D_BRANCH]))
        bc = _silu(_conv_pre(_conv_taps(bc_raw, pbc_ref[...]), cw[:, D_BRANCH:], cb[:, D_BRANCH:]))
        pxs_ref[...] = xs_raw
        pbc_ref[...] = bc_raw

        dt, d_a, a_cs, dt_x, acs_x, tril = _chunk_decay(
            dt_ref[...], dtb_ref[...], al_ref[...], expand, qc)
        a_cst = a_cs.T
        aend_x = acs_x[qc - 1:qc, :]
        ea_x = jnp.exp(acs_x)
        dec_x = jnp.exp(aend_x - acs_x)
        xt = xs * dt_x
        xtb = xt.astype(BF16)
        xdb = (xt * dec_x).astype(BF16)
        d_x = _dot01(jnp.broadcast_to(ds_ref[...], (8, LANES)), expand, 3)[0:1, :]
        st_ref[0, 0] = state_ref[...]

        for g in range(N_GROUPS):
            gs = slice(g * GROUP_W, (g + 1) * GROUP_W)
            bg = bc[:, g * D_STATE:(g + 1) * D_STATE]
            cg = bc[:, (N_GROUPS + g) * D_STATE:(N_GROUPS + g + 1) * D_STATE]
            bgb = bg.astype(BF16)
            cgb = cg.astype(BF16)
            cbm = _dot(cgb, bgb, _NT)
            st_in = state_ref[g]
            y_off = _dot(cgb, st_in.astype(BF16)) * ea_x[:, gs]
            for k in range(HEADS_PER_GROUP):
                h = g * HEADS_PER_GROUP + k
                hs = slice(h * HEAD_DIM, (h + 1) * HEAD_DIM)
                seg = a_cs[:, h:h + 1] - a_cst[h:h + 1, :]
                gh = cbm * jnp.exp(jnp.where(tril, seg, -1e30))
                y_h = _dot(gh.astype(BF16), xtb[:, hs]) + y_off[:, k * HEAD_DIM:(k + 1) * HEAD_DIM]
                y_ref[:, hs] = y_h + d_x[:, hs] * xs[:, hs]
            state_ref[g] = st_in * jnp.exp(aend_x[:, gs]) + _dot(bg.T.astype(BF16), xdb[:, gs])

    nblk = lambda w, off: pl.BlockSpec((qc, w), lambda b, c: (b * nc + c, off))
    full = lambda r, w: pl.BlockSpec((r, w), lambda b, c: (0, 0))
    return pl.pallas_call(
        body, name="ssd_fwd",
        grid=(nb, nc),
        in_specs=[nblk(D_BRANCH, COL_XS // D_BRANCH), nblk(D_BC, COL_BC // D_BC),
                  nblk(LANES, COL_DT // LANES),
                  full(CONV_TAPS, D_CONV), full(1, D_CONV), full(1, LANES), full(1, LANES),
                  full(1, LANES)],
        out_specs=[pl.BlockSpec((qc, D_BRANCH), lambda b, c: (b * nc + c, 0)),
                   pl.BlockSpec((1, 1, N_GROUPS, D_STATE, GROUP_W), lambda b, c: (b, c, 0, 0, 0))],
        out_shape=[jax.ShapeDtypeStruct((nb * seq, D_BRANCH), F32),
                   jax.ShapeDtypeStruct((nb, nc, N_GROUPS, D_STATE, GROUP_W), F32)],
        scratch_shapes=[pltpu.VMEM((qc, D_BRANCH), F32), pltpu.VMEM((qc, D_BC), F32),
                        pltpu.VMEM((N_GROUPS, D_STATE, GROUP_W), F32)],
        compiler_params=_params(2),
    )(proj, proj, proj, conv_w, conv_b, dtb, alog, dskip)


def _ssd_bwd(proj, d_y, states, conv_w, conv_b, dtb, alog, dskip, nb, seq, slabs=()):
    qc = SSD_CHUNK
    nc = seq // qc

    def body(xs_ref, bc_ref, dt_ref, pxs_ref, pbc_ref, dy_ref, st_ref, stn_ref,
             cw_ref, cb_ref, dtb_ref, al_ref, ds_ref,
             dx_ref, gcw_ref, gcb_ref, gdtb_ref, gal_ref, gds_ref,
             dst_ref, nxs_ref, nbc_ref, yd_ref, dxt_ref):
        step = pl.program_id(1)
        chunk = nc - 1 - step

        @pl.when(step == 0)
        def _():
            dst_ref[...] = jnp.zeros_like(dst_ref)
            nxs_ref[...] = jnp.zeros_like(nxs_ref)
            nbc_ref[...] = jnp.zeros_like(nbc_ref)

        @pl.when((pl.program_id(0) == 0) & (step == 0))
        def _():
            gcw_ref[...] = jnp.zeros_like(gcw_ref)
            gcb_ref[...] = jnp.zeros_like(gcb_ref)
            gdtb_ref[...] = jnp.zeros_like(gdtb_ref)
            gal_ref[...] = jnp.zeros_like(gal_ref)
            gds_ref[...] = jnp.zeros_like(gds_ref)

        expand = _head_expand()
        collapse = lambda v: _dot01(v, expand, 2, _NT)
        first = jnp.where(chunk == 0, 0.0, 1.0)
        xs_raw = xs_ref[...]
        bc_raw = bc_ref[...]
        pxs = pxs_ref[...] * first
        pbc = pbc_ref[...] * first
        cw = cw_ref[...]
        cb = cb_ref[...]
        taps_xs = _conv_taps(xs_raw, pxs)
        taps_bc = _conv_taps(bc_raw, pbc)
        xs, dsilu_xs = _silu_and_grad(_conv_pre(taps_xs, cw[:, :D_BRANCH], cb[:, :D_BRANCH]))
        bc, dsilu_bc = _silu_and_grad(_conv_pre(taps_bc, cw[:, D_BRANCH:], cb[:, D_BRANCH:]))

        dt_in = dt_ref[...] + dtb_ref[...]
        dt, d_a, a_cs, dt_x, acs_x, tril = _chunk_decay(
            dt_ref[...], dtb_ref[...], al_ref[...], expand, qc)
        a_cst = a_cs.T
        aend_x = acs_x[qc - 1:qc, :]
        ea_x = jnp.exp(acs_x)
        dec_x = jnp.exp(aend_x - acs_x)
        xt = xs * dt_x
        xtb = xt.astype(BF16)
        xdb = (xt * dec_x).astype(BF16)
        d_x = _dot01(jnp.broadcast_to(ds_ref[...], (8, LANES)), expand, 3)[0:1, :]

        dy = dy_ref[...]
        dyb = dy.astype(BF16)
        dyeab = (dy * ea_x).astype(BF16)
        gds_ref[...] += collapse(jnp.broadcast_to(jnp.sum(dy * xs, axis=0, keepdims=True),
                                                  (8, D_BRANCH)))[0:1, :]

        d_bc = []
        d_cc = []
        y_offs = []
        dxt_states = []
        end_terms = []
        for g in range(N_GROUPS):
            gs = slice(g * GROUP_W, (g + 1) * GROUP_W)
            bg = bc[:, g * D_STATE:(g + 1) * D_STATE]
            cg = bc[:, (N_GROUPS + g) * D_STATE:(N_GROUPS + g + 1) * D_STATE]
            bgb = bg.astype(BF16)
            cgb = cg.astype(BF16)
            cbm = _dot(cgb, bgb, _NT)
            st_in = st_ref[0, 0, g]
            st_inb = st_in.astype(BF16)
            d_st = dst_ref[g]
            d_stb = d_st.astype(BF16)
            y_offs.append(_dot(cgb, st_inb) * ea_x[:, gs])
            dxt_states.append(_dot(bgb, d_stb) * dec_x[:, gs])
            d_c = _dot(dyeab[:, gs], st_inb, _NT)
            d_b = _dot(xdb[:, gs], d_stb, _NT)
            d_cb = jnp.zeros((qc, qc), F32)
            for k in range(HEADS_PER_GROUP):
                h = g * HEADS_PER_GROUP + k
                hs = slice(h * HEAD_DIM, (h + 1) * HEAD_DIM)
                seg = a_cs[:, h:h + 1] - a_cst[h:h + 1, :]
                lh = jnp.exp(jnp.where(tril, seg, -1e30))
                ghb = (cbm * lh).astype(BF16)
                d_cb = d_cb + _dot(dyb[:, hs], xtb[:, hs], _NT) * lh
                yd_ref[:, hs] = _dot(ghb, xtb[:, hs])
                dxt_ref[:, hs] = _dot(ghb, dyb[:, hs], _TN)
            d_cbb = d_cb.astype(BF16)
            d_cc.append(d_c + _dot(d_cbb, bgb))
            d_bc.append(d_b + _dot(d_cbb, cgb, _TN))
            end_terms.append(jnp.sum(d_st * stn_ref[0, 0, g], axis=0, keepdims=True))
            dst_ref[g] = d_st * jnp.exp(aend_x[:, gs]) + _dot(cg.T.astype(BF16), dyeab[:, gs])

        y_off = jnp.concatenate(y_offs, axis=1)
        dxt_state = jnp.concatenate(dxt_states, axis=1)
        dxt = dxt_ref[...] + dxt_state
        last = jnp.where(chunk == nc - 1, 0.0, 1.0)
        end_c = collapse(jnp.broadcast_to(jnp.concatenate(end_terms, axis=1), (8, D_BRANCH)))[0:1, :]
        da_cs = collapse(dyb.astype(F32) * yd_ref[...] - dxt_ref[...] * xtb.astype(F32)
                         + dy * y_off - dxt_state * xt)
        da_cs = da_cs + jnp.where(_row_iota(da_cs.shape) == qc - 1, end_c * last, 0.0)
        triu = lax.broadcasted_iota(jnp.int32, (qc, qc), 0) <= lax.broadcasted_iota(jnp.int32, (qc, qc), 1)
        dd_a = _dot01(da_cs, jnp.where(triu, 1.0, 0.0).astype(BF16), 3, m_left=True)
        ddt = dd_a * (-jnp.exp(al_ref[...])) + collapse(dxt * xs)
        head_lanes = _lane_iota(ddt.shape) < N_HEADS
        ddt_raw = jnp.where(head_lanes, ddt * _sigmoid(dt_in), 0.0)
        gal_ref[...] += jnp.sum(jnp.where(head_lanes, dd_a * d_a, 0.0), axis=0, keepdims=True)
        gdtb_ref[...] += jnp.sum(ddt_raw, axis=0, keepdims=True)

        dpre_xs = (dxt * dt_x + d_x * dy) * dsilu_xs
        dpre_bc = jnp.concatenate(d_bc + d_cc, axis=1) * dsilu_bc
        gcb_ref[...] += jnp.concatenate([jnp.sum(dpre_xs, axis=0, keepdims=True),
                                         jnp.sum(dpre_bc, axis=0, keepdims=True)], axis=1)
        nxs = nxs_ref[...]
        nbc = nbc_ref[...]
        du_xs = jnp.zeros_like(dpre_xs)
        du_bc = jnp.zeros_like(dpre_bc)
        for i in range(CONV_TAPS):
            k = CONV_TAPS - 1 - i
            gcw_ref[i:i + 1, :] += jnp.concatenate(
                [jnp.sum(dpre_xs * taps_xs[i], axis=0, keepdims=True),
                 jnp.sum(dpre_bc * taps_bc[i], axis=0, keepdims=True)], axis=1)
            du_xs = du_xs + _shift_up(dpre_xs, nxs, k) * cw[i:i + 1, :D_BRANCH]
            du_bc = du_bc + _shift_up(dpre_bc, nbc, k) * cw[i:i + 1, D_BRANCH:]
        nxs_ref[...] = dpre_xs
        nbc_ref[...] = dpre_bc

        dx_ref[:, :D_BRANCH] = du_xs.astype(BF16)
        dx_ref[:, D_BRANCH:D_CONV] = du_bc.astype(BF16)
        dx_ref[:, D_CONV:D_CONV + LANES] = ddt_raw.astype(BF16)
        dx_ref[:, D_CONV + LANES:] = jnp.zeros((qc, 2048 - D_CONV - LANES), BF16)

    rev = lambda b, c: b * nc + (nc - 1 - c)
    prv = lambda b, c: b * nc + jnp.maximum(nc - 2 - c, 0)
    nblk = lambda w, off, f: pl.BlockSpec((qc, w), lambda b, c: (f(b, c), off))
    full = lambda r, w: pl.BlockSpec((r, w), lambda b, c: (0, 0))
    st_spec = lambda f: pl.BlockSpec((1, 1, N_GROUPS, D_STATE, GROUP_W),
                                     lambda b, c: (b, f(c), 0, 0, 0))
    return _call_with_exchange(
        body, (proj, proj, proj, proj, proj, d_y, states, states, conv_w, conv_b, dtb, alog, dskip),
        slabs, (True,) * len(slabs), name="ssd_bwd", grid=(nb, nc),
        in_specs=[nblk(D_BRANCH, COL_XS // D_BRANCH, rev), nblk(D_BC, COL_BC // D_BC, rev),
                  nblk(LANES, COL_DT // LANES, rev),
                  nblk(D_BRANCH, COL_XS // D_BRANCH, prv), nblk(D_BC, COL_BC // D_BC, prv),
                  nblk(D_BRANCH, 0, rev),
                  st_spec(lambda c: nc - 1 - c), st_spec(lambda c: jnp.minimum(nc - c, nc - 1)),
                  full(CONV_TAPS, D_CONV), full(1, D_CONV), full(1, LANES), full(1, LANES),
                  full(1, LANES)],
        out_specs=[nblk(2048, 0, rev), full(8, D_CONV), full(1, D_CONV), full(1, LANES),
                   full(1, LANES), full(1, LANES)],
        out_shape=[jax.ShapeDtypeStruct((nb * seq, 2048), BF16),
                   jax.ShapeDtypeStruct((8, D_CONV), F32), jax.ShapeDtypeStruct((1, D_CONV), F32),
                   jax.ShapeDtypeStruct((1, LANES), F32), jax.ShapeDtypeStruct((1, LANES), F32),
                   jax.ShapeDtypeStruct((1, LANES), F32)],
        scratch_shapes=[pltpu.VMEM((N_GROUPS, D_STATE, GROUP_W), F32),
                        pltpu.VMEM((qc, D_BRANCH), F32), pltpu.VMEM((qc, D_BC), F32),
                        pltpu.VMEM((qc, D_BRANCH), F32), pltpu.VMEM((qc, D_BRANCH), F32)])


def _mid(o_sb, y_ssd, proj, x2, target, sb_w, ssd_w, w_out_b, tm):
    t = x2.shape[0]
    inv_d = 1.0 / D_MODEL

    def body(o_ref, y_ref, zsb_ref, zssd_ref, x_ref, tg_ref, sbw_ref, ssdw_ref, w_ref,
             dout_ref, dosb_ref, dy_ref, dz_ref, gw_ref, gsb_ref, gssd_ref, loss_ref):
        @pl.when(pl.program_id(0) == 0)
        def _():
            gw_ref[...] = jnp.zeros_like(gw_ref)
            gsb_ref[...] = jnp.zeros_like(gsb_ref)
            gssd_ref[...] = jnp.zeros_like(gssd_ref)
            loss_ref[...] = jnp.zeros_like(loss_ref)

        def branch(val, z, w):
            gate, dgate = _silu_and_grad(z)
            g = val * gate
            r = lax.rsqrt(jnp.mean(g * g, axis=1, keepdims=True) + EPS)
            xhat = g * r
            return (gate, dgate, r, xhat), (xhat * w).astype(BF16)

        o = o_ref[...]
        y = y_ref[...]
        saved_a, mix_a = branch(o, zsb_ref[...], sbw_ref[...])
        saved_b, mix_b = branch(y, zssd_ref[...], ssdw_ref[...])
        out = x_ref[...] + _dot(mix_a, w_ref[:D_BRANCH, :]) + _dot(mix_b, w_ref[D_BRANCH:, :])
        diff = out - tg_ref[...]
        loss_ref[...] += 0.5 * inv_d * jnp.sum(diff * diff)
        d_out = diff * inv_d
        dout_ref[...] = d_out
        d_outb = d_out.astype(BF16)
        gw_ref[:D_BRANCH, :] += _dot(mix_a, d_outb, _TN)
        gw_ref[D_BRANCH:, :] += _dot(mix_b, d_outb, _TN)

        def branch_bwd(dmix, val, w, saved):
            gate, dgate, r, xhat = saved
            gg = dmix * w
            m = jnp.mean(gg * xhat, axis=1, keepdims=True)
            dg = r * (gg - xhat * m)
            return dg * gate, dg * val * dgate, jnp.sum(dmix * xhat, axis=0, keepdims=True)

        dmix_a = _dot(d_outb, w_ref[:D_BRANCH, :], _NT)
        dmix_b = _dot(d_outb, w_ref[D_BRANCH:, :], _NT)
        d_o, dz_a, gsb = branch_bwd(dmix_a, o, sbw_ref[...], saved_a)
        d_y, dz_b, gssd = branch_bwd(dmix_b, y, ssdw_ref[...], saved_b)
        dosb_ref[...] = d_o
        dy_ref[...] = d_y
        dz_ref[:, :D_BRANCH] = dz_a.astype(BF16)
        dz_ref[:, D_BRANCH:] = dz_b.astype(BF16)
        gsb_ref[...] += gsb
        gssd_ref[...] += gssd

    row = lambda w, off: pl.BlockSpec((tm, w), lambda i: (i, off))
    full = lambda r, w: pl.BlockSpec((r, w), lambda i: (0, 0))
    resident = pl.BlockSpec((2 * D_BRANCH, D_MODEL), lambda i: (0, 0), pipeline_mode=pl.Buffered(1))
    tok = jax.ShapeDtypeStruct((t, D_MODEL), F32)
    return pl.pallas_call(
        body, name="mid",
        grid=(t // tm,),
        in_specs=[row(D_BRANCH, 0), row(D_BRANCH, 0), row(D_BRANCH, 3), row(D_BRANCH, 4),
                  row(D_MODEL, 0), row(D_MODEL, 0), full(1, D_BRANCH), full(1, D_BRANCH),
                  resident],
        out_specs=[row(D_MODEL, 0), row(D_BRANCH, 0), row(D_BRANCH, 0), row(2 * D_BRANCH, 0),
                   resident, full(1, D_BRANCH), full(1, D_BRANCH),
                   full(1, LANES)],
        out_shape=[tok, tok, tok, jax.ShapeDtypeStruct((t, 2 * D_BRANCH), BF16),
                   jax.ShapeDtypeStruct((2 * D_BRANCH, D_MODEL), F32),
                   jax.ShapeDtypeStruct((1, D_BRANCH), F32), jax.ShapeDtypeStruct((1, D_BRANCH), F32),
                   jax.ShapeDtypeStruct((1, LANES), F32)],
        compiler_params=_params(1),
    )(o_sb, y_ssd, proj, proj, x2, target, sb_w, ssd_w, w_out_b)


_DPROJ_FIRST = (0, 1, 2, 3, 5)
_DPROJ_BLOCKS = (1, 1, 1, 2, 2)


def _in_proj_bwd_x(d_parts, w_in_t, x2, d_out, norm_w, tm, slabs=()):
    t = x2.shape[0]
    n_parts = len(d_parts)

    def body(*refs):
        dp_refs = refs[:n_parts]
        w_ref, x_ref, dout_ref, nw_ref, gx_ref, gnw_ref = refs[n_parts:]

        @pl.when(pl.program_id(0) == 0)
        def _():
            gnw_ref[...] = jnp.zeros_like(gnw_ref)

        d_hn = None
        for p in range(n_parts):
            rows = slice(_DPROJ_FIRST[p] * 1024, (_DPROJ_FIRST[p] + _DPROJ_BLOCKS[p]) * 1024)
            term = _dot(dp_refs[p][...], w_ref[rows, :])
            d_hn = term if d_hn is None else d_hn + term
        xf = x_ref[...]
        r = lax.rsqrt(jnp.mean(xf * xf, axis=1, keepdims=True) + EPS)
        xhat = xf * r
        g = d_hn * nw_ref[...]
        m = jnp.mean(g * xhat, axis=1, keepdims=True)
        gx_ref[...] = dout_ref[...] + r * (g - xhat * m)
        gnw_ref[...] += jnp.sum(d_hn * xhat, axis=0, keepdims=True)

    row = lambda w: pl.BlockSpec((tm, w), lambda i: (i, 0))
    return _call_with_exchange(
        body, (*d_parts, w_in_t, x2, d_out, norm_w), slabs, (True,) * len(slabs),
        name="in_proj_bwd_x", grid=(t // tm,),
        in_specs=[row(1024 * _DPROJ_BLOCKS[p]) for p in range(n_parts)] + [
                  pl.BlockSpec((D_IN_PAD, D_MODEL), lambda i: (0, 0), pipeline_mode=pl.Buffered(1)),
                  row(D_MODEL), row(D_MODEL), pl.BlockSpec((1, D_MODEL), lambda i: (0, 0))],
        out_specs=[row(D_MODEL), pl.BlockSpec((1, D_MODEL), lambda i: (0, 0))],
        out_shape=[jax.ShapeDtypeStruct((t, D_MODEL), F32), jax.ShapeDtypeStruct((1, D_MODEL), F32)])


def _in_proj_bwd_w(hn, d_parts, tm):
    t = hn.shape[0]
    n_parts = len(d_parts)

    def body(hn_ref, *refs):
        dp_refs, gw_ref = refs[:n_parts], refs[n_parts]

        @pl.when(pl.program_id(0) == 0)
        def _():
            gw_ref[...] = jnp.zeros_like(gw_ref)

        hnt = hn_ref[...].astype(F32).T.astype(BF16)
        for p in range(n_parts):
            cols = slice(_DPROJ_FIRST[p] * 1024, (_DPROJ_FIRST[p] + _DPROJ_BLOCKS[p]) * 1024)
            gw_ref[:, cols] += _dot(hnt, dp_refs[p][...])

    return pl.pallas_call(
        body, name="in_proj_bwd_w",
        grid=(t // tm,),
        in_specs=[pl.BlockSpec((tm, D_MODEL), lambda i: (i, 0))]
                 + [pl.BlockSpec((tm, 1024 * _DPROJ_BLOCKS[p]), lambda i: (i, 0))
                    for p in range(n_parts)],
        out_specs=pl.BlockSpec((D_MODEL, D_IN_PAD), lambda i: (0, 0), pipeline_mode=pl.Buffered(1)),
        out_shape=jax.ShapeDtypeStruct((D_MODEL, D_IN_PAD), F32),
        compiler_params=_params(1),
    )(hn, *d_parts)


def _adamw(parts, w, m, v, tr, name):
    _, rows, cols = w.shape
    c1 = 1.0 - ADAM_B1 ** ADAM_STEP
    c2 = 1.0 - ADAM_B2 ** ADAM_STEP

    def body(p_ref, w_ref, m_ref, v_ref, g_ref, d_ref, nm_ref, nv_ref):
        g = p_ref[0].astype(F32)
        for s in range(1, N_DEV):
            g = g + p_ref[s].astype(F32)
        nm = ADAM_B1 * m_ref[0] + (1.0 - ADAM_B1) * g
        nv = ADAM_B2 * v_ref[0] + (1.0 - ADAM_B2) * (g * g)
        g_ref[0] = g
        nm_ref[0] = nm
        nv_ref[0] = nv
        d_ref[0] = -ADAM_LR * ((nm / c1) / (jnp.sqrt(nv / c2) + ADAM_EPS) + ADAM_WD * w_ref[0])

    blk = pl.BlockSpec((1, tr, cols), lambda i: (0, i, 0))
    shape = jax.ShapeDtypeStruct((1, rows, cols), F32)
    return pl.pallas_call(
        body, name=name,
        grid=(rows // tr,),
        in_specs=[pl.BlockSpec((N_DEV, tr, cols), lambda i: (0, i, 0)), blk, blk, blk],
        out_specs=[blk, blk, blk, blk],
        out_shape=[shape, shape, shape, shape],
        compiler_params=_params(1),
    )(parts, w, m, v)


def _mesh_place():
    x, y, c = lax.axis_index("x"), lax.axis_index("y"), lax.axis_index("c")
    return x, y, c, 4 * x + 2 * y + c


def _peer(x, y, c, k):
    px = 1 - x if k & 4 else x
    py = 1 - y if k & 2 else y
    pc = 1 - c if k & 1 else c
    return (px, py, pc), 4 * px + 2 * py + pc


def _exchange(srcs, scatter, name):
    n = len(srcs)

    def body(*refs):
        copies = _exchange_copies(refs[:n], refs[n:2 * n], scatter, *refs[2 * n:])
        _exchange_start(copies)
        _exchange_wait(copies)

    return pl.pallas_call(
        body, name=name,
        in_specs=[_ANY] * n, out_specs=[_ANY] * n, out_shape=_exchange_shapes(srcs, scatter),
        scratch_shapes=_exchange_sems(n),
    )(*srcs)


def _call_with_exchange(body, operands, srcs, scatter, *, name, grid, in_specs, out_specs,
                        out_shape, scratch_shapes=()):
    n_in, n_out, n_scr, n_x = len(in_specs), len(out_specs), len(scratch_shapes), len(srcs)
    params = _params(len(grid))
    if not n_x:
        return pl.pallas_call(body, name=name, grid=grid, in_specs=list(in_specs),
                              out_specs=list(out_specs), out_shape=list(out_shape),
                              scratch_shapes=list(scratch_shapes), compiler_params=params)(*operands)

    def wrapped(*refs):
        ins, refs = refs[:n_in], refs[n_in:]
        x_src, refs = refs[:n_x], refs[n_x:]
        outs, refs = refs[:n_out], refs[n_out:]
        x_dst, refs = refs[:n_x], refs[n_x:]
        scratch, sems = refs[:n_scr], refs[n_scr:]
        ids = [pl.program_id(a) for a in range(len(grid))]
        first = functools.reduce(jnp.logical_and, [i == 0 for i in ids])
        last = functools.reduce(jnp.logical_and, [i == n - 1 for i, n in zip(ids, grid)])

        @pl.when(first)
        def _():
            _exchange_start(_exchange_copies(x_src, x_dst, scatter, *sems))

        body(*ins, *outs, *scratch)

        @pl.when(last)
        def _():
            _exchange_wait(_exchange_copies(x_src, x_dst, scatter, *sems))

    return pl.pallas_call(
        wrapped, name=name, grid=grid,
        in_specs=list(in_specs) + [_ANY] * n_x, out_specs=list(out_specs) + [_ANY] * n_x,
        out_shape=list(out_shape) + _exchange_shapes(srcs, scatter),
        scratch_shapes=list(scratch_shapes) + _exchange_sems(n_x), compiler_params=params,
    )(*operands, *srcs)


def _gather_two_level(shard, name):
    def body(x_ref, out_ref, send_sems, recv_sems, local_sem):
        x, y, c, me = _mesh_place()
        sibling = (x, y, 1 - c)
        chips = [(1 - x, y), (x, 1 - y), (1 - x, 1 - y)]

        def slab(px, py, pc):
            return out_ref.at[4 * px + 2 * py + pc]

        def copy(k, block, to, src=None):
            return pltpu.make_async_remote_copy(
                src_ref=slab(*block) if src is None else src, dst_ref=slab(*block),
                send_sem=send_sems.at[k], recv_sem=recv_sems.at[k],
                device_id=to, device_id_type=pl.DeviceIdType.MESH)

        mine = pltpu.make_async_copy(x_ref, slab(x, y, c), local_sem)
        mine.start()
        first = [copy(0, (x, y, c), sibling, src=x_ref)]
        first += [copy(1 + j, (x, y, c), (*chip, c), src=x_ref) for j, chip in enumerate(chips)]
        for cp in first:
            cp.start()
        passed = [copy(4 + j, (*chip, c), sibling) for j, chip in enumerate(chips)]
        for j, chip in enumerate(chips):
            copy(1 + j, (*chip, c), (x, y, c)).wait_recv()
            passed[j].start()
        copy(0, sibling, (x, y, c)).wait_recv()
        for j, chip in enumerate(chips):
            copy(4 + j, (*chip, 1 - c), (x, y, c)).wait_recv()
        for cp in first + passed:
            cp.wait_send()
        mine.wait()

    return pl.pallas_call(
        body, name=name,
        in_specs=[_ANY], out_specs=_ANY,
        out_shape=jax.ShapeDtypeStruct((N_DEV,) + shard.shape, shard.dtype),
        scratch_shapes=[pltpu.SemaphoreType.DMA((N_DEV - 1,)), pltpu.SemaphoreType.DMA((N_DEV - 1,)),
                        pltpu.SemaphoreType.DMA],
    )(shard)


_ANY = pl.BlockSpec(memory_space=pl.ANY)


def _exchange_shapes(srcs, scatter):
    return [jax.ShapeDtypeStruct(s.shape if sc else (N_DEV,) + s.shape, s.dtype)
            for s, sc in zip(srcs, scatter)]


def _exchange_sems(n):
    return [pltpu.SemaphoreType.DMA((n * (N_DEV - 1),)),
            pltpu.SemaphoreType.DMA((n * (N_DEV - 1),)),
            pltpu.SemaphoreType.DMA((n,))]


def _exchange_copies(src_refs, dst_refs, scatter, send_sems, recv_sems, loc_sems):
    n = len(src_refs)
    x, y, c, me = _mesh_place()

    def src_of(i, idx):
        return src_refs[i].at[idx] if scatter[i] else src_refs[i]

    local = [pltpu.make_async_copy(src_of(i, me), dst_refs[i].at[me], loc_sems.at[i])
             for i in range(n)]
    sends, recvs = [], []
    for k in range(1, N_DEV):
        peer, pidx = _peer(x, y, c, k)
        for i in range(n):
            s = i * (N_DEV - 1) + k - 1
            for dst_slab, group in ((me, sends), (pidx, recvs)):
                group.append(pltpu.make_async_remote_copy(
                    src_ref=src_of(i, pidx), dst_ref=dst_refs[i].at[dst_slab],
                    send_sem=send_sems.at[s], recv_sem=recv_sems.at[s],
                    device_id=peer, device_id_type=pl.DeviceIdType.MESH))
    return local, sends, recvs


def _exchange_start(copies):
    local, sends, _ = copies
    for cp in local + sends:
        cp.start()


def _exchange_wait(copies):
    local, sends, recvs = copies
    for cp in recvs:
        cp.wait_recv()
    for cp in sends:
        cp.wait_send()
    for cp in local:
        cp.wait()


def _pad_lanes(v, width=LANES):
    return jnp.pad(v, ((0, 0), (0, width - v.shape[1])))


def _local_step(x, target, norm_w, w_in_b, q_norm_w, k_norm_w, conv_w, conv_b, dt_bias, a_log,
                d_skip, sb_norm_w, ssd_norm_w, w_out_b, tm=256, tq=512, tmid=256, blk=ATT_BLK,
                scatter=False, w_in_t=None):
    nb, seq, _ = x.shape
    t = nb * seq
    x2 = x.reshape(t, D_MODEL)
    tg2 = target.reshape(t, D_MODEL)
    qw2 = jnp.tile(q_norm_w, (1, 2))
    kw2 = jnp.tile(k_norm_w, (1, 2))
    dtb, alog, dsk = _pad_lanes(dt_bias), _pad_lanes(a_log), _pad_lanes(d_skip)

    if w_in_t is None:
        w_in_t = w_in_b.T
    if scatter:
        proj, hn, wout_all, cw_all = _in_proj(x2, norm_w, w_in_b, tm, (w_out_b, conv_w))
        w_out_b = wout_all.reshape(2 * D_BRANCH, D_MODEL)
        conv_w = jnp.transpose(cw_all, (1, 0, 2)).reshape(CONV_TAPS, D_CONV)
    else:
        proj, hn = _in_proj(x2, norm_w, w_in_b, tm)
    qs, kn, vb, kt = _qk_prep(proj, qw2, kw2, nb, seq, tq)
    o_sb, sb_tot, sb_low = _attn_fwd(qs, kn, vb, nb, seq, blk)
    y_ssd, states = _ssd_fwd(proj, conv_w, conv_b, dtb, alog, dsk, nb, seq)
    d_out, d_osb, d_y, d_z, g_wout, g_sbw, g_ssdw, loss = _mid(
        o_sb, y_ssd, proj, x2, tg2, sb_norm_w, ssd_norm_w, w_out_b, tmid)
    dqs, dkn, dvh = _attn_bwd(qs, kn, kt, vb, sb_tot, sb_low, d_osb, nb, seq, blk)
    dq_raw, dk_raw, dv_raw, g_qw, g_kw = _qk_bwd(proj, dqs, dkn, dvh, qw2, kw2, nb, seq, tq)
    wout_slabs = (g_wout.reshape(N_DEV, 2 * D_BRANCH // N_DEV, D_MODEL).astype(BF16),)
    d_xbc, g_cw, g_cb, g_dtb, g_alog, g_dsk, *moved = _ssd_bwd(
        proj, d_y, states, conv_w, conv_b, dtb, alog, dsk, nb, seq, wout_slabs if scatter else ())
    d_parts = [dq_raw, dk_raw, dv_raw, d_z, d_xbc]
    g_win = _in_proj_bwd_w(hn, d_parts, min(2 * tm, t))[:, :D_IN]
    g_cw = g_cw[:CONV_TAPS]
    if scatter:
        g_wout, = moved
        grad_x, g_nw, g_win, g_cw = _in_proj_bwd_x(
            d_parts, w_in_t, x2, d_out, norm_w, tm, _grad_slabs(g_win, g_cw))
    else:
        grad_x, g_nw = _in_proj_bwd_x(d_parts, w_in_t, x2, d_out, norm_w, tm)

    small = dict(
        norm_w=g_nw,
        q_norm_w=g_qw[:, :HEAD_DIM] + g_qw[:, HEAD_DIM:],
        k_norm_w=g_kw[:, :HEAD_DIM] + g_kw[:, HEAD_DIM:],
        conv_b=g_cb, dt_bias=g_dtb[:, :N_HEADS], A_log=g_alog[:, :N_HEADS],
        D_skip=g_dsk[:, :N_HEADS], sb_norm_w=g_sbw, ssd_norm_w=g_ssdw)
    return loss[0, 0], grad_x.reshape(nb, seq, D_MODEL), g_win, g_wout, g_cw, small


def _grad_slabs(g_win, g_cw):
    w_sh = D_IN // N_DEV
    c_sh = D_CONV // N_DEV
    return (jnp.transpose(g_win.reshape(D_MODEL, N_DEV, w_sh), (1, 0, 2)).astype(BF16),
            jnp.pad(jnp.transpose(g_cw.reshape(CONV_TAPS, N_DEV, c_sh), (1, 0, 2)),
                    ((0, 0), (0, 8 - CONV_TAPS), (0, 0))))


_SMALL = ("norm_w", "q_norm_w", "k_norm_w", "conv_b", "dt_bias", "A_log", "D_skip",
          "sb_norm_w", "ssd_norm_w")


def _pack_small(vals):
    flat = jnp.concatenate([_pad_lanes(vals[n], -(-vals[n].shape[1] // LANES) * LANES)
                            for n in _SMALL], axis=1)
    return jnp.pad(flat, ((0, 0), (0, 48 * LANES - flat.shape[1]))).reshape(48, LANES)


def _unpack_small(packed, like):
    out, r = {}, 0
    for n in _SMALL:
        width = like[n].shape[1]
        nr = -(-width // LANES)
        out[n] = packed[r:r + nr].reshape(1, nr * LANES)[:, :width]
        r += nr
    return out


def kernel(x, norm_w, w_in, q_norm_w, k_norm_w, conv_w, conv_b, dt_bias, A_log, D_skip, sb_norm_w, ssd_norm_w, w_out, loss_target, m_norm_w, m_w_in, m_q_norm_w, m_k_norm_w, m_conv_w, m_conv_b, m_dt_bias, m_A_log, m_D_skip, m_sb_norm_w, m_ssd_norm_w, m_w_out, v_norm_w, v_w_in, v_q_norm_w, v_k_norm_w, v_conv_w, v_conv_b, v_dt_bias, v_A_log, v_D_skip, v_sb_norm_w, v_ssd_norm_w, v_w_out):
    win_all = _gather_two_level(w_in[0].astype(BF16), "gather_w_in")
    w_in_b = jnp.pad(jnp.transpose(win_all, (1, 0, 2)).reshape(D_MODEL, D_IN),
                     ((0, 0), (0, D_IN_PAD - D_IN)))
    w_in_t = jnp.pad(jnp.transpose(win_all, (0, 2, 1)).reshape(D_IN, D_MODEL),
                     ((0, D_IN_PAD - D_IN), (0, 0)))

    loss, grad_x, win_parts, wout_parts, cw_parts, g_small = _local_step(
        x, loss_target, norm_w, w_in_b, q_norm_w, k_norm_w, conv_w[0], conv_b, dt_bias, A_log,
        D_skip, sb_norm_w, ssd_norm_w, w_out[0].astype(BF16), scatter=True, w_in_t=w_in_t)
    packed = _pack_small(g_small).at[-1, 0].set(loss)
    small_parts, = _exchange([packed], [False], "gather_small_grads")
    loss = jnp.sum(small_parts[:, -1, 0])

    small_w = dict(norm_w=norm_w, q_norm_w=q_norm_w, k_norm_w=k_norm_w, conv_b=conv_b,
                   dt_bias=dt_bias, A_log=A_log, D_skip=D_skip, sb_norm_w=sb_norm_w,
                   ssd_norm_w=ssd_norm_w)
    small_m = dict(norm_w=m_norm_w, q_norm_w=m_q_norm_w, k_norm_w=m_k_norm_w, conv_b=m_conv_b,
                   dt_bias=m_dt_bias, A_log=m_A_log, D_skip=m_D_skip, sb_norm_w=m_sb_norm_w,
                   ssd_norm_w=m_ssd_norm_w)
    small_v = dict(norm_w=v_norm_w, q_norm_w=v_q_norm_w, k_norm_w=v_k_norm_w, conv_b=v_conv_b,
                   dt_bias=v_dt_bias, A_log=v_A_log, D_skip=v_D_skip, sb_norm_w=v_sb_norm_w,
                   ssd_norm_w=v_ssd_norm_w)

    pad8 = lambda a: jnp.pad(a, ((0, 0), (0, 8 - CONV_TAPS), (0, 0)))
    r_win = _adamw(win_parts, w_in, m_w_in, v_w_in, 128, "adamw_w_in")
    r_wout = _adamw(wout_parts, w_out, m_w_out, v_w_out, 128, "adamw_w_out")
    r_cw = _adamw(cw_parts, pad8(conv_w), pad8(m_conv_w), pad8(v_conv_w), 8, "adamw_conv_w")
    r_small = _adamw(small_parts, _pack_small(small_w)[None], _pack_small(small_m)[None],
                     _pack_small(small_v)[None], 48, "adamw_small")

    res = {"w_in": r_win, "w_out": r_wout, "conv_w": [a[:, :CONV_TAPS] for a in r_cw]}
    unpacked = [_unpack_small(a[0], small_w) for a in r_small]
    for n in _SMALL:
        res[n] = [u[n] for u in unpacked]
    order = ("norm_w", "w_in", "q_norm_w", "k_norm_w", "conv_w", "conv_b", "dt_bias", "A_log",
             "D_skip", "sb_norm_w", "ssd_norm_w", "w_out")
    outs = [loss, grad_x]
    for kind in range(4):
        outs += [res[n][kind] for n in order]
    return tuple(outs)
```

```python
import functools
import math

import jax
import jax.numpy as jnp
from jax import lax
from jax.experimental import pallas as pl
from jax.experimental.pallas import tpu as pltpu

F32 = jnp.float32
BF16 = jnp.bfloat16

D_MODEL = 1024
N_HEADS = 16
HEAD_DIM = 64
N_PAIRS = N_HEADS // 2
D_BRANCH = 1024
N_GROUPS = 2
HEADS_PER_GROUP = 8
D_STATE = 128
GROUP_W = HEADS_PER_GROUP * HEAD_DIM
D_BC = 2 * N_GROUPS * D_STATE
D_CONV = D_BRANCH + D_BC
D_IN = 6672
D_IN_PAD = 7168
N_COLBLK = D_IN_PAD // 1024
COL_XS = 5120
COL_BC = 6144
COL_DT = 6656
EPS = 1e-6
CONV_TAPS = 4
N_DEV = 8

LANES = 128
SSD_CHUNK = 128
ATT_BLK = 256
ATT_HEADS = 4
ATT_W = ATT_HEADS * HEAD_DIM
N_ATT_GROUPS = N_HEADS // ATT_HEADS
EXP_UNDERFLOW = -105.0
VMEM_LIMIT = 56 * 1024 * 1024

ADAM_LR = 0.001
ADAM_B1 = 0.9
ADAM_B2 = 0.999
ADAM_EPS = 1e-08
ADAM_WD = 0.01
ADAM_STEP = 10

_NT = (((1,), (1,)), ((), ()))
_TN = (((0,), (0,)), ((), ()))


def _params(n_grid):
    return pltpu.CompilerParams(dimension_semantics=("arbitrary",) * n_grid,
                                vmem_limit_bytes=VMEM_LIMIT)


def _dot(a, b, dims=None, precision=None):
    if dims is None:
        return jnp.dot(a, b, preferred_element_type=F32, precision=precision)
    return lax.dot_general(a, b, dims, preferred_element_type=F32, precision=precision)


def _sigmoid(x):
    return 1.0 / (1.0 + jnp.exp(-x))


def _softplus(x):
    return jnp.maximum(x, 0.0) + jnp.log(1.0 + jnp.exp(-jnp.abs(x)))


def _split_bf16(x):
    hi = x.astype(BF16)
    lo = (x - hi.astype(F32)).astype(BF16)
    return hi, lo


def _lane_iota(shape):
    return lax.broadcasted_iota(jnp.int32, shape, len(shape) - 1)


def _row_iota(shape):
    return lax.broadcasted_iota(jnp.int32, shape, len(shape) - 2)


def _pair_sum(x):
    r = lax.broadcasted_iota(jnp.int32, (LANES, LANES), 0)
    c = lax.broadcasted_iota(jnp.int32, (LANES, LANES), 1)
    same_head = jnp.where(r // HEAD_DIM == c // HEAD_DIM, 1.0, 0.0).astype(BF16)
    hi, lo = _split_bf16(x)
    return _dot(hi, same_head) + _dot(lo, same_head)


def _head_lanes(x, a):
    lane = _lane_iota(x.shape)
    mine = (lane >= a * HEAD_DIM) & (lane < (a + 1) * HEAD_DIM)
    return jnp.where(mine, x, jnp.zeros_like(x))


def _head_expand():
    r = lax.broadcasted_iota(jnp.int32, (LANES, D_BRANCH), 0)
    c = lax.broadcasted_iota(jnp.int32, (LANES, D_BRANCH), 1)
    return jnp.where(c // HEAD_DIM == r, 1.0, 0.0).astype(BF16)


def _in_proj(x2, norm_w, w_in_b, tm, shards=()):
    t = x2.shape[0]

    def body(x_ref, nw_ref, w_ref, proj_ref, hn_ref):
        xf = x_ref[...]
        r = lax.rsqrt(jnp.mean(xf * xf, axis=1, keepdims=True) + EPS)
        hn = (xf * r * nw_ref[...]).astype(BF16)
        hn_ref[...] = hn
        for j in range(N_COLBLK):
            cols = slice(j * 1024, (j + 1) * 1024)
            proj_ref[:, cols] = _dot(hn, w_ref[:, cols])

    return _call_with_exchange(
        body, (x2, norm_w, w_in_b), shards, (False,) * len(shards), name="in_proj",
        grid=(t // tm,),
        in_specs=[pl.BlockSpec((tm, D_MODEL), lambda i: (i, 0)),
                  pl.BlockSpec((1, D_MODEL), lambda i: (0, 0)),
                  pl.BlockSpec((D_MODEL, D_IN_PAD), lambda i: (0, 0), pipeline_mode=pl.Buffered(1))],
        out_specs=[pl.BlockSpec((tm, D_IN_PAD), lambda i: (i, 0)),
                   pl.BlockSpec((tm, D_MODEL), lambda i: (i, 0))],
        out_shape=[jax.ShapeDtypeStruct((t, D_IN_PAD), F32),
                   jax.ShapeDtypeStruct((t, D_MODEL), BF16)])


def _qk_prep(proj, qw2, kw2, nb, seq, tq):
    nl = seq // tq
    scale = 1.0 / math.sqrt(HEAD_DIM)

    def body(q_ref, k_ref, v_ref, qw_ref, kw_ref, qs_ref, kn_ref, vb_ref, kt_ref, ksq_ref):
        def norm(x, w):
            r = lax.rsqrt(_pair_sum(x * x) * (1.0 / HEAD_DIM) + EPS)
            return x * r * w

        vb_ref[...] = v_ref[...].astype(BF16)
        for p in range(N_PAIRS):
            cols = slice(p * LANES, (p + 1) * LANES)
            kn = norm(k_ref[:, cols], kw_ref[...])
            knb = kn.astype(BF16)
            qs_ref[:, cols] = (norm(q_ref[:, cols], qw_ref[...]) * scale).astype(BF16)
            kn_ref[:, cols] = knb
            kt_ref[0, p] = kn.T.astype(BF16)
            kf = knb.astype(F32)
            ksq_ref[0, 0, p:p + 1, :] = jnp.max(_pair_sum(kf * kf), axis=0, keepdims=True) * 1.0001

    tok_shape = jax.ShapeDtypeStruct((nb * seq, D_BRANCH), BF16)
    tok = lambda blk: pl.BlockSpec((tq, D_BRANCH), lambda b, i: (b * nl + i, blk))
    vec = pl.BlockSpec((1, LANES), lambda b, i: (0, 0))
    return pl.pallas_call(
        body, name="qk_prep",
        grid=(nb, nl),
        in_specs=[tok(0), tok(1), tok(2), vec, vec],
        out_specs=[tok(0), tok(0), tok(0),
                   pl.BlockSpec((1, N_PAIRS, LANES, tq), lambda b, i: (b, 0, 0, i)),
                   pl.BlockSpec((1, 1, N_PAIRS, LANES), lambda b, i: (b, i, 0, 0))],
        out_shape=[tok_shape, tok_shape, tok_shape,
                   jax.ShapeDtypeStruct((nb, N_PAIRS, LANES, seq), BF16),
                   jax.ShapeDtypeStruct((nb, nl, N_PAIRS, LANES), F32)],
        compiler_params=_params(2),
    )(proj, proj, proj, qw2, kw2)


def _attn_fwd(qs, kn, vb, ksq, nb, seq, blk):
    nq = seq // blk

    def body(q_ref, k_ref, v_ref, ksq_ref, o_ref, tot_ref, low_ref):
        qi = pl.program_id(2)
        r_i = lax.broadcasted_iota(jnp.int32, (blk, blk), 0)
        c_i = lax.broadcasted_iota(jnp.int32, (blk, blk), 1)
        csum = jnp.where(r_i >= c_i, 1.0, 0.0).astype(BF16)
        causal = c_i < r_i
        heads = range(ATT_HEADS)
        head = _head_lanes

        q_blk = q_ref[...]
        qf = q_blk.astype(F32)
        q_head = [head(q_blk, a) for a in heads]
        zmax = []
        for a in heads:
            qsq = jnp.sum(head(qf * qf, a), axis=1, keepdims=True)
            kmax = ksq_ref[0, 0, a // 2:a // 2 + 1, (a % 2) * HEAD_DIM:(a % 2) * HEAD_DIM + 1]
            zmax.append(1.01 * jnp.sqrt(qsq * kmax) + 0.01)

        def exhausted(run):
            top = functools.reduce(jnp.maximum, [jnp.max(run[a] + zmax[a]) for a in heads])
            return top < EXP_UNDERFLOW

        def sweep(blocks, run, acc):
            offs = [pl.multiple_of(j * blk, blk) for j, _, _ in blocks]
            z = [[_dot(q_head[a], k_ref[pl.ds(off, blk), :], _NT) for a in heads]
                 for off in offs]
            cl = []
            for (_, diag, valid), zb in zip(blocks, z):
                lkb = []
                for a in heads:
                    lk = -_softplus(zb[a])
                    if diag:
                        lk = jnp.where(causal, lk, 0.0)
                    if valid is not None:
                        lk = jnp.where(valid, lk, 0.0)
                    lkb.append(lk.astype(BF16))
                cl.append([_dot(lkb[a], csum) for a in heads])
            for (_, diag, valid), zb, clb, off in zip(blocks, z, cl, offs):
                w = []
                for a in heads:
                    wa = jnp.exp(zb[a] + clb[a] + run[a])
                    if diag:
                        wa = jnp.where(causal, wa, 0.0)
                    if valid is not None:
                        wa = jnp.where(valid, wa, 0.0)
                    w.append(wa.astype(BF16))
                run = [run[a] + clb[a][:, 0:1] for a in heads]
                v_blk = v_ref[pl.ds(off, blk), :]
                for a in heads:
                    acc = acc + _dot(w[a], head(v_blk, a))
            return run, acc

        run = [jnp.zeros((blk, 1), F32)] * ATT_HEADS
        acc = jnp.zeros((blk, ATT_W), F32)
        run, acc = sweep([(qi, True, None), (jnp.maximum(qi - 1, 0), False, qi >= 1)], run, acc)
        low = jnp.maximum(qi - 1, 0)

        def more(carry):
            low, done, _, _ = carry
            return (low > 0) & jnp.logical_not(done)

        def pair(carry):
            low, _, run, acc = carry
            run, acc = sweep([(low - 1, False, None), (jnp.maximum(low - 2, 0), False, low >= 2)],
                             run, acc)
            return jnp.maximum(low - 2, 0), exhausted(run), run, acc

        low, _, run, acc = lax.while_loop(more, pair, (low, exhausted(run), run, acc))
        low_ref[pl.program_id(0) * N_ATT_GROUPS + pl.program_id(1), qi] = low.astype(F32)
        o_ref[...] = acc
        for a in heads:
            as_row = jnp.sum(jnp.where(r_i == c_i, run[a], 0.0), axis=0, keepdims=True)
            tot_ref[0, a, 0] = jnp.broadcast_to(as_row, (8, blk))

    return pl.pallas_call(
        body, name="sb_attn_fwd",
        grid=(nb, N_ATT_GROUPS, nq),
        in_specs=[pl.BlockSpec((blk, ATT_W), lambda b, h, i: (b * nq + i, h)),
                  pl.BlockSpec((seq, ATT_W), lambda b, h, i: (b, h)),
                  pl.BlockSpec((seq, ATT_W), lambda b, h, i: (b, h)),
                  pl.BlockSpec((1, 1, ATT_HEADS // 2, LANES), lambda b, h, i: (b, h, 0, 0))],
        out_specs=[pl.BlockSpec((blk, ATT_W), lambda b, h, i: (b * nq + i, h)),
                   pl.BlockSpec((1, ATT_HEADS, 1, 8, blk), lambda b, h, i: (b, h, i, 0, 0)),
                   pl.BlockSpec(memory_space=pltpu.SMEM)],
        out_shape=[jax.ShapeDtypeStruct((nb * seq, D_BRANCH), F32),
                   jax.ShapeDtypeStruct((nb, N_HEADS, nq, 8, blk), F32),
                   jax.ShapeDtypeStruct((nb * N_ATT_GROUPS, nq), F32)],
        compiler_params=_params(3),
    )(qs, kn, vb, ksq.reshape(nb, N_ATT_GROUPS, ATT_HEADS // 2, LANES))


def _attn_bwd(qs, kn, kt, vb, tot, low, d_o, nb, seq, blk):
    nq = seq // blk

    def body(q_ref, k_ref, kt_ref, v_ref, tot_ref, low_ref, do_ref, dq_ref, dk_ref, dv_ref):
        qi = pl.program_id(2)

        @pl.when(qi == 0)
        def _():
            dk_ref[...] = jnp.zeros_like(dk_ref)
            dv_ref[...] = jnp.zeros_like(dv_ref)

        r_i = lax.broadcasted_iota(jnp.int32, (blk, blk), 0)
        c_i = lax.broadcasted_iota(jnp.int32, (blk, blk), 1)
        before = jnp.where(c_i < r_i, 1.0, 0.0).astype(BF16)
        upto = jnp.where(c_i <= r_i, 1.0, 0.0).astype(BF16)
        causal = r_i < c_i

        heads = range(ATT_HEADS)
        q_head = [_head_lanes(q_ref[...], a) for a in heads]
        d_ob = [_head_lanes(do_ref[...].astype(BF16), a) for a in heads]
        total = [tot_ref[0, a, 0][0:1, :] for a in heads]

        def sweep(blocks, lsum, esum, dqt):
            def keep(x, diag, valid):
                if diag:
                    x = jnp.where(causal, x, 0.0)
                if valid is not None:
                    x = jnp.where(valid, x, 0.0)
                return x

            offs = [pl.multiple_of(j * blk, blk) for j, _, _ in blocks]
            zt = [[_dot(k_ref[pl.ds(off, blk), :], q_head[a], _NT) for a in heads]
                  for off in offs]
            dwt = [[_dot(v_ref[pl.ds(off, blk), :], d_ob[a], _NT) for a in heads]
                   for off in offs]
            sp, lk, lpre = [], [], []
            for (_, diag, valid), ztb in zip(blocks, zt):
                sp.append([_softplus(ztb[a]) for a in heads])
                lk.append([keep(-sp[-1][a], diag, valid).astype(BF16) for a in heads])
                lpre.append([_dot(before, lk[-1][a]) for a in heads])
            wt, et, epre = [], [], []
            for i, (_, diag, valid) in enumerate(blocks):
                wt.append([keep(jnp.exp(zt[i][a] + (total[a] - lsum[a] - lpre[i][a])), diag, valid)
                           for a in heads])
                et.append([dwt[i][a] * wt[i][a] for a in heads])
                split = [_split_bf16(et[i][a]) for a in heads]
                epre.append([_dot(upto, split[a][0]) + _dot(upto, split[a][1]) for a in heads])
                lsum = [lsum[a] + lpre[i][a][blk - 1:blk, :] + lk[i][a][blk - 1:blk, :]
                        for a in heads]
            for i, (_, diag, valid) in enumerate(blocks):
                dzb = [keep(et[i][a] - jnp.exp(zt[i][a] - sp[i][a]) * (esum[a] + epre[i][a]),
                            diag, valid).astype(BF16) for a in heads]
                esum = [esum[a] + epre[i][a][blk - 1:blk, :] for a in heads]
                dk_ref[pl.ds(offs[i], blk), :] += functools.reduce(
                    jnp.add, [_dot(dzb[a], q_head[a]) for a in heads])
                dv_ref[pl.ds(offs[i], blk), :] += functools.reduce(
                    jnp.add, [_dot(wt[i][a].astype(BF16), d_ob[a]) for a in heads])
                dqt = [dqt[a] + _dot(kt_ref[0, a // 2, (a % 2) * HEAD_DIM:(a % 2 + 1) * HEAD_DIM,
                                            pl.ds(offs[i], blk)], dzb[a]) for a in heads]
            return lsum, esum, dqt

        row = [jnp.zeros((1, blk), F32)] * ATT_HEADS
        dqt = [jnp.zeros((HEAD_DIM, blk), F32)] * ATT_HEADS
        low = low_ref[pl.program_id(0) * N_ATT_GROUPS + pl.program_id(1), qi].astype(jnp.int32)
        low = jnp.clip(low, 0, jnp.maximum(qi - 1, 0))

        def pair(carry):
            j, lsum, esum, dqt = carry
            return (j + 2,) + sweep([(j, False, None), (j + 1, False, j + 1 < qi - 1)],
                                    lsum, esum, dqt)

        _, lsum, esum, dqt = lax.while_loop(lambda c: c[0] < qi - 1, pair, (low, row, row, dqt))
        _, _, dqt = sweep([(jnp.maximum(qi - 1, 0), False, qi >= 1), (qi, True, None)],
                          lsum, esum, dqt)
        dq_ref[...] = jnp.concatenate(dqt, axis=0).T

    seq_blk = pl.BlockSpec((seq, ATT_W), lambda b, h, i: (b, h))
    tok = pl.BlockSpec((blk, ATT_W), lambda b, h, i: (b * nq + i, h))
    tok_shape = jax.ShapeDtypeStruct((nb * seq, D_BRANCH), F32)
    return pl.pallas_call(
        body, name="sb_attn_bwd",
        grid=(nb, N_ATT_GROUPS, nq),
        in_specs=[tok, seq_blk,
                  pl.BlockSpec((1, ATT_HEADS // 2, LANES, seq), lambda b, h, i: (b, h, 0, 0)),
                  seq_blk,
                  pl.BlockSpec((1, ATT_HEADS, 1, 8, blk), lambda b, h, i: (b, h, i, 0, 0)),
                  pl.BlockSpec(memory_space=pltpu.SMEM),
                  tok],
        out_specs=[tok, seq_blk, seq_blk],
        out_shape=[tok_shape, tok_shape, tok_shape],
        compiler_params=_params(3),
    )(qs, kn, kt, vb, tot, low, d_o)


def _qk_bwd(proj, dqs, dkn, dvh, qw2, kw2, nb, seq, tq):
    nl = seq // tq
    scale = 1.0 / math.sqrt(HEAD_DIM)

    def body(q_ref, k_ref, dq_ref, dk_ref, dv_ref, qw_ref, kw_ref,
             dqr_ref, dkr_ref, dvr_ref, gq_ref, gk_ref):
        @pl.when((pl.program_id(0) == 0) & (pl.program_id(1) == 0))
        def _():
            gq_ref[...] = jnp.zeros_like(gq_ref)
            gk_ref[...] = jnp.zeros_like(gk_ref)

        def norm_bwd(x, w, dy):
            r = lax.rsqrt(_pair_sum(x * x) * (1.0 / HEAD_DIM) + EPS)
            xhat = x * r
            g = dy * w
            m = _pair_sum(g * xhat) * (1.0 / HEAD_DIM)
            return r * (g - xhat * m), jnp.sum(dy * xhat, axis=0, keepdims=True)

        dvr_ref[...] = dv_ref[...].astype(BF16)
        gq = jnp.zeros((1, LANES), F32)
        gk = jnp.zeros((1, LANES), F32)
        for p in range(N_PAIRS):
            cols = slice(p * LANES, (p + 1) * LANES)
            dqr, gq_p = norm_bwd(q_ref[:, cols], qw_ref[...], dq_ref[:, cols] * scale)
            dkr, gk_p = norm_bwd(k_ref[:, cols], kw_ref[...], dk_ref[:, cols])
            dqr_ref[:, cols] = dqr.astype(BF16)
            dkr_ref[:, cols] = dkr.astype(BF16)
            gq, gk = gq + gq_p, gk + gk_p
        gq_ref[...] += gq
        gk_ref[...] += gk

    tok = lambda blk: pl.BlockSpec((tq, D_BRANCH), lambda b, i: (b * nl + i, blk))
    vec = pl.BlockSpec((1, LANES), lambda b, i: (0, 0))
    tshape = jax.ShapeDtypeStruct((nb * seq, D_BRANCH), BF16)
    return pl.pallas_call(
        body, name="qk_bwd",
        grid=(nb, nl),
        in_specs=[tok(0), tok(1), tok(0), tok(0), tok(0), vec, vec],
        out_specs=[tok(0), tok(0), tok(0), vec, vec],
        out_shape=[tshape, tshape, tshape,
                   jax.ShapeDtypeStruct((1, LANES), F32), jax.ShapeDtypeStruct((1, LANES), F32)],
        compiler_params=_params(2),
    )(proj, proj, dqs, dkn, dvh, qw2, kw2)


def _shift_down(cur, prev, k):
    if k == 0:
        return cur
    rows = _row_iota(cur.shape)
    return jnp.where(rows < k, pltpu.roll(prev, k, axis=0), pltpu.roll(cur, k, axis=0))


def _shift_up(cur, nxt, k):
    if k == 0:
        return cur
    n = cur.shape[0]
    rows = _row_iota(cur.shape)
    return jnp.where(rows < n - k, pltpu.roll(cur, n - k, axis=0), pltpu.roll(nxt, n - k, axis=0))


def _conv_taps(cur, prev):
    return [_shift_down(cur, prev, CONV_TAPS - 1 - i) for i in range(CONV_TAPS)]


def _conv_pre(taps, w, b):
    out = b
    for i in range(CONV_TAPS):
        out = out + taps[i] * w[i:i + 1, :]
    return out


def _silu(x):
    return x * _sigmoid(x)


def _silu_and_grad(x):
    s = _sigmoid(x)
    return x * s, s * (1.0 + x * (1.0 - s))


def _dot01(x, m01, parts, dims=None, m_left=False):
    total, rest = None, x
    for i in range(parts):
        piece = rest.astype(BF16)
        if i + 1 < parts:
            rest = rest - piece.astype(F32)
        term = _dot(m01, piece, dims) if m_left else _dot(piece, m01, dims)
        total = term if total is None else total + term
    return total


def _chunk_decay(dt_raw, dtb, alog, expand, qc):
    dt = _softplus(dt_raw + dtb)
    d_a = dt * (-jnp.exp(alog))
    r_i = lax.broadcasted_iota(jnp.int32, (qc, qc), 0)
    c_i = lax.broadcasted_iota(jnp.int32, (qc, qc), 1)
    tril = r_i >= c_i
    a_cs = _dot01(d_a, jnp.where(tril, 1.0, 0.0).astype(BF16), 3, m_left=True)
    dt_x = _dot01(dt, expand, 3)
    acs_x = _dot01(a_cs, expand, 3)
    return dt, d_a, a_cs, dt_x, acs_x, tril


def _ssd_fwd(proj, conv_w, conv_b, dtb, alog, dskip, nb, seq):
    qc = SSD_CHUNK
    nc = seq // qc

    def body(xs_ref, bc_ref, dt_ref, cw_ref, cb_ref, dtb_ref, al_ref, ds_ref,
             y_ref, st_ref, pxs_ref, pbc_ref, state_ref):
        @pl.when(pl.program_id(1) == 0)
        def _():
            pxs_ref[...] = jnp.zeros_like(pxs_ref)
            pbc_ref[...] = jnp.zeros_like(pbc_ref)
            state_ref[...] = jnp.zeros_like(state_ref)

        expand = _head_expand()
        xs_raw = xs_ref[...]
        bc_raw = bc_ref[...]
        cw = cw_ref[...]
        cb = cb_ref[...]
        xs = _silu(_conv_pre(_conv_taps(xs_raw, pxs_ref[...]), cw[:, :D_BRANCH], cb[:, :D_BRANCH]))
        bc = _silu(_conv_pre(_conv_taps(bc_raw, pbc_ref[...]), cw[:, D_BRANCH:], cb[:, D_BRANCH:]))
        pxs_ref[...] = xs_raw
        pbc_ref[...] = bc_raw

        dt, d_a, a_cs, dt_x, acs_x, tril = _chunk_decay(
            dt_ref[...], dtb_ref[...], al_ref[...], expand, qc)
        a_cst = a_cs.T
        aend_x = acs_x[qc - 1:qc, :]
        ea_x = jnp.exp(acs_x)
        dec_x = jnp.exp(aend_x - acs_x)
        xt = xs * dt_x
        xtb = xt.astype(BF16)
        xdb = (xt * dec_x).astype(BF16)
        d_x = _dot01(jnp.broadcast_to(ds_ref[...], (8, LANES)), expand, 3)[0:1, :]
        st_ref[0, 0] = state_ref[...]

        for g in range(N_GROUPS):
            gs = slice(g * GROUP_W, (g + 1) * GROUP_W)
            bg = bc[:, g * D_STATE:(g + 1) * D_STATE]
            cg = bc[:, (N_GROUPS + g) * D_STATE:(N_GROUPS + g + 1) * D_STATE]
            bgb = bg.astype(BF16)
            cgb = cg.astype(BF16)
            cbm = _dot(cgb, bgb, _NT)
            st_in = state_ref[g]
            y_off = _dot(cgb, st_in.astype(BF16)) * ea_x[:, gs]
            for k in range(HEADS_PER_GROUP):
                h = g * HEADS_PER_GROUP + k
                hs = slice(h * HEAD_DIM, (h + 1) * HEAD_DIM)
                seg = a_cs[:, h:h + 1] - a_cst[h:h + 1, :]
                gh = cbm * jnp.exp(jnp.where(tril, seg, -1e30))
                y_h = _dot(gh.astype(BF16), xtb[:, hs]) + y_off[:, k * HEAD_DIM:(k + 1) * HEAD_DIM]
                y_ref[:, hs] = y_h + d_x[:, hs] * xs[:, hs]
            state_ref[g] = st_in * jnp.exp(aend_x[:, gs]) + _dot(bg.T.astype(BF16), xdb[:, gs])

    nblk = lambda w, off: pl.BlockSpec((qc, w), lambda b, c: (b * nc + c, off))
    full = lambda r, w: pl.BlockSpec((r, w), lambda b, c: (0, 0))
    return pl.pallas_call(
        body, name="ssd_fwd",
        grid=(nb, nc),
        in_specs=[nblk(D_BRANCH, COL_XS // D_BRANCH), nblk(D_BC, COL_BC // D_BC),
                  nblk(LANES, COL_DT // LANES),
                  full(CONV_TAPS, D_CONV), full(1, D_CONV), full(1, LANES), full(1, LANES),
                  full(1, LANES)],
        out_specs=[pl.BlockSpec((qc, D_BRANCH), lambda b, c: (b * nc + c, 0)),
                   pl.BlockSpec((1, 1, N_GROUPS, D_STATE, GROUP_W), lambda b, c: (b, c, 0, 0, 0))],
        out_shape=[jax.ShapeDtypeStruct((nb * seq, D_BRANCH), F32),
                   jax.ShapeDtypeStruct((nb, nc, N_GROUPS, D_STATE, GROUP_W), F32)],
        scratch_shapes=[pltpu.VMEM((qc, D_BRANCH), F32), pltpu.VMEM((qc, D_BC), F32),
                        pltpu.VMEM((N_GROUPS, D_STATE, GROUP_W), F32)],
        compiler_params=_params(2),
    )(proj, proj, proj, conv_w, conv_b, dtb, alog, dskip)


def _ssd_bwd(proj, d_y, states, conv_w, conv_b, dtb, alog, dskip, nb, seq, slabs=()):
    qc = SSD_CHUNK
    nc = seq // qc

    def body(xs_ref, bc_ref, dt_ref, pxs_ref, pbc_ref, dy_ref, st_ref, stn_ref,
             cw_ref, cb_ref, dtb_ref, al_ref, ds_ref,
             dx_ref, gcw_ref, gcb_ref, gdtb_ref, gal_ref, gds_ref,
             dst_ref, nxs_ref, nbc_ref, yd_ref, dxt_ref):
        step = pl.program_id(1)
        chunk = nc - 1 - step

        @pl.when(step == 0)
        def _():
            dst_ref[...] = jnp.zeros_like(dst_ref)
            nxs_ref[...] = jnp.zeros_like(nxs_ref)
            nbc_ref[...] = jnp.zeros_like(nbc_ref)

        @pl.when((pl.program_id(0) == 0) & (step == 0))
        def _():
            gcw_ref[...] = jnp.zeros_like(gcw_ref)
            gcb_ref[...] = jnp.zeros_like(gcb_ref)
            gdtb_ref[...] = jnp.zeros_like(gdtb_ref)
            gal_ref[...] = jnp.zeros_like(gal_ref)
            gds_ref[...] = jnp.zeros_like(gds_ref)

        expand = _head_expand()
        collapse = lambda v: _dot01(v, expand, 2, _NT)
        first = jnp.where(chunk == 0, 0.0, 1.0)
        xs_raw = xs_ref[...]
        bc_raw = bc_ref[...]
        pxs = pxs_ref[...] * first
        pbc = pbc_ref[...] * first
        cw = cw_ref[...]
        cb = cb_ref[...]
        taps_xs = _conv_taps(xs_raw, pxs)
        taps_bc = _conv_taps(bc_raw, pbc)
        xs, dsilu_xs = _silu_and_grad(_conv_pre(taps_xs, cw[:, :D_BRANCH], cb[:, :D_BRANCH]))
        bc, dsilu_bc = _silu_and_grad(_conv_pre(taps_bc, cw[:, D_BRANCH:], cb[:, D_BRANCH:]))

        dt_in = dt_ref[...] + dtb_ref[...]
        dt, d_a, a_cs, dt_x, acs_x, tril = _chunk_decay(
            dt_ref[...], dtb_ref[...], al_ref[...], expand, qc)
        a_cst = a_cs.T
        aend_x = acs_x[qc - 1:qc, :]
        ea_x = jnp.exp(acs_x)
        dec_x = jnp.exp(aend_x - acs_x)
        xt = xs * dt_x
        xtb = xt.astype(BF16)
        xdb = (xt * dec_x).astype(BF16)
        d_x = _dot01(jnp.broadcast_to(ds_ref[...], (8, LANES)), expand, 3)[0:1, :]

        dy = dy_ref[...]
        dyb = dy.astype(BF16)
        dyeab = (dy * ea_x).astype(BF16)
        gds_ref[...] += collapse(jnp.broadcast_to(jnp.sum(dy * xs, axis=0, keepdims=True),
                                                  (8, D_BRANCH)))[0:1, :]

        d_bc = []
        d_cc = []
        y_offs = []
        dxt_states = []
        end_terms = []
        for g in range(N_GROUPS):
            gs = slice(g * GROUP_W, (g + 1) * GROUP_W)
            bg = bc[:, g * D_STATE:(g + 1) * D_STATE]
            cg = bc[:, (N_GROUPS + g) * D_STATE:(N_GROUPS + g + 1) * D_STATE]
            bgb = bg.astype(BF16)
            cgb = cg.astype(BF16)
            cbm = _dot(cgb, bgb, _NT)
            st_in = st_ref[0, 0, g]
            st_inb = st_in.astype(BF16)
            d_st = dst_ref[g]
            d_stb = d_st.astype(BF16)
            y_offs.append(_dot(cgb, st_inb) * ea_x[:, gs])
            dxt_states.append(_dot(bgb, d_stb) * dec_x[:, gs])
            d_c = _dot(dyeab[:, gs], st_inb, _NT)
            d_b = _dot(xdb[:, gs], d_stb, _NT)
            d_cb = jnp.zeros((qc, qc), F32)
            for k in range(HEADS_PER_GROUP):
                h = g * HEADS_PER_GROUP + k
                hs = slice(h * HEAD_DIM, (h + 1) * HEAD_DIM)
                seg = a_cs[:, h:h + 1] - a_cst[h:h + 1, :]
                lh = jnp.exp(jnp.where(tril, seg, -1e30))
                ghb = (cbm * lh).astype(BF16)
                d_cb = d_cb + _dot(dyb[:, hs], xtb[:, hs], _NT) * lh
                yd_ref[:, hs] = _dot(ghb, xtb[:, hs])
                dxt_ref[:, hs] = _dot(ghb, dyb[:, hs], _TN)
            d_cbb = d_cb.astype(BF16)
            d_cc.append(d_c + _dot(d_cbb, bgb))
            d_bc.append(d_b + _dot(d_cbb, cgb, _TN))
            end_terms.append(jnp.sum(d_st * stn_ref[0, 0, g], axis=0, keepdims=True))
            dst_ref[g] = d_st * jnp.exp(aend_x[:, gs]) + _dot(cg.T.astype(BF16), dyeab[:, gs])

        y_off = jnp.concatenate(y_offs, axis=1)
        dxt_state = jnp.concatenate(dxt_states, axis=1)
        dxt = dxt_ref[...] + dxt_state
        last = jnp.where(chunk == nc - 1, 0.0, 1.0)
        end_c = collapse(jnp.broadcast_to(jnp.concatenate(end_terms, axis=1), (8, D_BRANCH)))[0:1, :]
        da_cs = collapse(dyb.astype(F32) * yd_ref[...] - dxt_ref[...] * xtb.astype(F32)
                         + dy * y_off - dxt_state * xt)
        da_cs = da_cs + jnp.where(_row_iota(da_cs.shape) == qc - 1, end_c * last, 0.0)
        triu = lax.broadcasted_iota(jnp.int32, (qc, qc), 0) <= lax.broadcasted_iota(jnp.int32, (qc, qc), 1)
        dd_a = _dot01(da_cs, jnp.where(triu, 1.0, 0.0).astype(BF16), 3, m_left=True)
        ddt = dd_a * (-jnp.exp(al_ref[...])) + collapse(dxt * xs)
        head_lanes = _lane_iota(ddt.shape) < N_HEADS
        ddt_raw = jnp.where(head_lanes, ddt * _sigmoid(dt_in), 0.0)
        gal_ref[...] += jnp.sum(jnp.where(head_lanes, dd_a * d_a, 0.0), axis=0, keepdims=True)
        gdtb_ref[...] += jnp.sum(ddt_raw, axis=0, keepdims=True)

        dpre_xs = (dxt * dt_x + d_x * dy) * dsilu_xs
        dpre_bc = jnp.concatenate(d_bc + d_cc, axis=1) * dsilu_bc
        gcb_ref[...] += jnp.concatenate([jnp.sum(dpre_xs, axis=0, keepdims=True),
                                         jnp.sum(dpre_bc, axis=0, keepdims=True)], axis=1)
        nxs = nxs_ref[...]
        nbc = nbc_ref[...]
        du_xs = jnp.zeros_like(dpre_xs)
        du_bc = jnp.zeros_like(dpre_bc)
        for i in range(CONV_TAPS):
            k = CONV_TAPS - 1 - i
            gcw_ref[i:i + 1, :] += jnp.concatenate(
                [jnp.sum(dpre_xs * taps_xs[i], axis=0, keepdims=True),
                 jnp.sum(dpre_bc * taps_bc[i], axis=0, keepdims=True)], axis=1)
            du_xs = du_xs + _shift_up(dpre_xs, nxs, k) * cw[i:i + 1, :D_BRANCH]
            du_bc = du_bc + _shift_up(dpre_bc, nbc, k) * cw[i:i + 1, D_BRANCH:]
        nxs_ref[...] = dpre_xs
        nbc_ref[...] = dpre_bc

        dx_ref[:, :D_BRANCH] = du_xs.astype(BF16)
        dx_ref[:, D_BRANCH:D_CONV] = du_bc.astype(BF16)
        dx_ref[:, D_CONV:D_CONV + LANES] = ddt_raw.astype(BF16)
        dx_ref[:, D_CONV + LANES:] = jnp.zeros((qc, 2048 - D_CONV - LANES), BF16)

    rev = lambda b, c: b * nc + (nc - 1 - c)
    prv = lambda b, c: b * nc + jnp.maximum(nc - 2 - c, 0)
    nblk = lambda w, off, f: pl.BlockSpec((qc, w), lambda b, c: (f(b, c), off))
    full = lambda r, w: pl.BlockSpec((r, w), lambda b, c: (0, 0))
    st_spec = lambda f: pl.BlockSpec((1, 1, N_GROUPS, D_STATE, GROUP_W),
                                     lambda b, c: (b, f(c), 0, 0, 0))
    return _call_with_exchange(
        body, (proj, proj, proj, proj, proj, d_y, states, states, conv_w, conv_b, dtb, alog, dskip),
        slabs, (True,) * len(slabs), name="ssd_bwd", grid=(nb, nc),
        in_specs=[nblk(D_BRANCH, COL_XS // D_BRANCH, rev), nblk(D_BC, COL_BC // D_BC, rev),
                  nblk(LANES, COL_DT // LANES, rev),
                  nblk(D_BRANCH, COL_XS // D_BRANCH, prv), nblk(D_BC, COL_BC // D_BC, prv),
                  nblk(D_BRANCH, 0, rev),
                  st_spec(lambda c: nc - 1 - c), st_spec(lambda c: jnp.minimum(nc - c, nc - 1)),
                  full(CONV_TAPS, D_CONV), full(1, D_CONV), full(1, LANES), full(1, LANES),
                  full(1, LANES)],
        out_specs=[nblk(2048, 0, rev), full(8, D_CONV), full(1, D_CONV), full(1, LANES),
                   full(1, LANES), full(1, LANES)],
        out_shape=[jax.ShapeDtypeStruct((nb * seq, 2048), BF16),
                   jax.ShapeDtypeStruct((8, D_CONV), F32), jax.ShapeDtypeStruct((1, D_CONV), F32),
                   jax.ShapeDtypeStruct((1, LANES), F32), jax.ShapeDtypeStruct((1, LANES), F32),
                   jax.ShapeDtypeStruct((1, LANES), F32)],
        scratch_shapes=[pltpu.VMEM((N_GROUPS, D_STATE, GROUP_W), F32),
                        pltpu.VMEM((qc, D_BRANCH), F32), pltpu.VMEM((qc, D_BC), F32),
                        pltpu.VMEM((qc, D_BRANCH), F32), pltpu.VMEM((qc, D_BRANCH), F32)])


def _mid(o_sb, y_ssd, proj, x2, target, sb_w, ssd_w, w_out_b, tm):
    t = x2.shape[0]
    inv_d = 1.0 / D_MODEL

    def body(o_ref, y_ref, zsb_ref, zssd_ref, x_ref, tg_ref, sbw_ref, ssdw_ref, w_ref,
             dout_ref, dosb_ref, dy_ref, dz_ref, gw_ref, gsb_ref, gssd_ref, loss_ref):
        @pl.when(pl.program_id(0) == 0)
        def _():
            gw_ref[...] = jnp.zeros_like(gw_ref)
            gsb_ref[...] = jnp.zeros_like(gsb_ref)
            gssd_ref[...] = jnp.zeros_like(gssd_ref)
            loss_ref[...] = jnp.zeros_like(loss_ref)

        def branch(val, z, w):
            gate, dgate = _silu_and_grad(z)
            g = val * gate
            r = lax.rsqrt(jnp.mean(g * g, axis=1, keepdims=True) + EPS)
            xhat = g * r
            return (gate, dgate, r, xhat), (xhat * w).astype(BF16)

        o = o_ref[...]
        y = y_ref[...]
        saved_a, mix_a = branch(o, zsb_ref[...], sbw_ref[...])
        saved_b, mix_b = branch(y, zssd_ref[...], ssdw_ref[...])
        out = x_ref[...] + _dot(mix_a, w_ref[:D_BRANCH, :]) + _dot(mix_b, w_ref[D_BRANCH:, :])
        diff = out - tg_ref[...]
        loss_ref[...] += 0.5 * inv_d * jnp.sum(diff * diff)
        d_out = diff * inv_d
        dout_ref[...] = d_out
        d_outb = d_out.astype(BF16)
        gw_ref[:D_BRANCH, :] += _dot(mix_a, d_outb, _TN)
        gw_ref[D_BRANCH:, :] += _dot(mix_b, d_outb, _TN)

        def branch_bwd(dmix, val, w, saved):
            gate, dgate, r, xhat = saved
            gg = dmix * w
            m = jnp.mean(gg * xhat, axis=1, keepdims=True)
            dg = r * (gg - xhat * m)
            return dg * gate, dg * val * dgate, jnp.sum(dmix * xhat, axis=0, keepdims=True)

        dmix_a = _dot(d_outb, w_ref[:D_BRANCH, :], _NT)
        dmix_b = _dot(d_outb, w_ref[D_BRANCH:, :], _NT)
        d_o, dz_a, gsb = branch_bwd(dmix_a, o, sbw_ref[...], saved_a)
        d_y, dz_b, gssd = branch_bwd(dmix_b, y, ssdw_ref[...], saved_b)
        dosb_ref[...] = d_o
        dy_ref[...] = d_y
        dz_ref[:, :D_BRANCH] = dz_a.astype(BF16)
        dz_ref[:, D_BRANCH:] = dz_b.astype(BF16)
        gsb_ref[...] += gsb
        gssd_ref[...] += gssd

    row = lambda w, off: pl.BlockSpec((tm, w), lambda i: (i, off))
    full = lambda r, w: pl.BlockSpec((r, w), lambda i: (0, 0))
    resident = pl.BlockSpec((2 * D_BRANCH, D_MODEL), lambda i: (0, 0), pipeline_mode=pl.Buffered(1))
    tok = jax.ShapeDtypeStruct((t, D_MODEL), F32)
    return pl.pallas_call(
        body, name="mid",
        grid=(t // tm,),
        in_specs=[row(D_BRANCH, 0), row(D_BRANCH, 0), row(D_BRANCH, 3), row(D_BRANCH, 4),
                  row(D_MODEL, 0), row(D_MODEL, 0), full(1, D_BRANCH), full(1, D_BRANCH),
                  resident],
        out_specs=[row(D_MODEL, 0), row(D_BRANCH, 0), row(D_BRANCH, 0), row(2 * D_BRANCH, 0),
                   resident, full(1, D_BRANCH), full(1, D_BRANCH),
                   full(1, LANES)],
        out_shape=[tok, tok, tok, jax.ShapeDtypeStruct((t, 2 * D_BRANCH), BF16),
                   jax.ShapeDtypeStruct((2 * D_BRANCH, D_MODEL), F32),
                   jax.ShapeDtypeStruct((1, D_BRANCH), F32), jax.ShapeDtypeStruct((1, D_BRANCH), F32),
                   jax.ShapeDtypeStruct((1, LANES), F32)],
        compiler_params=_params(1),
    )(o_sb, y_ssd, proj, proj, x2, target, sb_w, ssd_w, w_out_b)


_DPROJ_FIRST = (0, 1, 2, 3, 5)
_DPROJ_BLOCKS = (1, 1, 1, 2, 2)


def _in_proj_bwd_x(d_parts, w_in_t, x2, d_out, norm_w, tm, slabs=()):
    t = x2.shape[0]
    n_parts = len(d_parts)

    def body(*refs):
        dp_refs = refs[:n_parts]
        w_ref, x_ref, dout_ref, nw_ref, gx_ref, gnw_ref = refs[n_parts:]

        @pl.when(pl.program_id(0) == 0)
        def _():
            gnw_ref[...] = jnp.zeros_like(gnw_ref)

        d_hn = None
        for p in range(n_parts):
            rows = slice(_DPROJ_FIRST[p] * 1024, (_DPROJ_FIRST[p] + _DPROJ_BLOCKS[p]) * 1024)
            term = _dot(dp_refs[p][...], w_ref[rows, :])
            d_hn = term if d_hn is None else d_hn + term
        xf = x_ref[...]
        r = lax.rsqrt(jnp.mean(xf * xf, axis=1, keepdims=True) + EPS)
        xhat = xf * r
        g = d_hn * nw_ref[...]
        m = jnp.mean(g * xhat, axis=1, keepdims=True)
        gx_ref[...] = dout_ref[...] + r * (g - xhat * m)
        gnw_ref[...] += jnp.sum(d_hn * xhat, axis=0, keepdims=True)

    row = lambda w: pl.BlockSpec((tm, w), lambda i: (i, 0))
    return _call_with_exchange(
        body, (*d_parts, w_in_t, x2, d_out, norm_w), slabs, (True,) * len(slabs),
        name="in_proj_bwd_x", grid=(t // tm,),
        in_specs=[row(1024 * _DPROJ_BLOCKS[p]) for p in range(n_parts)] + [
                  pl.BlockSpec((D_IN_PAD, D_MODEL), lambda i: (0, 0), pipeline_mode=pl.Buffered(1)),
                  row(D_MODEL), row(D_MODEL), pl.BlockSpec((1, D_MODEL), lambda i: (0, 0))],
        out_specs=[row(D_MODEL), pl.BlockSpec((1, D_MODEL), lambda i: (0, 0))],
        out_shape=[jax.ShapeDtypeStruct((t, D_MODEL), F32), jax.ShapeDtypeStruct((1, D_MODEL), F32)])


def _in_proj_bwd_w(hn, d_parts, tm):
    t = hn.shape[0]
    n_parts = len(d_parts)

    def body(hn_ref, *refs):
        dp_refs, gw_ref = refs[:n_parts], refs[n_parts]

        @pl.when(pl.program_id(0) == 0)
        def _():
            gw_ref[...] = jnp.zeros_like(gw_ref)

        hnt = hn_ref[...].astype(F32).T.astype(BF16)
        for p in range(n_parts):
            cols = slice(_DPROJ_FIRST[p] * 1024, (_DPROJ_FIRST[p] + _DPROJ_BLOCKS[p]) * 1024)
            gw_ref[:, cols] += _dot(hnt, dp_refs[p][...])

    return pl.pallas_call(
        body, name="in_proj_bwd_w",
        grid=(t // tm,),
        in_specs=[pl.BlockSpec((tm, D_MODEL), lambda i: (i, 0))]
                 + [pl.BlockSpec((tm, 1024 * _DPROJ_BLOCKS[p]), lambda i: (i, 0))
                    for p in range(n_parts)],
        out_specs=pl.BlockSpec((D_MODEL, D_IN_PAD), lambda i: (0, 0), pipeline_mode=pl.Buffered(1)),
        out_shape=jax.ShapeDtypeStruct((D_MODEL, D_IN_PAD), F32),
        compiler_params=_params(1),
    )(hn, *d_parts)


def _adamw(parts, w, m, v, tr, name):
    _, rows, cols = w.shape
    c1 = 1.0 - ADAM_B1 ** ADAM_STEP
    c2 = 1.0 - ADAM_B2 ** ADAM_STEP

    def body(p_ref, w_ref, m_ref, v_ref, g_ref, d_ref, nm_ref, nv_ref):
        g = p_ref[0].astype(F32)
        for s in range(1, N_DEV):
            g = g + p_ref[s].astype(F32)
        nm = ADAM_B1 * m_ref[0] + (1.0 - ADAM_B1) * g
        nv = ADAM_B2 * v_ref[0] + (1.0 - ADAM_B2) * (g * g)
        g_ref[0] = g
        nm_ref[0] = nm
        nv_ref[0] = nv
        d_ref[0] = -ADAM_LR * ((nm / c1) / (jnp.sqrt(nv / c2) + ADAM_EPS) + ADAM_WD * w_ref[0])

    blk = pl.BlockSpec((1, tr, cols), lambda i: (0, i, 0))
    shape = jax.ShapeDtypeStruct((1, rows, cols), F32)
    return pl.pallas_call(
        body, name=name,
        grid=(rows // tr,),
        in_specs=[pl.BlockSpec((N_DEV, tr, cols), lambda i: (0, i, 0)), blk, blk, blk],
        out_specs=[blk, blk, blk, blk],
        out_shape=[shape, shape, shape, shape],
        compiler_params=_params(1),
    )(parts, w, m, v)


def _mesh_place():
    x, y, c = lax.axis_index("x"), lax.axis_index("y"), lax.axis_index("c")
    return x, y, c, 4 * x + 2 * y + c


def _peer(x, y, c, k):
    px = 1 - x if k & 4 else x
    py = 1 - y if k & 2 else y
    pc = 1 - c if k & 1 else c
    return (px, py, pc), 4 * px + 2 * py + pc


def _exchange(srcs, scatter, name):
    n = len(srcs)

    def body(*refs):
        copies = _exchange_copies(refs[:n], refs[n:2 * n], scatter, *refs[2 * n:])
        _exchange_start(copies)
        _exchange_wait(copies)

    return pl.pallas_call(
        body, name=name,
        in_specs=[_ANY] * n, out_specs=[_ANY] * n, out_shape=_exchange_shapes(srcs, scatter),
        scratch_shapes=_exchange_sems(n),
    )(*srcs)


def _call_with_exchange(body, operands, srcs, scatter, *, name, grid, in_specs, out_specs,
                        out_shape, scratch_shapes=()):
    n_in, n_out, n_scr, n_x = len(in_specs), len(out_specs), len(scratch_shapes), len(srcs)
    params = _params(len(grid))
    if not n_x:
        return pl.pallas_call(body, name=name, grid=grid, in_specs=list(in_specs),
                              out_specs=list(out_specs), out_shape=list(out_shape),
                              scratch_shapes=list(scratch_shapes), compiler_params=params)(*operands)

    def wrapped(*refs):
        ins, refs = refs[:n_in], refs[n_in:]
        x_src, refs = refs[:n_x], refs[n_x:]
        outs, refs = refs[:n_out], refs[n_out:]
        x_dst, refs = refs[:n_x], refs[n_x:]
        scratch, sems = refs[:n_scr], refs[n_scr:]
        ids = [pl.program_id(a) for a in range(len(grid))]
        first = functools.reduce(jnp.logical_and, [i == 0 for i in ids])
        last = functools.reduce(jnp.logical_and, [i == n - 1 for i, n in zip(ids, grid)])

        @pl.when(first)
        def _():
            _exchange_start(_exchange_copies(x_src, x_dst, scatter, *sems))

        body(*ins, *outs, *scratch)

        @pl.when(last)
        def _():
            _exchange_wait(_exchange_copies(x_src, x_dst, scatter, *sems))

    return pl.pallas_call(
        wrapped, name=name, grid=grid,
        in_specs=list(in_specs) + [_ANY] * n_x, out_specs=list(out_specs) + [_ANY] * n_x,
        out_shape=list(out_shape) + _exchange_shapes(srcs, scatter),
        scratch_shapes=list(scratch_shapes) + _exchange_sems(n_x), compiler_params=params,
    )(*operands, *srcs)


def _gather_two_level(shard, name):
    def body(x_ref, out_ref, send_sems, recv_sems, local_sem):
        x, y, c, me = _mesh_place()
        sibling = (x, y, 1 - c)
        chips = [(1 - x, y), (x, 1 - y), (1 - x, 1 - y)]

        def slab(px, py, pc):
            return out_ref.at[4 * px + 2 * py + pc]

        def copy(k, block, to, src=None):
            return pltpu.make_async_remote_copy(
                src_ref=slab(*block) if src is None else src, dst_ref=slab(*block),
                send_sem=send_sems.at[k], recv_sem=recv_sems.at[k],
                device_id=to, device_id_type=pl.DeviceIdType.MESH)

        mine = pltpu.make_async_copy(x_ref, slab(x, y, c), local_sem)
        mine.start()
        first = [copy(0, (x, y, c), sibling, src=x_ref)]
        first += [copy(1 + j, (x, y, c), (*chip, c), src=x_ref) for j, chip in enumerate(chips)]
        for cp in first:
            cp.start()
        passed = [copy(4 + j, (*chip, c), sibling) for j, chip in enumerate(chips)]
        for j, chip in enumerate(chips):
            copy(1 + j, (*chip, c), (x, y, c)).wait_recv()
            passed[j].start()
        copy(0, sibling, (x, y, c)).wait_recv()
        for j, chip in enumerate(chips):
            copy(4 + j, (*chip, 1 - c), (x, y, c)).wait_recv()
        for cp in first + passed:
            cp.wait_send()
        mine.wait()

    return pl.pallas_call(
        body, name=name,
        in_specs=[_ANY], out_specs=_ANY,
        out_shape=jax.ShapeDtypeStruct((N_DEV,) + shard.shape, shard.dtype),
        scratch_shapes=[pltpu.SemaphoreType.DMA((N_DEV - 1,)), pltpu.SemaphoreType.DMA((N_DEV - 1,)),
                        pltpu.SemaphoreType.DMA],
    )(shard)


_ANY = pl.BlockSpec(memory_space=pl.ANY)


def _exchange_shapes(srcs, scatter):
    return [jax.ShapeDtypeStruct(s.shape if sc else (N_DEV,) + s.shape, s.dtype)
            for s, sc in zip(srcs, scatter)]


def _exchange_sems(n):
    return [pltpu.SemaphoreType.DMA((n * (N_DEV - 1),)),
            pltpu.SemaphoreType.DMA((n * (N_DEV - 1),)),
            pltpu.SemaphoreType.DMA((n,))]


def _exchange_copies(src_refs, dst_refs, scatter, send_sems, recv_sems, loc_sems):
    n = len(src_refs)
    x, y, c, me = _mesh_place()

    def src_of(i, idx):
        return src_refs[i].at[idx] if scatter[i] else src_refs[i]

    local = [pltpu.make_async_copy(src_of(i, me), dst_refs[i].at[me], loc_sems.at[i])
             for i in range(n)]
    sends, recvs = [], []
    for k in range(1, N_DEV):
        peer, pidx = _peer(x, y, c, k)
        for i in range(n):
            s = i * (N_DEV - 1) + k - 1
            for dst_slab, group in ((me, sends), (pidx, recvs)):
                group.append(pltpu.make_async_remote_copy(
                    src_ref=src_of(i, pidx), dst_ref=dst_refs[i].at[dst_slab],
                    send_sem=send_sems.at[s], recv_sem=recv_sems.at[s],
                    device_id=peer, device_id_type=pl.DeviceIdType.MESH))
    return local, sends, recvs


def _exchange_start(copies):
    local, sends, _ = copies
    for cp in local + sends:
        cp.start()


def _exchange_wait(copies):
    local, sends, recvs = copies
    for cp in recvs:
        cp.wait_recv()
    for cp in sends:
        cp.wait_send()
    for cp in local:
        cp.wait()


def _pad_lanes(v, width=LANES):
    return jnp.pad(v, ((0, 0), (0, width - v.shape[1])))


def _local_step(x, target, norm_w, w_in_b, q_norm_w, k_norm_w, conv_w, conv_b, dt_bias, a_log,
                d_skip, sb_norm_w, ssd_norm_w, w_out_b, tm=256, tq=512, tmid=256, blk=ATT_BLK,
                scatter=False, w_in_t=None):
    nb, seq, _ = x.shape
    t = nb * seq
    x2 = x.reshape(t, D_MODEL)
    tg2 = target.reshape(t, D_MODEL)
    qw2 = jnp.tile(q_norm_w, (1, 2))
    kw2 = jnp.tile(k_norm_w, (1, 2))
    dtb, alog, dsk = _pad_lanes(dt_bias), _pad_lanes(a_log), _pad_lanes(d_skip)

    if w_in_t is None:
        w_in_t = w_in_b.T
    if scatter:
        proj, hn, wout_all, cw_all = _in_proj(x2, norm_w, w_in_b, tm, (w_out_b, conv_w))
        w_out_b = wout_all.reshape(2 * D_BRANCH, D_MODEL)
        conv_w = jnp.transpose(cw_all, (1, 0, 2)).reshape(CONV_TAPS, D_CONV)
    else:
        proj, hn = _in_proj(x2, norm_w, w_in_b, tm)
    qs, kn, vb, kt, ksq = _qk_prep(proj, qw2, kw2, nb, seq, tq)
    o_sb, sb_tot, sb_low = _attn_fwd(qs, kn, vb, jnp.max(ksq, axis=1), nb, seq, blk)
    y_ssd, states = _ssd_fwd(proj, conv_w, conv_b, dtb, alog, dsk, nb, seq)
    d_out, d_osb, d_y, d_z, g_wout, g_sbw, g_ssdw, loss = _mid(
        o_sb, y_ssd, proj, x2, tg2, sb_norm_w, ssd_norm_w, w_out_b, tmid)
    dqs, dkn, dvh = _attn_bwd(qs, kn, kt, vb, sb_tot, sb_low, d_osb, nb, seq, blk)
    dq_raw, dk_raw, dv_raw, g_qw, g_kw = _qk_bwd(proj, dqs, dkn, dvh, qw2, kw2, nb, seq, tq)
    wout_slabs = (g_wout.reshape(N_DEV, 2 * D_BRANCH // N_DEV, D_MODEL).astype(BF16),)
    d_xbc, g_cw, g_cb, g_dtb, g_alog, g_dsk, *moved = _ssd_bwd(
        proj, d_y, states, conv_w, conv_b, dtb, alog, dsk, nb, seq, wout_slabs if scatter else ())
    d_parts = [dq_raw, dk_raw, dv_raw, d_z, d_xbc]
    g_win = _in_proj_bwd_w(hn, d_parts, min(2 * tm, t))[:, :D_IN]
    g_cw = g_cw[:CONV_TAPS]
    if scatter:
        g_wout, = moved
        grad_x, g_nw, g_win, g_cw = _in_proj_bwd_x(
            d_parts, w_in_t, x2, d_out, norm_w, tm, _grad_slabs(g_win, g_cw))
    else:
        grad_x, g_nw = _in_proj_bwd_x(d_parts, w_in_t, x2, d_out, norm_w, tm)

    small = dict(
        norm_w=g_nw,
        q_norm_w=g_qw[:, :HEAD_DIM] + g_qw[:, HEAD_DIM:],
        k_norm_w=g_kw[:, :HEAD_DIM] + g_kw[:, HEAD_DIM:],
        conv_b=g_cb, dt_bias=g_dtb[:, :N_HEADS], A_log=g_alog[:, :N_HEADS],
        D_skip=g_dsk[:, :N_HEADS], sb_norm_w=g_sbw, ssd_norm_w=g_ssdw)
    return loss[0, 0], grad_x.reshape(nb, seq, D_MODEL), g_win, g_wout, g_cw, small


def _grad_slabs(g_win, g_cw):
    w_sh = D_IN // N_DEV
    c_sh = D_CONV // N_DEV
    return (jnp.transpose(g_win.reshape(D_MODEL, N_DEV, w_sh), (1, 0, 2)).astype(BF16),
            jnp.pad(jnp.transpose(g_cw.reshape(CONV_TAPS, N_DEV, c_sh), (1, 0, 2)),
                    ((0, 0), (0, 8 - CONV_TAPS), (0, 0))))


_SMALL = ("norm_w", "q_norm_w", "k_norm_w", "conv_b", "dt_bias", "A_log", "D_skip",
          "sb_norm_w", "ssd_norm_w")


def _pack_small(vals):
    flat = jnp.concatenate([_pad_lanes(vals[n], -(-vals[n].shape[1] // LANES) * LANES)
                            for n in _SMALL], axis=1)
    return jnp.pad(flat, ((0, 0), (0, 48 * LANES - flat.shape[1]))).reshape(48, LANES)


def _unpack_small(packed, like):
    out, r = {}, 0
    for n in _SMALL:
        width = like[n].shape[1]
        nr = -(-width // LANES)
        out[n] = packed[r:r + nr].reshape(1, nr * LANES)[:, :width]
        r += nr
    return out


def kernel(x, norm_w, w_in, q_norm_w, k_norm_w, conv_w, conv_b, dt_bias, A_log, D_skip, sb_norm_w, ssd_norm_w, w_out, loss_target, m_norm_w, m_w_in, m_q_norm_w, m_k_norm_w, m_conv_w, m_conv_b, m_dt_bias, m_A_log, m_D_skip, m_sb_norm_w, m_ssd_norm_w, m_w_out, v_norm_w, v_w_in, v_q_norm_w, v_k_norm_w, v_conv_w, v_conv_b, v_dt_bias, v_A_log, v_D_skip, v_sb_norm_w, v_ssd_norm_w, v_w_out):
    win_all = _gather_two_level(w_in[0].astype(BF16), "gather_w_in")
    w_in_b = jnp.pad(jnp.transpose(win_all, (1, 0, 2)).reshape(D_MODEL, D_IN),
                     ((0, 0), (0, D_IN_PAD - D_IN)))
    w_in_t = jnp.pad(jnp.transpose(win_all, (0, 2, 1)).reshape(D_IN, D_MODEL),
                     ((0, D_IN_PAD - D_IN), (0, 0)))

    loss, grad_x, win_parts, wout_parts, cw_parts, g_small = _local_step(
        x, loss_target, norm_w, w_in_b, q_norm_w, k_norm_w, conv_w[0], conv_b, dt_bias, A_log,
        D_skip, sb_norm_w, ssd_norm_w, w_out[0].astype(BF16), scatter=True, w_in_t=w_in_t)
    packed = _pack_small(g_small).at[-1, 0].set(loss)
    small_parts, = _exchange([packed], [False], "gather_small_grads")
    loss = jnp.sum(small_parts[:, -1, 0])

    small_w = dict(norm_w=norm_w, q_norm_w=q_norm_w, k_norm_w=k_norm_w, conv_b=conv_b,
                   dt_bias=dt_bias, A_log=A_log, D_skip=D_skip, sb_norm_w=sb_norm_w,
                   ssd_norm_w=ssd_norm_w)
    small_m = dict(norm_w=m_norm_w, q_norm_w=m_q_norm_w, k_norm_w=m_k_norm_w, conv_b=m_conv_b,
                   dt_bias=m_dt_bias, A_log=m_A_log, D_skip=m_D_skip, sb_norm_w=m_sb_norm_w,
                   ssd_norm_w=m_ssd_norm_w)
    small_v = dict(norm_w=v_norm_w, q_norm_w=v_q_norm_w, k_norm_w=v_k_norm_w, conv_b=v_conv_b,
                   dt_bias=v_dt_bias, A_log=v_A_log, D_skip=v_D_skip, sb_norm_w=v_sb_norm_w,
                   ssd_norm_w=v_ssd_norm_w)

    pad8 = lambda a: jnp.pad(a, ((0, 0), (0, 8 - CONV_TAPS), (0, 0)))
    r_win = _adamw(win_parts, w_in, m_w_in, v_w_in, 128, "adamw_w_in")
    r_wout = _adamw(wout_parts, w_out, m_w_out, v_w_out, 128, "adamw_w_out")
    r_cw = _adamw(cw_parts, pad8(conv_w), pad8(m_conv_w), pad8(v_conv_w), 8, "adamw_conv_w")
    r_small = _adamw(small_parts, _pack_small(small_w)[None], _pack_small(small_m)[None],
                     _pack_small(small_v)[None], 48, "adamw_small")

    res = {"w_in": r_win, "w_out": r_wout, "conv_w": [a[:, :CONV_TAPS] for a in r_cw]}
    unpacked = [_unpack_small(a[0], small_w) for a in r_small]
    for n in _SMALL:
        res[n] = [u[n] for u in unpacked]
    order = ("norm_w", "w_in", "q_norm_w", "k_norm_w", "conv_w", "conv_b", "dt_bias", "A_log",
             "D_skip", "sb_norm_w", "ssd_norm_w", "w_out")
    outs = [loss, grad_x]
    for kind in range(4):
        outs += [res[n][kind] for n in order]
    return tuple(outs)
```

```python
import functools
import math

import jax
import jax.numpy as jnp
from jax import lax
from jax.experimental import pallas as pl
from jax.experimental.pallas import tpu as pltpu

F32 = jnp.float32
BF16 = jnp.bfloat16

D_MODEL = 1024
N_HEADS = 16
HEAD_DIM = 64
N_PAIRS = N_HEADS // 2
D_BRANCH = 1024
N_GROUPS = 2
HEADS_PER_GROUP = 8
D_STATE = 128
GROUP_W = HEADS_PER_GROUP * HEAD_DIM
D_BC = 2 * N_GROUPS * D_STATE
D_CONV = D_BRANCH + D_BC
D_IN = 6672
COLBLK = 1024
D_IN_PAD = 7168
N_COLBLK = D_IN_PAD // COLBLK
COL_XS = 5120
COL_BC = 6144
COL_DT = 6656
EPS = 1e-6
CONV_TAPS = 4
N_DEV = 8

LANES = 128
SSD_CHUNK = 128
ATT_BLK = 256
ATT_HEADS = 4
ATT_W = ATT_HEADS * HEAD_DIM
N_ATT_GROUPS = N_HEADS // ATT_HEADS
EXP_UNDERFLOW = -105.0
VMEM_LIMIT = 56 * 1024 * 1024

ADAM_LR = 0.001
ADAM_B1 = 0.9
ADAM_B2 = 0.999
ADAM_EPS = 1e-08
ADAM_WD = 0.01
ADAM_STEP = 10

_NT = (((1,), (1,)), ((), ()))
_TN = (((0,), (0,)), ((), ()))


def _params(n_grid):
    return pltpu.CompilerParams(dimension_semantics=("arbitrary",) * n_grid,
                                vmem_limit_bytes=VMEM_LIMIT)


def _dot(a, b, dims=None):
    if dims is None:
        return jnp.dot(a, b, preferred_element_type=F32)
    return lax.dot_general(a, b, dims, preferred_element_type=F32)


def _sigmoid(x):
    return 1.0 / (1.0 + jnp.exp(-x))


def _softplus(x):
    return jnp.maximum(x, 0.0) + jnp.log(1.0 + jnp.exp(-jnp.abs(x)))


def _split_bf16(x):
    hi = x.astype(BF16)
    lo = (x - hi.astype(F32)).astype(BF16)
    return hi, lo


def _lane_iota(shape):
    return lax.broadcasted_iota(jnp.int32, shape, len(shape) - 1)


def _row_iota(shape):
    return lax.broadcasted_iota(jnp.int32, shape, len(shape) - 2)


def _pair_sum(x):
    r = lax.broadcasted_iota(jnp.int32, (LANES, LANES), 0)
    c = lax.broadcasted_iota(jnp.int32, (LANES, LANES), 1)
    same_head = jnp.where(r // HEAD_DIM == c // HEAD_DIM, 1.0, 0.0).astype(BF16)
    hi, lo = _split_bf16(x)
    return _dot(hi, same_head) + _dot(lo, same_head)


def _head_lanes(x, a):
    lane = _lane_iota(x.shape)
    mine = (lane >= a * HEAD_DIM) & (lane < (a + 1) * HEAD_DIM)
    return jnp.where(mine, x, jnp.zeros_like(x))


def _head_expand():
    r = lax.broadcasted_iota(jnp.int32, (LANES, D_BRANCH), 0)
    c = lax.broadcasted_iota(jnp.int32, (LANES, D_BRANCH), 1)
    return jnp.where(c // HEAD_DIM == r, 1.0, 0.0).astype(BF16)


def _in_proj(x2, norm_w, w_in_b, tm, shards=()):
    t = x2.shape[0]

    def body(x_ref, nw_ref, w_ref, proj_ref, hn_ref):
        xf = x_ref[...]
        r = lax.rsqrt(jnp.mean(xf * xf, axis=1, keepdims=True) + EPS)
        hn = (xf * r * nw_ref[...]).astype(BF16)
        hn_ref[...] = hn
        for j in range(N_COLBLK):
            cols = slice(j * COLBLK, (j + 1) * COLBLK)
            proj_ref[:, cols] = _dot(hn, w_ref[:, cols])

    return _call_with_exchange(
        body, (x2, norm_w, w_in_b), shards, (False,) * len(shards), name="in_proj",
        grid=(t // tm,),
        in_specs=[pl.BlockSpec((tm, D_MODEL), lambda i: (i, 0)),
                  pl.BlockSpec((1, D_MODEL), lambda i: (0, 0)),
                  pl.BlockSpec((D_MODEL, D_IN_PAD), lambda i: (0, 0), pipeline_mode=pl.Buffered(1))],
        out_specs=[pl.BlockSpec((tm, D_IN_PAD), lambda i: (i, 0)),
                   pl.BlockSpec((tm, D_MODEL), lambda i: (i, 0))],
        out_shape=[jax.ShapeDtypeStruct((t, D_IN_PAD), F32),
                   jax.ShapeDtypeStruct((t, D_MODEL), BF16)])


def _qk_prep(proj, qw2, kw2, nb, seq, tq):
    nl = seq // tq
    scale = 1.0 / math.sqrt(HEAD_DIM)

    def body(q_ref, k_ref, v_ref, qw_ref, kw_ref, qs_ref, kn_ref, vb_ref, kt_ref, ksq_ref):
        def norm(x, w):
            r = lax.rsqrt(_pair_sum(x * x) * (1.0 / HEAD_DIM) + EPS)
            return x * r * w

        vb_ref[...] = v_ref[...].astype(BF16)
        for p in range(N_PAIRS):
            cols = slice(p * LANES, (p + 1) * LANES)
            kn = norm(k_ref[:, cols], kw_ref[...])
            knb = kn.astype(BF16)
            qs_ref[:, cols] = (norm(q_ref[:, cols], qw_ref[...]) * scale).astype(BF16)
            kn_ref[:, cols] = knb
            kt_ref[0, p] = kn.T.astype(BF16)
            kf = knb.astype(F32)
            ksq_ref[0, 0, p:p + 1, :] = jnp.max(_pair_sum(kf * kf), axis=0, keepdims=True) * 1.0001

    tok_shape = jax.ShapeDtypeStruct((nb * seq, D_BRANCH), BF16)
    tok = lambda blk: pl.BlockSpec((tq, D_BRANCH), lambda b, i: (b * nl + i, blk))
    vec = pl.BlockSpec((1, LANES), lambda b, i: (0, 0))
    return pl.pallas_call(
        body, name="qk_prep",
        grid=(nb, nl),
        in_specs=[tok(0), tok(1), tok(2), vec, vec],
        out_specs=[tok(0), tok(0), tok(0),
                   pl.BlockSpec((1, N_PAIRS, LANES, tq), lambda b, i: (b, 0, 0, i)),
                   pl.BlockSpec((1, 1, N_PAIRS, LANES), lambda b, i: (b, i, 0, 0))],
        out_shape=[tok_shape, tok_shape, tok_shape,
                   jax.ShapeDtypeStruct((nb, N_PAIRS, LANES, seq), BF16),
                   jax.ShapeDtypeStruct((nb, nl, N_PAIRS, LANES), F32)],
        compiler_params=_params(2),
    )(proj, proj, proj, qw2, kw2)


def _attn_fwd(qs, kn, vb, ksq, nb, seq, blk):
    nq = seq // blk

    def body(q_ref, k_ref, v_ref, ksq_ref, o_ref, tot_ref, low_ref):
        qi = pl.program_id(2)
        r_i = lax.broadcasted_iota(jnp.int32, (blk, blk), 0)
        c_i = lax.broadcasted_iota(jnp.int32, (blk, blk), 1)
        csum = jnp.where(r_i >= c_i, 1.0, 0.0).astype(BF16)
        causal = c_i < r_i
        heads = range(ATT_HEADS)
        head = _head_lanes

        q_blk = q_ref[...]
        qf = q_blk.astype(F32)
        q_head = [head(q_blk, a) for a in heads]
        zmax = []
        for a in heads:
            qsq = jnp.sum(head(qf * qf, a), axis=1, keepdims=True)
            kmax = ksq_ref[0, 0, a // 2:a // 2 + 1, (a % 2) * HEAD_DIM:(a % 2) * HEAD_DIM + 1]
            zmax.append(1.01 * jnp.sqrt(qsq * kmax) + 0.01)

        def exhausted(run):
            top = functools.reduce(jnp.maximum, [jnp.max(run[a] + zmax[a]) for a in heads])
            return top < EXP_UNDERFLOW

        def sweep(blocks, run, acc):
            offs = [pl.multiple_of(j * blk, blk) for j, _, _ in blocks]
            z = [[_dot(q_head[a], k_ref[pl.ds(off, blk), :], _NT) for a in heads]
                 for off in offs]
            cl = []
            for (_, diag, valid), zb in zip(blocks, z):
                lkb = []
                for a in heads:
                    lk = -_softplus(zb[a])
                    if diag:
                        lk = jnp.where(causal, lk, 0.0)
                    if valid is not None:
                        lk = jnp.where(valid, lk, 0.0)
                    lkb.append(lk.astype(BF16))
                cl.append([_dot(lkb[a], csum) for a in heads])
            for (_, diag, valid), zb, clb, off in zip(blocks, z, cl, offs):
                w = []
                for a in heads:
                    wa = jnp.exp(zb[a] + clb[a] + run[a])
                    if diag:
                        wa = jnp.where(causal, wa, 0.0)
                    if valid is not None:
                        wa = jnp.where(valid, wa, 0.0)
                    w.append(wa.astype(BF16))
                run = [run[a] + clb[a][:, 0:1] for a in heads]
                v_blk = v_ref[pl.ds(off, blk), :]
                for a in heads:
                    acc = acc + _dot(w[a], head(v_blk, a))
            return run, acc

        run = [jnp.zeros((blk, 1), F32)] * ATT_HEADS
        acc = jnp.zeros((blk, ATT_W), F32)
        run, acc = sweep([(qi, True, None), (jnp.maximum(qi - 1, 0), False, qi >= 1)], run, acc)
        low = jnp.maximum(qi - 1, 0)

        def more(carry):
            low, done, _, _ = carry
            return (low > 0) & jnp.logical_not(done)

        def pair(carry):
            low, _, run, acc = carry
            run, acc = sweep([(low - 1, False, None), (jnp.maximum(low - 2, 0), False, low >= 2)],
                             run, acc)
            return jnp.maximum(low - 2, 0), exhausted(run), run, acc

        low, _, run, acc = lax.while_loop(more, pair, (low, exhausted(run), run, acc))
        low_ref[pl.program_id(0) * N_ATT_GROUPS + pl.program_id(1), qi] = low.astype(F32)
        o_ref[...] = acc
        for a in heads:
            as_row = jnp.sum(jnp.where(r_i == c_i, run[a], 0.0), axis=0, keepdims=True)
            tot_ref[0, a, 0] = jnp.broadcast_to(as_row, (8, blk))

    return pl.pallas_call(
        body, name="sb_attn_fwd",
        grid=(nb, N_ATT_GROUPS, nq),
        in_specs=[pl.BlockSpec((blk, ATT_W), lambda b, h, i: (b * nq + i, h)),
                  pl.BlockSpec((seq, ATT_W), lambda b, h, i: (b, h)),
                  pl.BlockSpec((seq, ATT_W), lambda b, h, i: (b, h)),
                  pl.BlockSpec((1, 1, ATT_HEADS // 2, LANES), lambda b, h, i: (b, h, 0, 0))],
        out_specs=[pl.BlockSpec((blk, ATT_W), lambda b, h, i: (b * nq + i, h)),
                   pl.BlockSpec((1, ATT_HEADS, 1, 8, blk), lambda b, h, i: (b, h, i, 0, 0)),
                   pl.BlockSpec(memory_space=pltpu.SMEM)],
        out_shape=[jax.ShapeDtypeStruct((nb * seq, D_BRANCH), F32),
                   jax.ShapeDtypeStruct((nb, N_HEADS, nq, 8, blk), F32),
                   jax.ShapeDtypeStruct((nb * N_ATT_GROUPS, nq), F32)],
        compiler_params=_params(3),
    )(qs, kn, vb, ksq.reshape(nb, N_ATT_GROUPS, ATT_HEADS // 2, LANES))


def _attn_bwd(qs, kn, kt, vb, tot, low, d_o, nb, seq, blk):
    nq = seq // blk

    def body(q_ref, k_ref, kt_ref, v_ref, tot_ref, low_ref, do_ref, dq_ref, dk_ref, dv_ref):
        qi = pl.program_id(2)

        @pl.when(qi == 0)
        def _():
            dk_ref[...] = jnp.zeros_like(dk_ref)
            dv_ref[...] = jnp.zeros_like(dv_ref)

        r_i = lax.broadcasted_iota(jnp.int32, (blk, blk), 0)
        c_i = lax.broadcasted_iota(jnp.int32, (blk, blk), 1)
        before = jnp.where(c_i < r_i, 1.0, 0.0).astype(BF16)
        upto = jnp.where(c_i <= r_i, 1.0, 0.0).astype(BF16)
        causal = r_i < c_i

        heads = range(ATT_HEADS)
        q_head = [_head_lanes(q_ref[...], a) for a in heads]
        d_ob = [_head_lanes(do_ref[...].astype(BF16), a) for a in heads]
        total = [tot_ref[0, a, 0][0:1, :] for a in heads]

        def sweep(blocks, lsum, esum, dqt):
            def keep(x, diag, valid):
                if diag:
                    x = jnp.where(causal, x, 0.0)
                if valid is not None:
                    x = jnp.where(valid, x, 0.0)
                return x

            offs = [pl.multiple_of(j * blk, blk) for j, _, _ in blocks]
            zt = [[_dot(k_ref[pl.ds(off, blk), :], q_head[a], _NT) for a in heads]
                  for off in offs]
            dwt = [[_dot(v_ref[pl.ds(off, blk), :], d_ob[a], _NT) for a in heads]
                   for off in offs]
            sp, lk, lpre = [], [], []
            for (_, diag, valid), ztb in zip(blocks, zt):
                sp.append([_softplus(ztb[a]) for a in heads])
                lk.append([keep(-sp[-1][a], diag, valid).astype(BF16) for a in heads])
                lpre.append([_dot(before, lk[-1][a]) for a in heads])
            wt, et, epre = [], [], []
            for i, (_, diag, valid) in enumerate(blocks):
                wt.append([keep(jnp.exp(zt[i][a] + (total[a] - lsum[a] - lpre[i][a])), diag, valid)
                           for a in heads])
                et.append([dwt[i][a] * wt[i][a] for a in heads])
                split = [_split_bf16(et[i][a]) for a in heads]
                epre.append([_dot(upto, split[a][0]) + _dot(upto, split[a][1]) for a in heads])
                lsum = [lsum[a] + lpre[i][a][blk - 1:blk, :] + lk[i][a][blk - 1:blk, :]
                        for a in heads]
            for i, (_, diag, valid) in enumerate(blocks):
                dzb = [keep(et[i][a] - jnp.exp(zt[i][a] - sp[i][a]) * (esum[a] + epre[i][a]),
                            diag, valid).astype(BF16) for a in heads]
                esum = [esum[a] + epre[i][a][blk - 1:blk, :] for a in heads]
                dk_ref[pl.ds(offs[i], blk), :] += functools.reduce(
                    jnp.add, [_dot(dzb[a], q_head[a]) for a in heads])
                dv_ref[pl.ds(offs[i], blk), :] += functools.reduce(
                    jnp.add, [_dot(wt[i][a].astype(BF16), d_ob[a]) for a in heads])
                dqt = [dqt[a] + _dot(kt_ref[0, a // 2, (a % 2) * HEAD_DIM:(a % 2 + 1) * HEAD_DIM,
                                            pl.ds(offs[i], blk)], dzb[a]) for a in heads]
            return lsum, esum, dqt

        row = [jnp.zeros((1, blk), F32)] * ATT_HEADS
        dqt = [jnp.zeros((HEAD_DIM, blk), F32)] * ATT_HEADS
        low = low_ref[pl.program_id(0) * N_ATT_GROUPS + pl.program_id(1), qi].astype(jnp.int32)
        low = jnp.clip(low, 0, jnp.maximum(qi - 1, 0))

        def pair(carry):
            j, lsum, esum, dqt = carry
            return (j + 2,) + sweep([(j, False, None), (j + 1, False, j + 1 < qi - 1)],
                                    lsum, esum, dqt)

        _, lsum, esum, dqt = lax.while_loop(lambda c: c[0] < qi - 1, pair, (low, row, row, dqt))
        _, _, dqt = sweep([(jnp.maximum(qi - 1, 0), False, qi >= 1), (qi, True, None)],
                          lsum, esum, dqt)
        dq_ref[...] = jnp.concatenate(dqt, axis=0).T

    seq_blk = pl.BlockSpec((seq, ATT_W), lambda b, h, i: (b, h))
    tok = pl.BlockSpec((blk, ATT_W), lambda b, h, i: (b * nq + i, h))
    tok_shape = jax.ShapeDtypeStruct((nb * seq, D_BRANCH), F32)
    return pl.pallas_call(
        body, name="sb_attn_bwd",
        grid=(nb, N_ATT_GROUPS, nq),
        in_specs=[tok, seq_blk,
                  pl.BlockSpec((1, ATT_HEADS // 2, LANES, seq), lambda b, h, i: (b, h, 0, 0)),
                  seq_blk,
                  pl.BlockSpec((1, ATT_HEADS, 1, 8, blk), lambda b, h, i: (b, h, i, 0, 0)),
                  pl.BlockSpec(memory_space=pltpu.SMEM),
                  tok],
        out_specs=[tok, seq_blk, seq_blk],
        out_shape=[tok_shape, tok_shape, tok_shape],
        compiler_params=_params(3),
    )(qs, kn, kt, vb, tot, low, d_o)


def _qk_bwd(proj, dqs, dkn, dvh, qw2, kw2, nb, seq, tq):
    nl = seq // tq
    scale = 1.0 / math.sqrt(HEAD_DIM)

    def body(q_ref, k_ref, dq_ref, dk_ref, dv_ref, qw_ref, kw_ref,
             dqr_ref, dkr_ref, dvr_ref, gq_ref, gk_ref):
        @pl.when((pl.program_id(0) == 0) & (pl.program_id(1) == 0))
        def _():
            gq_ref[...] = jnp.zeros_like(gq_ref)
            gk_ref[...] = jnp.zeros_like(gk_ref)

        def norm_bwd(x, w, dy):
            r = lax.rsqrt(_pair_sum(x * x) * (1.0 / HEAD_DIM) + EPS)
            xhat = x * r
            g = dy * w
            m = _pair_sum(g * xhat) * (1.0 / HEAD_DIM)
            return r * (g - xhat * m), jnp.sum(dy * xhat, axis=0, keepdims=True)

        dvr_ref[...] = dv_ref[...].astype(BF16)
        gq = jnp.zeros((1, LANES), F32)
        gk = jnp.zeros((1, LANES), F32)
        for p in range(N_PAIRS):
            cols = slice(p * LANES, (p + 1) * LANES)
            dqr, gq_p = norm_bwd(q_ref[:, cols], qw_ref[...], dq_ref[:, cols] * scale)
            dkr, gk_p = norm_bwd(k_ref[:, cols], kw_ref[...], dk_ref[:, cols])
            dqr_ref[:, cols] = dqr.astype(BF16)
            dkr_ref[:, cols] = dkr.astype(BF16)
            gq, gk = gq + gq_p, gk + gk_p
        gq_ref[...] += gq
        gk_ref[...] += gk

    tok = lambda blk: pl.BlockSpec((tq, D_BRANCH), lambda b, i: (b * nl + i, blk))
    vec = pl.BlockSpec((1, LANES), lambda b, i: (0, 0))
    tshape = jax.ShapeDtypeStruct((nb * seq, D_BRANCH), BF16)
    return pl.pallas_call(
        body, name="qk_bwd",
        grid=(nb, nl),
        in_specs=[tok(0), tok(1), tok(0), tok(0), tok(0), vec, vec],
        out_specs=[tok(0), tok(0), tok(0), vec, vec],
        out_shape=[tshape, tshape, tshape,
                   jax.ShapeDtypeStruct((1, LANES), F32), jax.ShapeDtypeStruct((1, LANES), F32)],
        compiler_params=_params(2),
    )(proj, proj, dqs, dkn, dvh, qw2, kw2)


def _shift_down(cur, prev, k):
    if k == 0:
        return cur
    rows = _row_iota(cur.shape)
    return jnp.where(rows < k, pltpu.roll(prev, k, axis=0), pltpu.roll(cur, k, axis=0))


def _shift_up(cur, nxt, k):
    if k == 0:
        return cur
    n = cur.shape[0]
    rows = _row_iota(cur.shape)
    return jnp.where(rows < n - k, pltpu.roll(cur, n - k, axis=0), pltpu.roll(nxt, n - k, axis=0))


def _conv_taps(cur, prev):
    return [_shift_down(cur, prev, CONV_TAPS - 1 - i) for i in range(CONV_TAPS)]


def _conv_pre(taps, w, b):
    out = b
    for i in range(CONV_TAPS):
        out = out + taps[i] * w[i:i + 1, :]
    return out


def _silu(x):
    return x * _sigmoid(x)


def _silu_and_grad(x):
    s = _sigmoid(x)
    return x * s, s * (1.0 + x * (1.0 - s))


def _dot01(x, m01, parts, dims=None, m_left=False):
    total, rest = None, x
    for i in range(parts):
        piece = rest.astype(BF16)
        if i + 1 < parts:
            rest = rest - piece.astype(F32)
        term = _dot(m01, piece, dims) if m_left else _dot(piece, m01, dims)
        total = term if total is None else total + term
    return total


def _chunk_decay(dt_raw, dtb, alog, expand, qc):
    dt = _softplus(dt_raw + dtb)
    d_a = dt * (-jnp.exp(alog))
    r_i = lax.broadcasted_iota(jnp.int32, (qc, qc), 0)
    c_i = lax.broadcasted_iota(jnp.int32, (qc, qc), 1)
    tril = r_i >= c_i
    a_cs = _dot01(d_a, jnp.where(tril, 1.0, 0.0).astype(BF16), 3, m_left=True)
    dt_x = _dot01(dt, expand, 3)
    acs_x = _dot01(a_cs, expand, 3)
    return dt, d_a, a_cs, dt_x, acs_x, tril


def _ssd_fwd(proj, conv_w, conv_b, dtb, alog, dskip, nb, seq):
    qc = SSD_CHUNK
    nc = seq // qc

    def body(xs_ref, bc_ref, dt_ref, cw_ref, cb_ref, dtb_ref, al_ref, ds_ref,
             y_ref, st_ref, pxs_ref, pbc_ref, state_ref):
        @pl.when(pl.program_id(1) == 0)
        def _():
            pxs_ref[...] = jnp.zeros_like(pxs_ref)
            pbc_ref[...] = jnp.zeros_like(pbc_ref)
            state_ref[...] = jnp.zeros_like(state_ref)

        expand = _head_expand()
        xs_raw = xs_ref[...]
        bc_raw = bc_ref[...]
        cw = cw_ref[...]
        cb = cb_ref[...]
        xs = _silu(_conv_pre(_conv_taps(xs_raw, pxs_ref[...]), cw[:, :D_BRANCH], cb[:, :D_BRANCH]))
        bc = _silu(_conv_pre(_conv_taps(bc_raw, pbc_ref[...]), cw[:, D_BRANCH:], cb[:, D_BRANCH:]))
        pxs_ref[...] = xs_raw
        pbc_ref[...] = bc_raw

        dt, d_a, a_cs, dt_x, acs_x, tril = _chunk_decay(
            dt_ref[...], dtb_ref[...], al_ref[...], expand, qc)
        a_cst = a_cs.T
        aend_x = acs_x[qc - 1:qc, :]
        ea_x = jnp.exp(acs_x)
        dec_x = jnp.exp(aend_x - acs_x)
        xt = xs * dt_x
        xtb = xt.astype(BF16)
        xdb = (xt * dec_x).astype(BF16)
        d_x = _dot01(jnp.broadcast_to(ds_ref[...], (8, LANES)), expand, 3)[0:1, :]
        st_ref[0, 0] = state_ref[...]

        for g in range(N_GROUPS):
            gs = slice(g * GROUP_W, (g + 1) * GROUP_W)
            bg = bc[:, g * D_STATE:(g + 1) * D_STATE]
            cg = bc[:, (N_GROUPS + g) * D_STATE:(N_GROUPS + g + 1) * D_STATE]
            bgb = bg.astype(BF16)
            cgb = cg.astype(BF16)
            cbm = _dot(cgb, bgb, _NT)
            st_in = state_ref[g]
            y_off = _dot(cgb, st_in.astype(BF16)) * ea_x[:, gs]
            for k in range(HEADS_PER_GROUP):
                h = g * HEADS_PER_GROUP + k
                hs = slice(h * HEAD_DIM, (h + 1) * HEAD_DIM)
                seg = a_cs[:, h:h + 1] - a_cst[h:h + 1, :]
                gh = cbm * jnp.exp(jnp.where(tril, seg, -1e30))
                y_h = _dot(gh.astype(BF16), xtb[:, hs]) + y_off[:, k * HEAD_DIM:(k + 1) * HEAD_DIM]
                y_ref[:, hs] = y_h + d_x[:, hs] * xs[:, hs]
            state_ref[g] = st_in * jnp.exp(aend_x[:, gs]) + _dot(bg.T.astype(BF16), xdb[:, gs])

    nblk = lambda w, off: pl.BlockSpec((qc, w), lambda b, c: (b * nc + c, off))
    full = lambda r, w: pl.BlockSpec((r, w), lambda b, c: (0, 0))
    return pl.pallas_call(
        body, name="ssd_fwd",
        grid=(nb, nc),
        in_specs=[nblk(D_BRANCH, COL_XS // D_BRANCH), nblk(D_BC, COL_BC // D_BC),
                  nblk(LANES, COL_DT // LANES),
                  full(CONV_TAPS, D_CONV), full(1, D_CONV), full(1, LANES), full(1, LANES),
                  full(1, LANES)],
        out_specs=[pl.BlockSpec((qc, D_BRANCH), lambda b, c: (b * nc + c, 0)),
                   pl.BlockSpec((1, 1, N_GROUPS, D_STATE, GROUP_W), lambda b, c: (b, c, 0, 0, 0))],
        out_shape=[jax.ShapeDtypeStruct((nb * seq, D_BRANCH), F32),
                   jax.ShapeDtypeStruct((nb, nc, N_GROUPS, D_STATE, GROUP_W), F32)],
        scratch_shapes=[pltpu.VMEM((qc, D_BRANCH), F32), pltpu.VMEM((qc, D_BC), F32),
                        pltpu.VMEM((N_GROUPS, D_STATE, GROUP_W), F32)],
        compiler_params=_params(2),
    )(proj, proj, proj, conv_w, conv_b, dtb, alog, dskip)


def _ssd_bwd(proj, d_y, states, conv_w, conv_b, dtb, alog, dskip, nb, seq, slabs=()):
    qc = SSD_CHUNK
    nc = seq // qc

    def body(xs_ref, bc_ref, dt_ref, pxs_ref, pbc_ref, dy_ref, st_ref, stn_ref,
             cw_ref, cb_ref, dtb_ref, al_ref, ds_ref,
             dx_ref, gcw_ref, gcb_ref, gdtb_ref, gal_ref, gds_ref,
             dst_ref, nxs_ref, nbc_ref, yd_ref, dxt_ref):
        step = pl.program_id(1)
        chunk = nc - 1 - step

        @pl.when(step == 0)
        def _():
            dst_ref[...] = jnp.zeros_like(dst_ref)
            nxs_ref[...] = jnp.zeros_like(nxs_ref)
            nbc_ref[...] = jnp.zeros_like(nbc_ref)

        @pl.when((pl.program_id(0) == 0) & (step == 0))
        def _():
            gcw_ref[...] = jnp.zeros_like(gcw_ref)
            gcb_ref[...] = jnp.zeros_like(gcb_ref)
            gdtb_ref[...] = jnp.zeros_like(gdtb_ref)
            gal_ref[...] = jnp.zeros_like(gal_ref)
            gds_ref[...] = jnp.zeros_like(gds_ref)

        expand = _head_expand()
        collapse = lambda v: _dot01(v, expand, 2, _NT)
        first = jnp.where(chunk == 0, 0.0, 1.0)
        xs_raw = xs_ref[...]
        bc_raw = bc_ref[...]
        pxs = pxs_ref[...] * first
        pbc = pbc_ref[...] * first
        cw = cw_ref[...]
        cb = cb_ref[...]
        taps_xs = _conv_taps(xs_raw, pxs)
        taps_bc = _conv_taps(bc_raw, pbc)
        xs, dsilu_xs = _silu_and_grad(_conv_pre(taps_xs, cw[:, :D_BRANCH], cb[:, :D_BRANCH]))
        bc, dsilu_bc = _silu_and_grad(_conv_pre(taps_bc, cw[:, D_BRANCH:], cb[:, D_BRANCH:]))

        dt_in = dt_ref[...] + dtb_ref[...]
        dt, d_a, a_cs, dt_x, acs_x, tril = _chunk_decay(
            dt_ref[...], dtb_ref[...], al_ref[...], expand, qc)
        a_cst = a_cs.T
        aend_x = acs_x[qc - 1:qc, :]
        ea_x = jnp.exp(acs_x)
        dec_x = jnp.exp(aend_x - acs_x)
        xt = xs * dt_x
        xtb = xt.astype(BF16)
        xdb = (xt * dec_x).astype(BF16)
        d_x = _dot01(jnp.broadcast_to(ds_ref[...], (8, LANES)), expand, 3)[0:1, :]

        dy = dy_ref[...]
        dyb = dy.astype(BF16)
        dyeab = (dy * ea_x).astype(BF16)
        gds_ref[...] += collapse(jnp.broadcast_to(jnp.sum(dy * xs, axis=0, keepdims=True),
                                                  (8, D_BRANCH)))[0:1, :]

        d_bc = []
        d_cc = []
        y_offs = []
        dxt_states = []
        end_terms = []
        for g in range(N_GROUPS):
            gs = slice(g * GROUP_W, (g + 1) * GROUP_W)
            bg = bc[:, g * D_STATE:(g + 1) * D_STATE]
            cg = bc[:, (N_GROUPS + g) * D_STATE:(N_GROUPS + g + 1) * D_STATE]
            bgb = bg.astype(BF16)
            cgb = cg.astype(BF16)
            cbm = _dot(cgb, bgb, _NT)
            st_in = st_ref[0, 0, g]
            st_inb = st_in.astype(BF16)
            d_st = dst_ref[g]
            d_stb = d_st.astype(BF16)
            y_offs.append(_dot(cgb, st_inb) * ea_x[:, gs])
            dxt_states.append(_dot(bgb, d_stb) * dec_x[:, gs])
            d_c = _dot(dyeab[:, gs], st_inb, _NT)
            d_b = _dot(xdb[:, gs], d_stb, _NT)
            d_cb = jnp.zeros((qc, qc), F32)
            for k in range(HEADS_PER_GROUP):
                h = g * HEADS_PER_GROUP + k
                hs = slice(h * HEAD_DIM, (h + 1) * HEAD_DIM)
                seg = a_cs[:, h:h + 1] - a_cst[h:h + 1, :]
                lh = jnp.exp(jnp.where(tril, seg, -1e30))
                ghb = (cbm * lh).astype(BF16)
                d_cb = d_cb + _dot(dyb[:, hs], xtb[:, hs], _NT) * lh
                yd_ref[:, hs] = _dot(ghb, xtb[:, hs])
                dxt_ref[:, hs] = _dot(ghb, dyb[:, hs], _TN)
            d_cbb = d_cb.astype(BF16)
            d_cc.append(d_c + _dot(d_cbb, bgb))
            d_bc.append(d_b + _dot(d_cbb, cgb, _TN))
            end_terms.append(jnp.sum(d_st * stn_ref[0, 0, g], axis=0, keepdims=True))
            dst_ref[g] = d_st * jnp.exp(aend_x[:, gs]) + _dot(cg.T.astype(BF16), dyeab[:, gs])

        y_off = jnp.concatenate(y_offs, axis=1)
        dxt_state = jnp.concatenate(dxt_states, axis=1)
        dxt = dxt_ref[...] + dxt_state
        last = jnp.where(chunk == nc - 1, 0.0, 1.0)
        end_c = collapse(jnp.broadcast_to(jnp.concatenate(end_terms, axis=1), (8, D_BRANCH)))[0:1, :]
        da_cs = collapse(dyb.astype(F32) * yd_ref[...] - dxt_ref[...] * xtb.astype(F32)
                         + dy * y_off - dxt_state * xt)
        da_cs = da_cs + jnp.where(_row_iota(da_cs.shape) == qc - 1, end_c * last, 0.0)
        triu = lax.broadcasted_iota(jnp.int32, (qc, qc), 0) <= lax.broadcasted_iota(jnp.int32, (qc, qc), 1)
        dd_a = _dot01(da_cs, jnp.where(triu, 1.0, 0.0).astype(BF16), 3, m_left=True)
        ddt = dd_a * (-jnp.exp(al_ref[...])) + collapse(dxt * xs)
        head_lanes = _lane_iota(ddt.shape) < N_HEADS
        ddt_raw = jnp.where(head_lanes, ddt * _sigmoid(dt_in), 0.0)
        gal_ref[...] += jnp.sum(jnp.where(head_lanes, dd_a * d_a, 0.0), axis=0, keepdims=True)
        gdtb_ref[...] += jnp.sum(ddt_raw, axis=0, keepdims=True)

        dpre_xs = (dxt * dt_x + d_x * dy) * dsilu_xs
        dpre_bc = jnp.concatenate(d_bc + d_cc, axis=1) * dsilu_bc
        gcb_ref[...] += jnp.concatenate([jnp.sum(dpre_xs, axis=0, keepdims=True),
                                         jnp.sum(dpre_bc, axis=0, keepdims=True)], axis=1)
        nxs = nxs_ref[...]
        nbc = nbc_ref[...]
        du_xs = jnp.zeros_like(dpre_xs)
        du_bc = jnp.zeros_like(dpre_bc)
        for i in range(CONV_TAPS):
            k = CONV_TAPS - 1 - i
            gcw_ref[i:i + 1, :] += jnp.concatenate(
                [jnp.sum(dpre_xs * taps_xs[i], axis=0, keepdims=True),
                 jnp.sum(dpre_bc * taps_bc[i], axis=0, keepdims=True)], axis=1)
            du_xs = du_xs + _shift_up(dpre_xs, nxs, k) * cw[i:i + 1, :D_BRANCH]
            du_bc = du_bc + _shift_up(dpre_bc, nbc, k) * cw[i:i + 1, D_BRANCH:]
        nxs_ref[...] = dpre_xs
        nbc_ref[...] = dpre_bc

        dx_ref[:, :D_BRANCH] = du_xs.astype(BF16)
        dx_ref[:, D_BRANCH:D_CONV] = du_bc.astype(BF16)
        dx_ref[:, D_CONV:D_CONV + LANES] = ddt_raw.astype(BF16)
        dx_ref[:, D_CONV + LANES:] = jnp.zeros((qc, 2048 - D_CONV - LANES), BF16)

    rev = lambda b, c: b * nc + (nc - 1 - c)
    prv = lambda b, c: b * nc + jnp.maximum(nc - 2 - c, 0)
    nblk = lambda w, off, f: pl.BlockSpec((qc, w), lambda b, c: (f(b, c), off))
    full = lambda r, w: pl.BlockSpec((r, w), lambda b, c: (0, 0))
    st_spec = lambda f: pl.BlockSpec((1, 1, N_GROUPS, D_STATE, GROUP_W),
                                     lambda b, c: (b, f(c), 0, 0, 0))
    return _call_with_exchange(
        body, (proj, proj, proj, proj, proj, d_y, states, states, conv_w, conv_b, dtb, alog, dskip),
        slabs, (True,) * len(slabs), name="ssd_bwd", grid=(nb, nc),
        in_specs=[nblk(D_BRANCH, COL_XS // D_BRANCH, rev), nblk(D_BC, COL_BC // D_BC, rev),
                  nblk(LANES, COL_DT // LANES, rev),
                  nblk(D_BRANCH, COL_XS // D_BRANCH, prv), nblk(D_BC, COL_BC // D_BC, prv),
                  nblk(D_BRANCH, 0, rev),
                  st_spec(lambda c: nc - 1 - c), st_spec(lambda c: jnp.minimum(nc - c, nc - 1)),
                  full(CONV_TAPS, D_CONV), full(1, D_CONV), full(1, LANES), full(1, LANES),
                  full(1, LANES)],
        out_specs=[nblk(2048, 0, rev), full(8, D_CONV), full(1, D_CONV), full(1, LANES),
                   full(1, LANES), full(1, LANES)],
        out_shape=[jax.ShapeDtypeStruct((nb * seq, 2048), BF16),
                   jax.ShapeDtypeStruct((8, D_CONV), F32), jax.ShapeDtypeStruct((1, D_CONV), F32),
                   jax.ShapeDtypeStruct((1, LANES), F32), jax.ShapeDtypeStruct((1, LANES), F32),
                   jax.ShapeDtypeStruct((1, LANES), F32)],
        scratch_shapes=[pltpu.VMEM((N_GROUPS, D_STATE, GROUP_W), F32),
                        pltpu.VMEM((qc, D_BRANCH), F32), pltpu.VMEM((qc, D_BC), F32),
                        pltpu.VMEM((qc, D_BRANCH), F32), pltpu.VMEM((qc, D_BRANCH), F32)])


def _mid(o_sb, y_ssd, proj, x2, target, sb_w, ssd_w, w_out_b, tm):
    t = x2.shape[0]
    inv_d = 1.0 / D_MODEL

    def body(o_ref, y_ref, zsb_ref, zssd_ref, x_ref, tg_ref, sbw_ref, ssdw_ref, w_ref,
             dout_ref, dosb_ref, dy_ref, dz_ref, gw_ref, gsb_ref, gssd_ref, loss_ref):
        @pl.when(pl.program_id(0) == 0)
        def _():
            gw_ref[...] = jnp.zeros_like(gw_ref)
            gsb_ref[...] = jnp.zeros_like(gsb_ref)
            gssd_ref[...] = jnp.zeros_like(gssd_ref)
            loss_ref[...] = jnp.zeros_like(loss_ref)

        def branch(val, z, w):
            gate, dgate = _silu_and_grad(z)
            g = val * gate
            r = lax.rsqrt(jnp.mean(g * g, axis=1, keepdims=True) + EPS)
            xhat = g * r
            return (gate, dgate, r, xhat), (xhat * w).astype(BF16)

        o = o_ref[...]
        y = y_ref[...]
        saved_a, mix_a = branch(o, zsb_ref[...], sbw_ref[...])
        saved_b, mix_b = branch(y, zssd_ref[...], ssdw_ref[...])
        out = x_ref[...] + _dot(mix_a, w_ref[:D_BRANCH, :]) + _dot(mix_b, w_ref[D_BRANCH:, :])
        diff = out - tg_ref[...]
        loss_ref[...] += 0.5 * inv_d * jnp.sum(diff * diff)
        d_out = diff * inv_d
        dout_ref[...] = d_out
        d_outb = d_out.astype(BF16)
        gw_ref[:D_BRANCH, :] += _dot(mix_a, d_outb, _TN)
        gw_ref[D_BRANCH:, :] += _dot(mix_b, d_outb, _TN)

        def branch_bwd(dmix, val, w, saved):
            gate, dgate, r, xhat = saved
            gg = dmix * w
            m = jnp.mean(gg * xhat, axis=1, keepdims=True)
            dg = r * (gg - xhat * m)
            return dg * gate, dg * val * dgate, jnp.sum(dmix * xhat, axis=0, keepdims=True)

        dmix_a = _dot(d_outb, w_ref[:D_BRANCH, :], _NT)
        dmix_b = _dot(d_outb, w_ref[D_BRANCH:, :], _NT)
        d_o, dz_a, gsb = branch_bwd(dmix_a, o, sbw_ref[...], saved_a)
        d_y, dz_b, gssd = branch_bwd(dmix_b, y, ssdw_ref[...], saved_b)
        dosb_ref[...] = d_o
        dy_ref[...] = d_y
        dz_ref[:, :D_BRANCH] = dz_a.astype(BF16)
        dz_ref[:, D_BRANCH:] = dz_b.astype(BF16)
        gsb_ref[...] += gsb
        gssd_ref[...] += gssd

    row = lambda w, off: pl.BlockSpec((tm, w), lambda i: (i, off))
    full = lambda r, w: pl.BlockSpec((r, w), lambda i: (0, 0))
    resident = pl.BlockSpec((2 * D_BRANCH, D_MODEL), lambda i: (0, 0), pipeline_mode=pl.Buffered(1))
    tok = jax.ShapeDtypeStruct((t, D_MODEL), F32)
    return pl.pallas_call(
        body, name="mid",
        grid=(t // tm,),
        in_specs=[row(D_BRANCH, 0), row(D_BRANCH, 0), row(D_BRANCH, 3), row(D_BRANCH, 4),
                  row(D_MODEL, 0), row(D_MODEL, 0), full(1, D_BRANCH), full(1, D_BRANCH),
                  resident],
        out_specs=[row(D_MODEL, 0), row(D_BRANCH, 0), row(D_BRANCH, 0), row(2 * D_BRANCH, 0),
                   resident, full(1, D_BRANCH), full(1, D_BRANCH),
                   full(1, LANES)],
        out_shape=[tok, tok, tok, jax.ShapeDtypeStruct((t, 2 * D_BRANCH), BF16),
                   jax.ShapeDtypeStruct((2 * D_BRANCH, D_MODEL), F32),
                   jax.ShapeDtypeStruct((1, D_BRANCH), F32), jax.ShapeDtypeStruct((1, D_BRANCH), F32),
                   jax.ShapeDtypeStruct((1, LANES), F32)],
        compiler_params=_params(1),
    )(o_sb, y_ssd, proj, proj, x2, target, sb_w, ssd_w, w_out_b)


_DPROJ_FIRST = (0, 1, 2, 3, 5)
_DPROJ_BLOCKS = (1, 1, 1, 2, 2)


def _in_proj_bwd_x(d_parts, w_in_t, x2, d_out, norm_w, tm, slabs=()):
    t = x2.shape[0]
    n_parts = len(d_parts)

    def body(*refs):
        dp_refs = refs[:n_parts]
        w_ref, x_ref, dout_ref, nw_ref, gx_ref, gnw_ref = refs[n_parts:]

        @pl.when(pl.program_id(0) == 0)
        def _():
            gnw_ref[...] = jnp.zeros_like(gnw_ref)

        d_hn = None
        for p in range(n_parts):
            rows = slice(_DPROJ_FIRST[p] * COLBLK, (_DPROJ_FIRST[p] + _DPROJ_BLOCKS[p]) * COLBLK)
            term = _dot(dp_refs[p][...], w_ref[rows, :])
            d_hn = term if d_hn is None else d_hn + term
        xf = x_ref[...]
        r = lax.rsqrt(jnp.mean(xf * xf, axis=1, keepdims=True) + EPS)
        xhat = xf * r
        g = d_hn * nw_ref[...]
        m = jnp.mean(g * xhat, axis=1, keepdims=True)
        gx_ref[...] = dout_ref[...] + r * (g - xhat * m)
        gnw_ref[...] += jnp.sum(d_hn * xhat, axis=0, keepdims=True)

    row = lambda w: pl.BlockSpec((tm, w), lambda i: (i, 0))
    return _call_with_exchange(
        body, (*d_parts, w_in_t, x2, d_out, norm_w), slabs, (True,) * len(slabs),
        name="in_proj_bwd_x", grid=(t // tm,),
        in_specs=[row(COLBLK * _DPROJ_BLOCKS[p]) for p in range(n_parts)] + [
                  pl.BlockSpec((D_IN_PAD, D_MODEL), lambda i: (0, 0), pipeline_mode=pl.Buffered(1)),
                  row(D_MODEL), row(D_MODEL), pl.BlockSpec((1, D_MODEL), lambda i: (0, 0))],
        out_specs=[row(D_MODEL), pl.BlockSpec((1, D_MODEL), lambda i: (0, 0))],
        out_shape=[jax.ShapeDtypeStruct((t, D_MODEL), F32), jax.ShapeDtypeStruct((1, D_MODEL), F32)])


def _in_proj_bwd_w(hn, d_parts, tm):
    t = hn.shape[0]
    n_parts = len(d_parts)

    def body(hn_ref, *refs):
        dp_refs, gw_ref = refs[:n_parts], refs[n_parts]

        @pl.when(pl.program_id(0) == 0)
        def _():
            gw_ref[...] = jnp.zeros_like(gw_ref)

        hnt = hn_ref[...].astype(F32).T.astype(BF16)
        for p in range(n_parts):
            cols = slice(_DPROJ_FIRST[p] * COLBLK, (_DPROJ_FIRST[p] + _DPROJ_BLOCKS[p]) * COLBLK)
            gw_ref[:, cols] += _dot(hnt, dp_refs[p][...])

    return pl.pallas_call(
        body, name="in_proj_bwd_w",
        grid=(t // tm,),
        in_specs=[pl.BlockSpec((tm, D_MODEL), lambda i: (i, 0))]
                 + [pl.BlockSpec((tm, COLBLK * _DPROJ_BLOCKS[p]), lambda i: (i, 0))
                    for p in range(n_parts)],
        out_specs=pl.BlockSpec((D_MODEL, D_IN_PAD), lambda i: (0, 0), pipeline_mode=pl.Buffered(1)),
        out_shape=jax.ShapeDtypeStruct((D_MODEL, D_IN_PAD), F32),
        compiler_params=_params(1),
    )(hn, *d_parts)


def _adamw(parts, w, m, v, tr, name):
    _, rows, cols = w.shape
    c1 = 1.0 - ADAM_B1 ** ADAM_STEP
    c2 = 1.0 - ADAM_B2 ** ADAM_STEP

    def body(p_ref, w_ref, m_ref, v_ref, g_ref, d_ref, nm_ref, nv_ref):
        g = p_ref[0].astype(F32)
        for s in range(1, N_DEV):
            g = g + p_ref[s].astype(F32)
        nm = ADAM_B1 * m_ref[0] + (1.0 - ADAM_B1) * g
        nv = ADAM_B2 * v_ref[0] + (1.0 - ADAM_B2) * (g * g)
        g_ref[0] = g
        nm_ref[0] = nm
        nv_ref[0] = nv
        d_ref[0] = -ADAM_LR * ((nm / c1) / (jnp.sqrt(nv / c2) + ADAM_EPS) + ADAM_WD * w_ref[0])

    blk = pl.BlockSpec((1, tr, cols), lambda i: (0, i, 0))
    shape = jax.ShapeDtypeStruct((1, rows, cols), F32)
    return pl.pallas_call(
        body, name=name,
        grid=(rows // tr,),
        in_specs=[pl.BlockSpec((N_DEV, tr, cols), lambda i: (0, i, 0)), blk, blk, blk],
        out_specs=[blk, blk, blk, blk],
        out_shape=[shape, shape, shape, shape],
        compiler_params=_params(1),
    )(parts, w, m, v)


def _mesh_place():
    x, y, c = lax.axis_index("x"), lax.axis_index("y"), lax.axis_index("c")
    return x, y, c, 4 * x + 2 * y + c


def _peer(x, y, c, k):
    px = 1 - x if k & 4 else x
    py = 1 - y if k & 2 else y
    pc = 1 - c if k & 1 else c
    return (px, py, pc), 4 * px + 2 * py + pc


def _exchange(srcs, scatter, name):
    n = len(srcs)

    def body(*refs):
        copies = _exchange_copies(refs[:n], refs[n:2 * n], scatter, *refs[2 * n:])
        _exchange_start(copies)
        _exchange_wait(copies)

    return pl.pallas_call(
        body, name=name,
        in_specs=[_ANY] * n, out_specs=[_ANY] * n, out_shape=_exchange_shapes(srcs, scatter),
        scratch_shapes=_exchange_sems(n),
    )(*srcs)


def _call_with_exchange(body, operands, srcs, scatter, *, name, grid, in_specs, out_specs,
                        out_shape, scratch_shapes=()):
    n_in, n_out, n_scr, n_x = len(in_specs), len(out_specs), len(scratch_shapes), len(srcs)
    params = _params(len(grid))
    if not n_x:
        return pl.pallas_call(body, name=name, grid=grid, in_specs=list(in_specs),
                              out_specs=list(out_specs), out_shape=list(out_shape),
                              scratch_shapes=list(scratch_shapes), compiler_params=params)(*operands)

    def wrapped(*refs):
        ins, refs = refs[:n_in], refs[n_in:]
        x_src, refs = refs[:n_x], refs[n_x:]
        outs, refs = refs[:n_out], refs[n_out:]
        x_dst, refs = refs[:n_x], refs[n_x:]
        scratch, sems = refs[:n_scr], refs[n_scr:]
        ids = [pl.program_id(a) for a in range(len(grid))]
        first = functools.reduce(jnp.logical_and, [i == 0 for i in ids])
        last = functools.reduce(jnp.logical_and, [i == n - 1 for i, n in zip(ids, grid)])

        @pl.when(first)
        def _():
            _exchange_start(_exchange_copies(x_src, x_dst, scatter, *sems))

        body(*ins, *outs, *scratch)

        @pl.when(last)
        def _():
            _exchange_wait(_exchange_copies(x_src, x_dst, scatter, *sems))

    return pl.pallas_call(
        wrapped, name=name, grid=grid,
        in_specs=list(in_specs) + [_ANY] * n_x, out_specs=list(out_specs) + [_ANY] * n_x,
        out_shape=list(out_shape) + _exchange_shapes(srcs, scatter),
        scratch_shapes=list(scratch_shapes) + _exchange_sems(n_x), compiler_params=params,
    )(*operands, *srcs)


def _gather_two_level(shard, name):
    def body(x_ref, out_ref, send_sems, recv_sems, local_sem):
        x, y, c, me = _mesh_place()
        sibling = (x, y, 1 - c)
        chips = [(1 - x, y), (x, 1 - y), (1 - x, 1 - y)]

        def slab(px, py, pc):
            return out_ref.at[4 * px + 2 * py + pc]

        def copy(k, block, to, src=None):
            return pltpu.make_async_remote_copy(
                src_ref=slab(*block) if src is None else src, dst_ref=slab(*block),
                send_sem=send_sems.at[k], recv_sem=recv_sems.at[k],
                device_id=to, device_id_type=pl.DeviceIdType.MESH)

        mine = pltpu.make_async_copy(x_ref, slab(x, y, c), local_sem)
        mine.start()
        first = [copy(0, (x, y, c), sibling, src=x_ref)]
        first += [copy(1 + j, (x, y, c), (*chip, c), src=x_ref) for j, chip in enumerate(chips)]
        for cp in first:
            cp.start()
        passed = [copy(4 + j, (*chip, c), sibling) for j, chip in enumerate(chips)]
        for j, chip in enumerate(chips):
            copy(1 + j, (*chip, c), (x, y, c)).wait_recv()
            passed[j].start()
        copy(0, sibling, (x, y, c)).wait_recv()
        for j, chip in enumerate(chips):
            copy(4 + j, (*chip, 1 - c), (x, y, c)).wait_recv()
        for cp in first + passed:
            cp.wait_send()
        mine.wait()

    return pl.pallas_call(
        body, name=name,
        in_specs=[_ANY], out_specs=_ANY,
        out_shape=jax.ShapeDtypeStruct((N_DEV,) + shard.shape, shard.dtype),
        scratch_shapes=[pltpu.SemaphoreType.DMA((N_DEV - 1,)), pltpu.SemaphoreType.DMA((N_DEV - 1,)),
                        pltpu.SemaphoreType.DMA],
    )(shard)


_ANY = pl.BlockSpec(memory_space=pl.ANY)


def _exchange_shapes(srcs, scatter):
    return [jax.ShapeDtypeStruct(s.shape if sc else (N_DEV,) + s.shape, s.dtype)
            for s, sc in zip(srcs, scatter)]


def _exchange_sems(n):
    return [pltpu.SemaphoreType.DMA((n * (N_DEV - 1),)),
            pltpu.SemaphoreType.DMA((n * (N_DEV - 1),)),
            pltpu.SemaphoreType.DMA((n,))]


def _exchange_copies(src_refs, dst_refs, scatter, send_sems, recv_sems, loc_sems):
    n = len(src_refs)
    x, y, c, me = _mesh_place()

    def src_of(i, idx):
        return src_refs[i].at[idx] if scatter[i] else src_refs[i]

    local = [pltpu.make_async_copy(src_of(i, me), dst_refs[i].at[me], loc_sems.at[i])
             for i in range(n)]
    sends, recvs = [], []
    for k in range(1, N_DEV):
        peer, pidx = _peer(x, y, c, k)
        for i in range(n):
            s = i * (N_DEV - 1) + k - 1
            for dst_slab, group in ((me, sends), (pidx, recvs)):
                group.append(pltpu.make_async_remote_copy(
                    src_ref=src_of(i, pidx), dst_ref=dst_refs[i].at[dst_slab],
                    send_sem=send_sems.at[s], recv_sem=recv_sems.at[s],
                    device_id=peer, device_id_type=pl.DeviceIdType.MESH))
    return local, sends, recvs


def _exchange_start(copies):
    local, sends, _ = copies
    for cp in local + sends:
        cp.start()


def _exchange_wait(copies):
    local, sends, recvs = copies
    for cp in recvs:
        cp.wait_recv()
    for cp in sends:
        cp.wait_send()
    for cp in local:
        cp.wait()


def _pad_lanes(v, width=LANES):
    return jnp.pad(v, ((0, 0), (0, width - v.shape[1])))


def _local_step(x, target, norm_w, w_in_b, q_norm_w, k_norm_w, conv_w, conv_b, dt_bias, a_log,
                d_skip, sb_norm_w, ssd_norm_w, w_out_b, tm=256, tq=512, tmid=256, blk=ATT_BLK,
                scatter=False, w_in_t=None):
    nb, seq, _ = x.shape
    t = nb * seq
    x2 = x.reshape(t, D_MODEL)
    tg2 = target.reshape(t, D_MODEL)
    qw2 = jnp.tile(q_norm_w, (1, 2))
    kw2 = jnp.tile(k_norm_w, (1, 2))
    dtb, alog, dsk = _pad_lanes(dt_bias), _pad_lanes(a_log), _pad_lanes(d_skip)

    if w_in_t is None:
        w_in_t = w_in_b.T
    if scatter:
        proj, hn, wout_all, cw_all = _in_proj(x2, norm_w, w_in_b, tm, (w_out_b, conv_w))
        w_out_b = wout_all.reshape(2 * D_BRANCH, D_MODEL)
        conv_w = jnp.transpose(cw_all, (1, 0, 2)).reshape(CONV_TAPS, D_CONV)
    else:
        proj, hn = _in_proj(x2, norm_w, w_in_b, tm)
    qs, kn, vb, kt, ksq = _qk_prep(proj, qw2, kw2, nb, seq, tq)
    o_sb, sb_tot, sb_low = _attn_fwd(qs, kn, vb, jnp.max(ksq, axis=1), nb, seq, blk)
    y_ssd, states = _ssd_fwd(proj, conv_w, conv_b, dtb, alog, dsk, nb, seq)
    d_out, d_osb, d_y, d_z, g_wout, g_sbw, g_ssdw, loss = _mid(
        o_sb, y_ssd, proj, x2, tg2, sb_norm_w, ssd_norm_w, w_out_b, tmid)
    dqs, dkn, dvh = _attn_bwd(qs, kn, kt, vb, sb_tot, sb_low, d_osb, nb, seq, blk)
    dq_raw, dk_raw, dv_raw, g_qw, g_kw = _qk_bwd(proj, dqs, dkn, dvh, qw2, kw2, nb, seq, tq)
    wout_slabs = (g_wout.reshape(N_DEV, 2 * D_BRANCH // N_DEV, D_MODEL).astype(BF16),)
    d_xbc, g_cw, g_cb, g_dtb, g_alog, g_dsk, *moved = _ssd_bwd(
        proj, d_y, states, conv_w, conv_b, dtb, alog, dsk, nb, seq, wout_slabs if scatter else ())
    d_parts = [dq_raw, dk_raw, dv_raw, d_z, d_xbc]
    tall = min(2 * tm, t)
    g_win = _in_proj_bwd_w(hn, d_parts, tall)[:, :D_IN]
    g_cw = g_cw[:CONV_TAPS]
    if scatter:
        g_wout, = moved
        grad_x, g_nw, g_win, g_cw = _in_proj_bwd_x(
            d_parts, w_in_t, x2, d_out, norm_w, tall, _grad_slabs(g_win, g_cw))
    else:
        grad_x, g_nw = _in_proj_bwd_x(d_parts, w_in_t, x2, d_out, norm_w, tall)

    small = dict(
        norm_w=g_nw,
        q_norm_w=g_qw[:, :HEAD_DIM] + g_qw[:, HEAD_DIM:],
        k_norm_w=g_kw[:, :HEAD_DIM] + g_kw[:, HEAD_DIM:],
        conv_b=g_cb, dt_bias=g_dtb[:, :N_HEADS], A_log=g_alog[:, :N_HEADS],
        D_skip=g_dsk[:, :N_HEADS], sb_norm_w=g_sbw, ssd_norm_w=g_ssdw)
    return loss[0, 0], grad_x.reshape(nb, seq, D_MODEL), g_win, g_wout, g_cw, small


def _grad_slabs(g_win, g_cw):
    w_sh = D_IN // N_DEV
    c_sh = D_CONV // N_DEV
    return (jnp.transpose(g_win.reshape(D_MODEL, N_DEV, w_sh), (1, 0, 2)).astype(BF16),
            jnp.pad(jnp.transpose(g_cw.reshape(CONV_TAPS, N_DEV, c_sh), (1, 0, 2)),
                    ((0, 0), (0, 8 - CONV_TAPS), (0, 0))))


_SMALL = ("norm_w", "q_norm_w", "k_norm_w", "conv_b", "dt_bias", "A_log", "D_skip",
          "sb_norm_w", "ssd_norm_w")


def _pack_small(vals):
    flat = jnp.concatenate([_pad_lanes(vals[n], -(-vals[n].shape[1] // LANES) * LANES)
                            for n in _SMALL], axis=1)
    return jnp.pad(flat, ((0, 0), (0, 48 * LANES - flat.shape[1]))).reshape(48, LANES)


def _unpack_small(packed, like):
    out, r = {}, 0
    for n in _SMALL:
        width = like[n].shape[1]
        nr = -(-width // LANES)
        out[n] = packed[r:r + nr].reshape(1, nr * LANES)[:, :width]
        r += nr
    return out


def kernel(x, norm_w, w_in, q_norm_w, k_norm_w, conv_w, conv_b, dt_bias, A_log, D_skip, sb_norm_w, ssd_norm_w, w_out, loss_target, m_norm_w, m_w_in, m_q_norm_w, m_k_norm_w, m_conv_w, m_conv_b, m_dt_bias, m_A_log, m_D_skip, m_sb_norm_w, m_ssd_norm_w, m_w_out, v_norm_w, v_w_in, v_q_norm_w, v_k_norm_w, v_conv_w, v_conv_b, v_dt_bias, v_A_log, v_D_skip, v_sb_norm_w, v_ssd_norm_w, v_w_out):
    win_all = _gather_two_level(w_in[0].astype(BF16), "gather_w_in")
    w_in_b = jnp.pad(jnp.transpose(win_all, (1, 0, 2)).reshape(D_MODEL, D_IN),
                     ((0, 0), (0, D_IN_PAD - D_IN)))
    w_in_t = jnp.pad(jnp.transpose(win_all, (0, 2, 1)).reshape(D_IN, D_MODEL),
                     ((0, D_IN_PAD - D_IN), (0, 0)))

    loss, grad_x, win_parts, wout_parts, cw_parts, g_small = _local_step(
        x, loss_target, norm_w, w_in_b, q_norm_w, k_norm_w, conv_w[0], conv_b, dt_bias, A_log,
        D_skip, sb_norm_w, ssd_norm_w, w_out[0].astype(BF16), scatter=True, w_in_t=w_in_t)
    packed = _pack_small(g_small).at[-1, 0].set(loss)
    small_parts, = _exchange([packed], [False], "gather_small_grads")
    loss = jnp.sum(small_parts[:, -1, 0])

    small_w = dict(norm_w=norm_w, q_norm_w=q_norm_w, k_norm_w=k_norm_w, conv_b=conv_b,
                   dt_bias=dt_bias, A_log=A_log, D_skip=D_skip, sb_norm_w=sb_norm_w,
                   ssd_norm_w=ssd_norm_w)
    small_m = dict(norm_w=m_norm_w, q_norm_w=m_q_norm_w, k_norm_w=m_k_norm_w, conv_b=m_conv_b,
                   dt_bias=m_dt_bias, A_log=m_A_log, D_skip=m_D_skip, sb_norm_w=m_sb_norm_w,
                   ssd_norm_w=m_ssd_norm_w)
    small_v = dict(norm_w=v_norm_w, q_norm_w=v_q_norm_w, k_norm_w=v_k_norm_w, conv_b=v_conv_b,
                   dt_bias=v_dt_bias, A_log=v_A_log, D_skip=v_D_skip, sb_norm_w=v_sb_norm_w,
                   ssd_norm_w=v_ssd_norm_w)

    pad8 = lambda a: jnp.pad(a, ((0, 0), (0, 8 - CONV_TAPS), (0, 0)))
    r_win = _adamw(win_parts, w_in, m_w_in, v_w_in, 128, "adamw_w_in")
    r_wout = _adamw(wout_parts, w_out, m_w_out, v_w_out, 128, "adamw_w_out")
    r_cw = _adamw(cw_parts, pad8(conv_w), pad8(m_conv_w), pad8(v_conv_w), 8, "adamw_conv_w")
    r_small = _adamw(small_parts, _pack_small(small_w)[None], _pack_small(small_m)[None],
                     _pack_small(small_v)[None], 48, "adamw_small")

    res = {"w_in": r_win, "w_out": r_wout, "conv_w": [a[:, :CONV_TAPS] for a in r_cw]}
    unpacked = [_unpack_small(a[0], small_w) for a in r_small]
    for n in _SMALL:
        res[n] = [u[n] for u in unpacked]
    order = ("norm_w", "w_in", "q_norm_w", "k_norm_w", "conv_w", "conv_b", "dt_bias", "A_log",
             "D_skip", "sb_norm_w", "ssd_norm_w", "w_out")
    outs = [loss, grad_x]
    for kind in range(4):
        outs += [res[n][kind] for n in order]
    return tuple(outs)
```

```python
import functools
import math

import jax
import jax.numpy as jnp
from jax import lax
from jax.experimental import pallas as pl
from jax.experimental.pallas import tpu as pltpu

F32 = jnp.float32
BF16 = jnp.bfloat16

D_MODEL = 1024
N_HEADS = 16
HEAD_DIM = 64
N_PAIRS = N_HEADS // 2
D_BRANCH = 1024
N_GROUPS = 2
HEADS_PER_GROUP = 8
D_STATE = 128
GROUP_W = HEADS_PER_GROUP * HEAD_DIM
D_BC = 2 * N_GROUPS * D_STATE
D_CONV = D_BRANCH + D_BC
D_IN = 6672
COLBLK = 1024
D_IN_PAD = 7168
N_COLBLK = D_IN_PAD // COLBLK
COL_XS = 5120
COL_BC = 6144
COL_DT = 6656
EPS = 1e-6
CONV_TAPS = 4
N_DEV = 8

LANES = 128
SSD_CHUNK = 128
ATT_BLK = 256
ATT_HEADS = 4
ATT_W = ATT_HEADS * HEAD_DIM
N_ATT_GROUPS = N_HEADS // ATT_HEADS
EXP_UNDERFLOW = -105.0
VMEM_LIMIT = 56 * 1024 * 1024

ADAM_LR = 0.001
ADAM_B1 = 0.9
ADAM_B2 = 0.999
ADAM_EPS = 1e-08
ADAM_WD = 0.01
ADAM_STEP = 10

_NT = (((1,), (1,)), ((), ()))
_TN = (((0,), (0,)), ((), ()))


def _params(n_grid):
    return pltpu.CompilerParams(dimension_semantics=("arbitrary",) * n_grid,
                                vmem_limit_bytes=VMEM_LIMIT)


def _dot(a, b, dims=None):
    if dims is None:
        return jnp.dot(a, b, preferred_element_type=F32)
    return lax.dot_general(a, b, dims, preferred_element_type=F32)


def _sigmoid(x):
    return 1.0 / (1.0 + jnp.exp(-x))


def _softplus(x):
    return jnp.maximum(x, 0.0) + jnp.log(1.0 + jnp.exp(-jnp.abs(x)))


def _split_bf16(x):
    hi = x.astype(BF16)
    lo = (x - hi.astype(F32)).astype(BF16)
    return hi, lo


def _lane_iota(shape):
    return lax.broadcasted_iota(jnp.int32, shape, len(shape) - 1)


def _row_iota(shape):
    return lax.broadcasted_iota(jnp.int32, shape, len(shape) - 2)


def _pair_sum(x):
    r = lax.broadcasted_iota(jnp.int32, (LANES, LANES), 0)
    c = lax.broadcasted_iota(jnp.int32, (LANES, LANES), 1)
    same_head = jnp.where(r // HEAD_DIM == c // HEAD_DIM, 1.0, 0.0).astype(BF16)
    hi, lo = _split_bf16(x)
    return _dot(hi, same_head) + _dot(lo, same_head)


def _head_lanes(x, a):
    lane = _lane_iota(x.shape)
    mine = (lane >= a * HEAD_DIM) & (lane < (a + 1) * HEAD_DIM)
    return jnp.where(mine, x, jnp.zeros_like(x))


def _head_expand():
    r = lax.broadcasted_iota(jnp.int32, (LANES, D_BRANCH), 0)
    c = lax.broadcasted_iota(jnp.int32, (LANES, D_BRANCH), 1)
    return jnp.where(c // HEAD_DIM == r, 1.0, 0.0).astype(BF16)


def _in_proj(x2, norm_w, w_in_b, tm, shards=()):
    t = x2.shape[0]

    def body(x_ref, nw_ref, w_ref, proj_ref, hn_ref):
        xf = x_ref[...]
        r = lax.rsqrt(jnp.mean(xf * xf, axis=1, keepdims=True) + EPS)
        hn = (xf * r * nw_ref[...]).astype(BF16)
        hn_ref[...] = hn
        for j in range(N_COLBLK):
            cols = slice(j * COLBLK, (j + 1) * COLBLK)
            proj_ref[:, cols] = _dot(hn, w_ref[:, cols])

    return _call_with_exchange(
        body, (x2, norm_w, w_in_b), shards, (False,) * len(shards), name="in_proj",
        grid=(t // tm,),
        in_specs=[pl.BlockSpec((tm, D_MODEL), lambda i: (i, 0)),
                  pl.BlockSpec((1, D_MODEL), lambda i: (0, 0)),
                  pl.BlockSpec((D_MODEL, D_IN_PAD), lambda i: (0, 0), pipeline_mode=pl.Buffered(1))],
        out_specs=[pl.BlockSpec((tm, D_IN_PAD), lambda i: (i, 0)),
                   pl.BlockSpec((tm, D_MODEL), lambda i: (i, 0))],
        out_shape=[jax.ShapeDtypeStruct((t, D_IN_PAD), F32),
                   jax.ShapeDtypeStruct((t, D_MODEL), BF16)])


def _qk_prep(proj, qw2, kw2, nb, seq, tq):
    nl = seq // tq
    scale = 1.0 / math.sqrt(HEAD_DIM)

    def body(q_ref, k_ref, v_ref, qw_ref, kw_ref, qs_ref, kn_ref, vb_ref, kt_ref, ksq_ref):
        def norm(x, w):
            r = lax.rsqrt(_pair_sum(x * x) * (1.0 / HEAD_DIM) + EPS)
            return x * r * w

        vb_ref[...] = v_ref[...].astype(BF16)
        for p in range(N_PAIRS):
            cols = slice(p * LANES, (p + 1) * LANES)
            kn = norm(k_ref[:, cols], kw_ref[...])
            knb = kn.astype(BF16)
            qs_ref[:, cols] = (norm(q_ref[:, cols], qw_ref[...]) * scale).astype(BF16)
            kn_ref[:, cols] = knb
            kt_ref[0, p] = kn.T.astype(BF16)
            kf = knb.astype(F32)
            ksq_ref[0, 0, p:p + 1, :] = jnp.max(_pair_sum(kf * kf), axis=0, keepdims=True) * 1.0001

    tok_shape = jax.ShapeDtypeStruct((nb * seq, D_BRANCH), BF16)
    tok = lambda blk: pl.BlockSpec((tq, D_BRANCH), lambda b, i: (b * nl + i, blk))
    vec = pl.BlockSpec((1, LANES), lambda b, i: (0, 0))
    return pl.pallas_call(
        body, name="qk_prep",
        grid=(nb, nl),
        in_specs=[tok(0), tok(1), tok(2), vec, vec],
        out_specs=[tok(0), tok(0), tok(0),
                   pl.BlockSpec((1, N_PAIRS, LANES, tq), lambda b, i: (b, 0, 0, i)),
                   pl.BlockSpec((1, 1, N_PAIRS, LANES), lambda b, i: (b, i, 0, 0))],
        out_shape=[tok_shape, tok_shape, tok_shape,
                   jax.ShapeDtypeStruct((nb, N_PAIRS, LANES, seq), BF16),
                   jax.ShapeDtypeStruct((nb, nl, N_PAIRS, LANES), F32)],
        compiler_params=_params(2),
    )(proj, proj, proj, qw2, kw2)


def _attn_fwd(qs, kn, vb, ksq, nb, seq, blk):
    nq = seq // blk

    def body(q_ref, k_ref, v_ref, ksq_ref, o_ref, tot_ref, low_ref):
        qi = pl.program_id(2)
        r_i = lax.broadcasted_iota(jnp.int32, (blk, blk), 0)
        c_i = lax.broadcasted_iota(jnp.int32, (blk, blk), 1)
        csum = jnp.where(r_i >= c_i, 1.0, 0.0).astype(BF16)
        causal = c_i < r_i
        heads = range(ATT_HEADS)
        head = _head_lanes

        q_blk = q_ref[...]
        qf = q_blk.astype(F32)
        q_head = [head(q_blk, a) for a in heads]
        zmax = []
        for a in heads:
            qsq = jnp.sum(head(qf * qf, a), axis=1, keepdims=True)
            kmax = ksq_ref[0, 0, a // 2:a // 2 + 1, (a % 2) * HEAD_DIM:(a % 2) * HEAD_DIM + 1]
            zmax.append(1.01 * jnp.sqrt(qsq * kmax) + 0.01)

        def exhausted(run):
            top = functools.reduce(jnp.maximum, [jnp.max(run[a] + zmax[a]) for a in heads])
            return top < EXP_UNDERFLOW

        def sweep(blocks, run, acc):
            offs = [pl.multiple_of(j * blk, blk) for j, _, _ in blocks]
            z = [[_dot(q_head[a], k_ref[pl.ds(off, blk), :], _NT) for a in heads]
                 for off in offs]
            cl = []
            for (_, diag, valid), zb in zip(blocks, z):
                lkb = []
                for a in heads:
                    lk = -_softplus(zb[a])
                    if diag:
                        lk = jnp.where(causal, lk, 0.0)
                    if valid is not None:
                        lk = jnp.where(valid, lk, 0.0)
                    lkb.append(lk.astype(BF16))
                cl.append([_dot(lkb[a], csum) for a in heads])
            for (_, diag, valid), zb, clb, off in zip(blocks, z, cl, offs):
                w = []
                for a in heads:
                    wa = jnp.exp(zb[a] + clb[a] + run[a])
                    if diag:
                        wa = jnp.where(causal, wa, 0.0)
                    if valid is not None:
                        wa = jnp.where(valid, wa, 0.0)
                    w.append(wa.astype(BF16))
                run = [run[a] + clb[a][:, 0:1] for a in heads]
                v_blk = v_ref[pl.ds(off, blk), :]
                for a in heads:
                    acc = acc + _dot(w[a], head(v_blk, a))
            return run, acc

        run = [jnp.zeros((blk, 1), F32)] * ATT_HEADS
        acc = jnp.zeros((blk, ATT_W), F32)
        run, acc = sweep([(qi, True, None), (jnp.maximum(qi - 1, 0), False, qi >= 1)], run, acc)
        low = jnp.maximum(qi - 1, 0)

        def more(carry):
            low, done, _, _ = carry
            return (low > 0) & jnp.logical_not(done)

        def pair(carry):
            low, _, run, acc = carry
            run, acc = sweep([(low - 1, False, None), (jnp.maximum(low - 2, 0), False, low >= 2)],
                             run, acc)
            return jnp.maximum(low - 2, 0), exhausted(run), run, acc

        low, _, run, acc = lax.while_loop(more, pair, (low, exhausted(run), run, acc))
        low_ref[pl.program_id(0) * N_ATT_GROUPS + pl.program_id(1), qi] = low.astype(F32)
        o_ref[...] = acc
        for a in heads:
            as_row = jnp.sum(jnp.where(r_i == c_i, run[a], 0.0), axis=0, keepdims=True)
            tot_ref[0, a, 0] = jnp.broadcast_to(as_row, (8, blk))

    return pl.pallas_call(
        body, name="sb_attn_fwd",
        grid=(nb, N_ATT_GROUPS, nq),
        in_specs=[pl.BlockSpec((blk, ATT_W), lambda b, h, i: (b * nq + i, h)),
                  pl.BlockSpec((seq, ATT_W), lambda b, h, i: (b, h)),
                  pl.BlockSpec((seq, ATT_W), lambda b, h, i: (b, h)),
                  pl.BlockSpec((1, 1, ATT_HEADS // 2, LANES), lambda b, h, i: (b, h, 0, 0))],
        out_specs=[pl.BlockSpec((blk, ATT_W), lambda b, h, i: (b * nq + i, h)),
                   pl.BlockSpec((1, ATT_HEADS, 1, 8, blk), lambda b, h, i: (b, h, i, 0, 0)),
                   pl.BlockSpec(memory_space=pltpu.SMEM)],
        out_shape=[jax.ShapeDtypeStruct((nb * seq, D_BRANCH), F32),
                   jax.ShapeDtypeStruct((nb, N_HEADS, nq, 8, blk), F32),
                   jax.ShapeDtypeStruct((nb * N_ATT_GROUPS, nq), F32)],
        compiler_params=_params(3),
    )(qs, kn, vb, ksq.reshape(nb, N_ATT_GROUPS, ATT_HEADS // 2, LANES))


def _attn_bwd(qs, kn, kt, vb, tot, low, d_o, nb, seq, blk):
    nq = seq // blk

    def body(q_ref, k_ref, kt_ref, v_ref, tot_ref, low_ref, do_ref, dq_ref, dk_ref, dv_ref):
        qi = pl.program_id(2)

        @pl.when(qi == 0)
        def _():
            dk_ref[...] = jnp.zeros_like(dk_ref)
            dv_ref[...] = jnp.zeros_like(dv_ref)

        r_i = lax.broadcasted_iota(jnp.int32, (blk, blk), 0)
        c_i = lax.broadcasted_iota(jnp.int32, (blk, blk), 1)
        before = jnp.where(c_i < r_i, 1.0, 0.0).astype(BF16)
        upto = jnp.where(c_i <= r_i, 1.0, 0.0).astype(BF16)
        causal = r_i < c_i

        heads = range(ATT_HEADS)
        q_head = [_head_lanes(q_ref[...], a) for a in heads]
        d_ob = [_head_lanes(do_ref[...].astype(BF16), a) for a in heads]
        total = [tot_ref[0, a, 0][0:1, :] for a in heads]

        def sweep(blocks, lsum, esum, dqt):
            def keep(x, diag, valid):
                if diag:
                    x = jnp.where(causal, x, 0.0)
                if valid is not None:
                    x = jnp.where(valid, x, 0.0)
                return x

            offs = [pl.multiple_of(j * blk, blk) for j, _, _ in blocks]
            zt = [[_dot(k_ref[pl.ds(off, blk), :], q_head[a], _NT) for a in heads]
                  for off in offs]
            dwt = [[_dot(v_ref[pl.ds(off, blk), :], d_ob[a], _NT) for a in heads]
                   for off in offs]
            sp, lk, lpre = [], [], []
            for (_, diag, valid), ztb in zip(blocks, zt):
                sp.append([_softplus(ztb[a]) for a in heads])
                lk.append([keep(-sp[-1][a], diag, valid).astype(BF16) for a in heads])
                lpre.append([_dot(before, lk[-1][a]) for a in heads])
            wt, et, epre = [], [], []
            for i, (_, diag, valid) in enumerate(blocks):
                wt.append([keep(jnp.exp(zt[i][a] + (total[a] - lsum[a] - lpre[i][a])), diag, valid)
                           for a in heads])
                et.append([dwt[i][a] * wt[i][a] for a in heads])
                epre.append([_dot(upto, et[i][a].astype(BF16)) for a in heads])
                lsum = [lsum[a] + lpre[i][a][blk - 1:blk, :] + lk[i][a][blk - 1:blk, :]
                        for a in heads]
            for i, (_, diag, valid) in enumerate(blocks):
                dzb = [keep(et[i][a] - jnp.exp(zt[i][a] - sp[i][a]) * (esum[a] + epre[i][a]),
                            diag, valid).astype(BF16) for a in heads]
                esum = [esum[a] + epre[i][a][blk - 1:blk, :] for a in heads]
                dk_ref[pl.ds(offs[i], blk), :] += functools.reduce(
                    jnp.add, [_dot(dzb[a], q_head[a]) for a in heads])
                dv_ref[pl.ds(offs[i], blk), :] += functools.reduce(
                    jnp.add, [_dot(wt[i][a].astype(BF16), d_ob[a]) for a in heads])
                dqt = [dqt[a] + _dot(kt_ref[0, a // 2, (a % 2) * HEAD_DIM:(a % 2 + 1) * HEAD_DIM,
                                            pl.ds(offs[i], blk)], dzb[a]) for a in heads]
            return lsum, esum, dqt

        row = [jnp.zeros((1, blk), F32)] * ATT_HEADS
        dqt = [jnp.zeros((HEAD_DIM, blk), F32)] * ATT_HEADS
        low = low_ref[pl.program_id(0) * N_ATT_GROUPS + pl.program_id(1), qi].astype(jnp.int32)
        low = jnp.clip(low, 0, jnp.maximum(qi - 1, 0))

        def pair(carry):
            j, lsum, esum, dqt = carry
            return (j + 2,) + sweep([(j, False, None), (j + 1, False, j + 1 < qi - 1)],
                                    lsum, esum, dqt)

        _, lsum, esum, dqt = lax.while_loop(lambda c: c[0] < qi - 1, pair, (low, row, row, dqt))
        _, _, dqt = sweep([(jnp.maximum(qi - 1, 0), False, qi >= 1), (qi, True, None)],
                          lsum, esum, dqt)
        dq_ref[...] = jnp.concatenate(dqt, axis=0).T

    seq_blk = pl.BlockSpec((seq, ATT_W), lambda b, h, i: (b, h))
    tok = pl.BlockSpec((blk, ATT_W), lambda b, h, i: (b * nq + i, h))
    tok_shape = jax.ShapeDtypeStruct((nb * seq, D_BRANCH), F32)
    return pl.pallas_call(
        body, name="sb_attn_bwd",
        grid=(nb, N_ATT_GROUPS, nq),
        in_specs=[tok, seq_blk,
                  pl.BlockSpec((1, ATT_HEADS // 2, LANES, seq), lambda b, h, i: (b, h, 0, 0)),
                  seq_blk,
                  pl.BlockSpec((1, ATT_HEADS, 1, 8, blk), lambda b, h, i: (b, h, i, 0, 0)),
                  pl.BlockSpec(memory_space=pltpu.SMEM),
                  tok],
        out_specs=[tok, seq_blk, seq_blk],
        out_shape=[tok_shape, tok_shape, tok_shape],
        compiler_params=_params(3),
    )(qs, kn, kt, vb, tot, low, d_o)


def _qk_bwd(proj, dqs, dkn, dvh, qw2, kw2, nb, seq, tq):
    nl = seq // tq
    scale = 1.0 / math.sqrt(HEAD_DIM)

    def body(q_ref, k_ref, dq_ref, dk_ref, dv_ref, qw_ref, kw_ref,
             dqr_ref, dkr_ref, dvr_ref, gq_ref, gk_ref):
        @pl.when((pl.program_id(0) == 0) & (pl.program_id(1) == 0))
        def _():
            gq_ref[...] = jnp.zeros_like(gq_ref)
            gk_ref[...] = jnp.zeros_like(gk_ref)

        def norm_bwd(x, w, dy):
            r = lax.rsqrt(_pair_sum(x * x) * (1.0 / HEAD_DIM) + EPS)
            xhat = x * r
            g = dy * w
            m = _pair_sum(g * xhat) * (1.0 / HEAD_DIM)
            return r * (g - xhat * m), jnp.sum(dy * xhat, axis=0, keepdims=True)

        dvr_ref[...] = dv_ref[...].astype(BF16)
        gq = jnp.zeros((1, LANES), F32)
        gk = jnp.zeros((1, LANES), F32)
        for p in range(N_PAIRS):
            cols = slice(p * LANES, (p + 1) * LANES)
            dqr, gq_p = norm_bwd(q_ref[:, cols], qw_ref[...], dq_ref[:, cols] * scale)
            dkr, gk_p = norm_bwd(k_ref[:, cols], kw_ref[...], dk_ref[:, cols])
            dqr_ref[:, cols] = dqr.astype(BF16)
            dkr_ref[:, cols] = dkr.astype(BF16)
            gq, gk = gq + gq_p, gk + gk_p
        gq_ref[...] += gq
        gk_ref[...] += gk

    tok = lambda blk: pl.BlockSpec((tq, D_BRANCH), lambda b, i: (b * nl + i, blk))
    vec = pl.BlockSpec((1, LANES), lambda b, i: (0, 0))
    tshape = jax.ShapeDtypeStruct((nb * seq, D_BRANCH), BF16)
    return pl.pallas_call(
        body, name="qk_bwd",
        grid=(nb, nl),
        in_specs=[tok(0), tok(1), tok(0), tok(0), tok(0), vec, vec],
        out_specs=[tok(0), tok(0), tok(0), vec, vec],
        out_shape=[tshape, tshape, tshape,
                   jax.ShapeDtypeStruct((1, LANES), F32), jax.ShapeDtypeStruct((1, LANES), F32)],
        compiler_params=_params(2),
    )(proj, proj, dqs, dkn, dvh, qw2, kw2)


def _shift_down(cur, prev, k):
    if k == 0:
        return cur
    rows = _row_iota(cur.shape)
    return jnp.where(rows < k, pltpu.roll(prev, k, axis=0), pltpu.roll(cur, k, axis=0))


def _shift_up(cur, nxt, k):
    if k == 0:
        return cur
    n = cur.shape[0]
    rows = _row_iota(cur.shape)
    return jnp.where(rows < n - k, pltpu.roll(cur, n - k, axis=0), pltpu.roll(nxt, n - k, axis=0))


def _conv_taps(cur, prev):
    return [_shift_down(cur, prev, CONV_TAPS - 1 - i) for i in range(CONV_TAPS)]


def _conv_pre(taps, w, b):
    out = b
    for i in range(CONV_TAPS):
        out = out + taps[i] * w[i:i + 1, :]
    return out


def _silu(x):
    return x * _sigmoid(x)


def _silu_and_grad(x):
    s = _sigmoid(x)
    return x * s, s * (1.0 + x * (1.0 - s))


def _dot01(x, m01, parts, dims=None, m_left=False):
    total, rest = None, x
    for i in range(parts):
        piece = rest.astype(BF16)
        if i + 1 < parts:
            rest = rest - piece.astype(F32)
        term = _dot(m01, piece, dims) if m_left else _dot(piece, m01, dims)
        total = term if total is None else total + term
    return total


def _chunk_decay(dt_raw, dtb, alog, expand, qc):
    dt = _softplus(dt_raw + dtb)
    d_a = dt * (-jnp.exp(alog))
    r_i = lax.broadcasted_iota(jnp.int32, (qc, qc), 0)
    c_i = lax.broadcasted_iota(jnp.int32, (qc, qc), 1)
    tril = r_i >= c_i
    a_cs = _dot01(d_a, jnp.where(tril, 1.0, 0.0).astype(BF16), 3, m_left=True)
    dt_x = _dot01(dt, expand, 3)
    acs_x = _dot01(a_cs, expand, 3)
    return dt, d_a, a_cs, dt_x, acs_x, tril


def _ssd_fwd(proj, conv_w, conv_b, dtb, alog, dskip, nb, seq):
    qc = SSD_CHUNK
    nc = seq // qc

    def body(xs_ref, bc_ref, dt_ref, cw_ref, cb_ref, dtb_ref, al_ref, ds_ref,
             y_ref, st_ref, pxs_ref, pbc_ref, state_ref):
        @pl.when(pl.program_id(1) == 0)
        def _():
            pxs_ref[...] = jnp.zeros_like(pxs_ref)
            pbc_ref[...] = jnp.zeros_like(pbc_ref)
            state_ref[...] = jnp.zeros_like(state_ref)

        expand = _head_expand()
        xs_raw = xs_ref[...]
        bc_raw = bc_ref[...]
        cw = cw_ref[...]
        cb = cb_ref[...]
        xs = _silu(_conv_pre(_conv_taps(xs_raw, pxs_ref[...]), cw[:, :D_BRANCH], cb[:, :D_BRANCH]))
        bc = _silu(_conv_pre(_conv_taps(bc_raw, pbc_ref[...]), cw[:, D_BRANCH:], cb[:, D_BRANCH:]))
        pxs_ref[...] = xs_raw
        pbc_ref[...] = bc_raw

        dt, d_a, a_cs, dt_x, acs_x, tril = _chunk_decay(
            dt_ref[...], dtb_ref[...], al_ref[...], expand, qc)
        a_cst = a_cs.T
        aend_x = acs_x[qc - 1:qc, :]
        ea_x = jnp.exp(acs_x)
        dec_x = jnp.exp(aend_x - acs_x)
        xt = xs * dt_x
        xtb = xt.astype(BF16)
        xdb = (xt * dec_x).astype(BF16)
        d_x = _dot01(jnp.broadcast_to(ds_ref[...], (8, LANES)), expand, 3)[0:1, :]
        st_ref[0, 0] = state_ref[...]

        for g in range(N_GROUPS):
            gs = slice(g * GROUP_W, (g + 1) * GROUP_W)
            bg = bc[:, g * D_STATE:(g + 1) * D_STATE]
            cg = bc[:, (N_GROUPS + g) * D_STATE:(N_GROUPS + g + 1) * D_STATE]
            bgb = bg.astype(BF16)
            cgb = cg.astype(BF16)
            cbm = _dot(cgb, bgb, _NT)
            st_in = state_ref[g]
            y_off = _dot(cgb, st_in.astype(BF16)) * ea_x[:, gs]
            for k in range(HEADS_PER_GROUP):
                h = g * HEADS_PER_GROUP + k
                hs = slice(h * HEAD_DIM, (h + 1) * HEAD_DIM)
                seg = a_cs[:, h:h + 1] - a_cst[h:h + 1, :]
                gh = cbm * jnp.exp(jnp.where(tril, seg, -1e30))
                y_h = _dot(gh.astype(BF16), xtb[:, hs]) + y_off[:, k * HEAD_DIM:(k + 1) * HEAD_DIM]
                y_ref[:, hs] = y_h + d_x[:, hs] * xs[:, hs]
            state_ref[g] = st_in * jnp.exp(aend_x[:, gs]) + _dot(bg.T.astype(BF16), xdb[:, gs])

    nblk = lambda w, off: pl.BlockSpec((qc, w), lambda b, c: (b * nc + c, off))
    full = lambda r, w: pl.BlockSpec((r, w), lambda b, c: (0, 0))
    return pl.pallas_call(
        body, name="ssd_fwd",
        grid=(nb, nc),
        in_specs=[nblk(D_BRANCH, COL_XS // D_BRANCH), nblk(D_BC, COL_BC // D_BC),
                  nblk(LANES, COL_DT // LANES),
                  full(CONV_TAPS, D_CONV), full(1, D_CONV), full(1, LANES), full(1, LANES),
                  full(1, LANES)],
        out_specs=[pl.BlockSpec((qc, D_BRANCH), lambda b, c: (b * nc + c, 0)),
                   pl.BlockSpec((1, 1, N_GROUPS, D_STATE, GROUP_W), lambda b, c: (b, c, 0, 0, 0))],
        out_shape=[jax.ShapeDtypeStruct((nb * seq, D_BRANCH), F32),
                   jax.ShapeDtypeStruct((nb, nc, N_GROUPS, D_STATE, GROUP_W), F32)],
        scratch_shapes=[pltpu.VMEM((qc, D_BRANCH), F32), pltpu.VMEM((qc, D_BC), F32),
                        pltpu.VMEM((N_GROUPS, D_STATE, GROUP_W), F32)],
        compiler_params=_params(2),
    )(proj, proj, proj, conv_w, conv_b, dtb, alog, dskip)


def _ssd_bwd(proj, d_y, states, conv_w, conv_b, dtb, alog, dskip, nb, seq, slabs=()):
    qc = SSD_CHUNK
    nc = seq // qc

    def body(xs_ref, bc_ref, dt_ref, pxs_ref, pbc_ref, dy_ref, st_ref, stn_ref,
             cw_ref, cb_ref, dtb_ref, al_ref, ds_ref,
             dx_ref, gcw_ref, gcb_ref, gdtb_ref, gal_ref, gds_ref,
             dst_ref, nxs_ref, nbc_ref, yd_ref, dxt_ref):
        step = pl.program_id(1)
        chunk = nc - 1 - step

        @pl.when(step == 0)
        def _():
            dst_ref[...] = jnp.zeros_like(dst_ref)
            nxs_ref[...] = jnp.zeros_like(nxs_ref)
            nbc_ref[...] = jnp.zeros_like(nbc_ref)

        @pl.when((pl.program_id(0) == 0) & (step == 0))
        def _():
            gcw_ref[...] = jnp.zeros_like(gcw_ref)
            gcb_ref[...] = jnp.zeros_like(gcb_ref)
            gdtb_ref[...] = jnp.zeros_like(gdtb_ref)
            gal_ref[...] = jnp.zeros_like(gal_ref)
            gds_ref[...] = jnp.zeros_like(gds_ref)

        expand = _head_expand()
        collapse = lambda v: _dot01(v, expand, 2, _NT)
        first = jnp.where(chunk == 0, 0.0, 1.0)
        xs_raw = xs_ref[...]
        bc_raw = bc_ref[...]
        pxs = pxs_ref[...] * first
        pbc = pbc_ref[...] * first
        cw = cw_ref[...]
        cb = cb_ref[...]
        taps_xs = _conv_taps(xs_raw, pxs)
        taps_bc = _conv_taps(bc_raw, pbc)
        xs, dsilu_xs = _silu_and_grad(_conv_pre(taps_xs, cw[:, :D_BRANCH], cb[:, :D_BRANCH]))
        bc, dsilu_bc = _silu_and_grad(_conv_pre(taps_bc, cw[:, D_BRANCH:], cb[:, D_BRANCH:]))

        dt_in = dt_ref[...] + dtb_ref[...]
        dt, d_a, a_cs, dt_x, acs_x, tril = _chunk_decay(
            dt_ref[...], dtb_ref[...], al_ref[...], expand, qc)
        a_cst = a_cs.T
        aend_x = acs_x[qc - 1:qc, :]
        ea_x = jnp.exp(acs_x)
        dec_x = jnp.exp(aend_x - acs_x)
        xt = xs * dt_x
        xtb = xt.astype(BF16)
        xdb = (xt * dec_x).astype(BF16)
        d_x = _dot01(jnp.broadcast_to(ds_ref[...], (8, LANES)), expand, 3)[0:1, :]

        dy = dy_ref[...]
        dyb = dy.astype(BF16)
        dyeab = (dy * ea_x).astype(BF16)
        gds_ref[...] += collapse(jnp.broadcast_to(jnp.sum(dy * xs, axis=0, keepdims=True),
                                                  (8, D_BRANCH)))[0:1, :]

        d_bc = []
        d_cc = []
        y_offs = []
        dxt_states = []
        end_terms = []
        for g in range(N_GROUPS):
            gs = slice(g * GROUP_W, (g + 1) * GROUP_W)
            bg = bc[:, g * D_STATE:(g + 1) * D_STATE]
            cg = bc[:, (N_GROUPS + g) * D_STATE:(N_GROUPS + g + 1) * D_STATE]
            bgb = bg.astype(BF16)
            cgb = cg.astype(BF16)
            cbm = _dot(cgb, bgb, _NT)
            st_in = st_ref[0, 0, g]
            st_inb = st_in.astype(BF16)
            d_st = dst_ref[g]
            d_stb = d_st.astype(BF16)
            y_offs.append(_dot(cgb, st_inb) * ea_x[:, gs])
            dxt_states.append(_dot(bgb, d_stb) * dec_x[:, gs])
            d_c = _dot(dyeab[:, gs], st_inb, _NT)
            d_b = _dot(xdb[:, gs], d_stb, _NT)
            d_cb = jnp.zeros((qc, qc), F32)
            for k in range(HEADS_PER_GROUP):
                h = g * HEADS_PER_GROUP + k
                hs = slice(h * HEAD_DIM, (h + 1) * HEAD_DIM)
                seg = a_cs[:, h:h + 1] - a_cst[h:h + 1, :]
                lh = jnp.exp(jnp.where(tril, seg, -1e30))
                ghb = (cbm * lh).astype(BF16)
                d_cb = d_cb + _dot(dyb[:, hs], xtb[:, hs], _NT) * lh
                yd_ref[:, hs] = _dot(ghb, xtb[:, hs])
                dxt_ref[:, hs] = _dot(ghb, dyb[:, hs], _TN)
            d_cbb = d_cb.astype(BF16)
            d_cc.append(d_c + _dot(d_cbb, bgb))
            d_bc.append(d_b + _dot(d_cbb, cgb, _TN))
            end_terms.append(jnp.sum(d_st * stn_ref[0, 0, g], axis=0, keepdims=True))
            dst_ref[g] = d_st * jnp.exp(aend_x[:, gs]) + _dot(cg.T.astype(BF16), dyeab[:, gs])

        y_off = jnp.concatenate(y_offs, axis=1)
        dxt_state = jnp.concatenate(dxt_states, axis=1)
        dxt = dxt_ref[...] + dxt_state
        last = jnp.where(chunk == nc - 1, 0.0, 1.0)
        end_c = collapse(jnp.broadcast_to(jnp.concatenate(end_terms, axis=1), (8, D_BRANCH)))[0:1, :]
        da_cs = collapse(dyb.astype(F32) * yd_ref[...] - dxt_ref[...] * xtb.astype(F32)
                         + dy * y_off - dxt_state * xt)
        da_cs = da_cs + jnp.where(_row_iota(da_cs.shape) == qc - 1, end_c * last, 0.0)
        triu = lax.broadcasted_iota(jnp.int32, (qc, qc), 0) <= lax.broadcasted_iota(jnp.int32, (qc, qc), 1)
        dd_a = _dot01(da_cs, jnp.where(triu, 1.0, 0.0).astype(BF16), 3, m_left=True)
        ddt = dd_a * (-jnp.exp(al_ref[...])) + collapse(dxt * xs)
        head_lanes = _lane_iota(ddt.shape) < N_HEADS
        ddt_raw = jnp.where(head_lanes, ddt * _sigmoid(dt_in), 0.0)
        gal_ref[...] += jnp.sum(jnp.where(head_lanes, dd_a * d_a, 0.0), axis=0, keepdims=True)
        gdtb_ref[...] += jnp.sum(ddt_raw, axis=0, keepdims=True)

        dpre_xs = (dxt * dt_x + d_x * dy) * dsilu_xs
        dpre_bc = jnp.concatenate(d_bc + d_cc, axis=1) * dsilu_bc
        gcb_ref[...] += jnp.concatenate([jnp.sum(dpre_xs, axis=0, keepdims=True),
                                         jnp.sum(dpre_bc, axis=0, keepdims=True)], axis=1)
        nxs = nxs_ref[...]
        nbc = nbc_ref[...]
        du_xs = jnp.zeros_like(dpre_xs)
        du_bc = jnp.zeros_like(dpre_bc)
        for i in range(CONV_TAPS):
            k = CONV_TAPS - 1 - i
            gcw_ref[i:i + 1, :] += jnp.concatenate(
                [jnp.sum(dpre_xs * taps_xs[i], axis=0, keepdims=True),
                 jnp.sum(dpre_bc * taps_bc[i], axis=0, keepdims=True)], axis=1)
            du_xs = du_xs + _shift_up(dpre_xs, nxs, k) * cw[i:i + 1, :D_BRANCH]
            du_bc = du_bc + _shift_up(dpre_bc, nbc, k) * cw[i:i + 1, D_BRANCH:]
        nxs_ref[...] = dpre_xs
        nbc_ref[...] = dpre_bc

        dx_ref[:, :D_BRANCH] = du_xs.astype(BF16)
        dx_ref[:, D_BRANCH:D_CONV] = du_bc.astype(BF16)
        dx_ref[:, D_CONV:D_CONV + LANES] = ddt_raw.astype(BF16)
        dx_ref[:, D_CONV + LANES:] = jnp.zeros((qc, 2048 - D_CONV - LANES), BF16)

    rev = lambda b, c: b * nc + (nc - 1 - c)
    prv = lambda b, c: b * nc + jnp.maximum(nc - 2 - c, 0)
    nblk = lambda w, off, f: pl.BlockSpec((qc, w), lambda b, c: (f(b, c), off))
    full = lambda r, w: pl.BlockSpec((r, w), lambda b, c: (0, 0))
    st_spec = lambda f: pl.BlockSpec((1, 1, N_GROUPS, D_STATE, GROUP_W),
                                     lambda b, c: (b, f(c), 0, 0, 0))
    return _call_with_exchange(
        body, (proj, proj, proj, proj, proj, d_y, states, states, conv_w, conv_b, dtb, alog, dskip),
        slabs, (True,) * len(slabs), name="ssd_bwd", grid=(nb, nc),
        in_specs=[nblk(D_BRANCH, COL_XS // D_BRANCH, rev), nblk(D_BC, COL_BC // D_BC, rev),
                  nblk(LANES, COL_DT // LANES, rev),
                  nblk(D_BRANCH, COL_XS // D_BRANCH, prv), nblk(D_BC, COL_BC // D_BC, prv),
                  nblk(D_BRANCH, 0, rev),
                  st_spec(lambda c: nc - 1 - c), st_spec(lambda c: jnp.minimum(nc - c, nc - 1)),
                  full(CONV_TAPS, D_CONV), full(1, D_CONV), full(1, LANES), full(1, LANES),
                  full(1, LANES)],
        out_specs=[nblk(2048, 0, rev), full(8, D_CONV), full(1, D_CONV), full(1, LANES),
                   full(1, LANES), full(1, LANES)],
        out_shape=[jax.ShapeDtypeStruct((nb * seq, 2048), BF16),
                   jax.ShapeDtypeStruct((8, D_CONV), F32), jax.ShapeDtypeStruct((1, D_CONV), F32),
                   jax.ShapeDtypeStruct((1, LANES), F32), jax.ShapeDtypeStruct((1, LANES), F32),
                   jax.ShapeDtypeStruct((1, LANES), F32)],
        scratch_shapes=[pltpu.VMEM((N_GROUPS, D_STATE, GROUP_W), F32),
                        pltpu.VMEM((qc, D_BRANCH), F32), pltpu.VMEM((qc, D_BC), F32),
                        pltpu.VMEM((qc, D_BRANCH), F32), pltpu.VMEM((qc, D_BRANCH), F32)])


def _mid(o_sb, y_ssd, proj, x2, target, sb_w, ssd_w, w_out_b, tm):
    t = x2.shape[0]
    inv_d = 1.0 / D_MODEL

    def body(o_ref, y_ref, zsb_ref, zssd_ref, x_ref, tg_ref, sbw_ref, ssdw_ref, w_ref,
             dout_ref, dosb_ref, dy_ref, dz_ref, gw_ref, gsb_ref, gssd_ref, loss_ref):
        @pl.when(pl.program_id(0) == 0)
        def _():
            gw_ref[...] = jnp.zeros_like(gw_ref)
            gsb_ref[...] = jnp.zeros_like(gsb_ref)
            gssd_ref[...] = jnp.zeros_like(gssd_ref)
            loss_ref[...] = jnp.zeros_like(loss_ref)

        def branch(val, z, w):
            gate, dgate = _silu_and_grad(z)
            g = val * gate
            r = lax.rsqrt(jnp.mean(g * g, axis=1, keepdims=True) + EPS)
            xhat = g * r
            return (gate, dgate, r, xhat), (xhat * w).astype(BF16)

        o = o_ref[...]
        y = y_ref[...]
        saved_a, mix_a = branch(o, zsb_ref[...], sbw_ref[...])
        saved_b, mix_b = branch(y, zssd_ref[...], ssdw_ref[...])
        out = x_ref[...] + _dot(mix_a, w_ref[:D_BRANCH, :]) + _dot(mix_b, w_ref[D_BRANCH:, :])
        diff = out - tg_ref[...]
        loss_ref[...] += 0.5 * inv_d * jnp.sum(diff * diff)
        d_out = diff * inv_d
        dout_ref[...] = d_out
        d_outb = d_out.astype(BF16)
        gw_ref[:D_BRANCH, :] += _dot(mix_a, d_outb, _TN)
        gw_ref[D_BRANCH:, :] += _dot(mix_b, d_outb, _TN)

        def branch_bwd(dmix, val, w, saved):
            gate, dgate, r, xhat = saved
            gg = dmix * w
            m = jnp.mean(gg * xhat, axis=1, keepdims=True)
            dg = r * (gg - xhat * m)
            return dg * gate, dg * val * dgate, jnp.sum(dmix * xhat, axis=0, keepdims=True)

        dmix_a = _dot(d_outb, w_ref[:D_BRANCH, :], _NT)
        dmix_b = _dot(d_outb, w_ref[D_BRANCH:, :], _NT)
        d_o, dz_a, gsb = branch_bwd(dmix_a, o, sbw_ref[...], saved_a)
        d_y, dz_b, gssd = branch_bwd(dmix_b, y, ssdw_ref[...], saved_b)
        dosb_ref[...] = d_o
        dy_ref[...] = d_y
        dz_ref[:, :D_BRANCH] = dz_a.astype(BF16)
        dz_ref[:, D_BRANCH:] = dz_b.astype(BF16)
        gsb_ref[...] += gsb
        gssd_ref[...] += gssd

    row = lambda w, off: pl.BlockSpec((tm, w), lambda i: (i, off))
    full = lambda r, w: pl.BlockSpec((r, w), lambda i: (0, 0))
    resident = pl.BlockSpec((2 * D_BRANCH, D_MODEL), lambda i: (0, 0), pipeline_mode=pl.Buffered(1))
    tok = jax.ShapeDtypeStruct((t, D_MODEL), F32)
    return pl.pallas_call(
        body, name="mid",
        grid=(t // tm,),
        in_specs=[row(D_BRANCH, 0), row(D_BRANCH, 0), row(D_BRANCH, 3), row(D_BRANCH, 4),
                  row(D_MODEL, 0), row(D_MODEL, 0), full(1, D_BRANCH), full(1, D_BRANCH),
                  resident],
        out_specs=[row(D_MODEL, 0), row(D_BRANCH, 0), row(D_BRANCH, 0), row(2 * D_BRANCH, 0),
                   resident, full(1, D_BRANCH), full(1, D_BRANCH),
                   full(1, LANES)],
        out_shape=[tok, tok, tok, jax.ShapeDtypeStruct((t, 2 * D_BRANCH), BF16),
                   jax.ShapeDtypeStruct((2 * D_BRANCH, D_MODEL), F32),
                   jax.ShapeDtypeStruct((1, D_BRANCH), F32), jax.ShapeDtypeStruct((1, D_BRANCH), F32),
                   jax.ShapeDtypeStruct((1, LANES), F32)],
        compiler_params=_params(1),
    )(o_sb, y_ssd, proj, proj, x2, target, sb_w, ssd_w, w_out_b)


_DPROJ_FIRST = (0, 1, 2, 3, 5)
_DPROJ_BLOCKS = (1, 1, 1, 2, 2)


def _in_proj_bwd_x(d_parts, w_in_t, x2, d_out, norm_w, tm, slabs=()):
    t = x2.shape[0]
    n_parts = len(d_parts)

    def body(*refs):
        dp_refs = refs[:n_parts]
        w_ref, x_ref, dout_ref, nw_ref, gx_ref, gnw_ref = refs[n_parts:]

        @pl.when(pl.program_id(0) == 0)
        def _():
            gnw_ref[...] = jnp.zeros_like(gnw_ref)

        d_hn = None
        for p in range(n_parts):
            rows = slice(_DPROJ_FIRST[p] * COLBLK, (_DPROJ_FIRST[p] + _DPROJ_BLOCKS[p]) * COLBLK)
            term = _dot(dp_refs[p][...], w_ref[rows, :])
            d_hn = term if d_hn is None else d_hn + term
        xf = x_ref[...]
        r = lax.rsqrt(jnp.mean(xf * xf, axis=1, keepdims=True) + EPS)
        xhat = xf * r
        g = d_hn * nw_ref[...]
        m = jnp.mean(g * xhat, axis=1, keepdims=True)
        gx_ref[...] = dout_ref[...] + r * (g - xhat * m)
        gnw_ref[...] += jnp.sum(d_hn * xhat, axis=0, keepdims=True)

    row = lambda w: pl.BlockSpec((tm, w), lambda i: (i, 0))
    return _call_with_exchange(
        body, (*d_parts, w_in_t, x2, d_out, norm_w), slabs, (True,) * len(slabs),
        name="in_proj_bwd_x", grid=(t // tm,),
        in_specs=[row(COLBLK * _DPROJ_BLOCKS[p]) for p in range(n_parts)] + [
                  pl.BlockSpec((D_IN_PAD, D_MODEL), lambda i: (0, 0), pipeline_mode=pl.Buffered(1)),
                  row(D_MODEL), row(D_MODEL), pl.BlockSpec((1, D_MODEL), lambda i: (0, 0))],
        out_specs=[row(D_MODEL), pl.BlockSpec((1, D_MODEL), lambda i: (0, 0))],
        out_shape=[jax.ShapeDtypeStruct((t, D_MODEL), F32), jax.ShapeDtypeStruct((1, D_MODEL), F32)])


def _in_proj_bwd_w(hn, d_parts, tm):
    t = hn.shape[0]
    n_parts = len(d_parts)

    def body(hn_ref, *refs):
        dp_refs, gw_ref = refs[:n_parts], refs[n_parts]

        @pl.when(pl.program_id(0) == 0)
        def _():
            gw_ref[...] = jnp.zeros_like(gw_ref)

        hnt = hn_ref[...].astype(F32).T.astype(BF16)
        for p in range(n_parts):
            cols = slice(_DPROJ_FIRST[p] * COLBLK, (_DPROJ_FIRST[p] + _DPROJ_BLOCKS[p]) * COLBLK)
            gw_ref[:, cols] += _dot(hnt, dp_refs[p][...])

    return pl.pallas_call(
        body, name="in_proj_bwd_w",
        grid=(t // tm,),
        in_specs=[pl.BlockSpec((tm, D_MODEL), lambda i: (i, 0))]
                 + [pl.BlockSpec((tm, COLBLK * _DPROJ_BLOCKS[p]), lambda i: (i, 0))
                    for p in range(n_parts)],
        out_specs=pl.BlockSpec((D_MODEL, D_IN_PAD), lambda i: (0, 0), pipeline_mode=pl.Buffered(1)),
        out_shape=jax.ShapeDtypeStruct((D_MODEL, D_IN_PAD), F32),
        compiler_params=_params(1),
    )(hn, *d_parts)


def _adamw(parts, w, m, v, tr, name):
    _, rows, cols = w.shape
    c1 = 1.0 - ADAM_B1 ** ADAM_STEP
    c2 = 1.0 - ADAM_B2 ** ADAM_STEP

    def body(p_ref, w_ref, m_ref, v_ref, g_ref, d_ref, nm_ref, nv_ref):
        g = p_ref[0].astype(F32)
        for s in range(1, N_DEV):
            g = g + p_ref[s].astype(F32)
        nm = ADAM_B1 * m_ref[0] + (1.0 - ADAM_B1) * g
        nv = ADAM_B2 * v_ref[0] + (1.0 - ADAM_B2) * (g * g)
        g_ref[0] = g
        nm_ref[0] = nm
        nv_ref[0] = nv
        d_ref[0] = -ADAM_LR * ((nm / c1) / (jnp.sqrt(nv / c2) + ADAM_EPS) + ADAM_WD * w_ref[0])

    blk = pl.BlockSpec((1, tr, cols), lambda i: (0, i, 0))
    shape = jax.ShapeDtypeStruct((1, rows, cols), F32)
    return pl.pallas_call(
        body, name=name,
        grid=(rows // tr,),
        in_specs=[pl.BlockSpec((N_DEV, tr, cols), lambda i: (0, i, 0)), blk, blk, blk],
        out_specs=[blk, blk, blk, blk],
        out_shape=[shape, shape, shape, shape],
        compiler_params=_params(1),
    )(parts, w, m, v)


def _mesh_place():
    x, y, c = lax.axis_index("x"), lax.axis_index("y"), lax.axis_index("c")
    return x, y, c, 4 * x + 2 * y + c


def _peer(x, y, c, k):
    px = 1 - x if k & 4 else x
    py = 1 - y if k & 2 else y
    pc = 1 - c if k & 1 else c
    return (px, py, pc), 4 * px + 2 * py + pc


def _exchange(srcs, scatter, name):
    n = len(srcs)

    def body(*refs):
        copies = _exchange_copies(refs[:n], refs[n:2 * n], scatter, *refs[2 * n:])
        _exchange_start(copies)
        _exchange_wait(copies)

    return pl.pallas_call(
        body, name=name,
        in_specs=[_ANY] * n, out_specs=[_ANY] * n, out_shape=_exchange_shapes(srcs, scatter),
        scratch_shapes=_exchange_sems(n),
    )(*srcs)


def _call_with_exchange(body, operands, srcs, scatter, *, name, grid, in_specs, out_specs,
                        out_shape, scratch_shapes=()):
    n_in, n_out, n_scr, n_x = len(in_specs), len(out_specs), len(scratch_shapes), len(srcs)
    params = _params(len(grid))
    if not n_x:
        return pl.pallas_call(body, name=name, grid=grid, in_specs=list(in_specs),
                              out_specs=list(out_specs), out_shape=list(out_shape),
                              scratch_shapes=list(scratch_shapes), compiler_params=params)(*operands)

    def wrapped(*refs):
        ins, refs = refs[:n_in], refs[n_in:]
        x_src, refs = refs[:n_x], refs[n_x:]
        outs, refs = refs[:n_out], refs[n_out:]
        x_dst, refs = refs[:n_x], refs[n_x:]
        scratch, sems = refs[:n_scr], refs[n_scr:]
        ids = [pl.program_id(a) for a in range(len(grid))]
        first = functools.reduce(jnp.logical_and, [i == 0 for i in ids])
        last = functools.reduce(jnp.logical_and, [i == n - 1 for i, n in zip(ids, grid)])

        @pl.when(first)
        def _():
            _exchange_start(_exchange_copies(x_src, x_dst, scatter, *sems))

        body(*ins, *outs, *scratch)

        @pl.when(last)
        def _():
            _exchange_wait(_exchange_copies(x_src, x_dst, scatter, *sems))

    return pl.pallas_call(
        wrapped, name=name, grid=grid,
        in_specs=list(in_specs) + [_ANY] * n_x, out_specs=list(out_specs) + [_ANY] * n_x,
        out_shape=list(out_shape) + _exchange_shapes(srcs, scatter),
        scratch_shapes=list(scratch_shapes) + _exchange_sems(n_x), compiler_params=params,
    )(*operands, *srcs)


def _gather_two_level(shard, name):
    def body(x_ref, out_ref, send_sems, recv_sems, local_sem):
        x, y, c, me = _mesh_place()
        sibling = (x, y, 1 - c)
        chips = [(1 - x, y), (x, 1 - y), (1 - x, 1 - y)]

        def slab(px, py, pc):
            return out_ref.at[4 * px + 2 * py + pc]

        def copy(k, block, to, src=None):
            return pltpu.make_async_remote_copy(
                src_ref=slab(*block) if src is None else src, dst_ref=slab(*block),
                send_sem=send_sems.at[k], recv_sem=recv_sems.at[k],
                device_id=to, device_id_type=pl.DeviceIdType.MESH)

        mine = pltpu.make_async_copy(x_ref, slab(x, y, c), local_sem)
        mine.start()
        first = [copy(0, (x, y, c), sibling, src=x_ref)]
        first += [copy(1 + j, (x, y, c), (*chip, c), src=x_ref) for j, chip in enumerate(chips)]
        for cp in first:
            cp.start()
        passed = [copy(4 + j, (*chip, c), sibling) for j, chip in enumerate(chips)]
        for j, chip in enumerate(chips):
            copy(1 + j, (*chip, c), (x, y, c)).wait_recv()
            passed[j].start()
        copy(0, sibling, (x, y, c)).wait_recv()
        for j, chip in enumerate(chips):
            copy(4 + j, (*chip, 1 - c), (x, y, c)).wait_recv()
        for cp in first + passed:
            cp.wait_send()
        mine.wait()

    return pl.pallas_call(
        body, name=name,
        in_specs=[_ANY], out_specs=_ANY,
        out_shape=jax.ShapeDtypeStruct((N_DEV,) + shard.shape, shard.dtype),
        scratch_shapes=[pltpu.SemaphoreType.DMA((N_DEV - 1,)), pltpu.SemaphoreType.DMA((N_DEV - 1,)),
                        pltpu.SemaphoreType.DMA],
    )(shard)


_ANY = pl.BlockSpec(memory_space=pl.ANY)


def _exchange_shapes(srcs, scatter):
    return [jax.ShapeDtypeStruct(s.shape if sc else (N_DEV,) + s.shape, s.dtype)
            for s, sc in zip(srcs, scatter)]


def _exchange_sems(n):
    return [pltpu.SemaphoreType.DMA((n * (N_DEV - 1),)),
            pltpu.SemaphoreType.DMA((n * (N_DEV - 1),)),
            pltpu.SemaphoreType.DMA((n,))]


def _exchange_copies(src_refs, dst_refs, scatter, send_sems, recv_sems, loc_sems):
    n = len(src_refs)
    x, y, c, me = _mesh_place()

    def src_of(i, idx):
        return src_refs[i].at[idx] if scatter[i] else src_refs[i]

    local = [pltpu.make_async_copy(src_of(i, me), dst_refs[i].at[me], loc_sems.at[i])
             for i in range(n)]
    sends, recvs = [], []
    for k in range(1, N_DEV):
        peer, pidx = _peer(x, y, c, k)
        for i in range(n):
            s = i * (N_DEV - 1) + k - 1
            for dst_slab, group in ((me, sends), (pidx, recvs)):
                group.append(pltpu.make_async_remote_copy(
                    src_ref=src_of(i, pidx), dst_ref=dst_refs[i].at[dst_slab],
                    send_sem=send_sems.at[s], recv_sem=recv_sems.at[s],
                    device_id=peer, device_id_type=pl.DeviceIdType.MESH))
    return local, sends, recvs


def _exchange_start(copies):
    local, sends, _ = copies
    for cp in local + sends:
        cp.start()


def _exchange_wait(copies):
    local, sends, recvs = copies
    for cp in recvs:
        cp.wait_recv()
    for cp in sends:
        cp.wait_send()
    for cp in local:
        cp.wait()


def _pad_lanes(v, width=LANES):
    return jnp.pad(v, ((0, 0), (0, width - v.shape[1])))


def _local_step(x, target, norm_w, w_in_b, q_norm_w, k_norm_w, conv_w, conv_b, dt_bias, a_log,
                d_skip, sb_norm_w, ssd_norm_w, w_out_b, tm=256, tq=512, tmid=256, blk=ATT_BLK,
                scatter=False, w_in_t=None):
    nb, seq, _ = x.shape
    t = nb * seq
    x2 = x.reshape(t, D_MODEL)
    tg2 = target.reshape(t, D_MODEL)
    qw2 = jnp.tile(q_norm_w, (1, 2))
    kw2 = jnp.tile(k_norm_w, (1, 2))
    dtb, alog, dsk = _pad_lanes(dt_bias), _pad_lanes(a_log), _pad_lanes(d_skip)

    if w_in_t is None:
        w_in_t = w_in_b.T
    if scatter:
        proj, hn, wout_all, cw_all = _in_proj(x2, norm_w, w_in_b, tm, (w_out_b, conv_w))
        w_out_b = wout_all.reshape(2 * D_BRANCH, D_MODEL)
        conv_w = jnp.transpose(cw_all, (1, 0, 2)).reshape(CONV_TAPS, D_CONV)
    else:
        proj, hn = _in_proj(x2, norm_w, w_in_b, tm)
    qs, kn, vb, kt, ksq = _qk_prep(proj, qw2, kw2, nb, seq, tq)
    o_sb, sb_tot, sb_low = _attn_fwd(qs, kn, vb, jnp.max(ksq, axis=1), nb, seq, blk)
    y_ssd, states = _ssd_fwd(proj, conv_w, conv_b, dtb, alog, dsk, nb, seq)
    d_out, d_osb, d_y, d_z, g_wout, g_sbw, g_ssdw, loss = _mid(
        o_sb, y_ssd, proj, x2, tg2, sb_norm_w, ssd_norm_w, w_out_b, tmid)
    dqs, dkn, dvh = _attn_bwd(qs, kn, kt, vb, sb_tot, sb_low, d_osb, nb, seq, blk)
    dq_raw, dk_raw, dv_raw, g_qw, g_kw = _qk_bwd(proj, dqs, dkn, dvh, qw2, kw2, nb, seq, tq)
    wout_slabs = (g_wout.reshape(N_DEV, 2 * D_BRANCH // N_DEV, D_MODEL).astype(BF16),)
    d_xbc, g_cw, g_cb, g_dtb, g_alog, g_dsk, *moved = _ssd_bwd(
        proj, d_y, states, conv_w, conv_b, dtb, alog, dsk, nb, seq, wout_slabs if scatter else ())
    d_parts = [dq_raw, dk_raw, dv_raw, d_z, d_xbc]
    tall = min(2 * tm, t)
    g_win = _in_proj_bwd_w(hn, d_parts, tall)[:, :D_IN]
    g_cw = g_cw[:CONV_TAPS]
    if scatter:
        g_wout, = moved
        grad_x, g_nw, g_win, g_cw = _in_proj_bwd_x(
            d_parts, w_in_t, x2, d_out, norm_w, tall, _grad_slabs(g_win, g_cw))
    else:
        grad_x, g_nw = _in_proj_bwd_x(d_parts, w_in_t, x2, d_out, norm_w, tall)

    small = dict(
        norm_w=g_nw,
        q_norm_w=g_qw[:, :HEAD_DIM] + g_qw[:, HEAD_DIM:],
        k_norm_w=g_kw[:, :HEAD_DIM] + g_kw[:, HEAD_DIM:],
        conv_b=g_cb, dt_bias=g_dtb[:, :N_HEADS], A_log=g_alog[:, :N_HEADS],
        D_skip=g_dsk[:, :N_HEADS], sb_norm_w=g_sbw, ssd_norm_w=g_ssdw)
    return loss[0, 0], grad_x.reshape(nb, seq, D_MODEL), g_win, g_wout, g_cw, small


def _grad_slabs(g_win, g_cw):
    w_sh = D_IN // N_DEV
    c_sh = D_CONV // N_DEV
    return (jnp.transpose(g_win.reshape(D_MODEL, N_DEV, w_sh), (1, 0, 2)).astype(BF16),
            jnp.pad(jnp.transpose(g_cw.reshape(CONV_TAPS, N_DEV, c_sh), (1, 0, 2)),
                    ((0, 0), (0, 8 - CONV_TAPS), (0, 0))))


_SMALL = ("norm_w", "q_norm_w", "k_norm_w", "conv_b", "dt_bias", "A_log", "D_skip",
          "sb_norm_w", "ssd_norm_w")


def _pack_small(vals):
    flat = jnp.concatenate([_pad_lanes(vals[n], -(-vals[n].shape[1] // LANES) * LANES)
                            for n in _SMALL], axis=1)
    return jnp.pad(flat, ((0, 0), (0, 48 * LANES - flat.shape[1]))).reshape(48, LANES)


def _unpack_small(packed, like):
    out, r = {}, 0
    for n in _SMALL:
        width = like[n].shape[1]
        nr = -(-width // LANES)
        out[n] = packed[r:r + nr].reshape(1, nr * LANES)[:, :width]
        r += nr
    return out


def kernel(x, norm_w, w_in, q_norm_w, k_norm_w, conv_w, conv_b, dt_bias, A_log, D_skip, sb_norm_w, ssd_norm_w, w_out, loss_target, m_norm_w, m_w_in, m_q_norm_w, m_k_norm_w, m_conv_w, m_conv_b, m_dt_bias, m_A_log, m_D_skip, m_sb_norm_w, m_ssd_norm_w, m_w_out, v_norm_w, v_w_in, v_q_norm_w, v_k_norm_w, v_conv_w, v_conv_b, v_dt_bias, v_A_log, v_D_skip, v_sb_norm_w, v_ssd_norm_w, v_w_out):
    win_all = _gather_two_level(w_in[0].astype(BF16), "gather_w_in")
    w_in_b = jnp.pad(jnp.transpose(win_all, (1, 0, 2)).reshape(D_MODEL, D_IN),
                     ((0, 0), (0, D_IN_PAD - D_IN)))
    w_in_t = jnp.pad(jnp.transpose(win_all, (0, 2, 1)).reshape(D_IN, D_MODEL),
                     ((0, D_IN_PAD - D_IN), (0, 0)))

    loss, grad_x, win_parts, wout_parts, cw_parts, g_small = _local_step(
        x, loss_target, norm_w, w_in_b, q_norm_w, k_norm_w, conv_w[0], conv_b, dt_bias, A_log,
        D_skip, sb_norm_w, ssd_norm_w, w_out[0].astype(BF16), scatter=True, w_in_t=w_in_t)
    packed = _pack_small(g_small).at[-1, 0].set(loss)
    small_parts, = _exchange([packed], [False], "gather_small_grads")
    loss = jnp.sum(small_parts[:, -1, 0])

    small_w = dict(norm_w=norm_w, q_norm_w=q_norm_w, k_norm_w=k_norm_w, conv_b=conv_b,
                   dt_bias=dt_bias, A_log=A_log, D_skip=D_skip, sb_norm_w=sb_norm_w,
                   ssd_norm_w=ssd_norm_w)
    small_m = dict(norm_w=m_norm_w, q_norm_w=m_q_norm_w, k_norm_w=m_k_norm_w, conv_b=m_conv_b,
                   dt_bias=m_dt_bias, A_log=m_A_log, D_skip=m_D_skip, sb_norm_w=m_sb_norm_w,
                   ssd_norm_w=m_ssd_norm_w)
    small_v = dict(norm_w=v_norm_w, q_norm_w=v_q_norm_w, k_norm_w=v_k_norm_w, conv_b=v_conv_b,
                   dt_bias=v_dt_bias, A_log=v_A_log, D_skip=v_D_skip, sb_norm_w=v_sb_norm_w,
                   ssd_norm_w=v_ssd_norm_w)

    pad8 = lambda a: jnp.pad(a, ((0, 0), (0, 8 - CONV_TAPS), (0, 0)))
    r_win = _adamw(win_parts, w_in, m_w_in, v_w_in, 128, "adamw_w_in")
    r_wout = _adamw(wout_parts, w_out, m_w_out, v_w_out, 128, "adamw_w_out")
    r_cw = _adamw(cw_parts, pad8(conv_w), pad8(m_conv_w), pad8(v_conv_w), 8, "adamw_conv_w")
    r_small = _adamw(small_parts, _pack_small(small_w)[None], _pack_small(small_m)[None],
                     _pack_small(small_v)[None], 48, "adamw_small")

    res = {"w_in": r_win, "w_out": r_wout, "conv_w": [a[:, :CONV_TAPS] for a in r_cw]}
    unpacked = [_unpack_small(a[0], small_w) for a in r_small]
    for n in _SMALL:
        res[n] = [u[n] for u in unpacked]
    order = ("norm_w", "w_in", "q_norm_w", "k_norm_w", "conv_w", "conv_b", "dt_bias", "A_log",
             "D_skip", "sb_norm_w", "ssd_norm_w", "w_out")
    outs = [loss, grad_x]
    for kind in range(4):
        outs += [res[n][kind] for n in order]
    return tuple(outs)
```

```python
import functools
import math

import jax
import jax.numpy as jnp
from jax import lax
from jax.experimental import pallas as pl
from jax.experimental.pallas import tpu as pltpu

F32 = jnp.float32
BF16 = jnp.bfloat16

D_MODEL = 1024
N_HEADS = 16
HEAD_DIM = 64
N_PAIRS = N_HEADS // 2
D_BRANCH = 1024
N_GROUPS = 2
HEADS_PER_GROUP = 8
D_STATE = 128
GROUP_W = HEADS_PER_GROUP * HEAD_DIM
D_BC = 2 * N_GROUPS * D_STATE
D_CONV = D_BRANCH + D_BC
D_IN = 6672
COLBLK = 1024
D_IN_PAD = 7168
N_COLBLK = D_IN_PAD // COLBLK
COL_XS = 5120
COL_BC = 6144
COL_DT = 6656
EPS = 1e-6
CONV_TAPS = 4
N_DEV = 8

LANES = 128
SSD_CHUNK = 128
ATT_BLK = 256
ATT_HEADS = 4
ATT_W = ATT_HEADS * HEAD_DIM
N_ATT_GROUPS = N_HEADS // ATT_HEADS
EXP_UNDERFLOW = -105.0
VMEM_LIMIT = 56 * 1024 * 1024

ADAM_LR = 0.001
ADAM_B1 = 0.9
ADAM_B2 = 0.999
ADAM_EPS = 1e-08
ADAM_WD = 0.01
ADAM_STEP = 10

_NT = (((1,), (1,)), ((), ()))
_TN = (((0,), (0,)), ((), ()))


def _params(n_grid):
    return pltpu.CompilerParams(dimension_semantics=("arbitrary",) * n_grid,
                                vmem_limit_bytes=VMEM_LIMIT)


def _dot(a, b, dims=None):
    if dims is None:
        return jnp.dot(a, b, preferred_element_type=F32)
    return lax.dot_general(a, b, dims, preferred_element_type=F32)


def _sigmoid(x):
    return 1.0 / (1.0 + jnp.exp(-x))


def _softplus(x):
    return jnp.maximum(x, 0.0) + jnp.log(1.0 + jnp.exp(-jnp.abs(x)))


def _split_bf16(x):
    hi = x.astype(BF16)
    lo = (x - hi.astype(F32)).astype(BF16)
    return hi, lo


def _lane_iota(shape):
    return lax.broadcasted_iota(jnp.int32, shape, len(shape) - 1)


def _row_iota(shape):
    return lax.broadcasted_iota(jnp.int32, shape, len(shape) - 2)


def _pair_sum(x):
    r = lax.broadcasted_iota(jnp.int32, (LANES, LANES), 0)
    c = lax.broadcasted_iota(jnp.int32, (LANES, LANES), 1)
    same_head = jnp.where(r // HEAD_DIM == c // HEAD_DIM, 1.0, 0.0).astype(BF16)
    hi, lo = _split_bf16(x)
    return _dot(hi, same_head) + _dot(lo, same_head)


def _head_lanes(x, a):
    lane = _lane_iota(x.shape)
    mine = (lane >= a * HEAD_DIM) & (lane < (a + 1) * HEAD_DIM)
    return jnp.where(mine, x, jnp.zeros_like(x))


def _head_expand():
    r = lax.broadcasted_iota(jnp.int32, (LANES, D_BRANCH), 0)
    c = lax.broadcasted_iota(jnp.int32, (LANES, D_BRANCH), 1)
    return jnp.where(c // HEAD_DIM == r, 1.0, 0.0).astype(BF16)


def _in_proj(x2, norm_w, w_in_b, tm, shards=()):
    t = x2.shape[0]

    def body(x_ref, nw_ref, w_ref, proj_ref, hn_ref):
        xf = x_ref[...]
        r = lax.rsqrt(jnp.mean(xf * xf, axis=1, keepdims=True) + EPS)
        hn = (xf * r * nw_ref[...]).astype(BF16)
        hn_ref[...] = hn
        for j in range(N_COLBLK):
            cols = slice(j * COLBLK, (j + 1) * COLBLK)
            proj_ref[:, cols] = _dot(hn, w_ref[:, cols])

    return _call_with_exchange(
        body, (x2, norm_w, w_in_b), shards, (False,) * len(shards), name="in_proj",
        grid=(t // tm,),
        in_specs=[pl.BlockSpec((tm, D_MODEL), lambda i: (i, 0)),
                  pl.BlockSpec((1, D_MODEL), lambda i: (0, 0)),
                  pl.BlockSpec((D_MODEL, D_IN_PAD), lambda i: (0, 0), pipeline_mode=pl.Buffered(1))],
        out_specs=[pl.BlockSpec((tm, D_IN_PAD), lambda i: (i, 0)),
                   pl.BlockSpec((tm, D_MODEL), lambda i: (i, 0))],
        out_shape=[jax.ShapeDtypeStruct((t, D_IN_PAD), F32),
                   jax.ShapeDtypeStruct((t, D_MODEL), BF16)])


def _qk_prep(proj, qw2, kw2, nb, seq, tq):
    nl = seq // tq
    scale = 1.0 / math.sqrt(HEAD_DIM)

    def body(q_ref, k_ref, v_ref, qw_ref, kw_ref, qs_ref, kn_ref, vb_ref, kt_ref, ksq_ref):
        def norm(x, w):
            r = lax.rsqrt(_pair_sum(x * x) * (1.0 / HEAD_DIM) + EPS)
            return x * r * w

        vb_ref[...] = v_ref[...].astype(BF16)
        for p in range(N_PAIRS):
            cols = slice(p * LANES, (p + 1) * LANES)
            kn = norm(k_ref[:, cols], kw_ref[...])
            knb = kn.astype(BF16)
            qs_ref[:, cols] = (norm(q_ref[:, cols], qw_ref[...]) * scale).astype(BF16)
            kn_ref[:, cols] = knb
            kt_ref[0, p] = kn.T.astype(BF16)
            kf = knb.astype(F32)
            ksq_ref[0, 0, p:p + 1, :] = jnp.max(_pair_sum(kf * kf), axis=0, keepdims=True) * 1.0001

    tok_shape = jax.ShapeDtypeStruct((nb * seq, D_BRANCH), BF16)
    tok = lambda blk: pl.BlockSpec((tq, D_BRANCH), lambda b, i: (b * nl + i, blk))
    vec = pl.BlockSpec((1, LANES), lambda b, i: (0, 0))
    return pl.pallas_call(
        body, name="qk_prep",
        grid=(nb, nl),
        in_specs=[tok(0), tok(1), tok(2), vec, vec],
        out_specs=[tok(0), tok(0), tok(0),
                   pl.BlockSpec((1, N_PAIRS, LANES, tq), lambda b, i: (b, 0, 0, i)),
                   pl.BlockSpec((1, 1, N_PAIRS, LANES), lambda b, i: (b, i, 0, 0))],
        out_shape=[tok_shape, tok_shape, tok_shape,
                   jax.ShapeDtypeStruct((nb, N_PAIRS, LANES, seq), BF16),
                   jax.ShapeDtypeStruct((nb, nl, N_PAIRS, LANES), F32)],
        compiler_params=_params(2),
    )(proj, proj, proj, qw2, kw2)


def _attn_fwd(qs, kn, vb, ksq, nb, seq, blk):
    nq = seq // blk

    def body(q_ref, k_ref, v_ref, ksq_ref, o_ref, tot_ref, low_ref):
        qi = pl.program_id(2)
        r_i = lax.broadcasted_iota(jnp.int32, (blk, blk), 0)
        c_i = lax.broadcasted_iota(jnp.int32, (blk, blk), 1)
        csum = jnp.where(r_i >= c_i, 1.0, 0.0).astype(BF16)
        heads = range(ATT_HEADS)
        head = _head_lanes

        q_blk = q_ref[...]
        qf = q_blk.astype(F32)
        q_head = [head(q_blk, a) for a in heads]
        zmax = []
        for a in heads:
            qsq = jnp.sum(head(qf * qf, a), axis=1, keepdims=True)
            kmax = ksq_ref[0, 0, a // 2:a // 2 + 1, (a % 2) * HEAD_DIM:(a % 2) * HEAD_DIM + 1]
            zmax.append(1.01 * jnp.sqrt(qsq * kmax) + 0.01)

        def exhausted(run):
            top = functools.reduce(jnp.maximum, [jnp.max(run[a] + zmax[a]) for a in heads])
            return top < EXP_UNDERFLOW

        def sweep(blocks, run, acc):
            half = blk // 2

            def tiles(diag):
                return [(0, half, half), (half, half, blk)] if diag else [(0, blk, blk)]

            def keep(x, r0, diag):
                if diag:
                    rows = lax.broadcasted_iota(jnp.int32, x.shape, 0) + r0
                    x = jnp.where(lax.broadcasted_iota(jnp.int32, x.shape, 1) < rows, x, 0.0)
                return x

            offs = [pl.multiple_of(j * blk, blk) for j, _, _ in blocks]
            z = [[[_dot(q_head[a][r0:r0 + nr], k_ref[pl.ds(off, nk), :], _NT) for a in heads]
                  for r0, nr, nk in tiles(diag)] for (_, diag, _), off in zip(blocks, offs)]
            cl = [[[_dot(keep(-_softplus(zt[a]), r0, diag).astype(BF16), csum[:nk, :nk])
                    for a in heads]
                   for (r0, nr, nk), zt in zip(tiles(diag), zb)]
                  for (_, diag, _), zb in zip(blocks, z)]
            for (_, diag, valid), zb, clb, off in zip(blocks, z, cl, offs):
                rows_out, steps = [], [[] for _ in heads]
                for (r0, nr, nk), zt, clt in zip(tiles(diag), zb, clb):
                    v_blk = v_ref[pl.ds(off, nk), :]
                    part = None
                    for a in heads:
                        wa = keep(jnp.exp(zt[a] + clt[a] + run[a][r0:r0 + nr]), r0, diag)
                        term = _dot(wa.astype(BF16), head(v_blk, a))
                        part = term if part is None else part + term
                        steps[a].append(clt[a][:, 0:1])
                    rows_out.append(part)
                part = jnp.concatenate(rows_out, axis=0)
                steps = [jnp.concatenate(steps[a], axis=0) for a in heads]
                if valid is not None:
                    part = jnp.where(valid, part, 0.0)
                    steps = [jnp.where(valid, s, 0.0) for s in steps]
                acc = acc + part
                run = [run[a] + steps[a] for a in heads]
            return run, acc

        run = [jnp.zeros((blk, 1), F32)] * ATT_HEADS
        acc = jnp.zeros((blk, ATT_W), F32)
        run, acc = sweep([(qi, True, None), (jnp.maximum(qi - 1, 0), False, qi >= 1)], run, acc)
        low = jnp.maximum(qi - 1, 0)

        def more(carry):
            low, done, _, _ = carry
            return (low > 0) & jnp.logical_not(done)

        def pair(carry):
            low, _, run, acc = carry
            run, acc = sweep([(low - 1, False, None), (jnp.maximum(low - 2, 0), False, low >= 2)],
                             run, acc)
            return jnp.maximum(low - 2, 0), exhausted(run), run, acc

        low, _, run, acc = lax.while_loop(more, pair, (low, exhausted(run), run, acc))
        low_ref[pl.program_id(0) * N_ATT_GROUPS + pl.program_id(1), qi] = low.astype(F32)
        o_ref[...] = acc
        for a in heads:
            as_row = jnp.sum(jnp.where(r_i == c_i, run[a], 0.0), axis=0, keepdims=True)
            tot_ref[0, a, 0] = jnp.broadcast_to(as_row, (8, blk))

    return pl.pallas_call(
        body, name="sb_attn_fwd",
        grid=(nb, N_ATT_GROUPS, nq),
        in_specs=[pl.BlockSpec((blk, ATT_W), lambda b, h, i: (b * nq + i, h)),
                  pl.BlockSpec((seq, ATT_W), lambda b, h, i: (b, h)),
                  pl.BlockSpec((seq, ATT_W), lambda b, h, i: (b, h)),
                  pl.BlockSpec((1, 1, ATT_HEADS // 2, LANES), lambda b, h, i: (b, h, 0, 0))],
        out_specs=[pl.BlockSpec((blk, ATT_W), lambda b, h, i: (b * nq + i, h)),
                   pl.BlockSpec((1, ATT_HEADS, 1, 8, blk), lambda b, h, i: (b, h, i, 0, 0)),
                   pl.BlockSpec(memory_space=pltpu.SMEM)],
        out_shape=[jax.ShapeDtypeStruct((nb * seq, D_BRANCH), F32),
                   jax.ShapeDtypeStruct((nb, N_HEADS, nq, 8, blk), F32),
                   jax.ShapeDtypeStruct((nb * N_ATT_GROUPS, nq), F32)],
        compiler_params=_params(3),
    )(qs, kn, vb, ksq.reshape(nb, N_ATT_GROUPS, ATT_HEADS // 2, LANES))


def _attn_bwd(qs, kn, kt, vb, tot, low, d_o, nb, seq, blk):
    nq = seq // blk

    def body(q_ref, k_ref, kt_ref, v_ref, tot_ref, low_ref, do_ref, dq_ref, dk_ref, dv_ref):
        qi = pl.program_id(2)

        @pl.when(qi == 0)
        def _():
            dk_ref[...] = jnp.zeros_like(dk_ref)
            dv_ref[...] = jnp.zeros_like(dv_ref)

        r_i = lax.broadcasted_iota(jnp.int32, (blk, blk), 0)
        c_i = lax.broadcasted_iota(jnp.int32, (blk, blk), 1)
        before = jnp.where(c_i < r_i, 1.0, 0.0).astype(BF16)
        upto = jnp.where(c_i <= r_i, 1.0, 0.0).astype(BF16)
        causal = r_i < c_i

        heads = range(ATT_HEADS)
        q_head = [_head_lanes(q_ref[...], a) for a in heads]
        d_ob = [_head_lanes(do_ref[...].astype(BF16), a) for a in heads]
        total = [tot_ref[0, a, 0][0:1, :] for a in heads]

        def sweep(blocks, lsum, esum, dqt):
            def keep(x, diag):
                return jnp.where(causal, x, 0.0) if diag else x

            def there(x, valid):
                return x if valid is None else jnp.where(valid, x, 0.0)

            offs = [pl.multiple_of(j * blk, blk) for j, _, _ in blocks]
            zt = [[_dot(k_ref[pl.ds(off, blk), :], q_head[a], _NT) for a in heads]
                  for off in offs]
            dwt = [[_dot(v_ref[pl.ds(off, blk), :], d_ob[a], _NT) for a in heads]
                   for off in offs]
            sp, lk, lpre = [], [], []
            for (_, diag, _), ztb in zip(blocks, zt):
                sp.append([_softplus(ztb[a]) for a in heads])
                lk.append([keep(-sp[-1][a], diag).astype(BF16) for a in heads])
                lpre.append([_dot(before, lk[-1][a]) for a in heads])
            wt, et, epre = [], [], []
            for i, (_, diag, valid) in enumerate(blocks):
                wt.append([keep(jnp.exp(zt[i][a] + (total[a] - lsum[a] - lpre[i][a])), diag)
                           for a in heads])
                et.append([dwt[i][a] * wt[i][a] for a in heads])
                epre.append([_dot(upto, et[i][a].astype(BF16)) for a in heads])
                lsum = [lsum[a] + there(lpre[i][a][blk - 1:blk, :] + lk[i][a][blk - 1:blk, :], valid)
                        for a in heads]
            for i, (_, diag, valid) in enumerate(blocks):
                dzb = [keep(et[i][a] - jnp.exp(zt[i][a] - sp[i][a]) * (esum[a] + epre[i][a]),
                            diag).astype(BF16) for a in heads]
                esum = [esum[a] + there(epre[i][a][blk - 1:blk, :], valid) for a in heads]
                dk_ref[pl.ds(offs[i], blk), :] += there(functools.reduce(
                    jnp.add, [_dot(dzb[a], q_head[a]) for a in heads]), valid)
                dv_ref[pl.ds(offs[i], blk), :] += there(functools.reduce(
                    jnp.add, [_dot(wt[i][a].astype(BF16), d_ob[a]) for a in heads]), valid)
                dqt = [dqt[a] + there(_dot(
                    kt_ref[0, a // 2, (a % 2) * HEAD_DIM:(a % 2 + 1) * HEAD_DIM,
                           pl.ds(offs[i], blk)], dzb[a]), valid) for a in heads]
            return lsum, esum, dqt

        row = [jnp.zeros((1, blk), F32)] * ATT_HEADS
        dqt = [jnp.zeros((HEAD_DIM, blk), F32)] * ATT_HEADS
        low = low_ref[pl.program_id(0) * N_ATT_GROUPS + pl.program_id(1), qi].astype(jnp.int32)
        low = jnp.clip(low, 0, jnp.maximum(qi - 1, 0))

        def pair(carry):
            j, lsum, esum, dqt = carry
            return (j + 2,) + sweep([(j, False, None), (j + 1, False, j + 1 < qi - 1)],
                                    lsum, esum, dqt)

        _, lsum, esum, dqt = lax.while_loop(lambda c: c[0] < qi - 1, pair, (low, row, row, dqt))
        _, _, dqt = sweep([(jnp.maximum(qi - 1, 0), False, qi >= 1), (qi, True, None)],
                          lsum, esum, dqt)
        dq_ref[...] = jnp.concatenate(dqt, axis=0).T

    seq_blk = pl.BlockSpec((seq, ATT_W), lambda b, h, i: (b, h))
    tok = pl.BlockSpec((blk, ATT_W), lambda b, h, i: (b * nq + i, h))
    tok_shape = jax.ShapeDtypeStruct((nb * seq, D_BRANCH), F32)
    return pl.pallas_call(
        body, name="sb_attn_bwd",
        grid=(nb, N_ATT_GROUPS, nq),
        in_specs=[tok, seq_blk,
                  pl.BlockSpec((1, ATT_HEADS // 2, LANES, seq), lambda b, h, i: (b, h, 0, 0)),
                  seq_blk,
                  pl.BlockSpec((1, ATT_HEADS, 1, 8, blk), lambda b, h, i: (b, h, i, 0, 0)),
                  pl.BlockSpec(memory_space=pltpu.SMEM),
                  tok],
        out_specs=[tok, seq_blk, seq_blk],
        out_shape=[tok_shape, tok_shape, tok_shape],
        compiler_params=_params(3),
    )(qs, kn, kt, vb, tot, low, d_o)


def _qk_bwd(proj, dqs, dkn, dvh, qw2, kw2, nb, seq, tq):
    nl = seq // tq
    scale = 1.0 / math.sqrt(HEAD_DIM)

    def body(q_ref, k_ref, dq_ref, dk_ref, dv_ref, qw_ref, kw_ref,
             dqr_ref, dkr_ref, dvr_ref, gq_ref, gk_ref):
        @pl.when((pl.program_id(0) == 0) & (pl.program_id(1) == 0))
        def _():
            gq_ref[...] = jnp.zeros_like(gq_ref)
            gk_ref[...] = jnp.zeros_like(gk_ref)

        def norm_bwd(x, w, dy):
            r = lax.rsqrt(_pair_sum(x * x) * (1.0 / HEAD_DIM) + EPS)
            xhat = x * r
            g = dy * w
            m = _pair_sum(g * xhat) * (1.0 / HEAD_DIM)
            return r * (g - xhat * m), jnp.sum(dy * xhat, axis=0, keepdims=True)

        dvr_ref[...] = dv_ref[...].astype(BF16)
        gq = jnp.zeros((1, LANES), F32)
        gk = jnp.zeros((1, LANES), F32)
        for p in range(N_PAIRS):
            cols = slice(p * LANES, (p + 1) * LANES)
            dqr, gq_p = norm_bwd(q_ref[:, cols], qw_ref[...], dq_ref[:, cols] * scale)
            dkr, gk_p = norm_bwd(k_ref[:, cols], kw_ref[...], dk_ref[:, cols])
            dqr_ref[:, cols] = dqr.astype(BF16)
            dkr_ref[:, cols] = dkr.astype(BF16)
            gq, gk = gq + gq_p, gk + gk_p
        gq_ref[...] += gq
        gk_ref[...] += gk

    tok = lambda blk: pl.BlockSpec((tq, D_BRANCH), lambda b, i: (b * nl + i, blk))
    vec = pl.BlockSpec((1, LANES), lambda b, i: (0, 0))
    tshape = jax.ShapeDtypeStruct((nb * seq, D_BRANCH), BF16)
    return pl.pallas_call(
        body, name="qk_bwd",
        grid=(nb, nl),
        in_specs=[tok(0), tok(1), tok(0), tok(0), tok(0), vec, vec],
        out_specs=[tok(0), tok(0), tok(0), vec, vec],
        out_shape=[tshape, tshape, tshape,
                   jax.ShapeDtypeStruct((1, LANES), F32), jax.ShapeDtypeStruct((1, LANES), F32)],
        compiler_params=_params(2),
    )(proj, proj, dqs, dkn, dvh, qw2, kw2)


def _shift_down(cur, prev, k):
    if k == 0:
        return cur
    rows = _row_iota(cur.shape)
    return jnp.where(rows < k, pltpu.roll(prev, k, axis=0), pltpu.roll(cur, k, axis=0))


def _shift_up(cur, nxt, k):
    if k == 0:
        return cur
    n = cur.shape[0]
    rows = _row_iota(cur.shape)
    return jnp.where(rows < n - k, pltpu.roll(cur, n - k, axis=0), pltpu.roll(nxt, n - k, axis=0))


def _conv_taps(cur, prev):
    return [_shift_down(cur, prev, CONV_TAPS - 1 - i) for i in range(CONV_TAPS)]


def _conv_pre(taps, w, b):
    out = b
    for i in range(CONV_TAPS):
        out = out + taps[i] * w[i:i + 1, :]
    return out


def _silu(x):
    return x * _sigmoid(x)


def _silu_and_grad(x):
    s = _sigmoid(x)
    return x * s, s * (1.0 + x * (1.0 - s))


def _dot01(x, m01, parts, dims=None, m_left=False):
    total, rest = None, x
    for i in range(parts):
        piece = rest.astype(BF16)
        if i + 1 < parts:
            rest = rest - piece.astype(F32)
        term = _dot(m01, piece, dims) if m_left else _dot(piece, m01, dims)
        total = term if total is None else total + term
    return total


def _chunk_decay(dt_raw, dtb, alog, expand, qc):
    dt = _softplus(dt_raw + dtb)
    d_a = dt * (-jnp.exp(alog))
    r_i = lax.broadcasted_iota(jnp.int32, (qc, qc), 0)
    c_i = lax.broadcasted_iota(jnp.int32, (qc, qc), 1)
    tril = r_i >= c_i
    a_cs = _dot01(d_a, jnp.where(tril, 1.0, 0.0).astype(BF16), 3, m_left=True)
    dt_x = _dot01(dt, expand, 3)
    acs_x = _dot01(a_cs, expand, 3)
    return dt, d_a, a_cs, dt_x, acs_x, tril


def _ssd_fwd(proj, conv_w, conv_b, dtb, alog, dskip, nb, seq):
    qc = SSD_CHUNK
    nc = seq // qc

    def body(xs_ref, bc_ref, dt_ref, cw_ref, cb_ref, dtb_ref, al_ref, ds_ref,
             y_ref, st_ref, pxs_ref, pbc_ref, state_ref):
        @pl.when(pl.program_id(1) == 0)
        def _():
            pxs_ref[...] = jnp.zeros_like(pxs_ref)
            pbc_ref[...] = jnp.zeros_like(pbc_ref)
            state_ref[...] = jnp.zeros_like(state_ref)

        expand = _head_expand()
        xs_raw = xs_ref[...]
        bc_raw = bc_ref[...]
        cw = cw_ref[...]
        cb = cb_ref[...]
        xs = _silu(_conv_pre(_conv_taps(xs_raw, pxs_ref[...]), cw[:, :D_BRANCH], cb[:, :D_BRANCH]))
        bc = _silu(_conv_pre(_conv_taps(bc_raw, pbc_ref[...]), cw[:, D_BRANCH:], cb[:, D_BRANCH:]))
        pxs_ref[...] = xs_raw
        pbc_ref[...] = bc_raw

        dt, d_a, a_cs, dt_x, acs_x, tril = _chunk_decay(
            dt_ref[...], dtb_ref[...], al_ref[...], expand, qc)
        a_cst = a_cs.T
        aend_x = acs_x[qc - 1:qc, :]
        ea_x = jnp.exp(acs_x)
        dec_x = jnp.exp(aend_x - acs_x)
        xt = xs * dt_x
        xtb = xt.astype(BF16)
        xdb = (xt * dec_x).astype(BF16)
        d_x = _dot01(jnp.broadcast_to(ds_ref[...], (8, LANES)), expand, 3)[0:1, :]
        st_ref[0, 0] = state_ref[...]

        for g in range(N_GROUPS):
            gs = slice(g * GROUP_W, (g + 1) * GROUP_W)
            bg = bc[:, g * D_STATE:(g + 1) * D_STATE]
            cg = bc[:, (N_GROUPS + g) * D_STATE:(N_GROUPS + g + 1) * D_STATE]
            bgb = bg.astype(BF16)
            cgb = cg.astype(BF16)
            cbm = _dot(cgb, bgb, _NT)
            st_in = state_ref[g]
            y_off = _dot(cgb, st_in.astype(BF16)) * ea_x[:, gs]
            for k in range(HEADS_PER_GROUP):
                h = g * HEADS_PER_GROUP + k
                hs = slice(h * HEAD_DIM, (h + 1) * HEAD_DIM)
                seg = a_cs[:, h:h + 1] - a_cst[h:h + 1, :]
                gh = cbm * jnp.exp(jnp.where(tril, seg, -1e30))
                y_h = _dot(gh.astype(BF16), xtb[:, hs]) + y_off[:, k * HEAD_DIM:(k + 1) * HEAD_DIM]
                y_ref[:, hs] = y_h + d_x[:, hs] * xs[:, hs]
            state_ref[g] = st_in * jnp.exp(aend_x[:, gs]) + _dot(bg.T.astype(BF16), xdb[:, gs])

    nblk = lambda w, off: pl.BlockSpec((qc, w), lambda b, c: (b * nc + c, off))
    full = lambda r, w: pl.BlockSpec((r, w), lambda b, c: (0, 0))
    return pl.pallas_call(
        body, name="ssd_fwd",
        grid=(nb, nc),
        in_specs=[nblk(D_BRANCH, COL_XS // D_BRANCH), nblk(D_BC, COL_BC // D_BC),
                  nblk(LANES, COL_DT // LANES),
                  full(CONV_TAPS, D_CONV), full(1, D_CONV), full(1, LANES), full(1, LANES),
                  full(1, LANES)],
        out_specs=[pl.BlockSpec((qc, D_BRANCH), lambda b, c: (b * nc + c, 0)),
                   pl.BlockSpec((1, 1, N_GROUPS, D_STATE, GROUP_W), lambda b, c: (b, c, 0, 0, 0))],
        out_shape=[jax.ShapeDtypeStruct((nb * seq, D_BRANCH), F32),
                   jax.ShapeDtypeStruct((nb, nc, N_GROUPS, D_STATE, GROUP_W), F32)],
        scratch_shapes=[pltpu.VMEM((qc, D_BRANCH), F32), pltpu.VMEM((qc, D_BC), F32),
                        pltpu.VMEM((N_GROUPS, D_STATE, GROUP_W), F32)],
        compiler_params=_params(2),
    )(proj, proj, proj, conv_w, conv_b, dtb, alog, dskip)


def _ssd_bwd(proj, d_y, states, conv_w, conv_b, dtb, alog, dskip, nb, seq, slabs=()):
    qc = SSD_CHUNK
    nc = seq // qc

    def body(xs_ref, bc_ref, dt_ref, pxs_ref, pbc_ref, dy_ref, st_ref, stn_ref,
             cw_ref, cb_ref, dtb_ref, al_ref, ds_ref,
             dx_ref, gcw_ref, gcb_ref, gdtb_ref, gal_ref, gds_ref,
             dst_ref, nxs_ref, nbc_ref, yd_ref, dxt_ref):
        step = pl.program_id(1)
        chunk = nc - 1 - step

        @pl.when(step == 0)
        def _():
            dst_ref[...] = jnp.zeros_like(dst_ref)
            nxs_ref[...] = jnp.zeros_like(nxs_ref)
            nbc_ref[...] = jnp.zeros_like(nbc_ref)

        @pl.when((pl.program_id(0) == 0) & (step == 0))
        def _():
            gcw_ref[...] = jnp.zeros_like(gcw_ref)
            gcb_ref[...] = jnp.zeros_like(gcb_ref)
            gdtb_ref[...] = jnp.zeros_like(gdtb_ref)
            gal_ref[...] = jnp.zeros_like(gal_ref)
            gds_ref[...] = jnp.zeros_like(gds_ref)

        expand = _head_expand()
        collapse = lambda v: _dot01(v, expand, 2, _NT)
        first = jnp.where(chunk == 0, 0.0, 1.0)
        xs_raw = xs_ref[...]
        bc_raw = bc_ref[...]
        pxs = pxs_ref[...] * first
        pbc = pbc_ref[...] * first
        cw = cw_ref[...]
        cb = cb_ref[...]
        taps_xs = _conv_taps(xs_raw, pxs)
        taps_bc = _conv_taps(bc_raw, pbc)
        xs, dsilu_xs = _silu_and_grad(_conv_pre(taps_xs, cw[:, :D_BRANCH], cb[:, :D_BRANCH]))
        bc, dsilu_bc = _silu_and_grad(_conv_pre(taps_bc, cw[:, D_BRANCH:], cb[:, D_BRANCH:]))

        dt_in = dt_ref[...] + dtb_ref[...]
        dt, d_a, a_cs, dt_x, acs_x, tril = _chunk_decay(
            dt_ref[...], dtb_ref[...], al_ref[...], expand, qc)
        a_cst = a_cs.T
        aend_x = acs_x[qc - 1:qc, :]
        ea_x = jnp.exp(acs_x)
        dec_x = jnp.exp(aend_x - acs_x)
        xt = xs * dt_x
        xtb = xt.astype(BF16)
        xdb = (xt * dec_x).astype(BF16)
        d_x = _dot01(jnp.broadcast_to(ds_ref[...], (8, LANES)), expand, 3)[0:1, :]

        dy = dy_ref[...]
        dyb = dy.astype(BF16)
        dyeab = (dy * ea_x).astype(BF16)
        gds_ref[...] += collapse(jnp.broadcast_to(jnp.sum(dy * xs, axis=0, keepdims=True),
                                                  (8, D_BRANCH)))[0:1, :]

        d_bc = []
        d_cc = []
        y_offs = []
        dxt_states = []
        end_terms = []
        for g in range(N_GROUPS):
            gs = slice(g * GROUP_W, (g + 1) * GROUP_W)
            bg = bc[:, g * D_STATE:(g + 1) * D_STATE]
            cg = bc[:, (N_GROUPS + g) * D_STATE:(N_GROUPS + g + 1) * D_STATE]
            bgb = bg.astype(BF16)
            cgb = cg.astype(BF16)
            cbm = _dot(cgb, bgb, _NT)
            st_in = st_ref[0, 0, g]
            st_inb = st_in.astype(BF16)
            d_st = dst_ref[g]
            d_stb = d_st.astype(BF16)
            y_offs.append(_dot(cgb, st_inb) * ea_x[:, gs])
            dxt_states.append(_dot(bgb, d_stb) * dec_x[:, gs])
            d_c = _dot(dyeab[:, gs], st_inb, _NT)
            d_b = _dot(xdb[:, gs], d_stb, _NT)
            d_cb = jnp.zeros((qc, qc), F32)
            for k in range(HEADS_PER_GROUP):
                h = g * HEADS_PER_GROUP + k
                hs = slice(h * HEAD_DIM, (h + 1) * HEAD_DIM)
                seg = a_cs[:, h:h + 1] - a_cst[h:h + 1, :]
                lh = jnp.exp(jnp.where(tril, seg, -1e30))
                ghb = (cbm * lh).astype(BF16)
                d_cb = d_cb + _dot(dyb[:, hs], xtb[:, hs], _NT) * lh
                yd_ref[:, hs] = _dot(ghb, xtb[:, hs])
                dxt_ref[:, hs] = _dot(ghb, dyb[:, hs], _TN)
            d_cbb = d_cb.astype(BF16)
            d_cc.append(d_c + _dot(d_cbb, bgb))
            d_bc.append(d_b + _dot(d_cbb, cgb, _TN))
            end_terms.append(jnp.sum(d_st * stn_ref[0, 0, g], axis=0, keepdims=True))
            dst_ref[g] = d_st * jnp.exp(aend_x[:, gs]) + _dot(cg.T.astype(BF16), dyeab[:, gs])

        y_off = jnp.concatenate(y_offs, axis=1)
        dxt_state = jnp.concatenate(dxt_states, axis=1)
        dxt = dxt_ref[...] + dxt_state
        last = jnp.where(chunk == nc - 1, 0.0, 1.0)
        end_c = collapse(jnp.broadcast_to(jnp.concatenate(end_terms, axis=1), (8, D_BRANCH)))[0:1, :]
        da_cs = collapse(dyb.astype(F32) * yd_ref[...] - dxt_ref[...] * xtb.astype(F32)
                         + dy * y_off - dxt_state * xt)
        da_cs = da_cs + jnp.where(_row_iota(da_cs.shape) == qc - 1, end_c * last, 0.0)
        triu = lax.broadcasted_iota(jnp.int32, (qc, qc), 0) <= lax.broadcasted_iota(jnp.int32, (qc, qc), 1)
        dd_a = _dot01(da_cs, jnp.where(triu, 1.0, 0.0).astype(BF16), 3, m_left=True)
        ddt = dd_a * (-jnp.exp(al_ref[...])) + collapse(dxt * xs)
        head_lanes = _lane_iota(ddt.shape) < N_HEADS
        ddt_raw = jnp.where(head_lanes, ddt * _sigmoid(dt_in), 0.0)
        gal_ref[...] += jnp.sum(jnp.where(head_lanes, dd_a * d_a, 0.0), axis=0, keepdims=True)
        gdtb_ref[...] += jnp.sum(ddt_raw, axis=0, keepdims=True)

        dpre_xs = (dxt * dt_x + d_x * dy) * dsilu_xs
        dpre_bc = jnp.concatenate(d_bc + d_cc, axis=1) * dsilu_bc
        gcb_ref[...] += jnp.concatenate([jnp.sum(dpre_xs, axis=0, keepdims=True),
                                         jnp.sum(dpre_bc, axis=0, keepdims=True)], axis=1)
        nxs = nxs_ref[...]
        nbc = nbc_ref[...]
        du_xs = jnp.zeros_like(dpre_xs)
        du_bc = jnp.zeros_like(dpre_bc)
        for i in range(CONV_TAPS):
            k = CONV_TAPS - 1 - i
            gcw_ref[i:i + 1, :] += jnp.concatenate(
                [jnp.sum(dpre_xs * taps_xs[i], axis=0, keepdims=True),
                 jnp.sum(dpre_bc * taps_bc[i], axis=0, keepdims=True)], axis=1)
            du_xs = du_xs + _shift_up(dpre_xs, nxs, k) * cw[i:i + 1, :D_BRANCH]
            du_bc = du_bc + _shift_up(dpre_bc, nbc, k) * cw[i:i + 1, D_BRANCH:]
        nxs_ref[...] = dpre_xs
        nbc_ref[...] = dpre_bc

        dx_ref[:, :D_BRANCH] = du_xs.astype(BF16)
        dx_ref[:, D_BRANCH:D_CONV] = du_bc.astype(BF16)
        dx_ref[:, D_CONV:D_CONV + LANES] = ddt_raw.astype(BF16)
        dx_ref[:, D_CONV + LANES:] = jnp.zeros((qc, 2048 - D_CONV - LANES), BF16)

    rev = lambda b, c: b * nc + (nc - 1 - c)
    prv = lambda b, c: b * nc + jnp.maximum(nc - 2 - c, 0)
    nblk = lambda w, off, f: pl.BlockSpec((qc, w), lambda b, c: (f(b, c), off))
    full = lambda r, w: pl.BlockSpec((r, w), lambda b, c: (0, 0))
    st_spec = lambda f: pl.BlockSpec((1, 1, N_GROUPS, D_STATE, GROUP_W),
                                     lambda b, c: (b, f(c), 0, 0, 0))
    return _call_with_exchange(
        body, (proj, proj, proj, proj, proj, d_y, states, states, conv_w, conv_b, dtb, alog, dskip),
        slabs, (True,) * len(slabs), name="ssd_bwd", grid=(nb, nc),
        in_specs=[nblk(D_BRANCH, COL_XS // D_BRANCH, rev), nblk(D_BC, COL_BC // D_BC, rev),
                  nblk(LANES, COL_DT // LANES, rev),
                  nblk(D_BRANCH, COL_XS // D_BRANCH, prv), nblk(D_BC, COL_BC // D_BC, prv),
                  nblk(D_BRANCH, 0, rev),
                  st_spec(lambda c: nc - 1 - c), st_spec(lambda c: jnp.minimum(nc - c, nc - 1)),
                  full(CONV_TAPS, D_CONV), full(1, D_CONV), full(1, LANES), full(1, LANES),
                  full(1, LANES)],
        out_specs=[nblk(2048, 0, rev), full(8, D_CONV), full(1, D_CONV), full(1, LANES),
                   full(1, LANES), full(1, LANES)],
        out_shape=[jax.ShapeDtypeStruct((nb * seq, 2048), BF16),
                   jax.ShapeDtypeStruct((8, D_CONV), F32), jax.ShapeDtypeStruct((1, D_CONV), F32),
                   jax.ShapeDtypeStruct((1, LANES), F32), jax.ShapeDtypeStruct((1, LANES), F32),
                   jax.ShapeDtypeStruct((1, LANES), F32)],
        scratch_shapes=[pltpu.VMEM((N_GROUPS, D_STATE, GROUP_W), F32),
                        pltpu.VMEM((qc, D_BRANCH), F32), pltpu.VMEM((qc, D_BC), F32),
                        pltpu.VMEM((qc, D_BRANCH), F32), pltpu.VMEM((qc, D_BRANCH), F32)])


def _mid(o_sb, y_ssd, proj, x2, target, sb_w, ssd_w, w_out_b, tm):
    t = x2.shape[0]
    inv_d = 1.0 / D_MODEL

    def body(o_ref, y_ref, zsb_ref, zssd_ref, x_ref, tg_ref, sbw_ref, ssdw_ref, w_ref,
             dout_ref, dosb_ref, dy_ref, dz_ref, gw_ref, gsb_ref, gssd_ref, loss_ref):
        @pl.when(pl.program_id(0) == 0)
        def _():
            gw_ref[...] = jnp.zeros_like(gw_ref)
            gsb_ref[...] = jnp.zeros_like(gsb_ref)
            gssd_ref[...] = jnp.zeros_like(gssd_ref)
            loss_ref[...] = jnp.zeros_like(loss_ref)

        def branch(val, z, w):
            gate, dgate = _silu_and_grad(z)
            g = val * gate
            r = lax.rsqrt(jnp.mean(g * g, axis=1, keepdims=True) + EPS)
            xhat = g * r
            return (gate, dgate, r, xhat), (xhat * w).astype(BF16)

        o = o_ref[...]
        y = y_ref[...]
        saved_a, mix_a = branch(o, zsb_ref[...], sbw_ref[...])
        saved_b, mix_b = branch(y, zssd_ref[...], ssdw_ref[...])
        out = x_ref[...] + _dot(mix_a, w_ref[:D_BRANCH, :]) + _dot(mix_b, w_ref[D_BRANCH:, :])
        diff = out - tg_ref[...]
        loss_ref[...] += 0.5 * inv_d * jnp.sum(diff * diff)
        d_out = diff * inv_d
        dout_ref[...] = d_out
        d_outb = d_out.astype(BF16)
        gw_ref[:D_BRANCH, :] += _dot(mix_a, d_outb, _TN)
        gw_ref[D_BRANCH:, :] += _dot(mix_b, d_outb, _TN)

        def branch_bwd(dmix, val, w, saved):
            gate, dgate, r, xhat = saved
            gg = dmix * w
            m = jnp.mean(gg * xhat, axis=1, keepdims=True)
            dg = r * (gg - xhat * m)
            return dg * gate, dg * val * dgate, jnp.sum(dmix * xhat, axis=0, keepdims=True)

        dmix_a = _dot(d_outb, w_ref[:D_BRANCH, :], _NT)
        dmix_b = _dot(d_outb, w_ref[D_BRANCH:, :], _NT)
        d_o, dz_a, gsb = branch_bwd(dmix_a, o, sbw_ref[...], saved_a)
        d_y, dz_b, gssd = branch_bwd(dmix_b, y, ssdw_ref[...], saved_b)
        dosb_ref[...] = d_o
        dy_ref[...] = d_y
        dz_ref[:, :D_BRANCH] = dz_a.astype(BF16)
        dz_ref[:, D_BRANCH:] = dz_b.astype(BF16)
        gsb_ref[...] += gsb
        gssd_ref[...] += gssd

    row = lambda w, off: pl.BlockSpec((tm, w), lambda i: (i, off))
    full = lambda r, w: pl.BlockSpec((r, w), lambda i: (0, 0))
    resident = pl.BlockSpec((2 * D_BRANCH, D_MODEL), lambda i: (0, 0), pipeline_mode=pl.Buffered(1))
    tok = jax.ShapeDtypeStruct((t, D_MODEL), F32)
    return pl.pallas_call(
        body, name="mid",
        grid=(t // tm,),
        in_specs=[row(D_BRANCH, 0), row(D_BRANCH, 0), row(D_BRANCH, 3), row(D_BRANCH, 4),
                  row(D_MODEL, 0), row(D_MODEL, 0), full(1, D_BRANCH), full(1, D_BRANCH),
                  resident],
        out_specs=[row(D_MODEL, 0), row(D_BRANCH, 0), row(D_BRANCH, 0), row(2 * D_BRANCH, 0),
                   resident, full(1, D_BRANCH), full(1, D_BRANCH),
                   full(1, LANES)],
        out_shape=[tok, tok, tok, jax.ShapeDtypeStruct((t, 2 * D_BRANCH), BF16),
                   jax.ShapeDtypeStruct((2 * D_BRANCH, D_MODEL), F32),
                   jax.ShapeDtypeStruct((1, D_BRANCH), F32), jax.ShapeDtypeStruct((1, D_BRANCH), F32),
                   jax.ShapeDtypeStruct((1, LANES), F32)],
        compiler_params=_params(1),
    )(o_sb, y_ssd, proj, proj, x2, target, sb_w, ssd_w, w_out_b)


_DPROJ_FIRST = (0, 1, 2, 3, 5)
_DPROJ_BLOCKS = (1, 1, 1, 2, 2)


def _in_proj_bwd_x(d_parts, w_in_t, x2, d_out, norm_w, tm, slabs=()):
    t = x2.shape[0]
    n_parts = len(d_parts)

    def body(*refs):
        dp_refs = refs[:n_parts]
        w_ref, x_ref, dout_ref, nw_ref, gx_ref, gnw_ref = refs[n_parts:]

        @pl.when(pl.program_id(0) == 0)
        def _():
            gnw_ref[...] = jnp.zeros_like(gnw_ref)

        d_hn = None
        for p in range(n_parts):
            rows = slice(_DPROJ_FIRST[p] * COLBLK, (_DPROJ_FIRST[p] + _DPROJ_BLOCKS[p]) * COLBLK)
            term = _dot(dp_refs[p][...], w_ref[rows, :])
            d_hn = term if d_hn is None else d_hn + term
        xf = x_ref[...]
        r = lax.rsqrt(jnp.mean(xf * xf, axis=1, keepdims=True) + EPS)
        xhat = xf * r
        g = d_hn * nw_ref[...]
        m = jnp.mean(g * xhat, axis=1, keepdims=True)
        gx_ref[...] = dout_ref[...] + r * (g - xhat * m)
        gnw_ref[...] += jnp.sum(d_hn * xhat, axis=0, keepdims=True)

    row = lambda w: pl.BlockSpec((tm, w), lambda i: (i, 0))
    return _call_with_exchange(
        body, (*d_parts, w_in_t, x2, d_out, norm_w), slabs, (True,) * len(slabs),
        name="in_proj_bwd_x", grid=(t // tm,),
        in_specs=[row(COLBLK * _DPROJ_BLOCKS[p]) for p in range(n_parts)] + [
                  pl.BlockSpec((D_IN_PAD, D_MODEL), lambda i: (0, 0), pipeline_mode=pl.Buffered(1)),
                  row(D_MODEL), row(D_MODEL), pl.BlockSpec((1, D_MODEL), lambda i: (0, 0))],
        out_specs=[row(D_MODEL), pl.BlockSpec((1, D_MODEL), lambda i: (0, 0))],
        out_shape=[jax.ShapeDtypeStruct((t, D_MODEL), F32), jax.ShapeDtypeStruct((1, D_MODEL), F32)])


def _in_proj_bwd_w(hn, d_parts, tm):
    t = hn.shape[0]
    n_parts = len(d_parts)

    def body(hn_ref, *refs):
        dp_refs, gw_ref = refs[:n_parts], refs[n_parts]

        @pl.when(pl.program_id(0) == 0)
        def _():
            gw_ref[...] = jnp.zeros_like(gw_ref)

        hnt = hn_ref[...].astype(F32).T.astype(BF16)
        for p in range(n_parts):
            cols = slice(_DPROJ_FIRST[p] * COLBLK, (_DPROJ_FIRST[p] + _DPROJ_BLOCKS[p]) * COLBLK)
            gw_ref[:, cols] += _dot(hnt, dp_refs[p][...])

    return pl.pallas_call(
        body, name="in_proj_bwd_w",
        grid=(t // tm,),
        in_specs=[pl.BlockSpec((tm, D_MODEL), lambda i: (i, 0))]
                 + [pl.BlockSpec((tm, COLBLK * _DPROJ_BLOCKS[p]), lambda i: (i, 0))
                    for p in range(n_parts)],
        out_specs=pl.BlockSpec((D_MODEL, D_IN_PAD), lambda i: (0, 0), pipeline_mode=pl.Buffered(1)),
        out_shape=jax.ShapeDtypeStruct((D_MODEL, D_IN_PAD), F32),
        compiler_params=_params(1),
    )(hn, *d_parts)


def _adamw(parts, w, m, v, tr, name):
    _, rows, cols = w.shape
    c1 = 1.0 - ADAM_B1 ** ADAM_STEP
    c2 = 1.0 - ADAM_B2 ** ADAM_STEP

    def body(p_ref, w_ref, m_ref, v_ref, g_ref, d_ref, nm_ref, nv_ref):
        g = p_ref[0].astype(F32)
        for s in range(1, N_DEV):
            g = g + p_ref[s].astype(F32)
        nm = ADAM_B1 * m_ref[0] + (1.0 - ADAM_B1) * g
        nv = ADAM_B2 * v_ref[0] + (1.0 - ADAM_B2) * (g * g)
        g_ref[0] = g
        nm_ref[0] = nm
        nv_ref[0] = nv
        d_ref[0] = -ADAM_LR * ((nm / c1) / (jnp.sqrt(nv / c2) + ADAM_EPS) + ADAM_WD * w_ref[0])

    blk = pl.BlockSpec((1, tr, cols), lambda i: (0, i, 0))
    shape = jax.ShapeDtypeStruct((1, rows, cols), F32)
    return pl.pallas_call(
        body, name=name,
        grid=(rows // tr,),
        in_specs=[pl.BlockSpec((N_DEV, tr, cols), lambda i: (0, i, 0)), blk, blk, blk],
        out_specs=[blk, blk, blk, blk],
        out_shape=[shape, shape, shape, shape],
        compiler_params=_params(1),
    )(parts, w, m, v)


def _mesh_place():
    x, y, c = lax.axis_index("x"), lax.axis_index("y"), lax.axis_index("c")
    return x, y, c, 4 * x + 2 * y + c


def _peer(x, y, c, k):
    px = 1 - x if k & 4 else x
    py = 1 - y if k & 2 else y
    pc = 1 - c if k & 1 else c
    return (px, py, pc), 4 * px + 2 * py + pc


def _exchange(srcs, scatter, name):
    n = len(srcs)

    def body(*refs):
        copies = _exchange_copies(refs[:n], refs[n:2 * n], scatter, *refs[2 * n:])
        _exchange_start(copies)
        _exchange_wait(copies)

    return pl.pallas_call(
        body, name=name,
        in_specs=[_ANY] * n, out_specs=[_ANY] * n, out_shape=_exchange_shapes(srcs, scatter),
        scratch_shapes=_exchange_sems(n),
    )(*srcs)


def _call_with_exchange(body, operands, srcs, scatter, *, name, grid, in_specs, out_specs,
                        out_shape, scratch_shapes=()):
    n_in, n_out, n_scr, n_x = len(in_specs), len(out_specs), len(scratch_shapes), len(srcs)
    params = _params(len(grid))
    if not n_x:
        return pl.pallas_call(body, name=name, grid=grid, in_specs=list(in_specs),
                              out_specs=list(out_specs), out_shape=list(out_shape),
                              scratch_shapes=list(scratch_shapes), compiler_params=params)(*operands)

    def wrapped(*refs):
        ins, refs = refs[:n_in], refs[n_in:]
        x_src, refs = refs[:n_x], refs[n_x:]
        outs, refs = refs[:n_out], refs[n_out:]
        x_dst, refs = refs[:n_x], refs[n_x:]
        scratch, sems = refs[:n_scr], refs[n_scr:]
        ids = [pl.program_id(a) for a in range(len(grid))]
        first = functools.reduce(jnp.logical_and, [i == 0 for i in ids])
        last = functools.reduce(jnp.logical_and, [i == n - 1 for i, n in zip(ids, grid)])

        @pl.when(first)
        def _():
            _exchange_start(_exchange_copies(x_src, x_dst, scatter, *sems))

        body(*ins, *outs, *scratch)

        @pl.when(last)
        def _():
            _exchange_wait(_exchange_copies(x_src, x_dst, scatter, *sems))

    return pl.pallas_call(
        wrapped, name=name, grid=grid,
        in_specs=list(in_specs) + [_ANY] * n_x, out_specs=list(out_specs) + [_ANY] * n_x,
        out_shape=list(out_shape) + _exchange_shapes(srcs, scatter),
        scratch_shapes=list(scratch_shapes) + _exchange_sems(n_x), compiler_params=params,
    )(*operands, *srcs)


def _gather_two_level(shard, name):
    def body(x_ref, out_ref, send_sems, recv_sems, local_sem):
        x, y, c, me = _mesh_place()
        sibling = (x, y, 1 - c)
        chips = [(1 - x, y), (x, 1 - y), (1 - x, 1 - y)]

        def slab(px, py, pc):
            return out_ref.at[4 * px + 2 * py + pc]

        def copy(k, block, to, src=None):
            return pltpu.make_async_remote_copy(
                src_ref=slab(*block) if src is None else src, dst_ref=slab(*block),
                send_sem=send_sems.at[k], recv_sem=recv_sems.at[k],
                device_id=to, device_id_type=pl.DeviceIdType.MESH)

        mine = pltpu.make_async_copy(x_ref, slab(x, y, c), local_sem)
        mine.start()
        first = [copy(0, (x, y, c), sibling, src=x_ref)]
        first += [copy(1 + j, (x, y, c), (*chip, c), src=x_ref) for j, chip in enumerate(chips)]
        for cp in first:
            cp.start()
        passed = [copy(4 + j, (*chip, c), sibling) for j, chip in enumerate(chips)]
        for j, chip in enumerate(chips):
            copy(1 + j, (*chip, c), (x, y, c)).wait_recv()
            passed[j].start()
        copy(0, sibling, (x, y, c)).wait_recv()
        for j, chip in enumerate(chips):
            copy(4 + j, (*chip, 1 - c), (x, y, c)).wait_recv()
        for cp in first + passed:
            cp.wait_send()
        mine.wait()

    return pl.pallas_call(
        body, name=name,
        in_specs=[_ANY], out_specs=_ANY,
        out_shape=jax.ShapeDtypeStruct((N_DEV,) + shard.shape, shard.dtype),
        scratch_shapes=[pltpu.SemaphoreType.DMA((N_DEV - 1,)), pltpu.SemaphoreType.DMA((N_DEV - 1,)),
                        pltpu.SemaphoreType.DMA],
    )(shard)


_ANY = pl.BlockSpec(memory_space=pl.ANY)


def _exchange_shapes(srcs, scatter):
    return [jax.ShapeDtypeStruct(s.shape if sc else (N_DEV,) + s.shape, s.dtype)
            for s, sc in zip(srcs, scatter)]


def _exchange_sems(n):
    return [pltpu.SemaphoreType.DMA((n * (N_DEV - 1),)),
            pltpu.SemaphoreType.DMA((n * (N_DEV - 1),)),
            pltpu.SemaphoreType.DMA((n,))]


def _exchange_copies(src_refs, dst_refs, scatter, send_sems, recv_sems, loc_sems):
    n = len(src_refs)
    x, y, c, me = _mesh_place()

    def src_of(i, idx):
        return src_refs[i].at[idx] if scatter[i] else src_refs[i]

    local = [pltpu.make_async_copy(src_of(i, me), dst_refs[i].at[me], loc_sems.at[i])
             for i in range(n)]
    sends, recvs = [], []
    for k in range(1, N_DEV):
        peer, pidx = _peer(x, y, c, k)
        for i in range(n):
            s = i * (N_DEV - 1) + k - 1
            for dst_slab, group in ((me, sends), (pidx, recvs)):
                group.append(pltpu.make_async_remote_copy(
                    src_ref=src_of(i, pidx), dst_ref=dst_refs[i].at[dst_slab],
                    send_sem=send_sems.at[s], recv_sem=recv_sems.at[s],
                    device_id=peer, device_id_type=pl.DeviceIdType.MESH))
    return local, sends, recvs


def _exchange_start(copies):
    local, sends, _ = copies
    for cp in local + sends:
        cp.start()


def _exchange_wait(copies):
    local, sends, recvs = copies
    for cp in recvs:
        cp.wait_recv()
    for cp in sends:
        cp.wait_send()
    for cp in local:
        cp.wait()


def _pad_lanes(v, width=LANES):
    return jnp.pad(v, ((0, 0), (0, width - v.shape[1])))


def _local_step(x, target, norm_w, w_in_b, q_norm_w, k_norm_w, conv_w, conv_b, dt_bias, a_log,
                d_skip, sb_norm_w, ssd_norm_w, w_out_b, tm=256, tq=512, tmid=256, blk=ATT_BLK,
                scatter=False, w_in_t=None):
    nb, seq, _ = x.shape
    t = nb * seq
    x2 = x.reshape(t, D_MODEL)
    tg2 = target.reshape(t, D_MODEL)
    qw2 = jnp.tile(q_norm_w, (1, 2))
    kw2 = jnp.tile(k_norm_w, (1, 2))
    dtb, alog, dsk = _pad_lanes(dt_bias), _pad_lanes(a_log), _pad_lanes(d_skip)

    if w_in_t is None:
        w_in_t = w_in_b.T
    if scatter:
        proj, hn, wout_all, cw_all = _in_proj(x2, norm_w, w_in_b, tm, (w_out_b, conv_w))
        w_out_b = wout_all.reshape(2 * D_BRANCH, D_MODEL)
        conv_w = jnp.transpose(cw_all, (1, 0, 2)).reshape(CONV_TAPS, D_CONV)
    else:
        proj, hn = _in_proj(x2, norm_w, w_in_b, tm)
    qs, kn, vb, kt, ksq = _qk_prep(proj, qw2, kw2, nb, seq, tq)
    o_sb, sb_tot, sb_low = _attn_fwd(qs, kn, vb, jnp.max(ksq, axis=1), nb, seq, blk)
    y_ssd, states = _ssd_fwd(proj, conv_w, conv_b, dtb, alog, dsk, nb, seq)
    d_out, d_osb, d_y, d_z, g_wout, g_sbw, g_ssdw, loss = _mid(
        o_sb, y_ssd, proj, x2, tg2, sb_norm_w, ssd_norm_w, w_out_b, tmid)
    dqs, dkn, dvh = _attn_bwd(qs, kn, kt, vb, sb_tot, sb_low, d_osb, nb, seq, blk)
    dq_raw, dk_raw, dv_raw, g_qw, g_kw = _qk_bwd(proj, dqs, dkn, dvh, qw2, kw2, nb, seq, tq)
    wout_slabs = (g_wout.reshape(N_DEV, 2 * D_BRANCH // N_DEV, D_MODEL).astype(BF16),)
    d_xbc, g_cw, g_cb, g_dtb, g_alog, g_dsk, *moved = _ssd_bwd(
        proj, d_y, states, conv_w, conv_b, dtb, alog, dsk, nb, seq, wout_slabs if scatter else ())
    d_parts = [dq_raw, dk_raw, dv_raw, d_z, d_xbc]
    tall = min(2 * tm, t)
    g_win = _in_proj_bwd_w(hn, d_parts, tall)[:, :D_IN]
    g_cw = g_cw[:CONV_TAPS]
    if scatter:
        g_wout, = moved
        grad_x, g_nw, g_win, g_cw = _in_proj_bwd_x(
            d_parts, w_in_t, x2, d_out, norm_w, tall, _grad_slabs(g_win, g_cw))
    else:
        grad_x, g_nw = _in_proj_bwd_x(d_parts, w_in_t, x2, d_out, norm_w, tall)

    small = dict(
        norm_w=g_nw,
        q_norm_w=g_qw[:, :HEAD_DIM] + g_qw[:, HEAD_DIM:],
        k_norm_w=g_kw[:, :HEAD_DIM] + g_kw[:, HEAD_DIM:],
        conv_b=g_cb, dt_bias=g_dtb[:, :N_HEADS], A_log=g_alog[:, :N_HEADS],
        D_skip=g_dsk[:, :N_HEADS], sb_norm_w=g_sbw, ssd_norm_w=g_ssdw)
    return loss[0, 0], grad_x.reshape(nb, seq, D_MODEL), g_win, g_wout, g_cw, small


def _grad_slabs(g_win, g_cw):
    w_sh = D_IN // N_DEV
    c_sh = D_CONV // N_DEV
    return (jnp.transpose(g_win.reshape(D_MODEL, N_DEV, w_sh), (1, 0, 2)).astype(BF16),
            jnp.pad(jnp.transpose(g_cw.reshape(CONV_TAPS, N_DEV, c_sh), (1, 0, 2)),
                    ((0, 0), (0, 8 - CONV_TAPS), (0, 0))))


_SMALL = ("norm_w", "q_norm_w", "k_norm_w", "conv_b", "dt_bias", "A_log", "D_skip",
          "sb_norm_w", "ssd_norm_w")


def _pack_small(vals):
    flat = jnp.concatenate([_pad_lanes(vals[n], -(-vals[n].shape[1] // LANES) * LANES)
                            for n in _SMALL], axis=1)
    return jnp.pad(flat, ((0, 0), (0, 48 * LANES - flat.shape[1]))).reshape(48, LANES)


def _unpack_small(packed, like):
    out, r = {}, 0
    for n in _SMALL:
        width = like[n].shape[1]
        nr = -(-width // LANES)
        out[n] = packed[r:r + nr].reshape(1, nr * LANES)[:, :width]
        r += nr
    return out


def kernel(x, norm_w, w_in, q_norm_w, k_norm_w, conv_w, conv_b, dt_bias, A_log, D_skip, sb_norm_w, ssd_norm_w, w_out, loss_target, m_norm_w, m_w_in, m_q_norm_w, m_k_norm_w, m_conv_w, m_conv_b, m_dt_bias, m_A_log, m_D_skip, m_sb_norm_w, m_ssd_norm_w, m_w_out, v_norm_w, v_w_in, v_q_norm_w, v_k_norm_w, v_conv_w, v_conv_b, v_dt_bias, v_A_log, v_D_skip, v_sb_norm_w, v_ssd_norm_w, v_w_out):
    win_all = _gather_two_level(w_in[0].astype(BF16), "gather_w_in")
    w_in_b = jnp.pad(jnp.transpose(win_all, (1, 0, 2)).reshape(D_MODEL, D_IN),
                     ((0, 0), (0, D_IN_PAD - D_IN)))
    w_in_t = jnp.pad(jnp.transpose(win_all, (0, 2, 1)).reshape(D_IN, D_MODEL),
                     ((0, D_IN_PAD - D_IN), (0, 0)))

    loss, grad_x, win_parts, wout_parts, cw_parts, g_small = _local_step(
        x, loss_target, norm_w, w_in_b, q_norm_w, k_norm_w, conv_w[0], conv_b, dt_bias, A_log,
        D_skip, sb_norm_w, ssd_norm_w, w_out[0].astype(BF16), scatter=True, w_in_t=w_in_t)
    packed = _pack_small(g_small).at[-1, 0].set(loss)
    small_parts, = _exchange([packed], [False], "gather_small_grads")
    loss = jnp.sum(small_parts[:, -1, 0])

    small_w = dict(norm_w=norm_w, q_norm_w=q_norm_w, k_norm_w=k_norm_w, conv_b=conv_b,
                   dt_bias=dt_bias, A_log=A_log, D_skip=D_skip, sb_norm_w=sb_norm_w,
                   ssd_norm_w=ssd_norm_w)
    small_m = dict(norm_w=m_norm_w, q_norm_w=m_q_norm_w, k_norm_w=m_k_norm_w, conv_b=m_conv_b,
                   dt_bias=m_dt_bias, A_log=m_A_log, D_skip=m_D_skip, sb_norm_w=m_sb_norm_w,
                   ssd_norm_w=m_ssd_norm_w)
    small_v = dict(norm_w=v_norm_w, q_norm_w=v_q_norm_w, k_norm_w=v_k_norm_w, conv_b=v_conv_b,
                   dt_bias=v_dt_bias, A_log=v_A_log, D_skip=v_D_skip, sb_norm_w=v_sb_norm_w,
                   ssd_norm_w=v_ssd_norm_w)

    pad8 = lambda a: jnp.pad(a, ((0, 0), (0, 8 - CONV_TAPS), (0, 0)))
    r_win = _adamw(win_parts, w_in, m_w_in, v_w_in, 128, "adamw_w_in")
    r_wout = _adamw(wout_parts, w_out, m_w_out, v_w_out, 128, "adamw_w_out")
    r_cw = _adamw(cw_parts, pad8(conv_w), pad8(m_conv_w), pad8(v_conv_w), 8, "adamw_conv_w")
    r_small = _adamw(small_parts, _pack_small(small_w)[None], _pack_small(small_m)[None],
                     _pack_small(small_v)[None], 48, "adamw_small")

    res = {"w_in": r_win, "w_out": r_wout, "conv_w": [a[:, :CONV_TAPS] for a in r_cw]}
    unpacked = [_unpack_small(a[0], small_w) for a in r_small]
    for n in _SMALL:
        res[n] = [u[n] for u in unpacked]
    order = ("norm_w", "w_in", "q_norm_w", "k_norm_w", "conv_w", "conv_b", "dt_bias", "A_log",
             "D_skip", "sb_norm_w", "ssd_norm_w", "w_out")
    outs = [loss, grad_x]
    for kind in range(4):
        outs += [res[n][kind] for n in order]
    return tuple(outs)
```

```python
import functools
import math

import jax
import jax.numpy as jnp
from jax import lax
from jax.experimental import pallas as pl
from jax.experimental.pallas import tpu as pltpu

F32 = jnp.float32
BF16 = jnp.bfloat16

D_MODEL = 1024
N_HEADS = 16
HEAD_DIM = 64
N_PAIRS = N_HEADS // 2
D_BRANCH = 1024
N_GROUPS = 2
HEADS_PER_GROUP = 8
D_STATE = 128
GROUP_W = HEADS_PER_GROUP * HEAD_DIM
D_BC = 2 * N_GROUPS * D_STATE
D_CONV = D_BRANCH + D_BC
D_IN = 6672
COLBLK = 1024
D_IN_PAD = 7168
N_COLBLK = D_IN_PAD // COLBLK
COL_XS = 5120
COL_BC = 6144
COL_DT = 6656
EPS = 1e-6
CONV_TAPS = 4
N_DEV = 8

LANES = 128
SSD_CHUNK = 128
ATT_BLK = 256
ATT_HEADS = 4
ATT_W = ATT_HEADS * HEAD_DIM
N_ATT_GROUPS = N_HEADS // ATT_HEADS
EXP_UNDERFLOW = -105.0
VMEM_LIMIT = 56 * 1024 * 1024

ADAM_LR = 0.001
ADAM_B1 = 0.9
ADAM_B2 = 0.999
ADAM_EPS = 1e-08
ADAM_WD = 0.01
ADAM_STEP = 10

_NT = (((1,), (1,)), ((), ()))
_TN = (((0,), (0,)), ((), ()))


def _params(n_grid):
    return pltpu.CompilerParams(dimension_semantics=("arbitrary",) * n_grid,
                                vmem_limit_bytes=VMEM_LIMIT)


def _dot(a, b, dims=None):
    if dims is None:
        return jnp.dot(a, b, preferred_element_type=F32)
    return lax.dot_general(a, b, dims, preferred_element_type=F32)


def _sigmoid(x):
    return 1.0 / (1.0 + jnp.exp(-x))


def _softplus(x):
    return jnp.maximum(x, 0.0) + jnp.log(1.0 + jnp.exp(-jnp.abs(x)))


def _split_bf16(x):
    hi = x.astype(BF16)
    lo = (x - hi.astype(F32)).astype(BF16)
    return hi, lo


def _lane_iota(shape):
    return lax.broadcasted_iota(jnp.int32, shape, len(shape) - 1)


def _row_iota(shape):
    return lax.broadcasted_iota(jnp.int32, shape, len(shape) - 2)


def _pair_sum(x):
    r = lax.broadcasted_iota(jnp.int32, (LANES, LANES), 0)
    c = lax.broadcasted_iota(jnp.int32, (LANES, LANES), 1)
    same_head = jnp.where(r // HEAD_DIM == c // HEAD_DIM, 1.0, 0.0).astype(BF16)
    hi, lo = _split_bf16(x)
    return _dot(hi, same_head) + _dot(lo, same_head)


def _head_lanes(x, a):
    lane = _lane_iota(x.shape)
    mine = (lane >= a * HEAD_DIM) & (lane < (a + 1) * HEAD_DIM)
    return jnp.where(mine, x, jnp.zeros_like(x))


def _head_expand():
    r = lax.broadcasted_iota(jnp.int32, (LANES, D_BRANCH), 0)
    c = lax.broadcasted_iota(jnp.int32, (LANES, D_BRANCH), 1)
    return jnp.where(c // HEAD_DIM == r, 1.0, 0.0).astype(BF16)


def _in_proj(x2, norm_w, w_in_b, tm, shards=()):
    t = x2.shape[0]

    def body(x_ref, nw_ref, w_ref, proj_ref, hn_ref):
        xf = x_ref[...]
        r = lax.rsqrt(jnp.mean(xf * xf, axis=1, keepdims=True) + EPS)
        hn = (xf * r * nw_ref[...]).astype(BF16)
        hn_ref[...] = hn
        for j in range(N_COLBLK):
            cols = slice(j * COLBLK, (j + 1) * COLBLK)
            proj_ref[:, cols] = _dot(hn, w_ref[:, cols])

    return _call_with_exchange(
        body, (x2, norm_w, w_in_b), shards, (False,) * len(shards), name="in_proj",
        grid=(t // tm,),
        in_specs=[pl.BlockSpec((tm, D_MODEL), lambda i: (i, 0)),
                  pl.BlockSpec((1, D_MODEL), lambda i: (0, 0)),
                  pl.BlockSpec((D_MODEL, D_IN_PAD), lambda i: (0, 0), pipeline_mode=pl.Buffered(1))],
        out_specs=[pl.BlockSpec((tm, D_IN_PAD), lambda i: (i, 0)),
                   pl.BlockSpec((tm, D_MODEL), lambda i: (i, 0))],
        out_shape=[jax.ShapeDtypeStruct((t, D_IN_PAD), F32),
                   jax.ShapeDtypeStruct((t, D_MODEL), BF16)])


def _qk_prep(proj, qw2, kw2, nb, seq, tq):
    nl = seq // tq
    scale = 1.0 / math.sqrt(HEAD_DIM)

    def body(q_ref, k_ref, v_ref, qw_ref, kw_ref, qs_ref, kn_ref, vb_ref, kt_ref, ksq_ref):
        def norm(x, w):
            r = lax.rsqrt(_pair_sum(x * x) * (1.0 / HEAD_DIM) + EPS)
            return x * r * w

        vb_ref[...] = v_ref[...].astype(BF16)
        for p in range(N_PAIRS):
            cols = slice(p * LANES, (p + 1) * LANES)
            kn = norm(k_ref[:, cols], kw_ref[...])
            knb = kn.astype(BF16)
            qs_ref[:, cols] = (norm(q_ref[:, cols], qw_ref[...]) * scale).astype(BF16)
            kn_ref[:, cols] = knb
            kt_ref[0, p] = kn.T.astype(BF16)
            kf = knb.astype(F32)
            ksq_ref[0, 0, p:p + 1, :] = jnp.max(_pair_sum(kf * kf), axis=0, keepdims=True) * 1.0001

    tok_shape = jax.ShapeDtypeStruct((nb * seq, D_BRANCH), BF16)
    tok = lambda blk: pl.BlockSpec((tq, D_BRANCH), lambda b, i: (b * nl + i, blk))
    vec = pl.BlockSpec((1, LANES), lambda b, i: (0, 0))
    return pl.pallas_call(
        body, name="qk_prep",
        grid=(nb, nl),
        in_specs=[tok(0), tok(1), tok(2), vec, vec],
        out_specs=[tok(0), tok(0), tok(0),
                   pl.BlockSpec((1, N_PAIRS, LANES, tq), lambda b, i: (b, 0, 0, i)),
                   pl.BlockSpec((1, 1, N_PAIRS, LANES), lambda b, i: (b, i, 0, 0))],
        out_shape=[tok_shape, tok_shape, tok_shape,
                   jax.ShapeDtypeStruct((nb, N_PAIRS, LANES, seq), BF16),
                   jax.ShapeDtypeStruct((nb, nl, N_PAIRS, LANES), F32)],
        compiler_params=_params(2),
    )(proj, proj, proj, qw2, kw2)


def _attn_fwd(qs, kn, vb, ksq, nb, seq, blk):
    nq = seq // blk

    def body(q_ref, k_ref, v_ref, ksq_ref, o_ref, tot_ref, low_ref):
        qi = pl.program_id(2)
        r_i = lax.broadcasted_iota(jnp.int32, (blk, blk), 0)
        c_i = lax.broadcasted_iota(jnp.int32, (blk, blk), 1)
        csum = jnp.where(r_i >= c_i, 1.0, 0.0).astype(BF16)
        heads = range(ATT_HEADS)
        head = _head_lanes

        q_blk = q_ref[...]
        qf = q_blk.astype(F32)
        q_head = [head(q_blk, a) for a in heads]
        zmax = []
        for a in heads:
            qsq = jnp.sum(head(qf * qf, a), axis=1, keepdims=True)
            kmax = ksq_ref[0, 0, a // 2:a // 2 + 1, (a % 2) * HEAD_DIM:(a % 2) * HEAD_DIM + 1]
            zmax.append(1.01 * jnp.sqrt(qsq * kmax) + 0.01)

        def exhausted(run):
            top = functools.reduce(jnp.maximum, [jnp.max(run[a] + zmax[a]) for a in heads])
            return top < EXP_UNDERFLOW

        def sweep(blocks, run, acc):
            half = blk // 2

            def tiles(diag):
                return [(0, half, half), (half, half, blk)] if diag else [(0, blk, blk)]

            def keep(x, r0, diag):
                if diag:
                    rows = lax.broadcasted_iota(jnp.int32, x.shape, 0) + r0
                    x = jnp.where(lax.broadcasted_iota(jnp.int32, x.shape, 1) < rows, x, 0.0)
                return x

            offs = [pl.multiple_of(j * blk, blk) for j, _, _ in blocks]
            z = [[[_dot(q_head[a][r0:r0 + nr], k_ref[pl.ds(off, nk), :], _NT) for a in heads]
                  for r0, nr, nk in tiles(diag)] for (_, diag, _), off in zip(blocks, offs)]
            cl = [[[_dot(keep(-_softplus(zt[a]), r0, diag).astype(BF16), csum[:nk, :nk])
                    for a in heads]
                   for (r0, nr, nk), zt in zip(tiles(diag), zb)]
                  for (_, diag, _), zb in zip(blocks, z)]
            for (_, diag, valid), zb, clb, off in zip(blocks, z, cl, offs):
                rows_out, steps = [], [[] for _ in heads]
                for (r0, nr, nk), zt, clt in zip(tiles(diag), zb, clb):
                    v_blk = v_ref[pl.ds(off, nk), :]
                    part = None
                    for a in heads:
                        wa = keep(jnp.exp(zt[a] + clt[a] + run[a][r0:r0 + nr]), r0, diag)
                        term = _dot(wa.astype(BF16), head(v_blk, a))
                        part = term if part is None else part + term
                        steps[a].append(clt[a][:, 0:1])
                    rows_out.append(part)
                part = jnp.concatenate(rows_out, axis=0)
                steps = [jnp.concatenate(steps[a], axis=0) for a in heads]
                if valid is not None:
                    part = jnp.where(valid, part, 0.0)
                    steps = [jnp.where(valid, s, 0.0) for s in steps]
                acc = acc + part
                run = [run[a] + steps[a] for a in heads]
            return run, acc

        run = [jnp.zeros((blk, 1), F32)] * ATT_HEADS
        acc = jnp.zeros((blk, ATT_W), F32)
        run, acc = sweep([(qi, True, None), (jnp.maximum(qi - 1, 0), False, qi >= 1)], run, acc)
        low = jnp.maximum(qi - 1, 0)

        def more(carry):
            low, done, _, _ = carry
            return (low > 0) & jnp.logical_not(done)

        def pair(carry):
            low, _, run, acc = carry
            run, acc = sweep([(low - 1, False, None), (jnp.maximum(low - 2, 0), False, low >= 2)],
                             run, acc)
            return jnp.maximum(low - 2, 0), exhausted(run), run, acc

        low, _, run, acc = lax.while_loop(more, pair, (low, exhausted(run), run, acc))
        low_ref[pl.program_id(0) * N_ATT_GROUPS + pl.program_id(1), qi] = low.astype(F32)
        o_ref[...] = acc
        for a in heads:
            as_row = jnp.sum(jnp.where(r_i == c_i, run[a], 0.0), axis=0, keepdims=True)
            tot_ref[0, a, 0] = jnp.broadcast_to(as_row, (8, blk))

    return pl.pallas_call(
        body, name="sb_attn_fwd",
        grid=(nb, N_ATT_GROUPS, nq),
        in_specs=[pl.BlockSpec((blk, ATT_W), lambda b, h, i: (b * nq + i, h)),
                  pl.BlockSpec((seq, ATT_W), lambda b, h, i: (b, h)),
                  pl.BlockSpec((seq, ATT_W), lambda b, h, i: (b, h)),
                  pl.BlockSpec((1, 1, ATT_HEADS // 2, LANES), lambda b, h, i: (b, h, 0, 0))],
        out_specs=[pl.BlockSpec((blk, ATT_W), lambda b, h, i: (b * nq + i, h)),
                   pl.BlockSpec((1, ATT_HEADS, 1, 8, blk), lambda b, h, i: (b, h, i, 0, 0)),
                   pl.BlockSpec(memory_space=pltpu.SMEM)],
        out_shape=[jax.ShapeDtypeStruct((nb * seq, D_BRANCH), F32),
                   jax.ShapeDtypeStruct((nb, N_HEADS, nq, 8, blk), F32),
                   jax.ShapeDtypeStruct((nb * N_ATT_GROUPS, nq), F32)],
        compiler_params=_params(3),
    )(qs, kn, vb, ksq.reshape(nb, N_ATT_GROUPS, ATT_HEADS // 2, LANES))


def _attn_bwd(qs, kn, kt, vb, tot, low, d_o, nb, seq, blk):
    nq = seq // blk

    def body(q_ref, k_ref, kt_ref, v_ref, tot_ref, low_ref, do_ref, dq_ref, dk_ref, dv_ref):
        qi = pl.program_id(2)

        @pl.when(qi == 0)
        def _():
            dk_ref[...] = jnp.zeros_like(dk_ref)
            dv_ref[...] = jnp.zeros_like(dv_ref)

        r_i = lax.broadcasted_iota(jnp.int32, (blk, blk), 0)
        c_i = lax.broadcasted_iota(jnp.int32, (blk, blk), 1)
        before = jnp.where(c_i < r_i, 1.0, 0.0).astype(BF16)
        upto = jnp.where(c_i <= r_i, 1.0, 0.0).astype(BF16)
        causal = r_i < c_i

        heads = range(ATT_HEADS)
        q_head = [_head_lanes(q_ref[...], a) for a in heads]
        d_ob = [_head_lanes(do_ref[...].astype(BF16), a) for a in heads]
        total = [tot_ref[0, a, 0][0:1, :] for a in heads]

        def sweep(blocks, lsum, esum, dqt):
            def keep(x, diag):
                return jnp.where(causal, x, 0.0) if diag else x

            def there(x, valid):
                return x if valid is None else jnp.where(valid, x, 0.0)

            offs = [pl.multiple_of(j * blk, blk) for j, _, _ in blocks]
            zt = [[_dot(k_ref[pl.ds(off, blk), :], q_head[a], _NT) for a in heads]
                  for off in offs]
            dwt = [[_dot(v_ref[pl.ds(off, blk), :], d_ob[a], _NT) for a in heads]
                   for off in offs]
            sp, lk, lpre = [], [], []
            for (_, diag, _), ztb in zip(blocks, zt):
                sp.append([_softplus(ztb[a]) for a in heads])
                lk.append([keep(-sp[-1][a], diag).astype(BF16) for a in heads])
                lpre.append([_dot(before, lk[-1][a]) for a in heads])
            wt, et, epre = [], [], []
            for i, (_, diag, valid) in enumerate(blocks):
                wt.append([keep(jnp.exp(zt[i][a] + (total[a] - lsum[a] - lpre[i][a])), diag)
                           for a in heads])
                et.append([dwt[i][a] * wt[i][a] for a in heads])
                epre.append([_dot(upto, et[i][a].astype(BF16)) for a in heads])
                lsum = [lsum[a] + there(lpre[i][a][blk - 1:blk, :] + lk[i][a][blk - 1:blk, :], valid)
                        for a in heads]
            for i, (_, diag, valid) in enumerate(blocks):
                dzb = [keep(et[i][a] - jnp.exp(zt[i][a] - sp[i][a]) * (esum[a] + epre[i][a]),
                            diag).astype(BF16) for a in heads]
                esum = [esum[a] + there(epre[i][a][blk - 1:blk, :], valid) for a in heads]
                dk_ref[pl.ds(offs[i], blk), :] += there(functools.reduce(
                    jnp.add, [_dot(dzb[a], q_head[a]) for a in heads]), valid)
                dv_ref[pl.ds(offs[i], blk), :] += there(functools.reduce(
                    jnp.add, [_dot(wt[i][a].astype(BF16), d_ob[a]) for a in heads]), valid)
                dqt = [dqt[a] + there(_dot(
                    kt_ref[0, a // 2, (a % 2) * HEAD_DIM:(a % 2 + 1) * HEAD_DIM,
                           pl.ds(offs[i], blk)], dzb[a]), valid) for a in heads]
            return lsum, esum, dqt

        row = [jnp.zeros((1, blk), F32)] * ATT_HEADS
        dqt = [jnp.zeros((HEAD_DIM, blk), F32)] * ATT_HEADS
        low = low_ref[pl.program_id(0) * N_ATT_GROUPS + pl.program_id(1), qi].astype(jnp.int32)
        low = jnp.clip(low, 0, jnp.maximum(qi - 1, 0))

        def pair(carry):
            j, lsum, esum, dqt = carry
            return (j + 2,) + sweep([(j, False, None), (j + 1, False, j + 1 < qi - 1)],
                                    lsum, esum, dqt)

        _, lsum, esum, dqt = lax.while_loop(lambda c: c[0] < qi - 1, pair, (low, row, row, dqt))
        _, _, dqt = sweep([(jnp.maximum(qi - 1, 0), False, qi >= 1), (qi, True, None)],
                          lsum, esum, dqt)
        dq_ref[...] = jnp.concatenate(dqt, axis=0).T

    seq_blk = pl.BlockSpec((seq, ATT_W), lambda b, h, i: (b, h))
    tok = pl.BlockSpec((blk, ATT_W), lambda b, h, i: (b * nq + i, h))
    tok_shape = jax.ShapeDtypeStruct((nb * seq, D_BRANCH), F32)
    return pl.pallas_call(
        body, name="sb_attn_bwd",
        grid=(nb, N_ATT_GROUPS, nq),
        in_specs=[tok, seq_blk,
                  pl.BlockSpec((1, ATT_HEADS // 2, LANES, seq), lambda b, h, i: (b, h, 0, 0)),
                  seq_blk,
                  pl.BlockSpec((1, ATT_HEADS, 1, 8, blk), lambda b, h, i: (b, h, i, 0, 0)),
                  pl.BlockSpec(memory_space=pltpu.SMEM),
                  tok],
        out_specs=[tok, seq_blk, seq_blk],
        out_shape=[tok_shape, tok_shape, tok_shape],
        compiler_params=_params(3),
    )(qs, kn, kt, vb, tot, low, d_o)


def _qk_bwd(proj, dqs, dkn, dvh, qw2, kw2, nb, seq, tq):
    nl = seq // tq
    scale = 1.0 / math.sqrt(HEAD_DIM)

    def body(q_ref, k_ref, dq_ref, dk_ref, dv_ref, qw_ref, kw_ref,
             dqr_ref, dkr_ref, dvr_ref, gq_ref, gk_ref):
        @pl.when((pl.program_id(0) == 0) & (pl.program_id(1) == 0))
        def _():
            gq_ref[...] = jnp.zeros_like(gq_ref)
            gk_ref[...] = jnp.zeros_like(gk_ref)

        def norm_bwd(x, w, dy):
            r = lax.rsqrt(_pair_sum(x * x) * (1.0 / HEAD_DIM) + EPS)
            xhat = x * r
            g = dy * w
            m = _pair_sum(g * xhat) * (1.0 / HEAD_DIM)
            return r * (g - xhat * m), jnp.sum(dy * xhat, axis=0, keepdims=True)

        dvr_ref[...] = dv_ref[...].astype(BF16)
        gq = jnp.zeros((1, LANES), F32)
        gk = jnp.zeros((1, LANES), F32)
        for p in range(N_PAIRS):
            cols = slice(p * LANES, (p + 1) * LANES)
            dqr, gq_p = norm_bwd(q_ref[:, cols], qw_ref[...], dq_ref[:, cols] * scale)
            dkr, gk_p = norm_bwd(k_ref[:, cols], kw_ref[...], dk_ref[:, cols])
            dqr_ref[:, cols] = dqr.astype(BF16)
            dkr_ref[:, cols] = dkr.astype(BF16)
            gq, gk = gq + gq_p, gk + gk_p
        gq_ref[...] += gq
        gk_ref[...] += gk

    tok = lambda blk: pl.BlockSpec((tq, D_BRANCH), lambda b, i: (b * nl + i, blk))
    vec = pl.BlockSpec((1, LANES), lambda b, i: (0, 0))
    tshape = jax.ShapeDtypeStruct((nb * seq, D_BRANCH), BF16)
    return pl.pallas_call(
        body, name="qk_bwd",
        grid=(nb, nl),
        in_specs=[tok(0), tok(1), tok(0), tok(0), tok(0), vec, vec],
        out_specs=[tok(0), tok(0), tok(0), vec, vec],
        out_shape=[tshape, tshape, tshape,
                   jax.ShapeDtypeStruct((1, LANES), F32), jax.ShapeDtypeStruct((1, LANES), F32)],
        compiler_params=_params(2),
    )(proj, proj, dqs, dkn, dvh, qw2, kw2)


def _shift_down(cur, prev, k):
    if k == 0:
        return cur
    rows = _row_iota(cur.shape)
    return jnp.where(rows < k, pltpu.roll(prev, k, axis=0), pltpu.roll(cur, k, axis=0))


def _shift_up(cur, nxt, k):
    if k == 0:
        return cur
    n = cur.shape[0]
    rows = _row_iota(cur.shape)
    return jnp.where(rows < n - k, pltpu.roll(cur, n - k, axis=0), pltpu.roll(nxt, n - k, axis=0))


def _conv_taps(cur, prev):
    return [_shift_down(cur, prev, CONV_TAPS - 1 - i) for i in range(CONV_TAPS)]


def _conv_pre(taps, w, b):
    out = b
    for i in range(CONV_TAPS):
        out = out + taps[i] * w[i:i + 1, :]
    return out


def _silu(x):
    return x * _sigmoid(x)


def _silu_and_grad(x):
    s = _sigmoid(x)
    return x * s, s * (1.0 + x * (1.0 - s))


def _dot01(x, m01, parts, dims=None, m_left=False):
    total, rest = None, x
    for i in range(parts):
        piece = rest.astype(BF16)
        if i + 1 < parts:
            rest = rest - piece.astype(F32)
        term = _dot(m01, piece, dims) if m_left else _dot(piece, m01, dims)
        total = term if total is None else total + term
    return total


def _chunk_decay(dt_raw, dtb, alog, expand, qc):
    dt = _softplus(dt_raw + dtb)
    d_a = dt * (-jnp.exp(alog))
    r_i = lax.broadcasted_iota(jnp.int32, (qc, qc), 0)
    c_i = lax.broadcasted_iota(jnp.int32, (qc, qc), 1)
    tril = r_i >= c_i
    a_cs = _dot01(d_a, jnp.where(tril, 1.0, 0.0).astype(BF16), 3, m_left=True)
    dt_x = _dot01(dt, expand, 3)
    acs_x = _dot01(a_cs, expand, 3)
    return dt, d_a, a_cs, dt_x, acs_x, tril


def _ssd_fwd(proj, conv_w, conv_b, dtb, alog, dskip, nb, seq):
    qc = SSD_CHUNK
    nc = seq // qc

    def body(xs_ref, bc_ref, dt_ref, cw_ref, cb_ref, dtb_ref, al_ref, ds_ref,
             y_ref, st_ref, pxs_ref, pbc_ref, state_ref):
        @pl.when(pl.program_id(1) == 0)
        def _():
            pxs_ref[...] = jnp.zeros_like(pxs_ref)
            pbc_ref[...] = jnp.zeros_like(pbc_ref)
            state_ref[...] = jnp.zeros_like(state_ref)

        expand = _head_expand()
        xs_raw = xs_ref[...]
        bc_raw = bc_ref[...]
        cw = cw_ref[...]
        cb = cb_ref[...]
        xs = _silu(_conv_pre(_conv_taps(xs_raw, pxs_ref[...]), cw[:, :D_BRANCH], cb[:, :D_BRANCH]))
        bc = _silu(_conv_pre(_conv_taps(bc_raw, pbc_ref[...]), cw[:, D_BRANCH:], cb[:, D_BRANCH:]))
        pxs_ref[...] = xs_raw
        pbc_ref[...] = bc_raw

        dt, d_a, a_cs, dt_x, acs_x, tril = _chunk_decay(
            dt_ref[...], dtb_ref[...], al_ref[...], expand, qc)
        a_cst = a_cs.T
        aend_x = acs_x[qc - 1:qc, :]
        ea_x = jnp.exp(acs_x)
        dec_x = jnp.exp(aend_x - acs_x)
        xt = xs * dt_x
        xtb = xt.astype(BF16)
        xdb = (xt * dec_x).astype(BF16)
        d_x = _dot01(jnp.broadcast_to(ds_ref[...], (8, LANES)), expand, 3)[0:1, :]
        st_ref[0, 0] = state_ref[...]

        for g in range(N_GROUPS):
            gs = slice(g * GROUP_W, (g + 1) * GROUP_W)
            bg = bc[:, g * D_STATE:(g + 1) * D_STATE]
            cg = bc[:, (N_GROUPS + g) * D_STATE:(N_GROUPS + g + 1) * D_STATE]
            bgb = bg.astype(BF16)
            cgb = cg.astype(BF16)
            cbm = _dot(cgb, bgb, _NT)
            st_in = state_ref[g]
            y_off = _dot(cgb, st_in.astype(BF16)) * ea_x[:, gs]
            for k in range(HEADS_PER_GROUP):
                h = g * HEADS_PER_GROUP + k
                hs = slice(h * HEAD_DIM, (h + 1) * HEAD_DIM)
                seg = a_cs[:, h:h + 1] - a_cst[h:h + 1, :]
                gh = cbm * jnp.exp(jnp.where(tril, seg, -1e30))
                y_h = _dot(gh.astype(BF16), xtb[:, hs]) + y_off[:, k * HEAD_DIM:(k + 1) * HEAD_DIM]
                y_ref[:, hs] = y_h + d_x[:, hs] * xs[:, hs]
            state_ref[g] = st_in * jnp.exp(aend_x[:, gs]) + _dot(bg.T.astype(BF16), xdb[:, gs])

    nblk = lambda w, off: pl.BlockSpec((qc, w), lambda b, c: (b * nc + c, off))
    full = lambda r, w: pl.BlockSpec((r, w), lambda b, c: (0, 0))
    return pl.pallas_call(
        body, name="ssd_fwd",
        grid=(nb, nc),
        in_specs=[nblk(D_BRANCH, COL_XS // D_BRANCH), nblk(D_BC, COL_BC // D_BC),
                  nblk(LANES, COL_DT // LANES),
                  full(CONV_TAPS, D_CONV), full(1, D_CONV), full(1, LANES), full(1, LANES),
                  full(1, LANES)],
        out_specs=[pl.BlockSpec((qc, D_BRANCH), lambda b, c: (b * nc + c, 0)),
                   pl.BlockSpec((1, 1, N_GROUPS, D_STATE, GROUP_W), lambda b, c: (b, c, 0, 0, 0))],
        out_shape=[jax.ShapeDtypeStruct((nb * seq, D_BRANCH), F32),
                   jax.ShapeDtypeStruct((nb, nc, N_GROUPS, D_STATE, GROUP_W), F32)],
        scratch_shapes=[pltpu.VMEM((qc, D_BRANCH), F32), pltpu.VMEM((qc, D_BC), F32),
                        pltpu.VMEM((N_GROUPS, D_STATE, GROUP_W), F32)],
        compiler_params=_params(2),
    )(proj, proj, proj, conv_w, conv_b, dtb, alog, dskip)


def _ssd_bwd(proj, d_y, states, conv_w, conv_b, dtb, alog, dskip, nb, seq, slabs=()):
    qc = SSD_CHUNK
    nc = seq // qc

    def body(xs_ref, bc_ref, dt_ref, pxs_ref, pbc_ref, dy_ref, st_ref, stn_ref,
             cw_ref, cb_ref, dtb_ref, al_ref, ds_ref,
             dx_ref, gcw_ref, gcb_ref, gdtb_ref, gal_ref, gds_ref,
             dst_ref, nxs_ref, nbc_ref, yd_ref, dxt_ref):
        step = pl.program_id(1)
        chunk = nc - 1 - step

        @pl.when(step == 0)
        def _():
            dst_ref[...] = jnp.zeros_like(dst_ref)
            nxs_ref[...] = jnp.zeros_like(nxs_ref)
            nbc_ref[...] = jnp.zeros_like(nbc_ref)

        @pl.when((pl.program_id(0) == 0) & (step == 0))
        def _():
            gcw_ref[...] = jnp.zeros_like(gcw_ref)
            gcb_ref[...] = jnp.zeros_like(gcb_ref)
            gdtb_ref[...] = jnp.zeros_like(gdtb_ref)
            gal_ref[...] = jnp.zeros_like(gal_ref)
            gds_ref[...] = jnp.zeros_like(gds_ref)

        expand = _head_expand()
        collapse = lambda v: _dot01(v, expand, 2, _NT)
        first = jnp.where(chunk == 0, 0.0, 1.0)
        xs_raw = xs_ref[...]
        bc_raw = bc_ref[...]
        pxs = pxs_ref[...] * first
        pbc = pbc_ref[...] * first
        cw = cw_ref[...]
        cb = cb_ref[...]
        taps_xs = _conv_taps(xs_raw, pxs)
        taps_bc = _conv_taps(bc_raw, pbc)
        xs, dsilu_xs = _silu_and_grad(_conv_pre(taps_xs, cw[:, :D_BRANCH], cb[:, :D_BRANCH]))
        bc, dsilu_bc = _silu_and_grad(_conv_pre(taps_bc, cw[:, D_BRANCH:], cb[:, D_BRANCH:]))

        dt_in = dt_ref[...] + dtb_ref[...]
        dt, d_a, a_cs, dt_x, acs_x, tril = _chunk_decay(
            dt_ref[...], dtb_ref[...], al_ref[...], expand, qc)
        a_cst = a_cs.T
        aend_x = acs_x[qc - 1:qc, :]
        ea_x = jnp.exp(acs_x)
        dec_x = jnp.exp(aend_x - acs_x)
        xt = xs * dt_x
        xtb = xt.astype(BF16)
        xdb = (xt * dec_x).astype(BF16)
        d_x = _dot01(jnp.broadcast_to(ds_ref[...], (8, LANES)), expand, 3)[0:1, :]

        dy = dy_ref[...]
        dyb = dy.astype(BF16)
        dyeab = (dy * ea_x).astype(BF16)
        gds_ref[...] += collapse(jnp.broadcast_to(jnp.sum(dy * xs, axis=0, keepdims=True),
                                                  (8, D_BRANCH)))[0:1, :]

        d_bc = []
        d_cc = []
        y_offs = []
        dxt_states = []
        end_terms = []
        for g in range(N_GROUPS):
            gs = slice(g * GROUP_W, (g + 1) * GROUP_W)
            bg = bc[:, g * D_STATE:(g + 1) * D_STATE]
            cg = bc[:, (N_GROUPS + g) * D_STATE:(N_GROUPS + g + 1) * D_STATE]
            bgb = bg.astype(BF16)
            cgb = cg.astype(BF16)
            cbm = _dot(cgb, bgb, _NT)
            st_in = st_ref[0, 0, g]
            st_inb = st_in.astype(BF16)
            d_st = dst_ref[g]
            d_stb = d_st.astype(BF16)
            y_offs.append(_dot(cgb, st_inb) * ea_x[:, gs])
            dxt_states.append(_dot(bgb, d_stb) * dec_x[:, gs])
            d_c = _dot(dyeab[:, gs], st_inb, _NT)
            d_b = _dot(xdb[:, gs], d_stb, _NT)
            d_cb = jnp.zeros((qc, qc), F32)
            for k in range(HEADS_PER_GROUP):
                h = g * HEADS_PER_GROUP + k
                hs = slice(h * HEAD_DIM, (h + 1) * HEAD_DIM)
                seg = a_cs[:, h:h + 1] - a_cst[h:h + 1, :]
                lh = jnp.exp(jnp.where(tril, seg, -1e30))
                ghb = (cbm * lh).astype(BF16)
                d_cb = d_cb + _dot(dyb[:, hs], xtb[:, hs], _NT) * lh
                yd_ref[:, hs] = _dot(ghb, xtb[:, hs])
                dxt_ref[:, hs] = _dot(ghb, dyb[:, hs], _TN)
            d_cbb = d_cb.astype(BF16)
            d_cc.append(d_c + _dot(d_cbb, bgb))
            d_bc.append(d_b + _dot(d_cbb, cgb, _TN))
            end_terms.append(jnp.sum(d_st * stn_ref[0, 0, g], axis=0, keepdims=True))
            dst_ref[g] = d_st * jnp.exp(aend_x[:, gs]) + _dot(cg.T.astype(BF16), dyeab[:, gs])

        y_off = jnp.concatenate(y_offs, axis=1)
        dxt_state = jnp.concatenate(dxt_states, axis=1)
        dxt = dxt_ref[...] + dxt_state
        last = jnp.where(chunk == nc - 1, 0.0, 1.0)
        end_c = collapse(jnp.broadcast_to(jnp.concatenate(end_terms, axis=1), (8, D_BRANCH)))[0:1, :]
        da_cs = collapse(dyb.astype(F32) * yd_ref[...] - dxt_ref[...] * xtb.astype(F32)
                         + dy * y_off - dxt_state * xt)
        da_cs = da_cs + jnp.where(_row_iota(da_cs.shape) == qc - 1, end_c * last, 0.0)
        triu = lax.broadcasted_iota(jnp.int32, (qc, qc), 0) <= lax.broadcasted_iota(jnp.int32, (qc, qc), 1)
        dd_a = _dot01(da_cs, jnp.where(triu, 1.0, 0.0).astype(BF16), 3, m_left=True)
        ddt = dd_a * (-jnp.exp(al_ref[...])) + collapse(dxt * xs)
        head_lanes = _lane_iota(ddt.shape) < N_HEADS
        ddt_raw = jnp.where(head_lanes, ddt * _sigmoid(dt_in), 0.0)
        gal_ref[...] += jnp.sum(jnp.where(head_lanes, dd_a * d_a, 0.0), axis=0, keepdims=True)
        gdtb_ref[...] += jnp.sum(ddt_raw, axis=0, keepdims=True)

        dpre_xs = (dxt * dt_x + d_x * dy) * dsilu_xs
        dpre_bc = jnp.concatenate(d_bc + d_cc, axis=1) * dsilu_bc
        gcb_ref[...] += jnp.concatenate([jnp.sum(dpre_xs, axis=0, keepdims=True),
                                         jnp.sum(dpre_bc, axis=0, keepdims=True)], axis=1)
        nxs = nxs_ref[...]
        nbc = nbc_ref[...]
        du_xs = jnp.zeros_like(dpre_xs)
        du_bc = jnp.zeros_like(dpre_bc)
        for i in range(CONV_TAPS):
            k = CONV_TAPS - 1 - i
            gcw_ref[i:i + 1, :] += jnp.concatenate(
                [jnp.sum(dpre_xs * taps_xs[i], axis=0, keepdims=True),
                 jnp.sum(dpre_bc * taps_bc[i], axis=0, keepdims=True)], axis=1)
            du_xs = du_xs + _shift_up(dpre_xs, nxs, k) * cw[i:i + 1, :D_BRANCH]
            du_bc = du_bc + _shift_up(dpre_bc, nbc, k) * cw[i:i + 1, D_BRANCH:]
        nxs_ref[...] = dpre_xs
        nbc_ref[...] = dpre_bc

        dx_ref[:, :D_BRANCH] = du_xs.astype(BF16)
        dx_ref[:, D_BRANCH:D_CONV] = du_bc.astype(BF16)
        dx_ref[:, D_CONV:D_CONV + LANES] = ddt_raw.astype(BF16)
        dx_ref[:, D_CONV + LANES:] = jnp.zeros((qc, 2048 - D_CONV - LANES), BF16)

    rev = lambda b, c: b * nc + (nc - 1 - c)
    prv = lambda b, c: b * nc + jnp.maximum(nc - 2 - c, 0)
    nblk = lambda w, off, f: pl.BlockSpec((qc, w), lambda b, c: (f(b, c), off))
    full = lambda r, w: pl.BlockSpec((r, w), lambda b, c: (0, 0))
    st_spec = lambda f: pl.BlockSpec((1, 1, N_GROUPS, D_STATE, GROUP_W),
                                     lambda b, c: (b, f(c), 0, 0, 0))
    return _call_with_exchange(
        body, (proj, proj, proj, proj, proj, d_y, states, states, conv_w, conv_b, dtb, alog, dskip),
        slabs, (True,) * len(slabs), name="ssd_bwd", grid=(nb, nc),
        in_specs=[nblk(D_BRANCH, COL_XS // D_BRANCH, rev), nblk(D_BC, COL_BC // D_BC, rev),
                  nblk(LANES, COL_DT // LANES, rev),
                  nblk(D_BRANCH, COL_XS // D_BRANCH, prv), nblk(D_BC, COL_BC // D_BC, prv),
                  nblk(D_BRANCH, 0, rev),
                  st_spec(lambda c: nc - 1 - c), st_spec(lambda c: jnp.minimum(nc - c, nc - 1)),
                  full(CONV_TAPS, D_CONV), full(1, D_CONV), full(1, LANES), full(1, LANES),
                  full(1, LANES)],
        out_specs=[nblk(2048, 0, rev), full(8, D_CONV), full(1, D_CONV), full(1, LANES),
                   full(1, LANES), full(1, LANES)],
        out_shape=[jax.ShapeDtypeStruct((nb * seq, 2048), BF16),
                   jax.ShapeDtypeStruct((8, D_CONV), F32), jax.ShapeDtypeStruct((1, D_CONV), F32),
                   jax.ShapeDtypeStruct((1, LANES), F32), jax.ShapeDtypeStruct((1, LANES), F32),
                   jax.ShapeDtypeStruct((1, LANES), F32)],
        scratch_shapes=[pltpu.VMEM((N_GROUPS, D_STATE, GROUP_W), F32),
                        pltpu.VMEM((qc, D_BRANCH), F32), pltpu.VMEM((qc, D_BC), F32),
                        pltpu.VMEM((qc, D_BRANCH), F32), pltpu.VMEM((qc, D_BRANCH), F32)])


def _mid(o_sb, y_ssd, proj, x2, target, sb_w, ssd_w, w_out_b, tm):
    t = x2.shape[0]
    inv_d = 1.0 / D_MODEL

    def body(o_ref, y_ref, zsb_ref, zssd_ref, x_ref, tg_ref, sbw_ref, ssdw_ref, w_ref,
             dout_ref, dosb_ref, dy_ref, dz_ref, gw_ref, gsb_ref, gssd_ref, loss_ref):
        @pl.when(pl.program_id(0) == 0)
        def _():
            gw_ref[...] = jnp.zeros_like(gw_ref)
            gsb_ref[...] = jnp.zeros_like(gsb_ref)
            gssd_ref[...] = jnp.zeros_like(gssd_ref)
            loss_ref[...] = jnp.zeros_like(loss_ref)

        def branch(val, z, w):
            gate, dgate = _silu_and_grad(z)
            g = val * gate
            r = lax.rsqrt(jnp.mean(g * g, axis=1, keepdims=True) + EPS)
            xhat = g * r
            return (gate, dgate, r, xhat), (xhat * w).astype(BF16)

        o = o_ref[...]
        y = y_ref[...]
        saved_a, mix_a = branch(o, zsb_ref[...], sbw_ref[...])
        saved_b, mix_b = branch(y, zssd_ref[...], ssdw_ref[...])
        out = x_ref[...] + _dot(mix_a, w_ref[:D_BRANCH, :]) + _dot(mix_b, w_ref[D_BRANCH:, :])
        diff = out - tg_ref[...]
        loss_ref[...] += 0.5 * inv_d * jnp.sum(diff * diff)
        d_out = diff * inv_d
        dout_ref[...] = d_out
        d_outb = d_out.astype(BF16)
        gw_ref[:D_BRANCH, :] += _dot(mix_a, d_outb, _TN)
        gw_ref[D_BRANCH:, :] += _dot(mix_b, d_outb, _TN)

        def branch_bwd(dmix, val, w, saved):
            gate, dgate, r, xhat = saved
            gg = dmix * w
            m = jnp.mean(gg * xhat, axis=1, keepdims=True)
            dg = r * (gg - xhat * m)
            return dg * gate, dg * val * dgate, jnp.sum(dmix * xhat, axis=0, keepdims=True)

        dmix_a = _dot(d_outb, w_ref[:D_BRANCH, :], _NT)
        dmix_b = _dot(d_outb, w_ref[D_BRANCH:, :], _NT)
        d_o, dz_a, gsb = branch_bwd(dmix_a, o, sbw_ref[...], saved_a)
        d_y, dz_b, gssd = branch_bwd(dmix_b, y, ssdw_ref[...], saved_b)
        dosb_ref[...] = d_o
        dy_ref[...] = d_y
        dz_ref[:, :D_BRANCH] = dz_a.astype(BF16)
        dz_ref[:, D_BRANCH:] = dz_b.astype(BF16)
        gsb_ref[...] += gsb
        gssd_ref[...] += gssd

    row = lambda w, off: pl.BlockSpec((tm, w), lambda i: (i, off))
    full = lambda r, w: pl.BlockSpec((r, w), lambda i: (0, 0))
    resident = pl.BlockSpec((2 * D_BRANCH, D_MODEL), lambda i: (0, 0), pipeline_mode=pl.Buffered(1))
    tok = jax.ShapeDtypeStruct((t, D_MODEL), F32)
    return pl.pallas_call(
        body, name="mid",
        grid=(t // tm,),
        in_specs=[row(D_BRANCH, 0), row(D_BRANCH, 0), row(D_BRANCH, 3), row(D_BRANCH, 4),
                  row(D_MODEL, 0), row(D_MODEL, 0), full(1, D_BRANCH), full(1, D_BRANCH),
                  resident],
        out_specs=[row(D_MODEL, 0), row(D_BRANCH, 0), row(D_BRANCH, 0), row(2 * D_BRANCH, 0),
                   resident, full(1, D_BRANCH), full(1, D_BRANCH),
                   full(1, LANES)],
        out_shape=[tok, tok, tok, jax.ShapeDtypeStruct((t, 2 * D_BRANCH), BF16),
                   jax.ShapeDtypeStruct((2 * D_BRANCH, D_MODEL), F32),
                   jax.ShapeDtypeStruct((1, D_BRANCH), F32), jax.ShapeDtypeStruct((1, D_BRANCH), F32),
                   jax.ShapeDtypeStruct((1, LANES), F32)],
        compiler_params=_params(1),
    )(o_sb, y_ssd, proj, proj, x2, target, sb_w, ssd_w, w_out_b)


_DPROJ_FIRST = (0, 1, 2, 3, 5)
_DPROJ_BLOCKS = (1, 1, 1, 2, 2)


def _in_proj_bwd_x(d_parts, w_in_t, x2, d_out, norm_w, tm, slabs=()):
    t = x2.shape[0]
    n_parts = len(d_parts)

    def body(*refs):
        dp_refs = refs[:n_parts]
        w_ref, x_ref, dout_ref, nw_ref, gx_ref, gnw_ref = refs[n_parts:]

        @pl.when(pl.program_id(0) == 0)
        def _():
            gnw_ref[...] = jnp.zeros_like(gnw_ref)

        d_hn = None
        for p in range(n_parts):
            rows = slice(_DPROJ_FIRST[p] * COLBLK, (_DPROJ_FIRST[p] + _DPROJ_BLOCKS[p]) * COLBLK)
            term = _dot(dp_refs[p][...], w_ref[rows, :])
            d_hn = term if d_hn is None else d_hn + term
        xf = x_ref[...]
        r = lax.rsqrt(jnp.mean(xf * xf, axis=1, keepdims=True) + EPS)
        xhat = xf * r
        g = d_hn * nw_ref[...]
        m = jnp.mean(g * xhat, axis=1, keepdims=True)
        gx_ref[...] = dout_ref[...] + r * (g - xhat * m)
        gnw_ref[...] += jnp.sum(d_hn * xhat, axis=0, keepdims=True)

    row = lambda w: pl.BlockSpec((tm, w), lambda i: (i, 0))
    return _call_with_exchange(
        body, (*d_parts, w_in_t, x2, d_out, norm_w), slabs, (True,) * len(slabs),
        name="in_proj_bwd_x", grid=(t // tm,),
        in_specs=[row(COLBLK * _DPROJ_BLOCKS[p]) for p in range(n_parts)] + [
                  pl.BlockSpec((D_IN_PAD, D_MODEL), lambda i: (0, 0), pipeline_mode=pl.Buffered(1)),
                  row(D_MODEL), row(D_MODEL), pl.BlockSpec((1, D_MODEL), lambda i: (0, 0))],
        out_specs=[row(D_MODEL), pl.BlockSpec((1, D_MODEL), lambda i: (0, 0))],
        out_shape=[jax.ShapeDtypeStruct((t, D_MODEL), F32), jax.ShapeDtypeStruct((1, D_MODEL), F32)])


def _in_proj_bwd_w(hn, d_parts, tm):
    t = hn.shape[0]
    n_parts = len(d_parts)

    def body(hn_ref, *refs):
        dp_refs, gw_ref = refs[:n_parts], refs[n_parts]

        @pl.when(pl.program_id(0) == 0)
        def _():
            gw_ref[...] = jnp.zeros_like(gw_ref)

        hnt = hn_ref[...].astype(F32).T.astype(BF16)
        for p in range(n_parts):
            cols = slice(_DPROJ_FIRST[p] * COLBLK, (_DPROJ_FIRST[p] + _DPROJ_BLOCKS[p]) * COLBLK)
            gw_ref[:, cols] += _dot(hnt, dp_refs[p][...])

    return pl.pallas_call(
        body, name="in_proj_bwd_w",
        grid=(t // tm,),
        in_specs=[pl.BlockSpec((tm, D_MODEL), lambda i: (i, 0))]
                 + [pl.BlockSpec((tm, COLBLK * _DPROJ_BLOCKS[p]), lambda i: (i, 0))
                    for p in range(n_parts)],
        out_specs=pl.BlockSpec((D_MODEL, D_IN_PAD), lambda i: (0, 0), pipeline_mode=pl.Buffered(1)),
        out_shape=jax.ShapeDtypeStruct((D_MODEL, D_IN_PAD), F32),
        compiler_params=_params(1),
    )(hn, *d_parts)


def _adamw(parts, w, m, v, tr, name):
    _, rows, cols = w.shape
    c1 = 1.0 - ADAM_B1 ** ADAM_STEP
    c2 = 1.0 - ADAM_B2 ** ADAM_STEP

    def body(p_ref, w_ref, m_ref, v_ref, g_ref, d_ref, nm_ref, nv_ref):
        g = p_ref[0].astype(F32)
        for s in range(1, N_DEV):
            g = g + p_ref[s].astype(F32)
        nm = ADAM_B1 * m_ref[0] + (1.0 - ADAM_B1) * g
        nv = ADAM_B2 * v_ref[0] + (1.0 - ADAM_B2) * (g * g)
        g_ref[0] = g
        nm_ref[0] = nm
        nv_ref[0] = nv
        d_ref[0] = -ADAM_LR * ((nm / c1) / (jnp.sqrt(nv / c2) + ADAM_EPS) + ADAM_WD * w_ref[0])

    blk = pl.BlockSpec((1, tr, cols), lambda i: (0, i, 0))
    shape = jax.ShapeDtypeStruct((1, rows, cols), F32)
    return pl.pallas_call(
        body, name=name,
        grid=(rows // tr,),
        in_specs=[pl.BlockSpec((N_DEV, tr, cols), lambda i: (0, i, 0)), blk, blk, blk],
        out_specs=[blk, blk, blk, blk],
        out_shape=[shape, shape, shape, shape],
        compiler_params=_params(1),
    )(parts, w, m, v)


def _mesh_place():
    x, y, c = lax.axis_index("x"), lax.axis_index("y"), lax.axis_index("c")
    return x, y, c, 4 * x + 2 * y + c


def _peer(x, y, c, k):
    px = 1 - x if k & 4 else x
    py = 1 - y if k & 2 else y
    pc = 1 - c if k & 1 else c
    return (px, py, pc), 4 * px + 2 * py + pc


def _exchange(srcs, scatter, name):
    n = len(srcs)

    def body(*refs):
        copies = _exchange_copies(refs[:n], refs[n:2 * n], scatter, *refs[2 * n:])
        _exchange_start(copies)
        _exchange_wait(copies)

    return pl.pallas_call(
        body, name=name,
        in_specs=[_ANY] * n, out_specs=[_ANY] * n, out_shape=_exchange_shapes(srcs, scatter),
        scratch_shapes=_exchange_sems(n),
    )(*srcs)


def _call_with_exchange(body, operands, srcs, scatter, *, name, grid, in_specs, out_specs,
                        out_shape, scratch_shapes=()):
    n_in, n_out, n_scr, n_x = len(in_specs), len(out_specs), len(scratch_shapes), len(srcs)
    params = _params(len(grid))
    if not n_x:
        return pl.pallas_call(body, name=name, grid=grid, in_specs=list(in_specs),
                              out_specs=list(out_specs), out_shape=list(out_shape),
                              scratch_shapes=list(scratch_shapes), compiler_params=params)(*operands)

    def wrapped(*refs):
        ins, refs = refs[:n_in], refs[n_in:]
        x_src, refs = refs[:n_x], refs[n_x:]
        outs, refs = refs[:n_out], refs[n_out:]
        x_dst, refs = refs[:n_x], refs[n_x:]
        scratch, sems = refs[:n_scr], refs[n_scr:]
        ids = [pl.program_id(a) for a in range(len(grid))]
        first = functools.reduce(jnp.logical_and, [i == 0 for i in ids])
        last = functools.reduce(jnp.logical_and, [i == n - 1 for i, n in zip(ids, grid)])

        @pl.when(first)
        def _():
            _exchange_start(_exchange_copies(x_src, x_dst, scatter, *sems))

        body(*ins, *outs, *scratch)

        @pl.when(last)
        def _():
            _exchange_wait(_exchange_copies(x_src, x_dst, scatter, *sems))

    return pl.pallas_call(
        wrapped, name=name, grid=grid,
        in_specs=list(in_specs) + [_ANY] * n_x, out_specs=list(out_specs) + [_ANY] * n_x,
        out_shape=list(out_shape) + _exchange_shapes(srcs, scatter),
        scratch_shapes=list(scratch_shapes) + _exchange_sems(n_x), compiler_params=params,
    )(*operands, *srcs)


def _gather_two_level(shard, name):
    def body(x_ref, out_ref, send_sems, recv_sems, local_sem):
        x, y, c, me = _mesh_place()
        sibling = (x, y, 1 - c)
        chips = [(1 - x, y), (x, 1 - y), (1 - x, 1 - y)]

        def slab(px, py, pc):
            return out_ref.at[4 * px + 2 * py + pc]

        def copy(k, block, to, src=None):
            return pltpu.make_async_remote_copy(
                src_ref=slab(*block) if src is None else src, dst_ref=slab(*block),
                send_sem=send_sems.at[k], recv_sem=recv_sems.at[k],
                device_id=to, device_id_type=pl.DeviceIdType.MESH)

        mine = pltpu.make_async_copy(x_ref, slab(x, y, c), local_sem)
        mine.start()
        first = [copy(0, (x, y, c), sibling, src=x_ref)]
        first += [copy(1 + j, (x, y, c), (*chip, c), src=x_ref) for j, chip in enumerate(chips)]
        for cp in first:
            cp.start()
        passed = [copy(4 + j, (*chip, c), sibling) for j, chip in enumerate(chips)]
        for j, chip in enumerate(chips):
            copy(1 + j, (*chip, c), (x, y, c)).wait_recv()
            passed[j].start()
        copy(0, sibling, (x, y, c)).wait_recv()
        for j, chip in enumerate(chips):
            copy(4 + j, (*chip, 1 - c), (x, y, c)).wait_recv()
        for cp in first + passed:
            cp.wait_send()
        mine.wait()

    return pl.pallas_call(
        body, name=name,
        in_specs=[_ANY], out_specs=_ANY,
        out_shape=jax.ShapeDtypeStruct((N_DEV,) + shard.shape, shard.dtype),
        scratch_shapes=[pltpu.SemaphoreType.DMA((N_DEV - 1,)), pltpu.SemaphoreType.DMA((N_DEV - 1,)),
                        pltpu.SemaphoreType.DMA],
    )(shard)


_ANY = pl.BlockSpec(memory_space=pl.ANY)


def _exchange_shapes(srcs, scatter):
    return [jax.ShapeDtypeStruct(s.shape if sc else (N_DEV,) + s.shape, s.dtype)
            for s, sc in zip(srcs, scatter)]


def _exchange_sems(n):
    return [pltpu.SemaphoreType.DMA((n * (N_DEV - 1),)),
            pltpu.SemaphoreType.DMA((n * (N_DEV - 1),)),
            pltpu.SemaphoreType.DMA((n,))]


def _exchange_copies(src_refs, dst_refs, scatter, send_sems, recv_sems, loc_sems):
    n = len(src_refs)
    x, y, c, me = _mesh_place()

    def src_of(i, idx):
        return src_refs[i].at[idx] if scatter[i] else src_refs[i]

    local = [pltpu.make_async_copy(src_of(i, me), dst_refs[i].at[me], loc_sems.at[i])
             for i in range(n)]
    sends, recvs = [], []
    for k in range(1, N_DEV):
        peer, pidx = _peer(x, y, c, k)
        for i in range(n):
            s = i * (N_DEV - 1) + k - 1
            for dst_slab, group in ((me, sends), (pidx, recvs)):
                group.append(pltpu.make_async_remote_copy(
                    src_ref=src_of(i, pidx), dst_ref=dst_refs[i].at[dst_slab],
                    send_sem=send_sems.at[s], recv_sem=recv_sems.at[s],
                    device_id=peer, device_id_type=pl.DeviceIdType.MESH))
    return local, sends, recvs


def _exchange_start(copies):
    local, sends, _ = copies
    for cp in local + sends:
        cp.start()


def _exchange_wait(copies):
    local, sends, recvs = copies
    for cp in recvs:
        cp.wait_recv()
    for cp in sends:
        cp.wait_send()
    for cp in local:
        cp.wait()


def _pad_lanes(v, width=LANES):
    return jnp.pad(v, ((0, 0), (0, width - v.shape[1])))


def _local_step(x, target, norm_w, w_in_b, q_norm_w, k_norm_w, conv_w, conv_b, dt_bias, a_log,
                d_skip, sb_norm_w, ssd_norm_w, w_out_b, tm=256, tq=512, tmid=256, blk=ATT_BLK,
                scatter=False, w_in_t=None):
    nb, seq, _ = x.shape
    t = nb * seq
    x2 = x.reshape(t, D_MODEL)
    tg2 = target.reshape(t, D_MODEL)
    qw2 = jnp.tile(q_norm_w, (1, 2))
    kw2 = jnp.tile(k_norm_w, (1, 2))
    dtb, alog, dsk = _pad_lanes(dt_bias), _pad_lanes(a_log), _pad_lanes(d_skip)

    if w_in_t is None:
        w_in_t = w_in_b.T
    tall = min(2 * tm, t)
    if scatter:
        proj, hn, wout_all, cw_all = _in_proj(x2, norm_w, w_in_b, tall, (w_out_b, conv_w))
        w_out_b = wout_all.reshape(2 * D_BRANCH, D_MODEL)
        conv_w = jnp.transpose(cw_all, (1, 0, 2)).reshape(CONV_TAPS, D_CONV)
    else:
        proj, hn = _in_proj(x2, norm_w, w_in_b, tall)
    qs, kn, vb, kt, ksq = _qk_prep(proj, qw2, kw2, nb, seq, tq)
    o_sb, sb_tot, sb_low = _attn_fwd(qs, kn, vb, jnp.max(ksq, axis=1), nb, seq, blk)
    y_ssd, states = _ssd_fwd(proj, conv_w, conv_b, dtb, alog, dsk, nb, seq)
    d_out, d_osb, d_y, d_z, g_wout, g_sbw, g_ssdw, loss = _mid(
        o_sb, y_ssd, proj, x2, tg2, sb_norm_w, ssd_norm_w, w_out_b, tmid)
    dqs, dkn, dvh = _attn_bwd(qs, kn, kt, vb, sb_tot, sb_low, d_osb, nb, seq, blk)
    dq_raw, dk_raw, dv_raw, g_qw, g_kw = _qk_bwd(proj, dqs, dkn, dvh, qw2, kw2, nb, seq, tq)
    wout_slabs = (g_wout.reshape(N_DEV, 2 * D_BRANCH // N_DEV, D_MODEL).astype(BF16),)
    d_xbc, g_cw, g_cb, g_dtb, g_alog, g_dsk, *moved = _ssd_bwd(
        proj, d_y, states, conv_w, conv_b, dtb, alog, dsk, nb, seq, wout_slabs if scatter else ())
    d_parts = [dq_raw, dk_raw, dv_raw, d_z, d_xbc]
    g_win = _in_proj_bwd_w(hn, d_parts, tall)[:, :D_IN]
    g_cw = g_cw[:CONV_TAPS]
    if scatter:
        g_wout, = moved
        grad_x, g_nw, g_win, g_cw = _in_proj_bwd_x(
            d_parts, w_in_t, x2, d_out, norm_w, tall, _grad_slabs(g_win, g_cw))
    else:
        grad_x, g_nw = _in_proj_bwd_x(d_parts, w_in_t, x2, d_out, norm_w, tall)

    small = dict(
        norm_w=g_nw,
        q_norm_w=g_qw[:, :HEAD_DIM] + g_qw[:, HEAD_DIM:],
        k_norm_w=g_kw[:, :HEAD_DIM] + g_kw[:, HEAD_DIM:],
        conv_b=g_cb, dt_bias=g_dtb[:, :N_HEADS], A_log=g_alog[:, :N_HEADS],
        D_skip=g_dsk[:, :N_HEADS], sb_norm_w=g_sbw, ssd_norm_w=g_ssdw)
    return loss[0, 0], grad_x.reshape(nb, seq, D_MODEL), g_win, g_wout, g_cw, small


def _grad_slabs(g_win, g_cw):
    w_sh = D_IN // N_DEV
    c_sh = D_CONV // N_DEV
    return (jnp.transpose(g_win.reshape(D_MODEL, N_DEV, w_sh), (1, 0, 2)).astype(BF16),
            jnp.pad(jnp.transpose(g_cw.reshape(CONV_TAPS, N_DEV, c_sh), (1, 0, 2)),
                    ((0, 0), (0, 8 - CONV_TAPS), (0, 0))))


_SMALL = ("norm_w", "q_norm_w", "k_norm_w", "conv_b", "dt_bias", "A_log", "D_skip",
          "sb_norm_w", "ssd_norm_w")


def _pack_small(vals):
    flat = jnp.concatenate([_pad_lanes(vals[n], -(-vals[n].shape[1] // LANES) * LANES)
                            for n in _SMALL], axis=1)
    return jnp.pad(flat, ((0, 0), (0, 48 * LANES - flat.shape[1]))).reshape(48, LANES)


def _unpack_small(packed, like):
    out, r = {}, 0
    for n in _SMALL:
        width = like[n].shape[1]
        nr = -(-width // LANES)
        out[n] = packed[r:r + nr].reshape(1, nr * LANES)[:, :width]
        r += nr
    return out


def kernel(x, norm_w, w_in, q_norm_w, k_norm_w, conv_w, conv_b, dt_bias, A_log, D_skip, sb_norm_w, ssd_norm_w, w_out, loss_target, m_norm_w, m_w_in, m_q_norm_w, m_k_norm_w, m_conv_w, m_conv_b, m_dt_bias, m_A_log, m_D_skip, m_sb_norm_w, m_ssd_norm_w, m_w_out, v_norm_w, v_w_in, v_q_norm_w, v_k_norm_w, v_conv_w, v_conv_b, v_dt_bias, v_A_log, v_D_skip, v_sb_norm_w, v_ssd_norm_w, v_w_out):
    win_all = _gather_two_level(w_in[0].astype(BF16), "gather_w_in")
    w_in_b = jnp.pad(jnp.transpose(win_all, (1, 0, 2)).reshape(D_MODEL, D_IN),
                     ((0, 0), (0, D_IN_PAD - D_IN)))
    w_in_t = jnp.pad(jnp.transpose(win_all, (0, 2, 1)).reshape(D_IN, D_MODEL),
                     ((0, D_IN_PAD - D_IN), (0, 0)))

    loss, grad_x, win_parts, wout_parts, cw_parts, g_small = _local_step(
        x, loss_target, norm_w, w_in_b, q_norm_w, k_norm_w, conv_w[0], conv_b, dt_bias, A_log,
        D_skip, sb_norm_w, ssd_norm_w, w_out[0].astype(BF16), scatter=True, w_in_t=w_in_t)
    packed = _pack_small(g_small).at[-1, 0].set(loss)
    small_parts, = _exchange([packed], [False], "gather_small_grads")
    loss = jnp.sum(small_parts[:, -1, 0])

    small_w = dict(norm_w=norm_w, q_norm_w=q_norm_w, k_norm_w=k_norm_w, conv_b=conv_b,
                   dt_bias=dt_bias, A_log=A_log, D_skip=D_skip, sb_norm_w=sb_norm_w,
                   ssd_norm_w=ssd_norm_w)
    small_m = dict(norm_w=m_norm_w, q_norm_w=m_q_norm_w, k_norm_w=m_k_norm_w, conv_b=m_conv_b,
                   dt_bias=m_dt_bias, A_log=m_A_log, D_skip=m_D_skip, sb_norm_w=m_sb_norm_w,
                   ssd_norm_w=m_ssd_norm_w)
    small_v = dict(norm_w=v_norm_w, q_norm_w=v_q_norm_w, k_norm_w=v_k_norm_w, conv_b=v_conv_b,
                   dt_bias=v_dt_bias, A_log=v_A_log, D_skip=v_D_skip, sb_norm_w=v_sb_norm_w,
                   ssd_norm_w=v_ssd_norm_w)

    pad8 = lambda a: jnp.pad(a, ((0, 0), (0, 8 - CONV_TAPS), (0, 0)))
    r_win = _adamw(win_parts, w_in, m_w_in, v_w_in, 128, "adamw_w_in")
    r_wout = _adamw(wout_parts, w_out, m_w_out, v_w_out, 128, "adamw_w_out")
    r_cw = _adamw(cw_parts, pad8(conv_w), pad8(m_conv_w), pad8(v_conv_w), 8, "adamw_conv_w")
    r_small = _adamw(small_parts, _pack_small(small_w)[None], _pack_small(small_m)[None],
                     _pack_small(small_v)[None], 48, "adamw_small")

    res = {"w_in": r_win, "w_out": r_wout, "conv_w": [a[:, :CONV_TAPS] for a in r_cw]}
    unpacked = [_unpack_small(a[0], small_w) for a in r_small]
    for n in _SMALL:
        res[n] = [u[n] for u in unpacked]
    order = ("norm_w", "w_in", "q_norm_w", "k_norm_w", "conv_w", "conv_b", "dt_bias", "A_log",
             "D_skip", "sb_norm_w", "ssd_norm_w", "w_out")
    outs = [loss, grad_x]
    for kind in range(4):
        outs += [res[n][kind] for n in order]
    return tuple(outs)
```

```python
import functools
import math

import jax
import jax.numpy as jnp
from jax import lax
from jax.experimental import pallas as pl
from jax.experimental.pallas import tpu as pltpu

F32 = jnp.float32
BF16 = jnp.bfloat16

D_MODEL = 1024
N_HEADS = 16
HEAD_DIM = 64
N_PAIRS = N_HEADS // 2
D_BRANCH = 1024
N_GROUPS = 2
HEADS_PER_GROUP = 8
D_STATE = 128
GROUP_W = HEADS_PER_GROUP * HEAD_DIM
D_BC = 2 * N_GROUPS * D_STATE
D_CONV = D_BRANCH + D_BC
D_IN = 6672
COLBLK = 1024
D_IN_PAD = 7168
N_COLBLK = D_IN_PAD // COLBLK
COL_XS = 5120
COL_BC = 6144
COL_DT = 6656
EPS = 1e-6
CONV_TAPS = 4
N_DEV = 8

LANES = 128
SSD_CHUNK = 128
ATT_BLK = 256
ATT_HEADS = 4
ATT_W = ATT_HEADS * HEAD_DIM
N_ATT_GROUPS = N_HEADS // ATT_HEADS
EXP_UNDERFLOW = -105.0
VMEM_LIMIT = 56 * 1024 * 1024

ADAM_LR = 0.001
ADAM_B1 = 0.9
ADAM_B2 = 0.999
ADAM_EPS = 1e-08
ADAM_WD = 0.01
ADAM_STEP = 10

_NT = (((1,), (1,)), ((), ()))
_TN = (((0,), (0,)), ((), ()))


def _params(n_grid):
    return pltpu.CompilerParams(dimension_semantics=("arbitrary",) * n_grid,
                                vmem_limit_bytes=VMEM_LIMIT)


def _dot(a, b, dims=None):
    if dims is None:
        return jnp.dot(a, b, preferred_element_type=F32)
    return lax.dot_general(a, b, dims, preferred_element_type=F32)


def _sigmoid(x):
    return 1.0 / (1.0 + jnp.exp(-x))


def _softplus(x):
    return jnp.maximum(x, 0.0) + jnp.log(1.0 + jnp.exp(-jnp.abs(x)))


def _split_bf16(x):
    hi = x.astype(BF16)
    lo = (x - hi.astype(F32)).astype(BF16)
    return hi, lo


def _lane_iota(shape):
    return lax.broadcasted_iota(jnp.int32, shape, len(shape) - 1)


def _row_iota(shape):
    return lax.broadcasted_iota(jnp.int32, shape, len(shape) - 2)


def _pair_sum(x):
    r = lax.broadcasted_iota(jnp.int32, (LANES, LANES), 0)
    c = lax.broadcasted_iota(jnp.int32, (LANES, LANES), 1)
    same_head = jnp.where(r // HEAD_DIM == c // HEAD_DIM, 1.0, 0.0).astype(BF16)
    hi, lo = _split_bf16(x)
    return _dot(hi, same_head) + _dot(lo, same_head)


def _head_lanes(x, a):
    lane = _lane_iota(x.shape)
    mine = (lane >= a * HEAD_DIM) & (lane < (a + 1) * HEAD_DIM)
    return jnp.where(mine, x, jnp.zeros_like(x))


def _head_expand():
    r = lax.broadcasted_iota(jnp.int32, (LANES, D_BRANCH), 0)
    c = lax.broadcasted_iota(jnp.int32, (LANES, D_BRANCH), 1)
    return jnp.where(c // HEAD_DIM == r, 1.0, 0.0).astype(BF16)


def _in_proj(x2, norm_w, w_in_b, tm, shards=()):
    t = x2.shape[0]

    def body(x_ref, nw_ref, w_ref, proj_ref, hn_ref):
        xf = x_ref[...]
        r = lax.rsqrt(jnp.mean(xf * xf, axis=1, keepdims=True) + EPS)
        hn = (xf * r * nw_ref[...]).astype(BF16)
        hn_ref[...] = hn
        for j in range(N_COLBLK):
            cols = slice(j * COLBLK, (j + 1) * COLBLK)
            proj_ref[:, cols] = _dot(hn, w_ref[:, cols])

    return _call_with_exchange(
        body, (x2, norm_w, w_in_b), shards, (False,) * len(shards), name="in_proj",
        grid=(t // tm,),
        in_specs=[pl.BlockSpec((tm, D_MODEL), lambda i: (i, 0)),
                  pl.BlockSpec((1, D_MODEL), lambda i: (0, 0)),
                  pl.BlockSpec((D_MODEL, D_IN_PAD), lambda i: (0, 0), pipeline_mode=pl.Buffered(1))],
        out_specs=[pl.BlockSpec((tm, D_IN_PAD), lambda i: (i, 0)),
                   pl.BlockSpec((tm, D_MODEL), lambda i: (i, 0))],
        out_shape=[jax.ShapeDtypeStruct((t, D_IN_PAD), F32),
                   jax.ShapeDtypeStruct((t, D_MODEL), BF16)])


def _qk_prep(proj, qw2, kw2, nb, seq, tq):
    nl = seq // tq
    scale = 1.0 / math.sqrt(HEAD_DIM)

    def body(q_ref, k_ref, v_ref, qw_ref, kw_ref, qs_ref, kn_ref, vb_ref, kt_ref, ksq_ref):
        def norm(x, w):
            r = lax.rsqrt(_pair_sum(x * x) * (1.0 / HEAD_DIM) + EPS)
            return x * r * w

        vb_ref[...] = v_ref[...].astype(BF16)
        for p in range(N_PAIRS):
            cols = slice(p * LANES, (p + 1) * LANES)
            kn = norm(k_ref[:, cols], kw_ref[...])
            knb = kn.astype(BF16)
            qs_ref[:, cols] = (norm(q_ref[:, cols], qw_ref[...]) * scale).astype(BF16)
            kn_ref[:, cols] = knb
            kt_ref[0, p] = kn.T.astype(BF16)
            kf = knb.astype(F32)
            ksq_ref[0, 0, p:p + 1, :] = jnp.max(_pair_sum(kf * kf), axis=0, keepdims=True) * 1.0001

    tok_shape = jax.ShapeDtypeStruct((nb * seq, D_BRANCH), BF16)
    tok = lambda blk: pl.BlockSpec((tq, D_BRANCH), lambda b, i: (b * nl + i, blk))
    vec = pl.BlockSpec((1, LANES), lambda b, i: (0, 0))
    return pl.pallas_call(
        body, name="qk_prep",
        grid=(nb, nl),
        in_specs=[tok(0), tok(1), tok(2), vec, vec],
        out_specs=[tok(0), tok(0), tok(0),
                   pl.BlockSpec((1, N_PAIRS, LANES, tq), lambda b, i: (b, 0, 0, i)),
                   pl.BlockSpec((1, 1, N_PAIRS, LANES), lambda b, i: (b, i, 0, 0))],
        out_shape=[tok_shape, tok_shape, tok_shape,
                   jax.ShapeDtypeStruct((nb, N_PAIRS, LANES, seq), BF16),
                   jax.ShapeDtypeStruct((nb, nl, N_PAIRS, LANES), F32)],
        compiler_params=_params(2),
    )(proj, proj, proj, qw2, kw2)


def _attn_fwd(qs, kn, vb, ksq, nb, seq, blk):
    nq = seq // blk

    def body(q_ref, k_ref, v_ref, ksq_ref, o_ref, tot_ref, low_ref):
        qi = pl.program_id(2)
        r_i = lax.broadcasted_iota(jnp.int32, (blk, blk), 0)
        c_i = lax.broadcasted_iota(jnp.int32, (blk, blk), 1)
        csum = jnp.where(r_i >= c_i, 1.0, 0.0).astype(BF16)
        heads = range(ATT_HEADS)
        head = _head_lanes

        q_blk = q_ref[...]
        qf = q_blk.astype(F32)
        q_head = [head(q_blk, a) for a in heads]
        zmax = []
        for a in heads:
            qsq = jnp.sum(head(qf * qf, a), axis=1, keepdims=True)
            kmax = ksq_ref[0, 0, a // 2:a // 2 + 1, (a % 2) * HEAD_DIM:(a % 2) * HEAD_DIM + 1]
            zmax.append(1.01 * jnp.sqrt(qsq * kmax) + 0.01)

        def exhausted(run):
            top = functools.reduce(jnp.maximum, [jnp.max(run[a] + zmax[a]) for a in heads])
            return top < EXP_UNDERFLOW

        def sweep(blocks, run, acc):
            half = blk // 2

            def tiles(diag):
                return [(0, half, half), (half, half, blk)] if diag else [(0, blk, blk)]

            def keep(x, r0, diag):
                if diag:
                    rows = lax.broadcasted_iota(jnp.int32, x.shape, 0) + r0
                    x = jnp.where(lax.broadcasted_iota(jnp.int32, x.shape, 1) < rows, x, 0.0)
                return x

            offs = [pl.multiple_of(j * blk, blk) for j, _, _ in blocks]
            z = [[[_dot(q_head[a][r0:r0 + nr], k_ref[pl.ds(off, nk), :], _NT) for a in heads]
                  for r0, nr, nk in tiles(diag)] for (_, diag, _), off in zip(blocks, offs)]
            cl = [[[_dot(keep(-_softplus(zt[a]), r0, diag).astype(BF16), csum[:nk, :nk])
                    for a in heads]
                   for (r0, nr, nk), zt in zip(tiles(diag), zb)]
                  for (_, diag, _), zb in zip(blocks, z)]
            for (_, diag, valid), zb, clb, off in zip(blocks, z, cl, offs):
                rows_out, steps = [], [[] for _ in heads]
                for (r0, nr, nk), zt, clt in zip(tiles(diag), zb, clb):
                    v_blk = v_ref[pl.ds(off, nk), :]
                    part = None
                    for a in heads:
                        wa = keep(jnp.exp(zt[a] + clt[a] + run[a][r0:r0 + nr]), r0, diag)
                        term = _dot(wa.astype(BF16), head(v_blk, a))
                        part = term if part is None else part + term
                        steps[a].append(clt[a][:, 0:1])
                    rows_out.append(part)
                part = jnp.concatenate(rows_out, axis=0)
                steps = [jnp.concatenate(steps[a], axis=0) for a in heads]
                if valid is not None:
                    part = jnp.where(valid, part, 0.0)
                    steps = [jnp.where(valid, s, 0.0) for s in steps]
                acc = acc + part
                run = [run[a] + steps[a] for a in heads]
            return run, acc

        run = [jnp.zeros((blk, 1), F32)] * ATT_HEADS
        acc = jnp.zeros((blk, ATT_W), F32)
        run, acc = sweep([(qi, True, None), (jnp.maximum(qi - 1, 0), False, qi >= 1)], run, acc)
        low = jnp.maximum(qi - 1, 0)

        def more(carry):
            low, done, _, _ = carry
            return (low > 0) & jnp.logical_not(done)

        def pair(carry):
            low, _, run, acc = carry
            run, acc = sweep([(low - 1, False, None), (jnp.maximum(low - 2, 0), False, low >= 2)],
                             run, acc)
            return jnp.maximum(low - 2, 0), exhausted(run), run, acc

        low, _, run, acc = lax.while_loop(more, pair, (low, exhausted(run), run, acc))
        low_ref[pl.program_id(0) * N_ATT_GROUPS + pl.program_id(1), qi] = low.astype(F32)
        o_ref[...] = acc
        for a in heads:
            as_row = jnp.sum(jnp.where(r_i == c_i, run[a], 0.0), axis=0, keepdims=True)
            tot_ref[0, a, 0] = jnp.broadcast_to(as_row, (8, blk))

    return pl.pallas_call(
        body, name="sb_attn_fwd",
        grid=(nb, N_ATT_GROUPS, nq),
        in_specs=[pl.BlockSpec((blk, ATT_W), lambda b, h, i: (b * nq + i, h)),
                  pl.BlockSpec((seq, ATT_W), lambda b, h, i: (b, h)),
                  pl.BlockSpec((seq, ATT_W), lambda b, h, i: (b, h)),
                  pl.BlockSpec((1, 1, ATT_HEADS // 2, LANES), lambda b, h, i: (b, h, 0, 0))],
        out_specs=[pl.BlockSpec((blk, ATT_W), lambda b, h, i: (b * nq + i, h)),
                   pl.BlockSpec((1, ATT_HEADS, 1, 8, blk), lambda b, h, i: (b, h, i, 0, 0)),
                   pl.BlockSpec(memory_space=pltpu.SMEM)],
        out_shape=[jax.ShapeDtypeStruct((nb * seq, D_BRANCH), F32),
                   jax.ShapeDtypeStruct((nb, N_HEADS, nq, 8, blk), F32),
                   jax.ShapeDtypeStruct((nb * N_ATT_GROUPS, nq), F32)],
        compiler_params=_params(3),
    )(qs, kn, vb, ksq.reshape(nb, N_ATT_GROUPS, ATT_HEADS // 2, LANES))


def _attn_bwd(qs, kn, kt, vb, tot, low, d_o, nb, seq, blk):
    nq = seq // blk

    def body(q_ref, k_ref, kt_ref, v_ref, tot_ref, low_ref, do_ref, dq_ref, dk_ref, dv_ref):
        qi = pl.program_id(2)

        @pl.when(qi == 0)
        def _():
            dk_ref[...] = jnp.zeros_like(dk_ref)
            dv_ref[...] = jnp.zeros_like(dv_ref)

        r_i = lax.broadcasted_iota(jnp.int32, (blk, blk), 0)
        c_i = lax.broadcasted_iota(jnp.int32, (blk, blk), 1)
        before = jnp.where(c_i < r_i, 1.0, 0.0).astype(BF16)
        upto = jnp.where(c_i <= r_i, 1.0, 0.0).astype(BF16)
        causal = r_i < c_i

        heads = range(ATT_HEADS)
        q_head = [_head_lanes(q_ref[...], a) for a in heads]
        d_ob = [_head_lanes(do_ref[...].astype(BF16), a) for a in heads]
        total = [tot_ref[0, a, 0][0:1, :] for a in heads]

        def sweep(blocks, lsum, esum, dqt):
            def keep(x, diag):
                return jnp.where(causal, x, 0.0) if diag else x

            def there(x, valid):
                return x if valid is None else jnp.where(valid, x, 0.0)

            offs = [pl.multiple_of(j * blk, blk) for j, _, _ in blocks]
            zt = [[_dot(k_ref[pl.ds(off, blk), :], q_head[a], _NT) for a in heads]
                  for off in offs]
            dwt = [[_dot(v_ref[pl.ds(off, blk), :], d_ob[a], _NT) for a in heads]
                   for off in offs]
            sp, lk, lpre = [], [], []
            for (_, diag, _), ztb in zip(blocks, zt):
                sp.append([_softplus(ztb[a]) for a in heads])
                lk.append([keep(-sp[-1][a], diag).astype(BF16) for a in heads])
                lpre.append([_dot(before, lk[-1][a]) for a in heads])
            wt, et, epre = [], [], []
            for i, (_, diag, valid) in enumerate(blocks):
                wt.append([keep(jnp.exp(zt[i][a] + (total[a] - lsum[a] - lpre[i][a])), diag)
                           for a in heads])
                et.append([dwt[i][a] * wt[i][a] for a in heads])
                epre.append([_dot(upto, et[i][a].astype(BF16)) for a in heads])
                lsum = [lsum[a] + there(lpre[i][a][blk - 1:blk, :] + lk[i][a][blk - 1:blk, :], valid)
                        for a in heads]
            for i, (_, diag, valid) in enumerate(blocks):
                dzb = [keep(et[i][a] - jnp.exp(zt[i][a] - sp[i][a]) * (esum[a] + epre[i][a]),
                            diag).astype(BF16) for a in heads]
                esum = [esum[a] + there(epre[i][a][blk - 1:blk, :], valid) for a in heads]
                dk_ref[pl.ds(offs[i], blk), :] += there(functools.reduce(
                    jnp.add, [_dot(dzb[a], q_head[a]) for a in heads]), valid)
                dv_ref[pl.ds(offs[i], blk), :] += there(functools.reduce(
                    jnp.add, [_dot(wt[i][a].astype(BF16), d_ob[a]) for a in heads]), valid)
                dqt = [dqt[a] + there(_dot(
                    kt_ref[0, a // 2, (a % 2) * HEAD_DIM:(a % 2 + 1) * HEAD_DIM,
                           pl.ds(offs[i], blk)], dzb[a]), valid) for a in heads]
            return lsum, esum, dqt

        row = [jnp.zeros((1, blk), F32)] * ATT_HEADS
        dqt = [jnp.zeros((HEAD_DIM, blk), F32)] * ATT_HEADS
        low = low_ref[pl.program_id(0) * N_ATT_GROUPS + pl.program_id(1), qi].astype(jnp.int32)
        low = jnp.clip(low, 0, jnp.maximum(qi - 1, 0))

        def pair(carry):
            j, lsum, esum, dqt = carry
            return (j + 2,) + sweep([(j, False, None), (j + 1, False, j + 1 < qi - 1)],
                                    lsum, esum, dqt)

        _, lsum, esum, dqt = lax.while_loop(lambda c: c[0] < qi - 1, pair, (low, row, row, dqt))
        _, _, dqt = sweep([(jnp.maximum(qi - 1, 0), False, qi >= 1), (qi, True, None)],
                          lsum, esum, dqt)
        dq_ref[...] = jnp.concatenate(dqt, axis=0).T

    seq_blk = pl.BlockSpec((seq, ATT_W), lambda b, h, i: (b, h))
    tok = pl.BlockSpec((blk, ATT_W), lambda b, h, i: (b * nq + i, h))
    tok_shape = jax.ShapeDtypeStruct((nb * seq, D_BRANCH), F32)
    return pl.pallas_call(
        body, name="sb_attn_bwd",
        grid=(nb, N_ATT_GROUPS, nq),
        in_specs=[tok, seq_blk,
                  pl.BlockSpec((1, ATT_HEADS // 2, LANES, seq), lambda b, h, i: (b, h, 0, 0)),
                  seq_blk,
                  pl.BlockSpec((1, ATT_HEADS, 1, 8, blk), lambda b, h, i: (b, h, i, 0, 0)),
                  pl.BlockSpec(memory_space=pltpu.SMEM),
                  tok],
        out_specs=[tok, seq_blk, seq_blk],
        out_shape=[tok_shape, tok_shape, tok_shape],
        compiler_params=_params(3),
    )(qs, kn, kt, vb, tot, low, d_o)


def _qk_bwd(proj, dqs, dkn, dvh, qw2, kw2, nb, seq, tq):
    nl = seq // tq
    scale = 1.0 / math.sqrt(HEAD_DIM)

    def body(q_ref, k_ref, dq_ref, dk_ref, dv_ref, qw_ref, kw_ref,
             dqr_ref, dkr_ref, dvr_ref, gq_ref, gk_ref):
        @pl.when((pl.program_id(0) == 0) & (pl.program_id(1) == 0))
        def _():
            gq_ref[...] = jnp.zeros_like(gq_ref)
            gk_ref[...] = jnp.zeros_like(gk_ref)

        def norm_bwd(x, w, dy):
            r = lax.rsqrt(_pair_sum(x * x) * (1.0 / HEAD_DIM) + EPS)
            xhat = x * r
            g = dy * w
            m = _pair_sum(g * xhat) * (1.0 / HEAD_DIM)
            return r * (g - xhat * m), jnp.sum(dy * xhat, axis=0, keepdims=True)

        dvr_ref[...] = dv_ref[...].astype(BF16)
        gq = jnp.zeros((1, LANES), F32)
        gk = jnp.zeros((1, LANES), F32)
        for p in range(N_PAIRS):
            cols = slice(p * LANES, (p + 1) * LANES)
            dqr, gq_p = norm_bwd(q_ref[:, cols], qw_ref[...], dq_ref[:, cols] * scale)
            dkr, gk_p = norm_bwd(k_ref[:, cols], kw_ref[...], dk_ref[:, cols])
            dqr_ref[:, cols] = dqr.astype(BF16)
            dkr_ref[:, cols] = dkr.astype(BF16)
            gq, gk = gq + gq_p, gk + gk_p
        gq_ref[...] += gq
        gk_ref[...] += gk

    tok = lambda blk: pl.BlockSpec((tq, D_BRANCH), lambda b, i: (b * nl + i, blk))
    vec = pl.BlockSpec((1, LANES), lambda b, i: (0, 0))
    tshape = jax.ShapeDtypeStruct((nb * seq, D_BRANCH), BF16)
    return pl.pallas_call(
        body, name="qk_bwd",
        grid=(nb, nl),
        in_specs=[tok(0), tok(1), tok(0), tok(0), tok(0), vec, vec],
        out_specs=[tok(0), tok(0), tok(0), vec, vec],
        out_shape=[tshape, tshape, tshape,
                   jax.ShapeDtypeStruct((1, LANES), F32), jax.ShapeDtypeStruct((1, LANES), F32)],
        compiler_params=_params(2),
    )(proj, proj, dqs, dkn, dvh, qw2, kw2)


def _shift_down(cur, prev, k):
    if k == 0:
        return cur
    rows = _row_iota(cur.shape)
    return jnp.where(rows < k, pltpu.roll(prev, k, axis=0), pltpu.roll(cur, k, axis=0))


def _shift_up(cur, nxt, k):
    if k == 0:
        return cur
    n = cur.shape[0]
    rows = _row_iota(cur.shape)
    return jnp.where(rows < n - k, pltpu.roll(cur, n - k, axis=0), pltpu.roll(nxt, n - k, axis=0))


def _conv_taps(cur, prev):
    return [_shift_down(cur, prev, CONV_TAPS - 1 - i) for i in range(CONV_TAPS)]


def _conv_pre(taps, w, b):
    out = b
    for i in range(CONV_TAPS):
        out = out + taps[i] * w[i:i + 1, :]
    return out


def _silu(x):
    return x * _sigmoid(x)


def _silu_and_grad(x):
    s = _sigmoid(x)
    return x * s, s * (1.0 + x * (1.0 - s))


def _dot01(x, m01, parts, dims=None, m_left=False):
    total, rest = None, x
    for i in range(parts):
        piece = rest.astype(BF16)
        if i + 1 < parts:
            rest = rest - piece.astype(F32)
        term = _dot(m01, piece, dims) if m_left else _dot(piece, m01, dims)
        total = term if total is None else total + term
    return total


def _chunk_decay(dt_raw, dtb, alog, expand, qc):
    dt = _softplus(dt_raw + dtb)
    d_a = dt * (-jnp.exp(alog))
    r_i = lax.broadcasted_iota(jnp.int32, (qc, qc), 0)
    c_i = lax.broadcasted_iota(jnp.int32, (qc, qc), 1)
    tril = r_i >= c_i
    a_cs = _dot01(d_a, jnp.where(tril, 1.0, 0.0).astype(BF16), 3, m_left=True)
    dt_x = _dot01(dt, expand, 3)
    acs_x = _dot01(a_cs, expand, 3)
    return dt, d_a, a_cs, dt_x, acs_x, tril


def _ssd_fwd(proj, conv_w, conv_b, dtb, alog, dskip, nb, seq):
    qc = SSD_CHUNK
    nc = seq // qc

    def body(xs_ref, bc_ref, dt_ref, cw_ref, cb_ref, dtb_ref, al_ref, ds_ref,
             y_ref, st_ref, pxs_ref, pbc_ref, state_ref):
        @pl.when(pl.program_id(1) == 0)
        def _():
            pxs_ref[...] = jnp.zeros_like(pxs_ref)
            pbc_ref[...] = jnp.zeros_like(pbc_ref)
            state_ref[...] = jnp.zeros_like(state_ref)

        expand = _head_expand()
        xs_raw = xs_ref[...]
        bc_raw = bc_ref[...]
        cw = cw_ref[...]
        cb = cb_ref[...]
        xs = _silu(_conv_pre(_conv_taps(xs_raw, pxs_ref[...]), cw[:, :D_BRANCH], cb[:, :D_BRANCH]))
        bc = _silu(_conv_pre(_conv_taps(bc_raw, pbc_ref[...]), cw[:, D_BRANCH:], cb[:, D_BRANCH:]))
        pxs_ref[...] = xs_raw
        pbc_ref[...] = bc_raw

        dt, d_a, a_cs, dt_x, acs_x, tril = _chunk_decay(
            dt_ref[...], dtb_ref[...], al_ref[...], expand, qc)
        a_cst = a_cs.T
        aend_x = acs_x[qc - 1:qc, :]
        ea_x = jnp.exp(acs_x)
        dec_x = jnp.exp(aend_x - acs_x)
        xt = xs * dt_x
        xtb = xt.astype(BF16)
        xdb = (xt * dec_x).astype(BF16)
        d_x = _dot01(jnp.broadcast_to(ds_ref[...], (8, LANES)), expand, 3)[0:1, :]
        st_ref[0, 0] = state_ref[...]

        for g in range(N_GROUPS):
            gs = slice(g * GROUP_W, (g + 1) * GROUP_W)
            bg = bc[:, g * D_STATE:(g + 1) * D_STATE]
            cg = bc[:, (N_GROUPS + g) * D_STATE:(N_GROUPS + g + 1) * D_STATE]
            bgb = bg.astype(BF16)
            cgb = cg.astype(BF16)
            cbm = _dot(cgb, bgb, _NT)
            st_in = state_ref[g]
            y_off = _dot(cgb, st_in.astype(BF16)) * ea_x[:, gs]
            for k in range(HEADS_PER_GROUP):
                h = g * HEADS_PER_GROUP + k
                hs = slice(h * HEAD_DIM, (h + 1) * HEAD_DIM)
                seg = a_cs[:, h:h + 1] - a_cst[h:h + 1, :]
                gh = cbm * jnp.exp(jnp.where(tril, seg, -1e30))
                y_h = _dot(gh.astype(BF16), xtb[:, hs]) + y_off[:, k * HEAD_DIM:(k + 1) * HEAD_DIM]
                y_ref[:, hs] = y_h + d_x[:, hs] * xs[:, hs]
            state_ref[g] = st_in * jnp.exp(aend_x[:, gs]) + _dot(bg.T.astype(BF16), xdb[:, gs])

    nblk = lambda w, off: pl.BlockSpec((qc, w), lambda b, c: (b * nc + c, off))
    full = lambda r, w: pl.BlockSpec((r, w), lambda b, c: (0, 0))
    return pl.pallas_call(
        body, name="ssd_fwd",
        grid=(nb, nc),
        in_specs=[nblk(D_BRANCH, COL_XS // D_BRANCH), nblk(D_BC, COL_BC // D_BC),
                  nblk(LANES, COL_DT // LANES),
                  full(CONV_TAPS, D_CONV), full(1, D_CONV), full(1, LANES), full(1, LANES),
                  full(1, LANES)],
        out_specs=[pl.BlockSpec((qc, D_BRANCH), lambda b, c: (b * nc + c, 0)),
                   pl.BlockSpec((1, 1, N_GROUPS, D_STATE, GROUP_W), lambda b, c: (b, c, 0, 0, 0))],
        out_shape=[jax.ShapeDtypeStruct((nb * seq, D_BRANCH), F32),
                   jax.ShapeDtypeStruct((nb, nc, N_GROUPS, D_STATE, GROUP_W), F32)],
        scratch_shapes=[pltpu.VMEM((qc, D_BRANCH), F32), pltpu.VMEM((qc, D_BC), F32),
                        pltpu.VMEM((N_GROUPS, D_STATE, GROUP_W), F32)],
        compiler_params=_params(2),
    )(proj, proj, proj, conv_w, conv_b, dtb, alog, dskip)


def _ssd_bwd(proj, d_y, states, conv_w, conv_b, dtb, alog, dskip, nb, seq, slabs=()):
    qc = SSD_CHUNK
    nc = seq // qc

    def body(xs_ref, bc_ref, dt_ref, pxs_ref, pbc_ref, dy_ref, st_ref, stn_ref,
             cw_ref, cb_ref, dtb_ref, al_ref, ds_ref,
             dx_ref, gcw_ref, gcb_ref, gdtb_ref, gal_ref, gds_ref,
             dst_ref, nxs_ref, nbc_ref, yd_ref, dxt_ref):
        step = pl.program_id(1)
        chunk = nc - 1 - step

        @pl.when(step == 0)
        def _():
            dst_ref[...] = jnp.zeros_like(dst_ref)
            nxs_ref[...] = jnp.zeros_like(nxs_ref)
            nbc_ref[...] = jnp.zeros_like(nbc_ref)

        @pl.when((pl.program_id(0) == 0) & (step == 0))
        def _():
            gcw_ref[...] = jnp.zeros_like(gcw_ref)
            gcb_ref[...] = jnp.zeros_like(gcb_ref)
            gdtb_ref[...] = jnp.zeros_like(gdtb_ref)
            gal_ref[...] = jnp.zeros_like(gal_ref)
            gds_ref[...] = jnp.zeros_like(gds_ref)

        expand = _head_expand()
        collapse = lambda v: _dot01(v, expand, 2, _NT)
        first = jnp.where(chunk == 0, 0.0, 1.0)
        xs_raw = xs_ref[...]
        bc_raw = bc_ref[...]
        pxs = pxs_ref[...] * first
        pbc = pbc_ref[...] * first
        cw = cw_ref[...]
        cb = cb_ref[...]
        taps_xs = _conv_taps(xs_raw, pxs)
        taps_bc = _conv_taps(bc_raw, pbc)
        xs, dsilu_xs = _silu_and_grad(_conv_pre(taps_xs, cw[:, :D_BRANCH], cb[:, :D_BRANCH]))
        bc, dsilu_bc = _silu_and_grad(_conv_pre(taps_bc, cw[:, D_BRANCH:], cb[:, D_BRANCH:]))

        dt_in = dt_ref[...] + dtb_ref[...]
        dt, d_a, a_cs, dt_x, acs_x, tril = _chunk_decay(
            dt_ref[...], dtb_ref[...], al_ref[...], expand, qc)
        a_cst = a_cs.T
        aend_x = acs_x[qc - 1:qc, :]
        ea_x = jnp.exp(acs_x)
        dec_x = jnp.exp(aend_x - acs_x)
        xt = xs * dt_x
        xtb = xt.astype(BF16)
        xdb = (xt * dec_x).astype(BF16)
        d_x = _dot01(jnp.broadcast_to(ds_ref[...], (8, LANES)), expand, 3)[0:1, :]

        dy = dy_ref[...]
        dyb = dy.astype(BF16)
        dyeab = (dy * ea_x).astype(BF16)
        gds_ref[...] += collapse(jnp.broadcast_to(jnp.sum(dy * xs, axis=0, keepdims=True),
                                                  (8, D_BRANCH)))[0:1, :]

        d_bc = []
        d_cc = []
        y_offs = []
        dxt_states = []
        end_terms = []
        for g in range(N_GROUPS):
            gs = slice(g * GROUP_W, (g + 1) * GROUP_W)
            bg = bc[:, g * D_STATE:(g + 1) * D_STATE]
            cg = bc[:, (N_GROUPS + g) * D_STATE:(N_GROUPS + g + 1) * D_STATE]
            bgb = bg.astype(BF16)
            cgb = cg.astype(BF16)
            cbm = _dot(cgb, bgb, _NT)
            st_in = st_ref[0, 0, g]
            st_inb = st_in.astype(BF16)
            d_st = dst_ref[g]
            d_stb = d_st.astype(BF16)
            y_offs.append(_dot(cgb, st_inb) * ea_x[:, gs])
            dxt_states.append(_dot(bgb, d_stb) * dec_x[:, gs])
            d_c = _dot(dyeab[:, gs], st_inb, _NT)
            d_b = _dot(xdb[:, gs], d_stb, _NT)
            d_cb = jnp.zeros((qc, qc), F32)
            for k in range(HEADS_PER_GROUP):
                h = g * HEADS_PER_GROUP + k
                hs = slice(h * HEAD_DIM, (h + 1) * HEAD_DIM)
                seg = a_cs[:, h:h + 1] - a_cst[h:h + 1, :]
                lh = jnp.exp(jnp.where(tril, seg, -1e30))
                ghb = (cbm * lh).astype(BF16)
                d_cb = d_cb + _dot(dyb[:, hs], xtb[:, hs], _NT) * lh
                yd_ref[:, hs] = _dot(ghb, xtb[:, hs])
                dxt_ref[:, hs] = _dot(ghb, dyb[:, hs], _TN)
            d_cbb = d_cb.astype(BF16)
            d_cc.append(d_c + _dot(d_cbb, bgb))
            d_bc.append(d_b + _dot(d_cbb, cgb, _TN))
            end_terms.append(jnp.sum(d_st * stn_ref[0, 0, g], axis=0, keepdims=True))
            dst_ref[g] = d_st * jnp.exp(aend_x[:, gs]) + _dot(cg.T.astype(BF16), dyeab[:, gs])

        y_off = jnp.concatenate(y_offs, axis=1)
        dxt_state = jnp.concatenate(dxt_states, axis=1)
        dxt = dxt_ref[...] + dxt_state
        last = jnp.where(chunk == nc - 1, 0.0, 1.0)
        end_c = collapse(jnp.broadcast_to(jnp.concatenate(end_terms, axis=1), (8, D_BRANCH)))[0:1, :]
        da_cs = collapse(dyb.astype(F32) * yd_ref[...] - dxt_ref[...] * xtb.astype(F32)
                         + dy * y_off - dxt_state * xt)
        da_cs = da_cs + jnp.where(_row_iota(da_cs.shape) == qc - 1, end_c * last, 0.0)
        triu = lax.broadcasted_iota(jnp.int32, (qc, qc), 0) <= lax.broadcasted_iota(jnp.int32, (qc, qc), 1)
        dd_a = _dot01(da_cs, jnp.where(triu, 1.0, 0.0).astype(BF16), 3, m_left=True)
        ddt = dd_a * (-jnp.exp(al_ref[...])) + collapse(dxt * xs)
        head_lanes = _lane_iota(ddt.shape) < N_HEADS
        ddt_raw = jnp.where(head_lanes, ddt * _sigmoid(dt_in), 0.0)
        gal_ref[...] += jnp.sum(jnp.where(head_lanes, dd_a * d_a, 0.0), axis=0, keepdims=True)
        gdtb_ref[...] += jnp.sum(ddt_raw, axis=0, keepdims=True)

        dpre_xs = (dxt * dt_x + d_x * dy) * dsilu_xs
        dpre_bc = jnp.concatenate(d_bc + d_cc, axis=1) * dsilu_bc
        gcb_ref[...] += jnp.concatenate([jnp.sum(dpre_xs, axis=0, keepdims=True),
                                         jnp.sum(dpre_bc, axis=0, keepdims=True)], axis=1)
        nxs = nxs_ref[...]
        nbc = nbc_ref[...]
        du_xs = jnp.zeros_like(dpre_xs)
        du_bc = jnp.zeros_like(dpre_bc)
        for i in range(CONV_TAPS):
            k = CONV_TAPS - 1 - i
            gcw_ref[i:i + 1, :] += jnp.concatenate(
                [jnp.sum(dpre_xs * taps_xs[i], axis=0, keepdims=True),
                 jnp.sum(dpre_bc * taps_bc[i], axis=0, keepdims=True)], axis=1)
            du_xs = du_xs + _shift_up(dpre_xs, nxs, k) * cw[i:i + 1, :D_BRANCH]
            du_bc = du_bc + _shift_up(dpre_bc, nbc, k) * cw[i:i + 1, D_BRANCH:]
        nxs_ref[...] = dpre_xs
        nbc_ref[...] = dpre_bc

        dx_ref[:, :D_BRANCH] = du_xs.astype(BF16)
        dx_ref[:, D_BRANCH:D_CONV] = du_bc.astype(BF16)
        dx_ref[:, D_CONV:D_CONV + LANES] = ddt_raw.astype(BF16)
        dx_ref[:, D_CONV + LANES:] = jnp.zeros((qc, 2048 - D_CONV - LANES), BF16)

    rev = lambda b, c: b * nc + (nc - 1 - c)
    prv = lambda b, c: b * nc + jnp.maximum(nc - 2 - c, 0)
    nblk = lambda w, off, f: pl.BlockSpec((qc, w), lambda b, c: (f(b, c), off))
    full = lambda r, w: pl.BlockSpec((r, w), lambda b, c: (0, 0))
    st_spec = lambda f: pl.BlockSpec((1, 1, N_GROUPS, D_STATE, GROUP_W),
                                     lambda b, c: (b, f(c), 0, 0, 0))
    return _call_with_exchange(
        body, (proj, proj, proj, proj, proj, d_y, states, states, conv_w, conv_b, dtb, alog, dskip),
        slabs, (True,) * len(slabs), name="ssd_bwd", grid=(nb, nc),
        in_specs=[nblk(D_BRANCH, COL_XS // D_BRANCH, rev), nblk(D_BC, COL_BC // D_BC, rev),
                  nblk(LANES, COL_DT // LANES, rev),
                  nblk(D_BRANCH, COL_XS // D_BRANCH, prv), nblk(D_BC, COL_BC // D_BC, prv),
                  nblk(D_BRANCH, 0, rev),
                  st_spec(lambda c: nc - 1 - c), st_spec(lambda c: jnp.minimum(nc - c, nc - 1)),
                  full(CONV_TAPS, D_CONV), full(1, D_CONV), full(1, LANES), full(1, LANES),
                  full(1, LANES)],
        out_specs=[nblk(2048, 0, rev), full(8, D_CONV), full(1, D_CONV), full(1, LANES),
                   full(1, LANES), full(1, LANES)],
        out_shape=[jax.ShapeDtypeStruct((nb * seq, 2048), BF16),
                   jax.ShapeDtypeStruct((8, D_CONV), F32), jax.ShapeDtypeStruct((1, D_CONV), F32),
                   jax.ShapeDtypeStruct((1, LANES), F32), jax.ShapeDtypeStruct((1, LANES), F32),
                   jax.ShapeDtypeStruct((1, LANES), F32)],
        scratch_shapes=[pltpu.VMEM((N_GROUPS, D_STATE, GROUP_W), F32),
                        pltpu.VMEM((qc, D_BRANCH), F32), pltpu.VMEM((qc, D_BC), F32),
                        pltpu.VMEM((qc, D_BRANCH), F32), pltpu.VMEM((qc, D_BRANCH), F32)])


def _mid(o_sb, y_ssd, proj, x2, target, sb_w, ssd_w, w_out_b, tm):
    t = x2.shape[0]
    inv_d = 1.0 / D_MODEL

    def body(o_ref, y_ref, zsb_ref, zssd_ref, x_ref, tg_ref, sbw_ref, ssdw_ref, w_ref,
             dout_ref, dosb_ref, dy_ref, dz_ref, gw_ref, gsb_ref, gssd_ref, loss_ref):
        @pl.when(pl.program_id(0) == 0)
        def _():
            gw_ref[...] = jnp.zeros_like(gw_ref)
            gsb_ref[...] = jnp.zeros_like(gsb_ref)
            gssd_ref[...] = jnp.zeros_like(gssd_ref)
            loss_ref[...] = jnp.zeros_like(loss_ref)

        def branch(val, z, w):
            gate, dgate = _silu_and_grad(z)
            g = val * gate
            r = lax.rsqrt(jnp.mean(g * g, axis=1, keepdims=True) + EPS)
            xhat = g * r
            return (gate, dgate, r, xhat), (xhat * w).astype(BF16)

        o = o_ref[...]
        y = y_ref[...]
        saved_a, mix_a = branch(o, zsb_ref[...], sbw_ref[...])
        saved_b, mix_b = branch(y, zssd_ref[...], ssdw_ref[...])
        out = x_ref[...] + _dot(mix_a, w_ref[:D_BRANCH, :]) + _dot(mix_b, w_ref[D_BRANCH:, :])
        diff = out - tg_ref[...]
        loss_ref[...] += 0.5 * inv_d * jnp.sum(diff * diff)
        d_out = diff * inv_d
        dout_ref[...] = d_out
        d_outb = d_out.astype(BF16)
        gw_ref[:D_BRANCH, :] += _dot(mix_a, d_outb, _TN)
        gw_ref[D_BRANCH:, :] += _dot(mix_b, d_outb, _TN)

        def branch_bwd(dmix, val, w, saved):
            gate, dgate, r, xhat = saved
            gg = dmix * w
            m = jnp.mean(gg * xhat, axis=1, keepdims=True)
            dg = r * (gg - xhat * m)
            return dg * gate, dg * val * dgate, jnp.sum(dmix * xhat, axis=0, keepdims=True)

        dmix_a = _dot(d_outb, w_ref[:D_BRANCH, :], _NT)
        dmix_b = _dot(d_outb, w_ref[D_BRANCH:, :], _NT)
        d_o, dz_a, gsb = branch_bwd(dmix_a, o, sbw_ref[...], saved_a)
        d_y, dz_b, gssd = branch_bwd(dmix_b, y, ssdw_ref[...], saved_b)
        dosb_ref[...] = d_o
        dy_ref[...] = d_y
        dz_ref[:, :D_BRANCH] = dz_a.astype(BF16)
        dz_ref[:, D_BRANCH:] = dz_b.astype(BF16)
        gsb_ref[...] += gsb
        gssd_ref[...] += gssd

    row = lambda w, off: pl.BlockSpec((tm, w), lambda i: (i, off))
    full = lambda r, w: pl.BlockSpec((r, w), lambda i: (0, 0))
    resident = pl.BlockSpec((2 * D_BRANCH, D_MODEL), lambda i: (0, 0), pipeline_mode=pl.Buffered(1))
    tok = jax.ShapeDtypeStruct((t, D_MODEL), F32)
    return pl.pallas_call(
        body, name="mid",
        grid=(t // tm,),
        in_specs=[row(D_BRANCH, 0), row(D_BRANCH, 0), row(D_BRANCH, 3), row(D_BRANCH, 4),
                  row(D_MODEL, 0), row(D_MODEL, 0), full(1, D_BRANCH), full(1, D_BRANCH),
                  resident],
        out_specs=[row(D_MODEL, 0), row(D_BRANCH, 0), row(D_BRANCH, 0), row(2 * D_BRANCH, 0),
                   resident, full(1, D_BRANCH), full(1, D_BRANCH),
                   full(1, LANES)],
        out_shape=[tok, tok, tok, jax.ShapeDtypeStruct((t, 2 * D_BRANCH), BF16),
                   jax.ShapeDtypeStruct((2 * D_BRANCH, D_MODEL), F32),
                   jax.ShapeDtypeStruct((1, D_BRANCH), F32), jax.ShapeDtypeStruct((1, D_BRANCH), F32),
                   jax.ShapeDtypeStruct((1, LANES), F32)],
        compiler_params=_params(1),
    )(o_sb, y_ssd, proj, proj, x2, target, sb_w, ssd_w, w_out_b)


_DPROJ_FIRST = (0, 1, 2, 3, 5)
_DPROJ_BLOCKS = (1, 1, 1, 2, 2)


def _in_proj_bwd_x(d_parts, w_in_t, x2, d_out, norm_w, tm, slabs=()):
    t = x2.shape[0]
    n_parts = len(d_parts)

    def body(*refs):
        dp_refs = refs[:n_parts]
        w_ref, x_ref, dout_ref, nw_ref, gx_ref, gnw_ref = refs[n_parts:]

        @pl.when(pl.program_id(0) == 0)
        def _():
            gnw_ref[...] = jnp.zeros_like(gnw_ref)

        d_hn = None
        for p in range(n_parts):
            rows = slice(_DPROJ_FIRST[p] * COLBLK, (_DPROJ_FIRST[p] + _DPROJ_BLOCKS[p]) * COLBLK)
            term = _dot(dp_refs[p][...], w_ref[rows, :])
            d_hn = term if d_hn is None else d_hn + term
        xf = x_ref[...]
        r = lax.rsqrt(jnp.mean(xf * xf, axis=1, keepdims=True) + EPS)
        xhat = xf * r
        g = d_hn * nw_ref[...]
        m = jnp.mean(g * xhat, axis=1, keepdims=True)
        gx_ref[...] = dout_ref[...] + r * (g - xhat * m)
        gnw_ref[...] += jnp.sum(d_hn * xhat, axis=0, keepdims=True)

    row = lambda w: pl.BlockSpec((tm, w), lambda i: (i, 0))
    return _call_with_exchange(
        body, (*d_parts, w_in_t, x2, d_out, norm_w), slabs, (True,) * len(slabs),
        name="in_proj_bwd_x", grid=(t // tm,),
        in_specs=[row(COLBLK * _DPROJ_BLOCKS[p]) for p in range(n_parts)] + [
                  pl.BlockSpec((D_IN_PAD, D_MODEL), lambda i: (0, 0), pipeline_mode=pl.Buffered(1)),
                  row(D_MODEL), row(D_MODEL), pl.BlockSpec((1, D_MODEL), lambda i: (0, 0))],
        out_specs=[row(D_MODEL), pl.BlockSpec((1, D_MODEL), lambda i: (0, 0))],
        out_shape=[jax.ShapeDtypeStruct((t, D_MODEL), F32), jax.ShapeDtypeStruct((1, D_MODEL), F32)])


def _in_proj_bwd_w(hn, d_parts, tm):
    t = hn.shape[0]
    n_parts = len(d_parts)

    def body(hn_ref, *refs):
        dp_refs, gw_ref = refs[:n_parts], refs[n_parts]

        @pl.when(pl.program_id(0) == 0)
        def _():
            gw_ref[...] = jnp.zeros_like(gw_ref)

        hnt = hn_ref[...].astype(F32).T.astype(BF16)
        for p in range(n_parts):
            cols = slice(_DPROJ_FIRST[p] * COLBLK, (_DPROJ_FIRST[p] + _DPROJ_BLOCKS[p]) * COLBLK)
            gw_ref[:, cols] += _dot(hnt, dp_refs[p][...])

    return pl.pallas_call(
        body, name="in_proj_bwd_w",
        grid=(t // tm,),
        in_specs=[pl.BlockSpec((tm, D_MODEL), lambda i: (i, 0))]
                 + [pl.BlockSpec((tm, COLBLK * _DPROJ_BLOCKS[p]), lambda i: (i, 0))
                    for p in range(n_parts)],
        out_specs=pl.BlockSpec((D_MODEL, D_IN_PAD), lambda i: (0, 0), pipeline_mode=pl.Buffered(1)),
        out_shape=jax.ShapeDtypeStruct((D_MODEL, D_IN_PAD), F32),
        compiler_params=_params(1),
    )(hn, *d_parts)


def _adamw(parts, w, m, v, tr, name):
    _, rows, cols = w.shape
    c1 = 1.0 - ADAM_B1 ** ADAM_STEP
    c2 = 1.0 - ADAM_B2 ** ADAM_STEP

    def body(p_ref, w_ref, m_ref, v_ref, g_ref, d_ref, nm_ref, nv_ref):
        g = p_ref[0].astype(F32)
        for s in range(1, N_DEV):
            g = g + p_ref[s].astype(F32)
        nm = ADAM_B1 * m_ref[0] + (1.0 - ADAM_B1) * g
        nv = ADAM_B2 * v_ref[0] + (1.0 - ADAM_B2) * (g * g)
        g_ref[0] = g
        nm_ref[0] = nm
        nv_ref[0] = nv
        d_ref[0] = -ADAM_LR * ((nm / c1) / (jnp.sqrt(nv / c2) + ADAM_EPS) + ADAM_WD * w_ref[0])

    blk = pl.BlockSpec((1, tr, cols), lambda i: (0, i, 0))
    shape = jax.ShapeDtypeStruct((1, rows, cols), F32)
    return pl.pallas_call(
        body, name=name,
        grid=(rows // tr,),
        in_specs=[pl.BlockSpec((N_DEV, tr, cols), lambda i: (0, i, 0)), blk, blk, blk],
        out_specs=[blk, blk, blk, blk],
        out_shape=[shape, shape, shape, shape],
        compiler_params=_params(1),
    )(parts, w, m, v)


def _mesh_place():
    x, y, c = lax.axis_index("x"), lax.axis_index("y"), lax.axis_index("c")
    return x, y, c, 4 * x + 2 * y + c


def _peer(x, y, c, k):
    px = 1 - x if k & 4 else x
    py = 1 - y if k & 2 else y
    pc = 1 - c if k & 1 else c
    return (px, py, pc), 4 * px + 2 * py + pc


def _exchange(srcs, scatter, name):
    n = len(srcs)

    def body(*refs):
        copies = _exchange_copies(refs[:n], refs[n:2 * n], scatter, *refs[2 * n:])
        _exchange_start(copies)
        _exchange_wait(copies)

    return pl.pallas_call(
        body, name=name,
        in_specs=[_ANY] * n, out_specs=[_ANY] * n, out_shape=_exchange_shapes(srcs, scatter),
        scratch_shapes=_exchange_sems(n),
    )(*srcs)


def _call_with_exchange(body, operands, srcs, scatter, *, name, grid, in_specs, out_specs,
                        out_shape, scratch_shapes=()):
    n_in, n_out, n_scr, n_x = len(in_specs), len(out_specs), len(scratch_shapes), len(srcs)
    params = _params(len(grid))
    if not n_x:
        return pl.pallas_call(body, name=name, grid=grid, in_specs=list(in_specs),
                              out_specs=list(out_specs), out_shape=list(out_shape),
                              scratch_shapes=list(scratch_shapes), compiler_params=params)(*operands)

    def wrapped(*refs):
        ins, refs = refs[:n_in], refs[n_in:]
        x_src, refs = refs[:n_x], refs[n_x:]
        outs, refs = refs[:n_out], refs[n_out:]
        x_dst, refs = refs[:n_x], refs[n_x:]
        scratch, sems = refs[:n_scr], refs[n_scr:]
        ids = [pl.program_id(a) for a in range(len(grid))]
        first = functools.reduce(jnp.logical_and, [i == 0 for i in ids])
        last = functools.reduce(jnp.logical_and, [i == n - 1 for i, n in zip(ids, grid)])

        @pl.when(first)
        def _():
            _exchange_start(_exchange_copies(x_src, x_dst, scatter, *sems))

        body(*ins, *outs, *scratch)

        @pl.when(last)
        def _():
            _exchange_wait(_exchange_copies(x_src, x_dst, scatter, *sems))

    return pl.pallas_call(
        wrapped, name=name, grid=grid,
        in_specs=list(in_specs) + [_ANY] * n_x, out_specs=list(out_specs) + [_ANY] * n_x,
        out_shape=list(out_shape) + _exchange_shapes(srcs, scatter),
        scratch_shapes=list(scratch_shapes) + _exchange_sems(n_x), compiler_params=params,
    )(*operands, *srcs)


def _gather_two_level(shard, name):
    def body(x_ref, out_ref, send_sems, recv_sems, local_sem):
        x, y, c, me = _mesh_place()
        sibling = (x, y, 1 - c)
        chips = [(1 - x, y), (x, 1 - y), (1 - x, 1 - y)]

        def slab(px, py, pc):
            return out_ref.at[4 * px + 2 * py + pc]

        def copy(k, block, to, src=None):
            return pltpu.make_async_remote_copy(
                src_ref=slab(*block) if src is None else src, dst_ref=slab(*block),
                send_sem=send_sems.at[k], recv_sem=recv_sems.at[k],
                device_id=to, device_id_type=pl.DeviceIdType.MESH)

        mine = pltpu.make_async_copy(x_ref, slab(x, y, c), local_sem)
        mine.start()
        first = [copy(0, (x, y, c), sibling, src=x_ref)]
        first += [copy(1 + j, (x, y, c), (*chip, c), src=x_ref) for j, chip in enumerate(chips)]
        for cp in first:
            cp.start()
        passed = [copy(4 + j, (*chip, c), sibling) for j, chip in enumerate(chips)]
        for j, chip in enumerate(chips):
            copy(1 + j, (*chip, c), (x, y, c)).wait_recv()
            passed[j].start()
        copy(0, sibling, (x, y, c)).wait_recv()
        for j, chip in enumerate(chips):
            copy(4 + j, (*chip, 1 - c), (x, y, c)).wait_recv()
        for cp in first + passed:
            cp.wait_send()
        mine.wait()

    return pl.pallas_call(
        body, name=name,
        in_specs=[_ANY], out_specs=_ANY,
        out_shape=jax.ShapeDtypeStruct((N_DEV,) + shard.shape, shard.dtype),
        scratch_shapes=[pltpu.SemaphoreType.DMA((N_DEV - 1,)), pltpu.SemaphoreType.DMA((N_DEV - 1,)),
                        pltpu.SemaphoreType.DMA],
    )(shard)


_ANY = pl.BlockSpec(memory_space=pl.ANY)


def _exchange_shapes(srcs, scatter):
    return [jax.ShapeDtypeStruct(s.shape if sc else (N_DEV,) + s.shape, s.dtype)
            for s, sc in zip(srcs, scatter)]


def _exchange_sems(n):
    return [pltpu.SemaphoreType.DMA((n * (N_DEV - 1),)),
            pltpu.SemaphoreType.DMA((n * (N_DEV - 1),)),
            pltpu.SemaphoreType.DMA((n,))]


def _exchange_copies(src_refs, dst_refs, scatter, send_sems, recv_sems, loc_sems):
    n = len(src_refs)
    x, y, c, me = _mesh_place()

    def src_of(i, idx):
        return src_refs[i].at[idx] if scatter[i] else src_refs[i]

    local = [pltpu.make_async_copy(src_of(i, me), dst_refs[i].at[me], loc_sems.at[i])
             for i in range(n)]
    sends, recvs = [], []
    for k in range(1, N_DEV):
        peer, pidx = _peer(x, y, c, k)
        for i in range(n):
            s = i * (N_DEV - 1) + k - 1
            for dst_slab, group in ((me, sends), (pidx, recvs)):
                group.append(pltpu.make_async_remote_copy(
                    src_ref=src_of(i, pidx), dst_ref=dst_refs[i].at[dst_slab],
                    send_sem=send_sems.at[s], recv_sem=recv_sems.at[s],
                    device_id=peer, device_id_type=pl.DeviceIdType.MESH))
    return local, sends, recvs


def _exchange_start(copies):
    local, sends, _ = copies
    for cp in local + sends:
        cp.start()


def _exchange_wait(copies):
    local, sends, recvs = copies
    for cp in recvs:
        cp.wait_recv()
    for cp in sends:
        cp.wait_send()
    for cp in local:
        cp.wait()


def _pad_lanes(v, width=LANES):
    return jnp.pad(v, ((0, 0), (0, width - v.shape[1])))


def _local_step(x, target, norm_w, w_in_b, q_norm_w, k_norm_w, conv_w, conv_b, dt_bias, a_log,
                d_skip, sb_norm_w, ssd_norm_w, w_out_b, tm=256, tq=512, tmid=256, blk=ATT_BLK,
                scatter=False, w_in_t=None):
    nb, seq, _ = x.shape
    t = nb * seq
    x2 = x.reshape(t, D_MODEL)
    tg2 = target.reshape(t, D_MODEL)
    qw2 = jnp.tile(q_norm_w, (1, 2))
    kw2 = jnp.tile(k_norm_w, (1, 2))
    dtb, alog, dsk = _pad_lanes(dt_bias), _pad_lanes(a_log), _pad_lanes(d_skip)

    if w_in_t is None:
        w_in_t = w_in_b.T
    tall = min(2 * tm, t)
    if scatter:
        proj, hn, wout_all, cw_all = _in_proj(x2, norm_w, w_in_b, tall, (w_out_b, conv_w))
        w_out_b = wout_all.reshape(2 * D_BRANCH, D_MODEL)
        conv_w = jnp.transpose(cw_all, (1, 0, 2)).reshape(CONV_TAPS, D_CONV)
    else:
        proj, hn = _in_proj(x2, norm_w, w_in_b, tall)
    qs, kn, vb, kt, ksq = _qk_prep(proj, qw2, kw2, nb, seq, min(2 * tq, seq))
    o_sb, sb_tot, sb_low = _attn_fwd(qs, kn, vb, jnp.max(ksq, axis=1), nb, seq, blk)
    y_ssd, states = _ssd_fwd(proj, conv_w, conv_b, dtb, alog, dsk, nb, seq)
    d_out, d_osb, d_y, d_z, g_wout, g_sbw, g_ssdw, loss = _mid(
        o_sb, y_ssd, proj, x2, tg2, sb_norm_w, ssd_norm_w, w_out_b, tmid)
    dqs, dkn, dvh = _attn_bwd(qs, kn, kt, vb, sb_tot, sb_low, d_osb, nb, seq, blk)
    dq_raw, dk_raw, dv_raw, g_qw, g_kw = _qk_bwd(proj, dqs, dkn, dvh, qw2, kw2, nb, seq, tq)
    wout_slabs = (g_wout.reshape(N_DEV, 2 * D_BRANCH // N_DEV, D_MODEL).astype(BF16),)
    d_xbc, g_cw, g_cb, g_dtb, g_alog, g_dsk, *moved = _ssd_bwd(
        proj, d_y, states, conv_w, conv_b, dtb, alog, dsk, nb, seq, wout_slabs if scatter else ())
    d_parts = [dq_raw, dk_raw, dv_raw, d_z, d_xbc]
    g_win = _in_proj_bwd_w(hn, d_parts, tall)[:, :D_IN]
    g_cw = g_cw[:CONV_TAPS]
    if scatter:
        g_wout, = moved
        grad_x, g_nw, g_win, g_cw = _in_proj_bwd_x(
            d_parts, w_in_t, x2, d_out, norm_w, tall, _grad_slabs(g_win, g_cw))
    else:
        grad_x, g_nw = _in_proj_bwd_x(d_parts, w_in_t, x2, d_out, norm_w, tall)

    small = dict(
        norm_w=g_nw,
        q_norm_w=g_qw[:, :HEAD_DIM] + g_qw[:, HEAD_DIM:],
        k_norm_w=g_kw[:, :HEAD_DIM] + g_kw[:, HEAD_DIM:],
        conv_b=g_cb, dt_bias=g_dtb[:, :N_HEADS], A_log=g_alog[:, :N_HEADS],
        D_skip=g_dsk[:, :N_HEADS], sb_norm_w=g_sbw, ssd_norm_w=g_ssdw)
    return loss[0, 0], grad_x.reshape(nb, seq, D_MODEL), g_win, g_wout, g_cw, small


def _grad_slabs(g_win, g_cw):
    w_sh = D_IN // N_DEV
    c_sh = D_CONV // N_DEV
    return (jnp.transpose(g_win.reshape(D_MODEL, N_DEV, w_sh), (1, 0, 2)).astype(BF16),
            jnp.pad(jnp.transpose(g_cw.reshape(CONV_TAPS, N_DEV, c_sh), (1, 0, 2)),
                    ((0, 0), (0, 8 - CONV_TAPS), (0, 0))))


_SMALL = ("norm_w", "q_norm_w", "k_norm_w", "conv_b", "dt_bias", "A_log", "D_skip",
          "sb_norm_w", "ssd_norm_w")


def _pack_small(vals):
    flat = jnp.concatenate([_pad_lanes(vals[n], -(-vals[n].shape[1] // LANES) * LANES)
                            for n in _SMALL], axis=1)
    return jnp.pad(flat, ((0, 0), (0, 48 * LANES - flat.shape[1]))).reshape(48, LANES)


def _unpack_small(packed, like):
    out, r = {}, 0
    for n in _SMALL:
        width = like[n].shape[1]
        nr = -(-width // LANES)
        out[n] = packed[r:r + nr].reshape(1, nr * LANES)[:, :width]
        r += nr
    return out


def kernel(x, norm_w, w_in, q_norm_w, k_norm_w, conv_w, conv_b, dt_bias, A_log, D_skip, sb_norm_w, ssd_norm_w, w_out, loss_target, m_norm_w, m_w_in, m_q_norm_w, m_k_norm_w, m_conv_w, m_conv_b, m_dt_bias, m_A_log, m_D_skip, m_sb_norm_w, m_ssd_norm_w, m_w_out, v_norm_w, v_w_in, v_q_norm_w, v_k_norm_w, v_conv_w, v_conv_b, v_dt_bias, v_A_log, v_D_skip, v_sb_norm_w, v_ssd_norm_w, v_w_out):
    win_all = _gather_two_level(w_in[0].astype(BF16), "gather_w_in")
    w_in_b = jnp.pad(jnp.transpose(win_all, (1, 0, 2)).reshape(D_MODEL, D_IN),
                     ((0, 0), (0, D_IN_PAD - D_IN)))
    w_in_t = jnp.pad(jnp.transpose(win_all, (0, 2, 1)).reshape(D_IN, D_MODEL),
                     ((0, D_IN_PAD - D_IN), (0, 0)))

    loss, grad_x, win_parts, wout_parts, cw_parts, g_small = _local_step(
        x, loss_target, norm_w, w_in_b, q_norm_w, k_norm_w, conv_w[0], conv_b, dt_bias, A_log,
        D_skip, sb_norm_w, ssd_norm_w, w_out[0].astype(BF16), scatter=True, w_in_t=w_in_t)
    packed = _pack_small(g_small).at[-1, 0].set(loss)
    small_parts, = _exchange([packed], [False], "gather_small_grads")
    loss = jnp.sum(small_parts[:, -1, 0])

    small_w = dict(norm_w=norm_w, q_norm_w=q_norm_w, k_norm_w=k_norm_w, conv_b=conv_b,
                   dt_bias=dt_bias, A_log=A_log, D_skip=D_skip, sb_norm_w=sb_norm_w,
                   ssd_norm_w=ssd_norm_w)
    small_m = dict(norm_w=m_norm_w, q_norm_w=m_q_norm_w, k_norm_w=m_k_norm_w, conv_b=m_conv_b,
                   dt_bias=m_dt_bias, A_log=m_A_log, D_skip=m_D_skip, sb_norm_w=m_sb_norm_w,
                   ssd_norm_w=m_ssd_norm_w)
    small_v = dict(norm_w=v_norm_w, q_norm_w=v_q_norm_w, k_norm_w=v_k_norm_w, conv_b=v_conv_b,
                   dt_bias=v_dt_bias, A_log=v_A_log, D_skip=v_D_skip, sb_norm_w=v_sb_norm_w,
                   ssd_norm_w=v_ssd_norm_w)

    pad8 = lambda a: jnp.pad(a, ((0, 0), (0, 8 - CONV_TAPS), (0, 0)))
    r_win = _adamw(win_parts, w_in, m_w_in, v_w_in, 128, "adamw_w_in")
    r_wout = _adamw(wout_parts, w_out, m_w_out, v_w_out, 128, "adamw_w_out")
    r_cw = _adamw(cw_parts, pad8(conv_w), pad8(m_conv_w), pad8(v_conv_w), 8, "adamw_conv_w")
    r_small = _adamw(small_parts, _pack_small(small_w)[None], _pack_small(small_m)[None],
                     _pack_small(small_v)[None], 48, "adamw_small")

    res = {"w_in": r_win, "w_out": r_wout, "conv_w": [a[:, :CONV_TAPS] for a in r_cw]}
    unpacked = [_unpack_small(a[0], small_w) for a in r_small]
    for n in _SMALL:
        res[n] = [u[n] for u in unpacked]
    order = ("norm_w", "w_in", "q_norm_w", "k_norm_w", "conv_w", "conv_b", "dt_bias", "A_log",
             "D_skip", "sb_norm_w", "ssd_norm_w", "w_out")
    outs = [loss, grad_x]
    for kind in range(4):
        outs += [res[n][kind] for n in order]
    return tuple(outs)
```

```python
import functools
import math

import jax
import jax.numpy as jnp
from jax import lax
from jax.experimental import pallas as pl
from jax.experimental.pallas import tpu as pltpu

F32 = jnp.float32
BF16 = jnp.bfloat16

D_MODEL = 1024
N_HEADS = 16
HEAD_DIM = 64
N_PAIRS = N_HEADS // 2
D_BRANCH = 1024
N_GROUPS = 2
HEADS_PER_GROUP = 8
D_STATE = 128
GROUP_W = HEADS_PER_GROUP * HEAD_DIM
D_BC = 2 * N_GROUPS * D_STATE
D_CONV = D_BRANCH + D_BC
D_IN = 6672
COLBLK = 1024
D_IN_PAD = 7168
N_COLBLK = D_IN_PAD // COLBLK
COL_XS = 5120
COL_BC = 6144
COL_DT = 6656
EPS = 1e-6
CONV_TAPS = 4
N_DEV = 8

LANES = 128
SSD_CHUNK = 128
ATT_BLK = 256
ATT_HEADS = 4
ATT_W = ATT_HEADS * HEAD_DIM
N_ATT_GROUPS = N_HEADS // ATT_HEADS
EXP_UNDERFLOW = -105.0
VMEM_LIMIT = 56 * 1024 * 1024

ADAM_LR = 0.001
ADAM_B1 = 0.9
ADAM_B2 = 0.999
ADAM_EPS = 1e-08
ADAM_WD = 0.01
ADAM_STEP = 10

_NT = (((1,), (1,)), ((), ()))
_TN = (((0,), (0,)), ((), ()))


def _params(n_grid):
    return pltpu.CompilerParams(dimension_semantics=("arbitrary",) * n_grid,
                                vmem_limit_bytes=VMEM_LIMIT)


def _dot(a, b, dims=None):
    if dims is None:
        return jnp.dot(a, b, preferred_element_type=F32)
    return lax.dot_general(a, b, dims, preferred_element_type=F32)


def _sigmoid(x):
    return 1.0 / (1.0 + jnp.exp(-x))


def _softplus(x):
    return jnp.maximum(x, 0.0) + jnp.log(1.0 + jnp.exp(-jnp.abs(x)))


def _split_bf16(x):
    hi = x.astype(BF16)
    lo = (x - hi.astype(F32)).astype(BF16)
    return hi, lo


def _lane_iota(shape):
    return lax.broadcasted_iota(jnp.int32, shape, len(shape) - 1)


def _row_iota(shape):
    return lax.broadcasted_iota(jnp.int32, shape, len(shape) - 2)


def _pair_sum(x):
    r = lax.broadcasted_iota(jnp.int32, (LANES, LANES), 0)
    c = lax.broadcasted_iota(jnp.int32, (LANES, LANES), 1)
    same_head = jnp.where(r // HEAD_DIM == c // HEAD_DIM, 1.0, 0.0).astype(BF16)
    hi, lo = _split_bf16(x)
    return _dot(hi, same_head) + _dot(lo, same_head)


def _head_lanes(x, a):
    lane = _lane_iota(x.shape)
    mine = (lane >= a * HEAD_DIM) & (lane < (a + 1) * HEAD_DIM)
    return jnp.where(mine, x, jnp.zeros_like(x))


def _head_expand():
    r = lax.broadcasted_iota(jnp.int32, (LANES, D_BRANCH), 0)
    c = lax.broadcasted_iota(jnp.int32, (LANES, D_BRANCH), 1)
    return jnp.where(c // HEAD_DIM == r, 1.0, 0.0).astype(BF16)


def _in_proj(x2, norm_w, w_in_b, tm, shards=()):
    t = x2.shape[0]

    def body(x_ref, nw_ref, w_ref, proj_ref, hn_ref):
        xf = x_ref[...]
        r = lax.rsqrt(jnp.mean(xf * xf, axis=1, keepdims=True) + EPS)
        hn = (xf * r * nw_ref[...]).astype(BF16)
        hn_ref[...] = hn
        for j in range(N_COLBLK):
            cols = slice(j * COLBLK, (j + 1) * COLBLK)
            proj_ref[:, cols] = _dot(hn, w_ref[:, cols])

    return _call_with_exchange(
        body, (x2, norm_w, w_in_b), shards, (False,) * len(shards), name="in_proj",
        grid=(t // tm,),
        in_specs=[pl.BlockSpec((tm, D_MODEL), lambda i: (i, 0)),
                  pl.BlockSpec((1, D_MODEL), lambda i: (0, 0)),
                  pl.BlockSpec((D_MODEL, D_IN_PAD), lambda i: (0, 0), pipeline_mode=pl.Buffered(1))],
        out_specs=[pl.BlockSpec((tm, D_IN_PAD), lambda i: (i, 0)),
                   pl.BlockSpec((tm, D_MODEL), lambda i: (i, 0))],
        out_shape=[jax.ShapeDtypeStruct((t, D_IN_PAD), F32),
                   jax.ShapeDtypeStruct((t, D_MODEL), BF16)])


def _qk_prep(proj, qw2, kw2, nb, seq, tq):
    nl = seq // tq
    scale = 1.0 / math.sqrt(HEAD_DIM)

    def body(q_ref, k_ref, v_ref, qw_ref, kw_ref, qs_ref, kn_ref, vb_ref, kt_ref, ksq_ref):
        def norm(x, w):
            r = lax.rsqrt(_pair_sum(x * x) * (1.0 / HEAD_DIM) + EPS)
            return x * r * w

        vb_ref[...] = v_ref[...].astype(BF16)
        for p in range(N_PAIRS):
            cols = slice(p * LANES, (p + 1) * LANES)
            kn = norm(k_ref[:, cols], kw_ref[...])
            knb = kn.astype(BF16)
            qs_ref[:, cols] = (norm(q_ref[:, cols], qw_ref[...]) * scale).astype(BF16)
            kn_ref[:, cols] = knb
            kt_ref[0, p] = kn.T.astype(BF16)
            kf = knb.astype(F32)
            ksq_ref[0, 0, p:p + 1, :] = jnp.max(_pair_sum(kf * kf), axis=0, keepdims=True) * 1.0001

    tok_shape = jax.ShapeDtypeStruct((nb * seq, D_BRANCH), BF16)
    tok = lambda blk: pl.BlockSpec((tq, D_BRANCH), lambda b, i: (b * nl + i, blk))
    vec = pl.BlockSpec((1, LANES), lambda b, i: (0, 0))
    return pl.pallas_call(
        body, name="qk_prep",
        grid=(nb, nl),
        in_specs=[tok(0), tok(1), tok(2), vec, vec],
        out_specs=[tok(0), tok(0), tok(0),
                   pl.BlockSpec((1, N_PAIRS, LANES, tq), lambda b, i: (b, 0, 0, i)),
                   pl.BlockSpec((1, 1, N_PAIRS, LANES), lambda b, i: (b, i, 0, 0))],
        out_shape=[tok_shape, tok_shape, tok_shape,
                   jax.ShapeDtypeStruct((nb, N_PAIRS, LANES, seq), BF16),
                   jax.ShapeDtypeStruct((nb, nl, N_PAIRS, LANES), F32)],
        compiler_params=_params(2),
    )(proj, proj, proj, qw2, kw2)


def _attn_fwd(qs, kn, vb, ksq, nb, seq, blk):
    nq = seq // blk

    def body(q_ref, k_ref, v_ref, ksq_ref, o_ref, tot_ref, low_ref):
        qi = pl.program_id(2)
        r_i = lax.broadcasted_iota(jnp.int32, (blk, blk), 0)
        c_i = lax.broadcasted_iota(jnp.int32, (blk, blk), 1)
        csum = jnp.where(r_i >= c_i, 1.0, 0.0).astype(BF16)
        heads = range(ATT_HEADS)
        head = _head_lanes

        q_blk = q_ref[...]
        qf = q_blk.astype(F32)
        q_head = [head(q_blk, a) for a in heads]
        zmax = []
        for a in heads:
            qsq = jnp.sum(head(qf * qf, a), axis=1, keepdims=True)
            kmax = ksq_ref[0, 0, a // 2:a // 2 + 1, (a % 2) * HEAD_DIM:(a % 2) * HEAD_DIM + 1]
            zmax.append(1.01 * jnp.sqrt(qsq * kmax) + 0.01)

        def exhausted(run):
            top = functools.reduce(jnp.maximum, [jnp.max(run[a] + zmax[a]) for a in heads])
            return top < EXP_UNDERFLOW

        def sweep(blocks, run, acc):
            half = blk // 2

            def tiles(diag):
                return [(0, half, half), (half, half, blk)] if diag else [(0, blk, blk)]

            def keep(x, r0, diag):
                if diag:
                    rows = lax.broadcasted_iota(jnp.int32, x.shape, 0) + r0
                    x = jnp.where(lax.broadcasted_iota(jnp.int32, x.shape, 1) < rows, x, 0.0)
                return x

            offs = [pl.multiple_of(j * blk, blk) for j, _, _ in blocks]
            z = [[[_dot(q_head[a][r0:r0 + nr], k_ref[pl.ds(off, nk), :], _NT) for a in heads]
                  for r0, nr, nk in tiles(diag)] for (_, diag, _), off in zip(blocks, offs)]
            cl = [[[_dot(keep(-_softplus(zt[a]), r0, diag).astype(BF16), csum[:nk, :nk])
                    for a in heads]
                   for (r0, nr, nk), zt in zip(tiles(diag), zb)]
                  for (_, diag, _), zb in zip(blocks, z)]
            for (_, diag, valid), zb, clb, off in zip(blocks, z, cl, offs):
                rows_out, steps = [], [[] for _ in heads]
                for (r0, nr, nk), zt, clt in zip(tiles(diag), zb, clb):
                    v_blk = v_ref[pl.ds(off, nk), :]
                    part = None
                    for a in heads:
                        wa = keep(jnp.exp(zt[a] + clt[a] + run[a][r0:r0 + nr]), r0, diag)
                        term = _dot(wa.astype(BF16), head(v_blk, a))
                        part = term if part is None else part + term
                        steps[a].append(clt[a][:, 0:1])
                    rows_out.append(part)
                part = jnp.concatenate(rows_out, axis=0)
                steps = [jnp.concatenate(steps[a], axis=0) for a in heads]
                if valid is not None:
                    part = jnp.where(valid, part, 0.0)
                    steps = [jnp.where(valid, s, 0.0) for s in steps]
                acc = acc + part
                run = [run[a] + steps[a] for a in heads]
            return run, acc

        run = [jnp.zeros((blk, 1), F32)] * ATT_HEADS
        acc = jnp.zeros((blk, ATT_W), F32)
        run, acc = sweep([(qi, True, None), (jnp.maximum(qi - 1, 0), False, qi >= 1)], run, acc)
        low = jnp.maximum(qi - 1, 0)

        def more(carry):
            low, done, _, _ = carry
            return (low > 0) & jnp.logical_not(done)

        def pair(carry):
            low, _, run, acc = carry
            run, acc = sweep([(low - 1, False, None), (jnp.maximum(low - 2, 0), False, low >= 2)],
                             run, acc)
            return jnp.maximum(low - 2, 0), exhausted(run), run, acc

        low, _, run, acc = lax.while_loop(more, pair, (low, exhausted(run), run, acc))
        low_ref[pl.program_id(0) * N_ATT_GROUPS + pl.program_id(1), qi] = low.astype(F32)
        o_ref[...] = acc
        for a in heads:
            as_row = jnp.sum(jnp.where(r_i == c_i, run[a], 0.0), axis=0, keepdims=True)
            tot_ref[0, a, 0] = jnp.broadcast_to(as_row, (8, blk))

    return pl.pallas_call(
        body, name="sb_attn_fwd",
        grid=(nb, N_ATT_GROUPS, nq),
        in_specs=[pl.BlockSpec((blk, ATT_W), lambda b, h, i: (b * nq + i, h)),
                  pl.BlockSpec((seq, ATT_W), lambda b, h, i: (b, h)),
                  pl.BlockSpec((seq, ATT_W), lambda b, h, i: (b, h)),
                  pl.BlockSpec((1, 1, ATT_HEADS // 2, LANES), lambda b, h, i: (b, h, 0, 0))],
        out_specs=[pl.BlockSpec((blk, ATT_W), lambda b, h, i: (b * nq + i, h)),
                   pl.BlockSpec((1, ATT_HEADS, 1, 8, blk), lambda b, h, i: (b, h, i, 0, 0)),
                   pl.BlockSpec(memory_space=pltpu.SMEM)],
        out_shape=[jax.ShapeDtypeStruct((nb * seq, D_BRANCH), F32),
                   jax.ShapeDtypeStruct((nb, N_HEADS, nq, 8, blk), F32),
                   jax.ShapeDtypeStruct((nb * N_ATT_GROUPS, nq), F32)],
        compiler_params=_params(3),
    )(qs, kn, vb, ksq.reshape(nb, N_ATT_GROUPS, ATT_HEADS // 2, LANES))


def _attn_bwd(qs, kn, kt, vb, tot, low, d_o, nb, seq, blk):
    nq = seq // blk

    def body(q_ref, k_ref, kt_ref, v_ref, tot_ref, low_ref, do_ref, dq_ref, dk_ref, dvb_ref,
             dv_ref):
        qi = pl.program_id(2)

        @pl.when(qi == 0)
        def _():
            dk_ref[...] = jnp.zeros_like(dk_ref)
            dv_ref[...] = jnp.zeros_like(dv_ref)

        r_i = lax.broadcasted_iota(jnp.int32, (blk, blk), 0)
        c_i = lax.broadcasted_iota(jnp.int32, (blk, blk), 1)
        before = jnp.where(c_i < r_i, 1.0, 0.0).astype(BF16)
        upto = jnp.where(c_i <= r_i, 1.0, 0.0).astype(BF16)
        causal = r_i < c_i

        heads = range(ATT_HEADS)
        q_head = [_head_lanes(q_ref[...], a) for a in heads]
        d_ob = [_head_lanes(do_ref[...].astype(BF16), a) for a in heads]
        total = [tot_ref[0, a, 0][0:1, :] for a in heads]

        def sweep(blocks, lsum, esum, dqt):
            def keep(x, diag):
                return jnp.where(causal, x, 0.0) if diag else x

            def there(x, valid):
                return x if valid is None else jnp.where(valid, x, 0.0)

            offs = [pl.multiple_of(j * blk, blk) for j, _, _ in blocks]
            zt = [[_dot(k_ref[pl.ds(off, blk), :], q_head[a], _NT) for a in heads]
                  for off in offs]
            dwt = [[_dot(v_ref[pl.ds(off, blk), :], d_ob[a], _NT) for a in heads]
                   for off in offs]
            sp, lk, lpre = [], [], []
            for (_, diag, _), ztb in zip(blocks, zt):
                sp.append([_softplus(ztb[a]) for a in heads])
                lk.append([keep(-sp[-1][a], diag).astype(BF16) for a in heads])
                lpre.append([_dot(before, lk[-1][a]) for a in heads])
            wt, et, epre = [], [], []
            for i, (_, diag, valid) in enumerate(blocks):
                wt.append([keep(jnp.exp(zt[i][a] + (total[a] - lsum[a] - lpre[i][a])), diag)
                           for a in heads])
                et.append([dwt[i][a] * wt[i][a] for a in heads])
                epre.append([_dot(upto, et[i][a].astype(BF16)) for a in heads])
                lsum = [lsum[a] + there(lpre[i][a][blk - 1:blk, :] + lk[i][a][blk - 1:blk, :], valid)
                        for a in heads]
            for i, (_, diag, valid) in enumerate(blocks):
                dzb = [keep(et[i][a] - jnp.exp(zt[i][a] - sp[i][a]) * (esum[a] + epre[i][a]),
                            diag).astype(BF16) for a in heads]
                esum = [esum[a] + there(epre[i][a][blk - 1:blk, :], valid) for a in heads]
                dk_ref[pl.ds(offs[i], blk), :] += there(functools.reduce(
                    jnp.add, [_dot(dzb[a], q_head[a]) for a in heads]), valid)
                dv_ref[pl.ds(offs[i], blk), :] += there(functools.reduce(
                    jnp.add, [_dot(wt[i][a].astype(BF16), d_ob[a]) for a in heads]), valid)
                dqt = [dqt[a] + there(_dot(
                    kt_ref[0, a // 2, (a % 2) * HEAD_DIM:(a % 2 + 1) * HEAD_DIM,
                           pl.ds(offs[i], blk)], dzb[a]), valid) for a in heads]
            return lsum, esum, dqt

        row = [jnp.zeros((1, blk), F32)] * ATT_HEADS
        dqt = [jnp.zeros((HEAD_DIM, blk), F32)] * ATT_HEADS
        low = low_ref[pl.program_id(0) * N_ATT_GROUPS + pl.program_id(1), qi].astype(jnp.int32)
        low = jnp.clip(low, 0, jnp.maximum(qi - 1, 0))

        def pair(carry):
            j, lsum, esum, dqt = carry
            return (j + 2,) + sweep([(j, False, None), (j + 1, False, j + 1 < qi - 1)],
                                    lsum, esum, dqt)

        _, lsum, esum, dqt = lax.while_loop(lambda c: c[0] < qi - 1, pair, (low, row, row, dqt))
        _, _, dqt = sweep([(jnp.maximum(qi - 1, 0), False, qi >= 1), (qi, True, None)],
                          lsum, esum, dqt)
        dq_ref[...] = jnp.concatenate(dqt, axis=0).T

        @pl.when(qi == nq - 1)
        def _():
            dvb_ref[...] = dv_ref[...].astype(BF16)

    seq_blk = pl.BlockSpec((seq, ATT_W), lambda b, h, i: (b, h))
    tok = pl.BlockSpec((blk, ATT_W), lambda b, h, i: (b * nq + i, h))
    tok_shape = jax.ShapeDtypeStruct((nb * seq, D_BRANCH), F32)
    return pl.pallas_call(
        body, name="sb_attn_bwd",
        grid=(nb, N_ATT_GROUPS, nq),
        in_specs=[tok, seq_blk,
                  pl.BlockSpec((1, ATT_HEADS // 2, LANES, seq), lambda b, h, i: (b, h, 0, 0)),
                  seq_blk,
                  pl.BlockSpec((1, ATT_HEADS, 1, 8, blk), lambda b, h, i: (b, h, i, 0, 0)),
                  pl.BlockSpec(memory_space=pltpu.SMEM),
                  tok],
        out_specs=[tok, seq_blk, seq_blk],
        out_shape=[tok_shape, tok_shape, jax.ShapeDtypeStruct((nb * seq, D_BRANCH), BF16)],
        scratch_shapes=[pltpu.VMEM((seq, ATT_W), F32)],
        compiler_params=_params(3),
    )(qs, kn, kt, vb, tot, low, d_o)


def _qk_bwd(proj, dqs, dkn, qw2, kw2, nb, seq, tq):
    nl = seq // tq
    scale = 1.0 / math.sqrt(HEAD_DIM)

    def body(q_ref, k_ref, dq_ref, dk_ref, qw_ref, kw_ref, dqr_ref, dkr_ref, gq_ref, gk_ref):
        @pl.when((pl.program_id(0) == 0) & (pl.program_id(1) == 0))
        def _():
            gq_ref[...] = jnp.zeros_like(gq_ref)
            gk_ref[...] = jnp.zeros_like(gk_ref)

        def norm_bwd(x, w, dy):
            r = lax.rsqrt(_pair_sum(x * x) * (1.0 / HEAD_DIM) + EPS)
            xhat = x * r
            g = dy * w
            m = _pair_sum(g * xhat) * (1.0 / HEAD_DIM)
            return r * (g - xhat * m), jnp.sum(dy * xhat, axis=0, keepdims=True)

        gq = jnp.zeros((1, LANES), F32)
        gk = jnp.zeros((1, LANES), F32)
        for p in range(N_PAIRS):
            cols = slice(p * LANES, (p + 1) * LANES)
            dqr, gq_p = norm_bwd(q_ref[:, cols], qw_ref[...], dq_ref[:, cols] * scale)
            dkr, gk_p = norm_bwd(k_ref[:, cols], kw_ref[...], dk_ref[:, cols])
            dqr_ref[:, cols] = dqr.astype(BF16)
            dkr_ref[:, cols] = dkr.astype(BF16)
            gq, gk = gq + gq_p, gk + gk_p
        gq_ref[...] += gq
        gk_ref[...] += gk

    tok = lambda blk: pl.BlockSpec((tq, D_BRANCH), lambda b, i: (b * nl + i, blk))
    vec = pl.BlockSpec((1, LANES), lambda b, i: (0, 0))
    tshape = jax.ShapeDtypeStruct((nb * seq, D_BRANCH), BF16)
    return pl.pallas_call(
        body, name="qk_bwd",
        grid=(nb, nl),
        in_specs=[tok(0), tok(1), tok(0), tok(0), vec, vec],
        out_specs=[tok(0), tok(0), vec, vec],
        out_shape=[tshape, tshape,
                   jax.ShapeDtypeStruct((1, LANES), F32), jax.ShapeDtypeStruct((1, LANES), F32)],
        compiler_params=_params(2),
    )(proj, proj, dqs, dkn, qw2, kw2)


def _shift_down(cur, prev, k):
    if k == 0:
        return cur
    rows = _row_iota(cur.shape)
    return jnp.where(rows < k, pltpu.roll(prev, k, axis=0), pltpu.roll(cur, k, axis=0))


def _shift_up(cur, nxt, k):
    if k == 0:
        return cur
    n = cur.shape[0]
    rows = _row_iota(cur.shape)
    return jnp.where(rows < n - k, pltpu.roll(cur, n - k, axis=0), pltpu.roll(nxt, n - k, axis=0))


def _conv_taps(cur, prev):
    return [_shift_down(cur, prev, CONV_TAPS - 1 - i) for i in range(CONV_TAPS)]


def _conv_pre(taps, w, b):
    out = b
    for i in range(CONV_TAPS):
        out = out + taps[i] * w[i:i + 1, :]
    return out


def _silu(x):
    return x * _sigmoid(x)


def _silu_and_grad(x):
    s = _sigmoid(x)
    return x * s, s * (1.0 + x * (1.0 - s))


def _dot01(x, m01, parts, dims=None, m_left=False):
    total, rest = None, x
    for i in range(parts):
        piece = rest.astype(BF16)
        if i + 1 < parts:
            rest = rest - piece.astype(F32)
        term = _dot(m01, piece, dims) if m_left else _dot(piece, m01, dims)
        total = term if total is None else total + term
    return total


def _chunk_decay(dt_raw, dtb, alog, expand, qc):
    dt = _softplus(dt_raw + dtb)
    d_a = dt * (-jnp.exp(alog))
    r_i = lax.broadcasted_iota(jnp.int32, (qc, qc), 0)
    c_i = lax.broadcasted_iota(jnp.int32, (qc, qc), 1)
    tril = r_i >= c_i
    a_cs = _dot01(d_a, jnp.where(tril, 1.0, 0.0).astype(BF16), 3, m_left=True)
    dt_x = _dot01(dt, expand, 3)
    acs_x = _dot01(a_cs, expand, 3)
    return dt, d_a, a_cs, dt_x, acs_x, tril


def _ssd_fwd(proj, conv_w, conv_b, dtb, alog, dskip, nb, seq):
    qc = SSD_CHUNK
    nc = seq // qc

    def body(xs_ref, bc_ref, dt_ref, cw_ref, cb_ref, dtb_ref, al_ref, ds_ref,
             y_ref, st_ref, pxs_ref, pbc_ref, state_ref):
        @pl.when(pl.program_id(1) == 0)
        def _():
            pxs_ref[...] = jnp.zeros_like(pxs_ref)
            pbc_ref[...] = jnp.zeros_like(pbc_ref)
            state_ref[...] = jnp.zeros_like(state_ref)

        expand = _head_expand()
        xs_raw = xs_ref[...]
        bc_raw = bc_ref[...]
        cw = cw_ref[...]
        cb = cb_ref[...]
        xs = _silu(_conv_pre(_conv_taps(xs_raw, pxs_ref[...]), cw[:, :D_BRANCH], cb[:, :D_BRANCH]))
        bc = _silu(_conv_pre(_conv_taps(bc_raw, pbc_ref[...]), cw[:, D_BRANCH:], cb[:, D_BRANCH:]))
        pxs_ref[...] = xs_raw
        pbc_ref[...] = bc_raw

        dt, d_a, a_cs, dt_x, acs_x, tril = _chunk_decay(
            dt_ref[...], dtb_ref[...], al_ref[...], expand, qc)
        a_cst = a_cs.T
        aend_x = acs_x[qc - 1:qc, :]
        ea_x = jnp.exp(acs_x)
        dec_x = jnp.exp(aend_x - acs_x)
        xt = xs * dt_x
        xtb = xt.astype(BF16)
        xdb = (xt * dec_x).astype(BF16)
        d_x = _dot01(jnp.broadcast_to(ds_ref[...], (8, LANES)), expand, 3)[0:1, :]
        st_ref[0, 0] = state_ref[...]

        for g in range(N_GROUPS):
            gs = slice(g * GROUP_W, (g + 1) * GROUP_W)
            bg = bc[:, g * D_STATE:(g + 1) * D_STATE]
            cg = bc[:, (N_GROUPS + g) * D_STATE:(N_GROUPS + g + 1) * D_STATE]
            bgb = bg.astype(BF16)
            cgb = cg.astype(BF16)
            cbm = _dot(cgb, bgb, _NT)
            st_in = state_ref[g]
            y_off = _dot(cgb, st_in.astype(BF16)) * ea_x[:, gs]
            for k in range(HEADS_PER_GROUP):
                h = g * HEADS_PER_GROUP + k
                hs = slice(h * HEAD_DIM, (h + 1) * HEAD_DIM)
                seg = a_cs[:, h:h + 1] - a_cst[h:h + 1, :]
                gh = cbm * jnp.exp(jnp.where(tril, seg, -1e30))
                y_h = _dot(gh.astype(BF16), xtb[:, hs]) + y_off[:, k * HEAD_DIM:(k + 1) * HEAD_DIM]
                y_ref[:, hs] = y_h + d_x[:, hs] * xs[:, hs]
            state_ref[g] = st_in * jnp.exp(aend_x[:, gs]) + _dot(bg.T.astype(BF16), xdb[:, gs])

    nblk = lambda w, off: pl.BlockSpec((qc, w), lambda b, c: (b * nc + c, off))
    full = lambda r, w: pl.BlockSpec((r, w), lambda b, c: (0, 0))
    return pl.pallas_call(
        body, name="ssd_fwd",
        grid=(nb, nc),
        in_specs=[nblk(D_BRANCH, COL_XS // D_BRANCH), nblk(D_BC, COL_BC // D_BC),
                  nblk(LANES, COL_DT // LANES),
                  full(CONV_TAPS, D_CONV), full(1, D_CONV), full(1, LANES), full(1, LANES),
                  full(1, LANES)],
        out_specs=[pl.BlockSpec((qc, D_BRANCH), lambda b, c: (b * nc + c, 0)),
                   pl.BlockSpec((1, 1, N_GROUPS, D_STATE, GROUP_W), lambda b, c: (b, c, 0, 0, 0))],
        out_shape=[jax.ShapeDtypeStruct((nb * seq, D_BRANCH), F32),
                   jax.ShapeDtypeStruct((nb, nc, N_GROUPS, D_STATE, GROUP_W), F32)],
        scratch_shapes=[pltpu.VMEM((qc, D_BRANCH), F32), pltpu.VMEM((qc, D_BC), F32),
                        pltpu.VMEM((N_GROUPS, D_STATE, GROUP_W), F32)],
        compiler_params=_params(2),
    )(proj, proj, proj, conv_w, conv_b, dtb, alog, dskip)


def _ssd_bwd(proj, d_y, states, conv_w, conv_b, dtb, alog, dskip, nb, seq, slabs=()):
    qc = SSD_CHUNK
    nc = seq // qc

    def body(xs_ref, bc_ref, dt_ref, pxs_ref, pbc_ref, dy_ref, st_ref, stn_ref,
             cw_ref, cb_ref, dtb_ref, al_ref, ds_ref,
             dx_ref, gcw_ref, gcb_ref, gdtb_ref, gal_ref, gds_ref,
             dst_ref, nxs_ref, nbc_ref, yd_ref, dxt_ref):
        step = pl.program_id(1)
        chunk = nc - 1 - step

        @pl.when(step == 0)
        def _():
            dst_ref[...] = jnp.zeros_like(dst_ref)
            nxs_ref[...] = jnp.zeros_like(nxs_ref)
            nbc_ref[...] = jnp.zeros_like(nbc_ref)

        @pl.when((pl.program_id(0) == 0) & (step == 0))
        def _():
            gcw_ref[...] = jnp.zeros_like(gcw_ref)
            gcb_ref[...] = jnp.zeros_like(gcb_ref)
            gdtb_ref[...] = jnp.zeros_like(gdtb_ref)
            gal_ref[...] = jnp.zeros_like(gal_ref)
            gds_ref[...] = jnp.zeros_like(gds_ref)

        expand = _head_expand()
        collapse = lambda v: _dot01(v, expand, 2, _NT)
        first = jnp.where(chunk == 0, 0.0, 1.0)
        xs_raw = xs_ref[...]
        bc_raw = bc_ref[...]
        pxs = pxs_ref[...] * first
        pbc = pbc_ref[...] * first
        cw = cw_ref[...]
        cb = cb_ref[...]
        taps_xs = _conv_taps(xs_raw, pxs)
        taps_bc = _conv_taps(bc_raw, pbc)
        xs, dsilu_xs = _silu_and_grad(_conv_pre(taps_xs, cw[:, :D_BRANCH], cb[:, :D_BRANCH]))
        bc, dsilu_bc = _silu_and_grad(_conv_pre(taps_bc, cw[:, D_BRANCH:], cb[:, D_BRANCH:]))

        dt_in = dt_ref[...] + dtb_ref[...]
        dt, d_a, a_cs, dt_x, acs_x, tril = _chunk_decay(
            dt_ref[...], dtb_ref[...], al_ref[...], expand, qc)
        a_cst = a_cs.T
        aend_x = acs_x[qc - 1:qc, :]
        ea_x = jnp.exp(acs_x)
        dec_x = jnp.exp(aend_x - acs_x)
        xt = xs * dt_x
        xtb = xt.astype(BF16)
        xdb = (xt * dec_x).astype(BF16)
        d_x = _dot01(jnp.broadcast_to(ds_ref[...], (8, LANES)), expand, 3)[0:1, :]

        dy = dy_ref[...]
        dyb = dy.astype(BF16)
        dyeab = (dy * ea_x).astype(BF16)
        gds_ref[...] += collapse(jnp.broadcast_to(jnp.sum(dy * xs, axis=0, keepdims=True),
                                                  (8, D_BRANCH)))[0:1, :]

        d_bc = []
        d_cc = []
        y_offs = []
        dxt_states = []
        end_terms = []
        for g in range(N_GROUPS):
            gs = slice(g * GROUP_W, (g + 1) * GROUP_W)
            bg = bc[:, g * D_STATE:(g + 1) * D_STATE]
            cg = bc[:, (N_GROUPS + g) * D_STATE:(N_GROUPS + g + 1) * D_STATE]
            bgb = bg.astype(BF16)
            cgb = cg.astype(BF16)
            cbm = _dot(cgb, bgb, _NT)
            st_in = st_ref[0, 0, g]
            st_inb = st_in.astype(BF16)
            d_st = dst_ref[g]
            d_stb = d_st.astype(BF16)
            y_offs.append(_dot(cgb, st_inb) * ea_x[:, gs])
            dxt_states.append(_dot(bgb, d_stb) * dec_x[:, gs])
            d_c = _dot(dyeab[:, gs], st_inb, _NT)
            d_b = _dot(xdb[:, gs], d_stb, _NT)
            d_cb = jnp.zeros((qc, qc), F32)
            for k in range(HEADS_PER_GROUP):
                h = g * HEADS_PER_GROUP + k
                hs = slice(h * HEAD_DIM, (h + 1) * HEAD_DIM)
                seg = a_cs[:, h:h + 1] - a_cst[h:h + 1, :]
                lh = jnp.exp(jnp.where(tril, seg, -1e30))
                ghb = (cbm * lh).astype(BF16)
                d_cb = d_cb + _dot(dyb[:, hs], xtb[:, hs], _NT) * lh
                yd_ref[:, hs] = _dot(ghb, xtb[:, hs])
                dxt_ref[:, hs] = _dot(ghb, dyb[:, hs], _TN)
            d_cbb = d_cb.astype(BF16)
            d_cc.append(d_c + _dot(d_cbb, bgb))
            d_bc.append(d_b + _dot(d_cbb, cgb, _TN))
            end_terms.append(jnp.sum(d_st * stn_ref[0, 0, g], axis=0, keepdims=True))
            dst_ref[g] = d_st * jnp.exp(aend_x[:, gs]) + _dot(cg.T.astype(BF16), dyeab[:, gs])

        y_off = jnp.concatenate(y_offs, axis=1)
        dxt_state = jnp.concatenate(dxt_states, axis=1)
        dxt = dxt_ref[...] + dxt_state
        last = jnp.where(chunk == nc - 1, 0.0, 1.0)
        end_c = collapse(jnp.broadcast_to(jnp.concatenate(end_terms, axis=1), (8, D_BRANCH)))[0:1, :]
        da_cs = collapse(dyb.astype(F32) * yd_ref[...] - dxt_ref[...] * xtb.astype(F32)
                         + dy * y_off - dxt_state * xt)
        da_cs = da_cs + jnp.where(_row_iota(da_cs.shape) == qc - 1, end_c * last, 0.0)
        triu = lax.broadcasted_iota(jnp.int32, (qc, qc), 0) <= lax.broadcasted_iota(jnp.int32, (qc, qc), 1)
        dd_a = _dot01(da_cs, jnp.where(triu, 1.0, 0.0).astype(BF16), 3, m_left=True)
        ddt = dd_a * (-jnp.exp(al_ref[...])) + collapse(dxt * xs)
        head_lanes = _lane_iota(ddt.shape) < N_HEADS
        ddt_raw = jnp.where(head_lanes, ddt * _sigmoid(dt_in), 0.0)
        gal_ref[...] += jnp.sum(jnp.where(head_lanes, dd_a * d_a, 0.0), axis=0, keepdims=True)
        gdtb_ref[...] += jnp.sum(ddt_raw, axis=0, keepdims=True)

        dpre_xs = (dxt * dt_x + d_x * dy) * dsilu_xs
        dpre_bc = jnp.concatenate(d_bc + d_cc, axis=1) * dsilu_bc
        gcb_ref[...] += jnp.concatenate([jnp.sum(dpre_xs, axis=0, keepdims=True),
                                         jnp.sum(dpre_bc, axis=0, keepdims=True)], axis=1)
        nxs = nxs_ref[...]
        nbc = nbc_ref[...]
        du_xs = jnp.zeros_like(dpre_xs)
        du_bc = jnp.zeros_like(dpre_bc)
        for i in range(CONV_TAPS):
            k = CONV_TAPS - 1 - i
            gcw_ref[i:i + 1, :] += jnp.concatenate(
                [jnp.sum(dpre_xs * taps_xs[i], axis=0, keepdims=True),
                 jnp.sum(dpre_bc * taps_bc[i], axis=0, keepdims=True)], axis=1)
            du_xs = du_xs + _shift_up(dpre_xs, nxs, k) * cw[i:i + 1, :D_BRANCH]
            du_bc = du_bc + _shift_up(dpre_bc, nbc, k) * cw[i:i + 1, D_BRANCH:]
        nxs_ref[...] = dpre_xs
        nbc_ref[...] = dpre_bc

        dx_ref[:, :D_BRANCH] = du_xs.astype(BF16)
        dx_ref[:, D_BRANCH:D_CONV] = du_bc.astype(BF16)
        dx_ref[:, D_CONV:D_CONV + LANES] = ddt_raw.astype(BF16)
        dx_ref[:, D_CONV + LANES:] = jnp.zeros((qc, 2048 - D_CONV - LANES), BF16)

    rev = lambda b, c: b * nc + (nc - 1 - c)
    prv = lambda b, c: b * nc + jnp.maximum(nc - 2 - c, 0)
    nblk = lambda w, off, f: pl.BlockSpec((qc, w), lambda b, c: (f(b, c), off))
    full = lambda r, w: pl.BlockSpec((r, w), lambda b, c: (0, 0))
    st_spec = lambda f: pl.BlockSpec((1, 1, N_GROUPS, D_STATE, GROUP_W),
                                     lambda b, c: (b, f(c), 0, 0, 0))
    return _call_with_exchange(
        body, (proj, proj, proj, proj, proj, d_y, states, states, conv_w, conv_b, dtb, alog, dskip),
        slabs, (True,) * len(slabs), name="ssd_bwd", grid=(nb, nc),
        in_specs=[nblk(D_BRANCH, COL_XS // D_BRANCH, rev), nblk(D_BC, COL_BC // D_BC, rev),
                  nblk(LANES, COL_DT // LANES, rev),
                  nblk(D_BRANCH, COL_XS // D_BRANCH, prv), nblk(D_BC, COL_BC // D_BC, prv),
                  nblk(D_BRANCH, 0, rev),
                  st_spec(lambda c: nc - 1 - c), st_spec(lambda c: jnp.minimum(nc - c, nc - 1)),
                  full(CONV_TAPS, D_CONV), full(1, D_CONV), full(1, LANES), full(1, LANES),
                  full(1, LANES)],
        out_specs=[nblk(2048, 0, rev), full(8, D_CONV), full(1, D_CONV), full(1, LANES),
                   full(1, LANES), full(1, LANES)],
        out_shape=[jax.ShapeDtypeStruct((nb * seq, 2048), BF16),
                   jax.ShapeDtypeStruct((8, D_CONV), F32), jax.ShapeDtypeStruct((1, D_CONV), F32),
                   jax.ShapeDtypeStruct((1, LANES), F32), jax.ShapeDtypeStruct((1, LANES), F32),
                   jax.ShapeDtypeStruct((1, LANES), F32)],
        scratch_shapes=[pltpu.VMEM((N_GROUPS, D_STATE, GROUP_W), F32),
                        pltpu.VMEM((qc, D_BRANCH), F32), pltpu.VMEM((qc, D_BC), F32),
                        pltpu.VMEM((qc, D_BRANCH), F32), pltpu.VMEM((qc, D_BRANCH), F32)])


def _mid(o_sb, y_ssd, proj, x2, target, sb_w, ssd_w, w_out_b, tm):
    t = x2.shape[0]
    inv_d = 1.0 / D_MODEL

    def body(o_ref, y_ref, zsb_ref, zssd_ref, x_ref, tg_ref, sbw_ref, ssdw_ref, w_ref,
             dout_ref, dosb_ref, dy_ref, dz_ref, gw_ref, gsb_ref, gssd_ref, loss_ref):
        @pl.when(pl.program_id(0) == 0)
        def _():
            gw_ref[...] = jnp.zeros_like(gw_ref)
            gsb_ref[...] = jnp.zeros_like(gsb_ref)
            gssd_ref[...] = jnp.zeros_like(gssd_ref)
            loss_ref[...] = jnp.zeros_like(loss_ref)

        def branch(val, z, w):
            gate, dgate = _silu_and_grad(z)
            g = val * gate
            r = lax.rsqrt(jnp.mean(g * g, axis=1, keepdims=True) + EPS)
            xhat = g * r
            return (gate, dgate, r, xhat), (xhat * w).astype(BF16)

        o = o_ref[...]
        y = y_ref[...]
        saved_a, mix_a = branch(o, zsb_ref[...], sbw_ref[...])
        saved_b, mix_b = branch(y, zssd_ref[...], ssdw_ref[...])
        out = x_ref[...] + _dot(mix_a, w_ref[:D_BRANCH, :]) + _dot(mix_b, w_ref[D_BRANCH:, :])
        diff = out - tg_ref[...]
        loss_ref[...] += 0.5 * inv_d * jnp.sum(diff * diff)
        d_out = diff * inv_d
        dout_ref[...] = d_out
        d_outb = d_out.astype(BF16)
        gw_ref[:D_BRANCH, :] += _dot(mix_a, d_outb, _TN)
        gw_ref[D_BRANCH:, :] += _dot(mix_b, d_outb, _TN)

        def branch_bwd(dmix, val, w, saved):
            gate, dgate, r, xhat = saved
            gg = dmix * w
            m = jnp.mean(gg * xhat, axis=1, keepdims=True)
            dg = r * (gg - xhat * m)
            return dg * gate, dg * val * dgate, jnp.sum(dmix * xhat, axis=0, keepdims=True)

        dmix_a = _dot(d_outb, w_ref[:D_BRANCH, :], _NT)
        dmix_b = _dot(d_outb, w_ref[D_BRANCH:, :], _NT)
        d_o, dz_a, gsb = branch_bwd(dmix_a, o, sbw_ref[...], saved_a)
        d_y, dz_b, gssd = branch_bwd(dmix_b, y, ssdw_ref[...], saved_b)
        dosb_ref[...] = d_o
        dy_ref[...] = d_y
        dz_ref[:, :D_BRANCH] = dz_a.astype(BF16)
        dz_ref[:, D_BRANCH:] = dz_b.astype(BF16)
        gsb_ref[...] += gsb
        gssd_ref[...] += gssd

    row = lambda w, off: pl.BlockSpec((tm, w), lambda i: (i, off))
    full = lambda r, w: pl.BlockSpec((r, w), lambda i: (0, 0))
    resident = pl.BlockSpec((2 * D_BRANCH, D_MODEL), lambda i: (0, 0), pipeline_mode=pl.Buffered(1))
    tok = jax.ShapeDtypeStruct((t, D_MODEL), F32)
    return pl.pallas_call(
        body, name="mid",
        grid=(t // tm,),
        in_specs=[row(D_BRANCH, 0), row(D_BRANCH, 0), row(D_BRANCH, 3), row(D_BRANCH, 4),
                  row(D_MODEL, 0), row(D_MODEL, 0), full(1, D_BRANCH), full(1, D_BRANCH),
                  resident],
        out_specs=[row(D_MODEL, 0), row(D_BRANCH, 0), row(D_BRANCH, 0), row(2 * D_BRANCH, 0),
                   resident, full(1, D_BRANCH), full(1, D_BRANCH),
                   full(1, LANES)],
        out_shape=[tok, tok, tok, jax.ShapeDtypeStruct((t, 2 * D_BRANCH), BF16),
                   jax.ShapeDtypeStruct((2 * D_BRANCH, D_MODEL), F32),
                   jax.ShapeDtypeStruct((1, D_BRANCH), F32), jax.ShapeDtypeStruct((1, D_BRANCH), F32),
                   jax.ShapeDtypeStruct((1, LANES), F32)],
        compiler_params=_params(1),
    )(o_sb, y_ssd, proj, proj, x2, target, sb_w, ssd_w, w_out_b)


_DPROJ_FIRST = (0, 1, 2, 3, 5)
_DPROJ_BLOCKS = (1, 1, 1, 2, 2)


def _in_proj_bwd_x(d_parts, w_in_t, x2, d_out, norm_w, tm, slabs=()):
    t = x2.shape[0]
    n_parts = len(d_parts)

    def body(*refs):
        dp_refs = refs[:n_parts]
        w_ref, x_ref, dout_ref, nw_ref, gx_ref, gnw_ref = refs[n_parts:]

        @pl.when(pl.program_id(0) == 0)
        def _():
            gnw_ref[...] = jnp.zeros_like(gnw_ref)

        d_hn = None
        for p in range(n_parts):
            rows = slice(_DPROJ_FIRST[p] * COLBLK, (_DPROJ_FIRST[p] + _DPROJ_BLOCKS[p]) * COLBLK)
            term = _dot(dp_refs[p][...], w_ref[rows, :])
            d_hn = term if d_hn is None else d_hn + term
        xf = x_ref[...]
        r = lax.rsqrt(jnp.mean(xf * xf, axis=1, keepdims=True) + EPS)
        xhat = xf * r
        g = d_hn * nw_ref[...]
        m = jnp.mean(g * xhat, axis=1, keepdims=True)
        gx_ref[...] = dout_ref[...] + r * (g - xhat * m)
        gnw_ref[...] += jnp.sum(d_hn * xhat, axis=0, keepdims=True)

    row = lambda w: pl.BlockSpec((tm, w), lambda i: (i, 0))
    return _call_with_exchange(
        body, (*d_parts, w_in_t, x2, d_out, norm_w), slabs, (True,) * len(slabs),
        name="in_proj_bwd_x", grid=(t // tm,),
        in_specs=[row(COLBLK * _DPROJ_BLOCKS[p]) for p in range(n_parts)] + [
                  pl.BlockSpec((D_IN_PAD, D_MODEL), lambda i: (0, 0), pipeline_mode=pl.Buffered(1)),
                  row(D_MODEL), row(D_MODEL), pl.BlockSpec((1, D_MODEL), lambda i: (0, 0))],
        out_specs=[row(D_MODEL), pl.BlockSpec((1, D_MODEL), lambda i: (0, 0))],
        out_shape=[jax.ShapeDtypeStruct((t, D_MODEL), F32), jax.ShapeDtypeStruct((1, D_MODEL), F32)])


def _in_proj_bwd_w(hn, d_parts, tm):
    t = hn.shape[0]
    n_parts = len(d_parts)

    def body(hn_ref, *refs):
        dp_refs, gw_ref = refs[:n_parts], refs[n_parts]

        @pl.when(pl.program_id(0) == 0)
        def _():
            gw_ref[...] = jnp.zeros_like(gw_ref)

        hnt = hn_ref[...].astype(F32).T.astype(BF16)
        for p in range(n_parts):
            cols = slice(_DPROJ_FIRST[p] * COLBLK, (_DPROJ_FIRST[p] + _DPROJ_BLOCKS[p]) * COLBLK)
            gw_ref[:, cols] += _dot(hnt, dp_refs[p][...])

    return pl.pallas_call(
        body, name="in_proj_bwd_w",
        grid=(t // tm,),
        in_specs=[pl.BlockSpec((tm, D_MODEL), lambda i: (i, 0))]
                 + [pl.BlockSpec((tm, COLBLK * _DPROJ_BLOCKS[p]), lambda i: (i, 0))
                    for p in range(n_parts)],
        out_specs=pl.BlockSpec((D_MODEL, D_IN_PAD), lambda i: (0, 0), pipeline_mode=pl.Buffered(1)),
        out_shape=jax.ShapeDtypeStruct((D_MODEL, D_IN_PAD), F32),
        compiler_params=_params(1),
    )(hn, *d_parts)


def _adamw(parts, w, m, v, tr, name):
    _, rows, cols = w.shape
    c1 = 1.0 - ADAM_B1 ** ADAM_STEP
    c2 = 1.0 - ADAM_B2 ** ADAM_STEP

    def body(p_ref, w_ref, m_ref, v_ref, g_ref, d_ref, nm_ref, nv_ref):
        g = p_ref[0].astype(F32)
        for s in range(1, N_DEV):
            g = g + p_ref[s].astype(F32)
        nm = ADAM_B1 * m_ref[0] + (1.0 - ADAM_B1) * g
        nv = ADAM_B2 * v_ref[0] + (1.0 - ADAM_B2) * (g * g)
        g_ref[0] = g
        nm_ref[0] = nm
        nv_ref[0] = nv
        d_ref[0] = -ADAM_LR * ((nm / c1) / (jnp.sqrt(nv / c2) + ADAM_EPS) + ADAM_WD * w_ref[0])

    blk = pl.BlockSpec((1, tr, cols), lambda i: (0, i, 0))
    shape = jax.ShapeDtypeStruct((1, rows, cols), F32)
    return pl.pallas_call(
        body, name=name,
        grid=(rows // tr,),
        in_specs=[pl.BlockSpec((N_DEV, tr, cols), lambda i: (0, i, 0)), blk, blk, blk],
        out_specs=[blk, blk, blk, blk],
        out_shape=[shape, shape, shape, shape],
        compiler_params=_params(1),
    )(parts, w, m, v)


def _mesh_place():
    x, y, c = lax.axis_index("x"), lax.axis_index("y"), lax.axis_index("c")
    return x, y, c, 4 * x + 2 * y + c


def _peer(x, y, c, k):
    px = 1 - x if k & 4 else x
    py = 1 - y if k & 2 else y
    pc = 1 - c if k & 1 else c
    return (px, py, pc), 4 * px + 2 * py + pc


def _exchange(srcs, scatter, name):
    n = len(srcs)

    def body(*refs):
        copies = _exchange_copies(refs[:n], refs[n:2 * n], scatter, *refs[2 * n:])
        _exchange_start(copies)
        _exchange_wait(copies)

    return pl.pallas_call(
        body, name=name,
        in_specs=[_ANY] * n, out_specs=[_ANY] * n, out_shape=_exchange_shapes(srcs, scatter),
        scratch_shapes=_exchange_sems(n),
    )(*srcs)


def _call_with_exchange(body, operands, srcs, scatter, *, name, grid, in_specs, out_specs,
                        out_shape, scratch_shapes=()):
    n_in, n_out, n_scr, n_x = len(in_specs), len(out_specs), len(scratch_shapes), len(srcs)
    params = _params(len(grid))
    if not n_x:
        return pl.pallas_call(body, name=name, grid=grid, in_specs=list(in_specs),
                              out_specs=list(out_specs), out_shape=list(out_shape),
                              scratch_shapes=list(scratch_shapes), compiler_params=params)(*operands)

    def wrapped(*refs):
        ins, refs = refs[:n_in], refs[n_in:]
        x_src, refs = refs[:n_x], refs[n_x:]
        outs, refs = refs[:n_out], refs[n_out:]
        x_dst, refs = refs[:n_x], refs[n_x:]
        scratch, sems = refs[:n_scr], refs[n_scr:]
        ids = [pl.program_id(a) for a in range(len(grid))]
        first = functools.reduce(jnp.logical_and, [i == 0 for i in ids])
        last = functools.reduce(jnp.logical_and, [i == n - 1 for i, n in zip(ids, grid)])

        @pl.when(first)
        def _():
            _exchange_start(_exchange_copies(x_src, x_dst, scatter, *sems))

        body(*ins, *outs, *scratch)

        @pl.when(last)
        def _():
            _exchange_wait(_exchange_copies(x_src, x_dst, scatter, *sems))

    return pl.pallas_call(
        wrapped, name=name, grid=grid,
        in_specs=list(in_specs) + [_ANY] * n_x, out_specs=list(out_specs) + [_ANY] * n_x,
        out_shape=list(out_shape) + _exchange_shapes(srcs, scatter),
        scratch_shapes=list(scratch_shapes) + _exchange_sems(n_x), compiler_params=params,
    )(*operands, *srcs)


def _gather_two_level(shard, name):
    def body(x_ref, out_ref, send_sems, recv_sems, local_sem):
        x, y, c, me = _mesh_place()
        sibling = (x, y, 1 - c)
        chips = [(1 - x, y), (x, 1 - y), (1 - x, 1 - y)]

        def slab(px, py, pc):
            return out_ref.at[4 * px + 2 * py + pc]

        def copy(k, block, to, src=None):
            return pltpu.make_async_remote_copy(
                src_ref=slab(*block) if src is None else src, dst_ref=slab(*block),
                send_sem=send_sems.at[k], recv_sem=recv_sems.at[k],
                device_id=to, device_id_type=pl.DeviceIdType.MESH)

        mine = pltpu.make_async_copy(x_ref, slab(x, y, c), local_sem)
        mine.start()
        first = [copy(0, (x, y, c), sibling, src=x_ref)]
        first += [copy(1 + j, (x, y, c), (*chip, c), src=x_ref) for j, chip in enumerate(chips)]
        for cp in first:
            cp.start()
        passed = [copy(4 + j, (*chip, c), sibling) for j, chip in enumerate(chips)]
        for j, chip in enumerate(chips):
            copy(1 + j, (*chip, c), (x, y, c)).wait_recv()
            passed[j].start()
        copy(0, sibling, (x, y, c)).wait_recv()
        for j, chip in enumerate(chips):
            copy(4 + j, (*chip, 1 - c), (x, y, c)).wait_recv()
        for cp in first + passed:
            cp.wait_send()
        mine.wait()

    return pl.pallas_call(
        body, name=name,
        in_specs=[_ANY], out_specs=_ANY,
        out_shape=jax.ShapeDtypeStruct((N_DEV,) + shard.shape, shard.dtype),
        scratch_shapes=[pltpu.SemaphoreType.DMA((N_DEV - 1,)), pltpu.SemaphoreType.DMA((N_DEV - 1,)),
                        pltpu.SemaphoreType.DMA],
    )(shard)


_ANY = pl.BlockSpec(memory_space=pl.ANY)


def _exchange_shapes(srcs, scatter):
    return [jax.ShapeDtypeStruct(s.shape if sc else (N_DEV,) + s.shape, s.dtype)
            for s, sc in zip(srcs, scatter)]


def _exchange_sems(n):
    return [pltpu.SemaphoreType.DMA((n * (N_DEV - 1),)),
            pltpu.SemaphoreType.DMA((n * (N_DEV - 1),)),
            pltpu.SemaphoreType.DMA((n,))]


def _exchange_copies(src_refs, dst_refs, scatter, send_sems, recv_sems, loc_sems):
    n = len(src_refs)
    x, y, c, me = _mesh_place()

    def src_of(i, idx):
        return src_refs[i].at[idx] if scatter[i] else src_refs[i]

    local = [pltpu.make_async_copy(src_of(i, me), dst_refs[i].at[me], loc_sems.at[i])
             for i in range(n)]
    sends, recvs = [], []
    for k in range(1, N_DEV):
        peer, pidx = _peer(x, y, c, k)
        for i in range(n):
            s = i * (N_DEV - 1) + k - 1
            for dst_slab, group in ((me, sends), (pidx, recvs)):
                group.append(pltpu.make_async_remote_copy(
                    src_ref=src_of(i, pidx), dst_ref=dst_refs[i].at[dst_slab],
                    send_sem=send_sems.at[s], recv_sem=recv_sems.at[s],
                    device_id=peer, device_id_type=pl.DeviceIdType.MESH))
    return local, sends, recvs


def _exchange_start(copies):
    local, sends, _ = copies
    for cp in local + sends:
        cp.start()


def _exchange_wait(copies):
    local, sends, recvs = copies
    for cp in recvs:
        cp.wait_recv()
    for cp in sends:
        cp.wait_send()
    for cp in local:
        cp.wait()


def _pad_lanes(v, width=LANES):
    return jnp.pad(v, ((0, 0), (0, width - v.shape[1])))


def _local_step(x, target, norm_w, w_in_b, q_norm_w, k_norm_w, conv_w, conv_b, dt_bias, a_log,
                d_skip, sb_norm_w, ssd_norm_w, w_out_b, tm=256, tq=512, tmid=256, blk=ATT_BLK,
                scatter=False, w_in_t=None):
    nb, seq, _ = x.shape
    t = nb * seq
    x2 = x.reshape(t, D_MODEL)
    tg2 = target.reshape(t, D_MODEL)
    qw2 = jnp.tile(q_norm_w, (1, 2))
    kw2 = jnp.tile(k_norm_w, (1, 2))
    dtb, alog, dsk = _pad_lanes(dt_bias), _pad_lanes(a_log), _pad_lanes(d_skip)

    if w_in_t is None:
        w_in_t = w_in_b.T
    tall = min(2 * tm, t)
    if scatter:
        proj, hn, wout_all, cw_all = _in_proj(x2, norm_w, w_in_b, tall, (w_out_b, conv_w))
        w_out_b = wout_all.reshape(2 * D_BRANCH, D_MODEL)
        conv_w = jnp.transpose(cw_all, (1, 0, 2)).reshape(CONV_TAPS, D_CONV)
    else:
        proj, hn = _in_proj(x2, norm_w, w_in_b, tall)
    qs, kn, vb, kt, ksq = _qk_prep(proj, qw2, kw2, nb, seq, min(2 * tq, seq))
    o_sb, sb_tot, sb_low = _attn_fwd(qs, kn, vb, jnp.max(ksq, axis=1), nb, seq, blk)
    y_ssd, states = _ssd_fwd(proj, conv_w, conv_b, dtb, alog, dsk, nb, seq)
    d_out, d_osb, d_y, d_z, g_wout, g_sbw, g_ssdw, loss = _mid(
        o_sb, y_ssd, proj, x2, tg2, sb_norm_w, ssd_norm_w, w_out_b, tmid)
    dqs, dkn, dv_raw = _attn_bwd(qs, kn, kt, vb, sb_tot, sb_low, d_osb, nb, seq, blk)
    dq_raw, dk_raw, g_qw, g_kw = _qk_bwd(proj, dqs, dkn, qw2, kw2, nb, seq, tq)
    wout_slabs = (g_wout.reshape(N_DEV, 2 * D_BRANCH // N_DEV, D_MODEL).astype(BF16),)
    d_xbc, g_cw, g_cb, g_dtb, g_alog, g_dsk, *moved = _ssd_bwd(
        proj, d_y, states, conv_w, conv_b, dtb, alog, dsk, nb, seq, wout_slabs if scatter else ())
    d_parts = [dq_raw, dk_raw, dv_raw, d_z, d_xbc]
    g_win = _in_proj_bwd_w(hn, d_parts, tall)[:, :D_IN]
    g_cw = g_cw[:CONV_TAPS]
    if scatter:
        g_wout, = moved
        grad_x, g_nw, g_win, g_cw = _in_proj_bwd_x(
            d_parts, w_in_t, x2, d_out, norm_w, tall, _grad_slabs(g_win, g_cw))
    else:
        grad_x, g_nw = _in_proj_bwd_x(d_parts, w_in_t, x2, d_out, norm_w, tall)

    small = dict(
        norm_w=g_nw,
        q_norm_w=g_qw[:, :HEAD_DIM] + g_qw[:, HEAD_DIM:],
        k_norm_w=g_kw[:, :HEAD_DIM] + g_kw[:, HEAD_DIM:],
        conv_b=g_cb, dt_bias=g_dtb[:, :N_HEADS], A_log=g_alog[:, :N_HEADS],
        D_skip=g_dsk[:, :N_HEADS], sb_norm_w=g_sbw, ssd_norm_w=g_ssdw)
    return loss[0, 0], grad_x.reshape(nb, seq, D_MODEL), g_win, g_wout, g_cw, small


def _grad_slabs(g_win, g_cw):
    w_sh = D_IN // N_DEV
    c_sh = D_CONV // N_DEV
    return (jnp.transpose(g_win.reshape(D_MODEL, N_DEV, w_sh), (1, 0, 2)).astype(BF16),
            jnp.pad(jnp.transpose(g_cw.reshape(CONV_TAPS, N_DEV, c_sh), (1, 0, 2)),
                    ((0, 0), (0, 8 - CONV_TAPS), (0, 0))))


_SMALL = ("norm_w", "q_norm_w", "k_norm_w", "conv_b", "dt_bias", "A_log", "D_skip",
          "sb_norm_w", "ssd_norm_w")


def _pack_small(vals):
    flat = jnp.concatenate([_pad_lanes(vals[n], -(-vals[n].shape[1] // LANES) * LANES)
                            for n in _SMALL], axis=1)
    return jnp.pad(flat, ((0, 0), (0, 48 * LANES - flat.shape[1]))).reshape(48, LANES)


def _unpack_small(packed, like):
    out, r = {}, 0
    for n in _SMALL:
        width = like[n].shape[1]
        nr = -(-width // LANES)
        out[n] = packed[r:r + nr].reshape(1, nr * LANES)[:, :width]
        r += nr
    return out


def kernel(x, norm_w, w_in, q_norm_w, k_norm_w, conv_w, conv_b, dt_bias, A_log, D_skip, sb_norm_w, ssd_norm_w, w_out, loss_target, m_norm_w, m_w_in, m_q_norm_w, m_k_norm_w, m_conv_w, m_conv_b, m_dt_bias, m_A_log, m_D_skip, m_sb_norm_w, m_ssd_norm_w, m_w_out, v_norm_w, v_w_in, v_q_norm_w, v_k_norm_w, v_conv_w, v_conv_b, v_dt_bias, v_A_log, v_D_skip, v_sb_norm_w, v_ssd_norm_w, v_w_out):
    win_all = _gather_two_level(w_in[0].astype(BF16), "gather_w_in")
    w_in_b = jnp.pad(jnp.transpose(win_all, (1, 0, 2)).reshape(D_MODEL, D_IN),
                     ((0, 0), (0, D_IN_PAD - D_IN)))
    w_in_t = jnp.pad(jnp.transpose(win_all, (0, 2, 1)).reshape(D_IN, D_MODEL),
                     ((0, D_IN_PAD - D_IN), (0, 0)))

    loss, grad_x, win_parts, wout_parts, cw_parts, g_small = _local_step(
        x, loss_target, norm_w, w_in_b, q_norm_w, k_norm_w, conv_w[0], conv_b, dt_bias, A_log,
        D_skip, sb_norm_w, ssd_norm_w, w_out[0].astype(BF16), scatter=True, w_in_t=w_in_t)
    packed = _pack_small(g_small).at[-1, 0].set(loss)
    small_parts, = _exchange([packed], [False], "gather_small_grads")
    loss = jnp.sum(small_parts[:, -1, 0])

    small_w = dict(norm_w=norm_w, q_norm_w=q_norm_w, k_norm_w=k_norm_w, conv_b=conv_b,
                   dt_bias=dt_bias, A_log=A_log, D_skip=D_skip, sb_norm_w=sb_norm_w,
                   ssd_norm_w=ssd_norm_w)
    small_m = dict(norm_w=m_norm_w, q_norm_w=m_q_norm_w, k_norm_w=m_k_norm_w, conv_b=m_conv_b,
                   dt_bias=m_dt_bias, A_log=m_A_log, D_skip=m_D_skip, sb_norm_w=m_sb_norm_w,
                   ssd_norm_w=m_ssd_norm_w)
    small_v = dict(norm_w=v_norm_w, q_norm_w=v_q_norm_w, k_norm_w=v_k_norm_w, conv_b=v_conv_b,
                   dt_bias=v_dt_bias, A_log=v_A_log, D_skip=v_D_skip, sb_norm_w=v_sb_norm_w,
                   ssd_norm_w=v_ssd_norm_w)

    pad8 = lambda a: jnp.pad(a, ((0, 0), (0, 8 - CONV_TAPS), (0, 0)))
    r_win = _adamw(win_parts, w_in, m_w_in, v_w_in, 128, "adamw_w_in")
    r_wout = _adamw(wout_parts, w_out, m_w_out, v_w_out, 128, "adamw_w_out")
    r_cw = _adamw(cw_parts, pad8(conv_w), pad8(m_conv_w), pad8(v_conv_w), 8, "adamw_conv_w")
    r_small = _adamw(small_parts, _pack_small(small_w)[None], _pack_small(small_m)[None],
                     _pack_small(small_v)[None], 48, "adamw_small")

    res = {"w_in": r_win, "w_out": r_wout, "conv_w": [a[:, :CONV_TAPS] for a in r_cw]}
    unpacked = [_unpack_small(a[0], small_w) for a in r_small]
    for n in _SMALL:
        res[n] = [u[n] for u in unpacked]
    order = ("norm_w", "w_in", "q_norm_w", "k_norm_w", "conv_w", "conv_b", "dt_bias", "A_log",
             "D_skip", "sb_norm_w", "ssd_norm_w", "w_out")
    outs = [loss, grad_x]
    for kind in range(4):
        outs += [res[n][kind] for n in order]
    return tuple(outs)
```

```python
import functools
import math

import jax
import jax.numpy as jnp
from jax import lax
from jax.experimental import pallas as pl
from jax.experimental.pallas import tpu as pltpu

F32 = jnp.float32
BF16 = jnp.bfloat16

D_MODEL = 1024
N_HEADS = 16
HEAD_DIM = 64
N_PAIRS = N_HEADS // 2
D_BRANCH = 1024
N_GROUPS = 2
HEADS_PER_GROUP = 8
D_STATE = 128
GROUP_W = HEADS_PER_GROUP * HEAD_DIM
D_BC = 2 * N_GROUPS * D_STATE
D_CONV = D_BRANCH + D_BC
D_IN = 6672
COLBLK = 1024
D_IN_PAD = 7168
N_COLBLK = D_IN_PAD // COLBLK
COL_XS = 5120
COL_BC = 6144
COL_DT = 6656
EPS = 1e-6
CONV_TAPS = 4
N_DEV = 8

LANES = 128
SSD_CHUNK = 128
ATT_BLK = 256
ATT_HEADS = 4
ATT_W = ATT_HEADS * HEAD_DIM
N_ATT_GROUPS = N_HEADS // ATT_HEADS
EXP_UNDERFLOW = -105.0
VMEM_LIMIT = 56 * 1024 * 1024

ADAM_LR = 0.001
ADAM_B1 = 0.9
ADAM_B2 = 0.999
ADAM_EPS = 1e-08
ADAM_WD = 0.01
ADAM_STEP = 10

_NT = (((1,), (1,)), ((), ()))
_TN = (((0,), (0,)), ((), ()))


def _params(n_grid):
    return pltpu.CompilerParams(dimension_semantics=("arbitrary",) * n_grid,
                                vmem_limit_bytes=VMEM_LIMIT)


def _dot(a, b, dims=None):
    if dims is None:
        return jnp.dot(a, b, preferred_element_type=F32)
    return lax.dot_general(a, b, dims, preferred_element_type=F32)


def _sigmoid(x):
    return 1.0 / (1.0 + jnp.exp(-x))


def _softplus(x):
    return jnp.maximum(x, 0.0) + jnp.log(1.0 + jnp.exp(-jnp.abs(x)))


def _split_bf16(x):
    hi = x.astype(BF16)
    lo = (x - hi.astype(F32)).astype(BF16)
    return hi, lo


def _lane_iota(shape):
    return lax.broadcasted_iota(jnp.int32, shape, len(shape) - 1)


def _row_iota(shape):
    return lax.broadcasted_iota(jnp.int32, shape, len(shape) - 2)


def _pair_sum(x):
    r = lax.broadcasted_iota(jnp.int32, (LANES, LANES), 0)
    c = lax.broadcasted_iota(jnp.int32, (LANES, LANES), 1)
    same_head = jnp.where(r // HEAD_DIM == c // HEAD_DIM, 1.0, 0.0).astype(BF16)
    hi, lo = _split_bf16(x)
    return _dot(hi, same_head) + _dot(lo, same_head)


def _head_lanes(x, a):
    lane = _lane_iota(x.shape)
    mine = (lane >= a * HEAD_DIM) & (lane < (a + 1) * HEAD_DIM)
    return jnp.where(mine, x, jnp.zeros_like(x))


def _head_expand():
    r = lax.broadcasted_iota(jnp.int32, (LANES, D_BRANCH), 0)
    c = lax.broadcasted_iota(jnp.int32, (LANES, D_BRANCH), 1)
    return jnp.where(c // HEAD_DIM == r, 1.0, 0.0).astype(BF16)


def _in_proj(x2, norm_w, w_in_b, tm, shards=()):
    t = x2.shape[0]

    def body(x_ref, nw_ref, w_ref, proj_ref, hn_ref):
        xf = x_ref[...]
        r = lax.rsqrt(jnp.mean(xf * xf, axis=1, keepdims=True) + EPS)
        hn = (xf * r * nw_ref[...]).astype(BF16)
        hn_ref[...] = hn
        for j in range(N_COLBLK):
            cols = slice(j * COLBLK, (j + 1) * COLBLK)
            proj_ref[:, cols] = _dot(hn, w_ref[:, cols])

    return _call_with_exchange(
        body, (x2, norm_w, w_in_b), shards, (False,) * len(shards), name="in_proj",
        grid=(t // tm,),
        in_specs=[pl.BlockSpec((tm, D_MODEL), lambda i: (i, 0)),
                  pl.BlockSpec((1, D_MODEL), lambda i: (0, 0)),
                  pl.BlockSpec((D_MODEL, D_IN_PAD), lambda i: (0, 0), pipeline_mode=pl.Buffered(1))],
        out_specs=[pl.BlockSpec((tm, D_IN_PAD), lambda i: (i, 0)),
                   pl.BlockSpec((tm, D_MODEL), lambda i: (i, 0))],
        out_shape=[jax.ShapeDtypeStruct((t, D_IN_PAD), F32),
                   jax.ShapeDtypeStruct((t, D_MODEL), BF16)])


def _qk_prep(proj, qw2, kw2, nb, seq, tq):
    nl = seq // tq
    scale = 1.0 / math.sqrt(HEAD_DIM)

    def body(q_ref, k_ref, v_ref, qw_ref, kw_ref, qs_ref, kn_ref, vb_ref, kt_ref, ksq_ref):
        def norm(x, w):
            r = lax.rsqrt(_pair_sum(x * x) * (1.0 / HEAD_DIM) + EPS)
            return x * r * w

        vb_ref[...] = v_ref[...].astype(BF16)
        for p in range(N_PAIRS):
            cols = slice(p * LANES, (p + 1) * LANES)
            kn = norm(k_ref[:, cols], kw_ref[...])
            knb = kn.astype(BF16)
            qs_ref[:, cols] = (norm(q_ref[:, cols], qw_ref[...]) * scale).astype(BF16)
            kn_ref[:, cols] = knb
            kt_ref[0, p] = kn.T.astype(BF16)
            kf = knb.astype(F32)
            ksq_ref[0, 0, p:p + 1, :] = jnp.max(_pair_sum(kf * kf), axis=0, keepdims=True) * 1.0001

    tok_shape = jax.ShapeDtypeStruct((nb * seq, D_BRANCH), BF16)
    tok = lambda blk: pl.BlockSpec((tq, D_BRANCH), lambda b, i: (b * nl + i, blk))
    vec = pl.BlockSpec((1, LANES), lambda b, i: (0, 0))
    return pl.pallas_call(
        body, name="qk_prep",
        grid=(nb, nl),
        in_specs=[tok(0), tok(1), tok(2), vec, vec],
        out_specs=[tok(0), tok(0), tok(0),
                   pl.BlockSpec((1, N_PAIRS, LANES, tq), lambda b, i: (b, 0, 0, i)),
                   pl.BlockSpec((1, 1, N_PAIRS, LANES), lambda b, i: (b, i, 0, 0))],
        out_shape=[tok_shape, tok_shape, tok_shape,
                   jax.ShapeDtypeStruct((nb, N_PAIRS, LANES, seq), BF16),
                   jax.ShapeDtypeStruct((nb, nl, N_PAIRS, LANES), F32)],
        compiler_params=_params(2),
    )(proj, proj, proj, qw2, kw2)


def _attn_fwd(qs, kn, vb, ksq, nb, seq, blk):
    nq = seq // blk

    def body(q_ref, k_ref, v_ref, ksq_ref, o_ref, tot_ref, low_ref):
        qi = pl.program_id(2)
        r_i = lax.broadcasted_iota(jnp.int32, (blk, blk), 0)
        c_i = lax.broadcasted_iota(jnp.int32, (blk, blk), 1)
        csum = jnp.where(r_i >= c_i, 1.0, 0.0).astype(BF16)
        heads = range(ATT_HEADS)
        head = _head_lanes

        q_blk = q_ref[...]
        qf = q_blk.astype(F32)
        q_head = [head(q_blk, a) for a in heads]
        zmax = []
        for a in heads:
            qsq = jnp.sum(head(qf * qf, a), axis=1, keepdims=True)
            kmax = ksq_ref[0, 0, a // 2:a // 2 + 1, (a % 2) * HEAD_DIM:(a % 2) * HEAD_DIM + 1]
            zmax.append(1.01 * jnp.sqrt(qsq * kmax) + 0.01)

        def exhausted(run):
            top = functools.reduce(jnp.maximum, [jnp.max(run[a] + zmax[a]) for a in heads])
            return top < EXP_UNDERFLOW

        def sweep(blocks, run, acc):
            half = blk // 2

            def tiles(diag):
                return [(0, half, half), (half, half, blk)] if diag else [(0, blk, blk)]

            def keep(x, r0, diag):
                if diag:
                    rows = lax.broadcasted_iota(jnp.int32, x.shape, 0) + r0
                    x = jnp.where(lax.broadcasted_iota(jnp.int32, x.shape, 1) < rows, x, 0.0)
                return x

            offs = [pl.multiple_of(j * blk, blk) for j, _, _ in blocks]
            z = [[[_dot(q_head[a][r0:r0 + nr], k_ref[pl.ds(off, nk), :], _NT) for a in heads]
                  for r0, nr, nk in tiles(diag)] for (_, diag, _), off in zip(blocks, offs)]
            cl = [[[_dot(keep(-_softplus(zt[a]), r0, diag).astype(BF16), csum[:nk, :nk])
                    for a in heads]
                   for (r0, nr, nk), zt in zip(tiles(diag), zb)]
                  for (_, diag, _), zb in zip(blocks, z)]
            for (_, diag, valid), zb, clb, off in zip(blocks, z, cl, offs):
                rows_out, steps = [], [[] for _ in heads]
                for (r0, nr, nk), zt, clt in zip(tiles(diag), zb, clb):
                    v_blk = v_ref[pl.ds(off, nk), :]
                    part = None
                    for a in heads:
                        wa = keep(jnp.exp(zt[a] + clt[a] + run[a][r0:r0 + nr]), r0, diag)
                        term = _dot(wa.astype(BF16), head(v_blk, a))
                        part = term if part is None else part + term
                        steps[a].append(clt[a][:, 0:1])
                    rows_out.append(part)
                part = jnp.concatenate(rows_out, axis=0)
                steps = [jnp.concatenate(steps[a], axis=0) for a in heads]
                if valid is not None:
                    part = jnp.where(valid, part, 0.0)
                    steps = [jnp.where(valid, s, 0.0) for s in steps]
                acc = acc + part
                run = [run[a] + steps[a] for a in heads]
            return run, acc

        run = [jnp.zeros((blk, 1), F32)] * ATT_HEADS
        acc = jnp.zeros((blk, ATT_W), F32)
        run, acc = sweep([(qi, True, None), (jnp.maximum(qi - 1, 0), False, qi >= 1)], run, acc)
        low = jnp.maximum(qi - 1, 0)

        def more(carry):
            low, done, _, _ = carry
            return (low > 0) & jnp.logical_not(done)

        def pair(carry):
            low, _, run, acc = carry
            run, acc = sweep([(low - 1, False, None), (jnp.maximum(low - 2, 0), False, low >= 2)],
                             run, acc)
            return jnp.maximum(low - 2, 0), exhausted(run), run, acc

        low, _, run, acc = lax.while_loop(more, pair, (low, exhausted(run), run, acc))
        low_ref[pl.program_id(0) * N_ATT_GROUPS + pl.program_id(1), qi] = low.astype(F32)
        o_ref[...] = acc
        for a in heads:
            as_row = jnp.sum(jnp.where(r_i == c_i, run[a], 0.0), axis=0, keepdims=True)
            tot_ref[0, a, 0] = jnp.broadcast_to(as_row, (8, blk))

    return pl.pallas_call(
        body, name="sb_attn_fwd",
        grid=(nb, N_ATT_GROUPS, nq),
        in_specs=[pl.BlockSpec((blk, ATT_W), lambda b, h, i: (b * nq + i, h)),
                  pl.BlockSpec((seq, ATT_W), lambda b, h, i: (b, h)),
                  pl.BlockSpec((seq, ATT_W), lambda b, h, i: (b, h)),
                  pl.BlockSpec((1, 1, ATT_HEADS // 2, LANES), lambda b, h, i: (b, h, 0, 0))],
        out_specs=[pl.BlockSpec((blk, ATT_W), lambda b, h, i: (b * nq + i, h)),
                   pl.BlockSpec((1, ATT_HEADS, 1, 8, blk), lambda b, h, i: (b, h, i, 0, 0)),
                   pl.BlockSpec(memory_space=pltpu.SMEM)],
        out_shape=[jax.ShapeDtypeStruct((nb * seq, D_BRANCH), F32),
                   jax.ShapeDtypeStruct((nb, N_HEADS, nq, 8, blk), F32),
                   jax.ShapeDtypeStruct((nb * N_ATT_GROUPS, nq), F32)],
        compiler_params=_params(3),
    )(qs, kn, vb, ksq.reshape(nb, N_ATT_GROUPS, ATT_HEADS // 2, LANES))


def _attn_bwd(qs, kn, kt, vb, tot, low, d_o, nb, seq, blk):
    nq = seq // blk

    def body(q_ref, k_ref, kt_ref, v_ref, tot_ref, low_ref, do_ref, dq_ref, dk_ref, dvb_ref,
             dv_ref):
        qi = pl.program_id(2)

        @pl.when(qi == 0)
        def _():
            dk_ref[...] = jnp.zeros_like(dk_ref)
            dv_ref[...] = jnp.zeros_like(dv_ref)

        r_i = lax.broadcasted_iota(jnp.int32, (blk, blk), 0)
        c_i = lax.broadcasted_iota(jnp.int32, (blk, blk), 1)
        before = jnp.where(c_i < r_i, 1.0, 0.0).astype(BF16)
        upto = jnp.where(c_i <= r_i, 1.0, 0.0).astype(BF16)
        causal = r_i < c_i

        heads = range(ATT_HEADS)
        q_head = [_head_lanes(q_ref[...], a) for a in heads]
        d_ob = [_head_lanes(do_ref[...].astype(BF16), a) for a in heads]
        total = [tot_ref[0, a, 0][0:1, :] for a in heads]

        def sweep(blocks, lsum, esum, dqt):
            def keep(x, diag):
                return jnp.where(causal, x, 0.0) if diag else x

            def there(x, valid):
                return x if valid is None else jnp.where(valid, x, 0.0)

            offs = [pl.multiple_of(j * blk, blk) for j, _, _ in blocks]
            zt = [[_dot(k_ref[pl.ds(off, blk), :], q_head[a], _NT) for a in heads]
                  for off in offs]
            dwt = [[_dot(v_ref[pl.ds(off, blk), :], d_ob[a], _NT) for a in heads]
                   for off in offs]
            sp, lk, lpre = [], [], []
            for (_, diag, _), ztb in zip(blocks, zt):
                sp.append([_softplus(ztb[a]) for a in heads])
                lk.append([keep(-sp[-1][a], diag).astype(BF16) for a in heads])
                lpre.append([_dot(before, lk[-1][a]) for a in heads])
            wt, et, epre = [], [], []
            for i, (_, diag, valid) in enumerate(blocks):
                wt.append([keep(jnp.exp(zt[i][a] + (total[a] - lsum[a] - lpre[i][a])), diag)
                           for a in heads])
                et.append([dwt[i][a] * wt[i][a] for a in heads])
                epre.append([_dot(upto, et[i][a].astype(BF16)) for a in heads])
                lsum = [lsum[a] + there(lpre[i][a][blk - 1:blk, :] + lk[i][a][blk - 1:blk, :], valid)
                        for a in heads]
            for i, (_, diag, valid) in enumerate(blocks):
                dzb = [keep(et[i][a] - jnp.exp(zt[i][a] - sp[i][a]) * (esum[a] + epre[i][a]),
                            diag).astype(BF16) for a in heads]
                esum = [esum[a] + there(epre[i][a][blk - 1:blk, :], valid) for a in heads]
                dk_ref[pl.ds(offs[i], blk), :] += there(functools.reduce(
                    jnp.add, [_dot(dzb[a], q_head[a]) for a in heads]), valid)
                dv_ref[pl.ds(offs[i], blk), :] += there(functools.reduce(
                    jnp.add, [_dot(wt[i][a].astype(BF16), d_ob[a]) for a in heads]), valid)
                dqt = [dqt[a] + there(_dot(
                    kt_ref[0, a // 2, (a % 2) * HEAD_DIM:(a % 2 + 1) * HEAD_DIM,
                           pl.ds(offs[i], blk)], dzb[a]), valid) for a in heads]
            return lsum, esum, dqt

        row = [jnp.zeros((1, blk), F32)] * ATT_HEADS
        dqt = [jnp.zeros((HEAD_DIM, blk), F32)] * ATT_HEADS
        low = low_ref[pl.program_id(0) * N_ATT_GROUPS + pl.program_id(1), qi].astype(jnp.int32)
        low = jnp.clip(low, 0, jnp.maximum(qi - 1, 0))

        def pair(carry):
            j, lsum, esum, dqt = carry
            return (j + 2,) + sweep([(j, False, None), (j + 1, False, j + 1 < qi - 1)],
                                    lsum, esum, dqt)

        _, lsum, esum, dqt = lax.while_loop(lambda c: c[0] < qi - 1, pair, (low, row, row, dqt))
        _, _, dqt = sweep([(jnp.maximum(qi - 1, 0), False, qi >= 1), (qi, True, None)],
                          lsum, esum, dqt)
        dq_ref[...] = jnp.concatenate(dqt, axis=0).T

        @pl.when(qi == nq - 1)
        def _():
            dvb_ref[...] = dv_ref[...].astype(BF16)

    seq_blk = pl.BlockSpec((seq, ATT_W), lambda b, h, i: (b, h))
    tok = pl.BlockSpec((blk, ATT_W), lambda b, h, i: (b * nq + i, h))
    tok_shape = jax.ShapeDtypeStruct((nb * seq, D_BRANCH), F32)
    return pl.pallas_call(
        body, name="sb_attn_bwd",
        grid=(nb, N_ATT_GROUPS, nq),
        in_specs=[tok, seq_blk,
                  pl.BlockSpec((1, ATT_HEADS // 2, LANES, seq), lambda b, h, i: (b, h, 0, 0)),
                  seq_blk,
                  pl.BlockSpec((1, ATT_HEADS, 1, 8, blk), lambda b, h, i: (b, h, i, 0, 0)),
                  pl.BlockSpec(memory_space=pltpu.SMEM),
                  tok],
        out_specs=[tok, seq_blk, seq_blk],
        out_shape=[tok_shape, tok_shape, jax.ShapeDtypeStruct((nb * seq, D_BRANCH), BF16)],
        scratch_shapes=[pltpu.VMEM((seq, ATT_W), F32)],
        compiler_params=_params(3),
    )(qs, kn, kt, vb, tot, low, d_o)


def _qk_bwd(proj, dqs, dkn, qw2, kw2, nb, seq, tq):
    nl = seq // tq
    scale = 1.0 / math.sqrt(HEAD_DIM)

    def body(q_ref, k_ref, dq_ref, dk_ref, qw_ref, kw_ref, dqr_ref, dkr_ref, gq_ref, gk_ref):
        @pl.when((pl.program_id(0) == 0) & (pl.program_id(1) == 0))
        def _():
            gq_ref[...] = jnp.zeros_like(gq_ref)
            gk_ref[...] = jnp.zeros_like(gk_ref)

        def norm_bwd(x, w, dy):
            r = lax.rsqrt(_pair_sum(x * x) * (1.0 / HEAD_DIM) + EPS)
            xhat = x * r
            g = dy * w
            m = _pair_sum(g * xhat) * (1.0 / HEAD_DIM)
            return r * (g - xhat * m), jnp.sum(dy * xhat, axis=0, keepdims=True)

        gq = jnp.zeros((1, LANES), F32)
        gk = jnp.zeros((1, LANES), F32)
        for p in range(N_PAIRS):
            cols = slice(p * LANES, (p + 1) * LANES)
            dqr, gq_p = norm_bwd(q_ref[:, cols], qw_ref[...], dq_ref[:, cols] * scale)
            dkr, gk_p = norm_bwd(k_ref[:, cols], kw_ref[...], dk_ref[:, cols])
            dqr_ref[:, cols] = dqr.astype(BF16)
            dkr_ref[:, cols] = dkr.astype(BF16)
            gq, gk = gq + gq_p, gk + gk_p
        gq_ref[...] += gq
        gk_ref[...] += gk

    tok = lambda blk: pl.BlockSpec((tq, D_BRANCH), lambda b, i: (b * nl + i, blk))
    vec = pl.BlockSpec((1, LANES), lambda b, i: (0, 0))
    tshape = jax.ShapeDtypeStruct((nb * seq, D_BRANCH), BF16)
    return pl.pallas_call(
        body, name="qk_bwd",
        grid=(nb, nl),
        in_specs=[tok(0), tok(1), tok(0), tok(0), vec, vec],
        out_specs=[tok(0), tok(0), vec, vec],
        out_shape=[tshape, tshape,
                   jax.ShapeDtypeStruct((1, LANES), F32), jax.ShapeDtypeStruct((1, LANES), F32)],
        compiler_params=_params(2),
    )(proj, proj, dqs, dkn, qw2, kw2)


def _shift_down(cur, prev, k):
    if k == 0:
        return cur
    rows = _row_iota(cur.shape)
    return jnp.where(rows < k, pltpu.roll(prev, k, axis=0), pltpu.roll(cur, k, axis=0))


def _shift_up(cur, nxt, k):
    if k == 0:
        return cur
    n = cur.shape[0]
    rows = _row_iota(cur.shape)
    return jnp.where(rows < n - k, pltpu.roll(cur, n - k, axis=0), pltpu.roll(nxt, n - k, axis=0))


def _conv_taps(cur, prev):
    return [_shift_down(cur, prev, CONV_TAPS - 1 - i) for i in range(CONV_TAPS)]


def _conv_pre(taps, w, b):
    out = b
    for i in range(CONV_TAPS):
        out = out + taps[i] * w[i:i + 1, :]
    return out


def _silu(x):
    return x * _sigmoid(x)


def _silu_and_grad(x):
    s = _sigmoid(x)
    return x * s, s * (1.0 + x * (1.0 - s))


def _dot01(x, m01, parts, dims=None, m_left=False):
    total, rest = None, x
    for i in range(parts):
        piece = rest.astype(BF16)
        if i + 1 < parts:
            rest = rest - piece.astype(F32)
        term = _dot(m01, piece, dims) if m_left else _dot(piece, m01, dims)
        total = term if total is None else total + term
    return total


def _chunk_decay(dt_raw, dtb, alog, expand, qc):
    dt = _softplus(dt_raw + dtb)
    d_a = dt * (-jnp.exp(alog))
    r_i = lax.broadcasted_iota(jnp.int32, (qc, qc), 0)
    c_i = lax.broadcasted_iota(jnp.int32, (qc, qc), 1)
    tril = r_i >= c_i
    a_cs = _dot01(d_a, jnp.where(tril, 1.0, 0.0).astype(BF16), 3, m_left=True)
    dt_x = _dot01(dt, expand, 3)
    acs_x = _dot01(a_cs, expand, 3)
    return dt, d_a, a_cs, dt_x, acs_x, tril


def _ssd_fwd(proj, conv_w, conv_b, dtb, alog, dskip, nb, seq):
    qc = SSD_CHUNK
    nc = seq // qc

    def body(xs_ref, bc_ref, dt_ref, cw_ref, cb_ref, dtb_ref, al_ref, ds_ref,
             y_ref, st_ref, pxs_ref, pbc_ref, state_ref):
        @pl.when(pl.program_id(1) == 0)
        def _():
            pxs_ref[...] = jnp.zeros_like(pxs_ref)
            pbc_ref[...] = jnp.zeros_like(pbc_ref)
            state_ref[...] = jnp.zeros_like(state_ref)

        expand = _head_expand()
        xs_raw = xs_ref[...]
        bc_raw = bc_ref[...]
        cw = cw_ref[...]
        cb = cb_ref[...]
        xs = _silu(_conv_pre(_conv_taps(xs_raw, pxs_ref[...]), cw[:, :D_BRANCH], cb[:, :D_BRANCH]))
        bc = _silu(_conv_pre(_conv_taps(bc_raw, pbc_ref[...]), cw[:, D_BRANCH:], cb[:, D_BRANCH:]))
        pxs_ref[...] = xs_raw
        pbc_ref[...] = bc_raw

        dt, d_a, a_cs, dt_x, acs_x, tril = _chunk_decay(
            dt_ref[...], dtb_ref[...], al_ref[...], expand, qc)
        a_cst = a_cs.T
        aend_x = acs_x[qc - 1:qc, :]
        ea_x = jnp.exp(acs_x)
        dec_x = jnp.exp(aend_x - acs_x)
        xt = xs * dt_x
        xtb = xt.astype(BF16)
        xdb = (xt * dec_x).astype(BF16)
        d_x = _dot01(jnp.broadcast_to(ds_ref[...], (8, LANES)), expand, 3)[0:1, :]
        st_ref[0, 0] = state_ref[...]

        for g in range(N_GROUPS):
            gs = slice(g * GROUP_W, (g + 1) * GROUP_W)
            bg = bc[:, g * D_STATE:(g + 1) * D_STATE]
            cg = bc[:, (N_GROUPS + g) * D_STATE:(N_GROUPS + g + 1) * D_STATE]
            bgb = bg.astype(BF16)
            cgb = cg.astype(BF16)
            cbm = _dot(cgb, bgb, _NT)
            st_in = state_ref[g]
            y_off = _dot(cgb, st_in.astype(BF16)) * ea_x[:, gs]
            for k in range(HEADS_PER_GROUP):
                h = g * HEADS_PER_GROUP + k
                hs = slice(h * HEAD_DIM, (h + 1) * HEAD_DIM)
                seg = a_cs[:, h:h + 1] - a_cst[h:h + 1, :]
                gh = cbm * jnp.exp(jnp.where(tril, seg, -1e30))
                y_h = _dot(gh.astype(BF16), xtb[:, hs]) + y_off[:, k * HEAD_DIM:(k + 1) * HEAD_DIM]
                y_ref[:, hs] = y_h + d_x[:, hs] * xs[:, hs]
            state_ref[g] = st_in * jnp.exp(aend_x[:, gs]) + _dot(bg.T.astype(BF16), xdb[:, gs])

    nblk = lambda w, off: pl.BlockSpec((qc, w), lambda b, c: (b * nc + c, off))
    full = lambda r, w: pl.BlockSpec((r, w), lambda b, c: (0, 0))
    return pl.pallas_call(
        body, name="ssd_fwd",
        grid=(nb, nc),
        in_specs=[nblk(D_BRANCH, COL_XS // D_BRANCH), nblk(D_BC, COL_BC // D_BC),
                  nblk(LANES, COL_DT // LANES),
                  full(CONV_TAPS, D_CONV), full(1, D_CONV), full(1, LANES), full(1, LANES),
                  full(1, LANES)],
        out_specs=[pl.BlockSpec((qc, D_BRANCH), lambda b, c: (b * nc + c, 0)),
                   pl.BlockSpec((1, 1, N_GROUPS, D_STATE, GROUP_W), lambda b, c: (b, c, 0, 0, 0))],
        out_shape=[jax.ShapeDtypeStruct((nb * seq, D_BRANCH), F32),
                   jax.ShapeDtypeStruct((nb, nc, N_GROUPS, D_STATE, GROUP_W), F32)],
        scratch_shapes=[pltpu.VMEM((qc, D_BRANCH), F32), pltpu.VMEM((qc, D_BC), F32),
                        pltpu.VMEM((N_GROUPS, D_STATE, GROUP_W), F32)],
        compiler_params=_params(2),
    )(proj, proj, proj, conv_w, conv_b, dtb, alog, dskip)


def _ssd_bwd(proj, d_y, states, conv_w, conv_b, dtb, alog, dskip, nb, seq, slabs=()):
    qc = SSD_CHUNK
    nc = seq // qc

    def body(xs_ref, bc_ref, dt_ref, pxs_ref, pbc_ref, dy_ref, st_ref, stn_ref,
             cw_ref, cb_ref, dtb_ref, al_ref, ds_ref,
             dx_ref, gcw_ref, gcb_ref, gdtb_ref, gal_ref, gds_ref,
             dst_ref, nxs_ref, nbc_ref, yd_ref, dxt_ref):
        step = pl.program_id(1)
        chunk = nc - 1 - step

        @pl.when(step == 0)
        def _():
            dst_ref[...] = jnp.zeros_like(dst_ref)
            nxs_ref[...] = jnp.zeros_like(nxs_ref)
            nbc_ref[...] = jnp.zeros_like(nbc_ref)

        @pl.when((pl.program_id(0) == 0) & (step == 0))
        def _():
            gcw_ref[...] = jnp.zeros_like(gcw_ref)
            gcb_ref[...] = jnp.zeros_like(gcb_ref)
            gdtb_ref[...] = jnp.zeros_like(gdtb_ref)
            gal_ref[...] = jnp.zeros_like(gal_ref)
            gds_ref[...] = jnp.zeros_like(gds_ref)

        expand = _head_expand()
        collapse = lambda v: _dot01(v, expand, 2, _NT)
        first = jnp.where(chunk == 0, 0.0, 1.0)
        xs_raw = xs_ref[...]
        bc_raw = bc_ref[...]
        pxs = pxs_ref[...] * first
        pbc = pbc_ref[...] * first
        cw = cw_ref[...]
        cb = cb_ref[...]
        taps_xs = _conv_taps(xs_raw, pxs)
        taps_bc = _conv_taps(bc_raw, pbc)
        xs, dsilu_xs = _silu_and_grad(_conv_pre(taps_xs, cw[:, :D_BRANCH], cb[:, :D_BRANCH]))
        bc, dsilu_bc = _silu_and_grad(_conv_pre(taps_bc, cw[:, D_BRANCH:], cb[:, D_BRANCH:]))

        dt_in = dt_ref[...] + dtb_ref[...]
        dt, d_a, a_cs, dt_x, acs_x, tril = _chunk_decay(
            dt_ref[...], dtb_ref[...], al_ref[...], expand, qc)
        a_cst = a_cs.T
        aend_x = acs_x[qc - 1:qc, :]
        ea_x = jnp.exp(acs_x)
        dec_x = jnp.exp(aend_x - acs_x)
        xt = xs * dt_x
        xtb = xt.astype(BF16)
        xdb = (xt * dec_x).astype(BF16)
        d_x = _dot01(jnp.broadcast_to(ds_ref[...], (8, LANES)), expand, 3)[0:1, :]

        dy = dy_ref[...]
        dyb = dy.astype(BF16)
        dyeab = (dy * ea_x).astype(BF16)
        gds_ref[...] += collapse(jnp.broadcast_to(jnp.sum(dy * xs, axis=0, keepdims=True),
                                                  (8, D_BRANCH)))[0:1, :]

        d_bc = []
        d_cc = []
        y_offs = []
        dxt_states = []
        end_terms = []
        for g in range(N_GROUPS):
            gs = slice(g * GROUP_W, (g + 1) * GROUP_W)
            bg = bc[:, g * D_STATE:(g + 1) * D_STATE]
            cg = bc[:, (N_GROUPS + g) * D_STATE:(N_GROUPS + g + 1) * D_STATE]
            bgb = bg.astype(BF16)
            cgb = cg.astype(BF16)
            cbm = _dot(cgb, bgb, _NT)
            st_in = st_ref[0, 0, g]
            st_inb = st_in.astype(BF16)
            d_st = dst_ref[g]
            d_stb = d_st.astype(BF16)
            y_offs.append(_dot(cgb, st_inb) * ea_x[:, gs])
            dxt_states.append(_dot(bgb, d_stb) * dec_x[:, gs])
            d_c = _dot(dyeab[:, gs], st_inb, _NT)
            d_b = _dot(xdb[:, gs], d_stb, _NT)
            d_cb = jnp.zeros((qc, qc), F32)
            for k in range(HEADS_PER_GROUP):
                h = g * HEADS_PER_GROUP + k
                hs = slice(h * HEAD_DIM, (h + 1) * HEAD_DIM)
                seg = a_cs[:, h:h + 1] - a_cst[h:h + 1, :]
                lh = jnp.exp(jnp.where(tril, seg, -1e30))
                ghb = (cbm * lh).astype(BF16)
                d_cb = d_cb + _dot(dyb[:, hs], xtb[:, hs], _NT) * lh
                yd_ref[:, hs] = _dot(ghb, xtb[:, hs])
                dxt_ref[:, hs] = _dot(ghb, dyb[:, hs], _TN)
            d_cbb = d_cb.astype(BF16)
            d_cc.append(d_c + _dot(d_cbb, bgb))
            d_bc.append(d_b + _dot(d_cbb, cgb, _TN))
            end_terms.append(jnp.sum(d_st * stn_ref[0, 0, g], axis=0, keepdims=True))
            dst_ref[g] = d_st * jnp.exp(aend_x[:, gs]) + _dot(cg.T.astype(BF16), dyeab[:, gs])

        y_off = jnp.concatenate(y_offs, axis=1)
        dxt_state = jnp.concatenate(dxt_states, axis=1)
        dxt = dxt_ref[...] + dxt_state
        last = jnp.where(chunk == nc - 1, 0.0, 1.0)
        end_c = collapse(jnp.broadcast_to(jnp.concatenate(end_terms, axis=1), (8, D_BRANCH)))[0:1, :]
        da_cs = collapse(dyb.astype(F32) * yd_ref[...] - dxt_ref[...] * xtb.astype(F32)
                         + dy * y_off - dxt_state * xt)
        da_cs = da_cs + jnp.where(_row_iota(da_cs.shape) == qc - 1, end_c * last, 0.0)
        triu = lax.broadcasted_iota(jnp.int32, (qc, qc), 0) <= lax.broadcasted_iota(jnp.int32, (qc, qc), 1)
        dd_a = _dot01(da_cs, jnp.where(triu, 1.0, 0.0).astype(BF16), 3, m_left=True)
        ddt = dd_a * (-jnp.exp(al_ref[...])) + collapse(dxt * xs)
        head_lanes = _lane_iota(ddt.shape) < N_HEADS
        ddt_raw = jnp.where(head_lanes, ddt * _sigmoid(dt_in), 0.0)
        gal_ref[...] += jnp.sum(jnp.where(head_lanes, dd_a * d_a, 0.0), axis=0, keepdims=True)
        gdtb_ref[...] += jnp.sum(ddt_raw, axis=0, keepdims=True)

        dpre_xs = (dxt * dt_x + d_x * dy) * dsilu_xs
        dpre_bc = jnp.concatenate(d_bc + d_cc, axis=1) * dsilu_bc
        gcb_ref[...] += jnp.concatenate([jnp.sum(dpre_xs, axis=0, keepdims=True),
                                         jnp.sum(dpre_bc, axis=0, keepdims=True)], axis=1)
        nxs = nxs_ref[...]
        nbc = nbc_ref[...]
        du_xs = jnp.zeros_like(dpre_xs)
        du_bc = jnp.zeros_like(dpre_bc)
        for i in range(CONV_TAPS):
            k = CONV_TAPS - 1 - i
            gcw_ref[i:i + 1, :] += jnp.concatenate(
                [jnp.sum(dpre_xs * taps_xs[i], axis=0, keepdims=True),
                 jnp.sum(dpre_bc * taps_bc[i], axis=0, keepdims=True)], axis=1)
            du_xs = du_xs + _shift_up(dpre_xs, nxs, k) * cw[i:i + 1, :D_BRANCH]
            du_bc = du_bc + _shift_up(dpre_bc, nbc, k) * cw[i:i + 1, D_BRANCH:]
        nxs_ref[...] = dpre_xs
        nbc_ref[...] = dpre_bc

        dx_ref[:, :D_BRANCH] = du_xs.astype(BF16)
        dx_ref[:, D_BRANCH:D_CONV] = du_bc.astype(BF16)
        dx_ref[:, D_CONV:D_CONV + LANES] = ddt_raw.astype(BF16)
        dx_ref[:, D_CONV + LANES:] = jnp.zeros((qc, 2048 - D_CONV - LANES), BF16)

    rev = lambda b, c: b * nc + (nc - 1 - c)
    prv = lambda b, c: b * nc + jnp.maximum(nc - 2 - c, 0)
    nblk = lambda w, off, f: pl.BlockSpec((qc, w), lambda b, c: (f(b, c), off))
    full = lambda r, w: pl.BlockSpec((r, w), lambda b, c: (0, 0))
    st_spec = lambda f: pl.BlockSpec((1, 1, N_GROUPS, D_STATE, GROUP_W),
                                     lambda b, c: (b, f(c), 0, 0, 0))
    return _call_with_exchange(
        body, (proj, proj, proj, proj, proj, d_y, states, states, conv_w, conv_b, dtb, alog, dskip),
        slabs, (True,) * len(slabs), name="ssd_bwd", grid=(nb, nc),
        in_specs=[nblk(D_BRANCH, COL_XS // D_BRANCH, rev), nblk(D_BC, COL_BC // D_BC, rev),
                  nblk(LANES, COL_DT // LANES, rev),
                  nblk(D_BRANCH, COL_XS // D_BRANCH, prv), nblk(D_BC, COL_BC // D_BC, prv),
                  nblk(D_BRANCH, 0, rev),
                  st_spec(lambda c: nc - 1 - c), st_spec(lambda c: jnp.minimum(nc - c, nc - 1)),
                  full(CONV_TAPS, D_CONV), full(1, D_CONV), full(1, LANES), full(1, LANES),
                  full(1, LANES)],
        out_specs=[nblk(2048, 0, rev), full(8, D_CONV), full(1, D_CONV), full(1, LANES),
                   full(1, LANES), full(1, LANES)],
        out_shape=[jax.ShapeDtypeStruct((nb * seq, 2048), BF16),
                   jax.ShapeDtypeStruct((8, D_CONV), F32), jax.ShapeDtypeStruct((1, D_CONV), F32),
                   jax.ShapeDtypeStruct((1, LANES), F32), jax.ShapeDtypeStruct((1, LANES), F32),
                   jax.ShapeDtypeStruct((1, LANES), F32)],
        scratch_shapes=[pltpu.VMEM((N_GROUPS, D_STATE, GROUP_W), F32),
                        pltpu.VMEM((qc, D_BRANCH), F32), pltpu.VMEM((qc, D_BC), F32),
                        pltpu.VMEM((qc, D_BRANCH), F32), pltpu.VMEM((qc, D_BRANCH), F32)])


def _mid(o_sb, y_ssd, proj, x2, target, sb_w, ssd_w, w_out_b, tm):
    t = x2.shape[0]
    inv_d = 1.0 / D_MODEL

    def body(o_ref, y_ref, zsb_ref, zssd_ref, x_ref, tg_ref, sbw_ref, ssdw_ref, w_ref,
             dout_ref, dosb_ref, dy_ref, dz_ref, gw_ref, gsb_ref, gssd_ref, loss_ref):
        @pl.when(pl.program_id(0) == 0)
        def _():
            gw_ref[...] = jnp.zeros_like(gw_ref)
            gsb_ref[...] = jnp.zeros_like(gsb_ref)
            gssd_ref[...] = jnp.zeros_like(gssd_ref)
            loss_ref[...] = jnp.zeros_like(loss_ref)

        def branch(val, z, w):
            gate, dgate = _silu_and_grad(z)
            g = val * gate
            r = lax.rsqrt(jnp.mean(g * g, axis=1, keepdims=True) + EPS)
            xhat = g * r
            return (gate, dgate, r, xhat), (xhat * w).astype(BF16)

        o = o_ref[...]
        y = y_ref[...]
        saved_a, mix_a = branch(o, zsb_ref[...], sbw_ref[...])
        saved_b, mix_b = branch(y, zssd_ref[...], ssdw_ref[...])
        out = x_ref[...] + _dot(mix_a, w_ref[:D_BRANCH, :]) + _dot(mix_b, w_ref[D_BRANCH:, :])
        diff = out - tg_ref[...]
        loss_ref[...] += 0.5 * inv_d * jnp.sum(diff * diff)
        d_out = diff * inv_d
        dout_ref[...] = d_out
        d_outb = d_out.astype(BF16)
        gw_ref[:D_BRANCH, :] += _dot(mix_a, d_outb, _TN)
        gw_ref[D_BRANCH:, :] += _dot(mix_b, d_outb, _TN)

        def branch_bwd(dmix, val, w, saved):
            gate, dgate, r, xhat = saved
            gg = dmix * w
            m = jnp.mean(gg * xhat, axis=1, keepdims=True)
            dg = r * (gg - xhat * m)
            return dg * gate, dg * val * dgate, jnp.sum(dmix * xhat, axis=0, keepdims=True)

        dmix_a = _dot(d_outb, w_ref[:D_BRANCH, :], _NT)
        dmix_b = _dot(d_outb, w_ref[D_BRANCH:, :], _NT)
        d_o, dz_a, gsb = branch_bwd(dmix_a, o, sbw_ref[...], saved_a)
        d_y, dz_b, gssd = branch_bwd(dmix_b, y, ssdw_ref[...], saved_b)
        dosb_ref[...] = d_o.astype(BF16)
        dy_ref[...] = d_y
        dz_ref[:, :D_BRANCH] = dz_a.astype(BF16)
        dz_ref[:, D_BRANCH:] = dz_b.astype(BF16)
        gsb_ref[...] += gsb
        gssd_ref[...] += gssd

    row = lambda w, off: pl.BlockSpec((tm, w), lambda i: (i, off))
    full = lambda r, w: pl.BlockSpec((r, w), lambda i: (0, 0))
    resident = pl.BlockSpec((2 * D_BRANCH, D_MODEL), lambda i: (0, 0), pipeline_mode=pl.Buffered(1))
    tok = jax.ShapeDtypeStruct((t, D_MODEL), F32)
    return pl.pallas_call(
        body, name="mid",
        grid=(t // tm,),
        in_specs=[row(D_BRANCH, 0), row(D_BRANCH, 0), row(D_BRANCH, 3), row(D_BRANCH, 4),
                  row(D_MODEL, 0), row(D_MODEL, 0), full(1, D_BRANCH), full(1, D_BRANCH),
                  resident],
        out_specs=[row(D_MODEL, 0), row(D_BRANCH, 0), row(D_BRANCH, 0), row(2 * D_BRANCH, 0),
                   resident, full(1, D_BRANCH), full(1, D_BRANCH),
                   full(1, LANES)],
        out_shape=[tok, jax.ShapeDtypeStruct((t, D_BRANCH), BF16), tok,
                   jax.ShapeDtypeStruct((t, 2 * D_BRANCH), BF16),
                   jax.ShapeDtypeStruct((2 * D_BRANCH, D_MODEL), F32),
                   jax.ShapeDtypeStruct((1, D_BRANCH), F32), jax.ShapeDtypeStruct((1, D_BRANCH), F32),
                   jax.ShapeDtypeStruct((1, LANES), F32)],
        compiler_params=_params(1),
    )(o_sb, y_ssd, proj, proj, x2, target, sb_w, ssd_w, w_out_b)


_DPROJ_FIRST = (0, 1, 2, 3, 5)
_DPROJ_BLOCKS = (1, 1, 1, 2, 2)


def _in_proj_bwd_x(d_parts, w_in_t, x2, d_out, norm_w, tm, slabs=()):
    t = x2.shape[0]
    n_parts = len(d_parts)

    def body(*refs):
        dp_refs = refs[:n_parts]
        w_ref, x_ref, dout_ref, nw_ref, gx_ref, gnw_ref = refs[n_parts:]

        @pl.when(pl.program_id(0) == 0)
        def _():
            gnw_ref[...] = jnp.zeros_like(gnw_ref)

        d_hn = None
        for p in range(n_parts):
            rows = slice(_DPROJ_FIRST[p] * COLBLK, (_DPROJ_FIRST[p] + _DPROJ_BLOCKS[p]) * COLBLK)
            term = _dot(dp_refs[p][...], w_ref[rows, :])
            d_hn = term if d_hn is None else d_hn + term
        xf = x_ref[...]
        r = lax.rsqrt(jnp.mean(xf * xf, axis=1, keepdims=True) + EPS)
        xhat = xf * r
        g = d_hn * nw_ref[...]
        m = jnp.mean(g * xhat, axis=1, keepdims=True)
        gx_ref[...] = dout_ref[...] + r * (g - xhat * m)
        gnw_ref[...] += jnp.sum(d_hn * xhat, axis=0, keepdims=True)

    row = lambda w: pl.BlockSpec((tm, w), lambda i: (i, 0))
    return _call_with_exchange(
        body, (*d_parts, w_in_t, x2, d_out, norm_w), slabs, (True,) * len(slabs),
        name="in_proj_bwd_x", grid=(t // tm,),
        in_specs=[row(COLBLK * _DPROJ_BLOCKS[p]) for p in range(n_parts)] + [
                  pl.BlockSpec((D_IN_PAD, D_MODEL), lambda i: (0, 0), pipeline_mode=pl.Buffered(1)),
                  row(D_MODEL), row(D_MODEL), pl.BlockSpec((1, D_MODEL), lambda i: (0, 0))],
        out_specs=[row(D_MODEL), pl.BlockSpec((1, D_MODEL), lambda i: (0, 0))],
        out_shape=[jax.ShapeDtypeStruct((t, D_MODEL), F32), jax.ShapeDtypeStruct((1, D_MODEL), F32)])


def _in_proj_bwd_w(hn, d_parts, tm):
    t = hn.shape[0]
    n_parts = len(d_parts)

    def body(hn_ref, *refs):
        dp_refs, gw_ref = refs[:n_parts], refs[n_parts]

        @pl.when(pl.program_id(0) == 0)
        def _():
            gw_ref[...] = jnp.zeros_like(gw_ref)

        hnt = hn_ref[...].astype(F32).T.astype(BF16)
        for p in range(n_parts):
            cols = slice(_DPROJ_FIRST[p] * COLBLK, (_DPROJ_FIRST[p] + _DPROJ_BLOCKS[p]) * COLBLK)
            gw_ref[:, cols] += _dot(hnt, dp_refs[p][...])

    return pl.pallas_call(
        body, name="in_proj_bwd_w",
        grid=(t // tm,),
        in_specs=[pl.BlockSpec((tm, D_MODEL), lambda i: (i, 0))]
                 + [pl.BlockSpec((tm, COLBLK * _DPROJ_BLOCKS[p]), lambda i: (i, 0))
                    for p in range(n_parts)],
        out_specs=pl.BlockSpec((D_MODEL, D_IN_PAD), lambda i: (0, 0), pipeline_mode=pl.Buffered(1)),
        out_shape=jax.ShapeDtypeStruct((D_MODEL, D_IN_PAD), F32),
        compiler_params=_params(1),
    )(hn, *d_parts)


def _adamw(parts, w, m, v, tr, name):
    _, rows, cols = w.shape
    c1 = 1.0 - ADAM_B1 ** ADAM_STEP
    c2 = 1.0 - ADAM_B2 ** ADAM_STEP

    def body(p_ref, w_ref, m_ref, v_ref, g_ref, d_ref, nm_ref, nv_ref):
        g = p_ref[0].astype(F32)
        for s in range(1, N_DEV):
            g = g + p_ref[s].astype(F32)
        nm = ADAM_B1 * m_ref[0] + (1.0 - ADAM_B1) * g
        nv = ADAM_B2 * v_ref[0] + (1.0 - ADAM_B2) * (g * g)
        g_ref[0] = g
        nm_ref[0] = nm
        nv_ref[0] = nv
        d_ref[0] = -ADAM_LR * ((nm / c1) / (jnp.sqrt(nv / c2) + ADAM_EPS) + ADAM_WD * w_ref[0])

    blk = pl.BlockSpec((1, tr, cols), lambda i: (0, i, 0))
    shape = jax.ShapeDtypeStruct((1, rows, cols), F32)
    return pl.pallas_call(
        body, name=name,
        grid=(rows // tr,),
        in_specs=[pl.BlockSpec((N_DEV, tr, cols), lambda i: (0, i, 0)), blk, blk, blk],
        out_specs=[blk, blk, blk, blk],
        out_shape=[shape, shape, shape, shape],
        compiler_params=_params(1),
    )(parts, w, m, v)


def _mesh_place():
    x, y, c = lax.axis_index("x"), lax.axis_index("y"), lax.axis_index("c")
    return x, y, c, 4 * x + 2 * y + c


def _peer(x, y, c, k):
    px = 1 - x if k & 4 else x
    py = 1 - y if k & 2 else y
    pc = 1 - c if k & 1 else c
    return (px, py, pc), 4 * px + 2 * py + pc


def _exchange(srcs, scatter, name):
    n = len(srcs)

    def body(*refs):
        copies = _exchange_copies(refs[:n], refs[n:2 * n], scatter, *refs[2 * n:])
        _exchange_start(copies)
        _exchange_wait(copies)

    return pl.pallas_call(
        body, name=name,
        in_specs=[_ANY] * n, out_specs=[_ANY] * n, out_shape=_exchange_shapes(srcs, scatter),
        scratch_shapes=_exchange_sems(n),
    )(*srcs)


def _call_with_exchange(body, operands, srcs, scatter, *, name, grid, in_specs, out_specs,
                        out_shape, scratch_shapes=()):
    n_in, n_out, n_scr, n_x = len(in_specs), len(out_specs), len(scratch_shapes), len(srcs)
    params = _params(len(grid))
    if not n_x:
        return pl.pallas_call(body, name=name, grid=grid, in_specs=list(in_specs),
                              out_specs=list(out_specs), out_shape=list(out_shape),
                              scratch_shapes=list(scratch_shapes), compiler_params=params)(*operands)

    def wrapped(*refs):
        ins, refs = refs[:n_in], refs[n_in:]
        x_src, refs = refs[:n_x], refs[n_x:]
        outs, refs = refs[:n_out], refs[n_out:]
        x_dst, refs = refs[:n_x], refs[n_x:]
        scratch, sems = refs[:n_scr], refs[n_scr:]
        ids = [pl.program_id(a) for a in range(len(grid))]
        first = functools.reduce(jnp.logical_and, [i == 0 for i in ids])
        last = functools.reduce(jnp.logical_and, [i == n - 1 for i, n in zip(ids, grid)])

        @pl.when(first)
        def _():
            _exchange_start(_exchange_copies(x_src, x_dst, scatter, *sems))

        body(*ins, *outs, *scratch)

        @pl.when(last)
        def _():
            _exchange_wait(_exchange_copies(x_src, x_dst, scatter, *sems))

    return pl.pallas_call(
        wrapped, name=name, grid=grid,
        in_specs=list(in_specs) + [_ANY] * n_x, out_specs=list(out_specs) + [_ANY] * n_x,
        out_shape=list(out_shape) + _exchange_shapes(srcs, scatter),
        scratch_shapes=list(scratch_shapes) + _exchange_sems(n_x), compiler_params=params,
    )(*operands, *srcs)


def _gather_two_level(shard, name):
    def body(x_ref, out_ref, send_sems, recv_sems, local_sem):
        x, y, c, me = _mesh_place()
        sibling = (x, y, 1 - c)
        chips = [(1 - x, y), (x, 1 - y), (1 - x, 1 - y)]

        def slab(px, py, pc):
            return out_ref.at[4 * px + 2 * py + pc]

        def copy(k, block, to, src=None):
            return pltpu.make_async_remote_copy(
                src_ref=slab(*block) if src is None else src, dst_ref=slab(*block),
                send_sem=send_sems.at[k], recv_sem=recv_sems.at[k],
                device_id=to, device_id_type=pl.DeviceIdType.MESH)

        mine = pltpu.make_async_copy(x_ref, slab(x, y, c), local_sem)
        mine.start()
        first = [copy(0, (x, y, c), sibling, src=x_ref)]
        first += [copy(1 + j, (x, y, c), (*chip, c), src=x_ref) for j, chip in enumerate(chips)]
        for cp in first:
            cp.start()
        passed = [copy(4 + j, (*chip, c), sibling) for j, chip in enumerate(chips)]
        for j, chip in enumerate(chips):
            copy(1 + j, (*chip, c), (x, y, c)).wait_recv()
            passed[j].start()
        copy(0, sibling, (x, y, c)).wait_recv()
        for j, chip in enumerate(chips):
            copy(4 + j, (*chip, 1 - c), (x, y, c)).wait_recv()
        for cp in first + passed:
            cp.wait_send()
        mine.wait()

    return pl.pallas_call(
        body, name=name,
        in_specs=[_ANY], out_specs=_ANY,
        out_shape=jax.ShapeDtypeStruct((N_DEV,) + shard.shape, shard.dtype),
        scratch_shapes=[pltpu.SemaphoreType.DMA((N_DEV - 1,)), pltpu.SemaphoreType.DMA((N_DEV - 1,)),
                        pltpu.SemaphoreType.DMA],
    )(shard)


_ANY = pl.BlockSpec(memory_space=pl.ANY)


def _exchange_shapes(srcs, scatter):
    return [jax.ShapeDtypeStruct(s.shape if sc else (N_DEV,) + s.shape, s.dtype)
            for s, sc in zip(srcs, scatter)]


def _exchange_sems(n):
    return [pltpu.SemaphoreType.DMA((n * (N_DEV - 1),)),
            pltpu.SemaphoreType.DMA((n * (N_DEV - 1),)),
            pltpu.SemaphoreType.DMA((n,))]


def _exchange_copies(src_refs, dst_refs, scatter, send_sems, recv_sems, loc_sems):
    n = len(src_refs)
    x, y, c, me = _mesh_place()

    def src_of(i, idx):
        return src_refs[i].at[idx] if scatter[i] else src_refs[i]

    local = [pltpu.make_async_copy(src_of(i, me), dst_refs[i].at[me], loc_sems.at[i])
             for i in range(n)]
    sends, recvs = [], []
    for k in range(1, N_DEV):
        peer, pidx = _peer(x, y, c, k)
        for i in range(n):
            s = i * (N_DEV - 1) + k - 1
            for dst_slab, group in ((me, sends), (pidx, recvs)):
                group.append(pltpu.make_async_remote_copy(
                    src_ref=src_of(i, pidx), dst_ref=dst_refs[i].at[dst_slab],
                    send_sem=send_sems.at[s], recv_sem=recv_sems.at[s],
                    device_id=peer, device_id_type=pl.DeviceIdType.MESH))
    return local, sends, recvs


def _exchange_start(copies):
    local, sends, _ = copies
    for cp in local + sends:
        cp.start()


def _exchange_wait(copies):
    local, sends, recvs = copies
    for cp in recvs:
        cp.wait_recv()
    for cp in sends:
        cp.wait_send()
    for cp in local:
        cp.wait()


def _pad_lanes(v, width=LANES):
    return jnp.pad(v, ((0, 0), (0, width - v.shape[1])))


def _local_step(x, target, norm_w, w_in_b, q_norm_w, k_norm_w, conv_w, conv_b, dt_bias, a_log,
                d_skip, sb_norm_w, ssd_norm_w, w_out_b, tm=256, tq=512, tmid=256, blk=ATT_BLK,
                scatter=False, w_in_t=None):
    nb, seq, _ = x.shape
    t = nb * seq
    x2 = x.reshape(t, D_MODEL)
    tg2 = target.reshape(t, D_MODEL)
    qw2 = jnp.tile(q_norm_w, (1, 2))
    kw2 = jnp.tile(k_norm_w, (1, 2))
    dtb, alog, dsk = _pad_lanes(dt_bias), _pad_lanes(a_log), _pad_lanes(d_skip)

    if w_in_t is None:
        w_in_t = w_in_b.T
    tall = min(2 * tm, t)
    if scatter:
        proj, hn, wout_all, cw_all = _in_proj(x2, norm_w, w_in_b, tall, (w_out_b, conv_w))
        w_out_b = wout_all.reshape(2 * D_BRANCH, D_MODEL)
        conv_w = jnp.transpose(cw_all, (1, 0, 2)).reshape(CONV_TAPS, D_CONV)
    else:
        proj, hn = _in_proj(x2, norm_w, w_in_b, tall)
    qs, kn, vb, kt, ksq = _qk_prep(proj, qw2, kw2, nb, seq, min(2 * tq, seq))
    o_sb, sb_tot, sb_low = _attn_fwd(qs, kn, vb, jnp.max(ksq, axis=1), nb, seq, blk)
    y_ssd, states = _ssd_fwd(proj, conv_w, conv_b, dtb, alog, dsk, nb, seq)
    d_out, d_osb, d_y, d_z, g_wout, g_sbw, g_ssdw, loss = _mid(
        o_sb, y_ssd, proj, x2, tg2, sb_norm_w, ssd_norm_w, w_out_b, tmid)
    dqs, dkn, dv_raw = _attn_bwd(qs, kn, kt, vb, sb_tot, sb_low, d_osb, nb, seq, blk)
    dq_raw, dk_raw, g_qw, g_kw = _qk_bwd(proj, dqs, dkn, qw2, kw2, nb, seq, tq)
    wout_slabs = (g_wout.reshape(N_DEV, 2 * D_BRANCH // N_DEV, D_MODEL).astype(BF16),)
    d_xbc, g_cw, g_cb, g_dtb, g_alog, g_dsk, *moved = _ssd_bwd(
        proj, d_y, states, conv_w, conv_b, dtb, alog, dsk, nb, seq, wout_slabs if scatter else ())
    d_parts = [dq_raw, dk_raw, dv_raw, d_z, d_xbc]
    g_win = _in_proj_bwd_w(hn, d_parts, tall)[:, :D_IN]
    g_cw = g_cw[:CONV_TAPS]
    if scatter:
        g_wout, = moved
        grad_x, g_nw, g_win, g_cw = _in_proj_bwd_x(
            d_parts, w_in_t, x2, d_out, norm_w, tall, _grad_slabs(g_win, g_cw))
    else:
        grad_x, g_nw = _in_proj_bwd_x(d_parts, w_in_t, x2, d_out, norm_w, tall)

    small = dict(
        norm_w=g_nw,
        q_norm_w=g_qw[:, :HEAD_DIM] + g_qw[:, HEAD_DIM:],
        k_norm_w=g_kw[:, :HEAD_DIM] + g_kw[:, HEAD_DIM:],
        conv_b=g_cb, dt_bias=g_dtb[:, :N_HEADS], A_log=g_alog[:, :N_HEADS],
        D_skip=g_dsk[:, :N_HEADS], sb_norm_w=g_sbw, ssd_norm_w=g_ssdw)
    return loss[0, 0], grad_x.reshape(nb, seq, D_MODEL), g_win, g_wout, g_cw, small


def _grad_slabs(g_win, g_cw):
    w_sh = D_IN // N_DEV
    c_sh = D_CONV // N_DEV
    return (jnp.transpose(g_win.reshape(D_MODEL, N_DEV, w_sh), (1, 0, 2)).astype(BF16),
            jnp.pad(jnp.transpose(g_cw.reshape(CONV_TAPS, N_DEV, c_sh), (1, 0, 2)),
                    ((0, 0), (0, 8 - CONV_TAPS), (0, 0))))


_SMALL = ("norm_w", "q_norm_w", "k_norm_w", "conv_b", "dt_bias", "A_log", "D_skip",
          "sb_norm_w", "ssd_norm_w")


def _pack_small(vals):
    flat = jnp.concatenate([_pad_lanes(vals[n], -(-vals[n].shape[1] // LANES) * LANES)
                            for n in _SMALL], axis=1)
    return jnp.pad(flat, ((0, 0), (0, 48 * LANES - flat.shape[1]))).reshape(48, LANES)


def _unpack_small(packed, like):
    out, r = {}, 0
    for n in _SMALL:
        width = like[n].shape[1]
        nr = -(-width // LANES)
        out[n] = packed[r:r + nr].reshape(1, nr * LANES)[:, :width]
        r += nr
    return out


def kernel(x, norm_w, w_in, q_norm_w, k_norm_w, conv_w, conv_b, dt_bias, A_log, D_skip, sb_norm_w, ssd_norm_w, w_out, loss_target, m_norm_w, m_w_in, m_q_norm_w, m_k_norm_w, m_conv_w, m_conv_b, m_dt_bias, m_A_log, m_D_skip, m_sb_norm_w, m_ssd_norm_w, m_w_out, v_norm_w, v_w_in, v_q_norm_w, v_k_norm_w, v_conv_w, v_conv_b, v_dt_bias, v_A_log, v_D_skip, v_sb_norm_w, v_ssd_norm_w, v_w_out):
    win_all = _gather_two_level(w_in[0].astype(BF16), "gather_w_in")
    w_in_b = jnp.pad(jnp.transpose(win_all, (1, 0, 2)).reshape(D_MODEL, D_IN),
                     ((0, 0), (0, D_IN_PAD - D_IN)))
    w_in_t = jnp.pad(jnp.transpose(win_all, (0, 2, 1)).reshape(D_IN, D_MODEL),
                     ((0, D_IN_PAD - D_IN), (0, 0)))

    loss, grad_x, win_parts, wout_parts, cw_parts, g_small = _local_step(
        x, loss_target, norm_w, w_in_b, q_norm_w, k_norm_w, conv_w[0], conv_b, dt_bias, A_log,
        D_skip, sb_norm_w, ssd_norm_w, w_out[0].astype(BF16), scatter=True, w_in_t=w_in_t)
    packed = _pack_small(g_small).at[-1, 0].set(loss)
    small_parts, = _exchange([packed], [False], "gather_small_grads")
    loss = jnp.sum(small_parts[:, -1, 0])

    small_w = dict(norm_w=norm_w, q_norm_w=q_norm_w, k_norm_w=k_norm_w, conv_b=conv_b,
                   dt_bias=dt_bias, A_log=A_log, D_skip=D_skip, sb_norm_w=sb_norm_w,
                   ssd_norm_w=ssd_norm_w)
    small_m = dict(norm_w=m_norm_w, q_norm_w=m_q_norm_w, k_norm_w=m_k_norm_w, conv_b=m_conv_b,
                   dt_bias=m_dt_bias, A_log=m_A_log, D_skip=m_D_skip, sb_norm_w=m_sb_norm_w,
                   ssd_norm_w=m_ssd_norm_w)
    small_v = dict(norm_w=v_norm_w, q_norm_w=v_q_norm_w, k_norm_w=v_k_norm_w, conv_b=v_conv_b,
                   dt_bias=v_dt_bias, A_log=v_A_log, D_skip=v_D_skip, sb_norm_w=v_sb_norm_w,
                   ssd_norm_w=v_ssd_norm_w)

    pad8 = lambda a: jnp.pad(a, ((0, 0), (0, 8 - CONV_TAPS), (0, 0)))
    r_win = _adamw(win_parts, w_in, m_w_in, v_w_in, 128, "adamw_w_in")
    r_wout = _adamw(wout_parts, w_out, m_w_out, v_w_out, 128, "adamw_w_out")
    r_cw = _adamw(cw_parts, pad8(conv_w), pad8(m_conv_w), pad8(v_conv_w), 8, "adamw_conv_w")
    r_small = _adamw(small_parts, _pack_small(small_w)[None], _pack_small(small_m)[None],
                     _pack_small(small_v)[None], 48, "adamw_small")

    res = {"w_in": r_win, "w_out": r_wout, "conv_w": [a[:, :CONV_TAPS] for a in r_cw]}
    unpacked = [_unpack_small(a[0], small_w) for a in r_small]
    for n in _SMALL:
        res[n] = [u[n] for u in unpacked]
    order = ("norm_w", "w_in", "q_norm_w", "k_norm_w", "conv_w", "conv_b", "dt_bias", "A_log",
             "D_skip", "sb_norm_w", "ssd_norm_w", "w_out")
    outs = [loss, grad_x]
    for kind in range(4):
        outs += [res[n][kind] for n in order]
    return tuple(outs)
```
